```python
import math
import jax
import jax.numpy as jnp
from jax import lax
import numpy as np

D_MODEL = 1024
BATCH = 8
SEQ = 2048
DEPTH = 1

HEAD_DIM = 64
N_Q_HEADS = 16
N_KV_HEADS = 4
Q_PER_KV = N_Q_HEADS // N_KV_HEADS
ATTN_WIDTH = N_Q_HEADS * HEAD_DIM
KV_WIDTH = N_KV_HEADS * HEAD_DIM
DILATED_PATTERNS = ((128, 1), (512, 4), (2048, 16))
ATTN_BLOCK = 128
ROPE_THETA = 10000.0

SSM_HEADS = 16
SSM_HEAD_DIM = 64
SSM_WIDTH = SSM_HEADS * SSM_HEAD_DIM
SSM_STATE = 128
SSM_GROUPS = 2
SSM_HEADS_PER_GROUP = SSM_HEADS // SSM_GROUPS
BC_WIDTH = SSM_GROUPS * SSM_STATE
CONV_WIDTH = 4
CONV_CHANNELS = SSM_WIDTH + 2 * BC_WIDTH
SSD_CHUNK = 128
DT_MIN = 0.001
DT_MAX = 0.1
A_INIT_MIN = 1.0
A_INIT_MAX = 16.0

MIX_WIDTH = ATTN_WIDTH + SSM_WIDTH
IN_SPLITS = (
    ATTN_WIDTH,
    ATTN_WIDTH + KV_WIDTH,
    ATTN_WIDTH + 2 * KV_WIDTH,
    ATTN_WIDTH + 2 * KV_WIDTH + CONV_CHANNELS,
    ATTN_WIDTH + 2 * KV_WIDTH + CONV_CHANNELS + SSM_WIDTH,
)
IN_PROJ_COLS = IN_SPLITS[-1] + SSM_HEADS

D_FF = 2816
MACARON_WEIGHT = 0.5
NORM_EPS = 1e-6
POS_OFFSET_MAX = 1024

kernel_name = 'hymba_dilated_attn_mamba2_macaron_sandwich'


def rmsnorm(x, g):
    xf = x.astype(jnp.float32)
    y = xf * lax.rsqrt(jnp.mean(xf * xf, axis=-1, keepdims=True) + NORM_EPS)
    return (y * g.astype(jnp.float32)).astype(x.dtype)


def swiglu(x, w_gate, w_up, w_down):
    return (jax.nn.silu(x @ w_gate) * (x @ w_up)) @ w_down


def rope_cos_sin(positions):
    inv_freq = ROPE_THETA ** (-jnp.arange(0, HEAD_DIM, 2, dtype=jnp.float32) / HEAD_DIM)
    ang = positions.astype(jnp.float32)[..., None] * inv_freq
    ang = jnp.concatenate([ang, ang], axis=-1)
    return jnp.cos(ang), jnp.sin(ang)


def apply_rope(t, cos, sin):
    shape = cos.shape[:2] + (1,) * (t.ndim - 3) + (HEAD_DIM,)
    c = cos.reshape(shape)
    s = sin.reshape(shape)
    tf = t.astype(jnp.float32)
    t1, t2 = jnp.split(tf, 2, axis=-1)
    rot = jnp.concatenate([-t2, t1], axis=-1)
    return (tf * c + rot * s).astype(t.dtype)


def _to_strided_blocks(t, dilation, n_blocks):
    b, s = t.shape[:2]
    sub_len = s // dilation
    t = t.reshape((b, sub_len, dilation) + t.shape[2:])
    pad = n_blocks * ATTN_BLOCK - sub_len
    t = jnp.pad(t, [(0, 0), (0, pad)] + [(0, 0)] * (t.ndim - 2))
    return t.reshape((b, n_blocks, ATTN_BLOCK, dilation) + t.shape[3:])


def _from_strided_blocks(t, seq):
    b, n_blocks, _, dilation = t.shape[:4]
    sub_len = seq // dilation
    t = t.reshape((b, n_blocks * ATTN_BLOCK, dilation) + t.shape[4:])[:, :sub_len]
    return t.reshape((b, seq) + t.shape[3:])


def _with_prev_block(t):
    prev = jnp.concatenate([jnp.zeros_like(t[:, :1]), t[:, :-1]], axis=1)
    return jnp.concatenate([prev, t], axis=2)


def dilated_window_attention(q, k, v, window, dilation):
    b, s = q.shape[:2]
    sub_len = s // dilation
    span = window // dilation
    n_blocks = -(-sub_len // ATTN_BLOCK)
    qb = _to_strided_blocks(q, dilation, n_blocks)
    kk = _with_prev_block(_to_strided_blocks(k, dilation, n_blocks))
    vv = _with_prev_block(_to_strided_blocks(v, dilation, n_blocks))
    scores = jnp.einsum('bnqrkgh,bnsrkh->bnrkgqs', qb, kk, preferred_element_type=jnp.float32)
    qi = jnp.arange(ATTN_BLOCK)[:, None]
    si = jnp.arange(2 * ATTN_BLOCK)[None, :]
    dist = qi + ATTN_BLOCK - si
    band = (dist >= 0) & (dist <= span)
    key_idx = jnp.arange(n_blocks)[:, None, None] * ATTN_BLOCK + si[None] - ATTN_BLOCK
    mask = band[None] & (key_idx >= 0)
    scores = jnp.where(mask[None, :, None, None, None], scores, -jnp.inf)
    m = jnp.max(scores, axis=-1)
    p = jnp.exp(scores - m[..., None])
    l = jnp.sum(p, axis=-1)
    o = jnp.einsum('bnrkgqs,bnsrkh->bnqrkgh', p.astype(vv.dtype), vv, preferred_element_type=jnp.float32)
    l_t = jnp.moveaxis(l, -1, 2)
    o = o / l_t[..., None]
    return (_from_strided_blocks(o, s),
            _from_strided_blocks(jnp.moveaxis(m, -1, 2), s),
            _from_strided_blocks(l_t, s))


def mixture_of_dilations(q, k, v):
    outs, maxes, dens = [], [], []
    for window, dilation in DILATED_PATTERNS:
        o, m, l = dilated_window_attention(q, k, v, window, dilation)
        outs.append(o)
        maxes.append(m)
        dens.append(l)
    m_all = jnp.stack(maxes)
    w = jnp.stack(dens) * jnp.exp(m_all - jnp.max(m_all, axis=0, keepdims=True))
    o_all = jnp.stack(outs)
    return jnp.sum(w[..., None] * o_all, axis=0) / jnp.sum(w, axis=0)[..., None]


def causal_depthwise_conv(u, w, bias):
    y = lax.conv_general_dilated(u, w[:, None, :].astype(u.dtype), window_strides=(1,),
                                 padding=[(CONV_WIDTH - 1, 0)],
                                 dimension_numbers=('NWC', 'WIO', 'NWC'),
                                 feature_group_count=u.shape[-1])
    return y + bias


def ssd_chunked_scan(x, dt, a, b_in, c_in):
    bsz, s = x.shape[:2]
    n_chunks = s // SSD_CHUNK
    g, e, p, n = SSM_GROUPS, SSM_HEADS_PER_GROUP, SSM_HEAD_DIM, SSM_STATE
    xs = (x.astype(jnp.float32) * dt[..., None]).reshape(bsz, n_chunks, SSD_CHUNK, g, e, p)
    a_dt = jnp.moveaxis((dt * a).reshape(bsz, n_chunks, SSD_CHUNK, g, e), 2, -1)
    a_cs = jnp.cumsum(a_dt, axis=-1)
    bc = b_in.astype(jnp.float32).reshape(bsz, n_chunks, SSD_CHUNK, g, n)
    cc = c_in.astype(jnp.float32).reshape(bsz, n_chunks, SSD_CHUNK, g, n)
    idx = jnp.arange(SSD_CHUNK)
    causal = idx[:, None] >= idx[None, :]
    seg = a_cs[..., :, None] - a_cs[..., None, :]
    decay_in = jnp.exp(jnp.where(causal, seg, -jnp.inf))
    cb = jnp.einsum('bclgn,bcsgn->bcgls', cc, bc)
    y_diag = jnp.einsum('bcgels,bcsgep->bclgep', cb[:, :, :, None] * decay_in, xs)
    decay_to_end = jnp.exp(a_cs[..., -1:] - a_cs)
    states = jnp.einsum('bclgn,bcgel,bclgep->bcgepn', bc, decay_to_end, xs)
    chunk_decay = jnp.exp(a_cs[..., -1])

    def step(h, inp):
        st, dec = inp
        return h * dec[..., None, None] + st, h

    h0 = jnp.zeros((bsz, g, e, p, n), jnp.float32)
    _, prev_states = lax.scan(step, h0, (jnp.moveaxis(states, 1, 0), jnp.moveaxis(chunk_decay, 1, 0)))
    prev_states = jnp.moveaxis(prev_states, 0, 1)
    y_off = jnp.einsum('bclgn,bcgepn,bcgel->bclgep', cc, prev_states, jnp.exp(a_cs))
    return (y_diag + y_off).reshape(bsz, s, SSM_HEADS, p)


def hybrid_mixer(h, cos, sin, w_in, conv_w, conv_b, dt_bias, a_log, d_skip, ssm_norm, w_out):
    bsz, s, _ = h.shape
    proj = h @ w_in
    q, k, v, xbc, z, dt_raw = jnp.split(proj, list(IN_SPLITS), axis=-1)

    q = apply_rope(q.reshape(bsz, s, N_KV_HEADS, Q_PER_KV, HEAD_DIM), cos, sin) * (HEAD_DIM ** -0.5)
    k = apply_rope(k.reshape(bsz, s, N_KV_HEADS, HEAD_DIM), cos, sin)
    v = v.reshape(bsz, s, N_KV_HEADS, HEAD_DIM)
    attn_out = mixture_of_dilations(q, k, v).reshape(bsz, s, ATTN_WIDTH)

    xbc = jax.nn.silu(causal_depthwise_conv(xbc, conv_w, conv_b))
    xs, b_in, c_in = jnp.split(xbc, [SSM_WIDTH, SSM_WIDTH + BC_WIDTH], axis=-1)
    xs = xs.reshape(bsz, s, SSM_HEADS, SSM_HEAD_DIM)
    b_in = b_in.reshape(bsz, s, SSM_GROUPS, SSM_STATE)
    c_in = c_in.reshape(bsz, s, SSM_GROUPS, SSM_STATE)
    dt = jax.nn.softplus(dt_raw.astype(jnp.float32) + dt_bias.astype(jnp.float32))
    a = -jnp.exp(a_log.astype(jnp.float32))
    y = ssd_chunked_scan(xs, dt, a, b_in, c_in)
    y = y + d_skip.astype(jnp.float32)[:, None] * xs.astype(jnp.float32)
    y = y.reshape(bsz, s, SSM_WIDTH) * jax.nn.silu(z.astype(jnp.float32))
    y = y.reshape(bsz, s, SSM_GROUPS, SSM_WIDTH // SSM_GROUPS)
    y = y * lax.rsqrt(jnp.mean(y * y, axis=-1, keepdims=True) + NORM_EPS)
    y = y.reshape(bsz, s, SSM_WIDTH) * ssm_norm.astype(jnp.float32)

    mixed = jnp.concatenate([attn_out.astype(h.dtype), y.astype(h.dtype)], axis=-1)
    return mixed @ w_out


def _fwd_setup_inputs(seed: int = 0) -> dict:
    key = jax.random.key(seed)
    ks = jax.random.split(key, 24)
    f32 = jnp.float32

    def dense(k, shape, fan_in):
        return jax.random.normal(k, shape, f32) * fan_in ** -0.5

    def gain(k, shape):
        return 1.0 + 0.05 * jax.random.normal(k, shape, f32)

    x = jax.random.normal(ks[0], (BATCH, SEQ, D_MODEL), f32)
    positions = (jnp.arange(SEQ, dtype=jnp.int32)[None, :]
                 + jax.random.randint(ks[1], (BATCH, 1), 0, POS_OFFSET_MAX, dtype=jnp.int32))
    dt = jnp.exp(jax.random.uniform(ks[11], (DEPTH, SSM_HEADS), f32)
                 * (math.log(DT_MAX) - math.log(DT_MIN)) + math.log(DT_MIN))
    dt_bias = dt + jnp.log(-jnp.expm1(-dt))
    a_log = jnp.log(jax.random.uniform(ks[12], (DEPTH, SSM_HEADS), f32, A_INIT_MIN, A_INIT_MAX))
    return {
        'x': x,
        'positions': positions,
        'ffn1_pre_norm': gain(ks[2], (DEPTH, D_MODEL)),
        'ffn1_w_gate': dense(ks[3], (DEPTH, D_MODEL, D_FF), D_MODEL),
        'ffn1_w_up': dense(ks[4], (DEPTH, D_MODEL, D_FF), D_MODEL),
        'ffn1_w_down': dense(ks[5], (DEPTH, D_FF, D_MODEL), D_FF),
        'ffn1_post_norm': gain(ks[6], (DEPTH, D_MODEL)),
        'mix_pre_norm': gain(ks[7], (DEPTH, D_MODEL)),
        'w_in': dense(ks[8], (DEPTH, D_MODEL, IN_PROJ_COLS), D_MODEL),
        'conv_w': dense(ks[9], (DEPTH, CONV_WIDTH, CONV_CHANNELS), CONV_WIDTH),
        'conv_b': 0.02 * jax.random.normal(ks[10], (DEPTH, CONV_CHANNELS), f32),
        'dt_bias': dt_bias,
        'a_log': a_log,
        'd_skip': 1.0 + 0.1 * jax.random.normal(ks[13], (DEPTH, SSM_HEADS), f32),
        'ssm_norm': gain(ks[14], (DEPTH, SSM_WIDTH)),
        'w_out': dense(ks[15], (DEPTH, MIX_WIDTH, D_MODEL), MIX_WIDTH),
        'mix_post_norm': gain(ks[16], (DEPTH, D_MODEL)),
        'ffn2_pre_norm': gain(ks[17], (DEPTH, D_MODEL)),
        'ffn2_w_gate': dense(ks[18], (DEPTH, D_MODEL, D_FF), D_MODEL),
        'ffn2_w_up': dense(ks[19], (DEPTH, D_MODEL, D_FF), D_MODEL),
        'ffn2_w_down': dense(ks[20], (DEPTH, D_FF, D_MODEL), D_FF),
        'ffn2_post_norm': gain(ks[21], (DEPTH, D_MODEL)),
    }


def _fwd_reference(x, positions, ffn1_pre_norm, ffn1_w_gate, ffn1_w_up, ffn1_w_down, ffn1_post_norm,
              mix_pre_norm, w_in, conv_w, conv_b, dt_bias, a_log, d_skip, ssm_norm, w_out, mix_post_norm,
              ffn2_pre_norm, ffn2_w_gate, ffn2_w_up, ffn2_w_down, ffn2_post_norm):
    cos, sin = rope_cos_sin(positions)
    for i in range(DEPTH):
        h = swiglu(rmsnorm(x, ffn1_pre_norm[i]), ffn1_w_gate[i], ffn1_w_up[i], ffn1_w_down[i])
        x = x + MACARON_WEIGHT * rmsnorm(h, ffn1_post_norm[i])
        h = hybrid_mixer(rmsnorm(x, mix_pre_norm[i]), cos, sin, w_in[i], conv_w[i], conv_b[i],
                         dt_bias[i], a_log[i], d_skip[i], ssm_norm[i], w_out[i])
        x = x + rmsnorm(h, mix_post_norm[i])
        h = swiglu(rmsnorm(x, ffn2_pre_norm[i]), ffn2_w_gate[i], ffn2_w_up[i], ffn2_w_down[i])
        x = x + MACARON_WEIGHT * rmsnorm(h, ffn2_post_norm[i])
    return x


import jax as _jax
import jax.numpy as _jnp

TWIN_FORMAT = 'train_step'
FWD_PARAMS = ['x', 'positions', 'ffn1_pre_norm', 'ffn1_w_gate', 'ffn1_w_up', 'ffn1_w_down', 'ffn1_post_norm', 'mix_pre_norm', 'w_in', 'conv_w', 'conv_b', 'dt_bias', 'a_log', 'd_skip', 'ssm_norm', 'w_out', 'mix_post_norm', 'ffn2_pre_norm', 'ffn2_w_gate', 'ffn2_w_up', 'ffn2_w_down', 'ffn2_post_norm']
TWIN_WEIGHTS = ['ffn1_pre_norm', 'ffn1_w_gate', 'ffn1_w_up', 'ffn1_w_down', 'ffn1_post_norm', 'mix_pre_norm', 'w_in', 'conv_w', 'conv_b', 'dt_bias', 'a_log', 'd_skip', 'ssm_norm', 'w_out', 'mix_post_norm', 'ffn2_pre_norm', 'ffn2_w_gate', 'ffn2_w_up', 'ffn2_w_down', 'ffn2_post_norm']
TWIN_DIFF_INPUT = 'x'
TWIN_INPUTS = ['x', 'positions', 'ffn1_pre_norm', 'ffn1_w_gate', 'ffn1_w_up', 'ffn1_w_down', 'ffn1_post_norm', 'mix_pre_norm', 'w_in', 'conv_w', 'conv_b', 'dt_bias', 'a_log', 'd_skip', 'ssm_norm', 'w_out', 'mix_post_norm', 'ffn2_pre_norm', 'ffn2_w_gate', 'ffn2_w_up', 'ffn2_w_down', 'ffn2_post_norm', 'loss_target', 'm_ffn1_pre_norm', 'm_ffn1_w_gate', 'm_ffn1_w_up', 'm_ffn1_w_down', 'm_ffn1_post_norm', 'm_mix_pre_norm', 'm_w_in', 'm_conv_w', 'm_conv_b', 'm_dt_bias', 'm_a_log', 'm_d_skip', 'm_ssm_norm', 'm_w_out', 'm_mix_post_norm', 'm_ffn2_pre_norm', 'm_ffn2_w_gate', 'm_ffn2_w_up', 'm_ffn2_w_down', 'm_ffn2_post_norm', 'v_ffn1_pre_norm', 'v_ffn1_w_gate', 'v_ffn1_w_up', 'v_ffn1_w_down', 'v_ffn1_post_norm', 'v_mix_pre_norm', 'v_w_in', 'v_conv_w', 'v_conv_b', 'v_dt_bias', 'v_a_log', 'v_d_skip', 'v_ssm_norm', 'v_w_out', 'v_mix_post_norm', 'v_ffn2_pre_norm', 'v_ffn2_w_gate', 'v_ffn2_w_up', 'v_ffn2_w_down', 'v_ffn2_post_norm']
TWIN_OUTPUTS = ['loss', 'grad_x', 'grad_ffn1_pre_norm', 'grad_ffn1_w_gate', 'grad_ffn1_w_up', 'grad_ffn1_w_down', 'grad_ffn1_post_norm', 'grad_mix_pre_norm', 'grad_w_in', 'grad_conv_w', 'grad_conv_b', 'grad_dt_bias', 'grad_a_log', 'grad_d_skip', 'grad_ssm_norm', 'grad_w_out', 'grad_mix_post_norm', 'grad_ffn2_pre_norm', 'grad_ffn2_w_gate', 'grad_ffn2_w_up', 'grad_ffn2_w_down', 'grad_ffn2_post_norm', 'delta_ffn1_pre_norm', 'delta_ffn1_w_gate', 'delta_ffn1_w_up', 'delta_ffn1_w_down', 'delta_ffn1_post_norm', 'delta_mix_pre_norm', 'delta_w_in', 'delta_conv_w', 'delta_conv_b', 'delta_dt_bias', 'delta_a_log', 'delta_d_skip', 'delta_ssm_norm', 'delta_w_out', 'delta_mix_post_norm', 'delta_ffn2_pre_norm', 'delta_ffn2_w_gate', 'delta_ffn2_w_up', 'delta_ffn2_w_down', 'delta_ffn2_post_norm', 'new_m_ffn1_pre_norm', 'new_m_ffn1_w_gate', 'new_m_ffn1_w_up', 'new_m_ffn1_w_down', 'new_m_ffn1_post_norm', 'new_m_mix_pre_norm', 'new_m_w_in', 'new_m_conv_w', 'new_m_conv_b', 'new_m_dt_bias', 'new_m_a_log', 'new_m_d_skip', 'new_m_ssm_norm', 'new_m_w_out', 'new_m_mix_post_norm', 'new_m_ffn2_pre_norm', 'new_m_ffn2_w_gate', 'new_m_ffn2_w_up', 'new_m_ffn2_w_down', 'new_m_ffn2_post_norm', 'new_v_ffn1_pre_norm', 'new_v_ffn1_w_gate', 'new_v_ffn1_w_up', 'new_v_ffn1_w_down', 'new_v_ffn1_post_norm', 'new_v_mix_pre_norm', 'new_v_w_in', 'new_v_conv_w', 'new_v_conv_b', 'new_v_dt_bias', 'new_v_a_log', 'new_v_d_skip', 'new_v_ssm_norm', 'new_v_w_out', 'new_v_mix_post_norm', 'new_v_ffn2_pre_norm', 'new_v_ffn2_w_gate', 'new_v_ffn2_w_up', 'new_v_ffn2_w_down', 'new_v_ffn2_post_norm']
TWIN_LEAF_KINDS = {'loss': 'loss', 'grad_x': 'grad_x', 'grad_ffn1_pre_norm': 'grad_w', 'grad_ffn1_w_gate': 'grad_w', 'grad_ffn1_w_up': 'grad_w', 'grad_ffn1_w_down': 'grad_w', 'grad_ffn1_post_norm': 'grad_w', 'grad_mix_pre_norm': 'grad_w', 'grad_w_in': 'grad_w', 'grad_conv_w': 'grad_w', 'grad_conv_b': 'grad_w', 'grad_dt_bias': 'grad_w', 'grad_a_log': 'grad_w', 'grad_d_skip': 'grad_w', 'grad_ssm_norm': 'grad_w', 'grad_w_out': 'grad_w', 'grad_mix_post_norm': 'grad_w', 'grad_ffn2_pre_norm': 'grad_w', 'grad_ffn2_w_gate': 'grad_w', 'grad_ffn2_w_up': 'grad_w', 'grad_ffn2_w_down': 'grad_w', 'grad_ffn2_post_norm': 'grad_w', 'delta_ffn1_pre_norm': 'delta_w', 'delta_ffn1_w_gate': 'delta_w', 'delta_ffn1_w_up': 'delta_w', 'delta_ffn1_w_down': 'delta_w', 'delta_ffn1_post_norm': 'delta_w', 'delta_mix_pre_norm': 'delta_w', 'delta_w_in': 'delta_w', 'delta_conv_w': 'delta_w', 'delta_conv_b': 'delta_w', 'delta_dt_bias': 'delta_w', 'delta_a_log': 'delta_w', 'delta_d_skip': 'delta_w', 'delta_ssm_norm': 'delta_w', 'delta_w_out': 'delta_w', 'delta_mix_post_norm': 'delta_w', 'delta_ffn2_pre_norm': 'delta_w', 'delta_ffn2_w_gate': 'delta_w', 'delta_ffn2_w_up': 'delta_w', 'delta_ffn2_w_down': 'delta_w', 'delta_ffn2_post_norm': 'delta_w', 'new_m_ffn1_pre_norm': 'new_m', 'new_m_ffn1_w_gate': 'new_m', 'new_m_ffn1_w_up': 'new_m', 'new_m_ffn1_w_down': 'new_m', 'new_m_ffn1_post_norm': 'new_m', 'new_m_mix_pre_norm': 'new_m', 'new_m_w_in': 'new_m', 'new_m_conv_w': 'new_m', 'new_m_conv_b': 'new_m', 'new_m_dt_bias': 'new_m', 'new_m_a_log': 'new_m', 'new_m_d_skip': 'new_m', 'new_m_ssm_norm': 'new_m', 'new_m_w_out': 'new_m', 'new_m_mix_post_norm': 'new_m', 'new_m_ffn2_pre_norm': 'new_m', 'new_m_ffn2_w_gate': 'new_m', 'new_m_ffn2_w_up': 'new_m', 'new_m_ffn2_w_down': 'new_m', 'new_m_ffn2_post_norm': 'new_m', 'new_v_ffn1_pre_norm': 'new_v', 'new_v_ffn1_w_gate': 'new_v', 'new_v_ffn1_w_up': 'new_v', 'new_v_ffn1_w_down': 'new_v', 'new_v_ffn1_post_norm': 'new_v', 'new_v_mix_pre_norm': 'new_v', 'new_v_w_in': 'new_v', 'new_v_conv_w': 'new_v', 'new_v_conv_b': 'new_v', 'new_v_dt_bias': 'new_v', 'new_v_a_log': 'new_v', 'new_v_d_skip': 'new_v', 'new_v_ssm_norm': 'new_v', 'new_v_w_out': 'new_v', 'new_v_mix_post_norm': 'new_v', 'new_v_ffn2_pre_norm': 'new_v', 'new_v_ffn2_w_gate': 'new_v', 'new_v_ffn2_w_up': 'new_v', 'new_v_ffn2_w_down': 'new_v', 'new_v_ffn2_post_norm': 'new_v'}


def _forward(args):
    return _fwd_reference(*[args[k] for k in FWD_PARAMS])


def _output_shape():
    out = _jax.eval_shape(lambda: _forward(_fwd_setup_inputs(0)))
    return out.shape, out.dtype

N_MICROBATCH = 1
ADAM_LR = 0.001
ADAM_B1 = 0.9
ADAM_B2 = 0.999
ADAM_EPS = 1e-08
ADAM_WD = 0.01
ADAM_STEP = 10
PER_EXAMPLE_BATCH_AXIS = {'x': 0, 'positions': 0, 'loss_target': 0}
SHARED_INPUTS = []
_WEIGHT_DTYPES = {'ffn1_pre_norm': _jnp.float32, 'ffn1_w_gate': _jnp.float32, 'ffn1_w_up': _jnp.float32, 'ffn1_w_down': _jnp.float32, 'ffn1_post_norm': _jnp.float32, 'mix_pre_norm': _jnp.float32, 'w_in': _jnp.float32, 'conv_w': _jnp.float32, 'conv_b': _jnp.float32, 'dt_bias': _jnp.float32, 'a_log': _jnp.float32, 'd_skip': _jnp.float32, 'ssm_norm': _jnp.float32, 'w_out': _jnp.float32, 'mix_post_norm': _jnp.float32, 'ffn2_pre_norm': _jnp.float32, 'ffn2_w_gate': _jnp.float32, 'ffn2_w_up': _jnp.float32, 'ffn2_w_down': _jnp.float32, 'ffn2_post_norm': _jnp.float32}
MOMENT_SCALE = {'ffn1_pre_norm': 3.015105e-01, 'ffn1_w_gate': 1.255419e-01, 'ffn1_w_up': 1.284809e-01, 'ffn1_w_down': 2.154545e-01, 'ffn1_post_norm': 3.984799e+00, 'mix_pre_norm': 4.030506e-01, 'w_in': 1.960525e-01, 'conv_w': 2.438191e-01, 'conv_b': 4.852676e-01, 'dt_bias': 9.383494e-01, 'a_log': 9.879446e-01, 'd_skip': 1.854160e+00, 'ssm_norm': 3.350929e-01, 'w_out': 3.274065e-01, 'mix_post_norm': 1.608804e+01, 'ffn2_pre_norm': 2.323718e-01, 'ffn2_w_gate': 7.875065e-02, 'ffn2_w_up': 1.074279e-01, 'ffn2_w_down': 1.784296e-01, 'ffn2_post_norm': 4.001060e+00}


def _to_microbatches(a, axis):
    t = _jnp.moveaxis(a, axis, 0)
    t = t.reshape((N_MICROBATCH, t.shape[0] // N_MICROBATCH) + t.shape[1:])
    return _jnp.moveaxis(t, 1, axis + 1)


def setup_inputs(seed: int = 0) -> dict:
    inp = _fwd_setup_inputs(seed)
    key = _jax.random.fold_in(_jax.random.key(seed), 7919)
    shape, _ = _output_shape()
    out = dict(inp)
    out["loss_target"] = _jax.random.normal(_jax.random.fold_in(key, 0), shape, _jnp.float32)
    for i, name in enumerate(TWIN_WEIGHTS):
        w = inp[name].astype(_jnp.float32)
        if MOMENT_SCALE is None:
            s = _jnp.sqrt(_jnp.mean(_jnp.square(w)) + 1e-30)
        else:
            s = MOMENT_SCALE[name]
        km, kv = _jax.random.split(_jax.random.fold_in(key, i + 1))
        out[name] = w
        out["m_" + name] = s * _jax.random.normal(km, w.shape, _jnp.float32)
        out["v_" + name] = (s * s) * _jax.random.uniform(kv, w.shape, _jnp.float32, 0.5, 1.5)
    if N_MICROBATCH > 1:
        for name, axis in PER_EXAMPLE_BATCH_AXIS.items():
            out[name] = _to_microbatches(out[name], axis)
    return {'x': out['x'], 'positions': out['positions'], 'ffn1_pre_norm': out['ffn1_pre_norm'], 'ffn1_w_gate': out['ffn1_w_gate'], 'ffn1_w_up': out['ffn1_w_up'], 'ffn1_w_down': out['ffn1_w_down'], 'ffn1_post_norm': out['ffn1_post_norm'], 'mix_pre_norm': out['mix_pre_norm'], 'w_in': out['w_in'], 'conv_w': out['conv_w'], 'conv_b': out['conv_b'], 'dt_bias': out['dt_bias'], 'a_log': out['a_log'], 'd_skip': out['d_skip'], 'ssm_norm': out['ssm_norm'], 'w_out': out['w_out'], 'mix_post_norm': out['mix_post_norm'], 'ffn2_pre_norm': out['ffn2_pre_norm'], 'ffn2_w_gate': out['ffn2_w_gate'], 'ffn2_w_up': out['ffn2_w_up'], 'ffn2_w_down': out['ffn2_w_down'], 'ffn2_post_norm': out['ffn2_post_norm'], 'loss_target': out['loss_target'], 'm_ffn1_pre_norm': out['m_ffn1_pre_norm'], 'm_ffn1_w_gate': out['m_ffn1_w_gate'], 'm_ffn1_w_up': out['m_ffn1_w_up'], 'm_ffn1_w_down': out['m_ffn1_w_down'], 'm_ffn1_post_norm': out['m_ffn1_post_norm'], 'm_mix_pre_norm': out['m_mix_pre_norm'], 'm_w_in': out['m_w_in'], 'm_conv_w': out['m_conv_w'], 'm_conv_b': out['m_conv_b'], 'm_dt_bias': out['m_dt_bias'], 'm_a_log': out['m_a_log'], 'm_d_skip': out['m_d_skip'], 'm_ssm_norm': out['m_ssm_norm'], 'm_w_out': out['m_w_out'], 'm_mix_post_norm': out['m_mix_post_norm'], 'm_ffn2_pre_norm': out['m_ffn2_pre_norm'], 'm_ffn2_w_gate': out['m_ffn2_w_gate'], 'm_ffn2_w_up': out['m_ffn2_w_up'], 'm_ffn2_w_down': out['m_ffn2_w_down'], 'm_ffn2_post_norm': out['m_ffn2_post_norm'], 'v_ffn1_pre_norm': out['v_ffn1_pre_norm'], 'v_ffn1_w_gate': out['v_ffn1_w_gate'], 'v_ffn1_w_up': out['v_ffn1_w_up'], 'v_ffn1_w_down': out['v_ffn1_w_down'], 'v_ffn1_post_norm': out['v_ffn1_post_norm'], 'v_mix_pre_norm': out['v_mix_pre_norm'], 'v_w_in': out['v_w_in'], 'v_conv_w': out['v_conv_w'], 'v_conv_b': out['v_conv_b'], 'v_dt_bias': out['v_dt_bias'], 'v_a_log': out['v_a_log'], 'v_d_skip': out['v_d_skip'], 'v_ssm_norm': out['v_ssm_norm'], 'v_w_out': out['v_w_out'], 'v_mix_post_norm': out['v_mix_post_norm'], 'v_ffn2_pre_norm': out['v_ffn2_pre_norm'], 'v_ffn2_w_gate': out['v_ffn2_w_gate'], 'v_ffn2_w_up': out['v_ffn2_w_up'], 'v_ffn2_w_down': out['v_ffn2_w_down'], 'v_ffn2_post_norm': out['v_ffn2_post_norm']}


def _loss(weights, diff, rest, loss_target):
    with _jax.named_scope("forward"):
        args = {**rest, TWIN_DIFF_INPUT: diff, **{k: w.astype(_WEIGHT_DTYPES[k]) for k, w in weights.items()}}
        y = _forward(args)
    with _jax.named_scope("loss_head"):
        err = _jnp.square(y.astype(_jnp.float32) - loss_target)
        return 0.5 * _jnp.sum(_jnp.mean(err, axis=-1)) if err.ndim else 0.5 * err


def _adamw(w, g, m, v):
    m = ADAM_B1 * m + (1.0 - ADAM_B1) * g
    v = ADAM_B2 * v + (1.0 - ADAM_B2) * _jnp.square(g)
    m_hat = m / (1.0 - ADAM_B1 ** ADAM_STEP)
    v_hat = v / (1.0 - ADAM_B2 ** ADAM_STEP)
    delta = -ADAM_LR * (m_hat / (_jnp.sqrt(v_hat) + ADAM_EPS) + ADAM_WD * w)
    return delta, m, v


def reference(x, positions, ffn1_pre_norm, ffn1_w_gate, ffn1_w_up, ffn1_w_down, ffn1_post_norm, mix_pre_norm, w_in, conv_w, conv_b, dt_bias, a_log, d_skip, ssm_norm, w_out, mix_post_norm, ffn2_pre_norm, ffn2_w_gate, ffn2_w_up, ffn2_w_down, ffn2_post_norm, loss_target, m_ffn1_pre_norm, m_ffn1_w_gate, m_ffn1_w_up, m_ffn1_w_down, m_ffn1_post_norm, m_mix_pre_norm, m_w_in, m_conv_w, m_conv_b, m_dt_bias, m_a_log, m_d_skip, m_ssm_norm, m_w_out, m_mix_post_norm, m_ffn2_pre_norm, m_ffn2_w_gate, m_ffn2_w_up, m_ffn2_w_down, m_ffn2_post_norm, v_ffn1_pre_norm, v_ffn1_w_gate, v_ffn1_w_up, v_ffn1_w_down, v_ffn1_post_norm, v_mix_pre_norm, v_w_in, v_conv_w, v_conv_b, v_dt_bias, v_a_log, v_d_skip, v_ssm_norm, v_w_out, v_mix_post_norm, v_ffn2_pre_norm, v_ffn2_w_gate, v_ffn2_w_up, v_ffn2_w_down, v_ffn2_post_norm):
    given = dict(x=x, positions=positions, ffn1_pre_norm=ffn1_pre_norm, ffn1_w_gate=ffn1_w_gate, ffn1_w_up=ffn1_w_up, ffn1_w_down=ffn1_w_down, ffn1_post_norm=ffn1_post_norm, mix_pre_norm=mix_pre_norm, w_in=w_in, conv_w=conv_w, conv_b=conv_b, dt_bias=dt_bias, a_log=a_log, d_skip=d_skip, ssm_norm=ssm_norm, w_out=w_out, mix_post_norm=mix_post_norm, ffn2_pre_norm=ffn2_pre_norm, ffn2_w_gate=ffn2_w_gate, ffn2_w_up=ffn2_w_up, ffn2_w_down=ffn2_w_down, ffn2_post_norm=ffn2_post_norm, loss_target=loss_target, m_ffn1_pre_norm=m_ffn1_pre_norm, m_ffn1_w_gate=m_ffn1_w_gate, m_ffn1_w_up=m_ffn1_w_up, m_ffn1_w_down=m_ffn1_w_down, m_ffn1_post_norm=m_ffn1_post_norm, m_mix_pre_norm=m_mix_pre_norm, m_w_in=m_w_in, m_conv_w=m_conv_w, m_conv_b=m_conv_b, m_dt_bias=m_dt_bias, m_a_log=m_a_log, m_d_skip=m_d_skip, m_ssm_norm=m_ssm_norm, m_w_out=m_w_out, m_mix_post_norm=m_mix_post_norm, m_ffn2_pre_norm=m_ffn2_pre_norm, m_ffn2_w_gate=m_ffn2_w_gate, m_ffn2_w_up=m_ffn2_w_up, m_ffn2_w_down=m_ffn2_w_down, m_ffn2_post_norm=m_ffn2_post_norm, v_ffn1_pre_norm=v_ffn1_pre_norm, v_ffn1_w_gate=v_ffn1_w_gate, v_ffn1_w_up=v_ffn1_w_up, v_ffn1_w_down=v_ffn1_w_down, v_ffn1_post_norm=v_ffn1_post_norm, v_mix_pre_norm=v_mix_pre_norm, v_w_in=v_w_in, v_conv_w=v_conv_w, v_conv_b=v_conv_b, v_dt_bias=v_dt_bias, v_a_log=v_a_log, v_d_skip=v_d_skip, v_ssm_norm=v_ssm_norm, v_w_out=v_w_out, v_mix_post_norm=v_mix_post_norm, v_ffn2_pre_norm=v_ffn2_pre_norm, v_ffn2_w_gate=v_ffn2_w_gate, v_ffn2_w_up=v_ffn2_w_up, v_ffn2_w_down=v_ffn2_w_down, v_ffn2_post_norm=v_ffn2_post_norm)
    weights = {n: given[n] for n in TWIN_WEIGHTS}
    shared = {n: given[n] for n in SHARED_INPUTS}
    per_example = {n: given[n] for n in ['x', 'positions']}
    grad_fn = _jax.value_and_grad(_loss, argnums=(0, 1))

    def one_microbatch(ex, loss_target):
        ex = dict(ex)
        diff = ex.pop(TWIN_DIFF_INPUT)
        return grad_fn(weights, diff, {**shared, **ex}, loss_target)

    if N_MICROBATCH == 1:
        loss, (grad_w, grad_x) = one_microbatch(per_example, given["loss_target"])
    else:
        def body(carry, xs):
            loss_sum, grad_sum = carry
            l_k, (gw_k, gx_k) = one_microbatch(xs[0], xs[1])
            with _jax.named_scope("update"):
                return (loss_sum + l_k, _jax.tree.map(_jnp.add, grad_sum, gw_k)), gx_k

        init = (_jnp.zeros((), _jnp.float32), _jax.tree.map(_jnp.zeros_like, weights))
        (loss, grad_w), grad_x = _jax.lax.scan(body, init, (per_example, given["loss_target"]))
    with _jax.named_scope("update"):
        delta_w, new_m, new_v = {}, {}, {}
        for n in TWIN_WEIGHTS:
            delta_w[n], new_m[n], new_v[n] = _adamw(weights[n], grad_w[n], given["m_" + n], given["v_" + n])
    return (loss, grad_x, *[grad_w[n] for n in TWIN_WEIGHTS], *[delta_w[n] for n in TWIN_WEIGHTS],
            *[new_m[n] for n in TWIN_WEIGHTS], *[new_v[n] for n in TWIN_WEIGHTS])
```

```python
import functools
import math

import jax
import jax.numpy as jnp
from jax import lax
from jax.experimental import pallas as pl
from jax.experimental.pallas import tpu as pltpu

F32 = jnp.float32
BF16 = jnp.bfloat16
HI = lax.Precision.HIGHEST
MESH = pl.DeviceIdType.MESH

N_DEV = 8
T = 2048
D = 1024
FF = 2816
HD = 64
NQ = 16
NKV = 4
QW = NQ * HD
KVW = NKV * HD
SSM_W = 1024
SSM_H = 16
SSM_N = 128
CONV_C = SSM_W + 2 * 2 * SSM_N
IN_COLS = 4112
L = 128
NCH = T // L
AB = 256
NAB = T // AB
EPS = 1e-6
NEG = -1e30
ROPE_THETA = 10000.0
DILATIONS = ((128, 1), (512, 4), (2048, 16))

ADAM_LR = 0.001
ADAM_B1 = 0.9
ADAM_B2 = 0.999
ADAM_EPS = 1e-08
ADAM_WD = 0.01
ADAM_STEP = 10

VMEM_LIMIT = 58 * 1024 * 1024


def _params(sem, vmem=VMEM_LIMIT):
    return pltpu.CompilerParams(dimension_semantics=sem, vmem_limit_bytes=vmem)


def _dot(a, b):
    return jnp.dot(a, b, preferred_element_type=F32)


def _dot_nt(a, b):
    return lax.dot_general(a, b, (((1,), (1,)), ((), ())), preferred_element_type=F32)


def _dot_tn(a, b):
    return lax.dot_general(a, b, (((0,), (0,)), ((), ())), preferred_element_type=F32)


def _dot_hi(a, b):
    return jnp.dot(a, b, preferred_element_type=F32, precision=HI)


def _dot_nt_hi(a, b):
    return lax.dot_general(a, b, (((1,), (1,)), ((), ())), preferred_element_type=F32, precision=HI)


def _rs(x):
    return lax.rsqrt(jnp.mean(x * x, axis=-1, keepdims=True) + EPS)


def _sigmoid(x):
    return jax.nn.sigmoid(x)


def _dsilu(x, s):
    return s * (1.0 + x * (1.0 - s))


def _resident(shape):
    nd = len(shape)
    return pl.BlockSpec(shape, lambda *_: (0,) * nd, pipeline_mode=pl.Buffered(1))


def _const(shape):
    nd = len(shape)
    return pl.BlockSpec(shape, lambda *_: (0,) * nd)


def _rows(tm, cols):
    return pl.BlockSpec((tm, cols), lambda i: (i, 0))


def _ffn_fwd(x, gpre, wg, wu, wd, gpost, name):
    tm = 256

    def body(x_ref, gpre_ref, wg_ref, wu_ref, wd_ref, gpost_ref, xo_ref, n_ref, a_ref, b_ref, hm_ref, h_ref):
        xv = x_ref[...]
        n = (xv * _rs(xv) * gpre_ref[...]).astype(BF16)
        a = _dot(n, wg_ref[...])
        b = _dot(n, wu_ref[...])
        hm = (a * _sigmoid(a) * b).astype(BF16)
        h = _dot(hm, wd_ref[...])
        xo_ref[...] = xv + 0.5 * (h * _rs(h) * gpost_ref[...])
        n_ref[...] = n
        a_ref[...] = a.astype(BF16)
        b_ref[...] = b.astype(BF16)
        hm_ref[...] = hm
        h_ref[...] = h

    return pl.pallas_call(
        body, name=name, grid=(T // tm,),
        in_specs=[_rows(tm, D), _const((1, D)), _resident((D, FF)), _resident((D, FF)), _resident((FF, D)), _const((1, D))],
        out_specs=[_rows(tm, D), _rows(tm, D), _rows(tm, FF), _rows(tm, FF), _rows(tm, FF), _rows(tm, D)],
        out_shape=[jax.ShapeDtypeStruct((T, D), F32), jax.ShapeDtypeStruct((T, D), BF16), jax.ShapeDtypeStruct((T, FF), BF16),
                   jax.ShapeDtypeStruct((T, FF), BF16), jax.ShapeDtypeStruct((T, FF), BF16), jax.ShapeDtypeStruct((T, D), F32)],
        compiler_params=_params(("parallel",)),
    )(x, gpre, wg, wu, wd, gpost)


def _ffn_bwd(dxo, x, a, b, h, gpre, gpost, wg, wu, wd, name):
    tm = 256

    def body(dxo_ref, x_ref, a_ref, b_ref, h_ref, gpre_ref, gpost_ref, wg_ref, wu_ref, wd_ref,
             dx_ref, da_ref, db_ref, dh_ref, dgpre_ref, dgpost_ref):
        @pl.when(pl.program_id(0) == 0)
        def _():
            dgpre_ref[...] = jnp.zeros_like(dgpre_ref)
            dgpost_ref[...] = jnp.zeros_like(dgpost_ref)

        dy = dxo_ref[...]
        h = h_ref[...]
        hn = h * _rs(h)
        r2 = _rs(h)
        dgpost_ref[...] += jnp.sum(0.5 * dy * hn, axis=0, keepdims=True)
        gdy = 0.5 * dy * gpost_ref[...]
        dh = r2 * (gdy - hn * jnp.mean(gdy * hn, axis=-1, keepdims=True))
        dhb = dh.astype(BF16)
        dh_ref[...] = dhb
        dhm = _dot_nt(dhb, wd_ref[...])
        av = a_ref[...].astype(F32)
        bv = b_ref[...].astype(F32)
        sg = _sigmoid(av)
        db = (dhm * (av * sg)).astype(BF16)
        da = (dhm * bv * _dsilu(av, sg)).astype(BF16)
        da_ref[...] = da
        db_ref[...] = db
        dn = _dot_nt(da, wg_ref[...]) + _dot_nt(db, wu_ref[...])
        xv = x_ref[...]
        r = _rs(xv)
        xn = xv * r
        dgpre_ref[...] += jnp.sum(dn * xn, axis=0, keepdims=True)
        gdn = dn * gpre_ref[...]
        dx_ref[...] = dy + r * (gdn - xn * jnp.mean(gdn * xn, axis=-1, keepdims=True))

    return pl.pallas_call(
        body, name=name, grid=(T // tm,),
        in_specs=[_rows(tm, D), _rows(tm, D), _rows(tm, FF), _rows(tm, FF), _rows(tm, D), _const((1, D)), _const((1, D)),
                  _resident((D, FF)), _resident((D, FF)), _resident((FF, D))],
        out_specs=[_rows(tm, D), _rows(tm, FF), _rows(tm, FF), _rows(tm, D), _const((1, D)), _const((1, D))],
        out_shape=[jax.ShapeDtypeStruct((T, D), F32), jax.ShapeDtypeStruct((T, FF), BF16), jax.ShapeDtypeStruct((T, FF), BF16),
                   jax.ShapeDtypeStruct((T, D), BF16), jax.ShapeDtypeStruct((1, D), F32), jax.ShapeDtypeStruct((1, D), F32)],
        compiler_params=_params(("arbitrary",)),
    )(dxo, x, a, b, h, gpre, gpost, wg, wu, wd)


def _matmul_tn(a, b, name, tm=256):
    k, m = a.shape
    n = b.shape[1]

    def body(a_ref, b_ref, o_ref):
        o_ref[...] = _dot_tn(a_ref[...], b_ref[...])

    return pl.pallas_call(
        body, name=name, grid=(m // tm,),
        in_specs=[pl.BlockSpec((k, tm), lambda i: (0, i)), _resident((k, n))],
        out_specs=pl.BlockSpec((tm, n), lambda i: (i, 0)),
        out_shape=jax.ShapeDtypeStruct((m, n), F32),
        compiler_params=_params(("parallel",)),
    )(a, b)


def _rope_swap(t, lo_half):
    return jnp.where(lo_half, pltpu.roll(t, 96, 1), pltpu.roll(t, 32, 1))


def _inproj_fwd(x1, gpre, win, cos, sin_s, ex):
    tm = 256

    def body(x_ref, g_ref, w_ref, cos_ref, sin_ref, ex_ref, n_ref, q_ref, kx_ref, vx_ref, xbc_ref, z_ref, dt_ref):
        xv = x_ref[...]
        n = (xv * _rs(xv) * g_ref[...]).astype(BF16)
        n_ref[...] = n
        proj = _dot(n, w_ref[...])
        cs = cos_ref[...]
        sn = sin_ref[...]
        lo_half = (lax.broadcasted_iota(jnp.int32, (1, 128), 1) % HD) < (HD // 2)

        def rope(t):
            return t * cs + _rope_swap(t, lo_half) * sn

        for j in range(QW // 128):
            t = proj[:, 128 * j:128 * j + 128]
            q_ref[:, 128 * j:128 * j + 128] = (rope(t) * (HD ** -0.5)).astype(BF16)
        k = jnp.concatenate([rope(proj[:, QW + 128 * j:QW + 128 * j + 128]) for j in range(KVW // 128)], axis=1)
        v = proj[:, QW + KVW:QW + 2 * KVW]
        kx_ref[...] = _dot(k.astype(BF16), ex_ref[...]).astype(BF16)
        vx_ref[...] = _dot(v.astype(BF16), ex_ref[...]).astype(BF16)
        c0 = QW + 2 * KVW
        xbc_ref[...] = proj[:, c0:c0 + CONV_C]
        z_ref[...] = proj[:, c0 + CONV_C:c0 + CONV_C + SSM_W]
        dt_ref[...] = proj[:, c0 + CONV_C + SSM_W:IN_COLS]

    return pl.pallas_call(
        body, name="inproj_fwd", grid=(T // tm,),
        in_specs=[_rows(tm, D), _const((1, D)), _resident((D, IN_COLS)), _rows(tm, 128), _rows(tm, 128), _const((KVW, QW))],
        out_specs=[_rows(tm, D), _rows(tm, QW), _rows(tm, QW), _rows(tm, QW), _rows(tm, CONV_C), _rows(tm, SSM_W), _rows(tm, SSM_H)],
        out_shape=[jax.ShapeDtypeStruct((T, D), BF16), jax.ShapeDtypeStruct((T, QW), BF16), jax.ShapeDtypeStruct((T, QW), BF16),
                   jax.ShapeDtypeStruct((T, QW), BF16), jax.ShapeDtypeStruct((T, CONV_C), F32), jax.ShapeDtypeStruct((T, SSM_W), F32),
                   jax.ShapeDtypeStruct((T, SSM_H), F32)],
        compiler_params=_params(("parallel",)),
    )(x1, gpre, win, cos, sin_s, ex)


def _inproj_bwd(dres, dq, dkx, dvx, dxbc, dz, ddt, x1, gpre, win, cos, sin_s, exf):
    tm = 256

    def body(dres_ref, dq_ref, dkx_ref, dvx_ref, dxbc_ref, dz_ref, ddt_ref, x_ref, g_ref, w_ref, cos_ref, sin_ref, ex_ref,
             dx_ref, dp_ref, dg_ref):
        @pl.when(pl.program_id(0) == 0)
        def _():
            dg_ref[...] = jnp.zeros_like(dg_ref)

        cs = cos_ref[...]
        sn = sin_ref[...]
        lo_half = (lax.broadcasted_iota(jnp.int32, (1, 128), 1) % HD) < (HD // 2)

        def rope_t(t):
            return t * cs - _rope_swap(t, lo_half) * sn

        for j in range(QW // 128):
            dp_ref[:, 128 * j:128 * j + 128] = rope_t(dq_ref[:, 128 * j:128 * j + 128] * (HD ** -0.5)).astype(BF16)
        dk = _dot_nt_hi(dkx_ref[...], ex_ref[...])
        dv = _dot_nt_hi(dvx_ref[...], ex_ref[...])
        for j in range(KVW // 128):
            dp_ref[:, QW + 128 * j:QW + 128 * j + 128] = rope_t(dk[:, 128 * j:128 * j + 128]).astype(BF16)
        dp_ref[:, QW + KVW:QW + 2 * KVW] = dv.astype(BF16)
        c0 = QW + 2 * KVW
        dp_ref[:, c0:c0 + CONV_C] = dxbc_ref[...].astype(BF16)
        dp_ref[:, c0 + CONV_C:c0 + CONV_C + SSM_W] = dz_ref[...].astype(BF16)
        dp_ref[:, c0 + CONV_C + SSM_W:IN_COLS] = ddt_ref[...].astype(BF16)
        dn = _dot_nt(dp_ref[...], w_ref[...])
        xv = x_ref[...]
        r = _rs(xv)
        xn = xv * r
        dg_ref[...] += jnp.sum(dn * xn, axis=0, keepdims=True)
        gdn = dn * g_ref[...]
        dx_ref[...] = dres_ref[...] + r * (gdn - xn * jnp.mean(gdn * xn, axis=-1, keepdims=True))

    return pl.pallas_call(
        body, name="inproj_bwd", grid=(T // tm,),
        in_specs=[_rows(tm, D), _rows(tm, QW), _rows(tm, QW), _rows(tm, QW), _rows(tm, CONV_C), _rows(tm, SSM_W), _rows(tm, SSM_H),
                  _rows(tm, D), _const((1, D)), _resident((D, IN_COLS)), _rows(tm, 128), _rows(tm, 128), _const((KVW, QW))],
        out_specs=[_rows(tm, D), _rows(tm, IN_COLS), _const((1, D))],
        out_shape=[jax.ShapeDtypeStruct((T, D), F32), jax.ShapeDtypeStruct((T, IN_COLS), BF16), jax.ShapeDtypeStruct((1, D), F32)],
        compiler_params=_params(("arbitrary",)),
    )(dres, dq, dkx, dvx, dxbc, dz, ddt, x1, gpre, win, cos, sin_s, exf)


def _outproj_fwd(x1, attn, yn, wout, gpost):
    tm = 256

    def body(x_ref, at_ref, yn_ref, w_ref, g_ref, xo_ref, h_ref):
        h = _dot(at_ref[...], w_ref[0:QW, :]) + _dot(yn_ref[...], w_ref[QW:QW + SSM_W, :])
        h_ref[...] = h
        xo_ref[...] = x_ref[...] + h * _rs(h) * g_ref[...]

    return pl.pallas_call(
        body, name="outproj_fwd", grid=(T // tm,),
        in_specs=[_rows(tm, D), _rows(tm, QW), _rows(tm, SSM_W), _resident((QW + SSM_W, D)), _const((1, D))],
        out_specs=[_rows(tm, D), _rows(tm, D)],
        out_shape=[jax.ShapeDtypeStruct((T, D), F32), jax.ShapeDtypeStruct((T, D), F32)],
        compiler_params=_params(("parallel",)),
    )(x1, attn, yn, wout, gpost)


def _outproj_bwd(dx2, h2, gpost, wout):
    tm = 256

    def body(dy_ref, h_ref, g_ref, w_ref, dh_ref, dm_ref, dg_ref):
        @pl.when(pl.program_id(0) == 0)
        def _():
            dg_ref[...] = jnp.zeros_like(dg_ref)

        dy = dy_ref[...]
        h = h_ref[...]
        r = _rs(h)
        hn = h * r
        dg_ref[...] += jnp.sum(dy * hn, axis=0, keepdims=True)
        gdy = dy * g_ref[...]
        dh = (r * (gdy - hn * jnp.mean(gdy * hn, axis=-1, keepdims=True))).astype(BF16)
        dh_ref[...] = dh
        dm_ref[...] = _dot_nt(dh, w_ref[...])

    return pl.pallas_call(
        body, name="outproj_bwd", grid=(T // tm,),
        in_specs=[_rows(tm, D), _rows(tm, D), _const((1, D)), _resident((QW + SSM_W, D))],
        out_specs=[_rows(tm, D), _rows(tm, QW + SSM_W), _const((1, D))],
        out_shape=[jax.ShapeDtypeStruct((T, D), BF16), jax.ShapeDtypeStruct((T, QW + SSM_W), F32), jax.ShapeDtypeStruct((1, D), F32)],
        compiler_params=_params(("arbitrary",)),
    )(dx2, h2, gpost, wout)


def _attn_bias():
    d = jnp.arange(AB)[None, :, None] - jnp.arange(AB)[None, None, :] + AB * jnp.arange(NAB)[:, None, None]
    cnt = jnp.zeros(d.shape, F32)
    for window, dil in DILATIONS:
        cnt = cnt + ((d >= 0) & (d % dil == 0) & (d <= window)).astype(F32)
    return jnp.where(cnt > 0, jnp.log(jnp.maximum(cnt, 1.0)), NEG)


def _attn_fwd(q, kx, vx, bias):
    g_per = NQ // NKV
    wk = g_per * HD

    def body(q_ref, kx_ref, vx_ref, bias_ref, o_ref, lse_ref):
        i = pl.program_id(1)
        qv = q_ref[...]
        lane = lax.broadcasted_iota(jnp.int32, (1, wk), 1)
        lse_ref[...] = jnp.zeros_like(lse_ref)
        o_acc = jnp.zeros((AB, wk), F32)
        for g in range(g_per):
            mg = (lane // HD) == g
            qm = jnp.where(mg, qv, jnp.zeros_like(qv))

            def step(j, carry, mg=mg, qm=qm):
                m, l, acc = carry
                off = pl.multiple_of(j * AB, AB)
                kj = kx_ref[pl.ds(off, AB), :]
                vj = vx_ref[pl.ds(off, AB), :]
                vj = jnp.where(mg, vj, jnp.zeros_like(vj))
                s = _dot_nt(qm, kj) + bias_ref[i - j]
                m_new = jnp.maximum(m, jnp.max(s, axis=1, keepdims=True))
                p = jnp.exp(s - m_new)
                alpha = jnp.exp(m - m_new)
                l = alpha * l + jnp.sum(p, axis=1, keepdims=True)
                acc = alpha * acc + _dot(p.astype(BF16), vj)
                return m_new, l, acc

            m, l, acc = lax.fori_loop(0, i + 1, step,
                                      (jnp.full((AB, 1), NEG, F32), jnp.zeros((AB, 1), F32), jnp.zeros((AB, wk), F32)))
            o_acc = o_acc + acc / l
            lse_ref[:, g:g + 1] = m + jnp.log(l)
        o_ref[...] = o_acc.astype(BF16)

    return pl.pallas_call(
        body, name="attn_fwd", grid=(NKV, NAB),
        in_specs=[pl.BlockSpec((AB, wk), lambda kv, i: (i, kv)), pl.BlockSpec((T, wk), lambda kv, i: (0, kv)),
                  pl.BlockSpec((T, wk), lambda kv, i: (0, kv)), pl.BlockSpec((NAB, AB, AB), lambda kv, i: (0, 0, 0))],
        out_specs=[pl.BlockSpec((AB, wk), lambda kv, i: (i, kv)), pl.BlockSpec((AB, 128), lambda kv, i: (i, kv))],
        out_shape=[jax.ShapeDtypeStruct((T, QW), BF16), jax.ShapeDtypeStruct((T, NKV * 128), F32)],
        compiler_params=_params(("parallel", "parallel")),
    )(q, kx, vx, bias)


def _attn_bwd(q, kx, vx, o, dmix, lse, bias):
    g_per = NQ // NKV
    wk = g_per * HD

    def body(q_ref, kx_ref, vx_ref, o_ref, do_ref, lse_ref, bias_ref, dq_ref, dkx_ref, dvx_ref, delta):
        lane = lax.broadcasted_iota(jnp.int32, (1, wk), 1)
        dq_ref[...] = jnp.zeros_like(dq_ref)
        dkx_ref[...] = jnp.zeros_like(dkx_ref)
        dvx_ref[...] = jnp.zeros_like(dvx_ref)
        delta[...] = jnp.zeros_like(delta)
        prod = do_ref[...] * o_ref[...].astype(F32)
        for g in range(g_per):
            delta[:, g:g + 1] = jnp.sum(jnp.where((lane // HD) == g, prod, 0.0), axis=1, keepdims=True)

        def jloop(j, _):
            offj = pl.multiple_of(j * AB, AB)
            kj = kx_ref[pl.ds(offj, AB), :]
            vj = vx_ref[pl.ds(offj, AB), :]

            def iloop(i, _):
                offi = pl.multiple_of(i * AB, AB)
                qi = q_ref[pl.ds(offi, AB), :]
                doi = do_ref[pl.ds(offi, AB), :].astype(BF16)
                b = bias_ref[i - j]
                for g in range(g_per):
                    mg = (lane // HD) == g
                    qm = jnp.where(mg, qi, jnp.zeros_like(qi))
                    dom = jnp.where(mg, doi, jnp.zeros_like(doi))
                    s = _dot_nt(qm, kj) + b
                    p = jnp.exp(s - lse_ref[pl.ds(offi, AB), g:g + 1])
                    dp = _dot_nt(dom, vj)
                    ds = (p * (dp - delta[pl.ds(offi, AB), g:g + 1])).astype(BF16)
                    dvx_ref[pl.ds(offj, AB), :] += _dot_tn(p.astype(BF16), dom)
                    dkx_ref[pl.ds(offj, AB), :] += _dot_tn(ds, qm)
                    dq_ref[pl.ds(offi, AB), :] += _dot(ds, jnp.where(mg, kj, jnp.zeros_like(kj)))
                return 0

            lax.fori_loop(j, NAB, iloop, 0)
            return 0

        lax.fori_loop(0, NAB, jloop, 0)

    col = lambda kv: (0, kv)
    return pl.pallas_call(
        body, name="attn_bwd", grid=(NKV,),
        in_specs=[pl.BlockSpec((T, wk), col), pl.BlockSpec((T, wk), col), pl.BlockSpec((T, wk), col), pl.BlockSpec((T, wk), col),
                  pl.BlockSpec((T, wk), col), pl.BlockSpec((T, 128), col), pl.BlockSpec((NAB, AB, AB), lambda kv: (0, 0, 0))],
        out_specs=[pl.BlockSpec((T, wk), col), pl.BlockSpec((T, wk), col), pl.BlockSpec((T, wk), col)],
        out_shape=[jax.ShapeDtypeStruct((T, QW), F32)] * 3,
        scratch_shapes=[pltpu.VMEM((T, 128), F32)],
        compiler_params=_params(("parallel",)),
    )(q, kx, vx, o, dmix, lse, bias)


def _softplus(x):
    return jnp.maximum(x, 0.0) + jnp.log1p(jnp.exp(-jnp.abs(x)))


def _conv_taps(u, halo):
    zext = jnp.concatenate([halo, u], axis=0)
    return [pltpu.roll(zext, m, 0)[8:] for m in (1, 2, 3)]


def _ssd_chunk_common(u, halo, dtr, cw_ref, cb_ref, dtb_ref, alx_ref, e_ref):
    sh1, sh2, sh3 = _conv_taps(u, halo)
    xc = cb_ref[...] + cw_ref[3:4, :] * u + cw_ref[2:3, :] * sh1 + cw_ref[1:2, :] * sh2 + cw_ref[0:1, :] * sh3
    sg = _sigmoid(xc)
    act = xc * sg
    pre_x = _dot_hi(dtr + dtb_ref[...], e_ref[...])
    dt_x = _softplus(pre_x)
    a_x = -jnp.exp(alx_ref[...])
    ri = lax.broadcasted_iota(jnp.int32, (L, L), 0)
    ci = lax.broadcasted_iota(jnp.int32, (L, L), 1)
    tri = ri >= ci
    acs_x = _dot_hi(tri.astype(F32), dt_x * a_x)
    return dict(sh=(sh1, sh2, sh3), xc=xc, sg=sg, act=act, pre_x=pre_x, dt_x=dt_x, a_x=a_x, tri=tri, acs_x=acs_x)


def _decay(acs_x, acs_t, h, tri):
    col = acs_x[:, HD * h:HD * h + 1]
    row = acs_t[HD * h:HD * h + 1, :]
    return jnp.exp(jnp.where(tri, col - row, NEG))


def _ssd_fwd(xbc, z, dtr, convw, convb, dtb, alx, dskx, ssmn, e):
    def body(u_ref, z_ref, dtr_ref, cw_ref, cb_ref, dtb_ref, alx_ref, dsk_ref, sn_ref, e_ref,
             yn_ref, y_ref, hs_ref, halo, hst):
        @pl.when(pl.program_id(0) == 0)
        def _():
            halo[...] = jnp.zeros_like(halo)
            hst[...] = jnp.zeros_like(hst)

        u = u_ref[...]
        cm = _ssd_chunk_common(u, halo[...], dtr_ref[...], cw_ref, cb_ref, dtb_ref, alx_ref, e_ref)
        halo[...] = u[L - 8:, :]
        act, dt_x, acs_x, tri = cm["act"], cm["dt_x"], cm["acs_x"], cm["tri"]
        xs = act[:, :SSM_W]
        acs_l = acs_x[L - 1:L, :]
        lam_x = jnp.exp(acs_x)
        w_x = jnp.exp(acs_l - acs_x)
        gam_x = jnp.exp(acs_l)
        acs_t = acs_x.T
        xd = xs * dt_x
        xb = xd.astype(BF16)
        xw = (xd * w_x).astype(BF16)
        lo = lax.broadcasted_iota(jnp.int32, (1, 128), 1) < HD
        hs_ref[0] = hst[...]
        pieces = []
        for grp in range(2):
            bb = act[:, SSM_W + SSM_N * grp:SSM_W + SSM_N * (grp + 1)].astype(BF16)
            cb_ = act[:, SSM_W + 2 * SSM_N + SSM_N * grp:SSM_W + 2 * SSM_N + SSM_N * (grp + 1)].astype(BF16)
            cbm = _dot_nt(cb_, bb)
            for jj in range(4):
                j = 4 * grp + jj
                sl = slice(128 * j, 128 * j + 128)
                m0 = (cbm * _decay(acs_x, acs_t, 2 * j, tri)).astype(BF16)
                m1 = (cbm * _decay(acs_x, acs_t, 2 * j + 1, tri)).astype(BF16)
                x2 = xb[:, sl]
                ydiag = jnp.where(lo, _dot(m0, x2), _dot(m1, x2))
                hprev = hst[j]
                yoff = lam_x[:, sl] * _dot(cb_, hprev.astype(BF16))
                pieces.append(ydiag + yoff)
                hst[j] = gam_x[:, sl] * hprev + _dot_tn(bb, xw[:, sl])
        y = jnp.concatenate(pieces, axis=1) + dsk_ref[...] * xs
        y_ref[...] = y
        zv = z_ref[...]
        yz = y * (zv * _sigmoid(zv))
        half = SSM_W // 2
        yn = jnp.concatenate([yz[:, :half] * _rs(yz[:, :half]), yz[:, half:] * _rs(yz[:, half:])], axis=1)
        yn_ref[...] = (yn * sn_ref[...]).astype(BF16)

    return pl.pallas_call(
        body, name="ssd_fwd", grid=(NCH,),
        in_specs=[_rows(L, CONV_C), _rows(L, SSM_W), _rows(L, SSM_H), _const((4, CONV_C)), _const((1, CONV_C)), _const((1, SSM_H)),
                  _const((1, SSM_W)), _const((1, SSM_W)), _const((1, SSM_W)), _const((SSM_H, SSM_W))],
        out_specs=[_rows(L, SSM_W), _rows(L, SSM_W), pl.BlockSpec((1, 8, SSM_N, 128), lambda c: (c, 0, 0, 0))],
        out_shape=[jax.ShapeDtypeStruct((T, SSM_W), BF16), jax.ShapeDtypeStruct((T, SSM_W), F32),
                   jax.ShapeDtypeStruct((NCH, 8, SSM_N, 128), F32)],
        scratch_shapes=[pltpu.VMEM((8, CONV_C), F32), pltpu.VMEM((8, SSM_N, 128), F32)],
        compiler_params=_params(("arbitrary",)),
    )(xbc, z, dtr, convw, convb, dtb, alx, dskx, ssmn, e)


def _ssd_bwd(dmix, xbc, z, dtr, y, hs, convw, convb, dtb, alx, dskx, ssmn, e, e1):
    rev = lambda i: (NCH - 1 - i, 0)

    def body(dyn_ref, u_ref, uh_ref, z_ref, dtr_ref, y_ref, hs_ref, cw_ref, cb_ref, dtb_ref, alx_ref, dsk_ref, sn_ref, e_ref, e1_ref,
             dxbc_ref, dz_ref, ddt_ref, dcw_ref, dcb_ref, dsn_ref, dpar_ref, dh, duh, colbuf):
        step = pl.program_id(0)
        c = NCH - 1 - step

        @pl.when(step == 0)
        def _():
            for r in (dh, duh, dcw_ref, dcb_ref, dsn_ref, dpar_ref):
                r[...] = jnp.zeros_like(r)

        u = u_ref[...]
        halo = jnp.where(c > 0, uh_ref[...], 0.0)
        cm = _ssd_chunk_common(u, halo, dtr_ref[...], cw_ref, cb_ref, dtb_ref, alx_ref, e_ref)
        sh1, sh2, sh3 = cm["sh"]
        xc, sg, act, pre_x, dt_x, a_x, tri, acs_x = (cm[k] for k in ("xc", "sg", "act", "pre_x", "dt_x", "a_x", "tri", "acs_x"))
        xs = act[:, :SSM_W]
        acs_l = acs_x[L - 1:L, :]
        lam_x = jnp.exp(acs_x)
        w_x = jnp.exp(acs_l - acs_x)
        gam_x = jnp.exp(acs_l)
        acs_t = acs_x.T
        xd = xs * dt_x
        xb = xd.astype(BF16)
        xdw = xd * w_x
        xw = xdw.astype(BF16)
        lo = lax.broadcasted_iota(jnp.int32, (1, 128), 1) < HD
        row8 = lax.broadcasted_iota(jnp.int32, (8, 1), 0)

        dyn = dyn_ref[...]
        yv = y_ref[...]
        zv = z_ref[...]
        sz = _sigmoid(zv)
        siluz = zv * sz
        yz = yv * siluz
        half = SSM_W // 2
        gy = dyn * sn_ref[...]
        dyz_parts, yzn_parts = [], []
        for hf in range(2):
            part = yz[:, hf * half:(hf + 1) * half]
            r = _rs(part)
            pn = part * r
            gp = gy[:, hf * half:(hf + 1) * half]
            dyz_parts.append(r * (gp - pn * jnp.mean(gp * pn, axis=-1, keepdims=True)))
            yzn_parts.append(pn)
        dyz = jnp.concatenate(dyz_parts, axis=1)
        dsn_ref[...] += jnp.sum(dyn * jnp.concatenate(yzn_parts, axis=1), axis=0, keepdims=True)
        dy = dyz * siluz
        dz_ref[...] = dyz * yv * _dsilu(zv, sz)

        colbuf[...] = jnp.zeros_like(colbuf)
        dx_pieces, dacs_pieces, dacsl_pieces, db_pieces, dc_pieces = [], [], [], [], []
        for grp in range(2):
            bb = act[:, SSM_W + SSM_N * grp:SSM_W + SSM_N * (grp + 1)].astype(BF16)
            cb_ = act[:, SSM_W + 2 * SSM_N + SSM_N * grp:SSM_W + 2 * SSM_N + SSM_N * (grp + 1)].astype(BF16)
            cbm = _dot_nt(cb_, bb)
            dcbm = jnp.zeros((L, L), F32)
            dc_g = jnp.zeros((L, SSM_N), F32)
            db_g = jnp.zeros((L, SSM_N), F32)
            for jj in range(4):
                j = 4 * grp + jj
                sl = slice(128 * j, 128 * j + 128)
                dy2 = dy[:, sl]
                dy2b = dy2.astype(BF16)
                d0 = _decay(acs_x, acs_t, 2 * j, tri)
                d1 = _decay(acs_x, acs_t, 2 * j + 1, tri)
                m0 = cbm * d0
                m1 = cbm * d1
                x2 = xb[:, sl]
                hprev = hs_ref[0, j]
                hprevb = hprev.astype(BF16)
                dhn = dh[j]
                dhnb = dhn.astype(BF16)
                g2 = _dot(bb, dhnb)
                dx_pieces.append(jnp.where(lo, _dot_tn(m0.astype(BF16), dy2b), _dot_tn(m1.astype(BF16), dy2b)) + w_x[:, sl] * g2)
                zero = jnp.zeros_like(dy2b)
                dm0 = _dot_nt(jnp.where(lo, dy2b, zero), x2)
                dm1 = _dot_nt(jnp.where(lo, zero, dy2b), x2)
                dcbm = dcbm + dm0 * d0 + dm1 * d1
                e0 = dm0 * m0
                e1v = dm1 * m1
                colbuf[:, 2 * j:2 * j + 1] = jnp.sum(e0, axis=1, keepdims=True) - jnp.sum(e0.T, axis=1, keepdims=True)
                colbuf[:, 2 * j + 1:2 * j + 2] = jnp.sum(e1v, axis=1, keepdims=True) - jnp.sum(e1v.T, axis=1, keepdims=True)
                yoff = lam_x[:, sl] * _dot(cb_, hprevb)
                gxw = g2 * xdw[:, sl]
                dacs_pieces.append(dy2 * yoff - gxw)
                dacsl_pieces.append(jnp.sum(gxw, axis=0, keepdims=True) + gam_x[:, sl] * jnp.sum(dhn * hprev, axis=0, keepdims=True))
                dyl = (dy2 * lam_x[:, sl]).astype(BF16)
                dc_g = dc_g + _dot_nt(dyl, hprevb)
                db_g = db_g + _dot_nt(xw[:, sl], dhnb)
                dh[j] = gam_x[:, sl] * dhn + _dot_tn(cb_, dyl)
            dcbb = dcbm.astype(BF16)
            dc_pieces.append(dc_g + _dot(dcbb, bb))
            db_pieces.append(db_g + _dot_tn(dcbb, cb_))

        dxd = jnp.concatenate(dx_pieces, axis=1)
        rowi = lax.broadcasted_iota(jnp.int32, (L, 1), 0)
        dacs_x = (jnp.concatenate(dacs_pieces, axis=1) + _dot_hi(colbuf[...], e1_ref[...])
                  + jnp.where(rowi == L - 1, jnp.concatenate(dacsl_pieces, axis=1), 0.0))
        upper = lax.broadcasted_iota(jnp.int32, (L, L), 0) <= lax.broadcasted_iota(jnp.int32, (L, L), 1)
        dadt_x = _dot_hi(upper.astype(F32), dacs_x)
        ddt_x = dxd * xs + dadt_x * a_x
        ddtr = _dot_nt_hi(ddt_x * _sigmoid(pre_x), e_ref[...])
        ddt_ref[...] = ddtr
        dalx = jnp.sum(dadt_x * dt_x, axis=0, keepdims=True) * a_x
        ddskx = jnp.sum(dy * xs, axis=0, keepdims=True)
        par_x = jnp.where(row8 == 1, dalx, 0.0) + jnp.where(row8 == 2, ddskx, 0.0)
        dpar_ref[...] += _dot_nt_hi(par_x, e_ref[...]) + jnp.where(row8 == 0, jnp.sum(ddtr, axis=0, keepdims=True), 0.0)

        dxs = dxd * dt_x + dsk_ref[...] * dy
        dact = jnp.concatenate([dxs] + db_pieces + dc_pieces, axis=1)
        du = dact * _dsilu(xc, sg)
        dcb_ref[...] += jnp.sum(du, axis=0, keepdims=True)
        taps = (sh3, sh2, sh1, u)
        dcw = jnp.zeros((8, CONV_C), F32)
        for k in range(4):
            dcw = dcw + jnp.where(row8 == k, jnp.sum(du * taps[k], axis=0, keepdims=True), 0.0)
        dcw_ref[...] += dcw
        zext = jnp.concatenate([du, duh[...]], axis=0)
        f1, f2, f3 = (pltpu.roll(zext, L + 8 - m, 0)[:L] for m in (1, 2, 3))
        dxbc_ref[...] = cw_ref[3:4, :] * du + cw_ref[2:3, :] * f1 + cw_ref[1:2, :] * f2 + cw_ref[0:1, :] * f3
        duh[...] = du[:8, :]

    return pl.pallas_call(
        body, name="ssd_bwd", grid=(NCH,),
        in_specs=[pl.BlockSpec((L, SSM_W), lambda i: (NCH - 1 - i, 1)), pl.BlockSpec((L, CONV_C), rev),
                  pl.BlockSpec((8, CONV_C), lambda i: (jnp.maximum((NCH - 1 - i) * (L // 8) - 1, 0), 0)),
                  pl.BlockSpec((L, SSM_W), rev), pl.BlockSpec((L, SSM_H), rev), pl.BlockSpec((L, SSM_W), rev),
                  pl.BlockSpec((1, 8, SSM_N, 128), lambda i: (NCH - 1 - i, 0, 0, 0)),
                  _const((4, CONV_C)), _const((1, CONV_C)), _const((1, SSM_H)), _const((1, SSM_W)), _const((1, SSM_W)), _const((1, SSM_W)),
                  _const((SSM_H, SSM_W)), _const((128, SSM_W))],
        out_specs=[pl.BlockSpec((L, CONV_C), rev), pl.BlockSpec((L, SSM_W), rev), pl.BlockSpec((L, SSM_H), rev),
                   _const((8, CONV_C)), _const((1, CONV_C)), _const((1, SSM_W)), _const((8, SSM_H))],
        out_shape=[jax.ShapeDtypeStruct((T, CONV_C), F32), jax.ShapeDtypeStruct((T, SSM_W), F32), jax.ShapeDtypeStruct((T, SSM_H), F32),
                   jax.ShapeDtypeStruct((8, CONV_C), F32), jax.ShapeDtypeStruct((1, CONV_C), F32), jax.ShapeDtypeStruct((1, SSM_W), F32),
                   jax.ShapeDtypeStruct((8, SSM_H), F32)],
        scratch_shapes=[pltpu.VMEM((8, SSM_N, 128), F32), pltpu.VMEM((8, CONV_C), F32), pltpu.VMEM((L, 128), F32)],
        compiler_params=_params(("arbitrary",)),
    )(dmix, xbc, xbc, z, dtr, y, hs, convw, convb, dtb, alx, dskx, ssmn, e, e1)


def _loss_head(x3, target):
    tm = 512

    def body(x_ref, t_ref, dy_ref, ss_ref):
        @pl.when(pl.program_id(0) == 0)
        def _():
            ss_ref[...] = jnp.zeros_like(ss_ref)

        err = x_ref[...] - t_ref[...]
        dy_ref[...] = err * (1.0 / D)
        ss_ref[...] += jnp.sum(jnp.sum(err * err, axis=1, keepdims=True), axis=0, keepdims=True)

    return pl.pallas_call(
        body, name="loss_head", grid=(T // tm,),
        in_specs=[_rows(tm, D), _rows(tm, D)],
        out_specs=[_rows(tm, D), _const((1, 128))],
        out_shape=[jax.ShapeDtypeStruct((T, D), F32), jax.ShapeDtypeStruct((1, 128), F32)],
        compiler_params=_params(("arbitrary",)),
    )(x3, target)


def _adam_math(w, g, m, v):
    m = ADAM_B1 * m + (1.0 - ADAM_B1) * g
    v = ADAM_B2 * v + (1.0 - ADAM_B2) * (g * g)
    m_hat = m / (1.0 - ADAM_B1 ** ADAM_STEP)
    v_hat = v / (1.0 - ADAM_B2 ** ADAM_STEP)
    delta = -ADAM_LR * (m_hat / (jnp.sqrt(v_hat) + ADAM_EPS) + ADAM_WD * w)
    return delta, m, v


def _adamw(w, m, v, parts, name):
    rows, cols = w.shape
    tr = rows if rows <= 512 else 256
    assert rows % tr == 0
    n_parts = len(parts)

    def body(*refs):
        w_ref, m_ref, v_ref = refs[:3]
        p_refs = refs[3:3 + n_parts]
        g_ref, d_ref, nm_ref, nv_ref = refs[3 + n_parts:]
        g = p_refs[0][...].astype(F32)
        for p in p_refs[1:]:
            g = g + p[...].astype(F32)
        delta, nm, nv = _adam_math(w_ref[...], g, m_ref[...], v_ref[...])
        g_ref[...] = g
        d_ref[...] = delta
        nm_ref[...] = nm
        nv_ref[...] = nv

    spec = pl.BlockSpec((tr, cols), lambda i: (i, 0))
    return pl.pallas_call(
        body, name=name, grid=(rows // tr,),
        in_specs=[spec] * (3 + n_parts), out_specs=[spec] * 4,
        out_shape=[jax.ShapeDtypeStruct((rows, cols), F32)] * 4,
        compiler_params=_params(("parallel",)),
    )(w, m, v, *parts)


def _add_partials(mine, recv, name):
    rows, cols = mine.shape
    tr = next(t for t in (512, 256, 128) if rows % t == 0)

    def body(a_ref, b_ref, s_ref, sb_ref):
        s = a_ref[...] + b_ref[...].astype(F32)
        s_ref[...] = s
        sb_ref[...] = s.astype(BF16)

    spec = pl.BlockSpec((tr, cols), lambda i: (i, 0))
    return pl.pallas_call(
        body, name=name, grid=(rows // tr,),
        in_specs=[spec, spec], out_specs=[spec, spec],
        out_shape=[jax.ShapeDtypeStruct((rows, cols), F32), jax.ShapeDtypeStruct((rows, cols), BF16)],
        compiler_params=_params(("parallel",)),
    )(mine, recv)


def _place():
    return lax.axis_index("x"), lax.axis_index("y"), lax.axis_index("c")


ANY = pl.BlockSpec(memory_space=pl.ANY)


def _all_gather(arrays):
    n = len(arrays)

    def body(*refs):
        ins, outs = refs[:n], refs[n:2 * n]
        send_sems, recv_sems, local_sems = refs[2 * n:]
        x, y, c = _place()
        me, sibling = (x, y, c), (x, y, 1 - c)
        chips = [(1 - x, y), (x, 1 - y), (1 - x, 1 - y)]

        def idx(px, py, pc):
            return 4 * px + 2 * py + pc

        def copy(a, k, block, to, src=None):
            dst = outs[a].at[idx(*block)]
            return pltpu.make_async_remote_copy(src_ref=dst if src is None else src, dst_ref=dst,
                                                send_sem=send_sems.at[7 * a + k], recv_sem=recv_sems.at[7 * a + k],
                                                device_id=to, device_id_type=MESH)

        mine = [pltpu.make_async_copy(ins[a], outs[a].at[idx(*me)], local_sems.at[a]) for a in range(n)]
        for cp in mine:
            cp.start()
        first = []
        for a in range(n):
            first.append(copy(a, 0, me, sibling, src=ins[a]))
            first += [copy(a, 1 + j, me, (*chip, c), src=ins[a]) for j, chip in enumerate(chips)]
        for cp in first:
            cp.start()
        passed = []
        for j, chip in enumerate(chips):
            for a in range(n):
                copy(a, 1 + j, (*chip, c), me).wait_recv()
                fwd = copy(a, 4 + j, (*chip, c), sibling)
                fwd.start()
                passed.append(fwd)
        for a in range(n):
            copy(a, 0, sibling, me).wait_recv()
            for j, chip in enumerate(chips):
                copy(a, 4 + j, (*chip, 1 - c), me).wait_recv()
        for cp in first + passed:
            cp.wait_send()
        for cp in mine:
            cp.wait()

    return pl.pallas_call(
        body, name="all_gather_weights",
        in_specs=[ANY] * n, out_specs=[ANY] * n,
        out_shape=[jax.ShapeDtypeStruct((N_DEV,) + a.shape, a.dtype) for a in arrays],
        scratch_shapes=[pltpu.SemaphoreType.DMA((7 * n,)), pltpu.SemaphoreType.DMA((7 * n,)), pltpu.SemaphoreType.DMA((n,))],
    )(*arrays)


def _swap_with_sibling(arrays):
    n = len(arrays)

    def body(*refs):
        ins, outs = refs[:n], refs[n:2 * n]
        send_sems, recv_sems = refs[2 * n:]
        x, y, c = _place()
        cps = [pltpu.make_async_remote_copy(src_ref=ins[a], dst_ref=outs[a], send_sem=send_sems.at[a], recv_sem=recv_sems.at[a],
                                            device_id=(x, y, 1 - c), device_id_type=MESH) for a in range(n)]
        for cp in cps:
            cp.start()
        for cp in cps:
            cp.wait()

    return pl.pallas_call(
        body, name="grad_swap_sibling",
        in_specs=[ANY] * n, out_specs=[ANY] * n,
        out_shape=[jax.ShapeDtypeStruct(a.shape, a.dtype) for a in arrays],
        scratch_shapes=[pltpu.SemaphoreType.DMA((n,)), pltpu.SemaphoreType.DMA((n,))],
    )(*arrays)


def _exchange_chips(arrays):
    n = len(arrays)

    def body(*refs):
        ins, outs = refs[:n], refs[n:2 * n]
        send_sems, recv_sems = refs[2 * n:]
        x, y, c = _place()
        chips = [(1 - x, y), (x, 1 - y), (1 - x, 1 - y)]
        my_chip = 2 * x + y
        cps = []
        for a in range(n):
            for j, (px, py) in enumerate(chips):
                cps.append(pltpu.make_async_remote_copy(src_ref=ins[a].at[2 * px + py], dst_ref=outs[a].at[my_chip],
                                                        send_sem=send_sems.at[3 * a + j], recv_sem=recv_sems.at[3 * a + j],
                                                        device_id=(px, py, c), device_id_type=MESH))
        for cp in cps:
            cp.start()
        for a in range(n):
            for j, (px, py) in enumerate(chips):
                pltpu.make_async_remote_copy(src_ref=ins[a].at[my_chip], dst_ref=outs[a].at[2 * px + py],
                                             send_sem=send_sems.at[3 * a + j], recv_sem=recv_sems.at[3 * a + j],
                                             device_id=(px, py, c), device_id_type=MESH).wait_recv()
        for cp in cps:
            cp.wait_send()

    return pl.pallas_call(
        body, name="grad_exchange_chips",
        in_specs=[ANY] * n, out_specs=[ANY] * n,
        out_shape=[jax.ShapeDtypeStruct(a.shape, a.dtype) for a in arrays],
        scratch_shapes=[pltpu.SemaphoreType.DMA((3 * n,)), pltpu.SemaphoreType.DMA((3 * n,))],
    )(*arrays)


def _all_reduce_small(v):
    rows = v.shape[0]

    def body(v_ref, out_ref, gath, send_sems, recv_sems):
        x, y, c = _place()
        me, sibling = (x, y, c), (x, y, 1 - c)
        chips = [(1 - x, y), (x, 1 - y), (1 - x, 1 - y)]

        def blk(px, py, pc):
            return gath.at[pl.ds((4 * px + 2 * py + pc) * rows, rows), :]

        def copy(k, block, to, src=None):
            return pltpu.make_async_remote_copy(src_ref=blk(*block) if src is None else src, dst_ref=blk(*block),
                                                send_sem=send_sems.at[k], recv_sem=recv_sems.at[k], device_id=to, device_id_type=MESH)

        gath[pl.ds((4 * x + 2 * y + c) * rows, rows), :] = v_ref[...]
        first = [copy(0, me, sibling, src=v_ref)] + [copy(1 + j, me, (*chip, c), src=v_ref) for j, chip in enumerate(chips)]
        for cp in first:
            cp.start()
        passed = [copy(4 + j, (*chip, c), sibling) for j, chip in enumerate(chips)]
        for j, chip in enumerate(chips):
            copy(1 + j, (*chip, c), me).wait_recv()
            passed[j].start()
        copy(0, sibling, me).wait_recv()
        for j, chip in enumerate(chips):
            copy(4 + j, (*chip, 1 - c), me).wait_recv()
        for cp in first + passed:
            cp.wait_send()
        acc = gath[0:rows, :]
        for d in range(1, N_DEV):
            acc = acc + gath[d * rows:(d + 1) * rows, :]
        out_ref[...] = acc

    vm = pl.BlockSpec(memory_space=pltpu.VMEM)
    return pl.pallas_call(
        body, name="all_reduce_small",
        in_specs=[vm], out_specs=vm,
        out_shape=jax.ShapeDtypeStruct(v.shape, F32),
        scratch_shapes=[pltpu.VMEM((N_DEV * rows, 128), F32), pltpu.SemaphoreType.DMA((7,)), pltpu.SemaphoreType.DMA((7,))],
    )(v)


def _rope_tables(positions):
    inv_freq = ROPE_THETA ** (-jnp.arange(0, HD, 2, dtype=F32) / HD)
    ang = positions.reshape(T).astype(F32)[:, None] * inv_freq
    ang = jnp.concatenate([ang, ang, ang, ang], axis=-1)
    lo_half = (jnp.arange(128) % HD) < (HD // 2)
    return jnp.cos(ang), jnp.where(lo_half, -jnp.sin(ang), jnp.sin(ang))


def _selectors():
    lane = jnp.arange(QW)
    e = (lane[None, :] // HD == jnp.arange(SSM_H)[:, None]).astype(F32)
    e1 = ((lane[None, :] == HD * jnp.arange(128)[:, None]) & (jnp.arange(128)[:, None] < SSM_H)).astype(F32)
    src = jnp.arange(KVW)
    ex = ((lane[None, :] // (HD * (NQ // NKV)) == src[:, None] // HD) & (lane[None, :] % HD == src[:, None] % HD)).astype(F32)
    return e, e1, ex


def _local_step(x, positions, target, w):
    cos, sin_s = _rope_tables(positions)
    e, e1, exf = _selectors()
    bias = _attn_bias()
    alx = jnp.repeat(w["a_log"], HD, axis=1)
    dskx = jnp.repeat(w["d_skip"], HD, axis=1)

    x1, n1, a1, b1, hm1, h1 = _ffn_fwd(x, w["ffn1_pre_norm"], w["ffn1_w_gate"], w["ffn1_w_up"], w["ffn1_w_down"], w["ffn1_post_norm"], "ffn1_fwd")
    n2, q, kx, vx, xbc, z, dtr = _inproj_fwd(x1, w["mix_pre_norm"], w["w_in"], cos, sin_s, exf.astype(BF16))
    attn, lse = _attn_fwd(q, kx, vx, bias)
    yn, y, hs = _ssd_fwd(xbc, z, dtr, w["conv_w"], w["conv_b"], w["dt_bias"], alx, dskx, w["ssm_norm"], e)
    x2, h2 = _outproj_fwd(x1, attn, yn, w["w_out"], w["mix_post_norm"])
    x3, n3, a3, b3, hm3, h3 = _ffn_fwd(x2, w["ffn2_pre_norm"], w["ffn2_w_gate"], w["ffn2_w_up"], w["ffn2_w_down"], w["ffn2_post_norm"], "ffn2_fwd")
    dx3, ss = _loss_head(x3, target)

    g = {}
    dx2, da3, db3, dh3, g["ffn2_pre_norm"], g["ffn2_post_norm"] = _ffn_bwd(
        dx3, x2, a3, b3, h3, w["ffn2_pre_norm"], w["ffn2_post_norm"], w["ffn2_w_gate"], w["ffn2_w_up"], w["ffn2_w_down"], "ffn2_bwd")
    g["ffn2_w_down"] = _matmul_tn(hm3, dh3, "ffn2_dwd")
    g["ffn2_w_gate"] = _matmul_tn(n3, da3, "ffn2_dwg")
    g["ffn2_w_up"] = _matmul_tn(n3, db3, "ffn2_dwu")

    dh2, dmix, g["mix_post_norm"] = _outproj_bwd(dx2, h2, w["mix_post_norm"], w["w_out"])
    g["w_out"] = jnp.concatenate([_matmul_tn(attn, dh2, "dwout_attn"), _matmul_tn(yn, dh2, "dwout_ssm")], axis=0)
    dq, dkx, dvx = _attn_bwd(q, kx, vx, attn, dmix, lse, bias)
    dxbc, dz, ddt, dcw, g["conv_b"], g["ssm_norm"], dpar = _ssd_bwd(
        dmix, xbc, z, dtr, y, hs, w["conv_w"], w["conv_b"], w["dt_bias"], alx, dskx, w["ssm_norm"], e, e1)
    g["conv_w"] = dcw[0:4]
    g["dt_bias"], g["a_log"], g["d_skip"] = dpar[0:1], dpar[1:2], dpar[2:3]
    dx1, dproj, g["mix_pre_norm"] = _inproj_bwd(dx2, dq, dkx, dvx, dxbc, dz, ddt, x1, w["mix_pre_norm"], w["w_in"], cos, sin_s, exf)
    g["w_in"] = _matmul_tn(n2, dproj, "dwin")

    dx0, da1, db1, dh1, g["ffn1_pre_norm"], g["ffn1_post_norm"] = _ffn_bwd(
        dx1, x, a1, b1, h1, w["ffn1_pre_norm"], w["ffn1_post_norm"], w["ffn1_w_gate"], w["ffn1_w_up"], w["ffn1_w_down"], "ffn1_bwd")
    g["ffn1_w_down"] = _matmul_tn(hm1, dh1, "ffn1_dwd")
    g["ffn1_w_gate"] = _matmul_tn(n1, da1, "ffn1_dwg")
    g["ffn1_w_up"] = _matmul_tn(n1, db1, "ffn1_dwu")
    return ss, dx0, g


WEIGHTS = ['ffn1_pre_norm', 'ffn1_w_gate', 'ffn1_w_up', 'ffn1_w_down', 'ffn1_post_norm', 'mix_pre_norm', 'w_in', 'conv_w', 'conv_b',
           'dt_bias', 'a_log', 'd_skip', 'ssm_norm', 'w_out', 'mix_post_norm', 'ffn2_pre_norm', 'ffn2_w_gate', 'ffn2_w_up',
           'ffn2_w_down', 'ffn2_post_norm']
COL_SHARDED = ['ffn1_w_gate', 'ffn1_w_up', 'ffn2_w_gate', 'ffn2_w_up', 'w_in']
ROW_SHARDED = ['ffn1_w_down', 'ffn2_w_down', 'w_out']
BIG = COL_SHARDED + ROW_SHARDED
SMALL = ['ffn1_pre_norm', 'ffn1_post_norm', 'mix_pre_norm', 'conv_b', 'dt_bias', 'a_log', 'd_skip', 'ssm_norm', 'mix_post_norm',
         'ffn2_pre_norm', 'ffn2_post_norm']


def _gather_whole_weights(w):
    ggu = jnp.stack([w[n].astype(BF16) for n in COL_SHARDED[:4]])
    gdn = jnp.stack([w[n].astype(BF16) for n in ROW_SHARDED[:2]])
    gathered = _all_gather([ggu, gdn, w["w_in"].astype(BF16), w["w_out"].astype(BF16), w["conv_w"]])
    whole = {n: w[n] for n in SMALL}

    def cols(a):
        return jnp.transpose(a, (1, 0, 2)).reshape(a.shape[1], -1)

    for i, n in enumerate(COL_SHARDED[:4]):
        whole[n] = cols(gathered[0][:, i])
    for i, n in enumerate(ROW_SHARDED[:2]):
        whole[n] = gathered[1][:, i].reshape(-1, D)
    whole["w_in"] = cols(gathered[2])
    whole["w_out"] = gathered[3].reshape(-1, D)
    whole["conv_w"] = cols(gathered[4])
    return whole


def _shard_major(name, g):
    if name in COL_SHARDED:
        return jnp.transpose(g.reshape(g.shape[0], N_DEV, -1), (1, 0, 2))
    return g.reshape(N_DEV, -1, g.shape[1])


def kernel(x, positions, ffn1_pre_norm, ffn1_w_gate, ffn1_w_up, ffn1_w_down, ffn1_post_norm, mix_pre_norm, w_in, conv_w, conv_b, dt_bias, a_log, d_skip, ssm_norm, w_out, mix_post_norm, ffn2_pre_norm, ffn2_w_gate, ffn2_w_up, ffn2_w_down, ffn2_post_norm, loss_target, m_ffn1_pre_norm, m_ffn1_w_gate, m_ffn1_w_up, m_ffn1_w_down, m_ffn1_post_norm, m_mix_pre_norm, m_w_in, m_conv_w, m_conv_b, m_dt_bias, m_a_log, m_d_skip, m_ssm_norm, m_w_out, m_mix_post_norm, m_ffn2_pre_norm, m_ffn2_w_gate, m_ffn2_w_up, m_ffn2_w_down, m_ffn2_post_norm, v_ffn1_pre_norm, v_ffn1_w_gate, v_ffn1_w_up, v_ffn1_w_down, v_ffn1_post_norm, v_mix_pre_norm, v_w_in, v_conv_w, v_conv_b, v_dt_bias, v_a_log, v_d_skip, v_ssm_norm, v_w_out, v_mix_post_norm, v_ffn2_pre_norm, v_ffn2_w_gate, v_ffn2_w_up, v_ffn2_w_down, v_ffn2_post_norm):
    given = dict(locals())
    drop = lambda n, a: a if n in SMALL else a[0]
    w = {n: drop(n, given[n]) for n in WEIGHTS}
    m = {n: drop(n, given["m_" + n]) for n in WEIGHTS}
    v = {n: drop(n, given["v_" + n]) for n in WEIGHTS}
    cx, cy, cc = _place()
    my_chip = 2 * cx + cy
    others = [2 * (1 - cx) + cy, 2 * cx + (1 - cy), 2 * (1 - cx) + (1 - cy)]

    whole = _gather_whole_weights(w)
    ss, grad_x, g = _local_step(x[0], positions, loss_target[0], whole)
    loss = lax.psum(0.5 * ss[0, 0] / D, ("x", "y", "c"))

    sm = {n: _shard_major(n, g[n]) for n in BIG}
    by_core = {n: a.reshape((4, 2) + a.shape[1:]) for n, a in sm.items()}
    to_sibling = [lax.dynamic_index_in_dim(by_core[n], 1 - cc, axis=1, keepdims=False).astype(BF16) for n in BIG]
    from_sibling = _swap_with_sibling(to_sibling)
    chip_sum, chip_sum_b = {}, {}
    for n, recv in zip(BIG, from_sibling):
        mine = lax.dynamic_index_in_dim(by_core[n], cc, axis=1, keepdims=False)
        flat = (-1, mine.shape[-1])
        s, sb = _add_partials(mine.reshape(flat), recv.reshape(flat), "chip_sum_" + n)
        chip_sum[n], chip_sum_b[n] = s.reshape(mine.shape), sb.reshape(mine.shape)
    from_chips = _exchange_chips([chip_sum_b[n] for n in BIG])

    out_g, out_d, out_m, out_v = {}, {}, {}, {}
    for n, recv in zip(BIG, from_chips):
        parts = [lax.dynamic_index_in_dim(chip_sum[n], my_chip, axis=0, keepdims=False)]
        parts += [lax.dynamic_index_in_dim(recv, o, axis=0, keepdims=False) for o in others]
        out_g[n], out_d[n], out_m[n], out_v[n] = _adamw(w[n], m[n], v[n], parts, "adamw_" + n)

    small_g = jnp.concatenate([g[n].reshape(-1) for n in SMALL] + [g["conv_w"].reshape(-1)])
    n_small = small_g.shape[0]
    rows = -(-n_small // 128 // 8) * 8
    packed = jnp.pad(small_g, (0, rows * 128 - n_small)).reshape(rows, 128)
    total = _all_reduce_small(packed).reshape(-1)
    sizes = [w[n].size for n in SMALL]
    offs = [0]
    for s_ in sizes:
        offs.append(offs[-1] + s_)
    gs = {n: total[offs[i]:offs[i + 1]].reshape(w[n].shape) for i, n in enumerate(SMALL)}
    gcw = total[offs[-1]:offs[-1] + 4 * CONV_C].reshape(4, CONV_C)
    gs["conv_w"] = lax.dynamic_slice_in_dim(gcw, (4 * cx + 2 * cy + cc) * (CONV_C // N_DEV), CONV_C // N_DEV, axis=1)
    names = SMALL + ["conv_w"]

    def pack(d):
        flat = jnp.concatenate([d[n].reshape(-1) for n in names])
        return jnp.pad(flat, (0, rows * 128 - flat.shape[0])).reshape(rows, 128)

    pg, pd, pm, pv = _adamw(pack(w), pack(m), pack(v), [pack(gs)], "adamw_small")
    o2 = [0]
    for n in names:
        o2.append(o2[-1] + w[n].size)
    for i, n in enumerate(names):
        for dst, src in ((out_g, pg), (out_d, pd), (out_m, pm), (out_v, pv)):
            dst[n] = src.reshape(-1)[o2[i]:o2[i + 1]].reshape(w[n].shape)

    outs = [loss, grad_x[None]]
    for d in (out_g, out_d, out_m, out_v):
        outs += [d[n] if n in SMALL else d[n][None] for n in WEIGHTS]
    return tuple(outs)
```

```python
import functools
import math

import jax
import jax.numpy as jnp
from jax import lax
from jax.experimental import pallas as pl
from jax.experimental.pallas import tpu as pltpu

F32 = jnp.float32
BF16 = jnp.bfloat16
HI = lax.Precision.HIGHEST
MESH = pl.DeviceIdType.MESH

N_DEV = 8
T = 2048
D = 1024
FF = 2816
FSR = FF // N_DEV
FSH = 384
FFP = N_DEV * FSH
HD = 64
NQ = 16
NKV = 4
QW = NQ * HD
KVW = NKV * HD
SSM_W = 1024
SSM_H = 16
SSM_N = 128
CONV_C = SSM_W + 2 * 2 * SSM_N
IN_COLS = 4112
L = 128
NCH = T // L
AB = 256
NAB = T // AB
EPS = 1e-6
NEG = -1e30
ROPE_THETA = 10000.0
DILATIONS = ((128, 1), (512, 4), (2048, 16))

ADAM_LR = 0.001
ADAM_B1 = 0.9
ADAM_B2 = 0.999
ADAM_EPS = 1e-08
ADAM_WD = 0.01
ADAM_STEP = 10

VMEM_LIMIT = 58 * 1024 * 1024


def _params(sem, vmem=VMEM_LIMIT):
    return pltpu.CompilerParams(dimension_semantics=sem, vmem_limit_bytes=vmem)


def _dot(a, b):
    return jnp.dot(a, b, preferred_element_type=F32)


def _dot_nt(a, b):
    return lax.dot_general(a, b, (((1,), (1,)), ((), ())), preferred_element_type=F32)


def _dot_tn(a, b):
    return lax.dot_general(a, b, (((0,), (0,)), ((), ())), preferred_element_type=F32)


def _dot_hi(a, b):
    return jnp.dot(a, b, preferred_element_type=F32, precision=HI)


def _dot_nt_hi(a, b):
    return lax.dot_general(a, b, (((1,), (1,)), ((), ())), preferred_element_type=F32, precision=HI)


def _rs(x):
    return lax.rsqrt(jnp.mean(x * x, axis=-1, keepdims=True) + EPS)


def _sigmoid(x):
    return jax.nn.sigmoid(x)


def _dsilu(x, s):
    return s * (1.0 + x * (1.0 - s))


def _resident(shape):
    nd = len(shape)
    return pl.BlockSpec(shape, lambda *_: (0,) * nd, pipeline_mode=pl.Buffered(1))


def _const(shape):
    nd = len(shape)
    return pl.BlockSpec(shape, lambda *_: (0,) * nd)


def _rows(tm, cols):
    return pl.BlockSpec((tm, cols), lambda i: (i, 0))


def _ffn_fwd(x, gpre, wg, wu, wd, gpost, name):
    tm = 256

    def body(x_ref, gpre_ref, wg_ref, wu_ref, wd_ref, gpost_ref, xo_ref, n_ref, a_ref, b_ref, hm_ref, h_ref):
        xv = x_ref[...]
        n = (xv * _rs(xv) * gpre_ref[...]).astype(BF16)
        a = _dot(n, wg_ref[...])
        b = _dot(n, wu_ref[...])
        hm = (a * _sigmoid(a) * b).astype(BF16)
        h = _dot(hm, wd_ref[...])
        xo_ref[...] = xv + 0.5 * (h * _rs(h) * gpost_ref[...])
        n_ref[...] = n
        a_ref[...] = a.astype(BF16)
        b_ref[...] = b.astype(BF16)
        hm_ref[...] = hm
        h_ref[...] = h

    return pl.pallas_call(
        body, name=name, grid=(T // tm,),
        in_specs=[_rows(tm, D), _const((1, D)), _resident((D, FFP)), _resident((D, FFP)), _resident((FFP, D)), _const((1, D))],
        out_specs=[_rows(tm, D), _rows(tm, D), _rows(tm, FFP), _rows(tm, FFP), _rows(tm, FFP), _rows(tm, D)],
        out_shape=[jax.ShapeDtypeStruct((T, D), F32), jax.ShapeDtypeStruct((T, D), BF16), jax.ShapeDtypeStruct((T, FFP), BF16),
                   jax.ShapeDtypeStruct((T, FFP), BF16), jax.ShapeDtypeStruct((T, FFP), BF16), jax.ShapeDtypeStruct((T, D), F32)],
        compiler_params=_params(("parallel",)),
    )(x, gpre, wg, wu, wd, gpost)


def _ffn_bwd(dxo, x, a, b, h, gpre, gpost, wg, wu, wd, name):
    tm = 256

    def body(dxo_ref, x_ref, a_ref, b_ref, h_ref, gpre_ref, gpost_ref, wg_ref, wu_ref, wd_ref,
             dx_ref, da_ref, db_ref, dh_ref, dgpre_ref, dgpost_ref):
        @pl.when(pl.program_id(0) == 0)
        def _():
            dgpre_ref[...] = jnp.zeros_like(dgpre_ref)
            dgpost_ref[...] = jnp.zeros_like(dgpost_ref)

        dy = dxo_ref[...]
        h = h_ref[...]
        hn = h * _rs(h)
        r2 = _rs(h)
        dgpost_ref[...] += jnp.sum(0.5 * dy * hn, axis=0, keepdims=True)
        gdy = 0.5 * dy * gpost_ref[...]
        dh = r2 * (gdy - hn * jnp.mean(gdy * hn, axis=-1, keepdims=True))
        dhb = dh.astype(BF16)
        dh_ref[...] = dhb
        dhm = _dot_nt(dhb, wd_ref[...])
        av = a_ref[...].astype(F32)
        bv = b_ref[...].astype(F32)
        sg = _sigmoid(av)
        db = (dhm * (av * sg)).astype(BF16)
        da = (dhm * bv * _dsilu(av, sg)).astype(BF16)
        da_ref[...] = da
        db_ref[...] = db
        dn = _dot_nt(da, wg_ref[...]) + _dot_nt(db, wu_ref[...])
        xv = x_ref[...]
        r = _rs(xv)
        xn = xv * r
        dgpre_ref[...] += jnp.sum(dn * xn, axis=0, keepdims=True)
        gdn = dn * gpre_ref[...]
        dx_ref[...] = dy + r * (gdn - xn * jnp.mean(gdn * xn, axis=-1, keepdims=True))

    return pl.pallas_call(
        body, name=name, grid=(T // tm,),
        in_specs=[_rows(tm, D), _rows(tm, D), _rows(tm, FFP), _rows(tm, FFP), _rows(tm, D), _const((1, D)), _const((1, D)),
                  _resident((D, FFP)), _resident((D, FFP)), _resident((FFP, D))],
        out_specs=[_rows(tm, D), _rows(tm, FFP), _rows(tm, FFP), _rows(tm, D), _const((1, D)), _const((1, D))],
        out_shape=[jax.ShapeDtypeStruct((T, D), F32), jax.ShapeDtypeStruct((T, FFP), BF16), jax.ShapeDtypeStruct((T, FFP), BF16),
                   jax.ShapeDtypeStruct((T, D), BF16), jax.ShapeDtypeStruct((1, D), F32), jax.ShapeDtypeStruct((1, D), F32)],
        compiler_params=_params(("arbitrary",)),
    )(dxo, x, a, b, h, gpre, gpost, wg, wu, wd)


def _matmul_tn(a, b, name, tm=256):
    k, m = a.shape
    n = b.shape[1]
    assert m % tm == 0

    def body(a_ref, b_ref, o_ref, ob_ref):
        r = _dot_tn(a_ref[...], b_ref[...])
        o_ref[...] = r
        ob_ref[...] = r.astype(BF16)

    spec = pl.BlockSpec((tm, n), lambda i: (i, 0))
    return pl.pallas_call(
        body, name=name, grid=(m // tm,),
        in_specs=[pl.BlockSpec((k, tm), lambda i: (0, i)), _resident((k, n))],
        out_specs=[spec, spec],
        out_shape=[jax.ShapeDtypeStruct((m, n), F32), jax.ShapeDtypeStruct((m, n), BF16)],
        compiler_params=_params(("parallel",)),
    )(a, b)


def _matmul_tn_cols(a, b, name, tm=256):
    k, m = a.shape
    s = b.shape[1] // N_DEV
    assert m % tm == 0 and s % 128 == 0

    def body(a_ref, b_ref, o_ref, ob_ref):
        r = _dot_tn(a_ref[...], b_ref[...])
        for d in range(N_DEV):
            o_ref[d] = r[:, s * d:s * (d + 1)]
            ob_ref[d] = r[:, s * d:s * (d + 1)].astype(BF16)

    spec = pl.BlockSpec((N_DEV, tm, s), lambda i: (0, i, 0))
    return pl.pallas_call(
        body, name=name, grid=(m // tm,),
        in_specs=[pl.BlockSpec((k, tm), lambda i: (0, i)), _resident((k, N_DEV * s))],
        out_specs=[spec, spec],
        out_shape=[jax.ShapeDtypeStruct((N_DEV, m, s), F32), jax.ShapeDtypeStruct((N_DEV, m, s), BF16)],
        compiler_params=_params(("parallel",)),
    )(a, b)


def _dwout(attn, yn, dh2):
    rs = (QW + SSM_W) // N_DEV
    half = N_DEV // 2

    def body(at_ref, yn_ref, dh_ref, o_ref, ob_ref):
        i = pl.program_id(0)

        def put(r):
            o_ref[0] = r
            ob_ref[0] = r.astype(BF16)

        @pl.when(i < half)
        def _():
            put(_dot_tn(at_ref[...], dh_ref[...]))

        @pl.when(i >= half)
        def _():
            put(_dot_tn(yn_ref[...], dh_ref[...]))

    spec = pl.BlockSpec((1, rs, D), lambda i: (i, 0, 0))
    return pl.pallas_call(
        body, name="dwout", grid=(N_DEV,),
        in_specs=[pl.BlockSpec((T, rs), lambda i: (0, jnp.minimum(i, half - 1))),
                  pl.BlockSpec((T, rs), lambda i: (0, jnp.maximum(i - half, 0))), _resident((T, D))],
        out_specs=[spec, spec],
        out_shape=[jax.ShapeDtypeStruct((N_DEV, rs, D), F32), jax.ShapeDtypeStruct((N_DEV, rs, D), BF16)],
        compiler_params=_params(("parallel",)),
    )(attn, yn, dh2)


def _rope_swap(t, lo_half):
    return jnp.where(lo_half, pltpu.roll(t, 96, 1), pltpu.roll(t, 32, 1))


def _inproj_fwd(x1, gpre, win, cos, sin_s, ex):
    tm = 256

    def body(x_ref, g_ref, w_ref, cos_ref, sin_ref, ex_ref, n_ref, q_ref, kx_ref, vx_ref, xbc_ref, z_ref, dt_ref):
        xv = x_ref[...]
        n = (xv * _rs(xv) * g_ref[...]).astype(BF16)
        n_ref[...] = n
        proj = _dot(n, w_ref[...])
        cs = cos_ref[...]
        sn = sin_ref[...]
        lo_half = (lax.broadcasted_iota(jnp.int32, (1, 128), 1) % HD) < (HD // 2)

        def rope(t):
            return t * cs + _rope_swap(t, lo_half) * sn

        for j in range(QW // 128):
            t = proj[:, 128 * j:128 * j + 128]
            q_ref[:, 128 * j:128 * j + 128] = (rope(t) * (HD ** -0.5)).astype(BF16)
        k = jnp.concatenate([rope(proj[:, QW + 128 * j:QW + 128 * j + 128]) for j in range(KVW // 128)], axis=1)
        v = proj[:, QW + KVW:QW + 2 * KVW]
        kx_ref[...] = _dot(k.astype(BF16), ex_ref[...]).astype(BF16)
        vx_ref[...] = _dot(v.astype(BF16), ex_ref[...]).astype(BF16)
        c0 = QW + 2 * KVW
        xbc_ref[...] = proj[:, c0:c0 + CONV_C]
        z_ref[...] = proj[:, c0 + CONV_C:c0 + CONV_C + SSM_W]
        dt_ref[...] = proj[:, c0 + CONV_C + SSM_W:IN_COLS]

    return pl.pallas_call(
        body, name="inproj_fwd", grid=(T // tm,),
        in_specs=[_rows(tm, D), _const((1, D)), _resident((D, IN_COLS)), _rows(tm, 128), _rows(tm, 128), _const((KVW, QW))],
        out_specs=[_rows(tm, D), _rows(tm, QW), _rows(tm, QW), _rows(tm, QW), _rows(tm, CONV_C), _rows(tm, SSM_W), _rows(tm, SSM_H)],
        out_shape=[jax.ShapeDtypeStruct((T, D), BF16), jax.ShapeDtypeStruct((T, QW), BF16), jax.ShapeDtypeStruct((T, QW), BF16),
                   jax.ShapeDtypeStruct((T, QW), BF16), jax.ShapeDtypeStruct((T, CONV_C), F32), jax.ShapeDtypeStruct((T, SSM_W), F32),
                   jax.ShapeDtypeStruct((T, SSM_H), F32)],
        compiler_params=_params(("parallel",)),
    )(x1, gpre, win, cos, sin_s, ex)


def _inproj_bwd(dres, dq, dkx, dvx, dxbc, dz, ddt, x1, gpre, win, cos, sin_s, exf):
    tm = 256

    def body(dres_ref, dq_ref, dkx_ref, dvx_ref, dxbc_ref, dz_ref, ddt_ref, x_ref, g_ref, w_ref, cos_ref, sin_ref, ex_ref,
             dx_ref, dp_ref, dg_ref):
        @pl.when(pl.program_id(0) == 0)
        def _():
            dg_ref[...] = jnp.zeros_like(dg_ref)

        cs = cos_ref[...]
        sn = sin_ref[...]
        lo_half = (lax.broadcasted_iota(jnp.int32, (1, 128), 1) % HD) < (HD // 2)

        def rope_t(t):
            return t * cs - _rope_swap(t, lo_half) * sn

        for j in range(QW // 128):
            dp_ref[:, 128 * j:128 * j + 128] = rope_t(dq_ref[:, 128 * j:128 * j + 128] * (HD ** -0.5)).astype(BF16)
        dk = _dot_nt_hi(dkx_ref[...], ex_ref[...])
        dv = _dot_nt_hi(dvx_ref[...], ex_ref[...])
        for j in range(KVW // 128):
            dp_ref[:, QW + 128 * j:QW + 128 * j + 128] = rope_t(dk[:, 128 * j:128 * j + 128]).astype(BF16)
        dp_ref[:, QW + KVW:QW + 2 * KVW] = dv.astype(BF16)
        c0 = QW + 2 * KVW
        dp_ref[:, c0:c0 + CONV_C] = dxbc_ref[...].astype(BF16)
        dp_ref[:, c0 + CONV_C:c0 + CONV_C + SSM_W] = dz_ref[...].astype(BF16)
        dp_ref[:, c0 + CONV_C + SSM_W:IN_COLS] = ddt_ref[...].astype(BF16)
        dn = _dot_nt(dp_ref[...], w_ref[...])
        xv = x_ref[...]
        r = _rs(xv)
        xn = xv * r
        dg_ref[...] += jnp.sum(dn * xn, axis=0, keepdims=True)
        gdn = dn * g_ref[...]
        dx_ref[...] = dres_ref[...] + r * (gdn - xn * jnp.mean(gdn * xn, axis=-1, keepdims=True))

    return pl.pallas_call(
        body, name="inproj_bwd", grid=(T // tm,),
        in_specs=[_rows(tm, D), _rows(tm, QW), _rows(tm, QW), _rows(tm, QW), _rows(tm, CONV_C), _rows(tm, SSM_W), _rows(tm, SSM_H),
                  _rows(tm, D), _const((1, D)), _resident((D, IN_COLS)), _rows(tm, 128), _rows(tm, 128), _const((KVW, QW))],
        out_specs=[_rows(tm, D), _rows(tm, IN_COLS), _const((1, D))],
        out_shape=[jax.ShapeDtypeStruct((T, D), F32), jax.ShapeDtypeStruct((T, IN_COLS), BF16), jax.ShapeDtypeStruct((1, D), F32)],
        compiler_params=_params(("arbitrary",)),
    )(dres, dq, dkx, dvx, dxbc, dz, ddt, x1, gpre, win, cos, sin_s, exf)


def _outproj_fwd(x1, attn, yn, wout, gpost):
    tm = 256

    def body(x_ref, at_ref, yn_ref, w_ref, g_ref, xo_ref, h_ref):
        h = _dot(at_ref[...], w_ref[0:QW, :]) + _dot(yn_ref[...], w_ref[QW:QW + SSM_W, :])
        h_ref[...] = h
        xo_ref[...] = x_ref[...] + h * _rs(h) * g_ref[...]

    return pl.pallas_call(
        body, name="outproj_fwd", grid=(T // tm,),
        in_specs=[_rows(tm, D), _rows(tm, QW), _rows(tm, SSM_W), _resident((QW + SSM_W, D)), _const((1, D))],
        out_specs=[_rows(tm, D), _rows(tm, D)],
        out_shape=[jax.ShapeDtypeStruct((T, D), F32), jax.ShapeDtypeStruct((T, D), F32)],
        compiler_params=_params(("parallel",)),
    )(x1, attn, yn, wout, gpost)


def _outproj_bwd(dx2, h2, gpost, wout):
    tm = 256

    def body(dy_ref, h_ref, g_ref, w_ref, dh_ref, dm_ref, dg_ref):
        @pl.when(pl.program_id(0) == 0)
        def _():
            dg_ref[...] = jnp.zeros_like(dg_ref)

        dy = dy_ref[...]
        h = h_ref[...]
        r = _rs(h)
        hn = h * r
        dg_ref[...] += jnp.sum(dy * hn, axis=0, keepdims=True)
        gdy = dy * g_ref[...]
        dh = (r * (gdy - hn * jnp.mean(gdy * hn, axis=-1, keepdims=True))).astype(BF16)
        dh_ref[...] = dh
        dm_ref[...] = _dot_nt(dh, w_ref[...])

    return pl.pallas_call(
        body, name="outproj_bwd", grid=(T // tm,),
        in_specs=[_rows(tm, D), _rows(tm, D), _const((1, D)), _resident((QW + SSM_W, D))],
        out_specs=[_rows(tm, D), _rows(tm, QW + SSM_W), _const((1, D))],
        out_shape=[jax.ShapeDtypeStruct((T, D), BF16), jax.ShapeDtypeStruct((T, QW + SSM_W), F32), jax.ShapeDtypeStruct((1, D), F32)],
        compiler_params=_params(("arbitrary",)),
    )(dx2, h2, gpost, wout)


def _attn_bias():
    d = jnp.arange(AB)[:, None] - jnp.arange(T)[None, :] + (T - AB)
    cnt = jnp.zeros(d.shape, F32)
    for window, dil in DILATIONS:
        cnt = cnt + ((d >= 0) & (d % dil == 0) & (d <= window)).astype(F32)
    return jnp.where(cnt > 0, jnp.log(jnp.maximum(cnt, 1.0)), NEG)


G_PER = NQ // NKV
WK = G_PER * HD


def _attn_fwd(q, kx, vx, bias):
    def body(q_ref, kx_ref, vx_ref, bias_ref, o_ref, lse_ref):
        lane = lax.broadcasted_iota(jnp.int32, (1, WK), 1)
        lse_ref[...] = jnp.zeros_like(lse_ref)
        for i in range(NAB):
            n = (i + 1) * AB
            rows = slice(i * AB, n)
            qi = q_ref[rows, :]
            kxi = kx_ref[0:n, :]
            vxi = vx_ref[0:n, :]
            bb = bias_ref[:, (NAB - 1 - i) * AB:]
            o_acc = jnp.zeros((AB, WK), F32)
            for g in range(G_PER):
                mg = (lane // HD) == g
                s = _dot_nt(jnp.where(mg, qi, jnp.zeros_like(qi)), kxi) + bb
                m = jnp.max(s, axis=1, keepdims=True)
                p = jnp.exp(s - m)
                l = jnp.sum(p, axis=1, keepdims=True)
                o_acc = jnp.where(mg, _dot(p.astype(BF16), vxi) / l, o_acc)
                lse_ref[rows, g:g + 1] = m + jnp.log(l)
            o_ref[rows, :] = o_acc.astype(BF16)

    col = lambda kv: (0, kv)
    return pl.pallas_call(
        body, name="attn_fwd", grid=(NKV,),
        in_specs=[pl.BlockSpec((T, WK), col), pl.BlockSpec((T, WK), col), pl.BlockSpec((T, WK), col), _const((AB, T))],
        out_specs=[pl.BlockSpec((T, WK), col), pl.BlockSpec((T, 128), col)],
        out_shape=[jax.ShapeDtypeStruct((T, QW), BF16), jax.ShapeDtypeStruct((T, NKV * 128), F32)],
        compiler_params=_params(("parallel",)),
    )(q, kx, vx, bias)


def _attn_bwd(q, kx, vx, o, dmix, lse, bias):
    def body(q_ref, kx_ref, vx_ref, o_ref, do_ref, lse_ref, bias_ref, dq_ref, dkx_ref, dvx_ref):
        lane = lax.broadcasted_iota(jnp.int32, (1, WK), 1)
        dkx_ref[...] = jnp.zeros_like(dkx_ref)
        dvx_ref[...] = jnp.zeros_like(dvx_ref)
        for i in range(NAB):
            n = (i + 1) * AB
            rows = slice(i * AB, n)
            qi = q_ref[rows, :]
            dof = do_ref[rows, :]
            doi = dof.astype(BF16)
            prod = dof * o_ref[rows, :].astype(F32)
            kxi = kx_ref[0:n, :]
            vxi = vx_ref[0:n, :]
            bb = bias_ref[:, (NAB - 1 - i) * AB:]
            dq_acc = jnp.zeros((AB, WK), F32)
            for g in range(G_PER):
                mg = (lane // HD) == g
                qm = jnp.where(mg, qi, jnp.zeros_like(qi))
                dom = jnp.where(mg, doi, jnp.zeros_like(doi))
                delta = jnp.sum(jnp.where(mg, prod, 0.0), axis=1, keepdims=True)
                p = jnp.exp(_dot_nt(qm, kxi) + bb - lse_ref[rows, g:g + 1])
                ds = (p * (_dot_nt(dom, vxi) - delta)).astype(BF16)
                dvx_ref[0:n, :] += _dot_tn(p.astype(BF16), dom)
                dkx_ref[0:n, :] += _dot_tn(ds, qm)
                dq_acc = jnp.where(mg, _dot(ds, kxi), dq_acc)
            dq_ref[rows, :] = dq_acc

    col = lambda kv: (0, kv)
    return pl.pallas_call(
        body, name="attn_bwd", grid=(NKV,),
        in_specs=[pl.BlockSpec((T, WK), col), pl.BlockSpec((T, WK), col), pl.BlockSpec((T, WK), col), pl.BlockSpec((T, WK), col),
                  pl.BlockSpec((T, WK), col), pl.BlockSpec((T, 128), col), _const((AB, T))],
        out_specs=[pl.BlockSpec((T, WK), col), pl.BlockSpec((T, WK), col), pl.BlockSpec((T, WK), col)],
        out_shape=[jax.ShapeDtypeStruct((T, QW), F32)] * 3,
        compiler_params=_params(("parallel",)),
    )(q, kx, vx, o, dmix, lse, bias)


def _softplus(x):
    return jnp.maximum(x, 0.0) + jnp.log1p(jnp.exp(-jnp.abs(x)))


def _conv_taps(u, halo):
    zext = jnp.concatenate([halo, u], axis=0)
    return [pltpu.roll(zext, m, 0)[8:] for m in (1, 2, 3)]


def _ssd_chunk_common(u, halo, dtr, cw_ref, cb_ref, dtb_ref, alx_ref, e_ref):
    sh1, sh2, sh3 = _conv_taps(u, halo)
    xc = cb_ref[...] + cw_ref[3:4, :] * u + cw_ref[2:3, :] * sh1 + cw_ref[1:2, :] * sh2 + cw_ref[0:1, :] * sh3
    sg = _sigmoid(xc)
    act = xc * sg
    pre_x = _dot_hi(dtr + dtb_ref[...], e_ref[...])
    dt_x = _softplus(pre_x)
    a_x = -jnp.exp(alx_ref[...])
    ri = lax.broadcasted_iota(jnp.int32, (L, L), 0)
    ci = lax.broadcasted_iota(jnp.int32, (L, L), 1)
    tri = ri >= ci
    acs_x = _dot_hi(tri.astype(F32), dt_x * a_x)
    return dict(sh=(sh1, sh2, sh3), xc=xc, sg=sg, act=act, pre_x=pre_x, dt_x=dt_x, a_x=a_x, tri=tri, acs_x=acs_x)


def _decay(acs_x, acs_t, h, tri):
    col = acs_x[:, HD * h:HD * h + 1]
    row = acs_t[HD * h:HD * h + 1, :]
    return jnp.exp(jnp.where(tri, col - row, NEG))


def _ssd_fwd(xbc, z, dtr, convw, convb, dtb, alx, dskx, ssmn, e):
    def body(u_ref, z_ref, dtr_ref, cw_ref, cb_ref, dtb_ref, alx_ref, dsk_ref, sn_ref, e_ref,
             yn_ref, y_ref, hs_ref, halo, hst):
        @pl.when(pl.program_id(0) == 0)
        def _():
            halo[...] = jnp.zeros_like(halo)
            hst[...] = jnp.zeros_like(hst)

        u = u_ref[...]
        cm = _ssd_chunk_common(u, halo[...], dtr_ref[...], cw_ref, cb_ref, dtb_ref, alx_ref, e_ref)
        halo[...] = u[L - 8:, :]
        act, dt_x, acs_x, tri = cm["act"], cm["dt_x"], cm["acs_x"], cm["tri"]
        xs = act[:, :SSM_W]
        acs_l = acs_x[L - 1:L, :]
        lam_x = jnp.exp(acs_x)
        w_x = jnp.exp(acs_l - acs_x)
        gam_x = jnp.exp(acs_l)
        acs_t = acs_x.T
        xd = xs * dt_x
        xb = xd.astype(BF16)
        xw = (xd * w_x).astype(BF16)
        lo = lax.broadcasted_iota(jnp.int32, (1, 128), 1) < HD
        hs_ref[0] = hst[...]
        pieces = []
        for grp in range(2):
            bb = act[:, SSM_W + SSM_N * grp:SSM_W + SSM_N * (grp + 1)].astype(BF16)
            cb_ = act[:, SSM_W + 2 * SSM_N + SSM_N * grp:SSM_W + 2 * SSM_N + SSM_N * (grp + 1)].astype(BF16)
            cbm = _dot_nt(cb_, bb)
            for jj in range(4):
                j = 4 * grp + jj
                sl = slice(128 * j, 128 * j + 128)
                m0 = (cbm * _decay(acs_x, acs_t, 2 * j, tri)).astype(BF16)
                m1 = (cbm * _decay(acs_x, acs_t, 2 * j + 1, tri)).astype(BF16)
                x2 = xb[:, sl]
                ydiag = jnp.where(lo, _dot(m0, x2), _dot(m1, x2))
                hprev = hst[j]
                yoff = lam_x[:, sl] * _dot(cb_, hprev.astype(BF16))
                pieces.append(ydiag + yoff)
                hst[j] = gam_x[:, sl] * hprev + _dot_tn(bb, xw[:, sl])
        y = jnp.concatenate(pieces, axis=1) + dsk_ref[...] * xs
        y_ref[...] = y
        zv = z_ref[...]
        yz = y * (zv * _sigmoid(zv))
        half = SSM_W // 2
        yn = jnp.concatenate([yz[:, :half] * _rs(yz[:, :half]), yz[:, half:] * _rs(yz[:, half:])], axis=1)
        yn_ref[...] = (yn * sn_ref[...]).astype(BF16)

    return pl.pallas_call(
        body, name="ssd_fwd", grid=(NCH,),
        in_specs=[_rows(L, CONV_C), _rows(L, SSM_W), _rows(L, SSM_H), _const((4, CONV_C)), _const((1, CONV_C)), _const((1, SSM_H)),
                  _const((1, SSM_W)), _const((1, SSM_W)), _const((1, SSM_W)), _const((SSM_H, SSM_W))],
        out_specs=[_rows(L, SSM_W), _rows(L, SSM_W), pl.BlockSpec((1, 8, SSM_N, 128), lambda c: (c, 0, 0, 0))],
        out_shape=[jax.ShapeDtypeStruct((T, SSM_W), BF16), jax.ShapeDtypeStruct((T, SSM_W), F32),
                   jax.ShapeDtypeStruct((NCH, 8, SSM_N, 128), F32)],
        scratch_shapes=[pltpu.VMEM((8, CONV_C), F32), pltpu.VMEM((8, SSM_N, 128), F32)],
        compiler_params=_params(("arbitrary",)),
    )(xbc, z, dtr, convw, convb, dtb, alx, dskx, ssmn, e)


def _ssd_bwd(dmix, xbc, z, dtr, y, hs, convw, convb, dtb, alx, dskx, ssmn, e, e1):
    rev = lambda i: (NCH - 1 - i, 0)

    def body(dyn_ref, u_ref, uh_ref, z_ref, dtr_ref, y_ref, hs_ref, cw_ref, cb_ref, dtb_ref, alx_ref, dsk_ref, sn_ref, e_ref, e1_ref,
             dxbc_ref, dz_ref, ddt_ref, dcw_ref, dcb_ref, dsn_ref, dpar_ref, dh, duh, colbuf):
        step = pl.program_id(0)
        c = NCH - 1 - step

        @pl.when(step == 0)
        def _():
            for r in (dh, duh, dcw_ref, dcb_ref, dsn_ref, dpar_ref):
                r[...] = jnp.zeros_like(r)

        u = u_ref[...]
        halo = jnp.where(c > 0, uh_ref[...], 0.0)
        cm = _ssd_chunk_common(u, halo, dtr_ref[...], cw_ref, cb_ref, dtb_ref, alx_ref, e_ref)
        sh1, sh2, sh3 = cm["sh"]
        xc, sg, act, pre_x, dt_x, a_x, tri, acs_x = (cm[k] for k in ("xc", "sg", "act", "pre_x", "dt_x", "a_x", "tri", "acs_x"))
        xs = act[:, :SSM_W]
        acs_l = acs_x[L - 1:L, :]
        lam_x = jnp.exp(acs_x)
        w_x = jnp.exp(acs_l - acs_x)
        gam_x = jnp.exp(acs_l)
        acs_t = acs_x.T
        xd = xs * dt_x
        xb = xd.astype(BF16)
        xdw = xd * w_x
        xw = xdw.astype(BF16)
        lo = lax.broadcasted_iota(jnp.int32, (1, 128), 1) < HD
        row8 = lax.broadcasted_iota(jnp.int32, (8, 1), 0)

        dyn = dyn_ref[...]
        yv = y_ref[...]
        zv = z_ref[...]
        sz = _sigmoid(zv)
        siluz = zv * sz
        yz = yv * siluz
        half = SSM_W // 2
        gy = dyn * sn_ref[...]
        dyz_parts, yzn_parts = [], []
        for hf in range(2):
            part = yz[:, hf * half:(hf + 1) * half]
            r = _rs(part)
            pn = part * r
            gp = gy[:, hf * half:(hf + 1) * half]
            dyz_parts.append(r * (gp - pn * jnp.mean(gp * pn, axis=-1, keepdims=True)))
            yzn_parts.append(pn)
        dyz = jnp.concatenate(dyz_parts, axis=1)
        dsn_ref[...] += jnp.sum(dyn * jnp.concatenate(yzn_parts, axis=1), axis=0, keepdims=True)
        dy = dyz * siluz
        dz_ref[...] = dyz * yv * _dsilu(zv, sz)

        colbuf[...] = jnp.zeros_like(colbuf)
        dx_pieces, dacs_pieces, dacsl_pieces, db_pieces, dc_pieces = [], [], [], [], []
        for grp in range(2):
            bb = act[:, SSM_W + SSM_N * grp:SSM_W + SSM_N * (grp + 1)].astype(BF16)
            cb_ = act[:, SSM_W + 2 * SSM_N + SSM_N * grp:SSM_W + 2 * SSM_N + SSM_N * (grp + 1)].astype(BF16)
            cbm = _dot_nt(cb_, bb)
            dcbm = jnp.zeros((L, L), F32)
            dc_g = jnp.zeros((L, SSM_N), F32)
            db_g = jnp.zeros((L, SSM_N), F32)
            for jj in range(4):
                j = 4 * grp + jj
                sl = slice(128 * j, 128 * j + 128)
                dy2 = dy[:, sl]
                dy2b = dy2.astype(BF16)
                d0 = _decay(acs_x, acs_t, 2 * j, tri)
                d1 = _decay(acs_x, acs_t, 2 * j + 1, tri)
                m0 = cbm * d0
                m1 = cbm * d1
                x2 = xb[:, sl]
                hprev = hs_ref[0, j]
                hprevb = hprev.astype(BF16)
                dhn = dh[j]
                dhnb = dhn.astype(BF16)
                g2 = _dot(bb, dhnb)
                dx_pieces.append(jnp.where(lo, _dot_tn(m0.astype(BF16), dy2b), _dot_tn(m1.astype(BF16), dy2b)) + w_x[:, sl] * g2)
                zero = jnp.zeros_like(dy2b)
                dm0 = _dot_nt(jnp.where(lo, dy2b, zero), x2)
                dm1 = _dot_nt(jnp.where(lo, zero, dy2b), x2)
                dcbm = dcbm + dm0 * d0 + dm1 * d1
                e0 = dm0 * m0
                e1v = dm1 * m1
                colbuf[:, 2 * j:2 * j + 1] = jnp.sum(e0, axis=1, keepdims=True) - jnp.sum(e0.T, axis=1, keepdims=True)
                colbuf[:, 2 * j + 1:2 * j + 2] = jnp.sum(e1v, axis=1, keepdims=True) - jnp.sum(e1v.T, axis=1, keepdims=True)
                yoff = lam_x[:, sl] * _dot(cb_, hprevb)
                gxw = g2 * xdw[:, sl]
                dacs_pieces.append(dy2 * yoff - gxw)
                dacsl_pieces.append(jnp.sum(gxw, axis=0, keepdims=True) + gam_x[:, sl] * jnp.sum(dhn * hprev, axis=0, keepdims=True))
                dyl = (dy2 * lam_x[:, sl]).astype(BF16)
                dc_g = dc_g + _dot_nt(dyl, hprevb)
                db_g = db_g + _dot_nt(xw[:, sl], dhnb)
                dh[j] = gam_x[:, sl] * dhn + _dot_tn(cb_, dyl)
            dcbb = dcbm.astype(BF16)
            dc_pieces.append(dc_g + _dot(dcbb, bb))
            db_pieces.append(db_g + _dot_tn(dcbb, cb_))

        dxd = jnp.concatenate(dx_pieces, axis=1)
        rowi = lax.broadcasted_iota(jnp.int32, (L, 1), 0)
        dacs_x = (jnp.concatenate(dacs_pieces, axis=1) + _dot_hi(colbuf[...], e1_ref[...])
                  + jnp.where(rowi == L - 1, jnp.concatenate(dacsl_pieces, axis=1), 0.0))
        upper = lax.broadcasted_iota(jnp.int32, (L, L), 0) <= lax.broadcasted_iota(jnp.int32, (L, L), 1)
        dadt_x = _dot_hi(upper.astype(F32), dacs_x)
        ddt_x = dxd * xs + dadt_x * a_x
        ddtr = _dot_nt_hi(ddt_x * _sigmoid(pre_x), e_ref[...])
        ddt_ref[...] = ddtr
        dalx = jnp.sum(dadt_x * dt_x, axis=0, keepdims=True) * a_x
        ddskx = jnp.sum(dy * xs, axis=0, keepdims=True)
        par_x = jnp.where(row8 == 1, dalx, 0.0) + jnp.where(row8 == 2, ddskx, 0.0)
        dpar_ref[...] += _dot_nt_hi(par_x, e_ref[...]) + jnp.where(row8 == 0, jnp.sum(ddtr, axis=0, keepdims=True), 0.0)

        dxs = dxd * dt_x + dsk_ref[...] * dy
        dact = jnp.concatenate([dxs] + db_pieces + dc_pieces, axis=1)
        du = dact * _dsilu(xc, sg)
        dcb_ref[...] += jnp.sum(du, axis=0, keepdims=True)
        taps = (sh3, sh2, sh1, u)
        dcw = jnp.zeros((8, CONV_C), F32)
        for k in range(4):
            dcw = dcw + jnp.where(row8 == k, jnp.sum(du * taps[k], axis=0, keepdims=True), 0.0)
        dcw_ref[...] += dcw
        zext = jnp.concatenate([du, duh[...]], axis=0)
        f1, f2, f3 = (pltpu.roll(zext, L + 8 - m, 0)[:L] for m in (1, 2, 3))
        dxbc_ref[...] = cw_ref[3:4, :] * du + cw_ref[2:3, :] * f1 + cw_ref[1:2, :] * f2 + cw_ref[0:1, :] * f3
        duh[...] = du[:8, :]

    return pl.pallas_call(
        body, name="ssd_bwd", grid=(NCH,),
        in_specs=[pl.BlockSpec((L, SSM_W), lambda i: (NCH - 1 - i, 1)), pl.BlockSpec((L, CONV_C), rev),
                  pl.BlockSpec((8, CONV_C), lambda i: (jnp.maximum((NCH - 1 - i) * (L // 8) - 1, 0), 0)),
                  pl.BlockSpec((L, SSM_W), rev), pl.BlockSpec((L, SSM_H), rev), pl.BlockSpec((L, SSM_W), rev),
                  pl.BlockSpec((1, 8, SSM_N, 128), lambda i: (NCH - 1 - i, 0, 0, 0)),
                  _const((4, CONV_C)), _const((1, CONV_C)), _const((1, SSM_H)), _const((1, SSM_W)), _const((1, SSM_W)), _const((1, SSM_W)),
                  _const((SSM_H, SSM_W)), _const((128, SSM_W))],
        out_specs=[pl.BlockSpec((L, CONV_C), rev), pl.BlockSpec((L, SSM_W), rev), pl.BlockSpec((L, SSM_H), rev),
                   _const((8, CONV_C)), _const((1, CONV_C)), _const((1, SSM_W)), _const((8, SSM_H))],
        out_shape=[jax.ShapeDtypeStruct((T, CONV_C), F32), jax.ShapeDtypeStruct((T, SSM_W), F32), jax.ShapeDtypeStruct((T, SSM_H), F32),
                   jax.ShapeDtypeStruct((8, CONV_C), F32), jax.ShapeDtypeStruct((1, CONV_C), F32), jax.ShapeDtypeStruct((1, SSM_W), F32),
                   jax.ShapeDtypeStruct((8, SSM_H), F32)],
        scratch_shapes=[pltpu.VMEM((8, SSM_N, 128), F32), pltpu.VMEM((8, CONV_C), F32), pltpu.VMEM((L, 128), F32)],
        compiler_params=_params(("arbitrary",)),
    )(dmix, xbc, xbc, z, dtr, y, hs, convw, convb, dtb, alx, dskx, ssmn, e, e1)


def _loss_head(x3, target):
    tm = 512

    def body(x_ref, t_ref, dy_ref, ss_ref):
        @pl.when(pl.program_id(0) == 0)
        def _():
            ss_ref[...] = jnp.zeros_like(ss_ref)

        err = x_ref[...] - t_ref[...]
        dy_ref[...] = err * (1.0 / D)
        ss_ref[...] += jnp.sum(jnp.sum(err * err, axis=1, keepdims=True), axis=0, keepdims=True)

    return pl.pallas_call(
        body, name="loss_head", grid=(T // tm,),
        in_specs=[_rows(tm, D), _rows(tm, D)],
        out_specs=[_rows(tm, D), _const((1, 128))],
        out_shape=[jax.ShapeDtypeStruct((T, D), F32), jax.ShapeDtypeStruct((1, 128), F32)],
        compiler_params=_params(("arbitrary",)),
    )(x3, target)


def _adam_math(w, g, m, v):
    m = ADAM_B1 * m + (1.0 - ADAM_B1) * g
    v = ADAM_B2 * v + (1.0 - ADAM_B2) * (g * g)
    m_hat = m / (1.0 - ADAM_B1 ** ADAM_STEP)
    v_hat = v / (1.0 - ADAM_B2 ** ADAM_STEP)
    delta = -ADAM_LR * (m_hat / (jnp.sqrt(v_hat) + ADAM_EPS) + ADAM_WD * w)
    return delta, m, v


def _adamw(w, m, v, parts, name):
    rows, cols = w.shape
    tr = rows if rows <= 512 else 256
    assert rows % tr == 0
    n_parts = len(parts)

    def body(*refs):
        w_ref, m_ref, v_ref = refs[:3]
        p_refs = refs[3:3 + n_parts]
        g_ref, d_ref, nm_ref, nv_ref = refs[3 + n_parts:]
        g = p_refs[0][...].astype(F32)
        for p in p_refs[1:]:
            g = g + p[...].astype(F32)
        delta, nm, nv = _adam_math(w_ref[...], g, m_ref[...], v_ref[...])
        g_ref[...] = g
        d_ref[...] = delta
        nm_ref[...] = nm
        nv_ref[...] = nv

    spec = pl.BlockSpec((tr, cols), lambda i: (i, 0))
    return pl.pallas_call(
        body, name=name, grid=(rows // tr,),
        in_specs=[spec] * (3 + n_parts), out_specs=[spec] * 4,
        out_shape=[jax.ShapeDtypeStruct((rows, cols), F32)] * 4,
        compiler_params=_params(("parallel",)),
    )(w, m, v, *parts)


def _row_tile(rows):
    return rows if rows <= 512 else 256


def _adamw_sharded(w, m, v, chip_sum, from_chips, chip_ids, name):
    rows, cols = w.shape
    prow, pcol = chip_sum.shape[1:]
    tr = _row_tile(rows)
    ptr = prow if tr == rows else tr
    assert rows % tr == 0 and ptr >= tr and pcol >= cols

    def body(ids_ref, w_ref, m_ref, v_ref, s_ref, r1_ref, r2_ref, r3_ref, g_ref, d_ref, nm_ref, nv_ref):
        g = s_ref[0]
        for r in (r1_ref, r2_ref, r3_ref):
            g = g + r[0].astype(F32)
        g = g[:tr, :cols]
        delta, nm, nv = _adam_math(w_ref[...], g, m_ref[...], v_ref[...])
        g_ref[...] = g
        d_ref[...] = delta
        nm_ref[...] = nm
        nv_ref[...] = nv

    spec = pl.BlockSpec((tr, cols), lambda i, ids: (i, 0))
    part = lambda k: pl.BlockSpec((1, ptr, pcol), lambda i, ids: (ids[k], i, 0))
    return pl.pallas_call(
        body, name=name,
        grid_spec=pltpu.PrefetchScalarGridSpec(
            num_scalar_prefetch=1, grid=(rows // tr,),
            in_specs=[spec, spec, spec, part(0), part(1), part(2), part(3)], out_specs=[spec] * 4),
        out_shape=[jax.ShapeDtypeStruct((rows, cols), F32)] * 4,
        compiler_params=_params(("parallel",)),
    )(chip_ids, w, m, v, chip_sum, from_chips, from_chips, from_chips)


def _chip_sum(g32, recv, core, name):
    rows, cols = g32.shape[1:]
    tr = _row_tile(rows)
    assert rows % tr == 0

    def body(core_ref, a_ref, b_ref, s_ref, sb_ref):
        s = a_ref[...] + b_ref[...].astype(F32)
        s_ref[...] = s
        sb_ref[...] = s.astype(BF16)

    by_chip = pl.BlockSpec((1, tr, cols), lambda k, i, core_ref: (k, i, 0))
    return pl.pallas_call(
        body, name=name,
        grid_spec=pltpu.PrefetchScalarGridSpec(
            num_scalar_prefetch=1, grid=(N_DEV // 2, rows // tr),
            in_specs=[pl.BlockSpec((1, tr, cols), lambda k, i, core_ref: (2 * k + core_ref[0], i, 0)), by_chip],
            out_specs=[by_chip, by_chip]),
        out_shape=[jax.ShapeDtypeStruct((N_DEV // 2, rows, cols), F32), jax.ShapeDtypeStruct((N_DEV // 2, rows, cols), BF16)],
        compiler_params=_params(("parallel", "parallel")),
    )(core, g32, recv)


def _place():
    return lax.axis_index("x"), lax.axis_index("y"), lax.axis_index("c")


ANY = pl.BlockSpec(memory_space=pl.ANY)


def _all_gather(arrays, by_cols):
    n = len(arrays)

    def body(*refs):
        ins, outs = refs[:n], refs[n:2 * n]
        send_sems, recv_sems, local_sems = refs[2 * n:]
        x, y, c = _place()
        me, sibling = (x, y, c), (x, y, 1 - c)
        chips = [(1 - x, y), (x, 1 - y), (1 - x, 1 - y)]

        def place_of(a, px, py, pc):
            d = 4 * px + 2 * py + pc
            if by_cols[a]:
                cols = arrays[a].shape[1]
                return outs[a].at[:, pl.ds(pl.multiple_of(d * cols, 128), cols)]
            return outs[a].at[d]

        def copy(a, k, block, to, src=None):
            dst = place_of(a, *block)
            return pltpu.make_async_remote_copy(src_ref=dst if src is None else src, dst_ref=dst,
                                                send_sem=send_sems.at[7 * a + k], recv_sem=recv_sems.at[7 * a + k],
                                                device_id=to, device_id_type=MESH)

        mine = [pltpu.make_async_copy(ins[a], place_of(a, *me), local_sems.at[a]) for a in range(n)]
        for cp in mine:
            cp.start()
        first = []
        for a in range(n):
            first.append(copy(a, 0, me, sibling, src=ins[a]))
            first += [copy(a, 1 + j, me, (*chip, c), src=ins[a]) for j, chip in enumerate(chips)]
        for cp in first:
            cp.start()
        passed = []
        for j, chip in enumerate(chips):
            for a in range(n):
                copy(a, 1 + j, (*chip, c), me).wait_recv()
                fwd = copy(a, 4 + j, (*chip, c), sibling)
                fwd.start()
                passed.append(fwd)
        for a in range(n):
            copy(a, 0, sibling, me).wait_recv()
            for j, chip in enumerate(chips):
                copy(a, 4 + j, (*chip, 1 - c), me).wait_recv()
        for cp in first + passed:
            cp.wait_send()
        for cp in mine:
            cp.wait()

    return pl.pallas_call(
        body, name="all_gather_weights",
        in_specs=[ANY] * n, out_specs=[ANY] * n,
        out_shape=[jax.ShapeDtypeStruct((a.shape[0], N_DEV * a.shape[1]) if by_cols[i] else (N_DEV,) + a.shape, a.dtype)
                   for i, a in enumerate(arrays)],
        scratch_shapes=[pltpu.SemaphoreType.DMA((7 * n,)), pltpu.SemaphoreType.DMA((7 * n,)), pltpu.SemaphoreType.DMA((n,))],
    )(*arrays)


def _swap_with_sibling(arrays):
    n = len(arrays)
    half = N_DEV // 2

    def body(*refs):
        ins, outs = refs[:n], refs[n:2 * n]
        send_sems, recv_sems = refs[2 * n:]
        x, y, c = _place()
        cps = [pltpu.make_async_remote_copy(src_ref=ins[a].at[2 * k + (1 - c)], dst_ref=outs[a].at[k],
                                            send_sem=send_sems.at[half * a + k], recv_sem=recv_sems.at[half * a + k],
                                            device_id=(x, y, 1 - c), device_id_type=MESH)
               for a in range(n) for k in range(half)]
        for cp in cps:
            cp.start()
        for cp in cps:
            cp.wait()

    return pl.pallas_call(
        body, name="grad_swap_sibling",
        in_specs=[ANY] * n, out_specs=[ANY] * n,
        out_shape=[jax.ShapeDtypeStruct((half,) + a.shape[1:], a.dtype) for a in arrays],
        scratch_shapes=[pltpu.SemaphoreType.DMA((half * n,)), pltpu.SemaphoreType.DMA((half * n,))],
    )(*arrays)


def _exchange_chips(arrays):
    n = len(arrays)

    def body(*refs):
        ins, outs = refs[:n], refs[n:2 * n]
        send_sems, recv_sems = refs[2 * n:]
        x, y, c = _place()
        chips = [(1 - x, y), (x, 1 - y), (1 - x, 1 - y)]
        my_chip = 2 * x + y
        cps = []
        for a in range(n):
            for j, (px, py) in enumerate(chips):
                cps.append(pltpu.make_async_remote_copy(src_ref=ins[a].at[2 * px + py], dst_ref=outs[a].at[my_chip],
                                                        send_sem=send_sems.at[3 * a + j], recv_sem=recv_sems.at[3 * a + j],
                                                        device_id=(px, py, c), device_id_type=MESH))
        for cp in cps:
            cp.start()
        for a in range(n):
            for j, (px, py) in enumerate(chips):
                pltpu.make_async_remote_copy(src_ref=ins[a].at[my_chip], dst_ref=outs[a].at[2 * px + py],
                                             send_sem=send_sems.at[3 * a + j], recv_sem=recv_sems.at[3 * a + j],
                                             device_id=(px, py, c), device_id_type=MESH).wait_recv()
        for cp in cps:
            cp.wait_send()

    return pl.pallas_call(
        body, name="grad_exchange_chips",
        in_specs=[ANY] * n, out_specs=[ANY] * n,
        out_shape=[jax.ShapeDtypeStruct(a.shape, a.dtype) for a in arrays],
        scratch_shapes=[pltpu.SemaphoreType.DMA((3 * n,)), pltpu.SemaphoreType.DMA((3 * n,))],
    )(*arrays)


def _all_reduce_small(v):
    rows = v.shape[0]

    def body(v_ref, out_ref, gath, send_sems, recv_sems):
        x, y, c = _place()
        me, sibling = (x, y, c), (x, y, 1 - c)
        chips = [(1 - x, y), (x, 1 - y), (1 - x, 1 - y)]

        def blk(px, py, pc):
            return gath.at[pl.ds((4 * px + 2 * py + pc) * rows, rows), :]

        def copy(k, block, to, src=None):
            return pltpu.make_async_remote_copy(src_ref=blk(*block) if src is None else src, dst_ref=blk(*block),
                                                send_sem=send_sems.at[k], recv_sem=recv_sems.at[k], device_id=to, device_id_type=MESH)

        gath[pl.ds((4 * x + 2 * y + c) * rows, rows), :] = v_ref[...]
        first = [copy(0, me, sibling, src=v_ref)] + [copy(1 + j, me, (*chip, c), src=v_ref) for j, chip in enumerate(chips)]
        for cp in first:
            cp.start()
        passed = [copy(4 + j, (*chip, c), sibling) for j, chip in enumerate(chips)]
        for j, chip in enumerate(chips):
            copy(1 + j, (*chip, c), me).wait_recv()
            passed[j].start()
        copy(0, sibling, me).wait_recv()
        for j, chip in enumerate(chips):
            copy(4 + j, (*chip, 1 - c), me).wait_recv()
        for cp in first + passed:
            cp.wait_send()
        acc = gath[0:rows, :]
        for d in range(1, N_DEV):
            acc = acc + gath[d * rows:(d + 1) * rows, :]
        out_ref[...] = acc

    vm = pl.BlockSpec(memory_space=pltpu.VMEM)
    return pl.pallas_call(
        body, name="all_reduce_small",
        in_specs=[vm], out_specs=vm,
        out_shape=jax.ShapeDtypeStruct(v.shape, F32),
        scratch_shapes=[pltpu.VMEM((N_DEV * rows, 128), F32), pltpu.SemaphoreType.DMA((7,)), pltpu.SemaphoreType.DMA((7,))],
    )(v)


def _rope_tables(positions):
    inv_freq = ROPE_THETA ** (-jnp.arange(0, HD, 2, dtype=F32) / HD)
    ang = positions.reshape(T).astype(F32)[:, None] * inv_freq
    ang = jnp.concatenate([ang, ang, ang, ang], axis=-1)
    lo_half = (jnp.arange(128) % HD) < (HD // 2)
    return jnp.cos(ang), jnp.where(lo_half, -jnp.sin(ang), jnp.sin(ang))


def _selectors():
    lane = jnp.arange(QW)
    e = (lane[None, :] // HD == jnp.arange(SSM_H)[:, None]).astype(F32)
    e1 = ((lane[None, :] == HD * jnp.arange(128)[:, None]) & (jnp.arange(128)[:, None] < SSM_H)).astype(F32)
    src = jnp.arange(KVW)
    ex = ((lane[None, :] // (HD * (NQ // NKV)) == src[:, None] // HD) & (lane[None, :] % HD == src[:, None] % HD)).astype(F32)
    return e, e1, ex


def _local_step(x, positions, target, w):
    def row_shards(pair):
        return tuple(a.reshape(N_DEV, -1, a.shape[1]) for a in pair)
    cos, sin_s = _rope_tables(positions)
    e, e1, exf = _selectors()
    bias = _attn_bias()
    alx = jnp.repeat(w["a_log"], HD, axis=1)
    dskx = jnp.repeat(w["d_skip"], HD, axis=1)

    x1, n1, a1, b1, hm1, h1 = _ffn_fwd(x, w["ffn1_pre_norm"], w["ffn1_w_gate"], w["ffn1_w_up"], w["ffn1_w_down"], w["ffn1_post_norm"], "ffn1_fwd")
    n2, q, kx, vx, xbc, z, dtr = _inproj_fwd(x1, w["mix_pre_norm"], w["w_in"], cos, sin_s, exf.astype(BF16))
    attn, lse = _attn_fwd(q, kx, vx, bias)
    yn, y, hs = _ssd_fwd(xbc, z, dtr, w["conv_w"], w["conv_b"], w["dt_bias"], alx, dskx, w["ssm_norm"], e)
    x2, h2 = _outproj_fwd(x1, attn, yn, w["w_out"], w["mix_post_norm"])
    x3, n3, a3, b3, hm3, h3 = _ffn_fwd(x2, w["ffn2_pre_norm"], w["ffn2_w_gate"], w["ffn2_w_up"], w["ffn2_w_down"], w["ffn2_post_norm"], "ffn2_fwd")
    dx3, ss = _loss_head(x3, target)

    g = {}
    dx2, da3, db3, dh3, g["ffn2_pre_norm"], g["ffn2_post_norm"] = _ffn_bwd(
        dx3, x2, a3, b3, h3, w["ffn2_pre_norm"], w["ffn2_post_norm"], w["ffn2_w_gate"], w["ffn2_w_up"], w["ffn2_w_down"], "ffn2_bwd")
    g["ffn2_w_down"] = row_shards(_matmul_tn(hm3, dh3, "ffn2_dwd", tm=FSH))
    g["ffn2_w_gate"] = _matmul_tn_cols(n3, da3, "ffn2_dwg")
    g["ffn2_w_up"] = _matmul_tn_cols(n3, db3, "ffn2_dwu")

    dh2, dmix, g["mix_post_norm"] = _outproj_bwd(dx2, h2, w["mix_post_norm"], w["w_out"])
    g["w_out"] = _dwout(attn, yn, dh2)
    dq, dkx, dvx = _attn_bwd(q, kx, vx, attn, dmix, lse, bias)
    dxbc, dz, ddt, dcw, g["conv_b"], g["ssm_norm"], dpar = _ssd_bwd(
        dmix, xbc, z, dtr, y, hs, w["conv_w"], w["conv_b"], w["dt_bias"], alx, dskx, w["ssm_norm"], e, e1)
    g["conv_w"] = dcw[0:4]
    g["dt_bias"], g["a_log"], g["d_skip"] = dpar[0:1], dpar[1:2], dpar[2:3]
    dx1, dproj, g["mix_pre_norm"] = _inproj_bwd(dx2, dq, dkx, dvx, dxbc, dz, ddt, x1, w["mix_pre_norm"], w["w_in"], cos, sin_s, exf)
    g["w_in"] = tuple(jnp.transpose(a.reshape(D, N_DEV, -1), (1, 0, 2)) for a in _matmul_tn(n2, dproj, "dwin"))

    dx0, da1, db1, dh1, g["ffn1_pre_norm"], g["ffn1_post_norm"] = _ffn_bwd(
        dx1, x, a1, b1, h1, w["ffn1_pre_norm"], w["ffn1_post_norm"], w["ffn1_w_gate"], w["ffn1_w_up"], w["ffn1_w_down"], "ffn1_bwd")
    g["ffn1_w_down"] = row_shards(_matmul_tn(hm1, dh1, "ffn1_dwd", tm=FSH))
    g["ffn1_w_gate"] = _matmul_tn_cols(n1, da1, "ffn1_dwg")
    g["ffn1_w_up"] = _matmul_tn_cols(n1, db1, "ffn1_dwu")
    return ss, dx0, g


WEIGHTS = ['ffn1_pre_norm', 'ffn1_w_gate', 'ffn1_w_up', 'ffn1_w_down', 'ffn1_post_norm', 'mix_pre_norm', 'w_in', 'conv_w', 'conv_b',
           'dt_bias', 'a_log', 'd_skip', 'ssm_norm', 'w_out', 'mix_post_norm', 'ffn2_pre_norm', 'ffn2_w_gate', 'ffn2_w_up',
           'ffn2_w_down', 'ffn2_post_norm']
COL_SHARDED = ['ffn1_w_gate', 'ffn1_w_up', 'ffn2_w_gate', 'ffn2_w_up', 'w_in']
ROW_SHARDED = ['ffn1_w_down', 'ffn2_w_down', 'w_out']
BIG = COL_SHARDED + ROW_SHARDED
SMALL = ['ffn1_pre_norm', 'ffn1_post_norm', 'mix_pre_norm', 'conv_b', 'dt_bias', 'a_log', 'd_skip', 'ssm_norm', 'mix_post_norm',
         'ffn2_pre_norm', 'ffn2_post_norm']


def _gather_whole_weights(w):
    gate_up = [jnp.pad(w[n].astype(BF16), ((0, 0), (0, FSH - FSR))) for n in COL_SHARDED[:4]]
    down = [jnp.pad(w[n].astype(BF16), ((0, FSH - FSR), (0, 0))) for n in ROW_SHARDED[:2]]
    rest = [w["w_in"].astype(BF16), w["w_out"].astype(BF16), w["conv_w"]]
    gathered = _all_gather(gate_up + down + rest, [True] * 4 + [False] * 5)
    whole = {n: w[n] for n in SMALL}

    def cols(a):
        return jnp.transpose(a, (1, 0, 2)).reshape(a.shape[1], -1)

    for i, n in enumerate(COL_SHARDED[:4]):
        whole[n] = gathered[i]
    for i, n in enumerate(ROW_SHARDED[:2]):
        whole[n] = gathered[4 + i].reshape(-1, D)
    whole["w_in"] = cols(gathered[6])
    whole["w_out"] = gathered[7].reshape(-1, D)
    whole["conv_w"] = cols(gathered[8])
    return whole


def kernel(x, positions, ffn1_pre_norm, ffn1_w_gate, ffn1_w_up, ffn1_w_down, ffn1_post_norm, mix_pre_norm, w_in, conv_w, conv_b, dt_bias, a_log, d_skip, ssm_norm, w_out, mix_post_norm, ffn2_pre_norm, ffn2_w_gate, ffn2_w_up, ffn2_w_down, ffn2_post_norm, loss_target, m_ffn1_pre_norm, m_ffn1_w_gate, m_ffn1_w_up, m_ffn1_w_down, m_ffn1_post_norm, m_mix_pre_norm, m_w_in, m_conv_w, m_conv_b, m_dt_bias, m_a_log, m_d_skip, m_ssm_norm, m_w_out, m_mix_post_norm, m_ffn2_pre_norm, m_ffn2_w_gate, m_ffn2_w_up, m_ffn2_w_down, m_ffn2_post_norm, v_ffn1_pre_norm, v_ffn1_w_gate, v_ffn1_w_up, v_ffn1_w_down, v_ffn1_post_norm, v_mix_pre_norm, v_w_in, v_conv_w, v_conv_b, v_dt_bias, v_a_log, v_d_skip, v_ssm_norm, v_w_out, v_mix_post_norm, v_ffn2_pre_norm, v_ffn2_w_gate, v_ffn2_w_up, v_ffn2_w_down, v_ffn2_post_norm):
    given = dict(locals())
    drop = lambda n, a: a if n in SMALL else a[0]
    w = {n: drop(n, given[n]) for n in WEIGHTS}
    m = {n: drop(n, given["m_" + n]) for n in WEIGHTS}
    v = {n: drop(n, given["v_" + n]) for n in WEIGHTS}
    cx, cy, cc = _place()
    my_chip = 2 * cx + cy
    others = [2 * (1 - cx) + cy, 2 * cx + (1 - cy), 2 * (1 - cx) + (1 - cy)]

    whole = _gather_whole_weights(w)
    ss, grad_x, g = _local_step(x[0], positions, loss_target[0], whole)
    loss = lax.psum(0.5 * ss[0, 0] / D, ("x", "y", "c"))

    from_sibling = _swap_with_sibling([g[n][1] for n in BIG])
    core = jnp.stack([cc]).astype(jnp.int32)
    sums = [_chip_sum(g[n][0], recv, core, "chip_sum_" + n) for n, recv in zip(BIG, from_sibling)]
    from_chips = _exchange_chips([s[1] for s in sums])
    chip_ids = jnp.stack([my_chip] + others).astype(jnp.int32)
    out_g, out_d, out_m, out_v = {}, {}, {}, {}
    for n, s, recv in zip(BIG, sums, from_chips):
        out_g[n], out_d[n], out_m[n], out_v[n] = _adamw_sharded(w[n], m[n], v[n], s[0], recv, chip_ids, "adamw_" + n)

    small_g = jnp.concatenate([g[n].reshape(-1) for n in SMALL] + [g["conv_w"].reshape(-1)])
    n_small = small_g.shape[0]
    rows = -(-n_small // 128 // 8) * 8
    packed = jnp.pad(small_g, (0, rows * 128 - n_small)).reshape(rows, 128)
    total = _all_reduce_small(packed).reshape(-1)
    sizes = [w[n].size for n in SMALL]
    offs = [0]
    for s_ in sizes:
        offs.append(offs[-1] + s_)
    gs = {n: total[offs[i]:offs[i + 1]].reshape(w[n].shape) for i, n in enumerate(SMALL)}
    gcw = total[offs[-1]:offs[-1] + 4 * CONV_C].reshape(4, CONV_C)
    gs["conv_w"] = lax.dynamic_slice_in_dim(gcw, (4 * cx + 2 * cy + cc) * (CONV_C // N_DEV), CONV_C // N_DEV, axis=1)
    names = SMALL + ["conv_w"]

    def pack(d):
        flat = jnp.concatenate([d[n].reshape(-1) for n in names])
        return jnp.pad(flat, (0, rows * 128 - flat.shape[0])).reshape(rows, 128)

    pg, pd, pm, pv = _adamw(pack(w), pack(m), pack(v), [pack(gs)], "adamw_small")
    o2 = [0]
    for n in names:
        o2.append(o2[-1] + w[n].size)
    for i, n in enumerate(names):
        for dst, src in ((out_g, pg), (out_d, pd), (out_m, pm), (out_v, pv)):
            dst[n] = src.reshape(-1)[o2[i]:o2[i + 1]].reshape(w[n].shape)

    outs = [loss, grad_x[None]]
    for d in (out_g, out_d, out_m, out_v):
        outs += [d[n] if n in SMALL else d[n][None] for n in WEIGHTS]
    return tuple(outs)
```

```python
import functools
import math

import jax
import jax.numpy as jnp
from jax import lax
from jax.experimental import pallas as pl
from jax.experimental.pallas import tpu as pltpu

F32 = jnp.float32
BF16 = jnp.bfloat16
HI = lax.Precision.HIGHEST
MESH = pl.DeviceIdType.MESH

N_DEV = 8
T = 2048
D = 1024
FF = 2816
FSR = FF // N_DEV
FSH = 384
FFP = N_DEV * FSH
HD = 64
NQ = 16
NKV = 4
QW = NQ * HD
KVW = NKV * HD
SSM_W = 1024
SSM_H = 16
SSM_N = 128
CONV_C = SSM_W + 2 * 2 * SSM_N
IN_COLS = 4112
L = 128
NCH = T // L
AB = 256
NAB = T // AB
EPS = 1e-6
NEG = -1e30
ROPE_THETA = 10000.0
DILATIONS = ((128, 1), (512, 4), (2048, 16))

ADAM_LR = 0.001
ADAM_B1 = 0.9
ADAM_B2 = 0.999
ADAM_EPS = 1e-08
ADAM_WD = 0.01
ADAM_STEP = 10

VMEM_LIMIT = 58 * 1024 * 1024


def _params(sem, vmem=VMEM_LIMIT):
    return pltpu.CompilerParams(dimension_semantics=sem, vmem_limit_bytes=vmem)


def _dot(a, b):
    return jnp.dot(a, b, preferred_element_type=F32)


def _dot_nt(a, b):
    return lax.dot_general(a, b, (((1,), (1,)), ((), ())), preferred_element_type=F32)


def _dot_tn(a, b):
    return lax.dot_general(a, b, (((0,), (0,)), ((), ())), preferred_element_type=F32)


def _dot_hi(a, b):
    return jnp.dot(a, b, preferred_element_type=F32, precision=HI)


def _dot_nt_hi(a, b):
    return lax.dot_general(a, b, (((1,), (1,)), ((), ())), preferred_element_type=F32, precision=HI)


def _rs(x):
    return lax.rsqrt(jnp.mean(x * x, axis=-1, keepdims=True) + EPS)


def _sigmoid(x):
    return jax.nn.sigmoid(x)


def _dsilu(x, s):
    return s * (1.0 + x * (1.0 - s))


def _resident(shape):
    nd = len(shape)
    return pl.BlockSpec(shape, lambda *_: (0,) * nd, pipeline_mode=pl.Buffered(1))


def _const(shape):
    nd = len(shape)
    return pl.BlockSpec(shape, lambda *_: (0,) * nd)


def _rows(tm, cols):
    return pl.BlockSpec((tm, cols), lambda i: (i, 0))


ANY = pl.BlockSpec(memory_space=pl.ANY)


def _place():
    return lax.axis_index("x"), lax.axis_index("y"), lax.axis_index("c")


def _gather_duty(arrays, by_cols):
    n = len(arrays)
    results = [jax.ShapeDtypeStruct((a.shape[0], N_DEV * a.shape[1]) if by_cols[i] else (N_DEV,) + a.shape, a.dtype)
               for i, a in enumerate(arrays)]

    def make(ins, outs, send_sems, recv_sems, local_sems):
        x, y, c = _place()
        me, sibling = (x, y, c), (x, y, 1 - c)
        chips = [(1 - x, y), (x, 1 - y), (1 - x, 1 - y)]

        def place_of(a, px, py, pc):
            d = 4 * px + 2 * py + pc
            if by_cols[a]:
                cols = arrays[a].shape[1]
                return outs[a].at[:, pl.ds(pl.multiple_of(d * cols, 128), cols)]
            return outs[a].at[d]

        def copy(a, k, block, to, src=None):
            dst = place_of(a, *block)
            return pltpu.make_async_remote_copy(src_ref=dst if src is None else src, dst_ref=dst,
                                                send_sem=send_sems.at[7 * a + k], recv_sem=recv_sems.at[7 * a + k],
                                                device_id=to, device_id_type=MESH)

        def own(a):
            return pltpu.make_async_copy(ins[a], place_of(a, *me), local_sems.at[a])

        def first(a):
            return [copy(a, 0, me, sibling, src=ins[a])] + [copy(a, 1 + j, me, (*chip, c), src=ins[a]) for j, chip in enumerate(chips)]

        def start():
            for a in range(n):
                own(a).start()
            for a in range(n):
                for cp in first(a):
                    cp.start()

        def finish():
            for j, chip in enumerate(chips):
                for a in range(n):
                    copy(a, 1 + j, (*chip, c), me).wait_recv()
                    copy(a, 4 + j, (*chip, c), sibling).start()
            for a in range(n):
                copy(a, 0, sibling, me).wait_recv()
                for j, chip in enumerate(chips):
                    copy(a, 4 + j, (*chip, 1 - c), me).wait_recv()
            for a in range(n):
                for cp in first(a) + [copy(a, 4 + j, (*chip, c), sibling) for j, chip in enumerate(chips)]:
                    cp.wait_send()
                own(a).wait()

        return start, finish

    return dict(operands=list(arrays), results=results, sems=(7 * n, 7 * n, n), make=make)


def _swap_duty(arrays):
    n = len(arrays)
    half = N_DEV // 2
    results = [jax.ShapeDtypeStruct((half,) + a.shape[1:], a.dtype) for a in arrays]

    def make(ins, outs, send_sems, recv_sems):
        x, y, c = _place()

        def copies():
            return [pltpu.make_async_remote_copy(src_ref=ins[a].at[2 * k + (1 - c)], dst_ref=outs[a].at[k],
                                                 send_sem=send_sems.at[half * a + k], recv_sem=recv_sems.at[half * a + k],
                                                 device_id=(x, y, 1 - c), device_id_type=MESH)
                    for a in range(n) for k in range(half)]

        def start():
            for cp in copies():
                cp.start()

        def finish():
            for cp in copies():
                cp.wait()

        return start, finish

    return dict(operands=list(arrays), results=results, sems=(half * n, half * n), make=make)


def _exchange_duty(arrays):
    n = len(arrays)
    results = [jax.ShapeDtypeStruct(a.shape, a.dtype) for a in arrays]

    def make(ins, outs, send_sems, recv_sems):
        x, y, c = _place()
        chips = [(1 - x, y), (x, 1 - y), (1 - x, 1 - y)]
        my_chip = 2 * x + y

        def sends():
            return [pltpu.make_async_remote_copy(src_ref=ins[a].at[2 * px + py], dst_ref=outs[a].at[my_chip],
                                                 send_sem=send_sems.at[3 * a + j], recv_sem=recv_sems.at[3 * a + j],
                                                 device_id=(px, py, c), device_id_type=MESH)
                    for a in range(n) for j, (px, py) in enumerate(chips)]

        def start():
            for cp in sends():
                cp.start()

        def finish():
            for a in range(n):
                for j, (px, py) in enumerate(chips):
                    pltpu.make_async_remote_copy(src_ref=ins[a].at[my_chip], dst_ref=outs[a].at[2 * px + py],
                                                 send_sem=send_sems.at[3 * a + j], recv_sem=recv_sems.at[3 * a + j],
                                                 device_id=(px, py, c), device_id_type=MESH).wait_recv()
            for cp in sends():
                cp.wait_send()

        return start, finish

    return dict(operands=list(arrays), results=results, sems=(3 * n, 3 * n), make=make)


def _call(body, *, name, grid, in_specs, out_specs, out_shape, args, sem, scratch=(), duties=()):
    n_in, n_out, n_scr = len(in_specs), len(out_specs), len(scratch)
    sem_shapes = [pltpu.SemaphoreType.DMA((k,)) for d in duties for k in d["sems"]]

    def full(*refs):
        pos = [0]

        def take(k):
            pos[0] += k
            return refs[pos[0] - k:pos[0]]

        ins = take(n_in)
        d_ins = [take(len(d["operands"])) for d in duties]
        outs = take(n_out)
        d_outs = [take(len(d["results"])) for d in duties]
        scr = take(n_scr)
        d_sems = [take(len(d["sems"])) for d in duties]
        hooks = [d["make"](di, do, *ds) for d, di, do, ds in zip(duties, d_ins, d_outs, d_sems)]
        if grid and hooks:
            ids = [pl.program_id(k) for k in range(len(grid))]
            first = functools.reduce(jnp.logical_and, [i == 0 for i in ids])
            last = functools.reduce(jnp.logical_and, [i == g - 1 for i, g in zip(ids, grid)])

            @pl.when(first)
            def _():
                for start, _ in hooks:
                    start()

            body(*ins, *outs, *scr)

            @pl.when(last)
            def _():
                for _, finish in hooks:
                    finish()
        else:
            for start, _ in hooks:
                start()
            body(*ins, *outs, *scr)
            for _, finish in hooks:
                finish()

    d_args = [a for d in duties for a in d["operands"]]
    d_res = [r for d in duties for r in d["results"]]
    kwargs = dict(grid=grid) if grid else {}
    res = pl.pallas_call(
        full, name=name, in_specs=list(in_specs) + [ANY] * len(d_args), out_specs=list(out_specs) + [ANY] * len(d_res),
        out_shape=list(out_shape) + d_res, scratch_shapes=list(scratch) + sem_shapes,
        compiler_params=_params(sem) if grid else None, **kwargs,
    )(*args, *d_args)
    own, rest = list(res[:n_out]), list(res[n_out:])
    by_duty = []
    for d in duties:
        by_duty.append(rest[:len(d["results"])])
        rest = rest[len(d["results"]):]
    return own, by_duty


def _comm_only(duties, name):
    return _call(lambda: None, name=name, grid=None, in_specs=[], out_specs=[], out_shape=[], args=[], sem=None, duties=duties)[1]


def _ffn_fwd(x, gpre, wg, wu, wd, gpost, name, duties=()):
    tm = 256

    def body(x_ref, gpre_ref, wg_ref, wu_ref, wd_ref, gpost_ref, xo_ref, n_ref, a_ref, b_ref, hm_ref, h_ref):
        xv = x_ref[...]
        n = (xv * _rs(xv) * gpre_ref[...]).astype(BF16)
        a = _dot(n, wg_ref[...])
        b = _dot(n, wu_ref[...])
        hm = (a * _sigmoid(a) * b).astype(BF16)
        h = _dot(hm, wd_ref[...])
        xo_ref[...] = xv + 0.5 * (h * _rs(h) * gpost_ref[...])
        n_ref[...] = n
        a_ref[...] = a.astype(BF16)
        b_ref[...] = b.astype(BF16)
        hm_ref[...] = hm
        h_ref[...] = h

    return _call(
        body, name=name, grid=(T // tm,),
        in_specs=[_rows(tm, D), _const((1, D)), _resident((D, FFP)), _resident((D, FFP)), _resident((FFP, D)), _const((1, D))],
        out_specs=[_rows(tm, D), _rows(tm, D), _rows(tm, FFP), _rows(tm, FFP), _rows(tm, FFP), _rows(tm, D)],
        out_shape=[jax.ShapeDtypeStruct((T, D), F32), jax.ShapeDtypeStruct((T, D), BF16), jax.ShapeDtypeStruct((T, FFP), BF16),
                   jax.ShapeDtypeStruct((T, FFP), BF16), jax.ShapeDtypeStruct((T, FFP), BF16), jax.ShapeDtypeStruct((T, D), F32)],
        args=[x, gpre, wg, wu, wd, gpost], sem=("arbitrary",), duties=duties)


def _ffn_bwd(dxo, x, a, b, h, gpre, gpost, wg, wu, wd, name, duties=()):
    tm = 256

    def body(dxo_ref, x_ref, a_ref, b_ref, h_ref, gpre_ref, gpost_ref, wg_ref, wu_ref, wd_ref,
             dx_ref, da_ref, db_ref, dh_ref, dgpre_ref, dgpost_ref):
        @pl.when(pl.program_id(0) == 0)
        def _():
            dgpre_ref[...] = jnp.zeros_like(dgpre_ref)
            dgpost_ref[...] = jnp.zeros_like(dgpost_ref)

        dy = dxo_ref[...]
        h = h_ref[...]
        hn = h * _rs(h)
        r2 = _rs(h)
        dgpost_ref[...] += jnp.sum(0.5 * dy * hn, axis=0, keepdims=True)
        gdy = 0.5 * dy * gpost_ref[...]
        dh = r2 * (gdy - hn * jnp.mean(gdy * hn, axis=-1, keepdims=True))
        dhb = dh.astype(BF16)
        dh_ref[...] = dhb
        dhm = _dot_nt(dhb, wd_ref[...])
        av = a_ref[...].astype(F32)
        bv = b_ref[...].astype(F32)
        sg = _sigmoid(av)
        db = (dhm * (av * sg)).astype(BF16)
        da = (dhm * bv * _dsilu(av, sg)).astype(BF16)
        da_ref[...] = da
        db_ref[...] = db
        dn = _dot_nt(da, wg_ref[...]) + _dot_nt(db, wu_ref[...])
        xv = x_ref[...]
        r = _rs(xv)
        xn = xv * r
        dgpre_ref[...] += jnp.sum(dn * xn, axis=0, keepdims=True)
        gdn = dn * gpre_ref[...]
        dx_ref[...] = dy + r * (gdn - xn * jnp.mean(gdn * xn, axis=-1, keepdims=True))

    return _call(
        body, name=name, grid=(T // tm,),
        in_specs=[_rows(tm, D), _rows(tm, D), _rows(tm, FFP), _rows(tm, FFP), _rows(tm, D), _const((1, D)), _const((1, D)),
                  _resident((D, FFP)), _resident((D, FFP)), _resident((FFP, D))],
        out_specs=[_rows(tm, D), _rows(tm, FFP), _rows(tm, FFP), _rows(tm, D), _const((1, D)), _const((1, D))],
        out_shape=[jax.ShapeDtypeStruct((T, D), F32), jax.ShapeDtypeStruct((T, FFP), BF16), jax.ShapeDtypeStruct((T, FFP), BF16),
                   jax.ShapeDtypeStruct((T, D), BF16), jax.ShapeDtypeStruct((1, D), F32), jax.ShapeDtypeStruct((1, D), F32)],
        args=[dxo, x, a, b, h, gpre, gpost, wg, wu, wd], sem=("arbitrary",), duties=duties)


def _matmul_tn(a, b, name, tm=256):
    k, m = a.shape
    n = b.shape[1]
    assert m % tm == 0

    def body(a_ref, b_ref, o_ref, ob_ref):
        r = _dot_tn(a_ref[...], b_ref[...])
        o_ref[...] = r
        ob_ref[...] = r.astype(BF16)

    spec = pl.BlockSpec((tm, n), lambda i: (i, 0))
    return pl.pallas_call(
        body, name=name, grid=(m // tm,),
        in_specs=[pl.BlockSpec((k, tm), lambda i: (0, i)), _resident((k, n))],
        out_specs=[spec, spec],
        out_shape=[jax.ShapeDtypeStruct((m, n), F32), jax.ShapeDtypeStruct((m, n), BF16)],
        compiler_params=_params(("parallel",)),
    )(a, b)


def _matmul_tn_cols(a, b, name, tm=256):
    k, m = a.shape
    s = b.shape[1] // N_DEV
    assert m % tm == 0 and s % 128 == 0

    def body(a_ref, b_ref, o_ref, ob_ref):
        r = _dot_tn(a_ref[...], b_ref[...])
        for d in range(N_DEV):
            o_ref[d] = r[:, s * d:s * (d + 1)]
            ob_ref[d] = r[:, s * d:s * (d + 1)].astype(BF16)

    spec = pl.BlockSpec((N_DEV, tm, s), lambda i: (0, i, 0))
    return pl.pallas_call(
        body, name=name, grid=(m // tm,),
        in_specs=[pl.BlockSpec((k, tm), lambda i: (0, i)), _resident((k, N_DEV * s))],
        out_specs=[spec, spec],
        out_shape=[jax.ShapeDtypeStruct((N_DEV, m, s), F32), jax.ShapeDtypeStruct((N_DEV, m, s), BF16)],
        compiler_params=_params(("parallel",)),
    )(a, b)


def _dwout(attn, yn, dh2):
    rs = (QW + SSM_W) // N_DEV
    half = N_DEV // 2

    def body(at_ref, yn_ref, dh_ref, o_ref, ob_ref):
        i = pl.program_id(0)

        def put(r):
            o_ref[0] = r
            ob_ref[0] = r.astype(BF16)

        @pl.when(i < half)
        def _():
            put(_dot_tn(at_ref[...], dh_ref[...]))

        @pl.when(i >= half)
        def _():
            put(_dot_tn(yn_ref[...], dh_ref[...]))

    spec = pl.BlockSpec((1, rs, D), lambda i: (i, 0, 0))
    return pl.pallas_call(
        body, name="dwout", grid=(N_DEV,),
        in_specs=[pl.BlockSpec((T, rs), lambda i: (0, jnp.minimum(i, half - 1))),
                  pl.BlockSpec((T, rs), lambda i: (0, jnp.maximum(i - half, 0))), _resident((T, D))],
        out_specs=[spec, spec],
        out_shape=[jax.ShapeDtypeStruct((N_DEV, rs, D), F32), jax.ShapeDtypeStruct((N_DEV, rs, D), BF16)],
        compiler_params=_params(("parallel",)),
    )(attn, yn, dh2)


def _rope_swap(t, lo_half):
    return jnp.where(lo_half, pltpu.roll(t, 96, 1), pltpu.roll(t, 32, 1))


def _inproj_fwd(x1, gpre, win, cos, sin_s, ex, duties=()):
    tm = 256

    def body(x_ref, g_ref, w_ref, cos_ref, sin_ref, ex_ref, n_ref, q_ref, kx_ref, vx_ref, xbc_ref, z_ref, dt_ref):
        xv = x_ref[...]
        n = (xv * _rs(xv) * g_ref[...]).astype(BF16)
        n_ref[...] = n
        proj = _dot(n, w_ref[...])
        cs = cos_ref[...]
        sn = sin_ref[...]
        lo_half = (lax.broadcasted_iota(jnp.int32, (1, 128), 1) % HD) < (HD // 2)

        def rope(t):
            return t * cs + _rope_swap(t, lo_half) * sn

        for j in range(QW // 128):
            t = proj[:, 128 * j:128 * j + 128]
            q_ref[:, 128 * j:128 * j + 128] = (rope(t) * (HD ** -0.5)).astype(BF16)
        k = jnp.concatenate([rope(proj[:, QW + 128 * j:QW + 128 * j + 128]) for j in range(KVW // 128)], axis=1)
        v = proj[:, QW + KVW:QW + 2 * KVW]
        kx_ref[...] = _dot(k.astype(BF16), ex_ref[...]).astype(BF16)
        vx_ref[...] = _dot(v.astype(BF16), ex_ref[...]).astype(BF16)
        c0 = QW + 2 * KVW
        xbc_ref[...] = proj[:, c0:c0 + CONV_C]
        z_ref[...] = proj[:, c0 + CONV_C:c0 + CONV_C + SSM_W]
        dt_ref[...] = proj[:, c0 + CONV_C + SSM_W:IN_COLS]

    return _call(
        body, name="inproj_fwd", grid=(T // tm,),
        in_specs=[_rows(tm, D), _const((1, D)), _resident((D, IN_COLS)), _rows(tm, 128), _rows(tm, 128), _const((KVW, QW))],
        out_specs=[_rows(tm, D), _rows(tm, QW), _rows(tm, QW), _rows(tm, QW), _rows(tm, CONV_C), _rows(tm, SSM_W), _rows(tm, SSM_H)],
        out_shape=[jax.ShapeDtypeStruct((T, D), BF16), jax.ShapeDtypeStruct((T, QW), BF16), jax.ShapeDtypeStruct((T, QW), BF16),
                   jax.ShapeDtypeStruct((T, QW), BF16), jax.ShapeDtypeStruct((T, CONV_C), F32), jax.ShapeDtypeStruct((T, SSM_W), F32),
                   jax.ShapeDtypeStruct((T, SSM_H), F32)],
        args=[x1, gpre, win, cos, sin_s, ex], sem=("arbitrary",), duties=duties)


def _inproj_bwd(dres, dq, dkx, dvx, dxbc, dz, ddt, x1, gpre, win, cos, sin_s, exf):
    tm = 256

    def body(dres_ref, dq_ref, dkx_ref, dvx_ref, dxbc_ref, dz_ref, ddt_ref, x_ref, g_ref, w_ref, cos_ref, sin_ref, ex_ref,
             dx_ref, dp_ref, dg_ref):
        @pl.when(pl.program_id(0) == 0)
        def _():
            dg_ref[...] = jnp.zeros_like(dg_ref)

        cs = cos_ref[...]
        sn = sin_ref[...]
        lo_half = (lax.broadcasted_iota(jnp.int32, (1, 128), 1) % HD) < (HD // 2)

        def rope_t(t):
            return t * cs - _rope_swap(t, lo_half) * sn

        for j in range(QW // 128):
            dp_ref[:, 128 * j:128 * j + 128] = rope_t(dq_ref[:, 128 * j:128 * j + 128] * (HD ** -0.5)).astype(BF16)
        dk = _dot_nt_hi(dkx_ref[...], ex_ref[...])
        dv = _dot_nt_hi(dvx_ref[...], ex_ref[...])
        for j in range(KVW // 128):
            dp_ref[:, QW + 128 * j:QW + 128 * j + 128] = rope_t(dk[:, 128 * j:128 * j + 128]).astype(BF16)
        dp_ref[:, QW + KVW:QW + 2 * KVW] = dv.astype(BF16)
        c0 = QW + 2 * KVW
        dp_ref[:, c0:c0 + CONV_C] = dxbc_ref[...].astype(BF16)
        dp_ref[:, c0 + CONV_C:c0 + CONV_C + SSM_W] = dz_ref[...].astype(BF16)
        dp_ref[:, c0 + CONV_C + SSM_W:IN_COLS] = ddt_ref[...].astype(BF16)
        dn = _dot_nt(dp_ref[...], w_ref[...])
        xv = x_ref[...]
        r = _rs(xv)
        xn = xv * r
        dg_ref[...] += jnp.sum(dn * xn, axis=0, keepdims=True)
        gdn = dn * g_ref[...]
        dx_ref[...] = dres_ref[...] + r * (gdn - xn * jnp.mean(gdn * xn, axis=-1, keepdims=True))

    return pl.pallas_call(
        body, name="inproj_bwd", grid=(T // tm,),
        in_specs=[_rows(tm, D), _rows(tm, QW), _rows(tm, QW), _rows(tm, QW), _rows(tm, CONV_C), _rows(tm, SSM_W), _rows(tm, SSM_H),
                  _rows(tm, D), _const((1, D)), _resident((D, IN_COLS)), _rows(tm, 128), _rows(tm, 128), _const((KVW, QW))],
        out_specs=[_rows(tm, D), _rows(tm, IN_COLS), _const((1, D))],
        out_shape=[jax.ShapeDtypeStruct((T, D), F32), jax.ShapeDtypeStruct((T, IN_COLS), BF16), jax.ShapeDtypeStruct((1, D), F32)],
        compiler_params=_params(("arbitrary",)),
    )(dres, dq, dkx, dvx, dxbc, dz, ddt, x1, gpre, win, cos, sin_s, exf)


def _outproj_fwd(x1, attn, yn, wout, gpost):
    tm = 256

    def body(x_ref, at_ref, yn_ref, w_ref, g_ref, xo_ref, h_ref):
        h = _dot(at_ref[...], w_ref[0:QW, :]) + _dot(yn_ref[...], w_ref[QW:QW + SSM_W, :])
        h_ref[...] = h
        xo_ref[...] = x_ref[...] + h * _rs(h) * g_ref[...]

    return pl.pallas_call(
        body, name="outproj_fwd", grid=(T // tm,),
        in_specs=[_rows(tm, D), _rows(tm, QW), _rows(tm, SSM_W), _resident((QW + SSM_W, D)), _const((1, D))],
        out_specs=[_rows(tm, D), _rows(tm, D)],
        out_shape=[jax.ShapeDtypeStruct((T, D), F32), jax.ShapeDtypeStruct((T, D), F32)],
        compiler_params=_params(("parallel",)),
    )(x1, attn, yn, wout, gpost)


def _outproj_bwd(dx2, h2, gpost, wout, duties=()):
    tm = 256

    def body(dy_ref, h_ref, g_ref, w_ref, dh_ref, dm_ref, dg_ref):
        @pl.when(pl.program_id(0) == 0)
        def _():
            dg_ref[...] = jnp.zeros_like(dg_ref)

        dy = dy_ref[...]
        h = h_ref[...]
        r = _rs(h)
        hn = h * r
        dg_ref[...] += jnp.sum(dy * hn, axis=0, keepdims=True)
        gdy = dy * g_ref[...]
        dh = (r * (gdy - hn * jnp.mean(gdy * hn, axis=-1, keepdims=True))).astype(BF16)
        dh_ref[...] = dh
        dm_ref[...] = _dot_nt(dh, w_ref[...])

    return _call(
        body, name="outproj_bwd", grid=(T // tm,),
        in_specs=[_rows(tm, D), _rows(tm, D), _const((1, D)), _resident((QW + SSM_W, D))],
        out_specs=[_rows(tm, D), _rows(tm, QW + SSM_W), _const((1, D))],
        out_shape=[jax.ShapeDtypeStruct((T, D), BF16), jax.ShapeDtypeStruct((T, QW + SSM_W), F32), jax.ShapeDtypeStruct((1, D), F32)],
        args=[dx2, h2, gpost, wout], sem=("arbitrary",), duties=duties)


def _attn_bias():
    d = jnp.arange(AB)[:, None] - jnp.arange(T)[None, :] + (T - AB)
    cnt = jnp.zeros(d.shape, F32)
    for window, dil in DILATIONS:
        cnt = cnt + ((d >= 0) & (d % dil == 0) & (d <= window)).astype(F32)
    return jnp.where(cnt > 0, jnp.log(jnp.maximum(cnt, 1.0)), NEG)


G_PER = NQ // NKV
WK = G_PER * HD


def _attn_fwd(q, kx, vx, bias, duties=()):
    def body(q_ref, kx_ref, vx_ref, bias_ref, o_ref, lse_ref):
        lane = lax.broadcasted_iota(jnp.int32, (1, WK), 1)
        lse_ref[...] = jnp.zeros_like(lse_ref)
        for i in range(NAB):
            n = (i + 1) * AB
            rows = slice(i * AB, n)
            qi = q_ref[rows, :]
            kxi = kx_ref[0:n, :]
            vxi = vx_ref[0:n, :]
            bb = bias_ref[:, (NAB - 1 - i) * AB:]
            o_acc = jnp.zeros((AB, WK), F32)
            for g in range(G_PER):
                mg = (lane // HD) == g
                s = _dot_nt(jnp.where(mg, qi, jnp.zeros_like(qi)), kxi) + bb
                m = jnp.max(s, axis=1, keepdims=True)
                p = jnp.exp(s - m)
                l = jnp.sum(p, axis=1, keepdims=True)
                o_acc = jnp.where(mg, _dot(p.astype(BF16), vxi) / l, o_acc)
                lse_ref[rows, g:g + 1] = m + jnp.log(l)
            o_ref[rows, :] = o_acc.astype(BF16)

    col = lambda kv: (0, kv)
    return _call(
        body, name="attn_fwd", grid=(NKV,),
        in_specs=[pl.BlockSpec((T, WK), col), pl.BlockSpec((T, WK), col), pl.BlockSpec((T, WK), col), _const((AB, T))],
        out_specs=[pl.BlockSpec((T, WK), col), pl.BlockSpec((T, 128), col)],
        out_shape=[jax.ShapeDtypeStruct((T, QW), BF16), jax.ShapeDtypeStruct((T, NKV * 128), F32)],
        args=[q, kx, vx, bias], sem=("arbitrary",), duties=duties)


def _attn_bwd(q, kx, vx, o, dmix, lse, bias, duties=()):
    def body(q_ref, kx_ref, vx_ref, o_ref, do_ref, lse_ref, bias_ref, dq_ref, dkx_ref, dvx_ref):
        lane = lax.broadcasted_iota(jnp.int32, (1, WK), 1)
        dkx_ref[...] = jnp.zeros_like(dkx_ref)
        dvx_ref[...] = jnp.zeros_like(dvx_ref)
        for i in range(NAB):
            n = (i + 1) * AB
            rows = slice(i * AB, n)
            qi = q_ref[rows, :]
            dof = do_ref[rows, :]
            doi = dof.astype(BF16)
            prod = dof * o_ref[rows, :].astype(F32)
            kxi = kx_ref[0:n, :]
            vxi = vx_ref[0:n, :]
            bb = bias_ref[:, (NAB - 1 - i) * AB:]
            dq_acc = jnp.zeros((AB, WK), F32)
            for g in range(G_PER):
                mg = (lane // HD) == g
                qm = jnp.where(mg, qi, jnp.zeros_like(qi))
                dom = jnp.where(mg, doi, jnp.zeros_like(doi))
                delta = jnp.sum(jnp.where(mg, prod, 0.0), axis=1, keepdims=True)
                p = jnp.exp(_dot_nt(qm, kxi) + bb - lse_ref[rows, g:g + 1])
                ds = (p * (_dot_nt(dom, vxi) - delta)).astype(BF16)
                dvx_ref[0:n, :] += _dot_tn(p.astype(BF16), dom)
                dkx_ref[0:n, :] += _dot_tn(ds, qm)
                dq_acc = jnp.where(mg, _dot(ds, kxi), dq_acc)
            dq_ref[rows, :] = dq_acc

    col = lambda kv: (0, kv)
    return _call(
        body, name="attn_bwd", grid=(NKV,),
        in_specs=[pl.BlockSpec((T, WK), col), pl.BlockSpec((T, WK), col), pl.BlockSpec((T, WK), col), pl.BlockSpec((T, WK), col),
                  pl.BlockSpec((T, WK), col), pl.BlockSpec((T, 128), col), _const((AB, T))],
        out_specs=[pl.BlockSpec((T, WK), col), pl.BlockSpec((T, WK), col), pl.BlockSpec((T, WK), col)],
        out_shape=[jax.ShapeDtypeStruct((T, QW), F32)] * 3,
        args=[q, kx, vx, o, dmix, lse, bias], sem=("arbitrary",), duties=duties)


def _softplus(x):
    return jnp.maximum(x, 0.0) + jnp.log1p(jnp.exp(-jnp.abs(x)))


def _conv_taps(u, halo):
    zext = jnp.concatenate([halo, u], axis=0)
    return [pltpu.roll(zext, m, 0)[8:] for m in (1, 2, 3)]


def _ssd_chunk_common(u, halo, dtr, cw_ref, cb_ref, dtb_ref, alx_ref, e_ref):
    sh1, sh2, sh3 = _conv_taps(u, halo)
    xc = cb_ref[...] + cw_ref[3:4, :] * u + cw_ref[2:3, :] * sh1 + cw_ref[1:2, :] * sh2 + cw_ref[0:1, :] * sh3
    sg = _sigmoid(xc)
    act = xc * sg
    pre_x = _dot_hi(dtr + dtb_ref[...], e_ref[...])
    dt_x = _softplus(pre_x)
    a_x = -jnp.exp(alx_ref[...])
    ri = lax.broadcasted_iota(jnp.int32, (L, L), 0)
    ci = lax.broadcasted_iota(jnp.int32, (L, L), 1)
    tri = ri >= ci
    acs_x = _dot_hi(tri.astype(F32), dt_x * a_x)
    return dict(sh=(sh1, sh2, sh3), xc=xc, sg=sg, act=act, pre_x=pre_x, dt_x=dt_x, a_x=a_x, tri=tri, acs_x=acs_x)


def _decay(acs_x, acs_t, h, tri):
    col = acs_x[:, HD * h:HD * h + 1]
    row = acs_t[HD * h:HD * h + 1, :]
    return jnp.exp(jnp.where(tri, col - row, NEG))


def _ssd_fwd(xbc, z, dtr, convw, convb, dtb, alx, dskx, ssmn, e, duties=()):
    def body(u_ref, z_ref, dtr_ref, cw_ref, cb_ref, dtb_ref, alx_ref, dsk_ref, sn_ref, e_ref,
             yn_ref, y_ref, hs_ref, halo, hst):
        @pl.when(pl.program_id(0) == 0)
        def _():
            halo[...] = jnp.zeros_like(halo)
            hst[...] = jnp.zeros_like(hst)

        u = u_ref[...]
        cm = _ssd_chunk_common(u, halo[...], dtr_ref[...], cw_ref, cb_ref, dtb_ref, alx_ref, e_ref)
        halo[...] = u[L - 8:, :]
        act, dt_x, acs_x, tri = cm["act"], cm["dt_x"], cm["acs_x"], cm["tri"]
        xs = act[:, :SSM_W]
        acs_l = acs_x[L - 1:L, :]
        lam_x = jnp.exp(acs_x)
        w_x = jnp.exp(acs_l - acs_x)
        gam_x = jnp.exp(acs_l)
        acs_t = acs_x.T
        xd = xs * dt_x
        xb = xd.astype(BF16)
        xw = (xd * w_x).astype(BF16)
        lo = lax.broadcasted_iota(jnp.int32, (1, 128), 1) < HD
        hs_ref[0] = hst[...]
        pieces = []
        for grp in range(2):
            bb = act[:, SSM_W + SSM_N * grp:SSM_W + SSM_N * (grp + 1)].astype(BF16)
            cb_ = act[:, SSM_W + 2 * SSM_N + SSM_N * grp:SSM_W + 2 * SSM_N + SSM_N * (grp + 1)].astype(BF16)
            cbm = _dot_nt(cb_, bb)
            for jj in range(4):
                j = 4 * grp + jj
                sl = slice(128 * j, 128 * j + 128)
                m0 = (cbm * _decay(acs_x, acs_t, 2 * j, tri)).astype(BF16)
                m1 = (cbm * _decay(acs_x, acs_t, 2 * j + 1, tri)).astype(BF16)
                x2 = xb[:, sl]
                ydiag = jnp.where(lo, _dot(m0, x2), _dot(m1, x2))
                hprev = hst[j]
                yoff = lam_x[:, sl] * _dot(cb_, hprev.astype(BF16))
                pieces.append(ydiag + yoff)
                hst[j] = gam_x[:, sl] * hprev + _dot_tn(bb, xw[:, sl])
        y = jnp.concatenate(pieces, axis=1) + dsk_ref[...] * xs
        y_ref[...] = y
        zv = z_ref[...]
        yz = y * (zv * _sigmoid(zv))
        half = SSM_W // 2
        yn = jnp.concatenate([yz[:, :half] * _rs(yz[:, :half]), yz[:, half:] * _rs(yz[:, half:])], axis=1)
        yn_ref[...] = (yn * sn_ref[...]).astype(BF16)

    return _call(
        body, name="ssd_fwd", grid=(NCH,),
        in_specs=[_rows(L, CONV_C), _rows(L, SSM_W), _rows(L, SSM_H), _const((4, CONV_C)), _const((1, CONV_C)), _const((1, SSM_H)),
                  _const((1, SSM_W)), _const((1, SSM_W)), _const((1, SSM_W)), _const((SSM_H, SSM_W))],
        out_specs=[_rows(L, SSM_W), _rows(L, SSM_W), pl.BlockSpec((1, 8, SSM_N, 128), lambda c: (c, 0, 0, 0))],
        out_shape=[jax.ShapeDtypeStruct((T, SSM_W), BF16), jax.ShapeDtypeStruct((T, SSM_W), F32),
                   jax.ShapeDtypeStruct((NCH, 8, SSM_N, 128), F32)],
        scratch=[pltpu.VMEM((8, CONV_C), F32), pltpu.VMEM((8, SSM_N, 128), F32)],
        args=[xbc, z, dtr, convw, convb, dtb, alx, dskx, ssmn, e], sem=("arbitrary",), duties=duties)


def _ssd_bwd(dmix, xbc, z, dtr, y, hs, convw, convb, dtb, alx, dskx, ssmn, e, e1, duties=()):
    rev = lambda i: (NCH - 1 - i, 0)

    def body(dyn_ref, u_ref, uh_ref, z_ref, dtr_ref, y_ref, hs_ref, cw_ref, cb_ref, dtb_ref, alx_ref, dsk_ref, sn_ref, e_ref, e1_ref,
             dxbc_ref, dz_ref, ddt_ref, dcw_ref, dcb_ref, dsn_ref, dpar_ref, dh, duh, colbuf):
        step = pl.program_id(0)
        c = NCH - 1 - step

        @pl.when(step == 0)
        def _():
            for r in (dh, duh, dcw_ref, dcb_ref, dsn_ref, dpar_ref):
                r[...] = jnp.zeros_like(r)

        u = u_ref[...]
        halo = jnp.where(c > 0, uh_ref[...], 0.0)
        cm = _ssd_chunk_common(u, halo, dtr_ref[...], cw_ref, cb_ref, dtb_ref, alx_ref, e_ref)
        sh1, sh2, sh3 = cm["sh"]
        xc, sg, act, pre_x, dt_x, a_x, tri, acs_x = (cm[k] for k in ("xc", "sg", "act", "pre_x", "dt_x", "a_x", "tri", "acs_x"))
        xs = act[:, :SSM_W]
        acs_l = acs_x[L - 1:L, :]
        lam_x = jnp.exp(acs_x)
        w_x = jnp.exp(acs_l - acs_x)
        gam_x = jnp.exp(acs_l)
        acs_t = acs_x.T
        xd = xs * dt_x
        xb = xd.astype(BF16)
        xdw = xd * w_x
        xw = xdw.astype(BF16)
        lo = lax.broadcasted_iota(jnp.int32, (1, 128), 1) < HD
        row8 = lax.broadcasted_iota(jnp.int32, (8, 1), 0)

        dyn = dyn_ref[...]
        yv = y_ref[...]
        zv = z_ref[...]
        sz = _sigmoid(zv)
        siluz = zv * sz
        yz = yv * siluz
        half = SSM_W // 2
        gy = dyn * sn_ref[...]
        dyz_parts, yzn_parts = [], []
        for hf in range(2):
            part = yz[:, hf * half:(hf + 1) * half]
            r = _rs(part)
            pn = part * r
            gp = gy[:, hf * half:(hf + 1) * half]
            dyz_parts.append(r * (gp - pn * jnp.mean(gp * pn, axis=-1, keepdims=True)))
            yzn_parts.append(pn)
        dyz = jnp.concatenate(dyz_parts, axis=1)
        dsn_ref[...] += jnp.sum(dyn * jnp.concatenate(yzn_parts, axis=1), axis=0, keepdims=True)
        dy = dyz * siluz
        dz_ref[...] = dyz * yv * _dsilu(zv, sz)

        colbuf[...] = jnp.zeros_like(colbuf)
        dx_pieces, dacs_pieces, dacsl_pieces, db_pieces, dc_pieces = [], [], [], [], []
        for grp in range(2):
            bb = act[:, SSM_W + SSM_N * grp:SSM_W + SSM_N * (grp + 1)].astype(BF16)
            cb_ = act[:, SSM_W + 2 * SSM_N + SSM_N * grp:SSM_W + 2 * SSM_N + SSM_N * (grp + 1)].astype(BF16)
            cbm = _dot_nt(cb_, bb)
            dcbm = jnp.zeros((L, L), F32)
            dc_g = jnp.zeros((L, SSM_N), F32)
            db_g = jnp.zeros((L, SSM_N), F32)
            for jj in range(4):
                j = 4 * grp + jj
                sl = slice(128 * j, 128 * j + 128)
                dy2 = dy[:, sl]
                dy2b = dy2.astype(BF16)
                d0 = _decay(acs_x, acs_t, 2 * j, tri)
                d1 = _decay(acs_x, acs_t, 2 * j + 1, tri)
                m0 = cbm * d0
                m1 = cbm * d1
                x2 = xb[:, sl]
                hprev = hs_ref[0, j]
                hprevb = hprev.astype(BF16)
                dhn = dh[j]
                dhnb = dhn.astype(BF16)
                g2 = _dot(bb, dhnb)
                dx_pieces.append(jnp.where(lo, _dot_tn(m0.astype(BF16), dy2b), _dot_tn(m1.astype(BF16), dy2b)) + w_x[:, sl] * g2)
                zero = jnp.zeros_like(dy2b)
                dm0 = _dot_nt(jnp.where(lo, dy2b, zero), x2)
                dm1 = _dot_nt(jnp.where(lo, zero, dy2b), x2)
                dcbm = dcbm + dm0 * d0 + dm1 * d1
                e0 = dm0 * m0
                e1v = dm1 * m1
                colbuf[:, 2 * j:2 * j + 1] = jnp.sum(e0, axis=1, keepdims=True) - jnp.sum(e0.T, axis=1, keepdims=True)
                colbuf[:, 2 * j + 1:2 * j + 2] = jnp.sum(e1v, axis=1, keepdims=True) - jnp.sum(e1v.T, axis=1, keepdims=True)
                yoff = lam_x[:, sl] * _dot(cb_, hprevb)
                gxw = g2 * xdw[:, sl]
                dacs_pieces.append(dy2 * yoff - gxw)
                dacsl_pieces.append(jnp.sum(gxw, axis=0, keepdims=True) + gam_x[:, sl] * jnp.sum(dhn * hprev, axis=0, keepdims=True))
                dyl = (dy2 * lam_x[:, sl]).astype(BF16)
                dc_g = dc_g + _dot_nt(dyl, hprevb)
                db_g = db_g + _dot_nt(xw[:, sl], dhnb)
                dh[j] = gam_x[:, sl] * dhn + _dot_tn(cb_, dyl)
            dcbb = dcbm.astype(BF16)
            dc_pieces.append(dc_g + _dot(dcbb, bb))
            db_pieces.append(db_g + _dot_tn(dcbb, cb_))

        dxd = jnp.concatenate(dx_pieces, axis=1)
        rowi = lax.broadcasted_iota(jnp.int32, (L, 1), 0)
        dacs_x = (jnp.concatenate(dacs_pieces, axis=1) + _dot_hi(colbuf[...], e1_ref[...])
                  + jnp.where(rowi == L - 1, jnp.concatenate(dacsl_pieces, axis=1), 0.0))
        upper = lax.broadcasted_iota(jnp.int32, (L, L), 0) <= lax.broadcasted_iota(jnp.int32, (L, L), 1)
        dadt_x = _dot_hi(upper.astype(F32), dacs_x)
        ddt_x = dxd * xs + dadt_x * a_x
        ddtr = _dot_nt_hi(ddt_x * _sigmoid(pre_x), e_ref[...])
        ddt_ref[...] = ddtr
        dalx = jnp.sum(dadt_x * dt_x, axis=0, keepdims=True) * a_x
        ddskx = jnp.sum(dy * xs, axis=0, keepdims=True)
        par_x = jnp.where(row8 == 1, dalx, 0.0) + jnp.where(row8 == 2, ddskx, 0.0)
        dpar_ref[...] += _dot_nt_hi(par_x, e_ref[...]) + jnp.where(row8 == 0, jnp.sum(ddtr, axis=0, keepdims=True), 0.0)

        dxs = dxd * dt_x + dsk_ref[...] * dy
        dact = jnp.concatenate([dxs] + db_pieces + dc_pieces, axis=1)
        du = dact * _dsilu(xc, sg)
        dcb_ref[...] += jnp.sum(du, axis=0, keepdims=True)
        taps = (sh3, sh2, sh1, u)
        dcw = jnp.zeros((8, CONV_C), F32)
        for k in range(4):
            dcw = dcw + jnp.where(row8 == k, jnp.sum(du * taps[k], axis=0, keepdims=True), 0.0)
        dcw_ref[...] += dcw
        zext = jnp.concatenate([du, duh[...]], axis=0)
        f1, f2, f3 = (pltpu.roll(zext, L + 8 - m, 0)[:L] for m in (1, 2, 3))
        dxbc_ref[...] = cw_ref[3:4, :] * du + cw_ref[2:3, :] * f1 + cw_ref[1:2, :] * f2 + cw_ref[0:1, :] * f3
        duh[...] = du[:8, :]

    return _call(
        body, name="ssd_bwd", grid=(NCH,),
        in_specs=[pl.BlockSpec((L, SSM_W), lambda i: (NCH - 1 - i, 1)), pl.BlockSpec((L, CONV_C), rev),
                  pl.BlockSpec((8, CONV_C), lambda i: (jnp.maximum((NCH - 1 - i) * (L // 8) - 1, 0), 0)),
                  pl.BlockSpec((L, SSM_W), rev), pl.BlockSpec((L, SSM_H), rev), pl.BlockSpec((L, SSM_W), rev),
                  pl.BlockSpec((1, 8, SSM_N, 128), lambda i: (NCH - 1 - i, 0, 0, 0)),
                  _const((4, CONV_C)), _const((1, CONV_C)), _const((1, SSM_H)), _const((1, SSM_W)), _const((1, SSM_W)), _const((1, SSM_W)),
                  _const((SSM_H, SSM_W)), _const((128, SSM_W))],
        out_specs=[pl.BlockSpec((L, CONV_C), rev), pl.BlockSpec((L, SSM_W), rev), pl.BlockSpec((L, SSM_H), rev),
                   _const((8, CONV_C)), _const((1, CONV_C)), _const((1, SSM_W)), _const((8, SSM_H))],
        out_shape=[jax.ShapeDtypeStruct((T, CONV_C), F32), jax.ShapeDtypeStruct((T, SSM_W), F32), jax.ShapeDtypeStruct((T, SSM_H), F32),
                   jax.ShapeDtypeStruct((8, CONV_C), F32), jax.ShapeDtypeStruct((1, CONV_C), F32), jax.ShapeDtypeStruct((1, SSM_W), F32),
                   jax.ShapeDtypeStruct((8, SSM_H), F32)],
        scratch=[pltpu.VMEM((8, SSM_N, 128), F32), pltpu.VMEM((8, CONV_C), F32), pltpu.VMEM((L, 128), F32)],
        args=[dmix, xbc, xbc, z, dtr, y, hs, convw, convb, dtb, alx, dskx, ssmn, e, e1], sem=("arbitrary",), duties=duties)


def _loss_head(x3, target):
    tm = 512

    def body(x_ref, t_ref, dy_ref, ss_ref):
        @pl.when(pl.program_id(0) == 0)
        def _():
            ss_ref[...] = jnp.zeros_like(ss_ref)

        err = x_ref[...] - t_ref[...]
        dy_ref[...] = err * (1.0 / D)
        ss_ref[...] += jnp.sum(jnp.sum(err * err, axis=1, keepdims=True), axis=0, keepdims=True)

    return pl.pallas_call(
        body, name="loss_head", grid=(T // tm,),
        in_specs=[_rows(tm, D), _rows(tm, D)],
        out_specs=[_rows(tm, D), _const((1, 128))],
        out_shape=[jax.ShapeDtypeStruct((T, D), F32), jax.ShapeDtypeStruct((1, 128), F32)],
        compiler_params=_params(("arbitrary",)),
    )(x3, target)


def _adam_math(w, g, m, v):
    m = ADAM_B1 * m + (1.0 - ADAM_B1) * g
    v = ADAM_B2 * v + (1.0 - ADAM_B2) * (g * g)
    m_hat = m / (1.0 - ADAM_B1 ** ADAM_STEP)
    v_hat = v / (1.0 - ADAM_B2 ** ADAM_STEP)
    delta = -ADAM_LR * (m_hat / (jnp.sqrt(v_hat) + ADAM_EPS) + ADAM_WD * w)
    return delta, m, v


def _adamw(w, m, v, parts, name):
    rows, cols = w.shape
    tr = rows if rows <= 512 else 256
    assert rows % tr == 0
    n_parts = len(parts)

    def body(*refs):
        w_ref, m_ref, v_ref = refs[:3]
        p_refs = refs[3:3 + n_parts]
        g_ref, d_ref, nm_ref, nv_ref = refs[3 + n_parts:]
        g = p_refs[0][...].astype(F32)
        for p in p_refs[1:]:
            g = g + p[...].astype(F32)
        delta, nm, nv = _adam_math(w_ref[...], g, m_ref[...], v_ref[...])
        g_ref[...] = g
        d_ref[...] = delta
        nm_ref[...] = nm
        nv_ref[...] = nv

    spec = pl.BlockSpec((tr, cols), lambda i: (i, 0))
    return pl.pallas_call(
        body, name=name, grid=(rows // tr,),
        in_specs=[spec] * (3 + n_parts), out_specs=[spec] * 4,
        out_shape=[jax.ShapeDtypeStruct((rows, cols), F32)] * 4,
        compiler_params=_params(("parallel",)),
    )(w, m, v, *parts)


def _row_tile(rows):
    return rows if rows <= 512 else 256


def _adamw_sharded(w, m, v, chip_sum, from_chips, chip_ids, name):
    rows, cols = w.shape
    prow, pcol = chip_sum.shape[1:]
    tr = _row_tile(rows)
    ptr = prow if tr == rows else tr
    assert rows % tr == 0 and ptr >= tr and pcol >= cols

    def body(ids_ref, w_ref, m_ref, v_ref, s_ref, r1_ref, r2_ref, r3_ref, g_ref, d_ref, nm_ref, nv_ref):
        g = s_ref[0]
        for r in (r1_ref, r2_ref, r3_ref):
            g = g + r[0].astype(F32)
        g = g[:tr, :cols]
        delta, nm, nv = _adam_math(w_ref[...], g, m_ref[...], v_ref[...])
        g_ref[...] = g
        d_ref[...] = delta
        nm_ref[...] = nm
        nv_ref[...] = nv

    spec = pl.BlockSpec((tr, cols), lambda i, ids: (i, 0))
    part = lambda k: pl.BlockSpec((1, ptr, pcol), lambda i, ids: (ids[k], i, 0))
    return pl.pallas_call(
        body, name=name,
        grid_spec=pltpu.PrefetchScalarGridSpec(
            num_scalar_prefetch=1, grid=(rows // tr,),
            in_specs=[spec, spec, spec, part(0), part(1), part(2), part(3)], out_specs=[spec] * 4),
        out_shape=[jax.ShapeDtypeStruct((rows, cols), F32)] * 4,
        compiler_params=_params(("parallel",)),
    )(chip_ids, w, m, v, chip_sum, from_chips, from_chips, from_chips)


def _chip_sum(g32, recv, core, name):
    rows, cols = g32.shape[1:]
    tr = _row_tile(rows)
    assert rows % tr == 0

    def body(core_ref, a_ref, b_ref, s_ref, sb_ref):
        s = a_ref[...] + b_ref[...].astype(F32)
        s_ref[...] = s
        sb_ref[...] = s.astype(BF16)

    by_chip = pl.BlockSpec((1, tr, cols), lambda k, i, core_ref: (k, i, 0))
    return pl.pallas_call(
        body, name=name,
        grid_spec=pltpu.PrefetchScalarGridSpec(
            num_scalar_prefetch=1, grid=(N_DEV // 2, rows // tr),
            in_specs=[pl.BlockSpec((1, tr, cols), lambda k, i, core_ref: (2 * k + core_ref[0], i, 0)), by_chip],
            out_specs=[by_chip, by_chip]),
        out_shape=[jax.ShapeDtypeStruct((N_DEV // 2, rows, cols), F32), jax.ShapeDtypeStruct((N_DEV // 2, rows, cols), BF16)],
        compiler_params=_params(("parallel", "parallel")),
    )(core, g32, recv)


def _all_reduce_small(v):
    rows = v.shape[0]

    def body(v_ref, out_ref, gath, send_sems, recv_sems):
        x, y, c = _place()
        me, sibling = (x, y, c), (x, y, 1 - c)
        chips = [(1 - x, y), (x, 1 - y), (1 - x, 1 - y)]

        def blk(px, py, pc):
            return gath.at[pl.ds((4 * px + 2 * py + pc) * rows, rows), :]

        def copy(k, block, to, src=None):
            return pltpu.make_async_remote_copy(src_ref=blk(*block) if src is None else src, dst_ref=blk(*block),
                                                send_sem=send_sems.at[k], recv_sem=recv_sems.at[k], device_id=to, device_id_type=MESH)

        gath[pl.ds((4 * x + 2 * y + c) * rows, rows), :] = v_ref[...]
        first = [copy(0, me, sibling, src=v_ref)] + [copy(1 + j, me, (*chip, c), src=v_ref) for j, chip in enumerate(chips)]
        for cp in first:
            cp.start()
        passed = [copy(4 + j, (*chip, c), sibling) for j, chip in enumerate(chips)]
        for j, chip in enumerate(chips):
            copy(1 + j, (*chip, c), me).wait_recv()
            passed[j].start()
        copy(0, sibling, me).wait_recv()
        for j, chip in enumerate(chips):
            copy(4 + j, (*chip, 1 - c), me).wait_recv()
        for cp in first + passed:
            cp.wait_send()
        acc = gath[0:rows, :]
        for d in range(1, N_DEV):
            acc = acc + gath[d * rows:(d + 1) * rows, :]
        out_ref[...] = acc

    vm = pl.BlockSpec(memory_space=pltpu.VMEM)
    return pl.pallas_call(
        body, name="all_reduce_small",
        in_specs=[vm], out_specs=vm,
        out_shape=jax.ShapeDtypeStruct(v.shape, F32),
        scratch_shapes=[pltpu.VMEM((N_DEV * rows, 128), F32), pltpu.SemaphoreType.DMA((7,)), pltpu.SemaphoreType.DMA((7,))],
    )(v)


def _rope_tables(positions):
    inv_freq = ROPE_THETA ** (-jnp.arange(0, HD, 2, dtype=F32) / HD)
    ang = positions.reshape(T).astype(F32)[:, None] * inv_freq
    ang = jnp.concatenate([ang, ang, ang, ang], axis=-1)
    lo_half = (jnp.arange(128) % HD) < (HD // 2)
    return jnp.cos(ang), jnp.where(lo_half, -jnp.sin(ang), jnp.sin(ang))


def _selectors():
    lane = jnp.arange(QW)
    e = (lane[None, :] // HD == jnp.arange(SSM_H)[:, None]).astype(F32)
    e1 = ((lane[None, :] == HD * jnp.arange(128)[:, None]) & (jnp.arange(128)[:, None] < SSM_H)).astype(F32)
    src = jnp.arange(KVW)
    ex = ((lane[None, :] // (HD * (NQ // NKV)) == src[:, None] // HD) & (lane[None, :] % HD == src[:, None] % HD)).astype(F32)
    return e, e1, ex


WEIGHTS = ['ffn1_pre_norm', 'ffn1_w_gate', 'ffn1_w_up', 'ffn1_w_down', 'ffn1_post_norm', 'mix_pre_norm', 'w_in', 'conv_w', 'conv_b',
           'dt_bias', 'a_log', 'd_skip', 'ssm_norm', 'w_out', 'mix_post_norm', 'ffn2_pre_norm', 'ffn2_w_gate', 'ffn2_w_up',
           'ffn2_w_down', 'ffn2_post_norm']
COL_SHARDED = ['ffn1_w_gate', 'ffn1_w_up', 'ffn2_w_gate', 'ffn2_w_up', 'w_in']
ROW_SHARDED = ['ffn1_w_down', 'ffn2_w_down', 'w_out']
BIG = COL_SHARDED + ROW_SHARDED
SMALL = ['ffn1_pre_norm', 'ffn1_post_norm', 'mix_pre_norm', 'conv_b', 'dt_bias', 'a_log', 'd_skip', 'ssm_norm', 'mix_post_norm',
         'ffn2_pre_norm', 'ffn2_post_norm']
FFN1 = ['ffn1_w_gate', 'ffn1_w_up', 'ffn1_w_down']
FFN2 = ['ffn2_w_gate', 'ffn2_w_up', 'ffn2_w_down']


def _wire_block(name, a):
    if name in COL_SHARDED[:4]:
        return jnp.pad(a.astype(BF16), ((0, 0), (0, FSH - FSR)))
    if name in ROW_SHARDED[:2]:
        return jnp.pad(a.astype(BF16), ((0, FSH - FSR), (0, 0)))
    return a if name == "conv_w" else a.astype(BF16)


def _whole_from_gathered(name, a):
    if name in COL_SHARDED[:4]:
        return a
    if name in ("w_in", "conv_w"):
        return jnp.transpose(a, (1, 0, 2)).reshape(a.shape[1], -1)
    return a.reshape(-1, D)


def _step(x, positions, target, small, blocks=None, whole=None, core=None):
    dist = blocks is not None
    w = dict(small)
    if whole:
        w.update(whole)

    def gather(names):
        return [_gather_duty([_wire_block(n, blocks[n]) for n in names], [n in COL_SHARDED[:4] for n in names])] if dist else []

    def put(names, results):
        if dist:
            for n, r in zip(names, results[0]):
                w[n] = _whole_from_gathered(n, r)

    def row_shards(pair):
        return tuple(a.reshape(N_DEV, -1, a.shape[1]) for a in pair)

    g, sums, red = {}, {}, {}

    def swap(names):
        return [_swap_duty([g[n][1] for n in names])] if dist else []

    def chip_sums(names, from_sibling):
        if dist:
            for n, recv in zip(names, from_sibling):
                sums[n] = _chip_sum(g[n][0], recv, core, "chip_sum_" + n)

    def exchange(names):
        return [_exchange_duty([sums[n][1] for n in names])] if dist else []

    def reduced(names, from_chips):
        if dist:
            for n, recv in zip(names, from_chips):
                red[n] = (sums[n][0], recv)

    cos, sin_s = _rope_tables(positions)
    e, e1, exf = _selectors()
    bias = _attn_bias()
    alx = jnp.repeat(w["a_log"], HD, axis=1)
    dskx = jnp.repeat(w["d_skip"], HD, axis=1)

    if dist:
        put(FFN1, _comm_only(gather(FFN1), "gather_ffn1"))
    (x1, n1, a1, b1, hm1, h1), got = _ffn_fwd(x, w["ffn1_pre_norm"], w["ffn1_w_gate"], w["ffn1_w_up"], w["ffn1_w_down"],
                                              w["ffn1_post_norm"], "ffn1_fwd", gather(["w_in", "conv_w"]))
    put(["w_in", "conv_w"], got)
    (n2, q, kx, vx, xbc, z, dtr), got = _inproj_fwd(x1, w["mix_pre_norm"], w["w_in"], cos, sin_s, exf.astype(BF16), gather(["w_out"]))
    put(["w_out"], got)
    (attn, lse), got = _attn_fwd(q, kx, vx, bias, gather(FFN2[:2]))
    put(FFN2[:2], got)
    (yn, y, hs), got = _ssd_fwd(xbc, z, dtr, w["conv_w"], w["conv_b"], w["dt_bias"], alx, dskx, w["ssm_norm"], e, gather(FFN2[2:]))
    put(FFN2[2:], got)
    x2, h2 = _outproj_fwd(x1, attn, yn, w["w_out"], w["mix_post_norm"])
    (x3, n3, a3, b3, hm3, h3), _ = _ffn_fwd(x2, w["ffn2_pre_norm"], w["ffn2_w_gate"], w["ffn2_w_up"], w["ffn2_w_down"],
                                            w["ffn2_post_norm"], "ffn2_fwd")
    dx3, ss = _loss_head(x3, target)

    (dx2, da3, db3, dh3, g["ffn2_pre_norm"], g["ffn2_post_norm"]), _ = _ffn_bwd(
        dx3, x2, a3, b3, h3, w["ffn2_pre_norm"], w["ffn2_post_norm"], w["ffn2_w_gate"], w["ffn2_w_up"], w["ffn2_w_down"], "ffn2_bwd")
    g["ffn2_w_down"] = row_shards(_matmul_tn(hm3, dh3, "ffn2_dwd", tm=FSH))
    g["ffn2_w_gate"] = _matmul_tn_cols(n3, da3, "ffn2_dwg")
    g["ffn2_w_up"] = _matmul_tn_cols(n3, db3, "ffn2_dwu")

    (dh2, dmix, g["mix_post_norm"]), got = _outproj_bwd(dx2, h2, w["mix_post_norm"], w["w_out"], swap(FFN2))
    chip_sums(FFN2, got[0] if dist else None)
    g["w_out"] = _dwout(attn, yn, dh2)
    (dq, dkx, dvx), got = _attn_bwd(q, kx, vx, attn, dmix, lse, bias, exchange(FFN2) + swap(["w_out"]))
    if dist:
        reduced(FFN2, got[0])
        chip_sums(["w_out"], got[1])
    (dxbc, dz, ddt, dcw, g["conv_b"], g["ssm_norm"], dpar), got = _ssd_bwd(
        dmix, xbc, z, dtr, y, hs, w["conv_w"], w["conv_b"], w["dt_bias"], alx, dskx, w["ssm_norm"], e, e1, exchange(["w_out"]))
    reduced(["w_out"], got[0] if dist else None)
    g["conv_w"] = dcw[0:4]
    g["dt_bias"], g["a_log"], g["d_skip"] = dpar[0:1], dpar[1:2], dpar[2:3]
    dx1, dproj, g["mix_pre_norm"] = _inproj_bwd(dx2, dq, dkx, dvx, dxbc, dz, ddt, x1, w["mix_pre_norm"], w["w_in"], cos, sin_s, exf)
    g["w_in"] = tuple(jnp.transpose(a.reshape(D, N_DEV, -1), (1, 0, 2)) for a in _matmul_tn(n2, dproj, "dwin"))
    if dist:
        chip_sums(["w_in"], _comm_only(swap(["w_in"]), "swap_w_in")[0])

    (dx0, da1, db1, dh1, g["ffn1_pre_norm"], g["ffn1_post_norm"]), got = _ffn_bwd(
        dx1, x, a1, b1, h1, w["ffn1_pre_norm"], w["ffn1_post_norm"], w["ffn1_w_gate"], w["ffn1_w_up"], w["ffn1_w_down"], "ffn1_bwd",
        exchange(["w_in"]))
    reduced(["w_in"], got[0] if dist else None)
    g["ffn1_w_down"] = row_shards(_matmul_tn(hm1, dh1, "ffn1_dwd", tm=FSH))
    g["ffn1_w_gate"] = _matmul_tn_cols(n1, da1, "ffn1_dwg")
    g["ffn1_w_up"] = _matmul_tn_cols(n1, db1, "ffn1_dwu")
    if dist:
        chip_sums(FFN1, _comm_only(swap(FFN1), "swap_ffn1")[0])
        reduced(FFN1, _comm_only(exchange(FFN1), "exchange_ffn1")[0])
    return ss, dx0, g, red


def kernel(x, positions, ffn1_pre_norm, ffn1_w_gate, ffn1_w_up, ffn1_w_down, ffn1_post_norm, mix_pre_norm, w_in, conv_w, conv_b, dt_bias, a_log, d_skip, ssm_norm, w_out, mix_post_norm, ffn2_pre_norm, ffn2_w_gate, ffn2_w_up, ffn2_w_down, ffn2_post_norm, loss_target, m_ffn1_pre_norm, m_ffn1_w_gate, m_ffn1_w_up, m_ffn1_w_down, m_ffn1_post_norm, m_mix_pre_norm, m_w_in, m_conv_w, m_conv_b, m_dt_bias, m_a_log, m_d_skip, m_ssm_norm, m_w_out, m_mix_post_norm, m_ffn2_pre_norm, m_ffn2_w_gate, m_ffn2_w_up, m_ffn2_w_down, m_ffn2_post_norm, v_ffn1_pre_norm, v_ffn1_w_gate, v_ffn1_w_up, v_ffn1_w_down, v_ffn1_post_norm, v_mix_pre_norm, v_w_in, v_conv_w, v_conv_b, v_dt_bias, v_a_log, v_d_skip, v_ssm_norm, v_w_out, v_mix_post_norm, v_ffn2_pre_norm, v_ffn2_w_gate, v_ffn2_w_up, v_ffn2_w_down, v_ffn2_post_norm):
    given = dict(locals())
    drop = lambda n, a: a if n in SMALL else a[0]
    w = {n: drop(n, given[n]) for n in WEIGHTS}
    m = {n: drop(n, given["m_" + n]) for n in WEIGHTS}
    v = {n: drop(n, given["v_" + n]) for n in WEIGHTS}
    cx, cy, cc = _place()
    my_chip = 2 * cx + cy
    others = [2 * (1 - cx) + cy, 2 * cx + (1 - cy), 2 * (1 - cx) + (1 - cy)]

    core = jnp.stack([cc]).astype(jnp.int32)
    ss, grad_x, g, red = _step(x[0], positions, loss_target[0], {n: w[n] for n in SMALL},
                               blocks={n: w[n] for n in BIG + ["conv_w"]}, core=core)
    loss = lax.psum(0.5 * ss[0, 0] / D, ("x", "y", "c"))
    chip_ids = jnp.stack([my_chip] + others).astype(jnp.int32)
    out_g, out_d, out_m, out_v = {}, {}, {}, {}
    for n in BIG:
        out_g[n], out_d[n], out_m[n], out_v[n] = _adamw_sharded(w[n], m[n], v[n], red[n][0], red[n][1], chip_ids, "adamw_" + n)

    small_g = jnp.concatenate([g[n].reshape(-1) for n in SMALL] + [g["conv_w"].reshape(-1)])
    n_small = small_g.shape[0]
    rows = -(-n_small // 128 // 8) * 8
    packed = jnp.pad(small_g, (0, rows * 128 - n_small)).reshape(rows, 128)
    total = _all_reduce_small(packed).reshape(-1)
    sizes = [w[n].size for n in SMALL]
    offs = [0]
    for s_ in sizes:
        offs.append(offs[-1] + s_)
    gs = {n: total[offs[i]:offs[i + 1]].reshape(w[n].shape) for i, n in enumerate(SMALL)}
    gcw = total[offs[-1]:offs[-1] + 4 * CONV_C].reshape(4, CONV_C)
    gs["conv_w"] = lax.dynamic_slice_in_dim(gcw, (4 * cx + 2 * cy + cc) * (CONV_C // N_DEV), CONV_C // N_DEV, axis=1)
    names = SMALL + ["conv_w"]

    def pack(d):
        flat = jnp.concatenate([d[n].reshape(-1) for n in names])
        return jnp.pad(flat, (0, rows * 128 - flat.shape[0])).reshape(rows, 128)

    pg, pd, pm, pv = _adamw(pack(w), pack(m), pack(v), [pack(gs)], "adamw_small")
    o2 = [0]
    for n in names:
        o2.append(o2[-1] + w[n].size)
    for i, n in enumerate(names):
        for dst, src in ((out_g, pg), (out_d, pd), (out_m, pm), (out_v, pv)):
            dst[n] = src.reshape(-1)[o2[i]:o2[i + 1]].reshape(w[n].shape)

    outs = [loss, grad_x[None]]
    for d in (out_g, out_d, out_m, out_v):
        outs += [d[n] if n in SMALL else d[n][None] for n in WEIGHTS]
    return tuple(outs)
```

```python
import functools
import math

import jax
import jax.numpy as jnp
from jax import lax
from jax.experimental import pallas as pl
from jax.experimental.pallas import tpu as pltpu

F32 = jnp.float32
BF16 = jnp.bfloat16
HI = lax.Precision.HIGHEST
MESH = pl.DeviceIdType.MESH

N_DEV = 8
T = 2048
D = 1024
FF = 2816
FSR = FF // N_DEV
FSH = 384
FFP = N_DEV * FSH
HD = 64
NQ = 16
NKV = 4
QW = NQ * HD
KVW = NKV * HD
SSM_W = 1024
SSM_H = 16
SSM_N = 128
CONV_C = SSM_W + 2 * 2 * SSM_N
IN_COLS = 4112
L = 128
NCH = T // L
AB = 256
NAB = T // AB
EPS = 1e-6
NEG = -1e30
ROPE_THETA = 10000.0
DILATIONS = ((128, 1), (512, 4), (2048, 16))

ADAM_LR = 0.001
ADAM_B1 = 0.9
ADAM_B2 = 0.999
ADAM_EPS = 1e-08
ADAM_WD = 0.01
ADAM_STEP = 10

VMEM_LIMIT = 58 * 1024 * 1024


def _params(sem, vmem=VMEM_LIMIT):
    return pltpu.CompilerParams(dimension_semantics=sem, vmem_limit_bytes=vmem)


def _dot(a, b):
    return jnp.dot(a, b, preferred_element_type=F32)


def _dot_nt(a, b):
    return lax.dot_general(a, b, (((1,), (1,)), ((), ())), preferred_element_type=F32)


def _dot_tn(a, b):
    return lax.dot_general(a, b, (((0,), (0,)), ((), ())), preferred_element_type=F32)


def _dot_hi(a, b):
    return jnp.dot(a, b, preferred_element_type=F32, precision=HI)


def _dot_nt_hi(a, b):
    return lax.dot_general(a, b, (((1,), (1,)), ((), ())), preferred_element_type=F32, precision=HI)


def _rs(x):
    return lax.rsqrt(jnp.mean(x * x, axis=-1, keepdims=True) + EPS)


def _sigmoid(x):
    return jax.nn.sigmoid(x)


def _dsilu(x, s):
    return s * (1.0 + x * (1.0 - s))


def _resident(shape):
    nd = len(shape)
    return pl.BlockSpec(shape, lambda *_: (0,) * nd, pipeline_mode=pl.Buffered(1))


def _const(shape):
    nd = len(shape)
    return pl.BlockSpec(shape, lambda *_: (0,) * nd)


def _rows(tm, cols):
    return pl.BlockSpec((tm, cols), lambda i: (i, 0))


ANY = pl.BlockSpec(memory_space=pl.ANY)


def _place():
    return lax.axis_index("x"), lax.axis_index("y"), lax.axis_index("c")


def _gather_duty(arrays, by_cols):
    n = len(arrays)
    results = [jax.ShapeDtypeStruct((a.shape[0], N_DEV * a.shape[1]) if by_cols[i] else (N_DEV,) + a.shape, a.dtype)
               for i, a in enumerate(arrays)]

    def make(ins, outs, send_sems, recv_sems, local_sems):
        x, y, c = _place()
        me, sibling = (x, y, c), (x, y, 1 - c)
        chips = [(1 - x, y), (x, 1 - y), (1 - x, 1 - y)]

        def place_of(a, px, py, pc):
            d = 4 * px + 2 * py + pc
            if by_cols[a]:
                cols = arrays[a].shape[1]
                return outs[a].at[:, pl.ds(pl.multiple_of(d * cols, 128), cols)]
            return outs[a].at[d]

        def copy(a, k, block, to, src=None):
            dst = place_of(a, *block)
            return pltpu.make_async_remote_copy(src_ref=dst if src is None else src, dst_ref=dst,
                                                send_sem=send_sems.at[7 * a + k], recv_sem=recv_sems.at[7 * a + k],
                                                device_id=to, device_id_type=MESH)

        def own(a):
            return pltpu.make_async_copy(ins[a], place_of(a, *me), local_sems.at[a])

        def first(a):
            return [copy(a, 0, me, sibling, src=ins[a])] + [copy(a, 1 + j, me, (*chip, c), src=ins[a]) for j, chip in enumerate(chips)]

        def start():
            for a in range(n):
                own(a).start()
            for a in range(n):
                for cp in first(a):
                    cp.start()

        def finish():
            for j, chip in enumerate(chips):
                for a in range(n):
                    copy(a, 1 + j, (*chip, c), me).wait_recv()
                    copy(a, 4 + j, (*chip, c), sibling).start()
            for a in range(n):
                copy(a, 0, sibling, me).wait_recv()
                for j, chip in enumerate(chips):
                    copy(a, 4 + j, (*chip, 1 - c), me).wait_recv()
            for a in range(n):
                for cp in first(a) + [copy(a, 4 + j, (*chip, c), sibling) for j, chip in enumerate(chips)]:
                    cp.wait_send()
                own(a).wait()

        return start, finish

    return dict(operands=list(arrays), results=results, sems=(7 * n, 7 * n, n), make=make)


def _swap_duty(arrays):
    n = len(arrays)
    half = N_DEV // 2
    results = [jax.ShapeDtypeStruct((half,) + a.shape[1:], a.dtype) for a in arrays]

    def make(ins, outs, send_sems, recv_sems):
        x, y, c = _place()

        def copies():
            return [pltpu.make_async_remote_copy(src_ref=ins[a].at[2 * k + (1 - c)], dst_ref=outs[a].at[k],
                                                 send_sem=send_sems.at[half * a + k], recv_sem=recv_sems.at[half * a + k],
                                                 device_id=(x, y, 1 - c), device_id_type=MESH)
                    for a in range(n) for k in range(half)]

        def start():
            for cp in copies():
                cp.start()

        def finish():
            for cp in copies():
                cp.wait()

        return start, finish

    return dict(operands=list(arrays), results=results, sems=(half * n, half * n), make=make)


def _exchange_duty(arrays):
    n = len(arrays)
    results = [jax.ShapeDtypeStruct(a.shape, a.dtype) for a in arrays]

    def make(ins, outs, send_sems, recv_sems):
        x, y, c = _place()
        chips = [(1 - x, y), (x, 1 - y), (1 - x, 1 - y)]
        my_chip = 2 * x + y

        def sends():
            return [pltpu.make_async_remote_copy(src_ref=ins[a].at[2 * px + py], dst_ref=outs[a].at[my_chip],
                                                 send_sem=send_sems.at[3 * a + j], recv_sem=recv_sems.at[3 * a + j],
                                                 device_id=(px, py, c), device_id_type=MESH)
                    for a in range(n) for j, (px, py) in enumerate(chips)]

        def start():
            for cp in sends():
                cp.start()

        def finish():
            for a in range(n):
                for j, (px, py) in enumerate(chips):
                    pltpu.make_async_remote_copy(src_ref=ins[a].at[my_chip], dst_ref=outs[a].at[2 * px + py],
                                                 send_sem=send_sems.at[3 * a + j], recv_sem=recv_sems.at[3 * a + j],
                                                 device_id=(px, py, c), device_id_type=MESH).wait_recv()
            for cp in sends():
                cp.wait_send()

        return start, finish

    return dict(operands=list(arrays), results=results, sems=(3 * n, 3 * n), make=make)


def _call(body, *, name, grid, in_specs, out_specs, out_shape, args, sem, scratch=(), duties=()):
    n_in, n_out, n_scr = len(in_specs), len(out_specs), len(scratch)
    sem_shapes = [pltpu.SemaphoreType.DMA((k,)) for d in duties for k in d["sems"]]

    def full(*refs):
        pos = [0]

        def take(k):
            pos[0] += k
            return refs[pos[0] - k:pos[0]]

        ins = take(n_in)
        d_ins = [take(len(d["operands"])) for d in duties]
        outs = take(n_out)
        d_outs = [take(len(d["results"])) for d in duties]
        scr = take(n_scr)
        d_sems = [take(len(d["sems"])) for d in duties]
        hooks = [d["make"](di, do, *ds) for d, di, do, ds in zip(duties, d_ins, d_outs, d_sems)]
        if grid and hooks:
            ids = [pl.program_id(k) for k in range(len(grid))]
            first = functools.reduce(jnp.logical_and, [i == 0 for i in ids])
            last = functools.reduce(jnp.logical_and, [i == g - 1 for i, g in zip(ids, grid)])

            @pl.when(first)
            def _():
                for start, _ in hooks:
                    start()

            body(*ins, *outs, *scr)

            @pl.when(last)
            def _():
                for _, finish in hooks:
                    finish()
        else:
            for start, _ in hooks:
                start()
            body(*ins, *outs, *scr)
            for _, finish in hooks:
                finish()

    d_args = [a for d in duties for a in d["operands"]]
    d_res = [r for d in duties for r in d["results"]]
    kwargs = dict(grid=grid) if grid else {}
    res = pl.pallas_call(
        full, name=name, in_specs=list(in_specs) + [ANY] * len(d_args), out_specs=list(out_specs) + [ANY] * len(d_res),
        out_shape=list(out_shape) + d_res, scratch_shapes=list(scratch) + sem_shapes,
        compiler_params=_params(sem) if grid else None, **kwargs,
    )(*args, *d_args)
    own, rest = list(res[:n_out]), list(res[n_out:])
    by_duty = []
    for d in duties:
        by_duty.append(rest[:len(d["results"])])
        rest = rest[len(d["results"]):]
    return own, by_duty


def _comm_only(duties, name, after=()):
    return _call(lambda *_: None, name=name, grid=None, in_specs=[ANY] * len(after), out_specs=[], out_shape=[], args=list(after),
                 sem=None, duties=duties)[1]


HBM = pl.BlockSpec(memory_space=pltpu.HBM)
SEMS = pl.BlockSpec(memory_space=pltpu.SEMAPHORE)
SIDE_EFFECT = pltpu.SideEffectType.DATAFLOW_SIDE_EFFECTING
N_OTHER_CHIPS = 3


def _chip_copies(src_ref, land_ref, sems):
    x, y, c = _place()
    chips = [(1 - x, y), (x, 1 - y), (1 - x, 1 - y)]
    return [pltpu.make_async_remote_copy(src_ref=src_ref.at[2 * px + py], dst_ref=land_ref.at[2 * x + y],
                                         send_sem=sems[j], recv_sem=sems[N_OTHER_CHIPS + j], device_id=(px, py, c), device_id_type=MESH)
            for j, (px, py) in enumerate(chips)]


def _exchange_start(pb, name):
    n_sem = 2 * N_OTHER_CHIPS

    def body(pb_ref, land_ref, *rest):
        for cp in _chip_copies(pb_ref, land_ref, rest[:n_sem]):
            cp.start()
        token = rest[n_sem + 2]
        token[...] = jnp.zeros_like(token)

    res = pl.pallas_call(
        body, name=name,
        out_shape=(pltpu.SemaphoreType.DMA(()),) * n_sem + (pltpu.HBM(pb.shape, pb.dtype), pltpu.HBM(pb.shape, pb.dtype),
                                                              jax.ShapeDtypeStruct((8, 128), F32)),
        in_specs=(HBM, HBM), out_specs=(SEMS,) * n_sem + (HBM, HBM, pl.BlockSpec(memory_space=pltpu.VMEM)),
        input_output_aliases={0: n_sem, 1: n_sem + 1},
        compiler_params=pltpu.CompilerParams(has_side_effects=SIDE_EFFECT),
    )(pltpu.with_memory_space_constraint(pb, pltpu.HBM), pltpu.with_memory_space_constraint(lax.empty(pb.shape, pb.dtype), pltpu.HBM))
    return dict(sems=res[:n_sem], src=res[n_sem], land=res[n_sem + 1], token=res[n_sem + 2])


def _exchange_wait(started, after, name):
    n_sem = 2 * N_OTHER_CHIPS

    def body(pb_ref, land_ref, *rest):
        for cp in _chip_copies(pb_ref, land_ref, rest[:n_sem]):
            cp.wait_send()
            cp.wait_recv()

    src, land = started["src"], started["land"]
    return pl.pallas_call(
        body, name=name, out_shape=(pltpu.HBM(src.shape, src.dtype), pltpu.HBM(land.shape, land.dtype)),
        in_specs=(HBM, HBM) + (SEMS,) * n_sem + (ANY,), out_specs=(HBM, HBM), input_output_aliases={0: 0, 1: 1},
        compiler_params=pltpu.CompilerParams(has_side_effects=SIDE_EFFECT),
    )(src, land, *started["sems"], after)[1]


def _ffn_fwd(x, gpre, wg, wu, wd, gpost, name, duties=()):
    tm = 256

    def body(x_ref, gpre_ref, wg_ref, wu_ref, wd_ref, gpost_ref, xo_ref, n_ref, a_ref, b_ref, hm_ref, h_ref):
        xv = x_ref[...]
        n = (xv * _rs(xv) * gpre_ref[...]).astype(BF16)
        a = _dot(n, wg_ref[...])
        b = _dot(n, wu_ref[...])
        hm = (a * _sigmoid(a) * b).astype(BF16)
        h = _dot(hm, wd_ref[...])
        xo_ref[...] = xv + 0.5 * (h * _rs(h) * gpost_ref[...])
        n_ref[...] = n
        a_ref[...] = a.astype(BF16)
        b_ref[...] = b.astype(BF16)
        hm_ref[...] = hm
        h_ref[...] = h

    return _call(
        body, name=name, grid=(T // tm,),
        in_specs=[_rows(tm, D), _const((1, D)), _resident((D, FFP)), _resident((D, FFP)), _resident((FFP, D)), _const((1, D))],
        out_specs=[_rows(tm, D), _rows(tm, D), _rows(tm, FFP), _rows(tm, FFP), _rows(tm, FFP), _rows(tm, D)],
        out_shape=[jax.ShapeDtypeStruct((T, D), F32), jax.ShapeDtypeStruct((T, D), BF16), jax.ShapeDtypeStruct((T, FFP), BF16),
                   jax.ShapeDtypeStruct((T, FFP), BF16), jax.ShapeDtypeStruct((T, FFP), BF16), jax.ShapeDtypeStruct((T, D), F32)],
        args=[x, gpre, wg, wu, wd, gpost], sem=("arbitrary",), duties=duties)


def _ffn_bwd(dxo, x, a, b, h, gpre, gpost, wg, wu, wd, name, duties=()):
    tm = 256

    def body(dxo_ref, x_ref, a_ref, b_ref, h_ref, gpre_ref, gpost_ref, wg_ref, wu_ref, wd_ref,
             dx_ref, da_ref, db_ref, dh_ref, dgpre_ref, dgpost_ref):
        @pl.when(pl.program_id(0) == 0)
        def _():
            dgpre_ref[...] = jnp.zeros_like(dgpre_ref)
            dgpost_ref[...] = jnp.zeros_like(dgpost_ref)

        dy = dxo_ref[...]
        h = h_ref[...]
        hn = h * _rs(h)
        r2 = _rs(h)
        dgpost_ref[...] += jnp.sum(0.5 * dy * hn, axis=0, keepdims=True)
        gdy = 0.5 * dy * gpost_ref[...]
        dh = r2 * (gdy - hn * jnp.mean(gdy * hn, axis=-1, keepdims=True))
        dhb = dh.astype(BF16)
        dh_ref[...] = dhb
        dhm = _dot_nt(dhb, wd_ref[...])
        av = a_ref[...].astype(F32)
        bv = b_ref[...].astype(F32)
        sg = _sigmoid(av)
        db = (dhm * (av * sg)).astype(BF16)
        da = (dhm * bv * _dsilu(av, sg)).astype(BF16)
        da_ref[...] = da
        db_ref[...] = db
        dn = _dot_nt(da, wg_ref[...]) + _dot_nt(db, wu_ref[...])
        xv = x_ref[...]
        r = _rs(xv)
        xn = xv * r
        dgpre_ref[...] += jnp.sum(dn * xn, axis=0, keepdims=True)
        gdn = dn * gpre_ref[...]
        dx_ref[...] = dy + r * (gdn - xn * jnp.mean(gdn * xn, axis=-1, keepdims=True))

    return _call(
        body, name=name, grid=(T // tm,),
        in_specs=[_rows(tm, D), _rows(tm, D), _rows(tm, FFP), _rows(tm, FFP), _rows(tm, D), _const((1, D)), _const((1, D)),
                  _resident((D, FFP)), _resident((D, FFP)), _resident((FFP, D))],
        out_specs=[_rows(tm, D), _rows(tm, FFP), _rows(tm, FFP), _rows(tm, D), _const((1, D)), _const((1, D))],
        out_shape=[jax.ShapeDtypeStruct((T, D), F32), jax.ShapeDtypeStruct((T, FFP), BF16), jax.ShapeDtypeStruct((T, FFP), BF16),
                   jax.ShapeDtypeStruct((T, D), BF16), jax.ShapeDtypeStruct((1, D), F32), jax.ShapeDtypeStruct((1, D), F32)],
        args=[dxo, x, a, b, h, gpre, gpost, wg, wu, wd], sem=("arbitrary",), duties=duties)


def _matmul_tn(a, b, name, tm=256):
    k, m = a.shape
    n = b.shape[1]
    assert m % tm == 0

    def body(a_ref, b_ref, o_ref, ob_ref):
        r = _dot_tn(a_ref[...], b_ref[...])
        o_ref[...] = r
        ob_ref[...] = r.astype(BF16)

    spec = pl.BlockSpec((tm, n), lambda i: (i, 0))
    return pl.pallas_call(
        body, name=name, grid=(m // tm,),
        in_specs=[pl.BlockSpec((k, tm), lambda i: (0, i)), _resident((k, n))],
        out_specs=[spec, spec],
        out_shape=[jax.ShapeDtypeStruct((m, n), F32), jax.ShapeDtypeStruct((m, n), BF16)],
        compiler_params=_params(("parallel",)),
    )(a, b)


def _matmul_tn_cols(a, b, name, tm=256, after=(), duties=()):
    k, m = a.shape
    s = b.shape[1] // N_DEV
    assert m % tm == 0 and s % 128 == 0

    def body(a_ref, b_ref, *rest):
        o_ref, ob_ref = rest[len(after):]
        r = _dot_tn(a_ref[...], b_ref[...])
        for d in range(N_DEV):
            o_ref[d] = r[:, s * d:s * (d + 1)]
            ob_ref[d] = r[:, s * d:s * (d + 1)].astype(BF16)

    spec = pl.BlockSpec((N_DEV, tm, s), lambda i: (0, i, 0))
    return _call(
        body, name=name, grid=(m // tm,),
        in_specs=[pl.BlockSpec((k, tm), lambda i: (0, i)), _resident((k, N_DEV * s))] + [ANY] * len(after),
        out_specs=[spec, spec],
        out_shape=[jax.ShapeDtypeStruct((N_DEV, m, s), F32), jax.ShapeDtypeStruct((N_DEV, m, s), BF16)],
        args=[a, b, *after], sem=("arbitrary",), duties=duties)


def _dwout(attn, yn, dh2):
    rs = (QW + SSM_W) // N_DEV
    half = N_DEV // 2

    def body(at_ref, yn_ref, dh_ref, o_ref, ob_ref):
        i = pl.program_id(0)

        def put(r):
            o_ref[0] = r
            ob_ref[0] = r.astype(BF16)

        @pl.when(i < half)
        def _():
            put(_dot_tn(at_ref[...], dh_ref[...]))

        @pl.when(i >= half)
        def _():
            put(_dot_tn(yn_ref[...], dh_ref[...]))

    spec = pl.BlockSpec((1, rs, D), lambda i: (i, 0, 0))
    return pl.pallas_call(
        body, name="dwout", grid=(N_DEV,),
        in_specs=[pl.BlockSpec((T, rs), lambda i: (0, jnp.minimum(i, half - 1))),
                  pl.BlockSpec((T, rs), lambda i: (0, jnp.maximum(i - half, 0))), _resident((T, D))],
        out_specs=[spec, spec],
        out_shape=[jax.ShapeDtypeStruct((N_DEV, rs, D), F32), jax.ShapeDtypeStruct((N_DEV, rs, D), BF16)],
        compiler_params=_params(("parallel",)),
    )(attn, yn, dh2)


def _rope_swap(t, lo_half):
    return jnp.where(lo_half, pltpu.roll(t, 96, 1), pltpu.roll(t, 32, 1))


def _inproj_fwd(x1, gpre, win, cos, sin_s, ex, duties=()):
    tm = 256

    def body(x_ref, g_ref, w_ref, cos_ref, sin_ref, ex_ref, n_ref, q_ref, kx_ref, vx_ref, xbc_ref, z_ref, dt_ref):
        xv = x_ref[...]
        n = (xv * _rs(xv) * g_ref[...]).astype(BF16)
        n_ref[...] = n
        proj = _dot(n, w_ref[...])
        cs = cos_ref[...]
        sn = sin_ref[...]
        lo_half = (lax.broadcasted_iota(jnp.int32, (1, 128), 1) % HD) < (HD // 2)

        def rope(t):
            return t * cs + _rope_swap(t, lo_half) * sn

        for j in range(QW // 128):
            t = proj[:, 128 * j:128 * j + 128]
            q_ref[:, 128 * j:128 * j + 128] = (rope(t) * (HD ** -0.5)).astype(BF16)
        k = jnp.concatenate([rope(proj[:, QW + 128 * j:QW + 128 * j + 128]) for j in range(KVW // 128)], axis=1)
        v = proj[:, QW + KVW:QW + 2 * KVW]
        kx_ref[...] = _dot(k.astype(BF16), ex_ref[...]).astype(BF16)
        vx_ref[...] = _dot(v.astype(BF16), ex_ref[...]).astype(BF16)
        c0 = QW + 2 * KVW
        xbc_ref[...] = proj[:, c0:c0 + CONV_C]
        z_ref[...] = proj[:, c0 + CONV_C:c0 + CONV_C + SSM_W]
        dt_ref[...] = proj[:, c0 + CONV_C + SSM_W:IN_COLS]

    return _call(
        body, name="inproj_fwd", grid=(T // tm,),
        in_specs=[_rows(tm, D), _const((1, D)), _resident((D, IN_COLS)), _rows(tm, 128), _rows(tm, 128), _const((KVW, QW))],
        out_specs=[_rows(tm, D), _rows(tm, QW), _rows(tm, QW), _rows(tm, QW), _rows(tm, CONV_C), _rows(tm, SSM_W), _rows(tm, SSM_H)],
        out_shape=[jax.ShapeDtypeStruct((T, D), BF16), jax.ShapeDtypeStruct((T, QW), BF16), jax.ShapeDtypeStruct((T, QW), BF16),
                   jax.ShapeDtypeStruct((T, QW), BF16), jax.ShapeDtypeStruct((T, CONV_C), F32), jax.ShapeDtypeStruct((T, SSM_W), F32),
                   jax.ShapeDtypeStruct((T, SSM_H), F32)],
        args=[x1, gpre, win, cos, sin_s, ex], sem=("arbitrary",), duties=duties)


def _inproj_bwd(dres, dq, dkx, dvx, dxbc, dz, ddt, x1, gpre, win, cos, sin_s, exf):
    tm = 256

    def body(dres_ref, dq_ref, dkx_ref, dvx_ref, dxbc_ref, dz_ref, ddt_ref, x_ref, g_ref, w_ref, cos_ref, sin_ref, ex_ref,
             dx_ref, dp_ref, dg_ref):
        @pl.when(pl.program_id(0) == 0)
        def _():
            dg_ref[...] = jnp.zeros_like(dg_ref)

        cs = cos_ref[...]
        sn = sin_ref[...]
        lo_half = (lax.broadcasted_iota(jnp.int32, (1, 128), 1) % HD) < (HD // 2)

        def rope_t(t):
            return t * cs - _rope_swap(t, lo_half) * sn

        for j in range(QW // 128):
            dp_ref[:, 128 * j:128 * j + 128] = rope_t(dq_ref[:, 128 * j:128 * j + 128] * (HD ** -0.5)).astype(BF16)
        dk = _dot_nt_hi(dkx_ref[...], ex_ref[...])
        dv = _dot_nt_hi(dvx_ref[...], ex_ref[...])
        for j in range(KVW // 128):
            dp_ref[:, QW + 128 * j:QW + 128 * j + 128] = rope_t(dk[:, 128 * j:128 * j + 128]).astype(BF16)
        dp_ref[:, QW + KVW:QW + 2 * KVW] = dv.astype(BF16)
        c0 = QW + 2 * KVW
        dp_ref[:, c0:c0 + CONV_C] = dxbc_ref[...].astype(BF16)
        dp_ref[:, c0 + CONV_C:c0 + CONV_C + SSM_W] = dz_ref[...].astype(BF16)
        dp_ref[:, c0 + CONV_C + SSM_W:IN_COLS] = ddt_ref[...].astype(BF16)
        dn = _dot_nt(dp_ref[...], w_ref[...])
        xv = x_ref[...]
        r = _rs(xv)
        xn = xv * r
        dg_ref[...] += jnp.sum(dn * xn, axis=0, keepdims=True)
        gdn = dn * g_ref[...]
        dx_ref[...] = dres_ref[...] + r * (gdn - xn * jnp.mean(gdn * xn, axis=-1, keepdims=True))

    return pl.pallas_call(
        body, name="inproj_bwd", grid=(T // tm,),
        in_specs=[_rows(tm, D), _rows(tm, QW), _rows(tm, QW), _rows(tm, QW), _rows(tm, CONV_C), _rows(tm, SSM_W), _rows(tm, SSM_H),
                  _rows(tm, D), _const((1, D)), _resident((D, IN_COLS)), _rows(tm, 128), _rows(tm, 128), _const((KVW, QW))],
        out_specs=[_rows(tm, D), _rows(tm, IN_COLS), _const((1, D))],
        out_shape=[jax.ShapeDtypeStruct((T, D), F32), jax.ShapeDtypeStruct((T, IN_COLS), BF16), jax.ShapeDtypeStruct((1, D), F32)],
        compiler_params=_params(("arbitrary",)),
    )(dres, dq, dkx, dvx, dxbc, dz, ddt, x1, gpre, win, cos, sin_s, exf)


def _outproj_fwd(x1, attn, yn, wout, gpost):
    tm = 256

    def body(x_ref, at_ref, yn_ref, w_ref, g_ref, xo_ref, h_ref):
        h = _dot(at_ref[...], w_ref[0:QW, :]) + _dot(yn_ref[...], w_ref[QW:QW + SSM_W, :])
        h_ref[...] = h
        xo_ref[...] = x_ref[...] + h * _rs(h) * g_ref[...]

    return pl.pallas_call(
        body, name="outproj_fwd", grid=(T // tm,),
        in_specs=[_rows(tm, D), _rows(tm, QW), _rows(tm, SSM_W), _resident((QW + SSM_W, D)), _const((1, D))],
        out_specs=[_rows(tm, D), _rows(tm, D)],
        out_shape=[jax.ShapeDtypeStruct((T, D), F32), jax.ShapeDtypeStruct((T, D), F32)],
        compiler_params=_params(("parallel",)),
    )(x1, attn, yn, wout, gpost)


def _outproj_bwd(dx2, h2, gpost, wout, duties=()):
    tm = 256

    def body(dy_ref, h_ref, g_ref, w_ref, dh_ref, dm_ref, dg_ref):
        @pl.when(pl.program_id(0) == 0)
        def _():
            dg_ref[...] = jnp.zeros_like(dg_ref)

        dy = dy_ref[...]
        h = h_ref[...]
        r = _rs(h)
        hn = h * r
        dg_ref[...] += jnp.sum(dy * hn, axis=0, keepdims=True)
        gdy = dy * g_ref[...]
        dh = (r * (gdy - hn * jnp.mean(gdy * hn, axis=-1, keepdims=True))).astype(BF16)
        dh_ref[...] = dh
        dm_ref[...] = _dot_nt(dh, w_ref[...])

    return _call(
        body, name="outproj_bwd", grid=(T // tm,),
        in_specs=[_rows(tm, D), _rows(tm, D), _const((1, D)), _resident((QW + SSM_W, D))],
        out_specs=[_rows(tm, D), _rows(tm, QW + SSM_W), _const((1, D))],
        out_shape=[jax.ShapeDtypeStruct((T, D), BF16), jax.ShapeDtypeStruct((T, QW + SSM_W), F32), jax.ShapeDtypeStruct((1, D), F32)],
        args=[dx2, h2, gpost, wout], sem=("arbitrary",), duties=duties)


def _attn_bias():
    d = jnp.arange(AB)[:, None] - jnp.arange(T)[None, :] + (T - AB)
    cnt = jnp.zeros(d.shape, F32)
    for window, dil in DILATIONS:
        cnt = cnt + ((d >= 0) & (d % dil == 0) & (d <= window)).astype(F32)
    return jnp.where(cnt > 0, jnp.log(jnp.maximum(cnt, 1.0)), NEG)


G_PER = NQ // NKV
WK = G_PER * HD


def _attn_fwd(q, kx, vx, bias, duties=()):
    def body(q_ref, kx_ref, vx_ref, bias_ref, o_ref, lse_ref):
        lane = lax.broadcasted_iota(jnp.int32, (1, WK), 1)
        lse_ref[...] = jnp.zeros_like(lse_ref)
        for i in range(NAB):
            n = (i + 1) * AB
            rows = slice(i * AB, n)
            qi = q_ref[rows, :]
            kxi = kx_ref[0:n, :]
            vxi = vx_ref[0:n, :]
            bb = bias_ref[:, (NAB - 1 - i) * AB:]
            o_acc = jnp.zeros((AB, WK), F32)
            for g in range(G_PER):
                mg = (lane // HD) == g
                s = _dot_nt(jnp.where(mg, qi, jnp.zeros_like(qi)), kxi) + bb
                m = jnp.max(s, axis=1, keepdims=True)
                p = jnp.exp(s - m)
                l = jnp.sum(p, axis=1, keepdims=True)
                o_acc = jnp.where(mg, _dot(p.astype(BF16), vxi) / l, o_acc)
                lse_ref[rows, g:g + 1] = m + jnp.log(l)
            o_ref[rows, :] = o_acc.astype(BF16)

    col = lambda kv: (0, kv)
    return _call(
        body, name="attn_fwd", grid=(NKV,),
        in_specs=[pl.BlockSpec((T, WK), col), pl.BlockSpec((T, WK), col), pl.BlockSpec((T, WK), col), _const((AB, T))],
        out_specs=[pl.BlockSpec((T, WK), col), pl.BlockSpec((T, 128), col)],
        out_shape=[jax.ShapeDtypeStruct((T, QW), BF16), jax.ShapeDtypeStruct((T, NKV * 128), F32)],
        args=[q, kx, vx, bias], sem=("arbitrary",), duties=duties)


def _attn_bwd(q, kx, vx, o, dmix, lse, bias, duties=()):
    def body(q_ref, kx_ref, vx_ref, o_ref, do_ref, lse_ref, bias_ref, dq_ref, dkx_ref, dvx_ref):
        lane = lax.broadcasted_iota(jnp.int32, (1, WK), 1)
        dkx_ref[...] = jnp.zeros_like(dkx_ref)
        dvx_ref[...] = jnp.zeros_like(dvx_ref)
        for i in range(NAB):
            n = (i + 1) * AB
            rows = slice(i * AB, n)
            qi = q_ref[rows, :]
            dof = do_ref[rows, :]
            doi = dof.astype(BF16)
            prod = dof * o_ref[rows, :].astype(F32)
            kxi = kx_ref[0:n, :]
            vxi = vx_ref[0:n, :]
            bb = bias_ref[:, (NAB - 1 - i) * AB:]
            dq_acc = jnp.zeros((AB, WK), F32)
            for g in range(G_PER):
                mg = (lane // HD) == g
                qm = jnp.where(mg, qi, jnp.zeros_like(qi))
                dom = jnp.where(mg, doi, jnp.zeros_like(doi))
                delta = jnp.sum(jnp.where(mg, prod, 0.0), axis=1, keepdims=True)
                p = jnp.exp(_dot_nt(qm, kxi) + bb - lse_ref[rows, g:g + 1])
                ds = (p * (_dot_nt(dom, vxi) - delta)).astype(BF16)
                dvx_ref[0:n, :] += _dot_tn(p.astype(BF16), dom)
                dkx_ref[0:n, :] += _dot_tn(ds, qm)
                dq_acc = jnp.where(mg, _dot(ds, kxi), dq_acc)
            dq_ref[rows, :] = dq_acc

    col = lambda kv: (0, kv)
    return _call(
        body, name="attn_bwd", grid=(NKV,),
        in_specs=[pl.BlockSpec((T, WK), col), pl.BlockSpec((T, WK), col), pl.BlockSpec((T, WK), col), pl.BlockSpec((T, WK), col),
                  pl.BlockSpec((T, WK), col), pl.BlockSpec((T, 128), col), _const((AB, T))],
        out_specs=[pl.BlockSpec((T, WK), col), pl.BlockSpec((T, WK), col), pl.BlockSpec((T, WK), col)],
        out_shape=[jax.ShapeDtypeStruct((T, QW), F32)] * 3,
        args=[q, kx, vx, o, dmix, lse, bias], sem=("arbitrary",), duties=duties)


def _softplus(x):
    return jnp.maximum(x, 0.0) + jnp.log1p(jnp.exp(-jnp.abs(x)))


def _conv_taps(u, halo):
    zext = jnp.concatenate([halo, u], axis=0)
    return [pltpu.roll(zext, m, 0)[8:] for m in (1, 2, 3)]


def _ssd_chunk_common(u, halo, dtr, cw_ref, cb_ref, dtb_ref, alx_ref, e_ref):
    sh1, sh2, sh3 = _conv_taps(u, halo)
    xc = cb_ref[...] + cw_ref[3:4, :] * u + cw_ref[2:3, :] * sh1 + cw_ref[1:2, :] * sh2 + cw_ref[0:1, :] * sh3
    sg = _sigmoid(xc)
    act = xc * sg
    pre_x = _dot_hi(dtr + dtb_ref[...], e_ref[...])
    dt_x = _softplus(pre_x)
    a_x = -jnp.exp(alx_ref[...])
    ri = lax.broadcasted_iota(jnp.int32, (L, L), 0)
    ci = lax.broadcasted_iota(jnp.int32, (L, L), 1)
    tri = ri >= ci
    acs_x = _dot_hi(tri.astype(F32), dt_x * a_x)
    return dict(sh=(sh1, sh2, sh3), xc=xc, sg=sg, act=act, pre_x=pre_x, dt_x=dt_x, a_x=a_x, tri=tri, acs_x=acs_x)


def _decay(acs_x, acs_t, h, tri):
    col = acs_x[:, HD * h:HD * h + 1]
    row = acs_t[HD * h:HD * h + 1, :]
    return jnp.exp(jnp.where(tri, col - row, NEG))


def _ssd_fwd(xbc, z, dtr, convw, convb, dtb, alx, dskx, ssmn, e, duties=()):
    def body(u_ref, z_ref, dtr_ref, cw_ref, cb_ref, dtb_ref, alx_ref, dsk_ref, sn_ref, e_ref,
             yn_ref, y_ref, hs_ref, halo, hst):
        @pl.when(pl.program_id(0) == 0)
        def _():
            halo[...] = jnp.zeros_like(halo)
            hst[...] = jnp.zeros_like(hst)

        u = u_ref[...]
        cm = _ssd_chunk_common(u, halo[...], dtr_ref[...], cw_ref, cb_ref, dtb_ref, alx_ref, e_ref)
        halo[...] = u[L - 8:, :]
        act, dt_x, acs_x, tri = cm["act"], cm["dt_x"], cm["acs_x"], cm["tri"]
        xs = act[:, :SSM_W]
        acs_l = acs_x[L - 1:L, :]
        lam_x = jnp.exp(acs_x)
        w_x = jnp.exp(acs_l - acs_x)
        gam_x = jnp.exp(acs_l)
        acs_t = acs_x.T
        xd = xs * dt_x
        xb = xd.astype(BF16)
        xw = (xd * w_x).astype(BF16)
        lo = lax.broadcasted_iota(jnp.int32, (1, 128), 1) < HD
        hs_ref[0] = hst[...]
        pieces = []
        for grp in range(2):
            bb = act[:, SSM_W + SSM_N * grp:SSM_W + SSM_N * (grp + 1)].astype(BF16)
            cb_ = act[:, SSM_W + 2 * SSM_N + SSM_N * grp:SSM_W + 2 * SSM_N + SSM_N * (grp + 1)].astype(BF16)
            cbm = _dot_nt(cb_, bb)
            for jj in range(4):
                j = 4 * grp + jj
                sl = slice(128 * j, 128 * j + 128)
                m0 = (cbm * _decay(acs_x, acs_t, 2 * j, tri)).astype(BF16)
                m1 = (cbm * _decay(acs_x, acs_t, 2 * j + 1, tri)).astype(BF16)
                x2 = xb[:, sl]
                ydiag = jnp.where(lo, _dot(m0, x2), _dot(m1, x2))
                hprev = hst[j]
                yoff = lam_x[:, sl] * _dot(cb_, hprev.astype(BF16))
                pieces.append(ydiag + yoff)
                hst[j] = gam_x[:, sl] * hprev + _dot_tn(bb, xw[:, sl])
        y = jnp.concatenate(pieces, axis=1) + dsk_ref[...] * xs
        y_ref[...] = y
        zv = z_ref[...]
        yz = y * (zv * _sigmoid(zv))
        half = SSM_W // 2
        yn = jnp.concatenate([yz[:, :half] * _rs(yz[:, :half]), yz[:, half:] * _rs(yz[:, half:])], axis=1)
        yn_ref[...] = (yn * sn_ref[...]).astype(BF16)

    return _call(
        body, name="ssd_fwd", grid=(NCH,),
        in_specs=[_rows(L, CONV_C), _rows(L, SSM_W), _rows(L, SSM_H), _const((4, CONV_C)), _const((1, CONV_C)), _const((1, SSM_H)),
                  _const((1, SSM_W)), _const((1, SSM_W)), _const((1, SSM_W)), _const((SSM_H, SSM_W))],
        out_specs=[_rows(L, SSM_W), _rows(L, SSM_W), pl.BlockSpec((1, 8, SSM_N, 128), lambda c: (c, 0, 0, 0))],
        out_shape=[jax.ShapeDtypeStruct((T, SSM_W), BF16), jax.ShapeDtypeStruct((T, SSM_W), F32),
                   jax.ShapeDtypeStruct((NCH, 8, SSM_N, 128), F32)],
        scratch=[pltpu.VMEM((8, CONV_C), F32), pltpu.VMEM((8, SSM_N, 128), F32)],
        args=[xbc, z, dtr, convw, convb, dtb, alx, dskx, ssmn, e], sem=("arbitrary",), duties=duties)


def _ssd_bwd(dmix, xbc, z, dtr, y, hs, convw, convb, dtb, alx, dskx, ssmn, e, e1, duties=()):
    rev = lambda i: (NCH - 1 - i, 0)

    def body(dyn_ref, u_ref, uh_ref, z_ref, dtr_ref, y_ref, hs_ref, cw_ref, cb_ref, dtb_ref, alx_ref, dsk_ref, sn_ref, e_ref, e1_ref,
             dxbc_ref, dz_ref, ddt_ref, dcw_ref, dcb_ref, dsn_ref, dpar_ref, dh, duh, colbuf):
        step = pl.program_id(0)
        c = NCH - 1 - step

        @pl.when(step == 0)
        def _():
            for r in (dh, duh, dcw_ref, dcb_ref, dsn_ref, dpar_ref):
                r[...] = jnp.zeros_like(r)

        u = u_ref[...]
        halo = jnp.where(c > 0, uh_ref[...], 0.0)
        cm = _ssd_chunk_common(u, halo, dtr_ref[...], cw_ref, cb_ref, dtb_ref, alx_ref, e_ref)
        sh1, sh2, sh3 = cm["sh"]
        xc, sg, act, pre_x, dt_x, a_x, tri, acs_x = (cm[k] for k in ("xc", "sg", "act", "pre_x", "dt_x", "a_x", "tri", "acs_x"))
        xs = act[:, :SSM_W]
        acs_l = acs_x[L - 1:L, :]
        lam_x = jnp.exp(acs_x)
        w_x = jnp.exp(acs_l - acs_x)
        gam_x = jnp.exp(acs_l)
        acs_t = acs_x.T
        xd = xs * dt_x
        xb = xd.astype(BF16)
        xdw = xd * w_x
        xw = xdw.astype(BF16)
        lo = lax.broadcasted_iota(jnp.int32, (1, 128), 1) < HD
        row8 = lax.broadcasted_iota(jnp.int32, (8, 1), 0)

        dyn = dyn_ref[...]
        yv = y_ref[...]
        zv = z_ref[...]
        sz = _sigmoid(zv)
        siluz = zv * sz
        yz = yv * siluz
        half = SSM_W // 2
        gy = dyn * sn_ref[...]
        dyz_parts, yzn_parts = [], []
        for hf in range(2):
            part = yz[:, hf * half:(hf + 1) * half]
            r = _rs(part)
            pn = part * r
            gp = gy[:, hf * half:(hf + 1) * half]
            dyz_parts.append(r * (gp - pn * jnp.mean(gp * pn, axis=-1, keepdims=True)))
            yzn_parts.append(pn)
        dyz = jnp.concatenate(dyz_parts, axis=1)
        dsn_ref[...] += jnp.sum(dyn * jnp.concatenate(yzn_parts, axis=1), axis=0, keepdims=True)
        dy = dyz * siluz
        dz_ref[...] = dyz * yv * _dsilu(zv, sz)

        colbuf[...] = jnp.zeros_like(colbuf)
        dx_pieces, dacs_pieces, dacsl_pieces, db_pieces, dc_pieces = [], [], [], [], []
        for grp in range(2):
            bb = act[:, SSM_W + SSM_N * grp:SSM_W + SSM_N * (grp + 1)].astype(BF16)
            cb_ = act[:, SSM_W + 2 * SSM_N + SSM_N * grp:SSM_W + 2 * SSM_N + SSM_N * (grp + 1)].astype(BF16)
            cbm = _dot_nt(cb_, bb)
            dcbm = jnp.zeros((L, L), F32)
            dc_g = jnp.zeros((L, SSM_N), F32)
            db_g = jnp.zeros((L, SSM_N), F32)
            for jj in range(4):
                j = 4 * grp + jj
                sl = slice(128 * j, 128 * j + 128)
                dy2 = dy[:, sl]
                dy2b = dy2.astype(BF16)
                d0 = _decay(acs_x, acs_t, 2 * j, tri)
                d1 = _decay(acs_x, acs_t, 2 * j + 1, tri)
                m0 = cbm * d0
                m1 = cbm * d1
                x2 = xb[:, sl]
                hprev = hs_ref[0, j]
                hprevb = hprev.astype(BF16)
                dhn = dh[j]
                dhnb = dhn.astype(BF16)
                g2 = _dot(bb, dhnb)
                dx_pieces.append(jnp.where(lo, _dot_tn(m0.astype(BF16), dy2b), _dot_tn(m1.astype(BF16), dy2b)) + w_x[:, sl] * g2)
                zero = jnp.zeros_like(dy2b)
                dm0 = _dot_nt(jnp.where(lo, dy2b, zero), x2)
                dm1 = _dot_nt(jnp.where(lo, zero, dy2b), x2)
                dcbm = dcbm + dm0 * d0 + dm1 * d1
                e0 = dm0 * m0
                e1v = dm1 * m1
                colbuf[:, 2 * j:2 * j + 1] = jnp.sum(e0, axis=1, keepdims=True) - jnp.sum(e0.T, axis=1, keepdims=True)
                colbuf[:, 2 * j + 1:2 * j + 2] = jnp.sum(e1v, axis=1, keepdims=True) - jnp.sum(e1v.T, axis=1, keepdims=True)
                yoff = lam_x[:, sl] * _dot(cb_, hprevb)
                gxw = g2 * xdw[:, sl]
                dacs_pieces.append(dy2 * yoff - gxw)
                dacsl_pieces.append(jnp.sum(gxw, axis=0, keepdims=True) + gam_x[:, sl] * jnp.sum(dhn * hprev, axis=0, keepdims=True))
                dyl = (dy2 * lam_x[:, sl]).astype(BF16)
                dc_g = dc_g + _dot_nt(dyl, hprevb)
                db_g = db_g + _dot_nt(xw[:, sl], dhnb)
                dh[j] = gam_x[:, sl] * dhn + _dot_tn(cb_, dyl)
            dcbb = dcbm.astype(BF16)
            dc_pieces.append(dc_g + _dot(dcbb, bb))
            db_pieces.append(db_g + _dot_tn(dcbb, cb_))

        dxd = jnp.concatenate(dx_pieces, axis=1)
        rowi = lax.broadcasted_iota(jnp.int32, (L, 1), 0)
        dacs_x = (jnp.concatenate(dacs_pieces, axis=1) + _dot_hi(colbuf[...], e1_ref[...])
                  + jnp.where(rowi == L - 1, jnp.concatenate(dacsl_pieces, axis=1), 0.0))
        upper = lax.broadcasted_iota(jnp.int32, (L, L), 0) <= lax.broadcasted_iota(jnp.int32, (L, L), 1)
        dadt_x = _dot_hi(upper.astype(F32), dacs_x)
        ddt_x = dxd * xs + dadt_x * a_x
        ddtr = _dot_nt_hi(ddt_x * _sigmoid(pre_x), e_ref[...])
        ddt_ref[...] = ddtr
        dalx = jnp.sum(dadt_x * dt_x, axis=0, keepdims=True) * a_x
        ddskx = jnp.sum(dy * xs, axis=0, keepdims=True)
        par_x = jnp.where(row8 == 1, dalx, 0.0) + jnp.where(row8 == 2, ddskx, 0.0)
        dpar_ref[...] += _dot_nt_hi(par_x, e_ref[...]) + jnp.where(row8 == 0, jnp.sum(ddtr, axis=0, keepdims=True), 0.0)

        dxs = dxd * dt_x + dsk_ref[...] * dy
        dact = jnp.concatenate([dxs] + db_pieces + dc_pieces, axis=1)
        du = dact * _dsilu(xc, sg)
        dcb_ref[...] += jnp.sum(du, axis=0, keepdims=True)
        taps = (sh3, sh2, sh1, u)
        dcw = jnp.zeros((8, CONV_C), F32)
        for k in range(4):
            dcw = dcw + jnp.where(row8 == k, jnp.sum(du * taps[k], axis=0, keepdims=True), 0.0)
        dcw_ref[...] += dcw
        zext = jnp.concatenate([du, duh[...]], axis=0)
        f1, f2, f3 = (pltpu.roll(zext, L + 8 - m, 0)[:L] for m in (1, 2, 3))
        dxbc_ref[...] = cw_ref[3:4, :] * du + cw_ref[2:3, :] * f1 + cw_ref[1:2, :] * f2 + cw_ref[0:1, :] * f3
        duh[...] = du[:8, :]

    return _call(
        body, name="ssd_bwd", grid=(NCH,),
        in_specs=[pl.BlockSpec((L, SSM_W), lambda i: (NCH - 1 - i, 1)), pl.BlockSpec((L, CONV_C), rev),
                  pl.BlockSpec((8, CONV_C), lambda i: (jnp.maximum((NCH - 1 - i) * (L // 8) - 1, 0), 0)),
                  pl.BlockSpec((L, SSM_W), rev), pl.BlockSpec((L, SSM_H), rev), pl.BlockSpec((L, SSM_W), rev),
                  pl.BlockSpec((1, 8, SSM_N, 128), lambda i: (NCH - 1 - i, 0, 0, 0)),
                  _const((4, CONV_C)), _const((1, CONV_C)), _const((1, SSM_H)), _const((1, SSM_W)), _const((1, SSM_W)), _const((1, SSM_W)),
                  _const((SSM_H, SSM_W)), _const((128, SSM_W))],
        out_specs=[pl.BlockSpec((L, CONV_C), rev), pl.BlockSpec((L, SSM_W), rev), pl.BlockSpec((L, SSM_H), rev),
                   _const((8, CONV_C)), _const((1, CONV_C)), _const((1, SSM_W)), _const((8, SSM_H))],
        out_shape=[jax.ShapeDtypeStruct((T, CONV_C), F32), jax.ShapeDtypeStruct((T, SSM_W), F32), jax.ShapeDtypeStruct((T, SSM_H), F32),
                   jax.ShapeDtypeStruct((8, CONV_C), F32), jax.ShapeDtypeStruct((1, CONV_C), F32), jax.ShapeDtypeStruct((1, SSM_W), F32),
                   jax.ShapeDtypeStruct((8, SSM_H), F32)],
        scratch=[pltpu.VMEM((8, SSM_N, 128), F32), pltpu.VMEM((8, CONV_C), F32), pltpu.VMEM((L, 128), F32)],
        args=[dmix, xbc, xbc, z, dtr, y, hs, convw, convb, dtb, alx, dskx, ssmn, e, e1], sem=("arbitrary",), duties=duties)


def _loss_head(x3, target):
    tm = 512

    def body(x_ref, t_ref, dy_ref, ss_ref):
        @pl.when(pl.program_id(0) == 0)
        def _():
            ss_ref[...] = jnp.zeros_like(ss_ref)

        err = x_ref[...] - t_ref[...]
        dy_ref[...] = err * (1.0 / D)
        ss_ref[...] += jnp.sum(jnp.sum(err * err, axis=1, keepdims=True), axis=0, keepdims=True)

    return pl.pallas_call(
        body, name="loss_head", grid=(T // tm,),
        in_specs=[_rows(tm, D), _rows(tm, D)],
        out_specs=[_rows(tm, D), _const((1, 128))],
        out_shape=[jax.ShapeDtypeStruct((T, D), F32), jax.ShapeDtypeStruct((1, 128), F32)],
        compiler_params=_params(("arbitrary",)),
    )(x3, target)


def _adam_math(w, g, m, v):
    m = ADAM_B1 * m + (1.0 - ADAM_B1) * g
    v = ADAM_B2 * v + (1.0 - ADAM_B2) * (g * g)
    m_hat = m / (1.0 - ADAM_B1 ** ADAM_STEP)
    v_hat = v / (1.0 - ADAM_B2 ** ADAM_STEP)
    delta = -ADAM_LR * (m_hat / (jnp.sqrt(v_hat) + ADAM_EPS) + ADAM_WD * w)
    return delta, m, v


def _adamw(w, m, v, parts, name):
    rows, cols = w.shape
    tr = rows if rows <= 512 else 256
    assert rows % tr == 0
    n_parts = len(parts)

    def body(*refs):
        w_ref, m_ref, v_ref = refs[:3]
        p_refs = refs[3:3 + n_parts]
        g_ref, d_ref, nm_ref, nv_ref = refs[3 + n_parts:]
        g = p_refs[0][...].astype(F32)
        for p in p_refs[1:]:
            g = g + p[...].astype(F32)
        delta, nm, nv = _adam_math(w_ref[...], g, m_ref[...], v_ref[...])
        g_ref[...] = g
        d_ref[...] = delta
        nm_ref[...] = nm
        nv_ref[...] = nv

    spec = pl.BlockSpec((tr, cols), lambda i: (i, 0))
    return pl.pallas_call(
        body, name=name, grid=(rows // tr,),
        in_specs=[spec] * (3 + n_parts), out_specs=[spec] * 4,
        out_shape=[jax.ShapeDtypeStruct((rows, cols), F32)] * 4,
        compiler_params=_params(("parallel",)),
    )(w, m, v, *parts)


def _row_tile(rows):
    return rows if rows <= 512 else 256


def _adamw_sharded(w, m, v, chip_sum, from_chips, chip_ids, name, after=()):
    rows, cols = w.shape
    prow, pcol = chip_sum.shape[1:]
    tr = _row_tile(rows)
    ptr = prow if tr == rows else tr
    assert rows % tr == 0 and ptr >= tr and pcol >= cols

    def body(ids_ref, w_ref, m_ref, v_ref, s_ref, r1_ref, r2_ref, r3_ref, *rest):
        g_ref, d_ref, nm_ref, nv_ref = rest[len(after):]
        g = s_ref[0]
        for r in (r1_ref, r2_ref, r3_ref):
            g = g + r[0].astype(F32)
        g = g[:tr, :cols]
        delta, nm, nv = _adam_math(w_ref[...], g, m_ref[...], v_ref[...])
        g_ref[...] = g
        d_ref[...] = delta
        nm_ref[...] = nm
        nv_ref[...] = nv

    spec = pl.BlockSpec((tr, cols), lambda i, ids: (i, 0))
    part = lambda k: pl.BlockSpec((1, ptr, pcol), lambda i, ids: (ids[k], i, 0))
    return pl.pallas_call(
        body, name=name,
        grid_spec=pltpu.PrefetchScalarGridSpec(
            num_scalar_prefetch=1, grid=(rows // tr,),
            in_specs=[spec, spec, spec, part(0), part(1), part(2), part(3)] + [ANY] * len(after), out_specs=[spec] * 4),
        out_shape=[jax.ShapeDtypeStruct((rows, cols), F32)] * 4,
        compiler_params=_params(("parallel",)),
    )(chip_ids, w, m, v, chip_sum, from_chips, from_chips, from_chips, *after)


def _chip_sum(g32, recv, core, name):
    rows, cols = g32.shape[1:]
    tr = _row_tile(rows)
    assert rows % tr == 0

    def body(core_ref, a_ref, b_ref, s_ref, sb_ref):
        s = a_ref[...] + b_ref[...].astype(F32)
        s_ref[...] = s
        sb_ref[...] = s.astype(BF16)

    by_chip = pl.BlockSpec((1, tr, cols), lambda k, i, core_ref: (k, i, 0))
    return pl.pallas_call(
        body, name=name,
        grid_spec=pltpu.PrefetchScalarGridSpec(
            num_scalar_prefetch=1, grid=(N_DEV // 2, rows // tr),
            in_specs=[pl.BlockSpec((1, tr, cols), lambda k, i, core_ref: (2 * k + core_ref[0], i, 0)), by_chip],
            out_specs=[by_chip, by_chip]),
        out_shape=[jax.ShapeDtypeStruct((N_DEV // 2, rows, cols), F32), jax.ShapeDtypeStruct((N_DEV // 2, rows, cols), BF16)],
        compiler_params=_params(("parallel", "parallel")),
    )(core, g32, recv)


def _all_reduce_small(v):
    rows = v.shape[0]

    def body(v_ref, out_ref, gath, send_sems, recv_sems):
        x, y, c = _place()
        me, sibling = (x, y, c), (x, y, 1 - c)
        chips = [(1 - x, y), (x, 1 - y), (1 - x, 1 - y)]

        def blk(px, py, pc):
            return gath.at[pl.ds((4 * px + 2 * py + pc) * rows, rows), :]

        def copy(k, block, to, src=None):
            return pltpu.make_async_remote_copy(src_ref=blk(*block) if src is None else src, dst_ref=blk(*block),
                                                send_sem=send_sems.at[k], recv_sem=recv_sems.at[k], device_id=to, device_id_type=MESH)

        gath[pl.ds((4 * x + 2 * y + c) * rows, rows), :] = v_ref[...]
        first = [copy(0, me, sibling, src=v_ref)] + [copy(1 + j, me, (*chip, c), src=v_ref) for j, chip in enumerate(chips)]
        for cp in first:
            cp.start()
        passed = [copy(4 + j, (*chip, c), sibling) for j, chip in enumerate(chips)]
        for j, chip in enumerate(chips):
            copy(1 + j, (*chip, c), me).wait_recv()
            passed[j].start()
        copy(0, sibling, me).wait_recv()
        for j, chip in enumerate(chips):
            copy(4 + j, (*chip, 1 - c), me).wait_recv()
        for cp in first + passed:
            cp.wait_send()
        acc = gath[0:rows, :]
        for d in range(1, N_DEV):
            acc = acc + gath[d * rows:(d + 1) * rows, :]
        out_ref[...] = acc

    vm = pl.BlockSpec(memory_space=pltpu.VMEM)
    return pl.pallas_call(
        body, name="all_reduce_small",
        in_specs=[vm], out_specs=vm,
        out_shape=jax.ShapeDtypeStruct(v.shape, F32),
        scratch_shapes=[pltpu.VMEM((N_DEV * rows, 128), F32), pltpu.SemaphoreType.DMA((7,)), pltpu.SemaphoreType.DMA((7,))],
    )(v)


def _rope_tables(positions):
    inv_freq = ROPE_THETA ** (-jnp.arange(0, HD, 2, dtype=F32) / HD)
    ang = positions.reshape(T).astype(F32)[:, None] * inv_freq
    ang = jnp.concatenate([ang, ang, ang, ang], axis=-1)
    lo_half = (jnp.arange(128) % HD) < (HD // 2)
    return jnp.cos(ang), jnp.where(lo_half, -jnp.sin(ang), jnp.sin(ang))


def _selectors():
    lane = jnp.arange(QW)
    e = (lane[None, :] // HD == jnp.arange(SSM_H)[:, None]).astype(F32)
    e1 = ((lane[None, :] == HD * jnp.arange(128)[:, None]) & (jnp.arange(128)[:, None] < SSM_H)).astype(F32)
    src = jnp.arange(KVW)
    ex = ((lane[None, :] // (HD * (NQ // NKV)) == src[:, None] // HD) & (lane[None, :] % HD == src[:, None] % HD)).astype(F32)
    return e, e1, ex


WEIGHTS = ['ffn1_pre_norm', 'ffn1_w_gate', 'ffn1_w_up', 'ffn1_w_down', 'ffn1_post_norm', 'mix_pre_norm', 'w_in', 'conv_w', 'conv_b',
           'dt_bias', 'a_log', 'd_skip', 'ssm_norm', 'w_out', 'mix_post_norm', 'ffn2_pre_norm', 'ffn2_w_gate', 'ffn2_w_up',
           'ffn2_w_down', 'ffn2_post_norm']
COL_SHARDED = ['ffn1_w_gate', 'ffn1_w_up', 'ffn2_w_gate', 'ffn2_w_up', 'w_in']
ROW_SHARDED = ['ffn1_w_down', 'ffn2_w_down', 'w_out']
BIG = COL_SHARDED + ROW_SHARDED
SMALL = ['ffn1_pre_norm', 'ffn1_post_norm', 'mix_pre_norm', 'conv_b', 'dt_bias', 'a_log', 'd_skip', 'ssm_norm', 'mix_post_norm',
         'ffn2_pre_norm', 'ffn2_post_norm']
FFN1 = ['ffn1_w_gate', 'ffn1_w_up', 'ffn1_w_down']
FFN2 = ['ffn2_w_gate', 'ffn2_w_up', 'ffn2_w_down']


def _wire_block(name, a):
    if name in COL_SHARDED[:4]:
        return jnp.pad(a.astype(BF16), ((0, 0), (0, FSH - FSR)))
    if name in ROW_SHARDED[:2]:
        return jnp.pad(a.astype(BF16), ((0, FSH - FSR), (0, 0)))
    return a if name == "conv_w" else a.astype(BF16)


def _whole_from_gathered(name, a):
    if name in COL_SHARDED[:4]:
        return a
    if name in ("w_in", "conv_w"):
        return jnp.transpose(a, (1, 0, 2)).reshape(a.shape[1], -1)
    return a.reshape(-1, D)


def _step(x, positions, target, small, blocks=None, whole=None, core=None):
    dist = blocks is not None
    w = dict(small)
    if whole:
        w.update(whole)

    def gather(names):
        return [_gather_duty([_wire_block(n, blocks[n]) for n in names], [n in COL_SHARDED[:4] for n in names])] if dist else []

    def put(names, results):
        if dist:
            for n, r in zip(names, results[0]):
                w[n] = _whole_from_gathered(n, r)

    def row_shards(pair):
        return tuple(a.reshape(N_DEV, -1, a.shape[1]) for a in pair)

    g, sums, red = {}, {}, {}

    def swap(names):
        return [_swap_duty([g[n][1] for n in names])] if dist else []

    def chip_sums(names, from_sibling):
        if dist:
            for n, recv in zip(names, from_sibling):
                sums[n] = _chip_sum(g[n][0], recv, core, "chip_sum_" + n)

    def exchange(names):
        return [_exchange_duty([sums[n][1] for n in names])] if dist else []

    def reduced(names, from_chips):
        if dist:
            for n, recv in zip(names, from_chips):
                red[n] = (sums[n][0], recv)

    cos, sin_s = _rope_tables(positions)
    e, e1, exf = _selectors()
    bias = _attn_bias()
    alx = jnp.repeat(w["a_log"], HD, axis=1)
    dskx = jnp.repeat(w["d_skip"], HD, axis=1)

    if dist:
        put(FFN1, _comm_only(gather(FFN1), "gather_ffn1"))
    (x1, n1, a1, b1, hm1, h1), got = _ffn_fwd(x, w["ffn1_pre_norm"], w["ffn1_w_gate"], w["ffn1_w_up"], w["ffn1_w_down"],
                                              w["ffn1_post_norm"], "ffn1_fwd", gather(["w_in", "conv_w"]))
    put(["w_in", "conv_w"], got)
    (n2, q, kx, vx, xbc, z, dtr), got = _inproj_fwd(x1, w["mix_pre_norm"], w["w_in"], cos, sin_s, exf.astype(BF16), gather(["w_out"]))
    put(["w_out"], got)
    (attn, lse), got = _attn_fwd(q, kx, vx, bias, gather(FFN2[:2]))
    put(FFN2[:2], got)
    (yn, y, hs), got = _ssd_fwd(xbc, z, dtr, w["conv_w"], w["conv_b"], w["dt_bias"], alx, dskx, w["ssm_norm"], e, gather(FFN2[2:]))
    put(FFN2[2:], got)
    x2, h2 = _outproj_fwd(x1, attn, yn, w["w_out"], w["mix_post_norm"])
    (x3, n3, a3, b3, hm3, h3), _ = _ffn_fwd(x2, w["ffn2_pre_norm"], w["ffn2_w_gate"], w["ffn2_w_up"], w["ffn2_w_down"],
                                            w["ffn2_post_norm"], "ffn2_fwd")
    dx3, ss = _loss_head(x3, target)

    (dx2, da3, db3, dh3, g["ffn2_pre_norm"], g["ffn2_post_norm"]), _ = _ffn_bwd(
        dx3, x2, a3, b3, h3, w["ffn2_pre_norm"], w["ffn2_post_norm"], w["ffn2_w_gate"], w["ffn2_w_up"], w["ffn2_w_down"], "ffn2_bwd")
    g["ffn2_w_down"] = row_shards(_matmul_tn(hm3, dh3, "ffn2_dwd", tm=FSH))
    g["ffn2_w_gate"] = _matmul_tn_cols(n3, da3, "ffn2_dwg")[0]
    g["ffn2_w_up"] = _matmul_tn_cols(n3, db3, "ffn2_dwu")[0]

    (dh2, dmix, g["mix_post_norm"]), got = _outproj_bwd(dx2, h2, w["mix_post_norm"], w["w_out"], swap(FFN2))
    chip_sums(FFN2, got[0] if dist else None)
    g["w_out"] = _dwout(attn, yn, dh2)
    (dq, dkx, dvx), got = _attn_bwd(q, kx, vx, attn, dmix, lse, bias, exchange(FFN2) + swap(["w_out"]))
    if dist:
        reduced(FFN2, got[0])
        chip_sums(["w_out"], got[1])
    (dxbc, dz, ddt, dcw, g["conv_b"], g["ssm_norm"], dpar), got = _ssd_bwd(
        dmix, xbc, z, dtr, y, hs, w["conv_w"], w["conv_b"], w["dt_bias"], alx, dskx, w["ssm_norm"], e, e1, exchange(["w_out"]))
    reduced(["w_out"], got[0] if dist else None)
    g["conv_w"] = dcw[0:4]
    g["dt_bias"], g["a_log"], g["d_skip"] = dpar[0:1], dpar[1:2], dpar[2:3]
    dx1, dproj, g["mix_pre_norm"] = _inproj_bwd(dx2, dq, dkx, dvx, dxbc, dz, ddt, x1, w["mix_pre_norm"], w["w_in"], cos, sin_s, exf)
    g["w_in"] = tuple(jnp.transpose(a.reshape(D, N_DEV, -1), (1, 0, 2)) for a in _matmul_tn(n2, dproj, "dwin"))
    if dist:
        chip_sums(["w_in"], _comm_only(swap(["w_in"]), "swap_w_in")[0])

    (dx0, da1, db1, dh1, g["ffn1_pre_norm"], g["ffn1_post_norm"]), got = _ffn_bwd(
        dx1, x, a1, b1, h1, w["ffn1_pre_norm"], w["ffn1_post_norm"], w["ffn1_w_gate"], w["ffn1_w_up"], w["ffn1_w_down"], "ffn1_bwd",
        exchange(["w_in"]))
    reduced(["w_in"], got[0] if dist else None)
    started = {}
    g["ffn1_w_down"] = row_shards(_matmul_tn(hm1, dh1, "ffn1_dwd", tm=FSH))
    g["ffn1_w_gate"], got = _matmul_tn_cols(n1, da1, "ffn1_dwg", duties=swap(["ffn1_w_down"]))
    if dist:
        chip_sums(["ffn1_w_down"], got[0])
        started["ffn1_w_down"] = _exchange_start(sums["ffn1_w_down"][1], "start_exchange_ffn1_w_down")
    after = [started["ffn1_w_down"]["token"]] if dist else []
    g["ffn1_w_up"], got = _matmul_tn_cols(n1, db1, "ffn1_dwu", after=after, duties=swap(["ffn1_w_gate"]))
    if dist:
        chip_sums(["ffn1_w_gate"], got[0])
        started["ffn1_w_gate"] = _exchange_start(sums["ffn1_w_gate"][1], "start_exchange_ffn1_w_gate")
        chip_sums(["ffn1_w_up"], _comm_only(swap(["ffn1_w_up"]), "swap_ffn1_w_up", after=[started["ffn1_w_gate"]["token"]])[0])
        started["ffn1_w_up"] = _exchange_start(sums["ffn1_w_up"][1], "start_exchange_ffn1_w_up")
    return ss, dx0, g, red, {n: (sums[n][0], started[n]) for n in started}


def kernel(x, positions, ffn1_pre_norm, ffn1_w_gate, ffn1_w_up, ffn1_w_down, ffn1_post_norm, mix_pre_norm, w_in, conv_w, conv_b, dt_bias, a_log, d_skip, ssm_norm, w_out, mix_post_norm, ffn2_pre_norm, ffn2_w_gate, ffn2_w_up, ffn2_w_down, ffn2_post_norm, loss_target, m_ffn1_pre_norm, m_ffn1_w_gate, m_ffn1_w_up, m_ffn1_w_down, m_ffn1_post_norm, m_mix_pre_norm, m_w_in, m_conv_w, m_conv_b, m_dt_bias, m_a_log, m_d_skip, m_ssm_norm, m_w_out, m_mix_post_norm, m_ffn2_pre_norm, m_ffn2_w_gate, m_ffn2_w_up, m_ffn2_w_down, m_ffn2_post_norm, v_ffn1_pre_norm, v_ffn1_w_gate, v_ffn1_w_up, v_ffn1_w_down, v_ffn1_post_norm, v_mix_pre_norm, v_w_in, v_conv_w, v_conv_b, v_dt_bias, v_a_log, v_d_skip, v_ssm_norm, v_w_out, v_mix_post_norm, v_ffn2_pre_norm, v_ffn2_w_gate, v_ffn2_w_up, v_ffn2_w_down, v_ffn2_post_norm):
    given = dict(locals())
    drop = lambda n, a: a if n in SMALL else a[0]
    w = {n: drop(n, given[n]) for n in WEIGHTS}
    m = {n: drop(n, given["m_" + n]) for n in WEIGHTS}
    v = {n: drop(n, given["v_" + n]) for n in WEIGHTS}
    cx, cy, cc = _place()
    my_chip = 2 * cx + cy
    others = [2 * (1 - cx) + cy, 2 * cx + (1 - cy), 2 * (1 - cx) + (1 - cy)]

    core = jnp.stack([cc]).astype(jnp.int32)
    ss, grad_x, g, red, pending = _step(x[0], positions, loss_target[0], {n: w[n] for n in SMALL},
                                        blocks={n: w[n] for n in BIG + ["conv_w"]}, core=core)
    loss = lax.psum(0.5 * ss[0, 0] / D, ("x", "y", "c"))
    chip_ids = jnp.stack([my_chip] + others).astype(jnp.int32)
    out_g, out_d, out_m, out_v = {}, {}, {}, {}
    after = [pending["ffn1_w_up"][1]["token"]]
    for n in BIG:
        if n not in pending:
            out_g[n], out_d[n], out_m[n], out_v[n] = _adamw_sharded(w[n], m[n], v[n], red[n][0], red[n][1], chip_ids, "adamw_" + n, after)
            after = []

    small_g = jnp.concatenate([g[n].reshape(-1) for n in SMALL] + [g["conv_w"].reshape(-1)])
    n_small = small_g.shape[0]
    rows = -(-n_small // 128 // 8) * 8
    packed = jnp.pad(small_g, (0, rows * 128 - n_small)).reshape(rows, 128)
    total = _all_reduce_small(packed).reshape(-1)
    sizes = [w[n].size for n in SMALL]
    offs = [0]
    for s_ in sizes:
        offs.append(offs[-1] + s_)
    gs = {n: total[offs[i]:offs[i + 1]].reshape(w[n].shape) for i, n in enumerate(SMALL)}
    gcw = total[offs[-1]:offs[-1] + 4 * CONV_C].reshape(4, CONV_C)
    gs["conv_w"] = lax.dynamic_slice_in_dim(gcw, (4 * cx + 2 * cy + cc) * (CONV_C // N_DEV), CONV_C // N_DEV, axis=1)
    names = SMALL + ["conv_w"]

    def pack(d):
        flat = jnp.concatenate([d[n].reshape(-1) for n in names])
        return jnp.pad(flat, (0, rows * 128 - flat.shape[0])).reshape(rows, 128)

    pg, pd, pm, pv = _adamw(pack(w), pack(m), pack(v), [pack(gs)], "adamw_small")
    for n, (chip_sum, started) in pending.items():
        recv = _exchange_wait(started, pg, "wait_exchange_" + n)
        out_g[n], out_d[n], out_m[n], out_v[n] = _adamw_sharded(w[n], m[n], v[n], chip_sum, recv, chip_ids, "adamw_" + n)
    o2 = [0]
    for n in names:
        o2.append(o2[-1] + w[n].size)
    for i, n in enumerate(names):
        for dst, src in ((out_g, pg), (out_d, pd), (out_m, pm), (out_v, pv)):
            dst[n] = src.reshape(-1)[o2[i]:o2[i + 1]].reshape(w[n].shape)

    outs = [loss, grad_x[None]]
    for d in (out_g, out_d, out_m, out_v):
        outs += [d[n] if n in SMALL else d[n][None] for n in WEIGHTS]
    return tuple(outs)
```

```python
import functools
import math

import jax
import jax.numpy as jnp
from jax import lax
from jax.experimental import pallas as pl
from jax.experimental.pallas import tpu as pltpu

F32 = jnp.float32
BF16 = jnp.bfloat16
HI = lax.Precision.HIGHEST
MESH = pl.DeviceIdType.MESH

N_DEV = 8
T = 2048
D = 1024
FF = 2816
FSR = FF // N_DEV
FSH = 384
FFP = N_DEV * FSH
HD = 64
NQ = 16
NKV = 4
QW = NQ * HD
KVW = NKV * HD
SSM_W = 1024
SSM_H = 16
SSM_N = 128
CONV_C = SSM_W + 2 * 2 * SSM_N
IN_COLS = 4112
INP = 4224
L = 128
NCH = T // L
AB = 256
NAB = T // AB
EPS = 1e-6
NEG = -1e30
ROPE_THETA = 10000.0
DILATIONS = ((128, 1), (512, 4), (2048, 16))

ADAM_LR = 0.001
ADAM_B1 = 0.9
ADAM_B2 = 0.999
ADAM_EPS = 1e-08
ADAM_WD = 0.01
ADAM_STEP = 10

VMEM_LIMIT = 58 * 1024 * 1024


def _params(sem, vmem=VMEM_LIMIT):
    return pltpu.CompilerParams(dimension_semantics=sem, vmem_limit_bytes=vmem)


def _dot(a, b):
    return jnp.dot(a, b, preferred_element_type=F32)


def _dot_nt(a, b):
    return lax.dot_general(a, b, (((1,), (1,)), ((), ())), preferred_element_type=F32)


def _dot_tn(a, b):
    return lax.dot_general(a, b, (((0,), (0,)), ((), ())), preferred_element_type=F32)


def _dot_hi(a, b):
    return jnp.dot(a, b, preferred_element_type=F32, precision=HI)


def _dot_nt_hi(a, b):
    return lax.dot_general(a, b, (((1,), (1,)), ((), ())), preferred_element_type=F32, precision=HI)


def _rs(x):
    return lax.rsqrt(jnp.mean(x * x, axis=-1, keepdims=True) + EPS)


def _sigmoid(x):
    return jax.nn.sigmoid(x)


def _dsilu(x, s):
    return s * (1.0 + x * (1.0 - s))


def _resident(shape):
    nd = len(shape)
    return pl.BlockSpec(shape, lambda *_: (0,) * nd, pipeline_mode=pl.Buffered(1))


def _const(shape):
    nd = len(shape)
    return pl.BlockSpec(shape, lambda *_: (0,) * nd)


def _rows(tm, cols):
    return pl.BlockSpec((tm, cols), lambda i: (i, 0))


ANY = pl.BlockSpec(memory_space=pl.ANY)


def _place():
    return lax.axis_index("x"), lax.axis_index("y"), lax.axis_index("c")


def _gather_duty(arrays):
    n = len(arrays)
    results = [jax.ShapeDtypeStruct((N_DEV,) + a.shape, a.dtype) for a in arrays]

    def make(ins, outs, send_sems, recv_sems, local_sems):
        x, y, c = _place()
        me, sibling = (x, y, c), (x, y, 1 - c)
        chips = [(1 - x, y), (x, 1 - y), (1 - x, 1 - y)]

        def place_of(a, px, py, pc):
            return outs[a].at[4 * px + 2 * py + pc]

        def copy(a, k, block, to, src=None):
            dst = place_of(a, *block)
            return pltpu.make_async_remote_copy(src_ref=dst if src is None else src, dst_ref=dst,
                                                send_sem=send_sems.at[7 * a + k], recv_sem=recv_sems.at[7 * a + k],
                                                device_id=to, device_id_type=MESH)

        def own(a):
            return pltpu.make_async_copy(ins[a], place_of(a, *me), local_sems.at[a])

        def first(a):
            return [copy(a, 0, me, sibling, src=ins[a])] + [copy(a, 1 + j, me, (*chip, c), src=ins[a]) for j, chip in enumerate(chips)]

        def start():
            for a in range(n):
                own(a).start()
            for a in range(n):
                for cp in first(a):
                    cp.start()

        def finish():
            for j, chip in enumerate(chips):
                for a in range(n):
                    copy(a, 1 + j, (*chip, c), me).wait_recv()
                    copy(a, 4 + j, (*chip, c), sibling).start()
            for a in range(n):
                copy(a, 0, sibling, me).wait_recv()
                for j, chip in enumerate(chips):
                    copy(a, 4 + j, (*chip, 1 - c), me).wait_recv()
            for a in range(n):
                for cp in first(a) + [copy(a, 4 + j, (*chip, c), sibling) for j, chip in enumerate(chips)]:
                    cp.wait_send()
                own(a).wait()

        return start, finish

    return dict(operands=list(arrays), results=results, sems=(7 * n, 7 * n, n), make=make)


def _swap_duty(arrays):
    n = len(arrays)
    half = N_DEV // 2
    results = [jax.ShapeDtypeStruct((half,) + a.shape[1:], a.dtype) for a in arrays]

    def make(ins, outs, send_sems, recv_sems):
        x, y, c = _place()

        def copies():
            return [pltpu.make_async_remote_copy(src_ref=ins[a].at[2 * k + (1 - c)], dst_ref=outs[a].at[k],
                                                 send_sem=send_sems.at[half * a + k], recv_sem=recv_sems.at[half * a + k],
                                                 device_id=(x, y, 1 - c), device_id_type=MESH)
                    for a in range(n) for k in range(half)]

        def start():
            for cp in copies():
                cp.start()

        def finish():
            for cp in copies():
                cp.wait()

        return start, finish

    return dict(operands=list(arrays), results=results, sems=(half * n, half * n), make=make)


def _exchange_duty(arrays):
    n = len(arrays)
    results = [jax.ShapeDtypeStruct(a.shape, a.dtype) for a in arrays]

    def make(ins, outs, send_sems, recv_sems):
        x, y, c = _place()
        chips = [(1 - x, y), (x, 1 - y), (1 - x, 1 - y)]
        my_chip = 2 * x + y

        def sends():
            return [pltpu.make_async_remote_copy(src_ref=ins[a].at[2 * px + py], dst_ref=outs[a].at[my_chip],
                                                 send_sem=send_sems.at[3 * a + j], recv_sem=recv_sems.at[3 * a + j],
                                                 device_id=(px, py, c), device_id_type=MESH)
                    for a in range(n) for j, (px, py) in enumerate(chips)]

        def start():
            for cp in sends():
                cp.start()

        def finish():
            for a in range(n):
                for j, (px, py) in enumerate(chips):
                    pltpu.make_async_remote_copy(src_ref=ins[a].at[my_chip], dst_ref=outs[a].at[2 * px + py],
                                                 send_sem=send_sems.at[3 * a + j], recv_sem=recv_sems.at[3 * a + j],
                                                 device_id=(px, py, c), device_id_type=MESH).wait_recv()
            for cp in sends():
                cp.wait_send()

        return start, finish

    return dict(operands=list(arrays), results=results, sems=(3 * n, 3 * n), make=make)


def _call(body, *, name, grid, in_specs, out_specs, out_shape, args, sem, scratch=(), duties=(), after=()):
    n_in, n_out, n_scr = len(in_specs), len(out_specs), len(scratch)
    sem_shapes = [pltpu.SemaphoreType.DMA((k,)) for d in duties for k in d["sems"]]

    def full(*refs):
        pos = [0]

        def take(k):
            pos[0] += k
            return refs[pos[0] - k:pos[0]]

        ins = take(n_in)
        d_ins = [take(len(d["operands"])) for d in duties]
        take(len(after))
        outs = take(n_out)
        d_outs = [take(len(d["results"])) for d in duties]
        scr = take(n_scr)
        d_sems = [take(len(d["sems"])) for d in duties]
        hooks = [d["make"](di, do, *ds) for d, di, do, ds in zip(duties, d_ins, d_outs, d_sems)]
        if grid and hooks:
            ids = [pl.program_id(k) for k in range(len(grid))]
            first = functools.reduce(jnp.logical_and, [i == 0 for i in ids])
            last = functools.reduce(jnp.logical_and, [i == g - 1 for i, g in zip(ids, grid)])

            @pl.when(first)
            def _():
                for start, _ in hooks:
                    start()

            body(*ins, *outs, *scr)

            @pl.when(last)
            def _():
                for _, finish in hooks:
                    finish()
        else:
            for start, _ in hooks:
                start()
            body(*ins, *outs, *scr)
            for _, finish in hooks:
                finish()

    d_args = [a for d in duties for a in d["operands"]]
    d_res = [r for d in duties for r in d["results"]]
    kwargs = dict(grid=grid) if grid else {}
    res = pl.pallas_call(
        full, name=name, in_specs=list(in_specs) + [ANY] * (len(d_args) + len(after)), out_specs=list(out_specs) + [ANY] * len(d_res),
        out_shape=list(out_shape) + d_res, scratch_shapes=list(scratch) + sem_shapes,
        compiler_params=_params(sem) if grid else None, **kwargs,
    )(*args, *d_args, *after)
    own, rest = list(res[:n_out]), list(res[n_out:])
    by_duty = []
    for d in duties:
        by_duty.append(rest[:len(d["results"])])
        rest = rest[len(d["results"]):]
    return own, by_duty


def _comm_only(duties, name, after=()):
    return _call(lambda: None, name=name, grid=None, in_specs=[], out_specs=[], out_shape=[], args=[], sem=None, duties=duties,
                 after=after)[1]


HBM = pl.BlockSpec(memory_space=pltpu.HBM)
SEMS = pl.BlockSpec(memory_space=pltpu.SEMAPHORE)
SIDE_EFFECT = pltpu.SideEffectType.DATAFLOW_SIDE_EFFECTING
N_OTHER_CHIPS = 3


def _chip_copies(src_ref, land_ref, sems):
    x, y, c = _place()
    chips = [(1 - x, y), (x, 1 - y), (1 - x, 1 - y)]
    return [pltpu.make_async_remote_copy(src_ref=src_ref.at[2 * px + py], dst_ref=land_ref.at[2 * x + y],
                                         send_sem=sems[j], recv_sem=sems[N_OTHER_CHIPS + j], device_id=(px, py, c), device_id_type=MESH)
            for j, (px, py) in enumerate(chips)]


def _exchange_start(pb, name):
    n_sem = 2 * N_OTHER_CHIPS

    def body(pb_ref, land_ref, *rest):
        for cp in _chip_copies(pb_ref, land_ref, rest[:n_sem]):
            cp.start()
        token = rest[n_sem + 2]
        token[...] = jnp.zeros_like(token)

    res = pl.pallas_call(
        body, name=name,
        out_shape=(pltpu.SemaphoreType.DMA(()),) * n_sem + (pltpu.HBM(pb.shape, pb.dtype), pltpu.HBM(pb.shape, pb.dtype),
                                                              jax.ShapeDtypeStruct((8, 128), F32)),
        in_specs=(HBM, HBM), out_specs=(SEMS,) * n_sem + (HBM, HBM, pl.BlockSpec(memory_space=pltpu.VMEM)),
        input_output_aliases={0: n_sem, 1: n_sem + 1},
        compiler_params=pltpu.CompilerParams(has_side_effects=SIDE_EFFECT),
    )(pltpu.with_memory_space_constraint(pb, pltpu.HBM), pltpu.with_memory_space_constraint(lax.empty(pb.shape, pb.dtype), pltpu.HBM))
    return dict(sems=res[:n_sem], src=res[n_sem], land=res[n_sem + 1], token=res[n_sem + 2])


def _exchange_wait(started, after, name):
    n_sem = 2 * N_OTHER_CHIPS

    def body(pb_ref, land_ref, *rest):
        for cp in _chip_copies(pb_ref, land_ref, rest[:n_sem]):
            cp.wait_send()
            cp.wait_recv()

    src, land = started["src"], started["land"]
    return pl.pallas_call(
        body, name=name, out_shape=(pltpu.HBM(src.shape, src.dtype), pltpu.HBM(land.shape, land.dtype)),
        in_specs=(HBM, HBM) + (SEMS,) * n_sem + (ANY,) * len(after), out_specs=(HBM, HBM), input_output_aliases={0: 0, 1: 1},
        compiler_params=pltpu.CompilerParams(has_side_effects=SIDE_EFFECT),
    )(src, land, *started["sems"], *after)[1]


def _ffn_fwd(x, gpre, wg, wu, wd, gpost, name, duties=()):
    tm = 256

    def body(x_ref, gpre_ref, wg_ref, wu_ref, wd_ref, gpost_ref, xo_ref, n_ref, a_ref, b_ref, hm_ref, h_ref):
        xv = x_ref[...]
        n = (xv * _rs(xv) * gpre_ref[...]).astype(BF16)
        a = _dot_nt(n, wg_ref[...])
        b = _dot_nt(n, wu_ref[...])
        hm = (a * _sigmoid(a) * b).astype(BF16)
        h = _dot(hm, wd_ref[...])
        xo_ref[...] = xv + 0.5 * (h * _rs(h) * gpost_ref[...])
        n_ref[...] = n
        a_ref[...] = a.astype(BF16)
        b_ref[...] = b.astype(BF16)
        hm_ref[...] = hm
        h_ref[...] = h

    return _call(
        body, name=name, grid=(T // tm,),
        in_specs=[_rows(tm, D), _const((1, D)), _resident((FFP, D)), _resident((FFP, D)), _resident((FFP, D)), _const((1, D))],
        out_specs=[_rows(tm, D), _rows(tm, D), _rows(tm, FFP), _rows(tm, FFP), _rows(tm, FFP), _rows(tm, D)],
        out_shape=[jax.ShapeDtypeStruct((T, D), F32), jax.ShapeDtypeStruct((T, D), BF16), jax.ShapeDtypeStruct((T, FFP), BF16),
                   jax.ShapeDtypeStruct((T, FFP), BF16), jax.ShapeDtypeStruct((T, FFP), BF16), jax.ShapeDtypeStruct((T, D), F32)],
        args=[x, gpre, wg, wu, wd, gpost], sem=("arbitrary",), duties=duties)


def _ffn_bwd(dxo, x, a, b, h, gpre, gpost, wg, wu, wd, name, after=()):
    tm = 256

    def body(dxo_ref, x_ref, a_ref, b_ref, h_ref, gpre_ref, gpost_ref, wg_ref, wu_ref, wd_ref,
             dx_ref, da_ref, db_ref, dh_ref, dgpre_ref, dgpost_ref):
        @pl.when(pl.program_id(0) == 0)
        def _():
            dgpre_ref[...] = jnp.zeros_like(dgpre_ref)
            dgpost_ref[...] = jnp.zeros_like(dgpost_ref)

        dy = dxo_ref[...]
        h = h_ref[...]
        hn = h * _rs(h)
        r2 = _rs(h)
        dgpost_ref[...] += jnp.sum(0.5 * dy * hn, axis=0, keepdims=True)
        gdy = 0.5 * dy * gpost_ref[...]
        dh = r2 * (gdy - hn * jnp.mean(gdy * hn, axis=-1, keepdims=True))
        dhb = dh.astype(BF16)
        dh_ref[...] = dhb
        dhm = _dot_nt(dhb, wd_ref[...])
        av = a_ref[...].astype(F32)
        bv = b_ref[...].astype(F32)
        sg = _sigmoid(av)
        db = (dhm * (av * sg)).astype(BF16)
        da = (dhm * bv * _dsilu(av, sg)).astype(BF16)
        da_ref[...] = da
        db_ref[...] = db
        dn = _dot(da, wg_ref[...]) + _dot(db, wu_ref[...])
        xv = x_ref[...]
        r = _rs(xv)
        xn = xv * r
        dgpre_ref[...] += jnp.sum(dn * xn, axis=0, keepdims=True)
        gdn = dn * gpre_ref[...]
        dx_ref[...] = dy + r * (gdn - xn * jnp.mean(gdn * xn, axis=-1, keepdims=True))

    return _call(
        body, name=name, grid=(T // tm,),
        in_specs=[_rows(tm, D), _rows(tm, D), _rows(tm, FFP), _rows(tm, FFP), _rows(tm, D), _const((1, D)), _const((1, D)),
                  _resident((FFP, D)), _resident((FFP, D)), _resident((FFP, D))],
        out_specs=[_rows(tm, D), _rows(tm, FFP), _rows(tm, FFP), _rows(tm, D), _const((1, D)), _const((1, D))],
        out_shape=[jax.ShapeDtypeStruct((T, D), F32), jax.ShapeDtypeStruct((T, FFP), BF16), jax.ShapeDtypeStruct((T, FFP), BF16),
                   jax.ShapeDtypeStruct((T, D), BF16), jax.ShapeDtypeStruct((1, D), F32), jax.ShapeDtypeStruct((1, D), F32)],
        args=[dxo, x, a, b, h, gpre, gpost, wg, wu, wd], sem=("arbitrary",), after=after)


def _matmul_tn(a, b, name, tm=FSH, after=(), duties=()):
    k, m = a.shape
    n = b.shape[1]
    assert m % tm == 0

    def body(a_ref, b_ref, o_ref, ob_ref):
        r = _dot_tn(a_ref[...], b_ref[...])
        o_ref[...] = r
        ob_ref[...] = r.astype(BF16)

    spec = pl.BlockSpec((tm, n), lambda i: (i, 0))
    return _call(
        body, name=name, grid=(m // tm,),
        in_specs=[pl.BlockSpec((k, tm), lambda i: (0, i)), _resident((k, n))],
        out_specs=[spec, spec],
        out_shape=[jax.ShapeDtypeStruct((m, n), F32), jax.ShapeDtypeStruct((m, n), BF16)],
        args=[a, b], sem=("arbitrary",), duties=duties, after=after)


def _dwout(attn, yn, dh2):
    rs = (QW + SSM_W) // N_DEV
    half = N_DEV // 2

    def body(at_ref, yn_ref, dh_ref, o_ref, ob_ref):
        i = pl.program_id(0)

        def put(r):
            o_ref[0] = r
            ob_ref[0] = r.astype(BF16)

        @pl.when(i < half)
        def _():
            put(_dot_tn(at_ref[...], dh_ref[...]))

        @pl.when(i >= half)
        def _():
            put(_dot_tn(yn_ref[...], dh_ref[...]))

    spec = pl.BlockSpec((1, rs, D), lambda i: (i, 0, 0))
    return pl.pallas_call(
        body, name="dwout", grid=(N_DEV,),
        in_specs=[pl.BlockSpec((T, rs), lambda i: (0, jnp.minimum(i, half - 1))),
                  pl.BlockSpec((T, rs), lambda i: (0, jnp.maximum(i - half, 0))), _resident((T, D))],
        out_specs=[spec, spec],
        out_shape=[jax.ShapeDtypeStruct((N_DEV, rs, D), F32), jax.ShapeDtypeStruct((N_DEV, rs, D), BF16)],
        compiler_params=_params(("parallel",)),
    )(attn, yn, dh2)


def _rope_swap(t, lo_half):
    return jnp.where(lo_half, pltpu.roll(t, 96, 1), pltpu.roll(t, 32, 1))


def _inproj_fwd(x1, gpre, win, cos, sin_s, ex, duties=()):
    tm = 256

    def body(x_ref, g_ref, w_ref, cos_ref, sin_ref, ex_ref, n_ref, q_ref, kx_ref, vx_ref, xbc_ref, z_ref, dt_ref):
        xv = x_ref[...]
        n = (xv * _rs(xv) * g_ref[...]).astype(BF16)
        n_ref[...] = n
        proj = _dot_nt(n, w_ref[...])
        cs = cos_ref[...]
        sn = sin_ref[...]
        lo_half = (lax.broadcasted_iota(jnp.int32, (1, 128), 1) % HD) < (HD // 2)

        def rope(t):
            return t * cs + _rope_swap(t, lo_half) * sn

        for j in range(QW // 128):
            t = proj[:, 128 * j:128 * j + 128]
            q_ref[:, 128 * j:128 * j + 128] = (rope(t) * (HD ** -0.5)).astype(BF16)
        k = jnp.concatenate([rope(proj[:, QW + 128 * j:QW + 128 * j + 128]) for j in range(KVW // 128)], axis=1)
        v = proj[:, QW + KVW:QW + 2 * KVW]
        kx_ref[...] = _dot(k.astype(BF16), ex_ref[...]).astype(BF16)
        vx_ref[...] = _dot(v.astype(BF16), ex_ref[...]).astype(BF16)
        c0 = QW + 2 * KVW
        xbc_ref[...] = proj[:, c0:c0 + CONV_C]
        z_ref[...] = proj[:, c0 + CONV_C:c0 + CONV_C + SSM_W]
        dt_ref[...] = proj[:, c0 + CONV_C + SSM_W:IN_COLS]

    return _call(
        body, name="inproj_fwd", grid=(T // tm,),
        in_specs=[_rows(tm, D), _const((1, D)), _resident((INP, D)), _rows(tm, 128), _rows(tm, 128), _const((KVW, QW))],
        out_specs=[_rows(tm, D), _rows(tm, QW), _rows(tm, QW), _rows(tm, QW), _rows(tm, CONV_C), _rows(tm, SSM_W), _rows(tm, SSM_H)],
        out_shape=[jax.ShapeDtypeStruct((T, D), BF16), jax.ShapeDtypeStruct((T, QW), BF16), jax.ShapeDtypeStruct((T, QW), BF16),
                   jax.ShapeDtypeStruct((T, QW), BF16), jax.ShapeDtypeStruct((T, CONV_C), F32), jax.ShapeDtypeStruct((T, SSM_W), F32),
                   jax.ShapeDtypeStruct((T, SSM_H), F32)],
        args=[x1, gpre, win, cos, sin_s, ex], sem=("arbitrary",), duties=duties)


def _inproj_bwd(dres, dq, dkx, dvx, dxbc, dz, ddt, x1, gpre, win, cos, sin_s, exf):
    tm = 256

    def body(dres_ref, dq_ref, dkx_ref, dvx_ref, dxbc_ref, dz_ref, ddt_ref, x_ref, g_ref, w_ref, cos_ref, sin_ref, ex_ref,
             dx_ref, dp_ref, dg_ref):
        @pl.when(pl.program_id(0) == 0)
        def _():
            dg_ref[...] = jnp.zeros_like(dg_ref)

        cs = cos_ref[...]
        sn = sin_ref[...]
        lo_half = (lax.broadcasted_iota(jnp.int32, (1, 128), 1) % HD) < (HD // 2)

        def rope_t(t):
            return t * cs - _rope_swap(t, lo_half) * sn

        for j in range(QW // 128):
            dp_ref[:, 128 * j:128 * j + 128] = rope_t(dq_ref[:, 128 * j:128 * j + 128] * (HD ** -0.5)).astype(BF16)
        dk = _dot_nt_hi(dkx_ref[...], ex_ref[...])
        dv = _dot_nt_hi(dvx_ref[...], ex_ref[...])
        for j in range(KVW // 128):
            dp_ref[:, QW + 128 * j:QW + 128 * j + 128] = rope_t(dk[:, 128 * j:128 * j + 128]).astype(BF16)
        dp_ref[:, QW + KVW:QW + 2 * KVW] = dv.astype(BF16)
        c0 = QW + 2 * KVW
        dp_ref[:, c0:c0 + CONV_C] = dxbc_ref[...].astype(BF16)
        dp_ref[:, c0 + CONV_C:c0 + CONV_C + SSM_W] = dz_ref[...].astype(BF16)
        dp_ref[:, c0 + CONV_C + SSM_W:INP] = ddt_ref[...].astype(BF16)
        dn = _dot(dp_ref[...], w_ref[...])
        xv = x_ref[...]
        r = _rs(xv)
        xn = xv * r
        dg_ref[...] += jnp.sum(dn * xn, axis=0, keepdims=True)
        gdn = dn * g_ref[...]
        dx_ref[...] = dres_ref[...] + r * (gdn - xn * jnp.mean(gdn * xn, axis=-1, keepdims=True))

    return pl.pallas_call(
        body, name="inproj_bwd", grid=(T // tm,),
        in_specs=[_rows(tm, D), _rows(tm, QW), _rows(tm, QW), _rows(tm, QW), _rows(tm, CONV_C), _rows(tm, SSM_W), _rows(tm, 128),
                  _rows(tm, D), _const((1, D)), _resident((INP, D)), _rows(tm, 128), _rows(tm, 128), _const((KVW, QW))],
        out_specs=[_rows(tm, D), _rows(tm, INP), _const((1, D))],
        out_shape=[jax.ShapeDtypeStruct((T, D), F32), jax.ShapeDtypeStruct((T, INP), BF16), jax.ShapeDtypeStruct((1, D), F32)],
        compiler_params=_params(("arbitrary",)),
    )(dres, dq, dkx, dvx, dxbc, dz, ddt, x1, gpre, win, cos, sin_s, exf)


def _outproj_fwd(x1, attn, yn, wout, gpost):
    tm = 256

    def body(x_ref, at_ref, yn_ref, w_ref, g_ref, xo_ref, h_ref):
        h = _dot(at_ref[...], w_ref[0:QW, :]) + _dot(yn_ref[...], w_ref[QW:QW + SSM_W, :])
        h_ref[...] = h
        xo_ref[...] = x_ref[...] + h * _rs(h) * g_ref[...]

    return pl.pallas_call(
        body, name="outproj_fwd", grid=(T // tm,),
        in_specs=[_rows(tm, D), _rows(tm, QW), _rows(tm, SSM_W), _resident((QW + SSM_W, D)), _const((1, D))],
        out_specs=[_rows(tm, D), _rows(tm, D)],
        out_shape=[jax.ShapeDtypeStruct((T, D), F32), jax.ShapeDtypeStruct((T, D), F32)],
        compiler_params=_params(("parallel",)),
    )(x1, attn, yn, wout, gpost)


def _outproj_bwd(dx2, h2, gpost, wout, duties=()):
    tm = 256

    def body(dy_ref, h_ref, g_ref, w_ref, dh_ref, dm_ref, dg_ref):
        @pl.when(pl.program_id(0) == 0)
        def _():
            dg_ref[...] = jnp.zeros_like(dg_ref)

        dy = dy_ref[...]
        h = h_ref[...]
        r = _rs(h)
        hn = h * r
        dg_ref[...] += jnp.sum(dy * hn, axis=0, keepdims=True)
        gdy = dy * g_ref[...]
        dh = (r * (gdy - hn * jnp.mean(gdy * hn, axis=-1, keepdims=True))).astype(BF16)
        dh_ref[...] = dh
        dm_ref[...] = _dot_nt(dh, w_ref[...])

    return _call(
        body, name="outproj_bwd", grid=(T // tm,),
        in_specs=[_rows(tm, D), _rows(tm, D), _const((1, D)), _resident((QW + SSM_W, D))],
        out_specs=[_rows(tm, D), _rows(tm, QW + SSM_W), _const((1, D))],
        out_shape=[jax.ShapeDtypeStruct((T, D), BF16), jax.ShapeDtypeStruct((T, QW + SSM_W), F32), jax.ShapeDtypeStruct((1, D), F32)],
        args=[dx2, h2, gpost, wout], sem=("arbitrary",), duties=duties)


def _attn_bias():
    d = jnp.arange(AB)[:, None] - jnp.arange(T)[None, :] + (T - AB)
    cnt = jnp.zeros(d.shape, F32)
    for window, dil in DILATIONS:
        cnt = cnt + ((d >= 0) & (d % dil == 0) & (d <= window)).astype(F32)
    return jnp.where(cnt > 0, jnp.log(jnp.maximum(cnt, 1.0)), NEG)


G_PER = NQ // NKV
WK = G_PER * HD


def _attn_fwd(q, kx, vx, bias, duties=()):
    def body(q_ref, kx_ref, vx_ref, bias_ref, o_ref, lse_ref):
        lane = lax.broadcasted_iota(jnp.int32, (1, WK), 1)
        lse_ref[...] = jnp.zeros_like(lse_ref)
        for i in range(NAB):
            n = (i + 1) * AB
            rows = slice(i * AB, n)
            qi = q_ref[rows, :]
            kxi = kx_ref[0:n, :]
            vxi = vx_ref[0:n, :]
            bb = bias_ref[:, (NAB - 1 - i) * AB:]
            o_acc = jnp.zeros((AB, WK), F32)
            for g in range(G_PER):
                mg = (lane // HD) == g
                s = _dot_nt(jnp.where(mg, qi, jnp.zeros_like(qi)), kxi) + bb
                m = jnp.max(s, axis=1, keepdims=True)
                p = jnp.exp(s - m)
                l = jnp.sum(p, axis=1, keepdims=True)
                o_acc = jnp.where(mg, _dot(p.astype(BF16), vxi) / l, o_acc)
                lse_ref[rows, g:g + 1] = m + jnp.log(l)
            o_ref[rows, :] = o_acc.astype(BF16)

    col = lambda kv: (0, kv)
    return _call(
        body, name="attn_fwd", grid=(NKV,),
        in_specs=[pl.BlockSpec((T, WK), col), pl.BlockSpec((T, WK), col), pl.BlockSpec((T, WK), col), _const((AB, T))],
        out_specs=[pl.BlockSpec((T, WK), col), pl.BlockSpec((T, 128), col)],
        out_shape=[jax.ShapeDtypeStruct((T, QW), BF16), jax.ShapeDtypeStruct((T, NKV * 128), F32)],
        args=[q, kx, vx, bias], sem=("arbitrary",), duties=duties)


def _attn_bwd(q, kx, vx, o, dmix, lse, bias, duties=()):
    def body(q_ref, kx_ref, vx_ref, o_ref, do_ref, lse_ref, bias_ref, dq_ref, dkx_ref, dvx_ref):
        lane = lax.broadcasted_iota(jnp.int32, (1, WK), 1)
        dkx_ref[...] = jnp.zeros_like(dkx_ref)
        dvx_ref[...] = jnp.zeros_like(dvx_ref)
        for i in range(NAB):
            n = (i + 1) * AB
            rows = slice(i * AB, n)
            qi = q_ref[rows, :]
            dof = do_ref[rows, :]
            doi = dof.astype(BF16)
            prod = dof * o_ref[rows, :].astype(F32)
            kxi = kx_ref[0:n, :]
            vxi = vx_ref[0:n, :]
            bb = bias_ref[:, (NAB - 1 - i) * AB:]
            dq_acc = jnp.zeros((AB, WK), F32)
            for g in range(G_PER):
                mg = (lane // HD) == g
                qm = jnp.where(mg, qi, jnp.zeros_like(qi))
                dom = jnp.where(mg, doi, jnp.zeros_like(doi))
                delta = jnp.sum(jnp.where(mg, prod, 0.0), axis=1, keepdims=True)
                p = jnp.exp(_dot_nt(qm, kxi) + bb - lse_ref[rows, g:g + 1])
                ds = (p * (_dot_nt(dom, vxi) - delta)).astype(BF16)
                dvx_ref[0:n, :] += _dot_tn(p.astype(BF16), dom)
                dkx_ref[0:n, :] += _dot_tn(ds, qm)
                dq_acc = jnp.where(mg, _dot(ds, kxi), dq_acc)
            dq_ref[rows, :] = dq_acc

    col = lambda kv: (0, kv)
    return _call(
        body, name="attn_bwd", grid=(NKV,),
        in_specs=[pl.BlockSpec((T, WK), col), pl.BlockSpec((T, WK), col), pl.BlockSpec((T, WK), col), pl.BlockSpec((T, WK), col),
                  pl.BlockSpec((T, WK), col), pl.BlockSpec((T, 128), col), _const((AB, T))],
        out_specs=[pl.BlockSpec((T, WK), col), pl.BlockSpec((T, WK), col), pl.BlockSpec((T, WK), col)],
        out_shape=[jax.ShapeDtypeStruct((T, QW), F32)] * 3,
        args=[q, kx, vx, o, dmix, lse, bias], sem=("arbitrary",), duties=duties)


def _softplus(x):
    return jnp.maximum(x, 0.0) + jnp.log1p(jnp.exp(-jnp.abs(x)))


def _conv_taps(u, halo):
    zext = jnp.concatenate([halo, u], axis=0)
    return [pltpu.roll(zext, m, 0)[8:] for m in (1, 2, 3)]


def _ssd_chunk_common(u, halo, dtr, cw_ref, cb_ref, dtb_ref, alx_ref, e_ref):
    sh1, sh2, sh3 = _conv_taps(u, halo)
    xc = cb_ref[...] + cw_ref[3:4, :] * u + cw_ref[2:3, :] * sh1 + cw_ref[1:2, :] * sh2 + cw_ref[0:1, :] * sh3
    sg = _sigmoid(xc)
    act = xc * sg
    pre_x = _dot_hi(dtr + dtb_ref[...], e_ref[...])
    dt_x = _softplus(pre_x)
    a_x = -jnp.exp(alx_ref[...])
    ri = lax.broadcasted_iota(jnp.int32, (L, L), 0)
    ci = lax.broadcasted_iota(jnp.int32, (L, L), 1)
    tri = ri >= ci
    acs_x = _dot_hi(tri.astype(F32), dt_x * a_x)
    return dict(sh=(sh1, sh2, sh3), xc=xc, sg=sg, act=act, pre_x=pre_x, dt_x=dt_x, a_x=a_x, tri=tri, acs_x=acs_x)


def _decay(acs_x, acs_t, h, tri):
    col = acs_x[:, HD * h:HD * h + 1]
    row = acs_t[HD * h:HD * h + 1, :]
    return jnp.exp(jnp.where(tri, col - row, NEG))


def _ssd_fwd(xbc, z, dtr, convw, convb, dtb, alx, dskx, ssmn, e, duties=()):
    def body(u_ref, z_ref, dtr_ref, cw_ref, cb_ref, dtb_ref, alx_ref, dsk_ref, sn_ref, e_ref,
             yn_ref, y_ref, hs_ref, halo, hst):
        @pl.when(pl.program_id(0) == 0)
        def _():
            halo[...] = jnp.zeros_like(halo)
            hst[...] = jnp.zeros_like(hst)

        u = u_ref[...]
        cm = _ssd_chunk_common(u, halo[...], dtr_ref[...], cw_ref, cb_ref, dtb_ref, alx_ref, e_ref)
        halo[...] = u[L - 8:, :]
        act, dt_x, acs_x, tri = cm["act"], cm["dt_x"], cm["acs_x"], cm["tri"]
        xs = act[:, :SSM_W]
        acs_l = acs_x[L - 1:L, :]
        lam_x = jnp.exp(acs_x)
        w_x = jnp.exp(acs_l - acs_x)
        gam_x = jnp.exp(acs_l)
        acs_t = acs_x.T
        xd = xs * dt_x
        xb = xd.astype(BF16)
        xw = (xd * w_x).astype(BF16)
        lo = lax.broadcasted_iota(jnp.int32, (1, 128), 1) < HD
        hs_ref[0] = hst[...]
        pieces = []
        for grp in range(2):
            bb = act[:, SSM_W + SSM_N * grp:SSM_W + SSM_N * (grp + 1)].astype(BF16)
            cb_ = act[:, SSM_W + 2 * SSM_N + SSM_N * grp:SSM_W + 2 * SSM_N + SSM_N * (grp + 1)].astype(BF16)
            cbm = _dot_nt(cb_, bb)
            for jj in range(4):
                j = 4 * grp + jj
                sl = slice(128 * j, 128 * j + 128)
                m0 = (cbm * _decay(acs_x, acs_t, 2 * j, tri)).astype(BF16)
                m1 = (cbm * _decay(acs_x, acs_t, 2 * j + 1, tri)).astype(BF16)
                x2 = xb[:, sl]
                ydiag = jnp.where(lo, _dot(m0, x2), _dot(m1, x2))
                hprev = hst[j]
                yoff = lam_x[:, sl] * _dot(cb_, hprev.astype(BF16))
                pieces.append(ydiag + yoff)
                hst[j] = gam_x[:, sl] * hprev + _dot_tn(bb, xw[:, sl])
        y = jnp.concatenate(pieces, axis=1) + dsk_ref[...] * xs
        y_ref[...] = y
        zv = z_ref[...]
        yz = y * (zv * _sigmoid(zv))
        half = SSM_W // 2
        yn = jnp.concatenate([yz[:, :half] * _rs(yz[:, :half]), yz[:, half:] * _rs(yz[:, half:])], axis=1)
        yn_ref[...] = (yn * sn_ref[...]).astype(BF16)

    return _call(
        body, name="ssd_fwd", grid=(NCH,),
        in_specs=[_rows(L, CONV_C), _rows(L, SSM_W), _rows(L, SSM_H), _const((4, CONV_C)), _const((1, CONV_C)), _const((1, SSM_H)),
                  _const((1, SSM_W)), _const((1, SSM_W)), _const((1, SSM_W)), _const((SSM_H, SSM_W))],
        out_specs=[_rows(L, SSM_W), _rows(L, SSM_W), pl.BlockSpec((1, 8, SSM_N, 128), lambda c: (c, 0, 0, 0))],
        out_shape=[jax.ShapeDtypeStruct((T, SSM_W), BF16), jax.ShapeDtypeStruct((T, SSM_W), F32),
                   jax.ShapeDtypeStruct((NCH, 8, SSM_N, 128), F32)],
        scratch=[pltpu.VMEM((8, CONV_C), F32), pltpu.VMEM((8, SSM_N, 128), F32)],
        args=[xbc, z, dtr, convw, convb, dtb, alx, dskx, ssmn, e], sem=("arbitrary",), duties=duties)


def _ssd_bwd(dmix, xbc, z, dtr, y, hs, convw, convb, dtb, alx, dskx, ssmn, e, e1, duties=()):
    rev = lambda i: (NCH - 1 - i, 0)

    def body(dyn_ref, u_ref, uh_ref, z_ref, dtr_ref, y_ref, hs_ref, cw_ref, cb_ref, dtb_ref, alx_ref, dsk_ref, sn_ref, e_ref, e1_ref,
             dxbc_ref, dz_ref, ddt_ref, dcw_ref, dcb_ref, dsn_ref, dpar_ref, dh, duh, colbuf):
        step = pl.program_id(0)
        c = NCH - 1 - step

        @pl.when(step == 0)
        def _():
            for r in (dh, duh, dcw_ref, dcb_ref, dsn_ref, dpar_ref):
                r[...] = jnp.zeros_like(r)

        u = u_ref[...]
        halo = jnp.where(c > 0, uh_ref[...], 0.0)
        cm = _ssd_chunk_common(u, halo, dtr_ref[...], cw_ref, cb_ref, dtb_ref, alx_ref, e_ref)
        sh1, sh2, sh3 = cm["sh"]
        xc, sg, act, pre_x, dt_x, a_x, tri, acs_x = (cm[k] for k in ("xc", "sg", "act", "pre_x", "dt_x", "a_x", "tri", "acs_x"))
        xs = act[:, :SSM_W]
        acs_l = acs_x[L - 1:L, :]
        lam_x = jnp.exp(acs_x)
        w_x = jnp.exp(acs_l - acs_x)
        gam_x = jnp.exp(acs_l)
        acs_t = acs_x.T
        xd = xs * dt_x
        xb = xd.astype(BF16)
        xdw = xd * w_x
        xw = xdw.astype(BF16)
        lo = lax.broadcasted_iota(jnp.int32, (1, 128), 1) < HD
        row8 = lax.broadcasted_iota(jnp.int32, (8, 1), 0)

        dyn = dyn_ref[...]
        yv = y_ref[...]
        zv = z_ref[...]
        sz = _sigmoid(zv)
        siluz = zv * sz
        yz = yv * siluz
        half = SSM_W // 2
        gy = dyn * sn_ref[...]
        dyz_parts, yzn_parts = [], []
        for hf in range(2):
            part = yz[:, hf * half:(hf + 1) * half]
            r = _rs(part)
            pn = part * r
            gp = gy[:, hf * half:(hf + 1) * half]
            dyz_parts.append(r * (gp - pn * jnp.mean(gp * pn, axis=-1, keepdims=True)))
            yzn_parts.append(pn)
        dyz = jnp.concatenate(dyz_parts, axis=1)
        dsn_ref[...] += jnp.sum(dyn * jnp.concatenate(yzn_parts, axis=1), axis=0, keepdims=True)
        dy = dyz * siluz
        dz_ref[...] = dyz * yv * _dsilu(zv, sz)

        colbuf[...] = jnp.zeros_like(colbuf)
        dx_pieces, dacs_pieces, dacsl_pieces, db_pieces, dc_pieces = [], [], [], [], []
        for grp in range(2):
            bb = act[:, SSM_W + SSM_N * grp:SSM_W + SSM_N * (grp + 1)].astype(BF16)
            cb_ = act[:, SSM_W + 2 * SSM_N + SSM_N * grp:SSM_W + 2 * SSM_N + SSM_N * (grp + 1)].astype(BF16)
            cbm = _dot_nt(cb_, bb)
            dcbm = jnp.zeros((L, L), F32)
            dc_g = jnp.zeros((L, SSM_N), F32)
            db_g = jnp.zeros((L, SSM_N), F32)
            for jj in range(4):
                j = 4 * grp + jj
                sl = slice(128 * j, 128 * j + 128)
                dy2 = dy[:, sl]
                dy2b = dy2.astype(BF16)
                d0 = _decay(acs_x, acs_t, 2 * j, tri)
                d1 = _decay(acs_x, acs_t, 2 * j + 1, tri)
                m0 = cbm * d0
                m1 = cbm * d1
                x2 = xb[:, sl]
                hprev = hs_ref[0, j]
                hprevb = hprev.astype(BF16)
                dhn = dh[j]
                dhnb = dhn.astype(BF16)
                g2 = _dot(bb, dhnb)
                dx_pieces.append(jnp.where(lo, _dot_tn(m0.astype(BF16), dy2b), _dot_tn(m1.astype(BF16), dy2b)) + w_x[:, sl] * g2)
                zero = jnp.zeros_like(dy2b)
                dm0 = _dot_nt(jnp.where(lo, dy2b, zero), x2)
                dm1 = _dot_nt(jnp.where(lo, zero, dy2b), x2)
                dcbm = dcbm + dm0 * d0 + dm1 * d1
                e0 = dm0 * m0
                e1v = dm1 * m1
                colbuf[:, 2 * j:2 * j + 1] = jnp.sum(e0, axis=1, keepdims=True) - jnp.sum(e0.T, axis=1, keepdims=True)
                colbuf[:, 2 * j + 1:2 * j + 2] = jnp.sum(e1v, axis=1, keepdims=True) - jnp.sum(e1v.T, axis=1, keepdims=True)
                yoff = lam_x[:, sl] * _dot(cb_, hprevb)
                gxw = g2 * xdw[:, sl]
                dacs_pieces.append(dy2 * yoff - gxw)
                dacsl_pieces.append(jnp.sum(gxw, axis=0, keepdims=True) + gam_x[:, sl] * jnp.sum(dhn * hprev, axis=0, keepdims=True))
                dyl = (dy2 * lam_x[:, sl]).astype(BF16)
                dc_g = dc_g + _dot_nt(dyl, hprevb)
                db_g = db_g + _dot_nt(xw[:, sl], dhnb)
                dh[j] = gam_x[:, sl] * dhn + _dot_tn(cb_, dyl)
            dcbb = dcbm.astype(BF16)
            dc_pieces.append(dc_g + _dot(dcbb, bb))
            db_pieces.append(db_g + _dot_tn(dcbb, cb_))

        dxd = jnp.concatenate(dx_pieces, axis=1)
        rowi = lax.broadcasted_iota(jnp.int32, (L, 1), 0)
        dacs_x = (jnp.concatenate(dacs_pieces, axis=1) + _dot_hi(colbuf[...], e1_ref[...])
                  + jnp.where(rowi == L - 1, jnp.concatenate(dacsl_pieces, axis=1), 0.0))
        upper = lax.broadcasted_iota(jnp.int32, (L, L), 0) <= lax.broadcasted_iota(jnp.int32, (L, L), 1)
        dadt_x = _dot_hi(upper.astype(F32), dacs_x)
        ddt_x = dxd * xs + dadt_x * a_x
        ddtr = _dot_nt_hi(ddt_x * _sigmoid(pre_x), e_ref[...])
        ddt_ref[...] = jnp.zeros_like(ddt_ref)
        ddt_ref[:, 0:SSM_H] = ddtr
        dalx =jnp.sum(dadt_x * dt_x, axis=0, keepdims=True) * a_x
        ddskx = jnp.sum(dy * xs, axis=0, keepdims=True)
        par_x = jnp.where(row8 == 1, dalx, 0.0) + jnp.where(row8 == 2, ddskx, 0.0)
        dpar_ref[...] += _dot_nt_hi(par_x, e_ref[...]) + jnp.where(row8 == 0, jnp.sum(ddtr, axis=0, keepdims=True), 0.0)

        dxs = dxd * dt_x + dsk_ref[...] * dy
        dact = jnp.concatenate([dxs] + db_pieces + dc_pieces, axis=1)
        du = dact * _dsilu(xc, sg)
        dcb_ref[...] += jnp.sum(du, axis=0, keepdims=True)
        taps = (sh3, sh2, sh1, u)
        dcw = jnp.zeros((8, CONV_C), F32)
        for k in range(4):
            dcw = dcw + jnp.where(row8 == k, jnp.sum(du * taps[k], axis=0, keepdims=True), 0.0)
        dcw_ref[...] += dcw
        zext = jnp.concatenate([du, duh[...]], axis=0)
        f1, f2, f3 = (pltpu.roll(zext, L + 8 - m, 0)[:L] for m in (1, 2, 3))
        dxbc_ref[...] = cw_ref[3:4, :] * du + cw_ref[2:3, :] * f1 + cw_ref[1:2, :] * f2 + cw_ref[0:1, :] * f3
        duh[...] = du[:8, :]

    return _call(
        body, name="ssd_bwd", grid=(NCH,),
        in_specs=[pl.BlockSpec((L, SSM_W), lambda i: (NCH - 1 - i, 1)), pl.BlockSpec((L, CONV_C), rev),
                  pl.BlockSpec((8, CONV_C), lambda i: (jnp.maximum((NCH - 1 - i) * (L // 8) - 1, 0), 0)),
                  pl.BlockSpec((L, SSM_W), rev), pl.BlockSpec((L, SSM_H), rev), pl.BlockSpec((L, SSM_W), rev),
                  pl.BlockSpec((1, 8, SSM_N, 128), lambda i: (NCH - 1 - i, 0, 0, 0)),
                  _const((4, CONV_C)), _const((1, CONV_C)), _const((1, SSM_H)), _const((1, SSM_W)), _const((1, SSM_W)), _const((1, SSM_W)),
                  _const((SSM_H, SSM_W)), _const((128, SSM_W))],
        out_specs=[pl.BlockSpec((L, CONV_C), rev), pl.BlockSpec((L, SSM_W), rev), pl.BlockSpec((L, 128), rev),
                   _const((8, CONV_C)), _const((1, CONV_C)), _const((1, SSM_W)), _const((8, SSM_H))],
        out_shape=[jax.ShapeDtypeStruct((T, CONV_C), F32), jax.ShapeDtypeStruct((T, SSM_W), F32), jax.ShapeDtypeStruct((T, 128), F32),
                   jax.ShapeDtypeStruct((8, CONV_C), F32), jax.ShapeDtypeStruct((1, CONV_C), F32), jax.ShapeDtypeStruct((1, SSM_W), F32),
                   jax.ShapeDtypeStruct((8, SSM_H), F32)],
        scratch=[pltpu.VMEM((8, SSM_N, 128), F32), pltpu.VMEM((8, CONV_C), F32), pltpu.VMEM((L, 128), F32)],
        args=[dmix, xbc, xbc, z, dtr, y, hs, convw, convb, dtb, alx, dskx, ssmn, e, e1], sem=("arbitrary",), duties=duties)


def _loss_head(x3, target):
    tm = 512

    def body(x_ref, t_ref, dy_ref, ss_ref):
        @pl.when(pl.program_id(0) == 0)
        def _():
            ss_ref[...] = jnp.zeros_like(ss_ref)

        err = x_ref[...] - t_ref[...]
        dy_ref[...] = err * (1.0 / D)
        ss_ref[...] += jnp.sum(jnp.sum(err * err, axis=1, keepdims=True), axis=0, keepdims=True)

    return pl.pallas_call(
        body, name="loss_head", grid=(T // tm,),
        in_specs=[_rows(tm, D), _rows(tm, D)],
        out_specs=[_rows(tm, D), _const((1, 128))],
        out_shape=[jax.ShapeDtypeStruct((T, D), F32), jax.ShapeDtypeStruct((1, 128), F32)],
        compiler_params=_params(("arbitrary",)),
    )(x3, target)


def _adam_math(w, g, m, v):
    m = ADAM_B1 * m + (1.0 - ADAM_B1) * g
    v = ADAM_B2 * v + (1.0 - ADAM_B2) * (g * g)
    m_hat = m / (1.0 - ADAM_B1 ** ADAM_STEP)
    v_hat = v / (1.0 - ADAM_B2 ** ADAM_STEP)
    delta = -ADAM_LR * (m_hat / (jnp.sqrt(v_hat) + ADAM_EPS) + ADAM_WD * w)
    return delta, m, v


def _adamw(w, m, v, parts, name):
    rows, cols = w.shape
    tr = rows if rows <= 512 else 256
    assert rows % tr == 0
    n_parts = len(parts)

    def body(*refs):
        w_ref, m_ref, v_ref = refs[:3]
        p_refs = refs[3:3 + n_parts]
        g_ref, d_ref, nm_ref, nv_ref = refs[3 + n_parts:]
        g = p_refs[0][...].astype(F32)
        for p in p_refs[1:]:
            g = g + p[...].astype(F32)
        delta, nm, nv = _adam_math(w_ref[...], g, m_ref[...], v_ref[...])
        g_ref[...] = g
        d_ref[...] = delta
        nm_ref[...] = nm
        nv_ref[...] = nv

    spec = pl.BlockSpec((tr, cols), lambda i: (i, 0))
    return pl.pallas_call(
        body, name=name, grid=(rows // tr,),
        in_specs=[spec] * (3 + n_parts), out_specs=[spec] * 4,
        out_shape=[jax.ShapeDtypeStruct((rows, cols), F32)] * 4,
        compiler_params=_params(("parallel",)),
    )(w, m, v, *parts)


COL_TILE = 256


def _adamw_sharded(w, m, v, chip_sum, from_chips, chip_ids, name, after=()):
    rows, cols = w.shape
    prow = chip_sum.shape[1]
    assert cols % COL_TILE == 0 and prow >= rows and chip_sum.shape[2] == cols

    def body(ids_ref, w_ref, m_ref, v_ref, s_ref, r1_ref, r2_ref, r3_ref, *rest):
        g_ref, d_ref, nm_ref, nv_ref = rest[len(after):]
        g = s_ref[0]
        for r in (r1_ref, r2_ref, r3_ref):
            g = g + r[0].astype(F32)
        g = g[:rows]
        delta, nm, nv = _adam_math(w_ref[...], g, m_ref[...], v_ref[...])
        g_ref[...] = g
        d_ref[...] = delta
        nm_ref[...] = nm
        nv_ref[...] = nv

    spec = pl.BlockSpec((rows, COL_TILE), lambda i, ids: (0, i))
    part = lambda k: pl.BlockSpec((1, prow, COL_TILE), lambda i, ids: (ids[k], 0, i))
    return pl.pallas_call(
        body, name=name,
        grid_spec=pltpu.PrefetchScalarGridSpec(
            num_scalar_prefetch=1, grid=(cols // COL_TILE,),
            in_specs=[spec, spec, spec, part(0), part(1), part(2), part(3)] + [ANY] * len(after), out_specs=[spec] * 4),
        out_shape=[jax.ShapeDtypeStruct((rows, cols), F32)] * 4,
        compiler_params=_params(("parallel",)),
    )(chip_ids, w, m, v, chip_sum, from_chips, from_chips, from_chips, *after)


def _chip_sum(g32, recv, core, name):
    rows, cols = g32.shape[1:]
    assert cols % COL_TILE == 0

    def body(core_ref, a_ref, b_ref, s_ref, sb_ref):
        s = a_ref[...] + b_ref[...].astype(F32)
        s_ref[...] = s
        sb_ref[...] = s.astype(BF16)

    by_chip = pl.BlockSpec((1, rows, COL_TILE), lambda k, i, core_ref: (k, 0, i))
    return pl.pallas_call(
        body, name=name,
        grid_spec=pltpu.PrefetchScalarGridSpec(
            num_scalar_prefetch=1, grid=(N_DEV // 2, cols // COL_TILE),
            in_specs=[pl.BlockSpec((1, rows, COL_TILE), lambda k, i, core_ref: (2 * k + core_ref[0], 0, i)), by_chip],
            out_specs=[by_chip, by_chip]),
        out_shape=[jax.ShapeDtypeStruct((N_DEV // 2, rows, cols), F32), jax.ShapeDtypeStruct((N_DEV // 2, rows, cols), BF16)],
        compiler_params=_params(("parallel", "parallel")),
    )(core, g32, recv)


def _all_reduce_small(v, after=()):
    rows = v.shape[0]

    def body(v_ref, *rest):
        out_ref, gath, send_sems, recv_sems = rest[len(after):]
        x, y, c = _place()
        me, sibling = (x, y, c), (x, y, 1 - c)
        chips = [(1 - x, y), (x, 1 - y), (1 - x, 1 - y)]

        def blk(px, py, pc):
            return gath.at[pl.ds((4 * px + 2 * py + pc) * rows, rows), :]

        def copy(k, block, to, src=None):
            return pltpu.make_async_remote_copy(src_ref=blk(*block) if src is None else src, dst_ref=blk(*block),
                                                send_sem=send_sems.at[k], recv_sem=recv_sems.at[k], device_id=to, device_id_type=MESH)

        gath[pl.ds((4 * x + 2 * y + c) * rows, rows), :] = v_ref[...]
        first = [copy(0, me, sibling, src=v_ref)] + [copy(1 + j, me, (*chip, c), src=v_ref) for j, chip in enumerate(chips)]
        for cp in first:
            cp.start()
        passed = [copy(4 + j, (*chip, c), sibling) for j, chip in enumerate(chips)]
        for j, chip in enumerate(chips):
            copy(1 + j, (*chip, c), me).wait_recv()
            passed[j].start()
        copy(0, sibling, me).wait_recv()
        for j, chip in enumerate(chips):
            copy(4 + j, (*chip, 1 - c), me).wait_recv()
        for cp in first + passed:
            cp.wait_send()
        acc = gath[0:rows, :]
        for d in range(1, N_DEV):
            acc = acc + gath[d * rows:(d + 1) * rows, :]
        out_ref[...] = acc

    vm = pl.BlockSpec(memory_space=pltpu.VMEM)
    return pl.pallas_call(
        body, name="all_reduce_small",
        in_specs=[vm] + [ANY] * len(after), out_specs=vm,
        out_shape=jax.ShapeDtypeStruct(v.shape, F32),
        scratch_shapes=[pltpu.VMEM((N_DEV * rows, 128), F32), pltpu.SemaphoreType.DMA((7,)), pltpu.SemaphoreType.DMA((7,))],
    )(v, *after)


def _rope_tables(positions):
    inv_freq = ROPE_THETA ** (-jnp.arange(0, HD, 2, dtype=F32) / HD)
    ang = positions.reshape(T).astype(F32)[:, None] * inv_freq
    ang = jnp.concatenate([ang, ang, ang, ang], axis=-1)
    lo_half = (jnp.arange(128) % HD) < (HD // 2)
    return jnp.cos(ang), jnp.where(lo_half, -jnp.sin(ang), jnp.sin(ang))


def _selectors():
    lane = jnp.arange(QW)
    e = (lane[None, :] // HD == jnp.arange(SSM_H)[:, None]).astype(F32)
    e1 = ((lane[None, :] == HD * jnp.arange(128)[:, None]) & (jnp.arange(128)[:, None] < SSM_H)).astype(F32)
    src = jnp.arange(KVW)
    ex = ((lane[None, :] // (HD * (NQ // NKV)) == src[:, None] // HD) & (lane[None, :] % HD == src[:, None] % HD)).astype(F32)
    return e, e1, ex


WEIGHTS = ['ffn1_pre_norm', 'ffn1_w_gate', 'ffn1_w_up', 'ffn1_w_down', 'ffn1_post_norm', 'mix_pre_norm', 'w_in', 'conv_w', 'conv_b',
           'dt_bias', 'a_log', 'd_skip', 'ssm_norm', 'w_out', 'mix_post_norm', 'ffn2_pre_norm', 'ffn2_w_gate', 'ffn2_w_up',
           'ffn2_w_down', 'ffn2_post_norm']
COL_SHARDED = ['ffn1_w_gate', 'ffn1_w_up', 'ffn2_w_gate', 'ffn2_w_up', 'w_in']
ROW_SHARDED = ['ffn1_w_down', 'ffn2_w_down', 'w_out']
BIG = COL_SHARDED + ROW_SHARDED
FFN_BIG = COL_SHARDED[:4] + ROW_SHARDED[:2]
SMALL = ['ffn1_pre_norm', 'ffn1_post_norm', 'mix_pre_norm', 'conv_b', 'dt_bias', 'a_log', 'd_skip', 'ssm_norm', 'mix_post_norm',
         'ffn2_pre_norm', 'ffn2_post_norm']
FFN1 = ['ffn1_w_gate', 'ffn1_w_up', 'ffn1_w_down']
FFN2 = ['ffn2_w_gate', 'ffn2_w_up', 'ffn2_w_down']


def _wire_block(name, a):
    if name in FFN_BIG:
        return jnp.pad(a.astype(BF16), ((0, FSH - FSR), (0, 0)))
    return a if name == "conv_w" else a.astype(BF16)


def _whole_from_gathered(name, a):
    if name == "conv_w":
        return jnp.transpose(a, (1, 0, 2)).reshape(a.shape[1], -1)
    a = a.reshape(-1, D)
    return jnp.pad(a, ((0, INP - IN_COLS), (0, 0))) if name == "w_in" else a


def _step(x, positions, target, small, blocks=None, whole=None, core=None):
    dist = blocks is not None
    w = dict(small)
    if whole:
        w.update(whole)

    def gather(names):
        return [_gather_duty([_wire_block(n, blocks[n]) for n in names])] if dist else []

    def put(names, results):
        if dist:
            for n, r in zip(names, results[0]):
                w[n] = _whole_from_gathered(n, r)

    def row_shards(pair):
        return tuple(a.reshape(N_DEV, -1, a.shape[1]) for a in pair)

    g, sums, red = {}, {}, {}

    def swap(names):
        return [_swap_duty([g[n][1] for n in names])] if dist else []

    def chip_sums(names, from_sibling):
        if dist:
            for n, recv in zip(names, from_sibling):
                sums[n] = _chip_sum(g[n][0], recv, core, "chip_sum_" + n)

    def exchange(names):
        return [_exchange_duty([sums[n][1] for n in names])] if dist else []

    def reduced(names, from_chips):
        if dist:
            for n, recv in zip(names, from_chips):
                red[n] = (sums[n][0], recv)

    cos, sin_s = _rope_tables(positions)
    e, e1, exf = _selectors()
    bias = _attn_bias()
    alx = jnp.repeat(w["a_log"], HD, axis=1)
    dskx = jnp.repeat(w["d_skip"], HD, axis=1)

    if dist:
        put(FFN1, _comm_only(gather(FFN1), "gather_ffn1"))
    (x1, n1, a1, b1, hm1, h1), got = _ffn_fwd(x, w["ffn1_pre_norm"], w["ffn1_w_gate"], w["ffn1_w_up"], w["ffn1_w_down"],
                                              w["ffn1_post_norm"], "ffn1_fwd", gather(["w_in", "conv_w"]))
    put(["w_in", "conv_w"], got)
    (n2, q, kx, vx, xbc, z, dtr), got = _inproj_fwd(x1, w["mix_pre_norm"], w["w_in"], cos, sin_s, exf.astype(BF16), gather(["w_out"]))
    put(["w_out"], got)
    (attn, lse), got = _attn_fwd(q, kx, vx, bias, gather(FFN2[:2]))
    put(FFN2[:2], got)
    (yn, y, hs), got = _ssd_fwd(xbc, z, dtr, w["conv_w"], w["conv_b"], w["dt_bias"], alx, dskx, w["ssm_norm"], e, gather(FFN2[2:]))
    put(FFN2[2:], got)
    x2, h2 = _outproj_fwd(x1, attn, yn, w["w_out"], w["mix_post_norm"])
    (x3, n3, a3, b3, hm3, h3), _ = _ffn_fwd(x2, w["ffn2_pre_norm"], w["ffn2_w_gate"], w["ffn2_w_up"], w["ffn2_w_down"],
                                            w["ffn2_post_norm"], "ffn2_fwd")
    dx3, ss = _loss_head(x3, target)

    (dx2, da3, db3, dh3, g["ffn2_pre_norm"], g["ffn2_post_norm"]), _ = _ffn_bwd(
        dx3, x2, a3, b3, h3, w["ffn2_pre_norm"], w["ffn2_post_norm"], w["ffn2_w_gate"], w["ffn2_w_up"], w["ffn2_w_down"], "ffn2_bwd")
    g["ffn2_w_down"] = row_shards(_matmul_tn(hm3, dh3, "ffn2_dwd")[0])
    g["ffn2_w_gate"] = row_shards(_matmul_tn(da3, n3, "ffn2_dwg")[0])
    g["ffn2_w_up"] = row_shards(_matmul_tn(db3, n3, "ffn2_dwu")[0])

    (dh2, dmix, g["mix_post_norm"]), got = _outproj_bwd(dx2, h2, w["mix_post_norm"], w["w_out"], swap(FFN2))
    chip_sums(FFN2, got[0] if dist else None)
    g["w_out"] = _dwout(attn, yn, dh2)
    (dq, dkx, dvx), got = _attn_bwd(q, kx, vx, attn, dmix, lse, bias, exchange(FFN2) + swap(["w_out"]))
    if dist:
        reduced(FFN2, got[0])
        chip_sums(["w_out"], got[1])
    (dxbc, dz, ddt, dcw, g["conv_b"], g["ssm_norm"], dpar), got = _ssd_bwd(
        dmix, xbc, z, dtr, y, hs, w["conv_w"], w["conv_b"], w["dt_bias"], alx, dskx, w["ssm_norm"], e, e1, exchange(["w_out"]))
    reduced(["w_out"], got[0] if dist else None)
    g["conv_w"] = dcw[0:4]
    g["dt_bias"], g["a_log"], g["d_skip"] = dpar[0:1], dpar[1:2], dpar[2:3]
    dx1, dproj, g["mix_pre_norm"] = _inproj_bwd(dx2, dq, dkx, dvx, dxbc, dz, ddt, x1, w["mix_pre_norm"], w["w_in"], cos, sin_s, exf)
    g["w_in"] = tuple(a[:IN_COLS].reshape(N_DEV, -1, D) for a in _matmul_tn(dproj, n2, "dwin")[0])

    started = {}

    def start(n):
        started[n] = _exchange_start(sums[n][1], "start_exchange_" + n)
        return [started[n]["token"]]

    after = []
    if dist:
        chip_sums(["w_in"], _comm_only(swap(["w_in"]), "swap_w_in")[0])
        after = start("w_in")
    (dx0, da1, db1, dh1, g["ffn1_pre_norm"], g["ffn1_post_norm"]), _ = _ffn_bwd(
        dx1, x, a1, b1, h1, w["ffn1_pre_norm"], w["ffn1_post_norm"], w["ffn1_w_gate"], w["ffn1_w_up"], w["ffn1_w_down"], "ffn1_bwd",
        after)
    own, _ = _matmul_tn(hm1, dh1, "ffn1_dwd")
    g["ffn1_w_down"] = row_shards(own)
    own, got = _matmul_tn(da1, n1, "ffn1_dwg", duties=swap(["ffn1_w_down"]))
    g["ffn1_w_gate"] = row_shards(own)
    if dist:
        chip_sums(["ffn1_w_down"], got[0])
        after = start("ffn1_w_down")
    own, got = _matmul_tn(db1, n1, "ffn1_dwu", after=after, duties=swap(["ffn1_w_gate"]))
    g["ffn1_w_up"] = row_shards(own)
    if dist:
        chip_sums(["ffn1_w_gate"], got[0])
        after = start("ffn1_w_gate")
        chip_sums(["ffn1_w_up"], _comm_only(swap(["ffn1_w_up"]), "swap_ffn1_w_up", after=after)[0])
        start("ffn1_w_up")
    return ss, dx0, g, red, {n: (sums[n][0], started[n]) for n in started}


def kernel(x, positions, ffn1_pre_norm, ffn1_w_gate, ffn1_w_up, ffn1_w_down, ffn1_post_norm, mix_pre_norm, w_in, conv_w, conv_b, dt_bias, a_log, d_skip, ssm_norm, w_out, mix_post_norm, ffn2_pre_norm, ffn2_w_gate, ffn2_w_up, ffn2_w_down, ffn2_post_norm, loss_target, m_ffn1_pre_norm, m_ffn1_w_gate, m_ffn1_w_up, m_ffn1_w_down, m_ffn1_post_norm, m_mix_pre_norm, m_w_in, m_conv_w, m_conv_b, m_dt_bias, m_a_log, m_d_skip, m_ssm_norm, m_w_out, m_mix_post_norm, m_ffn2_pre_norm, m_ffn2_w_gate, m_ffn2_w_up, m_ffn2_w_down, m_ffn2_post_norm, v_ffn1_pre_norm, v_ffn1_w_gate, v_ffn1_w_up, v_ffn1_w_down, v_ffn1_post_norm, v_mix_pre_norm, v_w_in, v_conv_w, v_conv_b, v_dt_bias, v_a_log, v_d_skip, v_ssm_norm, v_w_out, v_mix_post_norm, v_ffn2_pre_norm, v_ffn2_w_gate, v_ffn2_w_up, v_ffn2_w_down, v_ffn2_post_norm):
    given = dict(locals())
    drop = lambda n, a: a if n in SMALL else (a[0].T if n in COL_SHARDED else a[0])
    w = {n: drop(n, given[n]) for n in WEIGHTS}
    m = {n: drop(n, given["m_" + n]) for n in WEIGHTS}
    v = {n: drop(n, given["v_" + n]) for n in WEIGHTS}
    cx, cy, cc = _place()
    my_chip = 2 * cx + cy
    others = [2 * (1 - cx) + cy, 2 * cx + (1 - cy), 2 * (1 - cx) + (1 - cy)]

    core = jnp.stack([cc]).astype(jnp.int32)
    ss, grad_x, g, red, pending = _step(x[0], positions, loss_target[0], {n: w[n] for n in SMALL},
                                        blocks={n: w[n] for n in BIG + ["conv_w"]}, core=core)
    loss = lax.psum(0.5 * ss[0, 0] / D, ("x", "y", "c"))
    chip_ids = jnp.stack([my_chip] + others).astype(jnp.int32)
    out_g, out_d, out_m, out_v = {}, {}, {}, {}
    last_start = [pending["ffn1_w_up"][1]["token"]]
    for n in BIG:
        if n not in pending:
            out_g[n], out_d[n], out_m[n], out_v[n] = _adamw_sharded(w[n], m[n], v[n], red[n][0], red[n][1], chip_ids, "adamw_" + n,
                                                                    last_start)

    small_g = jnp.concatenate([g[n].reshape(-1) for n in SMALL] + [g["conv_w"].reshape(-1)])
    n_small = small_g.shape[0]
    rows = -(-n_small // 128 // 8) * 8
    packed = jnp.pad(small_g, (0, rows * 128 - n_small)).reshape(rows, 128)
    total = _all_reduce_small(packed, last_start).reshape(-1)
    sizes = [w[n].size for n in SMALL]
    offs = [0]
    for s_ in sizes:
        offs.append(offs[-1] + s_)
    gs = {n: total[offs[i]:offs[i + 1]].reshape(w[n].shape) for i, n in enumerate(SMALL)}
    gcw = total[offs[-1]:offs[-1] + 4 * CONV_C].reshape(4, CONV_C)
    gs["conv_w"] = lax.dynamic_slice_in_dim(gcw, (4 * cx + 2 * cy + cc) * (CONV_C // N_DEV), CONV_C // N_DEV, axis=1)
    names = SMALL + ["conv_w"]

    def pack(d):
        flat = jnp.concatenate([d[n].reshape(-1) for n in names])
        return jnp.pad(flat, (0, rows * 128 - flat.shape[0])).reshape(rows, 128)

    pg, pd, pm, pv = _adamw(pack(w), pack(m), pack(v), [pack(gs)], "adamw_small")
    done = [pg] + [out_v[n] for n in BIG if n not in pending]
    for n, (chip_sum, started) in pending.items():
        recv = _exchange_wait(started, done, "wait_exchange_" + n)
        out_g[n], out_d[n], out_m[n], out_v[n] = _adamw_sharded(w[n], m[n], v[n], chip_sum, recv, chip_ids, "adamw_" + n)
        done = [out_v[n]]
    o2 = [0]
    for n in names:
        o2.append(o2[-1] + w[n].size)
    for i, n in enumerate(names):
        for dst, src in ((out_g, pg), (out_d, pd), (out_m, pm), (out_v, pv)):
            dst[n] = src.reshape(-1)[o2[i]:o2[i + 1]].reshape(w[n].shape)

    outs = [loss, grad_x[None]]
    for d in (out_g, out_d, out_m, out_v):
        outs += [d[n] if n in SMALL else (d[n].T[None] if n in COL_SHARDED else d[n][None]) for n in WEIGHTS]
    return tuple(outs)
```

```python
import functools
import math

import jax
import jax.numpy as jnp
from jax import lax
from jax.experimental import pallas as pl
from jax.experimental.pallas import tpu as pltpu

F32 = jnp.float32
BF16 = jnp.bfloat16
HI = lax.Precision.HIGHEST
MESH = pl.DeviceIdType.MESH

N_DEV = 8
T = 2048
D = 1024
FF = 2816
FSR = FF // N_DEV
FSH = 384
FFP = N_DEV * FSH
HD = 64
NQ = 16
NKV = 4
QW = NQ * HD
KVW = NKV * HD
SSM_W = 1024
SSM_H = 16
SSM_N = 128
CONV_C = SSM_W + 2 * 2 * SSM_N
IN_COLS = 4112
INP = 4224
L = 128
NCH = T // L
AB = 256
NAB = T // AB
EPS = 1e-6
NEG = -1e30
ROPE_THETA = 10000.0
DILATIONS = ((128, 1), (512, 4), (2048, 16))

ADAM_LR = 0.001
ADAM_B1 = 0.9
ADAM_B2 = 0.999
ADAM_EPS = 1e-08
ADAM_WD = 0.01
ADAM_STEP = 10

VMEM_LIMIT = 58 * 1024 * 1024


def _params(sem, vmem=VMEM_LIMIT):
    return pltpu.CompilerParams(dimension_semantics=sem, vmem_limit_bytes=vmem)


def _dot(a, b):
    return jnp.dot(a, b, preferred_element_type=F32)


def _dot_nt(a, b):
    return lax.dot_general(a, b, (((1,), (1,)), ((), ())), preferred_element_type=F32)


def _dot_tn(a, b):
    return lax.dot_general(a, b, (((0,), (0,)), ((), ())), preferred_element_type=F32)


def _dot_hi(a, b):
    return jnp.dot(a, b, preferred_element_type=F32, precision=HI)


def _dot_nt_hi(a, b):
    return lax.dot_general(a, b, (((1,), (1,)), ((), ())), preferred_element_type=F32, precision=HI)


def _rs(x):
    return lax.rsqrt(jnp.mean(x * x, axis=-1, keepdims=True) + EPS)


def _sigmoid(x):
    return jax.nn.sigmoid(x)


def _dsilu(x, s):
    return s * (1.0 + x * (1.0 - s))


def _resident(shape):
    nd = len(shape)
    return pl.BlockSpec(shape, lambda *_: (0,) * nd, pipeline_mode=pl.Buffered(1))


def _const(shape):
    nd = len(shape)
    return pl.BlockSpec(shape, lambda *_: (0,) * nd)


def _rows(tm, cols):
    return pl.BlockSpec((tm, cols), lambda i: (i, 0))


ANY = pl.BlockSpec(memory_space=pl.ANY)


def _place():
    return lax.axis_index("x"), lax.axis_index("y"), lax.axis_index("c")


def _gather_duty(arrays):
    n = len(arrays)
    results = [jax.ShapeDtypeStruct((N_DEV,) + a.shape, a.dtype) for a in arrays]

    def make(ins, outs, send_sems, recv_sems, local_sems):
        x, y, c = _place()
        me, sibling = (x, y, c), (x, y, 1 - c)
        chips = [(1 - x, y), (x, 1 - y), (1 - x, 1 - y)]

        def place_of(a, px, py, pc):
            return outs[a].at[4 * px + 2 * py + pc]

        def copy(a, k, block, to, src=None):
            dst = place_of(a, *block)
            return pltpu.make_async_remote_copy(src_ref=dst if src is None else src, dst_ref=dst,
                                                send_sem=send_sems.at[7 * a + k], recv_sem=recv_sems.at[7 * a + k],
                                                device_id=to, device_id_type=MESH)

        def own(a):
            return pltpu.make_async_copy(ins[a], place_of(a, *me), local_sems.at[a])

        def first(a):
            return [copy(a, 0, me, sibling, src=ins[a])] + [copy(a, 1 + j, me, (*chip, c), src=ins[a]) for j, chip in enumerate(chips)]

        def start():
            for a in range(n):
                own(a).start()
            for a in range(n):
                for cp in first(a):
                    cp.start()

        def finish():
            for j, chip in enumerate(chips):
                for a in range(n):
                    copy(a, 1 + j, (*chip, c), me).wait_recv()
                    copy(a, 4 + j, (*chip, c), sibling).start()
            for a in range(n):
                copy(a, 0, sibling, me).wait_recv()
                for j, chip in enumerate(chips):
                    copy(a, 4 + j, (*chip, 1 - c), me).wait_recv()
            for a in range(n):
                for cp in first(a) + [copy(a, 4 + j, (*chip, c), sibling) for j, chip in enumerate(chips)]:
                    cp.wait_send()
                own(a).wait()

        return start, finish

    return dict(operands=list(arrays), results=results, sems=(7 * n, 7 * n, n), make=make)


def _swap_duty(arrays):
    n = len(arrays)
    half = N_DEV // 2
    results = [jax.ShapeDtypeStruct((half,) + a.shape[1:], a.dtype) for a in arrays]

    def make(ins, outs, send_sems, recv_sems):
        x, y, c = _place()

        def copies():
            return [pltpu.make_async_remote_copy(src_ref=ins[a].at[2 * k + (1 - c)], dst_ref=outs[a].at[k],
                                                 send_sem=send_sems.at[half * a + k], recv_sem=recv_sems.at[half * a + k],
                                                 device_id=(x, y, 1 - c), device_id_type=MESH)
                    for a in range(n) for k in range(half)]

        def start():
            for cp in copies():
                cp.start()

        def finish():
            for cp in copies():
                cp.wait()

        return start, finish

    return dict(operands=list(arrays), results=results, sems=(half * n, half * n), make=make)


def _exchange_duty(arrays):
    n = len(arrays)
    results = [jax.ShapeDtypeStruct(a.shape, a.dtype) for a in arrays]

    def make(ins, outs, send_sems, recv_sems):
        x, y, c = _place()
        chips = [(1 - x, y), (x, 1 - y), (1 - x, 1 - y)]
        my_chip = 2 * x + y

        def sends():
            return [pltpu.make_async_remote_copy(src_ref=ins[a].at[2 * px + py], dst_ref=outs[a].at[my_chip],
                                                 send_sem=send_sems.at[3 * a + j], recv_sem=recv_sems.at[3 * a + j],
                                                 device_id=(px, py, c), device_id_type=MESH)
                    for a in range(n) for j, (px, py) in enumerate(chips)]

        def start():
            for cp in sends():
                cp.start()

        def finish():
            for a in range(n):
                for j, (px, py) in enumerate(chips):
                    pltpu.make_async_remote_copy(src_ref=ins[a].at[my_chip], dst_ref=outs[a].at[2 * px + py],
                                                 send_sem=send_sems.at[3 * a + j], recv_sem=recv_sems.at[3 * a + j],
                                                 device_id=(px, py, c), device_id_type=MESH).wait_recv()
            for cp in sends():
                cp.wait_send()

        return start, finish

    return dict(operands=list(arrays), results=results, sems=(3 * n, 3 * n), make=make)


def _call(body, *, name, grid, in_specs, out_specs, out_shape, args, sem, scratch=(), duties=(), after=()):
    n_in, n_out, n_scr = len(in_specs), len(out_specs), len(scratch)
    sem_shapes = [pltpu.SemaphoreType.DMA((k,)) for d in duties for k in d["sems"]]

    def full(*refs):
        pos = [0]

        def take(k):
            pos[0] += k
            return refs[pos[0] - k:pos[0]]

        ins = take(n_in)
        d_ins = [take(len(d["operands"])) for d in duties]
        take(len(after))
        outs = take(n_out)
        d_outs = [take(len(d["results"])) for d in duties]
        scr = take(n_scr)
        d_sems = [take(len(d["sems"])) for d in duties]
        hooks = [d["make"](di, do, *ds) for d, di, do, ds in zip(duties, d_ins, d_outs, d_sems)]
        if grid and hooks:
            ids = [pl.program_id(k) for k in range(len(grid))]
            first = functools.reduce(jnp.logical_and, [i == 0 for i in ids])
            last = functools.reduce(jnp.logical_and, [i == g - 1 for i, g in zip(ids, grid)])

            @pl.when(first)
            def _():
                for start, _ in hooks:
                    start()

            body(*ins, *outs, *scr)

            @pl.when(last)
            def _():
                for _, finish in hooks:
                    finish()
        else:
            for start, _ in hooks:
                start()
            body(*ins, *outs, *scr)
            for _, finish in hooks:
                finish()

    d_args = [a for d in duties for a in d["operands"]]
    d_res = [r for d in duties for r in d["results"]]
    kwargs = dict(grid=grid) if grid else {}
    res = pl.pallas_call(
        full, name=name, in_specs=list(in_specs) + [ANY] * (len(d_args) + len(after)), out_specs=list(out_specs) + [ANY] * len(d_res),
        out_shape=list(out_shape) + d_res, scratch_shapes=list(scratch) + sem_shapes,
        compiler_params=_params(sem) if grid else None, **kwargs,
    )(*args, *d_args, *after)
    own, rest = list(res[:n_out]), list(res[n_out:])
    by_duty = []
    for d in duties:
        by_duty.append(rest[:len(d["results"])])
        rest = rest[len(d["results"]):]
    return own, by_duty


def _comm_only(duties, name, after=()):
    return _call(lambda: None, name=name, grid=None, in_specs=[], out_specs=[], out_shape=[], args=[], sem=None, duties=duties,
                 after=after)[1]


HBM = pl.BlockSpec(memory_space=pltpu.HBM)
SEMS = pl.BlockSpec(memory_space=pltpu.SEMAPHORE)
SIDE_EFFECT = pltpu.SideEffectType.DATAFLOW_SIDE_EFFECTING
N_OTHER_CHIPS = 3


def _chip_copies(src_ref, land_ref, sems):
    x, y, c = _place()
    chips = [(1 - x, y), (x, 1 - y), (1 - x, 1 - y)]
    return [pltpu.make_async_remote_copy(src_ref=src_ref.at[2 * px + py], dst_ref=land_ref.at[2 * x + y],
                                         send_sem=sems[j], recv_sem=sems[N_OTHER_CHIPS + j], device_id=(px, py, c), device_id_type=MESH)
            for j, (px, py) in enumerate(chips)]


def _exchange_start(pb, name):
    n_sem = 2 * N_OTHER_CHIPS

    def body(pb_ref, land_ref, *rest):
        for cp in _chip_copies(pb_ref, land_ref, rest[:n_sem]):
            cp.start()
        token = rest[n_sem + 2]
        token[...] = jnp.zeros_like(token)

    res = pl.pallas_call(
        body, name=name,
        out_shape=(pltpu.SemaphoreType.DMA(()),) * n_sem + (pltpu.HBM(pb.shape, pb.dtype), pltpu.HBM(pb.shape, pb.dtype),
                                                              jax.ShapeDtypeStruct((8, 128), F32)),
        in_specs=(HBM, HBM), out_specs=(SEMS,) * n_sem + (HBM, HBM, pl.BlockSpec(memory_space=pltpu.VMEM)),
        input_output_aliases={0: n_sem, 1: n_sem + 1},
        compiler_params=pltpu.CompilerParams(has_side_effects=SIDE_EFFECT),
    )(pltpu.with_memory_space_constraint(pb, pltpu.HBM), pltpu.with_memory_space_constraint(lax.empty(pb.shape, pb.dtype), pltpu.HBM))
    return dict(sems=res[:n_sem], src=res[n_sem], land=res[n_sem + 1], token=res[n_sem + 2])


def _exchange_wait(started, after, name):
    n_sem = 2 * N_OTHER_CHIPS

    def body(pb_ref, land_ref, *rest):
        for cp in _chip_copies(pb_ref, land_ref, rest[:n_sem]):
            cp.wait_send()
            cp.wait_recv()

    src, land = started["src"], started["land"]
    return pl.pallas_call(
        body, name=name, out_shape=(pltpu.HBM(src.shape, src.dtype), pltpu.HBM(land.shape, land.dtype)),
        in_specs=(HBM, HBM) + (SEMS,) * n_sem + (ANY,) * len(after), out_specs=(HBM, HBM), input_output_aliases={0: 0, 1: 1},
        compiler_params=pltpu.CompilerParams(has_side_effects=SIDE_EFFECT),
    )(src, land, *started["sems"], *after)[1]


def _ffn_fwd(x, gpre, wg, wu, wd, gpost, name, duties=()):
    tm = 256

    def body(x_ref, gpre_ref, wg_ref, wu_ref, wd_ref, gpost_ref, xo_ref, n_ref, a_ref, b_ref, hm_ref, h_ref):
        xv = x_ref[...]
        n = (xv * _rs(xv) * gpre_ref[...]).astype(BF16)
        a = _dot_nt(n, wg_ref[...])
        b = _dot_nt(n, wu_ref[...])
        hm = (a * _sigmoid(a) * b).astype(BF16)
        h = _dot(hm, wd_ref[...])
        xo_ref[...] = xv + 0.5 * (h * _rs(h) * gpost_ref[...])
        n_ref[...] = n
        a_ref[...] = a.astype(BF16)
        b_ref[...] = b.astype(BF16)
        hm_ref[...] = hm
        h_ref[...] = h

    return _call(
        body, name=name, grid=(T // tm,),
        in_specs=[_rows(tm, D), _const((1, D)), _resident((FFP, D)), _resident((FFP, D)), _resident((FFP, D)), _const((1, D))],
        out_specs=[_rows(tm, D), _rows(tm, D), _rows(tm, FFP), _rows(tm, FFP), _rows(tm, FFP), _rows(tm, D)],
        out_shape=[jax.ShapeDtypeStruct((T, D), F32), jax.ShapeDtypeStruct((T, D), BF16), jax.ShapeDtypeStruct((T, FFP), BF16),
                   jax.ShapeDtypeStruct((T, FFP), BF16), jax.ShapeDtypeStruct((T, FFP), BF16), jax.ShapeDtypeStruct((T, D), F32)],
        args=[x, gpre, wg, wu, wd, gpost], sem=("arbitrary",), duties=duties)


def _ffn_bwd(dxo, x, a, b, h, gpre, gpost, wg, wu, wd, name, after=()):
    tm = 256

    def body(dxo_ref, x_ref, a_ref, b_ref, h_ref, gpre_ref, gpost_ref, wg_ref, wu_ref, wd_ref,
             dx_ref, da_ref, db_ref, dh_ref, dgpre_ref, dgpost_ref):
        @pl.when(pl.program_id(0) == 0)
        def _():
            dgpre_ref[...] = jnp.zeros_like(dgpre_ref)
            dgpost_ref[...] = jnp.zeros_like(dgpost_ref)

        dy = dxo_ref[...]
        h = h_ref[...]
        hn = h * _rs(h)
        r2 = _rs(h)
        dgpost_ref[...] += jnp.sum(0.5 * dy * hn, axis=0, keepdims=True)
        gdy = 0.5 * dy * gpost_ref[...]
        dh = r2 * (gdy - hn * jnp.mean(gdy * hn, axis=-1, keepdims=True))
        dhb = dh.astype(BF16)
        dh_ref[...] = dhb
        dhm = _dot_nt(dhb, wd_ref[...])
        av = a_ref[...].astype(F32)
        bv = b_ref[...].astype(F32)
        sg = _sigmoid(av)
        db = (dhm * (av * sg)).astype(BF16)
        da = (dhm * bv * _dsilu(av, sg)).astype(BF16)
        da_ref[...] = da
        db_ref[...] = db
        dn = _dot(da, wg_ref[...]) + _dot(db, wu_ref[...])
        xv = x_ref[...]
        r = _rs(xv)
        xn = xv * r
        dgpre_ref[...] += jnp.sum(dn * xn, axis=0, keepdims=True)
        gdn = dn * gpre_ref[...]
        dx_ref[...] = dy + r * (gdn - xn * jnp.mean(gdn * xn, axis=-1, keepdims=True))

    return _call(
        body, name=name, grid=(T // tm,),
        in_specs=[_rows(tm, D), _rows(tm, D), _rows(tm, FFP), _rows(tm, FFP), _rows(tm, D), _const((1, D)), _const((1, D)),
                  _resident((FFP, D)), _resident((FFP, D)), _resident((FFP, D))],
        out_specs=[_rows(tm, D), _rows(tm, FFP), _rows(tm, FFP), _rows(tm, D), _const((1, D)), _const((1, D))],
        out_shape=[jax.ShapeDtypeStruct((T, D), F32), jax.ShapeDtypeStruct((T, FFP), BF16), jax.ShapeDtypeStruct((T, FFP), BF16),
                   jax.ShapeDtypeStruct((T, D), BF16), jax.ShapeDtypeStruct((1, D), F32), jax.ShapeDtypeStruct((1, D), F32)],
        args=[dxo, x, a, b, h, gpre, gpost, wg, wu, wd], sem=("arbitrary",), after=after)


def _matmul_tn(a, b, name, tm=FSH, after=(), duties=()):
    k, m = a.shape
    n = b.shape[1]
    assert m % tm == 0

    def body(a_ref, b_ref, o_ref, ob_ref):
        r = _dot_tn(a_ref[...], b_ref[...])
        o_ref[...] = r
        ob_ref[...] = r.astype(BF16)

    spec = pl.BlockSpec((tm, n), lambda i: (i, 0))
    return _call(
        body, name=name, grid=(m // tm,),
        in_specs=[pl.BlockSpec((k, tm), lambda i: (0, i)), _resident((k, n))],
        out_specs=[spec, spec],
        out_shape=[jax.ShapeDtypeStruct((m, n), F32), jax.ShapeDtypeStruct((m, n), BF16)],
        args=[a, b], sem=("arbitrary",), duties=duties, after=after)


def _dwout(attn, yn, dh2):
    rs = (QW + SSM_W) // N_DEV
    half = N_DEV // 2

    def body(at_ref, yn_ref, dh_ref, o_ref, ob_ref):
        i = pl.program_id(0)

        def put(r):
            o_ref[0] = r
            ob_ref[0] = r.astype(BF16)

        @pl.when(i < half)
        def _():
            put(_dot_tn(at_ref[...], dh_ref[...]))

        @pl.when(i >= half)
        def _():
            put(_dot_tn(yn_ref[...], dh_ref[...]))

    spec = pl.BlockSpec((1, rs, D), lambda i: (i, 0, 0))
    return pl.pallas_call(
        body, name="dwout", grid=(N_DEV,),
        in_specs=[pl.BlockSpec((T, rs), lambda i: (0, jnp.minimum(i, half - 1))),
                  pl.BlockSpec((T, rs), lambda i: (0, jnp.maximum(i - half, 0))), _resident((T, D))],
        out_specs=[spec, spec],
        out_shape=[jax.ShapeDtypeStruct((N_DEV, rs, D), F32), jax.ShapeDtypeStruct((N_DEV, rs, D), BF16)],
        compiler_params=_params(("parallel",)),
    )(attn, yn, dh2)


def _rope_swap(t, lo_half):
    return jnp.where(lo_half, pltpu.roll(t, 96, 1), pltpu.roll(t, 32, 1))


def _inproj_fwd(x1, gpre, win, cos, sin_s, ex, duties=()):
    tm = 256

    def body(x_ref, g_ref, w_ref, cos_ref, sin_ref, ex_ref, n_ref, q_ref, kx_ref, vx_ref, xbc_ref, z_ref, dt_ref):
        xv = x_ref[...]
        n = (xv * _rs(xv) * g_ref[...]).astype(BF16)
        n_ref[...] = n
        proj = _dot_nt(n, w_ref[...])
        cs = cos_ref[...]
        sn = sin_ref[...]
        lo_half = (lax.broadcasted_iota(jnp.int32, (1, 128), 1) % HD) < (HD // 2)

        def rope(t):
            return t * cs + _rope_swap(t, lo_half) * sn

        for j in range(QW // 128):
            t = proj[:, 128 * j:128 * j + 128]
            q_ref[:, 128 * j:128 * j + 128] = (rope(t) * (HD ** -0.5)).astype(BF16)
        k = jnp.concatenate([rope(proj[:, QW + 128 * j:QW + 128 * j + 128]) for j in range(KVW // 128)], axis=1)
        v = proj[:, QW + KVW:QW + 2 * KVW]
        kx_ref[...] = _dot(k.astype(BF16), ex_ref[...]).astype(BF16)
        vx_ref[...] = _dot(v.astype(BF16), ex_ref[...]).astype(BF16)
        c0 = QW + 2 * KVW
        xbc_ref[...] = proj[:, c0:c0 + CONV_C]
        z_ref[...] = proj[:, c0 + CONV_C:c0 + CONV_C + SSM_W]
        dt_ref[...] = proj[:, c0 + CONV_C + SSM_W:IN_COLS]

    return _call(
        body, name="inproj_fwd", grid=(T // tm,),
        in_specs=[_rows(tm, D), _const((1, D)), _resident((INP, D)), _rows(tm, 128), _rows(tm, 128), _const((KVW, QW))],
        out_specs=[_rows(tm, D), _rows(tm, QW), _rows(tm, QW), _rows(tm, QW), _rows(tm, CONV_C), _rows(tm, SSM_W), _rows(tm, SSM_H)],
        out_shape=[jax.ShapeDtypeStruct((T, D), BF16), jax.ShapeDtypeStruct((T, QW), BF16), jax.ShapeDtypeStruct((T, QW), BF16),
                   jax.ShapeDtypeStruct((T, QW), BF16), jax.ShapeDtypeStruct((T, CONV_C), F32), jax.ShapeDtypeStruct((T, SSM_W), F32),
                   jax.ShapeDtypeStruct((T, SSM_H), F32)],
        args=[x1, gpre, win, cos, sin_s, ex], sem=("arbitrary",), duties=duties)


def _inproj_bwd(dres, dq, dkx, dvx, dxbc, dz, ddt, x1, gpre, win, cos, sin_s, exf):
    tm = 256

    def body(dres_ref, dq_ref, dkx_ref, dvx_ref, dxbc_ref, dz_ref, ddt_ref, x_ref, g_ref, w_ref, cos_ref, sin_ref, ex_ref,
             dx_ref, dp_ref, dg_ref):
        @pl.when(pl.program_id(0) == 0)
        def _():
            dg_ref[...] = jnp.zeros_like(dg_ref)

        cs = cos_ref[...]
        sn = sin_ref[...]
        lo_half = (lax.broadcasted_iota(jnp.int32, (1, 128), 1) % HD) < (HD // 2)

        def rope_t(t):
            return t * cs - _rope_swap(t, lo_half) * sn

        for j in range(QW // 128):
            dp_ref[:, 128 * j:128 * j + 128] = rope_t(dq_ref[:, 128 * j:128 * j + 128] * (HD ** -0.5)).astype(BF16)
        dk = _dot_nt_hi(dkx_ref[...], ex_ref[...])
        dv = _dot_nt_hi(dvx_ref[...], ex_ref[...])
        for j in range(KVW // 128):
            dp_ref[:, QW + 128 * j:QW + 128 * j + 128] = rope_t(dk[:, 128 * j:128 * j + 128]).astype(BF16)
        dp_ref[:, QW + KVW:QW + 2 * KVW] = dv.astype(BF16)
        c0 = QW + 2 * KVW
        dp_ref[:, c0:c0 + CONV_C] = dxbc_ref[...].astype(BF16)
        dp_ref[:, c0 + CONV_C:c0 + CONV_C + SSM_W] = dz_ref[...].astype(BF16)
        dp_ref[:, c0 + CONV_C + SSM_W:INP] = ddt_ref[...].astype(BF16)
        dn = _dot(dp_ref[...], w_ref[...])
        xv = x_ref[...]
        r = _rs(xv)
        xn = xv * r
        dg_ref[...] += jnp.sum(dn * xn, axis=0, keepdims=True)
        gdn = dn * g_ref[...]
        dx_ref[...] = dres_ref[...] + r * (gdn - xn * jnp.mean(gdn * xn, axis=-1, keepdims=True))

    return pl.pallas_call(
        body, name="inproj_bwd", grid=(T // tm,),
        in_specs=[_rows(tm, D), _rows(tm, QW), _rows(tm, QW), _rows(tm, QW), _rows(tm, CONV_C), _rows(tm, SSM_W), _rows(tm, 128),
                  _rows(tm, D), _const((1, D)), _resident((INP, D)), _rows(tm, 128), _rows(tm, 128), _const((KVW, QW))],
        out_specs=[_rows(tm, D), _rows(tm, INP), _const((1, D))],
        out_shape=[jax.ShapeDtypeStruct((T, D), F32), jax.ShapeDtypeStruct((T, INP), BF16), jax.ShapeDtypeStruct((1, D), F32)],
        compiler_params=_params(("arbitrary",)),
    )(dres, dq, dkx, dvx, dxbc, dz, ddt, x1, gpre, win, cos, sin_s, exf)


def _outproj_fwd(x1, attn, yn, wout, gpost):
    tm = 256

    def body(x_ref, at_ref, yn_ref, w_ref, g_ref, xo_ref, h_ref):
        h = _dot(at_ref[...], w_ref[0:QW, :]) + _dot(yn_ref[...], w_ref[QW:QW + SSM_W, :])
        h_ref[...] = h
        xo_ref[...] = x_ref[...] + h * _rs(h) * g_ref[...]

    return pl.pallas_call(
        body, name="outproj_fwd", grid=(T // tm,),
        in_specs=[_rows(tm, D), _rows(tm, QW), _rows(tm, SSM_W), _resident((QW + SSM_W, D)), _const((1, D))],
        out_specs=[_rows(tm, D), _rows(tm, D)],
        out_shape=[jax.ShapeDtypeStruct((T, D), F32), jax.ShapeDtypeStruct((T, D), F32)],
        compiler_params=_params(("parallel",)),
    )(x1, attn, yn, wout, gpost)


def _outproj_bwd(dx2, h2, gpost, wout, duties=()):
    tm = 256

    def body(dy_ref, h_ref, g_ref, w_ref, dh_ref, dm_ref, dg_ref):
        @pl.when(pl.program_id(0) == 0)
        def _():
            dg_ref[...] = jnp.zeros_like(dg_ref)

        dy = dy_ref[...]
        h = h_ref[...]
        r = _rs(h)
        hn = h * r
        dg_ref[...] += jnp.sum(dy * hn, axis=0, keepdims=True)
        gdy = dy * g_ref[...]
        dh = (r * (gdy - hn * jnp.mean(gdy * hn, axis=-1, keepdims=True))).astype(BF16)
        dh_ref[...] = dh
        dm_ref[...] = _dot_nt(dh, w_ref[...])

    return _call(
        body, name="outproj_bwd", grid=(T // tm,),
        in_specs=[_rows(tm, D), _rows(tm, D), _const((1, D)), _resident((QW + SSM_W, D))],
        out_specs=[_rows(tm, D), _rows(tm, QW + SSM_W), _const((1, D))],
        out_shape=[jax.ShapeDtypeStruct((T, D), BF16), jax.ShapeDtypeStruct((T, QW + SSM_W), F32), jax.ShapeDtypeStruct((1, D), F32)],
        args=[dx2, h2, gpost, wout], sem=("arbitrary",), duties=duties)


def _attn_bias():
    d = jnp.arange(AB)[:, None] - jnp.arange(T)[None, :] + (T - AB)
    cnt = jnp.zeros(d.shape, F32)
    for window, dil in DILATIONS:
        cnt = cnt + ((d >= 0) & (d % dil == 0) & (d <= window)).astype(F32)
    return jnp.where(cnt > 0, jnp.log(jnp.maximum(cnt, 1.0)), NEG)


G_PER = NQ // NKV
WK = G_PER * HD


def _attn_fwd(q, kx, vx, bias, duties=()):
    def body(q_ref, kx_ref, vx_ref, bias_ref, o_ref, lse_ref):
        lane = lax.broadcasted_iota(jnp.int32, (1, WK), 1)
        lse_ref[...] = jnp.zeros_like(lse_ref)
        for i in range(NAB):
            n = (i + 1) * AB
            rows = slice(i * AB, n)
            qi = q_ref[rows, :]
            kxi = kx_ref[0:n, :]
            vxi = vx_ref[0:n, :]
            bb = bias_ref[:, (NAB - 1 - i) * AB:]
            o_acc = jnp.zeros((AB, WK), F32)
            for g in range(G_PER):
                mg = (lane // HD) == g
                s = _dot_nt(jnp.where(mg, qi, jnp.zeros_like(qi)), kxi) + bb
                m = jnp.max(s, axis=1, keepdims=True)
                p = jnp.exp(s - m)
                l = jnp.sum(p, axis=1, keepdims=True)
                o_acc = jnp.where(mg, _dot(p.astype(BF16), vxi) / l, o_acc)
                lse_ref[rows, g:g + 1] = m + jnp.log(l)
            o_ref[rows, :] = o_acc.astype(BF16)

    col = lambda kv: (0, kv)
    return _call(
        body, name="attn_fwd", grid=(NKV,),
        in_specs=[pl.BlockSpec((T, WK), col), pl.BlockSpec((T, WK), col), pl.BlockSpec((T, WK), col), _const((AB, T))],
        out_specs=[pl.BlockSpec((T, WK), col), pl.BlockSpec((T, 128), col)],
        out_shape=[jax.ShapeDtypeStruct((T, QW), BF16), jax.ShapeDtypeStruct((T, NKV * 128), F32)],
        args=[q, kx, vx, bias], sem=("arbitrary",), duties=duties)


def _attn_bwd(q, kx, vx, o, dmix, lse, bias, duties=()):
    def body(q_ref, kx_ref, vx_ref, o_ref, do_ref, lse_ref, bias_ref, dq_ref, dkx_ref, dvx_ref):
        lane = lax.broadcasted_iota(jnp.int32, (1, WK), 1)
        dkx_ref[...] = jnp.zeros_like(dkx_ref)
        dvx_ref[...] = jnp.zeros_like(dvx_ref)
        for i in range(NAB):
            n = (i + 1) * AB
            rows = slice(i * AB, n)
            qi = q_ref[rows, :]
            dof = do_ref[rows, :]
            doi = dof.astype(BF16)
            prod = dof * o_ref[rows, :].astype(F32)
            kxi = kx_ref[0:n, :]
            vxi = vx_ref[0:n, :]
            bb = bias_ref[:, (NAB - 1 - i) * AB:]
            dq_acc = jnp.zeros((AB, WK), F32)
            for g in range(G_PER):
                mg = (lane // HD) == g
                qm = jnp.where(mg, qi, jnp.zeros_like(qi))
                dom = jnp.where(mg, doi, jnp.zeros_like(doi))
                delta = jnp.sum(jnp.where(mg, prod, 0.0), axis=1, keepdims=True)
                p = jnp.exp(_dot_nt(qm, kxi) + bb - lse_ref[rows, g:g + 1])
                ds = (p * (_dot_nt(dom, vxi) - delta)).astype(BF16)
                dvx_ref[0:n, :] += _dot_tn(p.astype(BF16), dom)
                dkx_ref[0:n, :] += _dot_tn(ds, qm)
                dq_acc = jnp.where(mg, _dot(ds, kxi), dq_acc)
            dq_ref[rows, :] = dq_acc

    col = lambda kv: (0, kv)
    return _call(
        body, name="attn_bwd", grid=(NKV,),
        in_specs=[pl.BlockSpec((T, WK), col), pl.BlockSpec((T, WK), col), pl.BlockSpec((T, WK), col), pl.BlockSpec((T, WK), col),
                  pl.BlockSpec((T, WK), col), pl.BlockSpec((T, 128), col), _const((AB, T))],
        out_specs=[pl.BlockSpec((T, WK), col), pl.BlockSpec((T, WK), col), pl.BlockSpec((T, WK), col)],
        out_shape=[jax.ShapeDtypeStruct((T, QW), F32)] * 3,
        args=[q, kx, vx, o, dmix, lse, bias], sem=("arbitrary",), duties=duties)


def _softplus(x):
    return jnp.maximum(x, 0.0) + jnp.log1p(jnp.exp(-jnp.abs(x)))


def _conv_taps(u, halo):
    zext = jnp.concatenate([halo, u], axis=0)
    return [pltpu.roll(zext, m, 0)[8:] for m in (1, 2, 3)]


def _ssd_chunk_common(u, halo, dtr, cw_ref, cb_ref, dtb_ref, alx_ref, e_ref):
    sh1, sh2, sh3 = _conv_taps(u, halo)
    xc = cb_ref[...] + cw_ref[3:4, :] * u + cw_ref[2:3, :] * sh1 + cw_ref[1:2, :] * sh2 + cw_ref[0:1, :] * sh3
    sg = _sigmoid(xc)
    act = xc * sg
    pre_x = _dot_hi(dtr + dtb_ref[...], e_ref[...])
    dt_x = _softplus(pre_x)
    a_x = -jnp.exp(alx_ref[...])
    ri = lax.broadcasted_iota(jnp.int32, (L, L), 0)
    ci = lax.broadcasted_iota(jnp.int32, (L, L), 1)
    tri = ri >= ci
    acs_x = _dot_hi(tri.astype(F32), dt_x * a_x)
    return dict(sh=(sh1, sh2, sh3), xc=xc, sg=sg, act=act, pre_x=pre_x, dt_x=dt_x, a_x=a_x, tri=tri, acs_x=acs_x)


def _decay(acs_x, acs_t, h, tri):
    col = acs_x[:, HD * h:HD * h + 1]
    row = acs_t[HD * h:HD * h + 1, :]
    return jnp.exp(jnp.where(tri, col - row, NEG))


def _ssd_fwd(xbc, z, dtr, convw, convb, dtb, alx, dskx, ssmn, e, duties=()):
    def body(u_ref, z_ref, dtr_ref, cw_ref, cb_ref, dtb_ref, alx_ref, dsk_ref, sn_ref, e_ref,
             yn_ref, y_ref, hs_ref, halo, hst):
        @pl.when(pl.program_id(0) == 0)
        def _():
            halo[...] = jnp.zeros_like(halo)
            hst[...] = jnp.zeros_like(hst)

        u = u_ref[...]
        cm = _ssd_chunk_common(u, halo[...], dtr_ref[...], cw_ref, cb_ref, dtb_ref, alx_ref, e_ref)
        halo[...] = u[L - 8:, :]
        act, dt_x, acs_x, tri = cm["act"], cm["dt_x"], cm["acs_x"], cm["tri"]
        xs = act[:, :SSM_W]
        acs_l = acs_x[L - 1:L, :]
        lam_x = jnp.exp(acs_x)
        w_x = jnp.exp(acs_l - acs_x)
        gam_x = jnp.exp(acs_l)
        acs_t = acs_x.T
        xd = xs * dt_x
        xb = xd.astype(BF16)
        xw = (xd * w_x).astype(BF16)
        lo = lax.broadcasted_iota(jnp.int32, (1, 128), 1) < HD
        hs_ref[0] = hst[...]
        pieces = []
        for grp in range(2):
            bb = act[:, SSM_W + SSM_N * grp:SSM_W + SSM_N * (grp + 1)].astype(BF16)
            cb_ = act[:, SSM_W + 2 * SSM_N + SSM_N * grp:SSM_W + 2 * SSM_N + SSM_N * (grp + 1)].astype(BF16)
            cbm = _dot_nt(cb_, bb)
            for jj in range(4):
                j = 4 * grp + jj
                sl = slice(128 * j, 128 * j + 128)
                m0 = (cbm * _decay(acs_x, acs_t, 2 * j, tri)).astype(BF16)
                m1 = (cbm * _decay(acs_x, acs_t, 2 * j + 1, tri)).astype(BF16)
                x2 = xb[:, sl]
                ydiag = jnp.where(lo, _dot(m0, x2), _dot(m1, x2))
                hprev = hst[j]
                yoff = lam_x[:, sl] * _dot(cb_, hprev.astype(BF16))
                pieces.append(ydiag + yoff)
                hst[j] = gam_x[:, sl] * hprev + _dot_tn(bb, xw[:, sl])
        y = jnp.concatenate(pieces, axis=1) + dsk_ref[...] * xs
        y_ref[...] = y
        zv = z_ref[...]
        yz = y * (zv * _sigmoid(zv))
        half = SSM_W // 2
        yn = jnp.concatenate([yz[:, :half] * _rs(yz[:, :half]), yz[:, half:] * _rs(yz[:, half:])], axis=1)
        yn_ref[...] = (yn * sn_ref[...]).astype(BF16)

    return _call(
        body, name="ssd_fwd", grid=(NCH,),
        in_specs=[_rows(L, CONV_C), _rows(L, SSM_W), _rows(L, SSM_H), _const((4, CONV_C)), _const((1, CONV_C)), _const((1, SSM_H)),
                  _const((1, SSM_W)), _const((1, SSM_W)), _const((1, SSM_W)), _const((SSM_H, SSM_W))],
        out_specs=[_rows(L, SSM_W), _rows(L, SSM_W), pl.BlockSpec((1, 8, SSM_N, 128), lambda c: (c, 0, 0, 0))],
        out_shape=[jax.ShapeDtypeStruct((T, SSM_W), BF16), jax.ShapeDtypeStruct((T, SSM_W), F32),
                   jax.ShapeDtypeStruct((NCH, 8, SSM_N, 128), F32)],
        scratch=[pltpu.VMEM((8, CONV_C), F32), pltpu.VMEM((8, SSM_N, 128), F32)],
        args=[xbc, z, dtr, convw, convb, dtb, alx, dskx, ssmn, e], sem=("arbitrary",), duties=duties)


def _ssd_bwd(dmix, xbc, z, dtr, y, hs, convw, convb, dtb, alx, dskx, ssmn, e, e1, duties=()):
    rev = lambda i: (NCH - 1 - i, 0)

    def body(dyn_ref, u_ref, uh_ref, z_ref, dtr_ref, y_ref, hs_ref, cw_ref, cb_ref, dtb_ref, alx_ref, dsk_ref, sn_ref, e_ref, e1_ref,
             dxbc_ref, dz_ref, ddt_ref, dcw_ref, dcb_ref, dsn_ref, dpar_ref, dh, duh, colbuf):
        step = pl.program_id(0)
        c = NCH - 1 - step

        @pl.when(step == 0)
        def _():
            for r in (dh, duh, dcw_ref, dcb_ref, dsn_ref, dpar_ref):
                r[...] = jnp.zeros_like(r)

        u = u_ref[...]
        halo = jnp.where(c > 0, uh_ref[...], 0.0)
        cm = _ssd_chunk_common(u, halo, dtr_ref[...], cw_ref, cb_ref, dtb_ref, alx_ref, e_ref)
        sh1, sh2, sh3 = cm["sh"]
        xc, sg, act, pre_x, dt_x, a_x, tri, acs_x = (cm[k] for k in ("xc", "sg", "act", "pre_x", "dt_x", "a_x", "tri", "acs_x"))
        xs = act[:, :SSM_W]
        acs_l = acs_x[L - 1:L, :]
        lam_x = jnp.exp(acs_x)
        w_x = jnp.exp(acs_l - acs_x)
        gam_x = jnp.exp(acs_l)
        acs_t = acs_x.T
        xd = xs * dt_x
        xb = xd.astype(BF16)
        xdw = xd * w_x
        xw = xdw.astype(BF16)
        lo = lax.broadcasted_iota(jnp.int32, (1, 128), 1) < HD
        row8 = lax.broadcasted_iota(jnp.int32, (8, 1), 0)

        dyn = dyn_ref[...]
        yv = y_ref[...]
        zv = z_ref[...]
        sz = _sigmoid(zv)
        siluz = zv * sz
        yz = yv * siluz
        half = SSM_W // 2
        gy = dyn * sn_ref[...]
        dyz_parts, yzn_parts = [], []
        for hf in range(2):
            part = yz[:, hf * half:(hf + 1) * half]
            r = _rs(part)
            pn = part * r
            gp = gy[:, hf * half:(hf + 1) * half]
            dyz_parts.append(r * (gp - pn * jnp.mean(gp * pn, axis=-1, keepdims=True)))
            yzn_parts.append(pn)
        dyz = jnp.concatenate(dyz_parts, axis=1)
        dsn_ref[...] += jnp.sum(dyn * jnp.concatenate(yzn_parts, axis=1), axis=0, keepdims=True)
        dy = dyz * siluz
        dz_ref[...] = dyz * yv * _dsilu(zv, sz)

        colbuf[...] = jnp.zeros_like(colbuf)
        dx_pieces, dacs_pieces, dacsl_pieces, db_pieces, dc_pieces = [], [], [], [], []
        for grp in range(2):
            bb = act[:, SSM_W + SSM_N * grp:SSM_W + SSM_N * (grp + 1)].astype(BF16)
            cb_ = act[:, SSM_W + 2 * SSM_N + SSM_N * grp:SSM_W + 2 * SSM_N + SSM_N * (grp + 1)].astype(BF16)
            cbm = _dot_nt(cb_, bb)
            dcbm = jnp.zeros((L, L), F32)
            dc_g = jnp.zeros((L, SSM_N), F32)
            db_g = jnp.zeros((L, SSM_N), F32)
            for jj in range(4):
                j = 4 * grp + jj
                sl = slice(128 * j, 128 * j + 128)
                dy2 = dy[:, sl]
                dy2b = dy2.astype(BF16)
                d0 = _decay(acs_x, acs_t, 2 * j, tri)
                d1 = _decay(acs_x, acs_t, 2 * j + 1, tri)
                m0 = cbm * d0
                m1 = cbm * d1
                x2 = xb[:, sl]
                hprev = hs_ref[0, j]
                hprevb = hprev.astype(BF16)
                dhn = dh[j]
                dhnb = dhn.astype(BF16)
                g2 = _dot(bb, dhnb)
                dx_pieces.append(jnp.where(lo, _dot_tn(m0.astype(BF16), dy2b), _dot_tn(m1.astype(BF16), dy2b)) + w_x[:, sl] * g2)
                zero = jnp.zeros_like(dy2b)
                dm0 = _dot_nt(jnp.where(lo, dy2b, zero), x2)
                dm1 = _dot_nt(jnp.where(lo, zero, dy2b), x2)
                dcbm = dcbm + dm0 * d0 + dm1 * d1
                e0 = dm0 * m0
                e1v = dm1 * m1
                colbuf[:, 2 * j:2 * j + 1] = jnp.sum(e0, axis=1, keepdims=True) - jnp.sum(e0.T, axis=1, keepdims=True)
                colbuf[:, 2 * j + 1:2 * j + 2] = jnp.sum(e1v, axis=1, keepdims=True) - jnp.sum(e1v.T, axis=1, keepdims=True)
                yoff = lam_x[:, sl] * _dot(cb_, hprevb)
                gxw = g2 * xdw[:, sl]
                dacs_pieces.append(dy2 * yoff - gxw)
                dacsl_pieces.append(jnp.sum(gxw, axis=0, keepdims=True) + gam_x[:, sl] * jnp.sum(dhn * hprev, axis=0, keepdims=True))
                dyl = (dy2 * lam_x[:, sl]).astype(BF16)
                dc_g = dc_g + _dot_nt(dyl, hprevb)
                db_g = db_g + _dot_nt(xw[:, sl], dhnb)
                dh[j] = gam_x[:, sl] * dhn + _dot_tn(cb_, dyl)
            dcbb = dcbm.astype(BF16)
            dc_pieces.append(dc_g + _dot(dcbb, bb))
            db_pieces.append(db_g + _dot_tn(dcbb, cb_))

        dxd = jnp.concatenate(dx_pieces, axis=1)
        rowi = lax.broadcasted_iota(jnp.int32, (L, 1), 0)
        dacs_x = (jnp.concatenate(dacs_pieces, axis=1) + _dot_hi(colbuf[...], e1_ref[...])
                  + jnp.where(rowi == L - 1, jnp.concatenate(dacsl_pieces, axis=1), 0.0))
        upper = lax.broadcasted_iota(jnp.int32, (L, L), 0) <= lax.broadcasted_iota(jnp.int32, (L, L), 1)
        dadt_x = _dot_hi(upper.astype(F32), dacs_x)
        ddt_x = dxd * xs + dadt_x * a_x
        ddtr = _dot_nt_hi(ddt_x * _sigmoid(pre_x), e_ref[...])
        ddt_ref[...] = jnp.zeros_like(ddt_ref)
        ddt_ref[:, 0:SSM_H] = ddtr
        dalx =jnp.sum(dadt_x * dt_x, axis=0, keepdims=True) * a_x
        ddskx = jnp.sum(dy * xs, axis=0, keepdims=True)
        par_x = jnp.where(row8 == 1, dalx, 0.0) + jnp.where(row8 == 2, ddskx, 0.0)
        dpar_ref[...] += _dot_nt_hi(par_x, e_ref[...]) + jnp.where(row8 == 0, jnp.sum(ddtr, axis=0, keepdims=True), 0.0)

        dxs = dxd * dt_x + dsk_ref[...] * dy
        dact = jnp.concatenate([dxs] + db_pieces + dc_pieces, axis=1)
        du = dact * _dsilu(xc, sg)
        dcb_ref[...] += jnp.sum(du, axis=0, keepdims=True)
        taps = (sh3, sh2, sh1, u)
        dcw = jnp.zeros((8, CONV_C), F32)
        for k in range(4):
            dcw = dcw + jnp.where(row8 == k, jnp.sum(du * taps[k], axis=0, keepdims=True), 0.0)
        dcw_ref[...] += dcw
        zext = jnp.concatenate([du, duh[...]], axis=0)
        f1, f2, f3 = (pltpu.roll(zext, L + 8 - m, 0)[:L] for m in (1, 2, 3))
        dxbc_ref[...] = cw_ref[3:4, :] * du + cw_ref[2:3, :] * f1 + cw_ref[1:2, :] * f2 + cw_ref[0:1, :] * f3
        duh[...] = du[:8, :]

    return _call(
        body, name="ssd_bwd", grid=(NCH,),
        in_specs=[pl.BlockSpec((L, SSM_W), lambda i: (NCH - 1 - i, 1)), pl.BlockSpec((L, CONV_C), rev),
                  pl.BlockSpec((8, CONV_C), lambda i: (jnp.maximum((NCH - 1 - i) * (L // 8) - 1, 0), 0)),
                  pl.BlockSpec((L, SSM_W), rev), pl.BlockSpec((L, SSM_H), rev), pl.BlockSpec((L, SSM_W), rev),
                  pl.BlockSpec((1, 8, SSM_N, 128), lambda i: (NCH - 1 - i, 0, 0, 0)),
                  _const((4, CONV_C)), _const((1, CONV_C)), _const((1, SSM_H)), _const((1, SSM_W)), _const((1, SSM_W)), _const((1, SSM_W)),
                  _const((SSM_H, SSM_W)), _const((128, SSM_W))],
        out_specs=[pl.BlockSpec((L, CONV_C), rev), pl.BlockSpec((L, SSM_W), rev), pl.BlockSpec((L, 128), rev),
                   _const((8, CONV_C)), _const((1, CONV_C)), _const((1, SSM_W)), _const((8, SSM_H))],
        out_shape=[jax.ShapeDtypeStruct((T, CONV_C), F32), jax.ShapeDtypeStruct((T, SSM_W), F32), jax.ShapeDtypeStruct((T, 128), F32),
                   jax.ShapeDtypeStruct((8, CONV_C), F32), jax.ShapeDtypeStruct((1, CONV_C), F32), jax.ShapeDtypeStruct((1, SSM_W), F32),
                   jax.ShapeDtypeStruct((8, SSM_H), F32)],
        scratch=[pltpu.VMEM((8, SSM_N, 128), F32), pltpu.VMEM((8, CONV_C), F32), pltpu.VMEM((L, 128), F32)],
        args=[dmix, xbc, xbc, z, dtr, y, hs, convw, convb, dtb, alx, dskx, ssmn, e, e1], sem=("arbitrary",), duties=duties)


def _loss_head(x3, target):
    tm = 512

    def body(x_ref, t_ref, dy_ref, ss_ref):
        @pl.when(pl.program_id(0) == 0)
        def _():
            ss_ref[...] = jnp.zeros_like(ss_ref)

        err = x_ref[...] - t_ref[...]
        dy_ref[...] = err * (1.0 / D)
        ss_ref[...] += jnp.sum(jnp.sum(err * err, axis=1, keepdims=True), axis=0, keepdims=True)

    return pl.pallas_call(
        body, name="loss_head", grid=(T // tm,),
        in_specs=[_rows(tm, D), _rows(tm, D)],
        out_specs=[_rows(tm, D), _const((1, 128))],
        out_shape=[jax.ShapeDtypeStruct((T, D), F32), jax.ShapeDtypeStruct((1, 128), F32)],
        compiler_params=_params(("arbitrary",)),
    )(x3, target)


def _adam_math(w, g, m, v):
    m = ADAM_B1 * m + (1.0 - ADAM_B1) * g
    v = ADAM_B2 * v + (1.0 - ADAM_B2) * (g * g)
    m_hat = m / (1.0 - ADAM_B1 ** ADAM_STEP)
    v_hat = v / (1.0 - ADAM_B2 ** ADAM_STEP)
    delta = -ADAM_LR * (m_hat / (jnp.sqrt(v_hat) + ADAM_EPS) + ADAM_WD * w)
    return delta, m, v


def _adamw(w, m, v, parts, name, after=()):
    rows, cols = w.shape
    tr = rows if rows <= 512 else 256
    assert rows % tr == 0
    n_parts = len(parts)

    def body(*refs):
        w_ref, m_ref, v_ref = refs[:3]
        p_refs = refs[3:3 + n_parts]
        g_ref, d_ref, nm_ref, nv_ref = refs[3 + n_parts + len(after):]
        g = p_refs[0][...].astype(F32)
        for p in p_refs[1:]:
            g = g + p[...].astype(F32)
        delta, nm, nv = _adam_math(w_ref[...], g, m_ref[...], v_ref[...])
        g_ref[...] = g
        d_ref[...] = delta
        nm_ref[...] = nm
        nv_ref[...] = nv

    spec = pl.BlockSpec((tr, cols), lambda i: (i, 0))
    return pl.pallas_call(
        body, name=name, grid=(rows // tr,),
        in_specs=[spec] * (3 + n_parts) + [ANY] * len(after), out_specs=[spec] * 4,
        out_shape=[jax.ShapeDtypeStruct((rows, cols), F32)] * 4,
        compiler_params=_params(("parallel",)),
    )(w, m, v, *parts, *after)


COL_TILE = 512


def _adamw_sharded(w, m, v, chip_sum, from_chips, chip_ids, name, after=()):
    rows, cols = w.shape
    prow = chip_sum.shape[1]
    assert cols % COL_TILE == 0 and prow >= rows and chip_sum.shape[2] == cols

    def body(ids_ref, w_ref, m_ref, v_ref, s_ref, r1_ref, r2_ref, r3_ref, *rest):
        g_ref, d_ref, nm_ref, nv_ref = rest[len(after):]
        g = s_ref[0]
        for r in (r1_ref, r2_ref, r3_ref):
            g = g + r[0].astype(F32)
        g = g[:rows]
        delta, nm, nv = _adam_math(w_ref[...], g, m_ref[...], v_ref[...])
        g_ref[...] = g
        d_ref[...] = delta
        nm_ref[...] = nm
        nv_ref[...] = nv

    spec = pl.BlockSpec((rows, COL_TILE), lambda i, ids: (0, i))
    part = lambda k: pl.BlockSpec((1, prow, COL_TILE), lambda i, ids: (ids[k], 0, i))
    return pl.pallas_call(
        body, name=name,
        grid_spec=pltpu.PrefetchScalarGridSpec(
            num_scalar_prefetch=1, grid=(cols // COL_TILE,),
            in_specs=[spec, spec, spec, part(0), part(1), part(2), part(3)] + [ANY] * len(after), out_specs=[spec] * 4),
        out_shape=[jax.ShapeDtypeStruct((rows, cols), F32)] * 4,
        compiler_params=_params(("parallel",)),
    )(chip_ids, w, m, v, chip_sum, from_chips, from_chips, from_chips, *after)


def _chip_sum(g32, recv, core, name):
    rows, cols = g32.shape[1:]

    def body(core_ref, a_ref, b_ref, s_ref, sb_ref):
        s = a_ref[...] + b_ref[...].astype(F32)
        s_ref[...] = s
        sb_ref[...] = s.astype(BF16)

    by_chip = pl.BlockSpec((1, rows, cols), lambda k, core_ref: (k, 0, 0))
    return pl.pallas_call(
        body, name=name,
        grid_spec=pltpu.PrefetchScalarGridSpec(
            num_scalar_prefetch=1, grid=(N_DEV // 2,),
            in_specs=[pl.BlockSpec((1, rows, cols), lambda k, core_ref: (2 * k + core_ref[0], 0, 0)), by_chip],
            out_specs=[by_chip, by_chip]),
        out_shape=[jax.ShapeDtypeStruct((N_DEV // 2, rows, cols), F32), jax.ShapeDtypeStruct((N_DEV // 2, rows, cols), BF16)],
        compiler_params=_params(("parallel",)),
    )(core, g32, recv)


def _all_reduce_small(v, after=()):
    rows = v.shape[0]

    def body(v_ref, *rest):
        out_ref, gath, send_sems, recv_sems = rest[len(after):]
        x, y, c = _place()
        me, sibling = (x, y, c), (x, y, 1 - c)
        chips = [(1 - x, y), (x, 1 - y), (1 - x, 1 - y)]

        def blk(px, py, pc):
            return gath.at[pl.ds((4 * px + 2 * py + pc) * rows, rows), :]

        def copy(k, block, to, src=None):
            return pltpu.make_async_remote_copy(src_ref=blk(*block) if src is None else src, dst_ref=blk(*block),
                                                send_sem=send_sems.at[k], recv_sem=recv_sems.at[k], device_id=to, device_id_type=MESH)

        gath[pl.ds((4 * x + 2 * y + c) * rows, rows), :] = v_ref[...]
        first = [copy(0, me, sibling, src=v_ref)] + [copy(1 + j, me, (*chip, c), src=v_ref) for j, chip in enumerate(chips)]
        for cp in first:
            cp.start()
        passed = [copy(4 + j, (*chip, c), sibling) for j, chip in enumerate(chips)]
        for j, chip in enumerate(chips):
            copy(1 + j, (*chip, c), me).wait_recv()
            passed[j].start()
        copy(0, sibling, me).wait_recv()
        for j, chip in enumerate(chips):
            copy(4 + j, (*chip, 1 - c), me).wait_recv()
        for cp in first + passed:
            cp.wait_send()
        acc = gath[0:rows, :]
        for d in range(1, N_DEV):
            acc = acc + gath[d * rows:(d + 1) * rows, :]
        out_ref[...] = acc

    vm = pl.BlockSpec(memory_space=pltpu.VMEM)
    return pl.pallas_call(
        body, name="all_reduce_small",
        in_specs=[vm] + [ANY] * len(after), out_specs=vm,
        out_shape=jax.ShapeDtypeStruct(v.shape, F32),
        scratch_shapes=[pltpu.VMEM((N_DEV * rows, 128), F32), pltpu.SemaphoreType.DMA((7,)), pltpu.SemaphoreType.DMA((7,))],
    )(v, *after)


def _rope_tables(positions):
    inv_freq = ROPE_THETA ** (-jnp.arange(0, HD, 2, dtype=F32) / HD)
    ang = positions.reshape(T).astype(F32)[:, None] * inv_freq
    ang = jnp.concatenate([ang, ang, ang, ang], axis=-1)
    lo_half = (jnp.arange(128) % HD) < (HD // 2)
    return jnp.cos(ang), jnp.where(lo_half, -jnp.sin(ang), jnp.sin(ang))


def _selectors():
    lane = jnp.arange(QW)
    e = (lane[None, :] // HD == jnp.arange(SSM_H)[:, None]).astype(F32)
    e1 = ((lane[None, :] == HD * jnp.arange(128)[:, None]) & (jnp.arange(128)[:, None] < SSM_H)).astype(F32)
    src = jnp.arange(KVW)
    ex = ((lane[None, :] // (HD * (NQ // NKV)) == src[:, None] // HD) & (lane[None, :] % HD == src[:, None] % HD)).astype(F32)
    return e, e1, ex


WEIGHTS = ['ffn1_pre_norm', 'ffn1_w_gate', 'ffn1_w_up', 'ffn1_w_down', 'ffn1_post_norm', 'mix_pre_norm', 'w_in', 'conv_w', 'conv_b',
           'dt_bias', 'a_log', 'd_skip', 'ssm_norm', 'w_out', 'mix_post_norm', 'ffn2_pre_norm', 'ffn2_w_gate', 'ffn2_w_up',
           'ffn2_w_down', 'ffn2_post_norm']
COL_SHARDED = ['ffn1_w_gate', 'ffn1_w_up', 'ffn2_w_gate', 'ffn2_w_up', 'w_in']
ROW_SHARDED = ['ffn1_w_down', 'ffn2_w_down', 'w_out']
BIG = COL_SHARDED + ROW_SHARDED
FFN_BIG = COL_SHARDED[:4] + ROW_SHARDED[:2]
SMALL = ['ffn1_pre_norm', 'ffn1_post_norm', 'mix_pre_norm', 'conv_b', 'dt_bias', 'a_log', 'd_skip', 'ssm_norm', 'mix_post_norm',
         'ffn2_pre_norm', 'ffn2_post_norm']
FFN1 = ['ffn1_w_gate', 'ffn1_w_up', 'ffn1_w_down']
FFN2 = ['ffn2_w_gate', 'ffn2_w_up', 'ffn2_w_down']


def _wire_block(name, a):
    if name in FFN_BIG:
        return jnp.pad(a.astype(BF16), ((0, FSH - FSR), (0, 0)))
    return a if name == "conv_w" else a.astype(BF16)


def _whole_from_gathered(name, a):
    if name == "conv_w":
        return jnp.transpose(a, (1, 0, 2)).reshape(a.shape[1], -1)
    a = a.reshape(-1, D)
    return jnp.concatenate([a, jnp.zeros((INP - IN_COLS, D), a.dtype)]) if name == "w_in" else a


def _step(x, positions, target, small, blocks=None, whole=None, core=None):
    dist = blocks is not None
    w = dict(small)
    if whole:
        w.update(whole)

    def gather(names):
        return [_gather_duty([_wire_block(n, blocks[n]) for n in names])] if dist else []

    def put(names, results):
        if dist:
            for n, r in zip(names, results[0]):
                w[n] = _whole_from_gathered(n, r)

    def row_shards(pair):
        return tuple(a.reshape(N_DEV, -1, a.shape[1]) for a in pair)

    g, sums, red = {}, {}, {}

    def swap(names):
        return [_swap_duty([g[n][1] for n in names])] if dist else []

    def chip_sums(names, from_sibling):
        if dist:
            for n, recv in zip(names, from_sibling):
                sums[n] = _chip_sum(g[n][0], recv, core, "chip_sum_" + n)

    def exchange(names):
        return [_exchange_duty([sums[n][1] for n in names])] if dist else []

    def reduced(names, from_chips):
        if dist:
            for n, recv in zip(names, from_chips):
                red[n] = (sums[n][0], recv)

    cos, sin_s = _rope_tables(positions)
    e, e1, exf = _selectors()
    bias = _attn_bias()
    alx = jnp.repeat(w["a_log"], HD, axis=1)
    dskx = jnp.repeat(w["d_skip"], HD, axis=1)

    if dist:
        put(FFN1, _comm_only(gather(FFN1), "gather_ffn1"))
    (x1, n1, a1, b1, hm1, h1), got = _ffn_fwd(x, w["ffn1_pre_norm"], w["ffn1_w_gate"], w["ffn1_w_up"], w["ffn1_w_down"],
                                              w["ffn1_post_norm"], "ffn1_fwd", gather(["w_in", "conv_w"]))
    put(["w_in", "conv_w"], got)
    (n2, q, kx, vx, xbc, z, dtr), got = _inproj_fwd(x1, w["mix_pre_norm"], w["w_in"], cos, sin_s, exf.astype(BF16), gather(["w_out"]))
    put(["w_out"], got)
    (attn, lse), got = _attn_fwd(q, kx, vx, bias, gather(FFN2[:2]))
    put(FFN2[:2], got)
    (yn, y, hs), got = _ssd_fwd(xbc, z, dtr, w["conv_w"], w["conv_b"], w["dt_bias"], alx, dskx, w["ssm_norm"], e, gather(FFN2[2:]))
    put(FFN2[2:], got)
    x2, h2 = _outproj_fwd(x1, attn, yn, w["w_out"], w["mix_post_norm"])
    (x3, n3, a3, b3, hm3, h3), _ = _ffn_fwd(x2, w["ffn2_pre_norm"], w["ffn2_w_gate"], w["ffn2_w_up"], w["ffn2_w_down"],
                                            w["ffn2_post_norm"], "ffn2_fwd")
    dx3, ss = _loss_head(x3, target)

    (dx2, da3, db3, dh3, g["ffn2_pre_norm"], g["ffn2_post_norm"]), _ = _ffn_bwd(
        dx3, x2, a3, b3, h3, w["ffn2_pre_norm"], w["ffn2_post_norm"], w["ffn2_w_gate"], w["ffn2_w_up"], w["ffn2_w_down"], "ffn2_bwd")
    g["ffn2_w_down"] = row_shards(_matmul_tn(hm3, dh3, "ffn2_dwd")[0])
    g["ffn2_w_gate"] = row_shards(_matmul_tn(da3, n3, "ffn2_dwg")[0])
    g["ffn2_w_up"] = row_shards(_matmul_tn(db3, n3, "ffn2_dwu")[0])

    (dh2, dmix, g["mix_post_norm"]), got = _outproj_bwd(dx2, h2, w["mix_post_norm"], w["w_out"], swap(FFN2))
    chip_sums(FFN2, got[0] if dist else None)
    g["w_out"] = _dwout(attn, yn, dh2)
    (dq, dkx, dvx), got = _attn_bwd(q, kx, vx, attn, dmix, lse, bias, exchange(FFN2) + swap(["w_out"]))
    if dist:
        reduced(FFN2, got[0])
        chip_sums(["w_out"], got[1])
    (dxbc, dz, ddt, dcw, g["conv_b"], g["ssm_norm"], dpar), got = _ssd_bwd(
        dmix, xbc, z, dtr, y, hs, w["conv_w"], w["conv_b"], w["dt_bias"], alx, dskx, w["ssm_norm"], e, e1, exchange(["w_out"]))
    reduced(["w_out"], got[0] if dist else None)
    g["conv_w"] = dcw[0:4]
    g["dt_bias"], g["a_log"], g["d_skip"] = dpar[0:1], dpar[1:2], dpar[2:3]
    dx1, dproj, g["mix_pre_norm"] = _inproj_bwd(dx2, dq, dkx, dvx, dxbc, dz, ddt, x1, w["mix_pre_norm"], w["w_in"], cos, sin_s, exf)
    dwin32 = _matmul_tn(dproj, n2, "dwin")[0][0][:IN_COLS].reshape(N_DEV, -1, D)
    g["w_in"] = (dwin32, dwin32.astype(BF16))

    started = {}

    def start(n):
        started[n] = _exchange_start(sums[n][1], "start_exchange_" + n)
        return [started[n]["token"]]

    after = []
    if dist:
        chip_sums(["w_in"], _comm_only(swap(["w_in"]), "swap_w_in")[0])
        after = start("w_in")
    (dx0, da1, db1, dh1, g["ffn1_pre_norm"], g["ffn1_post_norm"]), _ = _ffn_bwd(
        dx1, x, a1, b1, h1, w["ffn1_pre_norm"], w["ffn1_post_norm"], w["ffn1_w_gate"], w["ffn1_w_up"], w["ffn1_w_down"], "ffn1_bwd",
        after)
    total = None
    if dist:
        flat = jnp.concatenate([g[n].reshape(-1) for n in SMALL] + [g["conv_w"].reshape(-1), ss[0, 0:1]])
        rows = -(-flat.shape[0] // 128 // 8) * 8
        total = _all_reduce_small(jnp.pad(flat, (0, rows * 128 - flat.shape[0])).reshape(rows, 128), after)
        after = [total]
    own, _ = _matmul_tn(hm1, dh1, "ffn1_dwd", after=after)
    g["ffn1_w_down"] = row_shards(own)
    own, got = _matmul_tn(da1, n1, "ffn1_dwg", duties=swap(["ffn1_w_down"]))
    g["ffn1_w_gate"] = row_shards(own)
    if dist:
        chip_sums(["ffn1_w_down"], got[0])
        after = start("ffn1_w_down")
    own, got = _matmul_tn(db1, n1, "ffn1_dwu", after=after, duties=swap(["ffn1_w_gate"]))
    g["ffn1_w_up"] = row_shards(own)
    if dist:
        chip_sums(["ffn1_w_gate"], got[0])
        after = start("ffn1_w_gate")
        chip_sums(["ffn1_w_up"], _comm_only(swap(["ffn1_w_up"]), "swap_ffn1_w_up", after=after)[0])
        start("ffn1_w_up")
    return ss, dx0, g, red, {n: (sums[n][0], started[n]) for n in started}, total


def kernel(x, positions, ffn1_pre_norm, ffn1_w_gate, ffn1_w_up, ffn1_w_down, ffn1_post_norm, mix_pre_norm, w_in, conv_w, conv_b, dt_bias, a_log, d_skip, ssm_norm, w_out, mix_post_norm, ffn2_pre_norm, ffn2_w_gate, ffn2_w_up, ffn2_w_down, ffn2_post_norm, loss_target, m_ffn1_pre_norm, m_ffn1_w_gate, m_ffn1_w_up, m_ffn1_w_down, m_ffn1_post_norm, m_mix_pre_norm, m_w_in, m_conv_w, m_conv_b, m_dt_bias, m_a_log, m_d_skip, m_ssm_norm, m_w_out, m_mix_post_norm, m_ffn2_pre_norm, m_ffn2_w_gate, m_ffn2_w_up, m_ffn2_w_down, m_ffn2_post_norm, v_ffn1_pre_norm, v_ffn1_w_gate, v_ffn1_w_up, v_ffn1_w_down, v_ffn1_post_norm, v_mix_pre_norm, v_w_in, v_conv_w, v_conv_b, v_dt_bias, v_a_log, v_d_skip, v_ssm_norm, v_w_out, v_mix_post_norm, v_ffn2_pre_norm, v_ffn2_w_gate, v_ffn2_w_up, v_ffn2_w_down, v_ffn2_post_norm):
    given = dict(locals())
    drop = lambda n, a: a if n in SMALL else (a[0].T if n in COL_SHARDED else a[0])
    w = {n: drop(n, given[n]) for n in WEIGHTS}
    m = {n: drop(n, given["m_" + n]) for n in WEIGHTS}
    v = {n: drop(n, given["v_" + n]) for n in WEIGHTS}
    cx, cy, cc = _place()
    my_chip = 2 * cx + cy
    others = [2 * (1 - cx) + cy, 2 * cx + (1 - cy), 2 * (1 - cx) + (1 - cy)]

    core = jnp.stack([cc]).astype(jnp.int32)
    _, grad_x, g, red, pending, total = _step(x[0], positions, loss_target[0], {n: w[n] for n in SMALL},
                                              blocks={n: w[n] for n in BIG + ["conv_w"]}, core=core)
    total = total.reshape(-1)
    chip_ids = jnp.stack([my_chip] + others).astype(jnp.int32)
    out_g, out_d, out_m, out_v = {}, {}, {}, {}
    last_start = [pending["ffn1_w_up"][1]["token"]]
    for n in BIG:
        if n not in pending:
            out_g[n], out_d[n], out_m[n], out_v[n] = _adamw_sharded(w[n], m[n], v[n], red[n][0], red[n][1], chip_ids, "adamw_" + n,
                                                                    last_start)

    rows = total.shape[0] // 128
    sizes = [w[n].size for n in SMALL]
    offs = [0]
    for s_ in sizes:
        offs.append(offs[-1] + s_)
    gs = {n: total[offs[i]:offs[i + 1]].reshape(w[n].shape) for i, n in enumerate(SMALL)}
    gcw = total[offs[-1]:offs[-1] + 4 * CONV_C].reshape(4, CONV_C)
    loss = 0.5 * total[offs[-1] + 4 * CONV_C] / D
    gs["conv_w"] = lax.dynamic_slice_in_dim(gcw, (4 * cx + 2 * cy + cc) * (CONV_C // N_DEV), CONV_C // N_DEV, axis=1)
    names = SMALL + ["conv_w"]

    def pack(d):
        flat = jnp.concatenate([d[n].reshape(-1) for n in names])
        return jnp.pad(flat, (0, rows * 128 - flat.shape[0])).reshape(rows, 128)

    pg, pd, pm, pv = _adamw(pack(w), pack(m), pack(v), [pack(gs)], "adamw_small", last_start)
    done = [pg] + [out_v[n] for n in BIG if n not in pending]
    for n, (chip_sum, started) in pending.items():
        recv = _exchange_wait(started, done, "wait_exchange_" + n)
        out_g[n], out_d[n], out_m[n], out_v[n] = _adamw_sharded(w[n], m[n], v[n], chip_sum, recv, chip_ids, "adamw_" + n)
        done = [out_v[n]]
    o2 = [0]
    for n in names:
        o2.append(o2[-1] + w[n].size)
    for i, n in enumerate(names):
        for dst, src in ((out_g, pg), (out_d, pd), (out_m, pm), (out_v, pv)):
            dst[n] = src.reshape(-1)[o2[i]:o2[i + 1]].reshape(w[n].shape)

    outs = [loss, grad_x[None]]
    for d in (out_g, out_d, out_m, out_v):
        outs += [d[n] if n in SMALL else (d[n].T[None] if n in COL_SHARDED else d[n][None]) for n in WEIGHTS]
    return tuple(outs)
```

```python
import functools
import math

import jax
import jax.numpy as jnp
from jax import lax
from jax.experimental import pallas as pl
from jax.experimental.pallas import tpu as pltpu

F32 = jnp.float32
BF16 = jnp.bfloat16
MESH = pl.DeviceIdType.MESH

N_DEV = 8
T = 2048
D = 1024
FF = 2816
FSR = FF // N_DEV
FSH = 384
FFP = N_DEV * FSH
HD = 64
NQ = 16
NKV = 4
QW = NQ * HD
KVW = NKV * HD
SSM_W = 1024
SSM_H = 16
SSM_N = 128
CONV_C = SSM_W + 2 * 2 * SSM_N
IN_COLS = 4112
INP = 4224
L = 128
NCH = T // L
AB = 256
NAB = T // AB
EPS = 1e-6
NEG = -1e30
ROPE_THETA = 10000.0
DILATIONS = ((128, 1), (512, 4), (2048, 16))

ADAM_LR = 0.001
ADAM_B1 = 0.9
ADAM_B2 = 0.999
ADAM_EPS = 1e-08
ADAM_WD = 0.01
ADAM_STEP = 10

VMEM_LIMIT = 58 * 1024 * 1024


def _params(sem, vmem=VMEM_LIMIT):
    return pltpu.CompilerParams(dimension_semantics=sem, vmem_limit_bytes=vmem)


def _dot(a, b):
    return jnp.dot(a, b, preferred_element_type=F32)


def _dot_nt(a, b):
    return lax.dot_general(a, b, (((1,), (1,)), ((), ())), preferred_element_type=F32)


def _dot_tn(a, b):
    return lax.dot_general(a, b, (((0,), (0,)), ((), ())), preferred_element_type=F32)


def _split3(x):
    hi = x.astype(BF16)
    r1 = x - hi.astype(F32)
    mid = r1.astype(BF16)
    lo = (r1 - mid.astype(F32)).astype(BF16)
    return hi, mid, lo


def _dot_hi(a, b, a_is_01=False):
    if a_is_01:
        sel = a.astype(BF16)
        return sum(_dot(sel, p) for p in _split3(b))
    sel = b.astype(BF16)
    return sum(_dot(p, sel) for p in _split3(a))


def _dot_nt_hi(a, b):
    sel = b.astype(BF16)
    return sum(_dot_nt(p, sel) for p in _split3(a))


def _rs(x):
    return lax.rsqrt(jnp.mean(x * x, axis=-1, keepdims=True) + EPS)


def _sigmoid(x):
    return jax.nn.sigmoid(x)


def _dsilu(x, s):
    return s * (1.0 + x * (1.0 - s))


def _resident(shape):
    nd = len(shape)
    return pl.BlockSpec(shape, lambda *_: (0,) * nd, pipeline_mode=pl.Buffered(1))


def _const(shape):
    nd = len(shape)
    return pl.BlockSpec(shape, lambda *_: (0,) * nd)


def _rows(tm, cols):
    return pl.BlockSpec((tm, cols), lambda i: (i, 0))


ANY = pl.BlockSpec(memory_space=pl.ANY)


def _place():
    return lax.axis_index("x"), lax.axis_index("y"), lax.axis_index("c")


def _gather_duty(arrays):
    n = len(arrays)
    results = [jax.ShapeDtypeStruct((N_DEV,) + a.shape, a.dtype) for a in arrays]

    def make(ins, outs, send_sems, recv_sems, local_sems):
        x, y, c = _place()
        me, sibling = (x, y, c), (x, y, 1 - c)
        chips = [(1 - x, y), (x, 1 - y), (1 - x, 1 - y)]

        def place_of(a, px, py, pc):
            return outs[a].at[4 * px + 2 * py + pc]

        def copy(a, k, block, to, src=None):
            dst = place_of(a, *block)
            return pltpu.make_async_remote_copy(src_ref=dst if src is None else src, dst_ref=dst,
                                                send_sem=send_sems.at[7 * a + k], recv_sem=recv_sems.at[7 * a + k],
                                                device_id=to, device_id_type=MESH)

        def own(a):
            return pltpu.make_async_copy(ins[a], place_of(a, *me), local_sems.at[a])

        def first(a):
            return [copy(a, 0, me, sibling, src=ins[a])] + [copy(a, 1 + j, me, (*chip, c), src=ins[a]) for j, chip in enumerate(chips)]

        def start():
            for a in range(n):
                own(a).start()
            for a in range(n):
                for cp in first(a):
                    cp.start()

        def finish():
            for j, chip in enumerate(chips):
                for a in range(n):
                    copy(a, 1 + j, (*chip, c), me).wait_recv()
                    copy(a, 4 + j, (*chip, c), sibling).start()
            for a in range(n):
                copy(a, 0, sibling, me).wait_recv()
                for j, chip in enumerate(chips):
                    copy(a, 4 + j, (*chip, 1 - c), me).wait_recv()
            for a in range(n):
                for cp in first(a) + [copy(a, 4 + j, (*chip, c), sibling) for j, chip in enumerate(chips)]:
                    cp.wait_send()
                own(a).wait()

        return start, finish

    return dict(operands=list(arrays), results=results, sems=(7 * n, 7 * n, n), make=make)


def _swap_duty(arrays):
    n = len(arrays)
    half = N_DEV // 2
    results = [jax.ShapeDtypeStruct(a.shape, a.dtype) for a in arrays]

    def make(ins, outs, send_sems, recv_sems):
        x, y, c = _place()

        def copies():
            return [pltpu.make_async_remote_copy(src_ref=ins[a].at[k], dst_ref=outs[a].at[k],
                                                 send_sem=send_sems.at[half * a + k], recv_sem=recv_sems.at[half * a + k],
                                                 device_id=(x, y, 1 - c), device_id_type=MESH)
                    for a in range(n) for k in range(half)]

        def start():
            for cp in copies():
                cp.start()

        def finish():
            for cp in copies():
                cp.wait()

        return start, finish

    return dict(operands=list(arrays), results=results, sems=(half * n, half * n), make=make)


def _exchange_duty(arrays):
    n = len(arrays)
    results = [jax.ShapeDtypeStruct(a.shape, a.dtype) for a in arrays]

    def make(ins, outs, send_sems, recv_sems):
        x, y, c = _place()
        chips = [(1 - x, y), (x, 1 - y), (1 - x, 1 - y)]
        my_chip = 2 * x + y

        def sends():
            return [pltpu.make_async_remote_copy(src_ref=ins[a].at[2 * px + py], dst_ref=outs[a].at[my_chip],
                                                 send_sem=send_sems.at[3 * a + j], recv_sem=recv_sems.at[3 * a + j],
                                                 device_id=(px, py, c), device_id_type=MESH)
                    for a in range(n) for j, (px, py) in enumerate(chips)]

        def start():
            for cp in sends():
                cp.start()

        def finish():
            for a in range(n):
                for j, (px, py) in enumerate(chips):
                    pltpu.make_async_remote_copy(src_ref=ins[a].at[my_chip], dst_ref=outs[a].at[2 * px + py],
                                                 send_sem=send_sems.at[3 * a + j], recv_sem=recv_sems.at[3 * a + j],
                                                 device_id=(px, py, c), device_id_type=MESH).wait_recv()
            for cp in sends():
                cp.wait_send()

        return start, finish

    return dict(operands=list(arrays), results=results, sems=(3 * n, 3 * n), make=make)


def _call(body, *, name, grid, in_specs, out_specs, out_shape, args, sem, scratch=(), duties=(), after=()):
    n_in, n_out, n_scr = len(in_specs), len(out_specs), len(scratch)
    sem_shapes = [pltpu.SemaphoreType.DMA((k,)) for d in duties for k in d["sems"]]

    def full(*refs):
        pos = [0]

        def take(k):
            pos[0] += k
            return refs[pos[0] - k:pos[0]]

        ins = take(n_in)
        d_ins = [take(len(d["operands"])) for d in duties]
        take(len(after))
        outs = take(n_out)
        d_outs = [take(len(d["results"])) for d in duties]
        scr = take(n_scr)
        d_sems = [take(len(d["sems"])) for d in duties]
        hooks = [d["make"](di, do, *ds) for d, di, do, ds in zip(duties, d_ins, d_outs, d_sems)]
        if grid and hooks:
            ids = [pl.program_id(k) for k in range(len(grid))]
            first = functools.reduce(jnp.logical_and, [i == 0 for i in ids])
            last = functools.reduce(jnp.logical_and, [i == g - 1 for i, g in zip(ids, grid)])

            @pl.when(first)
            def _():
                for start, _ in hooks:
                    start()

            body(*ins, *outs, *scr)

            @pl.when(last)
            def _():
                for _, finish in hooks:
                    finish()
        else:
            for start, _ in hooks:
                start()
            body(*ins, *outs, *scr)
            for _, finish in hooks:
                finish()

    d_args = [a for d in duties for a in d["operands"]]
    d_res = [r for d in duties for r in d["results"]]
    kwargs = dict(grid=grid) if grid else {}
    res = pl.pallas_call(
        full, name=name, in_specs=list(in_specs) + [ANY] * (len(d_args) + len(after)), out_specs=list(out_specs) + [ANY] * len(d_res),
        out_shape=list(out_shape) + d_res, scratch_shapes=list(scratch) + sem_shapes,
        compiler_params=_params(sem) if grid else None, **kwargs,
    )(*args, *d_args, *after)
    own, rest = list(res[:n_out]), list(res[n_out:])
    by_duty = []
    for d in duties:
        by_duty.append(rest[:len(d["results"])])
        rest = rest[len(d["results"]):]
    return own, by_duty


def _comm_only(duties, name, after=()):
    return _call(lambda: None, name=name, grid=None, in_specs=[], out_specs=[], out_shape=[], args=[], sem=None, duties=duties,
                 after=after)[1]


HBM = pl.BlockSpec(memory_space=pltpu.HBM)
SEMS = pl.BlockSpec(memory_space=pltpu.SEMAPHORE)
SIDE_EFFECT = pltpu.SideEffectType.DATAFLOW_SIDE_EFFECTING
N_OTHER_CHIPS = 3


def _chip_copies(src_ref, land_ref, sems):
    x, y, c = _place()
    chips = [(1 - x, y), (x, 1 - y), (1 - x, 1 - y)]
    return [pltpu.make_async_remote_copy(src_ref=src_ref.at[2 * px + py], dst_ref=land_ref.at[2 * x + y],
                                         send_sem=sems[j], recv_sem=sems[N_OTHER_CHIPS + j], device_id=(px, py, c), device_id_type=MESH)
            for j, (px, py) in enumerate(chips)]


def _exchange_start(pb, name):
    n_sem = 2 * N_OTHER_CHIPS

    def body(pb_ref, land_ref, *rest):
        for cp in _chip_copies(pb_ref, land_ref, rest[:n_sem]):
            cp.start()
        token = rest[n_sem + 2]
        token[...] = jnp.zeros_like(token)

    res = pl.pallas_call(
        body, name=name,
        out_shape=(pltpu.SemaphoreType.DMA(()),) * n_sem + (pltpu.HBM(pb.shape, pb.dtype), pltpu.HBM(pb.shape, pb.dtype),
                                                              jax.ShapeDtypeStruct((8, 128), F32)),
        in_specs=(HBM, HBM), out_specs=(SEMS,) * n_sem + (HBM, HBM, pl.BlockSpec(memory_space=pltpu.VMEM)),
        input_output_aliases={0: n_sem, 1: n_sem + 1},
        compiler_params=pltpu.CompilerParams(has_side_effects=SIDE_EFFECT),
    )(pltpu.with_memory_space_constraint(pb, pltpu.HBM), pltpu.with_memory_space_constraint(lax.empty(pb.shape, pb.dtype), pltpu.HBM))
    return dict(sems=res[:n_sem], src=res[n_sem], land=res[n_sem + 1], token=res[n_sem + 2])


def _exchange_wait(started, after, name):
    n_sem = 2 * N_OTHER_CHIPS

    def body(pb_ref, land_ref, *rest):
        for cp in _chip_copies(pb_ref, land_ref, rest[:n_sem]):
            cp.wait_send()
            cp.wait_recv()

    src, land = started["src"], started["land"]
    return pl.pallas_call(
        body, name=name, out_shape=(pltpu.HBM(src.shape, src.dtype), pltpu.HBM(land.shape, land.dtype)),
        in_specs=(HBM, HBM) + (SEMS,) * n_sem + (ANY,) * len(after), out_specs=(HBM, HBM), input_output_aliases={0: 0, 1: 1},
        compiler_params=pltpu.CompilerParams(has_side_effects=SIDE_EFFECT),
    )(src, land, *started["sems"], *after)[1]


def _ffn_fwd(x, gpre, wg, wu, wd, gpost, name, duties=()):
    tm = 256

    def body(x_ref, gpre_ref, wg_ref, wu_ref, wd_ref, gpost_ref, xo_ref, n_ref, a_ref, b_ref, hm_ref, h_ref):
        xv = x_ref[...]
        n = (xv * _rs(xv) * gpre_ref[...]).astype(BF16)
        a = _dot_nt(n, wg_ref[...])
        b = _dot_nt(n, wu_ref[...])
        hm = (a * _sigmoid(a) * b).astype(BF16)
        h = _dot(hm, wd_ref[...])
        xo_ref[...] = xv + 0.5 * (h * _rs(h) * gpost_ref[...])
        n_ref[...] = n
        a_ref[...] = a.astype(BF16)
        b_ref[...] = b.astype(BF16)
        hm_ref[...] = hm
        h_ref[...] = h

    return _call(
        body, name=name, grid=(T // tm,),
        in_specs=[_rows(tm, D), _const((1, D)), _resident((FFP, D)), _resident((FFP, D)), _resident((FFP, D)), _const((1, D))],
        out_specs=[_rows(tm, D), _rows(tm, D), _rows(tm, FFP), _rows(tm, FFP), _rows(tm, FFP), _rows(tm, D)],
        out_shape=[jax.ShapeDtypeStruct((T, D), F32), jax.ShapeDtypeStruct((T, D), BF16), jax.ShapeDtypeStruct((T, FFP), BF16),
                   jax.ShapeDtypeStruct((T, FFP), BF16), jax.ShapeDtypeStruct((T, FFP), BF16), jax.ShapeDtypeStruct((T, D), F32)],
        args=[x, gpre, wg, wu, wd, gpost], sem=("arbitrary",), duties=duties)


def _ffn_bwd(dxo, x, a, b, h, gpre, gpost, wg, wu, wd, name, after=()):
    tm = 256

    def body(dxo_ref, x_ref, a_ref, b_ref, h_ref, gpre_ref, gpost_ref, wg_ref, wu_ref, wd_ref,
             dx_ref, da_ref, db_ref, dh_ref, dgpre_ref, dgpost_ref):
        @pl.when(pl.program_id(0) == 0)
        def _():
            dgpre_ref[...] = jnp.zeros_like(dgpre_ref)
            dgpost_ref[...] = jnp.zeros_like(dgpost_ref)

        dy = dxo_ref[...]
        h = h_ref[...]
        hn = h * _rs(h)
        r2 = _rs(h)
        dgpost_ref[...] += jnp.sum(0.5 * dy * hn, axis=0, keepdims=True)
        gdy = 0.5 * dy * gpost_ref[...]
        dh = r2 * (gdy - hn * jnp.mean(gdy * hn, axis=-1, keepdims=True))
        dhb = dh.astype(BF16)
        dh_ref[...] = dhb
        dhm = _dot_nt(dhb, wd_ref[...])
        av = a_ref[...].astype(F32)
        bv = b_ref[...].astype(F32)
        sg = _sigmoid(av)
        db = (dhm * (av * sg)).astype(BF16)
        da = (dhm * bv * _dsilu(av, sg)).astype(BF16)
        da_ref[...] = da
        db_ref[...] = db
        dn = _dot(da, wg_ref[...]) + _dot(db, wu_ref[...])
        xv = x_ref[...]
        r = _rs(xv)
        xn = xv * r
        dgpre_ref[...] += jnp.sum(dn * xn, axis=0, keepdims=True)
        gdn = dn * gpre_ref[...]
        dx_ref[...] = dy + r * (gdn - xn * jnp.mean(gdn * xn, axis=-1, keepdims=True))

    return _call(
        body, name=name, grid=(T // tm,),
        in_specs=[_rows(tm, D), _rows(tm, D), _rows(tm, FFP), _rows(tm, FFP), _rows(tm, D), _const((1, D)), _const((1, D)),
                  _resident((FFP, D)), _resident((FFP, D)), _resident((FFP, D))],
        out_specs=[_rows(tm, D), _rows(tm, FFP), _rows(tm, FFP), _rows(tm, D), _const((1, D)), _const((1, D))],
        out_shape=[jax.ShapeDtypeStruct((T, D), F32), jax.ShapeDtypeStruct((T, FFP), BF16), jax.ShapeDtypeStruct((T, FFP), BF16),
                   jax.ShapeDtypeStruct((T, D), BF16), jax.ShapeDtypeStruct((1, D), F32), jax.ShapeDtypeStruct((1, D), F32)],
        args=[dxo, x, a, b, h, gpre, gpost, wg, wu, wd], sem=("arbitrary",), after=after)


def _core_index(core):
    return lax.axis_index("c") if core == "mesh" else core


def _by_core(put32, put16, step, core):
    mine = (step % 2) == _core_index(core)
    pl.when(mine)(put32)
    pl.when(jnp.logical_not(mine))(put16)


def _matmul_tn(a, b, name, core=None, after=(), duties=()):
    k, m = a.shape
    n = b.shape[1]
    assert m % FSH == 0
    a_spec = pl.BlockSpec((k, FSH), lambda i: (0, i))
    if core is None:
        def body(a_ref, b_ref, o_ref):
            o_ref[...] = _dot_tn(a_ref[...], b_ref[...])

        out_specs = [pl.BlockSpec((FSH, n), lambda i: (i, 0))]
        out_shape = [jax.ShapeDtypeStruct((m, n), F32)]
    else:
        assert m == N_DEV * FSH

        def body(a_ref, b_ref, o_ref, ob_ref):
            r = _dot_tn(a_ref[...], b_ref[...])

            def put32():
                o_ref[0] = r

            def put16():
                ob_ref[0] = r.astype(BF16)

            _by_core(put32, put16, pl.program_id(0), core)

        out_specs = [pl.BlockSpec((1, FSH, n), lambda i: (i // 2, 0, 0))] * 2
        out_shape = [jax.ShapeDtypeStruct((N_DEV // 2, FSH, n), F32), jax.ShapeDtypeStruct((N_DEV // 2, FSH, n), BF16)]
    return _call(body, name=name, grid=(m // FSH,), in_specs=[a_spec, _resident((k, n))], out_specs=out_specs, out_shape=out_shape,
                 args=[a, b], sem=("arbitrary",), duties=duties, after=after)


def _dwout(attn, yn, dh2, core):
    rs = (QW + SSM_W) // N_DEV
    half = N_DEV // 2

    def body(at_ref, yn_ref, dh_ref, o_ref, ob_ref):
        i = pl.program_id(0)

        def put(r):
            def put32():
                o_ref[0] = r

            def put16():
                ob_ref[0] = r.astype(BF16)

            _by_core(put32, put16, i, core)

        @pl.when(i < half)
        def _():
            put(_dot_tn(at_ref[...], dh_ref[...]))

        @pl.when(i >= half)
        def _():
            put(_dot_tn(yn_ref[...], dh_ref[...]))

    spec = pl.BlockSpec((1, rs, D), lambda i: (i // 2, 0, 0))
    return pl.pallas_call(
        body, name="dwout", grid=(N_DEV,),
        in_specs=[pl.BlockSpec((T, rs), lambda i: (0, jnp.minimum(i, half - 1))),
                  pl.BlockSpec((T, rs), lambda i: (0, jnp.maximum(i - half, 0))), _resident((T, D))],
        out_specs=[spec, spec],
        out_shape=[jax.ShapeDtypeStruct((half, rs, D), F32), jax.ShapeDtypeStruct((half, rs, D), BF16)],
        compiler_params=_params(("arbitrary",)),
    )(attn, yn, dh2)


def _rope_swap(t, lo_half):
    return jnp.where(lo_half, pltpu.roll(t, 96, 1), pltpu.roll(t, 32, 1))


def _inproj_fwd(x1, gpre, win, cos, sin_s, ex, duties=()):
    tm = 256

    def body(x_ref, g_ref, w_ref, cos_ref, sin_ref, ex_ref, n_ref, q_ref, kx_ref, vx_ref, xbc_ref, z_ref, dt_ref):
        xv = x_ref[...]
        n = (xv * _rs(xv) * g_ref[...]).astype(BF16)
        n_ref[...] = n
        proj = _dot_nt(n, w_ref[...])
        cs = cos_ref[...]
        sn = sin_ref[...]
        lo_half = (lax.broadcasted_iota(jnp.int32, (1, 128), 1) % HD) < (HD // 2)

        def rope(t):
            return t * cs + _rope_swap(t, lo_half) * sn

        for j in range(QW // 128):
            t = proj[:, 128 * j:128 * j + 128]
            q_ref[:, 128 * j:128 * j + 128] = (rope(t) * (HD ** -0.5)).astype(BF16)
        k = jnp.concatenate([rope(proj[:, QW + 128 * j:QW + 128 * j + 128]) for j in range(KVW // 128)], axis=1)
        v = proj[:, QW + KVW:QW + 2 * KVW]
        kx_ref[...] = _dot(k.astype(BF16), ex_ref[...]).astype(BF16)
        vx_ref[...] = _dot(v.astype(BF16), ex_ref[...]).astype(BF16)
        c0 = QW + 2 * KVW
        xbc_ref[...] = proj[:, c0:c0 + CONV_C]
        z_ref[...] = proj[:, c0 + CONV_C:c0 + CONV_C + SSM_W]
        dt_ref[...] = proj[:, c0 + CONV_C + SSM_W:IN_COLS]

    return _call(
        body, name="inproj_fwd", grid=(T // tm,),
        in_specs=[_rows(tm, D), _const((1, D)), _resident((INP, D)), _rows(tm, 128), _rows(tm, 128), _const((KVW, QW))],
        out_specs=[_rows(tm, D), _rows(tm, QW), _rows(tm, QW), _rows(tm, QW), _rows(tm, CONV_C), _rows(tm, SSM_W), _rows(tm, SSM_H)],
        out_shape=[jax.ShapeDtypeStruct((T, D), BF16), jax.ShapeDtypeStruct((T, QW), BF16), jax.ShapeDtypeStruct((T, QW), BF16),
                   jax.ShapeDtypeStruct((T, QW), BF16), jax.ShapeDtypeStruct((T, CONV_C), F32), jax.ShapeDtypeStruct((T, SSM_W), F32),
                   jax.ShapeDtypeStruct((T, SSM_H), F32)],
        args=[x1, gpre, win, cos, sin_s, ex], sem=("arbitrary",), duties=duties)


def _inproj_bwd(dres, dq, dkx, dvx, dxbc, dz, ddt, x1, gpre, win, cos, sin_s, exf):
    tm = 256

    def body(dres_ref, dq_ref, dkx_ref, dvx_ref, dxbc_ref, dz_ref, ddt_ref, x_ref, g_ref, w_ref, cos_ref, sin_ref, ex_ref,
             dx_ref, dp_ref, dg_ref):
        @pl.when(pl.program_id(0) == 0)
        def _():
            dg_ref[...] = jnp.zeros_like(dg_ref)

        cs = cos_ref[...]
        sn = sin_ref[...]
        lo_half = (lax.broadcasted_iota(jnp.int32, (1, 128), 1) % HD) < (HD // 2)

        def rope_t(t):
            return t * cs - _rope_swap(t, lo_half) * sn

        for j in range(QW // 128):
            dp_ref[:, 128 * j:128 * j + 128] = rope_t(dq_ref[:, 128 * j:128 * j + 128] * (HD ** -0.5)).astype(BF16)
        dk = _dot_nt_hi(dkx_ref[...], ex_ref[...])
        dv = _dot_nt_hi(dvx_ref[...], ex_ref[...])
        for j in range(KVW // 128):
            dp_ref[:, QW + 128 * j:QW + 128 * j + 128] = rope_t(dk[:, 128 * j:128 * j + 128]).astype(BF16)
        dp_ref[:, QW + KVW:QW + 2 * KVW] = dv.astype(BF16)
        c0 = QW + 2 * KVW
        dp_ref[:, c0:c0 + CONV_C] = dxbc_ref[...].astype(BF16)
        dp_ref[:, c0 + CONV_C:c0 + CONV_C + SSM_W] = dz_ref[...].astype(BF16)
        dp_ref[:, c0 + CONV_C + SSM_W:INP] = ddt_ref[...].astype(BF16)
        dn = _dot(dp_ref[...], w_ref[...])
        xv = x_ref[...]
        r = _rs(xv)
        xn = xv * r
        dg_ref[...] += jnp.sum(dn * xn, axis=0, keepdims=True)
        gdn = dn * g_ref[...]
        dx_ref[...] = dres_ref[...] + r * (gdn - xn * jnp.mean(gdn * xn, axis=-1, keepdims=True))

    return pl.pallas_call(
        body, name="inproj_bwd", grid=(T // tm,),
        in_specs=[_rows(tm, D), _rows(tm, QW), _rows(tm, QW), _rows(tm, QW), _rows(tm, CONV_C), _rows(tm, SSM_W), _rows(tm, 128),
                  _rows(tm, D), _const((1, D)), _resident((INP, D)), _rows(tm, 128), _rows(tm, 128), _const((KVW, QW))],
        out_specs=[_rows(tm, D), _rows(tm, INP), _const((1, D))],
        out_shape=[jax.ShapeDtypeStruct((T, D), F32), jax.ShapeDtypeStruct((T, INP), BF16), jax.ShapeDtypeStruct((1, D), F32)],
        compiler_params=_params(("arbitrary",)),
    )(dres, dq, dkx, dvx, dxbc, dz, ddt, x1, gpre, win, cos, sin_s, exf)


def _outproj_fwd(x1, attn, yn, wout, gpost):
    tm = 256

    def body(x_ref, at_ref, yn_ref, w_ref, g_ref, xo_ref, h_ref):
        h = _dot(at_ref[...], w_ref[0:QW, :]) + _dot(yn_ref[...], w_ref[QW:QW + SSM_W, :])
        h_ref[...] = h
        xo_ref[...] = x_ref[...] + h * _rs(h) * g_ref[...]

    return pl.pallas_call(
        body, name="outproj_fwd", grid=(T // tm,),
        in_specs=[_rows(tm, D), _rows(tm, QW), _rows(tm, SSM_W), _resident((QW + SSM_W, D)), _const((1, D))],
        out_specs=[_rows(tm, D), _rows(tm, D)],
        out_shape=[jax.ShapeDtypeStruct((T, D), F32), jax.ShapeDtypeStruct((T, D), F32)],
        compiler_params=_params(("parallel",)),
    )(x1, attn, yn, wout, gpost)


def _outproj_bwd(dx2, h2, gpost, wout, duties=()):
    tm = 256

    def body(dy_ref, h_ref, g_ref, w_ref, dh_ref, dm_ref, dg_ref):
        @pl.when(pl.program_id(0) == 0)
        def _():
            dg_ref[...] = jnp.zeros_like(dg_ref)

        dy = dy_ref[...]
        h = h_ref[...]
        r = _rs(h)
        hn = h * r
        dg_ref[...] += jnp.sum(dy * hn, axis=0, keepdims=True)
        gdy = dy * g_ref[...]
        dh = (r * (gdy - hn * jnp.mean(gdy * hn, axis=-1, keepdims=True))).astype(BF16)
        dh_ref[...] = dh
        dm_ref[...] = _dot_nt(dh, w_ref[...])

    return _call(
        body, name="outproj_bwd", grid=(T // tm,),
        in_specs=[_rows(tm, D), _rows(tm, D), _const((1, D)), _resident((QW + SSM_W, D))],
        out_specs=[_rows(tm, D), _rows(tm, QW + SSM_W), _const((1, D))],
        out_shape=[jax.ShapeDtypeStruct((T, D), BF16), jax.ShapeDtypeStruct((T, QW + SSM_W), F32), jax.ShapeDtypeStruct((1, D), F32)],
        args=[dx2, h2, gpost, wout], sem=("arbitrary",), duties=duties)


def _attn_bias():
    d = jnp.arange(AB)[:, None] - jnp.arange(T)[None, :] + (T - AB)
    cnt = jnp.zeros(d.shape, F32)
    for window, dil in DILATIONS:
        cnt = cnt + ((d >= 0) & (d % dil == 0) & (d <= window)).astype(F32)
    return jnp.where(cnt > 0, jnp.log(jnp.maximum(cnt, 1.0)), NEG)


G_PER = NQ // NKV
WK = G_PER * HD


def _attn_fwd(q, kx, vx, bias, duties=()):
    def body(q_ref, kx_ref, vx_ref, bias_ref, o_ref, lse_ref):
        lane = lax.broadcasted_iota(jnp.int32, (1, WK), 1)
        lse_ref[...] = jnp.zeros_like(lse_ref)
        for i in range(NAB):
            n = (i + 1) * AB
            rows = slice(i * AB, n)
            qi = q_ref[rows, :]
            kxi = kx_ref[0:n, :]
            vxi = vx_ref[0:n, :]
            bb = bias_ref[:, (NAB - 1 - i) * AB:]
            o_acc = jnp.zeros((AB, WK), F32)
            for g in range(G_PER):
                mg = (lane // HD) == g
                s = _dot_nt(jnp.where(mg, qi, jnp.zeros_like(qi)), kxi) + bb
                m = jnp.max(s, axis=1, keepdims=True)
                p = jnp.exp(s - m)
                l = jnp.sum(p, axis=1, keepdims=True)
                o_acc = jnp.where(mg, _dot(p.astype(BF16), vxi) / l, o_acc)
                lse_ref[rows, g:g + 1] = m + jnp.log(l)
            o_ref[rows, :] = o_acc.astype(BF16)

    col = lambda kv: (0, kv)
    return _call(
        body, name="attn_fwd", grid=(NKV,),
        in_specs=[pl.BlockSpec((T, WK), col), pl.BlockSpec((T, WK), col), pl.BlockSpec((T, WK), col), _const((AB, T))],
        out_specs=[pl.BlockSpec((T, WK), col), pl.BlockSpec((T, 128), col)],
        out_shape=[jax.ShapeDtypeStruct((T, QW), BF16), jax.ShapeDtypeStruct((T, NKV * 128), F32)],
        args=[q, kx, vx, bias], sem=("arbitrary",), duties=duties)


def _attn_bwd(q, kx, vx, o, dmix, lse, bias, duties=()):
    def body(q_ref, kx_ref, vx_ref, o_ref, do_ref, lse_ref, bias_ref, dq_ref, dkx_ref, dvx_ref):
        lane = lax.broadcasted_iota(jnp.int32, (1, WK), 1)
        dkx_ref[...] = jnp.zeros_like(dkx_ref)
        dvx_ref[...] = jnp.zeros_like(dvx_ref)
        for i in range(NAB):
            n = (i + 1) * AB
            rows = slice(i * AB, n)
            qi = q_ref[rows, :]
            dof = do_ref[rows, :]
            doi = dof.astype(BF16)
            prod = dof * o_ref[rows, :].astype(F32)
            kxi = kx_ref[0:n, :]
            vxi = vx_ref[0:n, :]
            bb = bias_ref[:, (NAB - 1 - i) * AB:]
            dq_acc = jnp.zeros((AB, WK), F32)
            for g in range(G_PER):
                mg = (lane // HD) == g
                qm = jnp.where(mg, qi, jnp.zeros_like(qi))
                dom = jnp.where(mg, doi, jnp.zeros_like(doi))
                delta = jnp.sum(jnp.where(mg, prod, 0.0), axis=1, keepdims=True)
                p = jnp.exp(_dot_nt(qm, kxi) + bb - lse_ref[rows, g:g + 1])
                ds = (p * (_dot_nt(dom, vxi) - delta)).astype(BF16)
                dvx_ref[0:n, :] += _dot_tn(p.astype(BF16), dom)
                dkx_ref[0:n, :] += _dot_tn(ds, qm)
                dq_acc = jnp.where(mg, _dot(ds, kxi), dq_acc)
            dq_ref[rows, :] = dq_acc

    col = lambda kv: (0, kv)
    return _call(
        body, name="attn_bwd", grid=(NKV,),
        in_specs=[pl.BlockSpec((T, WK), col), pl.BlockSpec((T, WK), col), pl.BlockSpec((T, WK), col), pl.BlockSpec((T, WK), col),
                  pl.BlockSpec((T, WK), col), pl.BlockSpec((T, 128), col), _const((AB, T))],
        out_specs=[pl.BlockSpec((T, WK), col), pl.BlockSpec((T, WK), col), pl.BlockSpec((T, WK), col)],
        out_shape=[jax.ShapeDtypeStruct((T, QW), F32)] * 3,
        args=[q, kx, vx, o, dmix, lse, bias], sem=("arbitrary",), duties=duties)


def _softplus(x):
    return jnp.maximum(x, 0.0) + jnp.log1p(jnp.exp(-jnp.abs(x)))


def _conv_taps(u, halo):
    zext = jnp.concatenate([halo, u], axis=0)
    return [pltpu.roll(zext, m, 0)[8:] for m in (1, 2, 3)]


def _ssd_chunk_common(u, halo, dtr, cw_ref, cb_ref, dtb_ref, alx_ref, e_ref):
    sh1, sh2, sh3 = _conv_taps(u, halo)
    xc = cb_ref[...] + cw_ref[3:4, :] * u + cw_ref[2:3, :] * sh1 + cw_ref[1:2, :] * sh2 + cw_ref[0:1, :] * sh3
    sg = _sigmoid(xc)
    act = xc * sg
    pre_x = _dot_hi(dtr + dtb_ref[...], e_ref[...])
    dt_x = _softplus(pre_x)
    a_x = -jnp.exp(alx_ref[...])
    ri = lax.broadcasted_iota(jnp.int32, (L, L), 0)
    ci = lax.broadcasted_iota(jnp.int32, (L, L), 1)
    tri = ri >= ci
    acs_x = _dot_hi(tri, dt_x * a_x, a_is_01=True)
    return dict(sh=(sh1, sh2, sh3), xc=xc, sg=sg, act=act, pre_x=pre_x, dt_x=dt_x, a_x=a_x, tri=tri, acs_x=acs_x)


def _decay(acs_x, acs_t, h, tri):
    col = acs_x[:, HD * h:HD * h + 1]
    row = acs_t[HD * h:HD * h + 1, :]
    return jnp.exp(jnp.where(tri, col - row, NEG))


def _ssd_fwd(xbc, z, dtr, convw, convb, dtb, alx, dskx, ssmn, e, duties=()):
    def body(u_ref, z_ref, dtr_ref, cw_ref, cb_ref, dtb_ref, alx_ref, dsk_ref, sn_ref, e_ref,
             yn_ref, y_ref, hs_ref, halo, hst):
        @pl.when(pl.program_id(0) == 0)
        def _():
            halo[...] = jnp.zeros_like(halo)
            hst[...] = jnp.zeros_like(hst)

        u = u_ref[...]
        cm = _ssd_chunk_common(u, halo[...], dtr_ref[...], cw_ref, cb_ref, dtb_ref, alx_ref, e_ref)
        halo[...] = u[L - 8:, :]
        act, dt_x, acs_x, tri = cm["act"], cm["dt_x"], cm["acs_x"], cm["tri"]
        xs = act[:, :SSM_W]
        acs_l = acs_x[L - 1:L, :]
        lam_x = jnp.exp(acs_x)
        w_x = jnp.exp(acs_l - acs_x)
        gam_x = jnp.exp(acs_l)
        acs_t = acs_x.T
        xd = xs * dt_x
        xb = xd.astype(BF16)
        xw = (xd * w_x).astype(BF16)
        lo = lax.broadcasted_iota(jnp.int32, (1, 128), 1) < HD
        hs_ref[0] = hst[...]
        pieces = []
        for grp in range(2):
            bb = act[:, SSM_W + SSM_N * grp:SSM_W + SSM_N * (grp + 1)].astype(BF16)
            cb_ = act[:, SSM_W + 2 * SSM_N + SSM_N * grp:SSM_W + 2 * SSM_N + SSM_N * (grp + 1)].astype(BF16)
            cbm = _dot_nt(cb_, bb)
            for jj in range(4):
                j = 4 * grp + jj
                sl = slice(128 * j, 128 * j + 128)
                m0 = (cbm * _decay(acs_x, acs_t, 2 * j, tri)).astype(BF16)
                m1 = (cbm * _decay(acs_x, acs_t, 2 * j + 1, tri)).astype(BF16)
                x2 = xb[:, sl]
                ydiag = jnp.where(lo, _dot(m0, x2), _dot(m1, x2))
                hprev = hst[j]
                yoff = lam_x[:, sl] * _dot(cb_, hprev.astype(BF16))
                pieces.append(ydiag + yoff)
                hst[j] = gam_x[:, sl] * hprev + _dot_tn(bb, xw[:, sl])
        y = jnp.concatenate(pieces, axis=1) + dsk_ref[...] * xs
        y_ref[...] = y
        zv = z_ref[...]
        yz = y * (zv * _sigmoid(zv))
        half = SSM_W // 2
        yn = jnp.concatenate([yz[:, :half] * _rs(yz[:, :half]), yz[:, half:] * _rs(yz[:, half:])], axis=1)
        yn_ref[...] = (yn * sn_ref[...]).astype(BF16)

    return _call(
        body, name="ssd_fwd", grid=(NCH,),
        in_specs=[_rows(L, CONV_C), _rows(L, SSM_W), _rows(L, SSM_H), _const((4, CONV_C)), _const((1, CONV_C)), _const((1, SSM_H)),
                  _const((1, SSM_W)), _const((1, SSM_W)), _const((1, SSM_W)), _const((SSM_H, SSM_W))],
        out_specs=[_rows(L, SSM_W), _rows(L, SSM_W), pl.BlockSpec((1, 8, SSM_N, 128), lambda c: (c, 0, 0, 0))],
        out_shape=[jax.ShapeDtypeStruct((T, SSM_W), BF16), jax.ShapeDtypeStruct((T, SSM_W), F32),
                   jax.ShapeDtypeStruct((NCH, 8, SSM_N, 128), F32)],
        scratch=[pltpu.VMEM((8, CONV_C), F32), pltpu.VMEM((8, SSM_N, 128), F32)],
        args=[xbc, z, dtr, convw, convb, dtb, alx, dskx, ssmn, e], sem=("arbitrary",), duties=duties)


def _ssd_bwd(dmix, xbc, z, dtr, y, hs, convw, convb, dtb, alx, dskx, ssmn, e, e1, duties=()):
    rev = lambda i: (NCH - 1 - i, 0)

    def body(dyn_ref, u_ref, uh_ref, z_ref, dtr_ref, y_ref, hs_ref, cw_ref, cb_ref, dtb_ref, alx_ref, dsk_ref, sn_ref, e_ref, e1_ref,
             dxbc_ref, dz_ref, ddt_ref, dcw_ref, dcb_ref, dsn_ref, dpar_ref, dh, duh, colbuf):
        step = pl.program_id(0)
        c = NCH - 1 - step

        @pl.when(step == 0)
        def _():
            for r in (dh, duh, dcw_ref, dcb_ref, dsn_ref, dpar_ref):
                r[...] = jnp.zeros_like(r)

        u = u_ref[...]
        halo = jnp.where(c > 0, uh_ref[...], 0.0)
        cm = _ssd_chunk_common(u, halo, dtr_ref[...], cw_ref, cb_ref, dtb_ref, alx_ref, e_ref)
        sh1, sh2, sh3 = cm["sh"]
        xc, sg, act, pre_x, dt_x, a_x, tri, acs_x = (cm[k] for k in ("xc", "sg", "act", "pre_x", "dt_x", "a_x", "tri", "acs_x"))
        xs = act[:, :SSM_W]
        acs_l = acs_x[L - 1:L, :]
        lam_x = jnp.exp(acs_x)
        w_x = jnp.exp(acs_l - acs_x)
        gam_x = jnp.exp(acs_l)
        acs_t = acs_x.T
        xd = xs * dt_x
        xb = xd.astype(BF16)
        xdw = xd * w_x
        xw = xdw.astype(BF16)
        lo = lax.broadcasted_iota(jnp.int32, (1, 128), 1) < HD
        row8 = lax.broadcasted_iota(jnp.int32, (8, 1), 0)

        dyn = dyn_ref[...]
        yv = y_ref[...]
        zv = z_ref[...]
        sz = _sigmoid(zv)
        siluz = zv * sz
        yz = yv * siluz
        half = SSM_W // 2
        gy = dyn * sn_ref[...]
        dyz_parts, yzn_parts = [], []
        for hf in range(2):
            part = yz[:, hf * half:(hf + 1) * half]
            r = _rs(part)
            pn = part * r
            gp = gy[:, hf * half:(hf + 1) * half]
            dyz_parts.append(r * (gp - pn * jnp.mean(gp * pn, axis=-1, keepdims=True)))
            yzn_parts.append(pn)
        dyz = jnp.concatenate(dyz_parts, axis=1)
        dsn_ref[...] += jnp.sum(dyn * jnp.concatenate(yzn_parts, axis=1), axis=0, keepdims=True)
        dy = dyz * siluz
        dz_ref[...] = dyz * yv * _dsilu(zv, sz)

        colbuf[...] = jnp.zeros_like(colbuf)
        dx_pieces, dacs_pieces, dacsl_pieces, db_pieces, dc_pieces = [], [], [], [], []
        for grp in range(2):
            bb = act[:, SSM_W + SSM_N * grp:SSM_W + SSM_N * (grp + 1)].astype(BF16)
            cb_ = act[:, SSM_W + 2 * SSM_N + SSM_N * grp:SSM_W + 2 * SSM_N + SSM_N * (grp + 1)].astype(BF16)
            cbm = _dot_nt(cb_, bb)
            dcbm = jnp.zeros((L, L), F32)
            dc_g = jnp.zeros((L, SSM_N), F32)
            db_g = jnp.zeros((L, SSM_N), F32)
            for jj in range(4):
                j = 4 * grp + jj
                sl = slice(128 * j, 128 * j + 128)
                dy2 = dy[:, sl]
                dy2b = dy2.astype(BF16)
                d0 = _decay(acs_x, acs_t, 2 * j, tri)
                d1 = _decay(acs_x, acs_t, 2 * j + 1, tri)
                m0 = cbm * d0
                m1 = cbm * d1
                x2 = xb[:, sl]
                hprev = hs_ref[0, j]
                hprevb = hprev.astype(BF16)
                dhn = dh[j]
                dhnb = dhn.astype(BF16)
                g2 = _dot(bb, dhnb)
                dx_pieces.append(jnp.where(lo, _dot_tn(m0.astype(BF16), dy2b), _dot_tn(m1.astype(BF16), dy2b)) + w_x[:, sl] * g2)
                zero = jnp.zeros_like(dy2b)
                dm0 = _dot_nt(jnp.where(lo, dy2b, zero), x2)
                dm1 = _dot_nt(jnp.where(lo, zero, dy2b), x2)
                dcbm = dcbm + dm0 * d0 + dm1 * d1
                e0 = dm0 * m0
                e1v = dm1 * m1
                colbuf[:, 2 * j:2 * j + 1] = jnp.sum(e0, axis=1, keepdims=True) - jnp.sum(e0.T, axis=1, keepdims=True)
                colbuf[:, 2 * j + 1:2 * j + 2] = jnp.sum(e1v, axis=1, keepdims=True) - jnp.sum(e1v.T, axis=1, keepdims=True)
                yoff = lam_x[:, sl] * _dot(cb_, hprevb)
                gxw = g2 * xdw[:, sl]
                dacs_pieces.append(dy2 * yoff - gxw)
                dacsl_pieces.append(jnp.sum(gxw, axis=0, keepdims=True) + gam_x[:, sl] * jnp.sum(dhn * hprev, axis=0, keepdims=True))
                dyl = (dy2 * lam_x[:, sl]).astype(BF16)
                dc_g = dc_g + _dot_nt(dyl, hprevb)
                db_g = db_g + _dot_nt(xw[:, sl], dhnb)
                dh[j] = gam_x[:, sl] * dhn + _dot_tn(cb_, dyl)
            dcbb = dcbm.astype(BF16)
            dc_pieces.append(dc_g + _dot(dcbb, bb))
            db_pieces.append(db_g + _dot_tn(dcbb, cb_))

        dxd = jnp.concatenate(dx_pieces, axis=1)
        rowi = lax.broadcasted_iota(jnp.int32, (L, 1), 0)
        dacs_x = (jnp.concatenate(dacs_pieces, axis=1) + _dot_hi(colbuf[...], e1_ref[...])
                  + jnp.where(rowi == L - 1, jnp.concatenate(dacsl_pieces, axis=1), 0.0))
        upper = lax.broadcasted_iota(jnp.int32, (L, L), 0) <= lax.broadcasted_iota(jnp.int32, (L, L), 1)
        dadt_x = _dot_hi(upper, dacs_x, a_is_01=True)
        ddt_x = dxd * xs + dadt_x * a_x
        ddtr = _dot_nt_hi(ddt_x * _sigmoid(pre_x), e_ref[...])
        ddt_ref[...] = jnp.zeros_like(ddt_ref)
        ddt_ref[:, 0:SSM_H] = ddtr
        dalx =jnp.sum(dadt_x * dt_x, axis=0, keepdims=True) * a_x
        ddskx = jnp.sum(dy * xs, axis=0, keepdims=True)
        par_x = jnp.where(row8 == 1, dalx, 0.0) + jnp.where(row8 == 2, ddskx, 0.0)
        dpar_ref[...] += _dot_nt_hi(par_x, e_ref[...]) + jnp.where(row8 == 0, jnp.sum(ddtr, axis=0, keepdims=True), 0.0)

        dxs = dxd * dt_x + dsk_ref[...] * dy
        dact = jnp.concatenate([dxs] + db_pieces + dc_pieces, axis=1)
        du = dact * _dsilu(xc, sg)
        dcb_ref[...] += jnp.sum(du, axis=0, keepdims=True)
        taps = (sh3, sh2, sh1, u)
        dcw = jnp.zeros((8, CONV_C), F32)
        for k in range(4):
            dcw = dcw + jnp.where(row8 == k, jnp.sum(du * taps[k], axis=0, keepdims=True), 0.0)
        dcw_ref[...] += dcw
        zext = jnp.concatenate([du, duh[...]], axis=0)
        f1, f2, f3 = (pltpu.roll(zext, L + 8 - m, 0)[:L] for m in (1, 2, 3))
        dxbc_ref[...] = cw_ref[3:4, :] * du + cw_ref[2:3, :] * f1 + cw_ref[1:2, :] * f2 + cw_ref[0:1, :] * f3
        duh[...] = du[:8, :]

    return _call(
        body, name="ssd_bwd", grid=(NCH,),
        in_specs=[pl.BlockSpec((L, SSM_W), lambda i: (NCH - 1 - i, 1)), pl.BlockSpec((L, CONV_C), rev),
                  pl.BlockSpec((8, CONV_C), lambda i: (jnp.maximum((NCH - 1 - i) * (L // 8) - 1, 0), 0)),
                  pl.BlockSpec((L, SSM_W), rev), pl.BlockSpec((L, SSM_H), rev), pl.BlockSpec((L, SSM_W), rev),
                  pl.BlockSpec((1, 8, SSM_N, 128), lambda i: (NCH - 1 - i, 0, 0, 0)),
                  _const((4, CONV_C)), _const((1, CONV_C)), _const((1, SSM_H)), _const((1, SSM_W)), _const((1, SSM_W)), _const((1, SSM_W)),
                  _const((SSM_H, SSM_W)), _const((128, SSM_W))],
        out_specs=[pl.BlockSpec((L, CONV_C), rev), pl.BlockSpec((L, SSM_W), rev), pl.BlockSpec((L, 128), rev),
                   _const((8, CONV_C)), _const((1, CONV_C)), _const((1, SSM_W)), _const((8, SSM_H))],
        out_shape=[jax.ShapeDtypeStruct((T, CONV_C), F32), jax.ShapeDtypeStruct((T, SSM_W), F32), jax.ShapeDtypeStruct((T, 128), F32),
                   jax.ShapeDtypeStruct((8, CONV_C), F32), jax.ShapeDtypeStruct((1, CONV_C), F32), jax.ShapeDtypeStruct((1, SSM_W), F32),
                   jax.ShapeDtypeStruct((8, SSM_H), F32)],
        scratch=[pltpu.VMEM((8, SSM_N, 128), F32), pltpu.VMEM((8, CONV_C), F32), pltpu.VMEM((L, 128), F32)],
        args=[dmix, xbc, xbc, z, dtr, y, hs, convw, convb, dtb, alx, dskx, ssmn, e, e1], sem=("arbitrary",), duties=duties)


def _loss_head(x3, target):
    tm = 512

    def body(x_ref, t_ref, dy_ref, ss_ref):
        @pl.when(pl.program_id(0) == 0)
        def _():
            ss_ref[...] = jnp.zeros_like(ss_ref)

        err = x_ref[...] - t_ref[...]
        dy_ref[...] = err * (1.0 / D)
        ss_ref[...] += jnp.sum(jnp.sum(err * err, axis=1, keepdims=True), axis=0, keepdims=True)

    return pl.pallas_call(
        body, name="loss_head", grid=(T // tm,),
        in_specs=[_rows(tm, D), _rows(tm, D)],
        out_specs=[_rows(tm, D), _const((1, 128))],
        out_shape=[jax.ShapeDtypeStruct((T, D), F32), jax.ShapeDtypeStruct((1, 128), F32)],
        compiler_params=_params(("arbitrary",)),
    )(x3, target)


def _adam_math(w, g, m, v):
    m = ADAM_B1 * m + (1.0 - ADAM_B1) * g
    v = ADAM_B2 * v + (1.0 - ADAM_B2) * (g * g)
    m_hat = m / (1.0 - ADAM_B1 ** ADAM_STEP)
    v_hat = v / (1.0 - ADAM_B2 ** ADAM_STEP)
    delta = -ADAM_LR * (m_hat / (jnp.sqrt(v_hat) + ADAM_EPS) + ADAM_WD * w)
    return delta, m, v


def _adamw(w, m, v, parts, name, after=()):
    rows, cols = w.shape
    tr = rows if rows <= 512 else 256
    assert rows % tr == 0
    n_parts = len(parts)

    def body(*refs):
        w_ref, m_ref, v_ref = refs[:3]
        p_refs = refs[3:3 + n_parts]
        g_ref, d_ref, nm_ref, nv_ref = refs[3 + n_parts + len(after):]
        g = p_refs[0][...].astype(F32)
        for p in p_refs[1:]:
            g = g + p[...].astype(F32)
        delta, nm, nv = _adam_math(w_ref[...], g, m_ref[...], v_ref[...])
        g_ref[...] = g
        d_ref[...] = delta
        nm_ref[...] = nm
        nv_ref[...] = nv

    spec = pl.BlockSpec((tr, cols), lambda i: (i, 0))
    return pl.pallas_call(
        body, name=name, grid=(rows // tr,),
        in_specs=[spec] * (3 + n_parts) + [ANY] * len(after), out_specs=[spec] * 4,
        out_shape=[jax.ShapeDtypeStruct((rows, cols), F32)] * 4,
        compiler_params=_params(("parallel",)),
    )(w, m, v, *parts, *after)


COL_TILE = 512


def _adamw_sharded(w, m, v, chip_sum, from_chips, other_chips, name, after=()):
    rows, cols = w.shape
    prow = chip_sum.shape[0]
    assert cols % COL_TILE == 0 and prow >= rows and chip_sum.shape[1] == cols

    def body(ids_ref, w_ref, m_ref, v_ref, s_ref, r1_ref, r2_ref, r3_ref, *rest):
        g_ref, d_ref, nm_ref, nv_ref = rest[len(after):]
        g = s_ref[...]
        for r in (r1_ref, r2_ref, r3_ref):
            g = g + r[0].astype(F32)
        g = g[:rows]
        delta, nm, nv = _adam_math(w_ref[...], g, m_ref[...], v_ref[...])
        g_ref[...] = g
        d_ref[...] = delta
        nm_ref[...] = nm
        nv_ref[...] = nv

    spec = pl.BlockSpec((rows, COL_TILE), lambda i, ids: (0, i))
    part = lambda k: pl.BlockSpec((1, prow, COL_TILE), lambda i, ids: (ids[k], 0, i))
    return pl.pallas_call(
        body, name=name,
        grid_spec=pltpu.PrefetchScalarGridSpec(
            num_scalar_prefetch=1, grid=(cols // COL_TILE,),
            in_specs=[spec, spec, spec, pl.BlockSpec((prow, COL_TILE), lambda i, ids: (0, i)), part(0), part(1), part(2)]
            + [ANY] * len(after), out_specs=[spec] * 4),
        out_shape=[jax.ShapeDtypeStruct((rows, cols), F32)] * 4,
        compiler_params=_params(("parallel",)),
    )(other_chips, w, m, v, chip_sum, from_chips, from_chips, from_chips, *after)


def _chip_sum(mine, recv, name):
    rows, cols = mine.shape[1:]

    def body(a_ref, b_ref, s_ref, sb_ref):
        s = a_ref[0] + b_ref[0].astype(F32)
        sb_ref[0] = s.astype(BF16)

        @pl.when(pl.program_id(0) == 2 * lax.axis_index("x") + lax.axis_index("y"))
        def _():
            s_ref[...] = s

    by_chip = pl.BlockSpec((1, rows, cols), lambda k: (k, 0, 0))
    return pl.pallas_call(
        body, name=name, grid=(N_DEV // 2,),
        in_specs=[by_chip, by_chip], out_specs=[_const((rows, cols)), by_chip],
        out_shape=[jax.ShapeDtypeStruct((rows, cols), F32), jax.ShapeDtypeStruct((N_DEV // 2, rows, cols), BF16)],
        compiler_params=_params(("arbitrary",)),
    )(mine, recv)


def _all_reduce_small(v, after=()):
    rows = v.shape[0]

    def body(v_ref, *rest):
        out_ref, gath, send_sems, recv_sems = rest[len(after):]
        x, y, c = _place()
        me, sibling = (x, y, c), (x, y, 1 - c)
        chips = [(1 - x, y), (x, 1 - y), (1 - x, 1 - y)]

        def blk(px, py, pc):
            return gath.at[pl.ds((4 * px + 2 * py + pc) * rows, rows), :]

        def copy(k, block, to, src=None):
            return pltpu.make_async_remote_copy(src_ref=blk(*block) if src is None else src, dst_ref=blk(*block),
                                                send_sem=send_sems.at[k], recv_sem=recv_sems.at[k], device_id=to, device_id_type=MESH)

        gath[pl.ds((4 * x + 2 * y + c) * rows, rows), :] = v_ref[...]
        first = [copy(0, me, sibling, src=v_ref)] + [copy(1 + j, me, (*chip, c), src=v_ref) for j, chip in enumerate(chips)]
        for cp in first:
            cp.start()
        passed = [copy(4 + j, (*chip, c), sibling) for j, chip in enumerate(chips)]
        for j, chip in enumerate(chips):
            copy(1 + j, (*chip, c), me).wait_recv()
            passed[j].start()
        copy(0, sibling, me).wait_recv()
        for j, chip in enumerate(chips):
            copy(4 + j, (*chip, 1 - c), me).wait_recv()
        for cp in first + passed:
            cp.wait_send()
        acc = gath[0:rows, :]
        for d in range(1, N_DEV):
            acc = acc + gath[d * rows:(d + 1) * rows, :]
        out_ref[...] = acc

    vm = pl.BlockSpec(memory_space=pltpu.VMEM)
    return pl.pallas_call(
        body, name="all_reduce_small",
        in_specs=[vm] + [ANY] * len(after), out_specs=vm,
        out_shape=jax.ShapeDtypeStruct(v.shape, F32),
        scratch_shapes=[pltpu.VMEM((N_DEV * rows, 128), F32), pltpu.SemaphoreType.DMA((7,)), pltpu.SemaphoreType.DMA((7,))],
    )(v, *after)


def _rope_tables(positions):
    inv_freq = ROPE_THETA ** (-jnp.arange(0, HD, 2, dtype=F32) / HD)
    ang = positions.reshape(T).astype(F32)[:, None] * inv_freq
    ang = jnp.concatenate([ang, ang, ang, ang], axis=-1)
    lo_half = (jnp.arange(128) % HD) < (HD // 2)
    return jnp.cos(ang), jnp.where(lo_half, -jnp.sin(ang), jnp.sin(ang))


def _selectors():
    lane = jnp.arange(QW)
    e = (lane[None, :] // HD == jnp.arange(SSM_H)[:, None]).astype(F32)
    e1 = ((lane[None, :] == HD * jnp.arange(128)[:, None]) & (jnp.arange(128)[:, None] < SSM_H)).astype(F32)
    src = jnp.arange(KVW)
    ex = ((lane[None, :] // (HD * (NQ // NKV)) == src[:, None] // HD) & (lane[None, :] % HD == src[:, None] % HD)).astype(F32)
    return e, e1, ex


WEIGHTS = ['ffn1_pre_norm', 'ffn1_w_gate', 'ffn1_w_up', 'ffn1_w_down', 'ffn1_post_norm', 'mix_pre_norm', 'w_in', 'conv_w', 'conv_b',
           'dt_bias', 'a_log', 'd_skip', 'ssm_norm', 'w_out', 'mix_post_norm', 'ffn2_pre_norm', 'ffn2_w_gate', 'ffn2_w_up',
           'ffn2_w_down', 'ffn2_post_norm']
COL_SHARDED = ['ffn1_w_gate', 'ffn1_w_up', 'ffn2_w_gate', 'ffn2_w_up', 'w_in']
ROW_SHARDED = ['ffn1_w_down', 'ffn2_w_down', 'w_out']
BIG = COL_SHARDED + ROW_SHARDED
FFN_BIG = COL_SHARDED[:4] + ROW_SHARDED[:2]
SMALL = ['ffn1_pre_norm', 'ffn1_post_norm', 'mix_pre_norm', 'conv_b', 'dt_bias', 'a_log', 'd_skip', 'ssm_norm', 'mix_post_norm',
         'ffn2_pre_norm', 'ffn2_post_norm']
FFN1 = ['ffn1_w_gate', 'ffn1_w_up', 'ffn1_w_down']
FFN2 = ['ffn2_w_gate', 'ffn2_w_up', 'ffn2_w_down']


def _wire_block(name, a):
    if name in FFN_BIG:
        return jnp.pad(a.astype(BF16), ((0, FSH - FSR), (0, 0)))
    return a if name == "conv_w" else a.astype(BF16)


def _whole_from_gathered(name, a):
    if name == "conv_w":
        return jnp.transpose(a, (1, 0, 2)).reshape(a.shape[1], -1)
    a = a.reshape(-1, D)
    return jnp.concatenate([a, jnp.zeros((INP - IN_COLS, D), a.dtype)]) if name == "w_in" else a


def _step(x, positions, target, small, blocks=None, whole=None):
    dist = blocks is not None
    core = "mesh" if dist else 0
    w = dict(small)
    if whole:
        w.update(whole)

    def gather(names):
        return [_gather_duty([_wire_block(n, blocks[n]) for n in names])] if dist else []

    def put(names, results):
        if dist:
            for n, r in zip(names, results[0]):
                w[n] = _whole_from_gathered(n, r)

    g, sums, red = {}, {}, {}

    def swap(names):
        return [_swap_duty([g[n][1] for n in names])] if dist else []

    def chip_sums(names, from_sibling):
        if dist:
            for n, recv in zip(names, from_sibling):
                sums[n] = _chip_sum(g[n][0], recv, "chip_sum_" + n)

    def exchange(names):
        return [_exchange_duty([sums[n][1] for n in names])] if dist else []

    def reduced(names, from_chips):
        if dist:
            for n, recv in zip(names, from_chips):
                red[n] = (sums[n][0], recv)

    cos, sin_s = _rope_tables(positions)
    e, e1, exf = _selectors()
    bias = _attn_bias()
    alx = jnp.repeat(w["a_log"], HD, axis=1)
    dskx = jnp.repeat(w["d_skip"], HD, axis=1)

    if dist:
        put(FFN1, _comm_only(gather(FFN1), "gather_ffn1"))
    (x1, n1, a1, b1, hm1, h1), got = _ffn_fwd(x, w["ffn1_pre_norm"], w["ffn1_w_gate"], w["ffn1_w_up"], w["ffn1_w_down"],
                                              w["ffn1_post_norm"], "ffn1_fwd", gather(["w_in", "conv_w"]))
    put(["w_in", "conv_w"], got)
    (n2, q, kx, vx, xbc, z, dtr), got = _inproj_fwd(x1, w["mix_pre_norm"], w["w_in"], cos, sin_s, exf.astype(BF16), gather(["w_out"]))
    put(["w_out"], got)
    (attn, lse), got = _attn_fwd(q, kx, vx, bias, gather(FFN2[:2]))
    put(FFN2[:2], got)
    (yn, y, hs), got = _ssd_fwd(xbc, z, dtr, w["conv_w"], w["conv_b"], w["dt_bias"], alx, dskx, w["ssm_norm"], e, gather(FFN2[2:]))
    put(FFN2[2:], got)
    x2, h2 = _outproj_fwd(x1, attn, yn, w["w_out"], w["mix_post_norm"])
    (x3, n3, a3, b3, hm3, h3), _ = _ffn_fwd(x2, w["ffn2_pre_norm"], w["ffn2_w_gate"], w["ffn2_w_up"], w["ffn2_w_down"],
                                            w["ffn2_post_norm"], "ffn2_fwd")
    dx3, ss = _loss_head(x3, target)

    (dx2, da3, db3, dh3, g["ffn2_pre_norm"], g["ffn2_post_norm"]), _ = _ffn_bwd(
        dx3, x2, a3, b3, h3, w["ffn2_pre_norm"], w["ffn2_post_norm"], w["ffn2_w_gate"], w["ffn2_w_up"], w["ffn2_w_down"], "ffn2_bwd")
    g["ffn2_w_down"] = _matmul_tn(hm3, dh3, "ffn2_dwd", core)[0]
    g["ffn2_w_gate"] = _matmul_tn(da3, n3, "ffn2_dwg", core)[0]
    g["ffn2_w_up"] = _matmul_tn(db3, n3, "ffn2_dwu", core)[0]

    (dh2, dmix, g["mix_post_norm"]), got = _outproj_bwd(dx2, h2, w["mix_post_norm"], w["w_out"], swap(FFN2))
    chip_sums(FFN2, got[0] if dist else None)
    g["w_out"] = _dwout(attn, yn, dh2, core)
    (dq, dkx, dvx), got = _attn_bwd(q, kx, vx, attn, dmix, lse, bias, exchange(FFN2) + swap(["w_out"]))
    if dist:
        reduced(FFN2, got[0])
        chip_sums(["w_out"], got[1])
    (dxbc, dz, ddt, dcw, g["conv_b"], g["ssm_norm"], dpar), got = _ssd_bwd(
        dmix, xbc, z, dtr, y, hs, w["conv_w"], w["conv_b"], w["dt_bias"], alx, dskx, w["ssm_norm"], e, e1, exchange(["w_out"]))
    reduced(["w_out"], got[0] if dist else None)
    g["conv_w"] = dcw[0:4]
    g["dt_bias"], g["a_log"], g["d_skip"] = dpar[0:1], dpar[1:2], dpar[2:3]
    dx1, dproj, g["mix_pre_norm"] = _inproj_bwd(dx2, dq, dkx, dvx, dxbc, dz, ddt, x1, w["mix_pre_norm"], w["w_in"], cos, sin_s, exf)
    dwin = _matmul_tn(dproj, n2, "dwin")[0][0][:IN_COLS].reshape(N_DEV // 2, 2, -1, D)
    c_idx = _core_index(core)
    g["w_in"] = (lax.dynamic_index_in_dim(dwin, c_idx, 1, keepdims=False),
                 lax.dynamic_index_in_dim(dwin, 1 - c_idx, 1, keepdims=False).astype(BF16))

    started = {}

    def start(n):
        started[n] = _exchange_start(sums[n][1], "start_exchange_" + n)
        return [started[n]["token"]]

    after = []
    if dist:
        chip_sums(["w_in"], _comm_only(swap(["w_in"]), "swap_w_in")[0])
        after = start("w_in")
    (dx0, da1, db1, dh1, g["ffn1_pre_norm"], g["ffn1_post_norm"]), _ = _ffn_bwd(
        dx1, x, a1, b1, h1, w["ffn1_pre_norm"], w["ffn1_post_norm"], w["ffn1_w_gate"], w["ffn1_w_up"], w["ffn1_w_down"], "ffn1_bwd",
        after)
    total = None
    if dist:
        flat = jnp.concatenate([g[n].reshape(-1) for n in SMALL] + [g["conv_w"].reshape(-1), ss[0, 0:1]])
        rows = -(-flat.shape[0] // 128 // 8) * 8
        total = _all_reduce_small(jnp.pad(flat, (0, rows * 128 - flat.shape[0])).reshape(rows, 128), after)
        after = [total]
    g["ffn1_w_down"], _ = _matmul_tn(hm1, dh1, "ffn1_dwd", core, after=after)
    g["ffn1_w_gate"], got = _matmul_tn(da1, n1, "ffn1_dwg", core, duties=swap(["ffn1_w_down"]))
    if dist:
        chip_sums(["ffn1_w_down"], got[0])
        after = start("ffn1_w_down")
    g["ffn1_w_up"], got = _matmul_tn(db1, n1, "ffn1_dwu", core, after=after, duties=swap(["ffn1_w_gate"]))
    if dist:
        chip_sums(["ffn1_w_gate"], got[0])
        after = start("ffn1_w_gate")
        chip_sums(["ffn1_w_up"], _comm_only(swap(["ffn1_w_up"]), "swap_ffn1_w_up", after=after)[0])
        start("ffn1_w_up")
    return ss, dx0, g, red, {n: (sums[n][0], started[n]) for n in started}, total


def kernel(x, positions, ffn1_pre_norm, ffn1_w_gate, ffn1_w_up, ffn1_w_down, ffn1_post_norm, mix_pre_norm, w_in, conv_w, conv_b, dt_bias, a_log, d_skip, ssm_norm, w_out, mix_post_norm, ffn2_pre_norm, ffn2_w_gate, ffn2_w_up, ffn2_w_down, ffn2_post_norm, loss_target, m_ffn1_pre_norm, m_ffn1_w_gate, m_ffn1_w_up, m_ffn1_w_down, m_ffn1_post_norm, m_mix_pre_norm, m_w_in, m_conv_w, m_conv_b, m_dt_bias, m_a_log, m_d_skip, m_ssm_norm, m_w_out, m_mix_post_norm, m_ffn2_pre_norm, m_ffn2_w_gate, m_ffn2_w_up, m_ffn2_w_down, m_ffn2_post_norm, v_ffn1_pre_norm, v_ffn1_w_gate, v_ffn1_w_up, v_ffn1_w_down, v_ffn1_post_norm, v_mix_pre_norm, v_w_in, v_conv_w, v_conv_b, v_dt_bias, v_a_log, v_d_skip, v_ssm_norm, v_w_out, v_mix_post_norm, v_ffn2_pre_norm, v_ffn2_w_gate, v_ffn2_w_up, v_ffn2_w_down, v_ffn2_post_norm):
    given = dict(locals())
    drop = lambda n, a: a if n in SMALL else (a[0].T if n in COL_SHARDED else a[0])
    w = {n: drop(n, given[n]) for n in WEIGHTS}
    m = {n: drop(n, given["m_" + n]) for n in WEIGHTS}
    v = {n: drop(n, given["v_" + n]) for n in WEIGHTS}
    cx, cy, cc = _place()
    others = [2 * (1 - cx) + cy, 2 * cx + (1 - cy), 2 * (1 - cx) + (1 - cy)]

    _, grad_x, g, red, pending, total = _step(x[0], positions, loss_target[0], {n: w[n] for n in SMALL},
                                              blocks={n: w[n] for n in BIG + ["conv_w"]})
    total = total.reshape(-1)
    chip_ids = jnp.stack(others).astype(jnp.int32)
    out_g, out_d, out_m, out_v = {}, {}, {}, {}
    last_start = [pending["ffn1_w_up"][1]["token"]]
    for n in BIG:
        if n not in pending:
            out_g[n], out_d[n], out_m[n], out_v[n] = _adamw_sharded(w[n], m[n], v[n], red[n][0], red[n][1], chip_ids, "adamw_" + n,
                                                                    last_start)

    rows = total.shape[0] // 128
    sizes = [w[n].size for n in SMALL]
    offs = [0]
    for s_ in sizes:
        offs.append(offs[-1] + s_)
    gs = {n: total[offs[i]:offs[i + 1]].reshape(w[n].shape) for i, n in enumerate(SMALL)}
    gcw = total[offs[-1]:offs[-1] + 4 * CONV_C].reshape(4, CONV_C)
    loss = 0.5 * total[offs[-1] + 4 * CONV_C] / D
    gs["conv_w"] = lax.dynamic_slice_in_dim(gcw, (4 * cx + 2 * cy + cc) * (CONV_C // N_DEV), CONV_C // N_DEV, axis=1)
    names = SMALL + ["conv_w"]

    def pack(d):
        flat = jnp.concatenate([d[n].reshape(-1) for n in names])
        return jnp.pad(flat, (0, rows * 128 - flat.shape[0])).reshape(rows, 128)

    pg, pd, pm, pv = _adamw(pack(w), pack(m), pack(v), [pack(gs)], "adamw_small", last_start)
    done = [pg] + [out_v[n] for n in BIG if n not in pending]
    for n, (chip_sum, started) in pending.items():
        recv = _exchange_wait(started, done, "wait_exchange_" + n)
        out_g[n], out_d[n], out_m[n], out_v[n] = _adamw_sharded(w[n], m[n], v[n], chip_sum, recv, chip_ids, "adamw_" + n)
        done = [out_v[n]]
    o2 = [0]
    for n in names:
        o2.append(o2[-1] + w[n].size)
    for i, n in enumerate(names):
        for dst, src in ((out_g, pg), (out_d, pd), (out_m, pm), (out_v, pv)):
            dst[n] = src.reshape(-1)[o2[i]:o2[i + 1]].reshape(w[n].shape)

    outs = [loss, grad_x[None]]
    for d in (out_g, out_d, out_m, out_v):
        outs += [d[n] if n in SMALL else (d[n].T[None] if n in COL_SHARDED else d[n][None]) for n in WEIGHTS]
    return tuple(outs)
```

```python
import functools
import math

import jax
import jax.numpy as jnp
from jax import lax
from jax.experimental import pallas as pl
from jax.experimental.pallas import tpu as pltpu

F32 = jnp.float32
BF16 = jnp.bfloat16
MESH = pl.DeviceIdType.MESH

N_DEV = 8
T = 2048
D = 1024
FF = 2816
FSR = FF // N_DEV
FSH = 384
FFP = N_DEV * FSH
HD = 64
NQ = 16
NKV = 4
QW = NQ * HD
KVW = NKV * HD
SSM_W = 1024
SSM_H = 16
SSM_N = 128
CONV_C = SSM_W + 2 * 2 * SSM_N
IN_COLS = 4112
INP = 4224
ISR = IN_COLS // N_DEV
ISW = 528
ISG = 640
L = 128
NCH = T // L
AB = 256
NAB = T // AB
EPS = 1e-6
NEG = -1e30
ROPE_THETA = 10000.0
DILATIONS = ((128, 1), (512, 4), (2048, 16))

ADAM_LR = 0.001
ADAM_B1 = 0.9
ADAM_B2 = 0.999
ADAM_EPS = 1e-08
ADAM_WD = 0.01
ADAM_STEP = 10

VMEM_LIMIT = 58 * 1024 * 1024


def _params(sem, vmem=VMEM_LIMIT):
    return pltpu.CompilerParams(dimension_semantics=sem, vmem_limit_bytes=vmem)


def _dot(a, b):
    return jnp.dot(a, b, preferred_element_type=F32)


def _dot_nt(a, b):
    return lax.dot_general(a, b, (((1,), (1,)), ((), ())), preferred_element_type=F32)


def _dot_tn(a, b):
    return lax.dot_general(a, b, (((0,), (0,)), ((), ())), preferred_element_type=F32)


def _split3(x):
    hi = x.astype(BF16)
    r1 = x - hi.astype(F32)
    mid = r1.astype(BF16)
    lo = (r1 - mid.astype(F32)).astype(BF16)
    return hi, mid, lo


def _dot_hi(a, b, a_is_01=False):
    if a_is_01:
        sel = a.astype(BF16)
        return sum(_dot(sel, p) for p in _split3(b))
    sel = b.astype(BF16)
    return sum(_dot(p, sel) for p in _split3(a))


def _dot_nt_hi(a, b):
    sel = b.astype(BF16)
    return sum(_dot_nt(p, sel) for p in _split3(a))


def _rs(x):
    return lax.rsqrt(jnp.mean(x * x, axis=-1, keepdims=True) + EPS)


def _sigmoid(x):
    return jax.nn.sigmoid(x)


def _dsilu(x, s):
    return s * (1.0 + x * (1.0 - s))


def _resident(shape):
    nd = len(shape)
    return pl.BlockSpec(shape, lambda *_: (0,) * nd, pipeline_mode=pl.Buffered(1))


def _const(shape):
    nd = len(shape)
    return pl.BlockSpec(shape, lambda *_: (0,) * nd)


def _rows(tm, cols):
    return pl.BlockSpec((tm, cols), lambda i: (i, 0))


ANY = pl.BlockSpec(memory_space=pl.ANY)


def _place():
    return lax.axis_index("x"), lax.axis_index("y"), lax.axis_index("c")


def _gather_duty(arrays):
    n = len(arrays)
    results = [jax.ShapeDtypeStruct((N_DEV,) + a.shape, a.dtype) for a in arrays]

    def make(ins, outs, send_sems, recv_sems, local_sems):
        x, y, c = _place()
        me, sibling = (x, y, c), (x, y, 1 - c)
        chips = [(1 - x, y), (x, 1 - y), (1 - x, 1 - y)]

        def place_of(a, px, py, pc):
            return outs[a].at[4 * px + 2 * py + pc]

        def copy(a, k, block, to, src=None):
            dst = place_of(a, *block)
            return pltpu.make_async_remote_copy(src_ref=dst if src is None else src, dst_ref=dst,
                                                send_sem=send_sems.at[7 * a + k], recv_sem=recv_sems.at[7 * a + k],
                                                device_id=to, device_id_type=MESH)

        def own(a):
            return pltpu.make_async_copy(ins[a], place_of(a, *me), local_sems.at[a])

        def first(a):
            return [copy(a, 0, me, sibling, src=ins[a])] + [copy(a, 1 + j, me, (*chip, c), src=ins[a]) for j, chip in enumerate(chips)]

        def start():
            for a in range(n):
                own(a).start()
            for a in range(n):
                for cp in first(a):
                    cp.start()

        def finish():
            for j, chip in enumerate(chips):
                for a in range(n):
                    copy(a, 1 + j, (*chip, c), me).wait_recv()
                    copy(a, 4 + j, (*chip, c), sibling).start()
            for a in range(n):
                copy(a, 0, sibling, me).wait_recv()
                for j, chip in enumerate(chips):
                    copy(a, 4 + j, (*chip, 1 - c), me).wait_recv()
            for a in range(n):
                for cp in first(a) + [copy(a, 4 + j, (*chip, c), sibling) for j, chip in enumerate(chips)]:
                    cp.wait_send()
                own(a).wait()

        return start, finish

    return dict(operands=list(arrays), results=results, sems=(7 * n, 7 * n, n), make=make)


def _swap_duty(arrays):
    n = len(arrays)
    half = N_DEV // 2
    results = [jax.ShapeDtypeStruct(a.shape, a.dtype) for a in arrays]

    def make(ins, outs, send_sems, recv_sems):
        x, y, c = _place()

        def copies():
            return [pltpu.make_async_remote_copy(src_ref=ins[a].at[k], dst_ref=outs[a].at[k],
                                                 send_sem=send_sems.at[half * a + k], recv_sem=recv_sems.at[half * a + k],
                                                 device_id=(x, y, 1 - c), device_id_type=MESH)
                    for a in range(n) for k in range(half)]

        def start():
            for cp in copies():
                cp.start()

        def finish():
            for cp in copies():
                cp.wait()

        return start, finish

    return dict(operands=list(arrays), results=results, sems=(half * n, half * n), make=make)


def _exchange_duty(arrays):
    n = len(arrays)
    results = [jax.ShapeDtypeStruct(a.shape, a.dtype) for a in arrays]

    def make(ins, outs, send_sems, recv_sems):
        x, y, c = _place()
        chips = [(1 - x, y), (x, 1 - y), (1 - x, 1 - y)]
        my_chip = 2 * x + y

        def sends():
            return [pltpu.make_async_remote_copy(src_ref=ins[a].at[2 * px + py], dst_ref=outs[a].at[my_chip],
                                                 send_sem=send_sems.at[3 * a + j], recv_sem=recv_sems.at[3 * a + j],
                                                 device_id=(px, py, c), device_id_type=MESH)
                    for a in range(n) for j, (px, py) in enumerate(chips)]

        def start():
            for cp in sends():
                cp.start()

        def finish():
            for a in range(n):
                for j, (px, py) in enumerate(chips):
                    pltpu.make_async_remote_copy(src_ref=ins[a].at[my_chip], dst_ref=outs[a].at[2 * px + py],
                                                 send_sem=send_sems.at[3 * a + j], recv_sem=recv_sems.at[3 * a + j],
                                                 device_id=(px, py, c), device_id_type=MESH).wait_recv()
            for cp in sends():
                cp.wait_send()

        return start, finish

    return dict(operands=list(arrays), results=results, sems=(3 * n, 3 * n), make=make)


def _call(body, *, name, grid, in_specs, out_specs, out_shape, args, sem, scratch=(), duties=(), after=()):
    n_in, n_out, n_scr = len(in_specs), len(out_specs), len(scratch)
    sem_shapes = [pltpu.SemaphoreType.DMA((k,)) for d in duties for k in d["sems"]]

    def full(*refs):
        pos = [0]

        def take(k):
            pos[0] += k
            return refs[pos[0] - k:pos[0]]

        ins = take(n_in)
        d_ins = [take(len(d["operands"])) for d in duties]
        take(len(after))
        outs = take(n_out)
        d_outs = [take(len(d["results"])) for d in duties]
        scr = take(n_scr)
        d_sems = [take(len(d["sems"])) for d in duties]
        hooks = [d["make"](di, do, *ds) for d, di, do, ds in zip(duties, d_ins, d_outs, d_sems)]
        if grid and hooks:
            ids = [pl.program_id(k) for k in range(len(grid))]
            first = functools.reduce(jnp.logical_and, [i == 0 for i in ids])
            last = functools.reduce(jnp.logical_and, [i == g - 1 for i, g in zip(ids, grid)])

            @pl.when(first)
            def _():
                for start, _ in hooks:
                    start()

            body(*ins, *outs, *scr)

            @pl.when(last)
            def _():
                for _, finish in hooks:
                    finish()
        else:
            for start, _ in hooks:
                start()
            body(*ins, *outs, *scr)
            for _, finish in hooks:
                finish()

    d_args = [a for d in duties for a in d["operands"]]
    d_res = [r for d in duties for r in d["results"]]
    kwargs = dict(grid=grid) if grid else {}
    res = pl.pallas_call(
        full, name=name, in_specs=list(in_specs) + [ANY] * (len(d_args) + len(after)), out_specs=list(out_specs) + [ANY] * len(d_res),
        out_shape=list(out_shape) + d_res, scratch_shapes=list(scratch) + sem_shapes,
        compiler_params=_params(sem) if grid else None, **kwargs,
    )(*args, *d_args, *after)
    own, rest = list(res[:n_out]), list(res[n_out:])
    by_duty = []
    for d in duties:
        by_duty.append(rest[:len(d["results"])])
        rest = rest[len(d["results"]):]
    return own, by_duty


def _comm_only(duties, name, after=()):
    return _call(lambda: None, name=name, grid=None, in_specs=[], out_specs=[], out_shape=[], args=[], sem=None, duties=duties,
                 after=after)[1]


HBM = pl.BlockSpec(memory_space=pltpu.HBM)
SEMS = pl.BlockSpec(memory_space=pltpu.SEMAPHORE)
SIDE_EFFECT = pltpu.SideEffectType.DATAFLOW_SIDE_EFFECTING
N_OTHER_CHIPS = 3


def _chip_copies(src_ref, land_ref, sems):
    x, y, c = _place()
    chips = [(1 - x, y), (x, 1 - y), (1 - x, 1 - y)]
    return [pltpu.make_async_remote_copy(src_ref=src_ref.at[2 * px + py], dst_ref=land_ref.at[2 * x + y],
                                         send_sem=sems[j], recv_sem=sems[N_OTHER_CHIPS + j], device_id=(px, py, c), device_id_type=MESH)
            for j, (px, py) in enumerate(chips)]


def _exchange_start(pb, name):
    n_sem = 2 * N_OTHER_CHIPS

    def body(pb_ref, land_ref, *rest):
        for cp in _chip_copies(pb_ref, land_ref, rest[:n_sem]):
            cp.start()
        token = rest[n_sem + 2]
        token[...] = jnp.zeros_like(token)

    res = pl.pallas_call(
        body, name=name,
        out_shape=(pltpu.SemaphoreType.DMA(()),) * n_sem + (pltpu.HBM(pb.shape, pb.dtype), pltpu.HBM(pb.shape, pb.dtype),
                                                              jax.ShapeDtypeStruct((8, 128), F32)),
        in_specs=(HBM, HBM), out_specs=(SEMS,) * n_sem + (HBM, HBM, pl.BlockSpec(memory_space=pltpu.VMEM)),
        input_output_aliases={0: n_sem, 1: n_sem + 1},
        compiler_params=pltpu.CompilerParams(has_side_effects=SIDE_EFFECT),
    )(pltpu.with_memory_space_constraint(pb, pltpu.HBM), pltpu.with_memory_space_constraint(lax.empty(pb.shape, pb.dtype), pltpu.HBM))
    return dict(sems=res[:n_sem], src=res[n_sem], land=res[n_sem + 1], token=res[n_sem + 2])


def _exchange_wait(started, after, name):
    n_sem = 2 * N_OTHER_CHIPS

    def body(pb_ref, land_ref, *rest):
        for cp in _chip_copies(pb_ref, land_ref, rest[:n_sem]):
            cp.wait_send()
            cp.wait_recv()

    src, land = started["src"], started["land"]
    return pl.pallas_call(
        body, name=name, out_shape=(pltpu.HBM(src.shape, src.dtype), pltpu.HBM(land.shape, land.dtype)),
        in_specs=(HBM, HBM) + (SEMS,) * n_sem + (ANY,) * len(after), out_specs=(HBM, HBM), input_output_aliases={0: 0, 1: 1},
        compiler_params=pltpu.CompilerParams(has_side_effects=SIDE_EFFECT),
    )(src, land, *started["sems"], *after)[1]


def _ffn_fwd(x, gpre, wg, wu, wd, gpost, name, duties=()):
    tm = 256

    def body(x_ref, gpre_ref, wg_ref, wu_ref, wd_ref, gpost_ref, xo_ref, n_ref, a_ref, b_ref, hm_ref, h_ref):
        xv = x_ref[...]
        n = (xv * _rs(xv) * gpre_ref[...]).astype(BF16)
        a = _dot_nt(n, wg_ref[...])
        b = _dot_nt(n, wu_ref[...])
        hm = (a * _sigmoid(a) * b).astype(BF16)
        h = _dot(hm, wd_ref[...])
        xo_ref[...] = xv + 0.5 * (h * _rs(h) * gpost_ref[...])
        n_ref[...] = n
        a_ref[...] = a.astype(BF16)
        b_ref[...] = b.astype(BF16)
        hm_ref[...] = hm
        h_ref[...] = h

    return _call(
        body, name=name, grid=(T // tm,),
        in_specs=[_rows(tm, D), _const((1, D)), _resident((FFP, D)), _resident((FFP, D)), _resident((FFP, D)), _const((1, D))],
        out_specs=[_rows(tm, D), _rows(tm, D), _rows(tm, FFP), _rows(tm, FFP), _rows(tm, FFP), _rows(tm, D)],
        out_shape=[jax.ShapeDtypeStruct((T, D), F32), jax.ShapeDtypeStruct((T, D), BF16), jax.ShapeDtypeStruct((T, FFP), BF16),
                   jax.ShapeDtypeStruct((T, FFP), BF16), jax.ShapeDtypeStruct((T, FFP), BF16), jax.ShapeDtypeStruct((T, D), F32)],
        args=[x, gpre, wg, wu, wd, gpost], sem=("arbitrary",), duties=duties)


def _ffn_bwd(dxo, x, a, b, h, gpre, gpost, wg, wu, wd, name, after=()):
    tm = 256

    def body(dxo_ref, x_ref, a_ref, b_ref, h_ref, gpre_ref, gpost_ref, wg_ref, wu_ref, wd_ref,
             dx_ref, da_ref, db_ref, dh_ref, dgpre_ref, dgpost_ref):
        @pl.when(pl.program_id(0) == 0)
        def _():
            dgpre_ref[...] = jnp.zeros_like(dgpre_ref)
            dgpost_ref[...] = jnp.zeros_like(dgpost_ref)

        dy = dxo_ref[...]
        h = h_ref[...]
        hn = h * _rs(h)
        r2 = _rs(h)
        dgpost_ref[...] += jnp.sum(0.5 * dy * hn, axis=0, keepdims=True)
        gdy = 0.5 * dy * gpost_ref[...]
        dh = r2 * (gdy - hn * jnp.mean(gdy * hn, axis=-1, keepdims=True))
        dhb = dh.astype(BF16)
        dh_ref[...] = dhb
        dhm = _dot_nt(dhb, wd_ref[...])
        av = a_ref[...].astype(F32)
        bv = b_ref[...].astype(F32)
        sg = _sigmoid(av)
        db = (dhm * (av * sg)).astype(BF16)
        da = (dhm * bv * _dsilu(av, sg)).astype(BF16)
        da_ref[...] = da
        db_ref[...] = db
        dn = _dot(da, wg_ref[...]) + _dot(db, wu_ref[...])
        xv = x_ref[...]
        r = _rs(xv)
        xn = xv * r
        dgpre_ref[...] += jnp.sum(dn * xn, axis=0, keepdims=True)
        gdn = dn * gpre_ref[...]
        dx_ref[...] = dy + r * (gdn - xn * jnp.mean(gdn * xn, axis=-1, keepdims=True))

    return _call(
        body, name=name, grid=(T // tm,),
        in_specs=[_rows(tm, D), _rows(tm, D), _rows(tm, FFP), _rows(tm, FFP), _rows(tm, D), _const((1, D)), _const((1, D)),
                  _resident((FFP, D)), _resident((FFP, D)), _resident((FFP, D))],
        out_specs=[_rows(tm, D), _rows(tm, FFP), _rows(tm, FFP), _rows(tm, D), _const((1, D)), _const((1, D))],
        out_shape=[jax.ShapeDtypeStruct((T, D), F32), jax.ShapeDtypeStruct((T, FFP), BF16), jax.ShapeDtypeStruct((T, FFP), BF16),
                   jax.ShapeDtypeStruct((T, D), BF16), jax.ShapeDtypeStruct((1, D), F32), jax.ShapeDtypeStruct((1, D), F32)],
        args=[dxo, x, a, b, h, gpre, gpost, wg, wu, wd], sem=("arbitrary",), after=after)


def _core_index(core):
    return lax.axis_index("c") if core == "mesh" else core


def _by_core(put32, put16, step, core):
    mine = (step % 2) == _core_index(core)
    pl.when(mine)(put32)
    pl.when(jnp.logical_not(mine))(put16)


def _matmul_tn(a, b, name, core, after=(), duties=()):
    k, m = a.shape
    n = b.shape[1]
    r = m // N_DEV
    assert m == N_DEV * r and r % 128 == 0

    def body(a_ref, b_ref, o_ref, ob_ref):
        res = _dot_tn(a_ref[...], b_ref[...])

        def put32():
            o_ref[0] = res

        def put16():
            ob_ref[0] = res.astype(BF16)

        _by_core(put32, put16, pl.program_id(0), core)

    spec = pl.BlockSpec((1, r, n), lambda i: (i // 2, 0, 0))
    return _call(body, name=name, grid=(N_DEV,), in_specs=[pl.BlockSpec((k, r), lambda i: (0, i)), _resident((k, n))],
                 out_specs=[spec, spec],
                 out_shape=[jax.ShapeDtypeStruct((N_DEV // 2, r, n), F32), jax.ShapeDtypeStruct((N_DEV // 2, r, n), BF16)],
                 args=[a, b], sem=("arbitrary",), duties=duties, after=after)


def _dwout(attn, yn, dh2, core):
    rs = (QW + SSM_W) // N_DEV
    half = N_DEV // 2

    def body(at_ref, yn_ref, dh_ref, o_ref, ob_ref):
        i = pl.program_id(0)

        def put(r):
            def put32():
                o_ref[0] = r

            def put16():
                ob_ref[0] = r.astype(BF16)

            _by_core(put32, put16, i, core)

        @pl.when(i < half)
        def _():
            put(_dot_tn(at_ref[...], dh_ref[...]))

        @pl.when(i >= half)
        def _():
            put(_dot_tn(yn_ref[...], dh_ref[...]))

    spec = pl.BlockSpec((1, rs, D), lambda i: (i // 2, 0, 0))
    return pl.pallas_call(
        body, name="dwout", grid=(N_DEV,),
        in_specs=[pl.BlockSpec((T, rs), lambda i: (0, jnp.minimum(i, half - 1))),
                  pl.BlockSpec((T, rs), lambda i: (0, jnp.maximum(i - half, 0))), _resident((T, D))],
        out_specs=[spec, spec],
        out_shape=[jax.ShapeDtypeStruct((half, rs, D), F32), jax.ShapeDtypeStruct((half, rs, D), BF16)],
        compiler_params=_params(("arbitrary",)),
    )(attn, yn, dh2)


def _rope_swap(t, lo_half):
    return jnp.where(lo_half, pltpu.roll(t, 96, 1), pltpu.roll(t, 32, 1))


def _inproj_fwd(x1, gpre, win, cos, sin_s, ex, duties=()):
    tm = 256

    def body(x_ref, g_ref, w_ref, cos_ref, sin_ref, ex_ref, n_ref, q_ref, kx_ref, vx_ref, xbc_ref, z_ref, dt_ref):
        xv = x_ref[...]
        n = (xv * _rs(xv) * g_ref[...]).astype(BF16)
        n_ref[...] = n
        by_dev = _dot_nt(n, w_ref[...])
        proj = jnp.concatenate([by_dev[:, ISW * d:ISW * d + ISR] for d in range(N_DEV)], axis=1)
        cs = cos_ref[...]
        sn = sin_ref[...]
        lo_half = (lax.broadcasted_iota(jnp.int32, (1, 128), 1) % HD) < (HD // 2)

        def rope(t):
            return t * cs + _rope_swap(t, lo_half) * sn

        for j in range(QW // 128):
            t = proj[:, 128 * j:128 * j + 128]
            q_ref[:, 128 * j:128 * j + 128] = (rope(t) * (HD ** -0.5)).astype(BF16)
        k = jnp.concatenate([rope(proj[:, QW + 128 * j:QW + 128 * j + 128]) for j in range(KVW // 128)], axis=1)
        v = proj[:, QW + KVW:QW + 2 * KVW]
        kx_ref[...] = _dot(k.astype(BF16), ex_ref[...]).astype(BF16)
        vx_ref[...] = _dot(v.astype(BF16), ex_ref[...]).astype(BF16)
        c0 = QW + 2 * KVW
        xbc_ref[...] = proj[:, c0:c0 + CONV_C]
        z_ref[...] = proj[:, c0 + CONV_C:c0 + CONV_C + SSM_W]
        dt_ref[...] = proj[:, c0 + CONV_C + SSM_W:IN_COLS]

    return _call(
        body, name="inproj_fwd", grid=(T // tm,),
        in_specs=[_rows(tm, D), _const((1, D)), _resident((INP, D)), _rows(tm, 128), _rows(tm, 128), _const((KVW, QW))],
        out_specs=[_rows(tm, D), _rows(tm, QW), _rows(tm, QW), _rows(tm, QW), _rows(tm, CONV_C), _rows(tm, SSM_W), _rows(tm, SSM_H)],
        out_shape=[jax.ShapeDtypeStruct((T, D), BF16), jax.ShapeDtypeStruct((T, QW), BF16), jax.ShapeDtypeStruct((T, QW), BF16),
                   jax.ShapeDtypeStruct((T, QW), BF16), jax.ShapeDtypeStruct((T, CONV_C), F32), jax.ShapeDtypeStruct((T, SSM_W), F32),
                   jax.ShapeDtypeStruct((T, SSM_H), F32)],
        args=[x1, gpre, win, cos, sin_s, ex], sem=("arbitrary",), duties=duties)


def _inproj_bwd(dres, dq, dkx, dvx, dxbc, dz, ddt, x1, gpre, win, cos, sin_s, exf):
    tm = 256

    def body(dres_ref, dq_ref, dkx_ref, dvx_ref, dxbc_ref, dz_ref, ddt_ref, x_ref, g_ref, w_ref, cos_ref, sin_ref, ex_ref,
             dx_ref, dps_ref, dg_ref, dp_ref):
        @pl.when(pl.program_id(0) == 0)
        def _():
            dg_ref[...] = jnp.zeros_like(dg_ref)

        cs = cos_ref[...]
        sn = sin_ref[...]
        lo_half = (lax.broadcasted_iota(jnp.int32, (1, 128), 1) % HD) < (HD // 2)

        def rope_t(t):
            return t * cs - _rope_swap(t, lo_half) * sn

        for j in range(QW // 128):
            dp_ref[:, 128 * j:128 * j + 128] = rope_t(dq_ref[:, 128 * j:128 * j + 128] * (HD ** -0.5)).astype(BF16)
        dk = _dot_nt_hi(dkx_ref[...], ex_ref[...])
        dv = _dot_nt_hi(dvx_ref[...], ex_ref[...])
        for j in range(KVW // 128):
            dp_ref[:, QW + 128 * j:QW + 128 * j + 128] = rope_t(dk[:, 128 * j:128 * j + 128]).astype(BF16)
        dp_ref[:, QW + KVW:QW + 2 * KVW] = dv.astype(BF16)
        c0 = QW + 2 * KVW
        dp_ref[:, c0:c0 + CONV_C] = dxbc_ref[...].astype(BF16)
        dp_ref[:, c0 + CONV_C:c0 + CONV_C + SSM_W] = dz_ref[...].astype(BF16)
        dp_ref[:, c0 + CONV_C + SSM_W:INP] = ddt_ref[...].astype(BF16)
        pieces = [dp_ref[:, ISR * d:ISR * (d + 1)] for d in range(N_DEV)]
        zw = jnp.zeros((tm, ISW - ISR), BF16)
        zg = jnp.zeros((tm, ISG - ISR), BF16)
        dn = _dot(jnp.concatenate([t for p in pieces for t in (p, zw)], axis=1), w_ref[...])
        for d in range(N_DEV):
            dps_ref[:, ISG * d:ISG * (d + 1)] = jnp.concatenate([pieces[d], zg], axis=1)
        xv = x_ref[...]
        r = _rs(xv)
        xn = xv * r
        dg_ref[...] += jnp.sum(dn * xn, axis=0, keepdims=True)
        gdn = dn * g_ref[...]
        dx_ref[...] = dres_ref[...] + r * (gdn - xn * jnp.mean(gdn * xn, axis=-1, keepdims=True))

    return pl.pallas_call(
        body, name="inproj_bwd", grid=(T // tm,),
        in_specs=[_rows(tm, D), _rows(tm, QW), _rows(tm, QW), _rows(tm, QW), _rows(tm, CONV_C), _rows(tm, SSM_W), _rows(tm, 128),
                  _rows(tm, D), _const((1, D)), _resident((INP, D)), _rows(tm, 128), _rows(tm, 128), _const((KVW, QW))],
        out_specs=[_rows(tm, D), _rows(tm, N_DEV * ISG), _const((1, D))],
        out_shape=[jax.ShapeDtypeStruct((T, D), F32), jax.ShapeDtypeStruct((T, N_DEV * ISG), BF16), jax.ShapeDtypeStruct((1, D), F32)],
        scratch_shapes=[pltpu.VMEM((tm, INP), BF16)],
        compiler_params=_params(("arbitrary",)),
    )(dres, dq, dkx, dvx, dxbc, dz, ddt, x1, gpre, win, cos, sin_s, exf)


def _outproj_fwd(x1, attn, yn, wout, gpost):
    tm = 256

    def body(x_ref, at_ref, yn_ref, w_ref, g_ref, xo_ref, h_ref):
        h = _dot(at_ref[...], w_ref[0:QW, :]) + _dot(yn_ref[...], w_ref[QW:QW + SSM_W, :])
        h_ref[...] = h
        xo_ref[...] = x_ref[...] + h * _rs(h) * g_ref[...]

    return pl.pallas_call(
        body, name="outproj_fwd", grid=(T // tm,),
        in_specs=[_rows(tm, D), _rows(tm, QW), _rows(tm, SSM_W), _resident((QW + SSM_W, D)), _const((1, D))],
        out_specs=[_rows(tm, D), _rows(tm, D)],
        out_shape=[jax.ShapeDtypeStruct((T, D), F32), jax.ShapeDtypeStruct((T, D), F32)],
        compiler_params=_params(("parallel",)),
    )(x1, attn, yn, wout, gpost)


def _outproj_bwd(dx2, h2, gpost, wout, duties=()):
    tm = 256

    def body(dy_ref, h_ref, g_ref, w_ref, dh_ref, dm_ref, dg_ref):
        @pl.when(pl.program_id(0) == 0)
        def _():
            dg_ref[...] = jnp.zeros_like(dg_ref)

        dy = dy_ref[...]
        h = h_ref[...]
        r = _rs(h)
        hn = h * r
        dg_ref[...] += jnp.sum(dy * hn, axis=0, keepdims=True)
        gdy = dy * g_ref[...]
        dh = (r * (gdy - hn * jnp.mean(gdy * hn, axis=-1, keepdims=True))).astype(BF16)
        dh_ref[...] = dh
        dm_ref[...] = _dot_nt(dh, w_ref[...])

    return _call(
        body, name="outproj_bwd", grid=(T // tm,),
        in_specs=[_rows(tm, D), _rows(tm, D), _const((1, D)), _resident((QW + SSM_W, D))],
        out_specs=[_rows(tm, D), _rows(tm, QW + SSM_W), _const((1, D))],
        out_shape=[jax.ShapeDtypeStruct((T, D), BF16), jax.ShapeDtypeStruct((T, QW + SSM_W), F32), jax.ShapeDtypeStruct((1, D), F32)],
        args=[dx2, h2, gpost, wout], sem=("arbitrary",), duties=duties)


def _attn_bias():
    d = jnp.arange(AB)[:, None] - jnp.arange(T)[None, :] + (T - AB)
    cnt = jnp.zeros(d.shape, F32)
    for window, dil in DILATIONS:
        cnt = cnt + ((d >= 0) & (d % dil == 0) & (d <= window)).astype(F32)
    return jnp.where(cnt > 0, jnp.log(jnp.maximum(cnt, 1.0)), NEG)


G_PER = NQ // NKV
WK = G_PER * HD


def _attn_fwd(q, kx, vx, bias, duties=()):
    def body(q_ref, kx_ref, vx_ref, bias_ref, o_ref, lse_ref):
        lane = lax.broadcasted_iota(jnp.int32, (1, WK), 1)
        lse_ref[...] = jnp.zeros_like(lse_ref)
        for i in range(NAB):
            n = (i + 1) * AB
            rows = slice(i * AB, n)
            qi = q_ref[rows, :]
            kxi = kx_ref[0:n, :]
            vxi = vx_ref[0:n, :]
            bb = bias_ref[:, (NAB - 1 - i) * AB:]
            o_acc = jnp.zeros((AB, WK), F32)
            for g in range(G_PER):
                mg = (lane // HD) == g
                s = _dot_nt(jnp.where(mg, qi, jnp.zeros_like(qi)), kxi) + bb
                m = jnp.max(s, axis=1, keepdims=True)
                p = jnp.exp(s - m)
                l = jnp.sum(p, axis=1, keepdims=True)
                o_acc = jnp.where(mg, _dot(p.astype(BF16), vxi) / l, o_acc)
                lse_ref[rows, g:g + 1] = m + jnp.log(l)
            o_ref[rows, :] = o_acc.astype(BF16)

    col = lambda kv: (0, kv)
    return _call(
        body, name="attn_fwd", grid=(NKV,),
        in_specs=[pl.BlockSpec((T, WK), col), pl.BlockSpec((T, WK), col), pl.BlockSpec((T, WK), col), _const((AB, T))],
        out_specs=[pl.BlockSpec((T, WK), col), pl.BlockSpec((T, 128), col)],
        out_shape=[jax.ShapeDtypeStruct((T, QW), BF16), jax.ShapeDtypeStruct((T, NKV * 128), F32)],
        args=[q, kx, vx, bias], sem=("arbitrary",), duties=duties)


def _attn_bwd(q, kx, vx, o, dmix, lse, bias, duties=()):
    def body(q_ref, kx_ref, vx_ref, o_ref, do_ref, lse_ref, bias_ref, dq_ref, dkx_ref, dvx_ref):
        lane = lax.broadcasted_iota(jnp.int32, (1, WK), 1)
        dkx_ref[...] = jnp.zeros_like(dkx_ref)
        dvx_ref[...] = jnp.zeros_like(dvx_ref)
        for i in range(NAB):
            n = (i + 1) * AB
            rows = slice(i * AB, n)
            qi = q_ref[rows, :]
            dof = do_ref[rows, :]
            doi = dof.astype(BF16)
            prod = dof * o_ref[rows, :].astype(F32)
            kxi = kx_ref[0:n, :]
            vxi = vx_ref[0:n, :]
            bb = bias_ref[:, (NAB - 1 - i) * AB:]
            dq_acc = jnp.zeros((AB, WK), F32)
            for g in range(G_PER):
                mg = (lane // HD) == g
                qm = jnp.where(mg, qi, jnp.zeros_like(qi))
                dom = jnp.where(mg, doi, jnp.zeros_like(doi))
                delta = jnp.sum(jnp.where(mg, prod, 0.0), axis=1, keepdims=True)
                p = jnp.exp(_dot_nt(qm, kxi) + bb - lse_ref[rows, g:g + 1])
                ds = (p * (_dot_nt(dom, vxi) - delta)).astype(BF16)
                dvx_ref[0:n, :] += _dot_tn(p.astype(BF16), dom)
                dkx_ref[0:n, :] += _dot_tn(ds, qm)
                dq_acc = jnp.where(mg, _dot(ds, kxi), dq_acc)
            dq_ref[rows, :] = dq_acc

    col = lambda kv: (0, kv)
    return _call(
        body, name="attn_bwd", grid=(NKV,),
        in_specs=[pl.BlockSpec((T, WK), col), pl.BlockSpec((T, WK), col), pl.BlockSpec((T, WK), col), pl.BlockSpec((T, WK), col),
                  pl.BlockSpec((T, WK), col), pl.BlockSpec((T, 128), col), _const((AB, T))],
        out_specs=[pl.BlockSpec((T, WK), col), pl.BlockSpec((T, WK), col), pl.BlockSpec((T, WK), col)],
        out_shape=[jax.ShapeDtypeStruct((T, QW), F32)] * 3,
        args=[q, kx, vx, o, dmix, lse, bias], sem=("arbitrary",), duties=duties)


def _softplus(x):
    return jnp.maximum(x, 0.0) + jnp.log1p(jnp.exp(-jnp.abs(x)))


def _conv_taps(u, halo):
    zext = jnp.concatenate([halo, u], axis=0)
    return [pltpu.roll(zext, m, 0)[8:] for m in (1, 2, 3)]


def _ssd_chunk_common(u, halo, dtr, cw_ref, cb_ref, dtb_ref, alx_ref, e_ref):
    sh1, sh2, sh3 = _conv_taps(u, halo)
    xc = cb_ref[...] + cw_ref[3:4, :] * u + cw_ref[2:3, :] * sh1 + cw_ref[1:2, :] * sh2 + cw_ref[0:1, :] * sh3
    sg = _sigmoid(xc)
    act = xc * sg
    pre_x = _dot_hi(dtr + dtb_ref[...], e_ref[...])
    dt_x = _softplus(pre_x)
    a_x = -jnp.exp(alx_ref[...])
    ri = lax.broadcasted_iota(jnp.int32, (L, L), 0)
    ci = lax.broadcasted_iota(jnp.int32, (L, L), 1)
    tri = ri >= ci
    acs_x = _dot_hi(tri, dt_x * a_x, a_is_01=True)
    return dict(sh=(sh1, sh2, sh3), xc=xc, sg=sg, act=act, pre_x=pre_x, dt_x=dt_x, a_x=a_x, tri=tri, acs_x=acs_x)


def _decay(acs_x, acs_t, h, tri):
    col = acs_x[:, HD * h:HD * h + 1]
    row = acs_t[HD * h:HD * h + 1, :]
    return jnp.exp(jnp.where(tri, col - row, NEG))


def _ssd_fwd(xbc, z, dtr, convw, convb, dtb, alx, dskx, ssmn, e, duties=()):
    def body(u_ref, z_ref, dtr_ref, cw_ref, cb_ref, dtb_ref, alx_ref, dsk_ref, sn_ref, e_ref,
             yn_ref, y_ref, hs_ref, halo, hst):
        @pl.when(pl.program_id(0) == 0)
        def _():
            halo[...] = jnp.zeros_like(halo)
            hst[...] = jnp.zeros_like(hst)

        u = u_ref[...]
        cm = _ssd_chunk_common(u, halo[...], dtr_ref[...], cw_ref, cb_ref, dtb_ref, alx_ref, e_ref)
        halo[...] = u[L - 8:, :]
        act, dt_x, acs_x, tri = cm["act"], cm["dt_x"], cm["acs_x"], cm["tri"]
        xs = act[:, :SSM_W]
        acs_l = acs_x[L - 1:L, :]
        lam_x = jnp.exp(acs_x)
        w_x = jnp.exp(acs_l - acs_x)
        gam_x = jnp.exp(acs_l)
        acs_t = acs_x.T
        xd = xs * dt_x
        xb = xd.astype(BF16)
        xw = (xd * w_x).astype(BF16)
        lo = lax.broadcasted_iota(jnp.int32, (1, 128), 1) < HD
        hs_ref[0] = hst[...]
        pieces = []
        for grp in range(2):
            bb = act[:, SSM_W + SSM_N * grp:SSM_W + SSM_N * (grp + 1)].astype(BF16)
            cb_ = act[:, SSM_W + 2 * SSM_N + SSM_N * grp:SSM_W + 2 * SSM_N + SSM_N * (grp + 1)].astype(BF16)
            cbm = _dot_nt(cb_, bb)
            for jj in range(4):
                j = 4 * grp + jj
                sl = slice(128 * j, 128 * j + 128)
                m0 = (cbm * _decay(acs_x, acs_t, 2 * j, tri)).astype(BF16)
                m1 = (cbm * _decay(acs_x, acs_t, 2 * j + 1, tri)).astype(BF16)
                x2 = xb[:, sl]
                ydiag = jnp.where(lo, _dot(m0, x2), _dot(m1, x2))
                hprev = hst[j]
                yoff = lam_x[:, sl] * _dot(cb_, hprev.astype(BF16))
                pieces.append(ydiag + yoff)
                hst[j] = gam_x[:, sl] * hprev + _dot_tn(bb, xw[:, sl])
        y = jnp.concatenate(pieces, axis=1) + dsk_ref[...] * xs
        y_ref[...] = y
        zv = z_ref[...]
        yz = y * (zv * _sigmoid(zv))
        half = SSM_W // 2
        yn = jnp.concatenate([yz[:, :half] * _rs(yz[:, :half]), yz[:, half:] * _rs(yz[:, half:])], axis=1)
        yn_ref[...] = (yn * sn_ref[...]).astype(BF16)

    return _call(
        body, name="ssd_fwd", grid=(NCH,),
        in_specs=[_rows(L, CONV_C), _rows(L, SSM_W), _rows(L, SSM_H), _const((4, CONV_C)), _const((1, CONV_C)), _const((1, SSM_H)),
                  _const((1, SSM_W)), _const((1, SSM_W)), _const((1, SSM_W)), _const((SSM_H, SSM_W))],
        out_specs=[_rows(L, SSM_W), _rows(L, SSM_W), pl.BlockSpec((1, 8, SSM_N, 128), lambda c: (c, 0, 0, 0))],
        out_shape=[jax.ShapeDtypeStruct((T, SSM_W), BF16), jax.ShapeDtypeStruct((T, SSM_W), F32),
                   jax.ShapeDtypeStruct((NCH, 8, SSM_N, 128), F32)],
        scratch=[pltpu.VMEM((8, CONV_C), F32), pltpu.VMEM((8, SSM_N, 128), F32)],
        args=[xbc, z, dtr, convw, convb, dtb, alx, dskx, ssmn, e], sem=("arbitrary",), duties=duties)


def _ssd_bwd(dmix, xbc, z, dtr, y, hs, convw, convb, dtb, alx, dskx, ssmn, e, e1, duties=()):
    rev = lambda i: (NCH - 1 - i, 0)

    def body(dyn_ref, u_ref, uh_ref, z_ref, dtr_ref, y_ref, hs_ref, cw_ref, cb_ref, dtb_ref, alx_ref, dsk_ref, sn_ref, e_ref, e1_ref,
             dxbc_ref, dz_ref, ddt_ref, dcw_ref, dcb_ref, dsn_ref, dpar_ref, dh, duh, colbuf):
        step = pl.program_id(0)
        c = NCH - 1 - step

        @pl.when(step == 0)
        def _():
            for r in (dh, duh, dcw_ref, dcb_ref, dsn_ref, dpar_ref):
                r[...] = jnp.zeros_like(r)

        u = u_ref[...]
        halo = jnp.where(c > 0, uh_ref[...], 0.0)
        cm = _ssd_chunk_common(u, halo, dtr_ref[...], cw_ref, cb_ref, dtb_ref, alx_ref, e_ref)
        sh1, sh2, sh3 = cm["sh"]
        xc, sg, act, pre_x, dt_x, a_x, tri, acs_x = (cm[k] for k in ("xc", "sg", "act", "pre_x", "dt_x", "a_x", "tri", "acs_x"))
        xs = act[:, :SSM_W]
        acs_l = acs_x[L - 1:L, :]
        lam_x = jnp.exp(acs_x)
        w_x = jnp.exp(acs_l - acs_x)
        gam_x = jnp.exp(acs_l)
        acs_t = acs_x.T
        xd = xs * dt_x
        xb = xd.astype(BF16)
        xdw = xd * w_x
        xw = xdw.astype(BF16)
        lo = lax.broadcasted_iota(jnp.int32, (1, 128), 1) < HD
        row8 = lax.broadcasted_iota(jnp.int32, (8, 1), 0)

        dyn = dyn_ref[...]
        yv = y_ref[...]
        zv = z_ref[...]
        sz = _sigmoid(zv)
        siluz = zv * sz
        yz = yv * siluz
        half = SSM_W // 2
        gy = dyn * sn_ref[...]
        dyz_parts, yzn_parts = [], []
        for hf in range(2):
            part = yz[:, hf * half:(hf + 1) * half]
            r = _rs(part)
            pn = part * r
            gp = gy[:, hf * half:(hf + 1) * half]
            dyz_parts.append(r * (gp - pn * jnp.mean(gp * pn, axis=-1, keepdims=True)))
            yzn_parts.append(pn)
        dyz = jnp.concatenate(dyz_parts, axis=1)
        dsn_ref[...] += jnp.sum(dyn * jnp.concatenate(yzn_parts, axis=1), axis=0, keepdims=True)
        dy = dyz * siluz
        dz_ref[...] = dyz * yv * _dsilu(zv, sz)

        colbuf[...] = jnp.zeros_like(colbuf)
        dx_pieces, dacs_pieces, dacsl_pieces, db_pieces, dc_pieces = [], [], [], [], []
        for grp in range(2):
            bb = act[:, SSM_W + SSM_N * grp:SSM_W + SSM_N * (grp + 1)].astype(BF16)
            cb_ = act[:, SSM_W + 2 * SSM_N + SSM_N * grp:SSM_W + 2 * SSM_N + SSM_N * (grp + 1)].astype(BF16)
            cbm = _dot_nt(cb_, bb)
            dcbm = jnp.zeros((L, L), F32)
            dc_g = jnp.zeros((L, SSM_N), F32)
            db_g = jnp.zeros((L, SSM_N), F32)
            for jj in range(4):
                j = 4 * grp + jj
                sl = slice(128 * j, 128 * j + 128)
                dy2 = dy[:, sl]
                dy2b = dy2.astype(BF16)
                d0 = _decay(acs_x, acs_t, 2 * j, tri)
                d1 = _decay(acs_x, acs_t, 2 * j + 1, tri)
                m0 = cbm * d0
                m1 = cbm * d1
                x2 = xb[:, sl]
                hprev = hs_ref[0, j]
                hprevb = hprev.astype(BF16)
                dhn = dh[j]
                dhnb = dhn.astype(BF16)
                g2 = _dot(bb, dhnb)
                dx_pieces.append(jnp.where(lo, _dot_tn(m0.astype(BF16), dy2b), _dot_tn(m1.astype(BF16), dy2b)) + w_x[:, sl] * g2)
                zero = jnp.zeros_like(dy2b)
                dm0 = _dot_nt(jnp.where(lo, dy2b, zero), x2)
                dm1 = _dot_nt(jnp.where(lo, zero, dy2b), x2)
                dcbm = dcbm + dm0 * d0 + dm1 * d1
                e0 = dm0 * m0
                e1v = dm1 * m1
                colbuf[:, 2 * j:2 * j + 1] = jnp.sum(e0, axis=1, keepdims=True) - jnp.sum(e0.T, axis=1, keepdims=True)
                colbuf[:, 2 * j + 1:2 * j + 2] = jnp.sum(e1v, axis=1, keepdims=True) - jnp.sum(e1v.T, axis=1, keepdims=True)
                yoff = lam_x[:, sl] * _dot(cb_, hprevb)
                gxw = g2 * xdw[:, sl]
                dacs_pieces.append(dy2 * yoff - gxw)
                dacsl_pieces.append(jnp.sum(gxw, axis=0, keepdims=True) + gam_x[:, sl] * jnp.sum(dhn * hprev, axis=0, keepdims=True))
                dyl = (dy2 * lam_x[:, sl]).astype(BF16)
                dc_g = dc_g + _dot_nt(dyl, hprevb)
                db_g = db_g + _dot_nt(xw[:, sl], dhnb)
                dh[j] = gam_x[:, sl] * dhn + _dot_tn(cb_, dyl)
            dcbb = dcbm.astype(BF16)
            dc_pieces.append(dc_g + _dot(dcbb, bb))
            db_pieces.append(db_g + _dot_tn(dcbb, cb_))

        dxd = jnp.concatenate(dx_pieces, axis=1)
        rowi = lax.broadcasted_iota(jnp.int32, (L, 1), 0)
        dacs_x = (jnp.concatenate(dacs_pieces, axis=1) + _dot_hi(colbuf[...], e1_ref[...])
                  + jnp.where(rowi == L - 1, jnp.concatenate(dacsl_pieces, axis=1), 0.0))
        upper = lax.broadcasted_iota(jnp.int32, (L, L), 0) <= lax.broadcasted_iota(jnp.int32, (L, L), 1)
        dadt_x = _dot_hi(upper, dacs_x, a_is_01=True)
        ddt_x = dxd * xs + dadt_x * a_x
        ddtr = _dot_nt_hi(ddt_x * _sigmoid(pre_x), e_ref[...])
        ddt_ref[...] = jnp.zeros_like(ddt_ref)
        ddt_ref[:, 0:SSM_H] = ddtr
        dalx =jnp.sum(dadt_x * dt_x, axis=0, keepdims=True) * a_x
        ddskx = jnp.sum(dy * xs, axis=0, keepdims=True)
        par_x = jnp.where(row8 == 1, dalx, 0.0) + jnp.where(row8 == 2, ddskx, 0.0)
        dpar_ref[...] += _dot_nt_hi(par_x, e_ref[...]) + jnp.where(row8 == 0, jnp.sum(ddtr, axis=0, keepdims=True), 0.0)

        dxs = dxd * dt_x + dsk_ref[...] * dy
        dact = jnp.concatenate([dxs] + db_pieces + dc_pieces, axis=1)
        du = dact * _dsilu(xc, sg)
        dcb_ref[...] += jnp.sum(du, axis=0, keepdims=True)
        taps = (sh3, sh2, sh1, u)
        dcw = jnp.zeros((8, CONV_C), F32)
        for k in range(4):
            dcw = dcw + jnp.where(row8 == k, jnp.sum(du * taps[k], axis=0, keepdims=True), 0.0)
        dcw_ref[...] += dcw
        zext = jnp.concatenate([du, duh[...]], axis=0)
        f1, f2, f3 = (pltpu.roll(zext, L + 8 - m, 0)[:L] for m in (1, 2, 3))
        dxbc_ref[...] = cw_ref[3:4, :] * du + cw_ref[2:3, :] * f1 + cw_ref[1:2, :] * f2 + cw_ref[0:1, :] * f3
        duh[...] = du[:8, :]

    return _call(
        body, name="ssd_bwd", grid=(NCH,),
        in_specs=[pl.BlockSpec((L, SSM_W), lambda i: (NCH - 1 - i, 1)), pl.BlockSpec((L, CONV_C), rev),
                  pl.BlockSpec((8, CONV_C), lambda i: (jnp.maximum((NCH - 1 - i) * (L // 8) - 1, 0), 0)),
                  pl.BlockSpec((L, SSM_W), rev), pl.BlockSpec((L, SSM_H), rev), pl.BlockSpec((L, SSM_W), rev),
                  pl.BlockSpec((1, 8, SSM_N, 128), lambda i: (NCH - 1 - i, 0, 0, 0)),
                  _const((4, CONV_C)), _const((1, CONV_C)), _const((1, SSM_H)), _const((1, SSM_W)), _const((1, SSM_W)), _const((1, SSM_W)),
                  _const((SSM_H, SSM_W)), _const((128, SSM_W))],
        out_specs=[pl.BlockSpec((L, CONV_C), rev), pl.BlockSpec((L, SSM_W), rev), pl.BlockSpec((L, 128), rev),
                   _const((8, CONV_C)), _const((1, CONV_C)), _const((1, SSM_W)), _const((8, SSM_H))],
        out_shape=[jax.ShapeDtypeStruct((T, CONV_C), F32), jax.ShapeDtypeStruct((T, SSM_W), F32), jax.ShapeDtypeStruct((T, 128), F32),
                   jax.ShapeDtypeStruct((8, CONV_C), F32), jax.ShapeDtypeStruct((1, CONV_C), F32), jax.ShapeDtypeStruct((1, SSM_W), F32),
                   jax.ShapeDtypeStruct((8, SSM_H), F32)],
        scratch=[pltpu.VMEM((8, SSM_N, 128), F32), pltpu.VMEM((8, CONV_C), F32), pltpu.VMEM((L, 128), F32)],
        args=[dmix, xbc, xbc, z, dtr, y, hs, convw, convb, dtb, alx, dskx, ssmn, e, e1], sem=("arbitrary",), duties=duties)


def _loss_head(x3, target):
    tm = 512

    def body(x_ref, t_ref, dy_ref, ss_ref):
        @pl.when(pl.program_id(0) == 0)
        def _():
            ss_ref[...] = jnp.zeros_like(ss_ref)

        err = x_ref[...] - t_ref[...]
        dy_ref[...] = err * (1.0 / D)
        ss_ref[...] += jnp.sum(jnp.sum(err * err, axis=1, keepdims=True), axis=0, keepdims=True)

    return pl.pallas_call(
        body, name="loss_head", grid=(T // tm,),
        in_specs=[_rows(tm, D), _rows(tm, D)],
        out_specs=[_rows(tm, D), _const((1, 128))],
        out_shape=[jax.ShapeDtypeStruct((T, D), F32), jax.ShapeDtypeStruct((1, 128), F32)],
        compiler_params=_params(("arbitrary",)),
    )(x3, target)


def _adam_math(w, g, m, v):
    m = ADAM_B1 * m + (1.0 - ADAM_B1) * g
    v = ADAM_B2 * v + (1.0 - ADAM_B2) * (g * g)
    m_hat = m / (1.0 - ADAM_B1 ** ADAM_STEP)
    v_hat = v / (1.0 - ADAM_B2 ** ADAM_STEP)
    delta = -ADAM_LR * (m_hat / (jnp.sqrt(v_hat) + ADAM_EPS) + ADAM_WD * w)
    return delta, m, v


def _adamw(w, m, v, parts, name, after=()):
    rows, cols = w.shape
    tr = rows if rows <= 512 else 256
    assert rows % tr == 0
    n_parts = len(parts)

    def body(*refs):
        w_ref, m_ref, v_ref = refs[:3]
        p_refs = refs[3:3 + n_parts]
        g_ref, d_ref, nm_ref, nv_ref = refs[3 + n_parts + len(after):]
        g = p_refs[0][...].astype(F32)
        for p in p_refs[1:]:
            g = g + p[...].astype(F32)
        delta, nm, nv = _adam_math(w_ref[...], g, m_ref[...], v_ref[...])
        g_ref[...] = g
        d_ref[...] = delta
        nm_ref[...] = nm
        nv_ref[...] = nv

    spec = pl.BlockSpec((tr, cols), lambda i: (i, 0))
    return pl.pallas_call(
        body, name=name, grid=(rows // tr,),
        in_specs=[spec] * (3 + n_parts) + [ANY] * len(after), out_specs=[spec] * 4,
        out_shape=[jax.ShapeDtypeStruct((rows, cols), F32)] * 4,
        compiler_params=_params(("parallel",)),
    )(w, m, v, *parts, *after)


COL_TILE = 512


def _adamw_sharded(w, m, v, chip_sum, from_chips, other_chips, name, after=()):
    rows, cols = w.shape
    prow = chip_sum.shape[0]
    assert cols % COL_TILE == 0 and prow >= rows and chip_sum.shape[1] == cols

    def body(ids_ref, w_ref, m_ref, v_ref, s_ref, r1_ref, r2_ref, r3_ref, *rest):
        g_ref, d_ref, nm_ref, nv_ref = rest[len(after):]
        g = s_ref[...]
        for r in (r1_ref, r2_ref, r3_ref):
            g = g + r[0].astype(F32)
        g = g[:rows]
        delta, nm, nv = _adam_math(w_ref[...], g, m_ref[...], v_ref[...])
        g_ref[...] = g
        d_ref[...] = delta
        nm_ref[...] = nm
        nv_ref[...] = nv

    spec = pl.BlockSpec((rows, COL_TILE), lambda i, ids: (0, i))
    part = lambda k: pl.BlockSpec((1, prow, COL_TILE), lambda i, ids: (ids[k], 0, i))
    return pl.pallas_call(
        body, name=name,
        grid_spec=pltpu.PrefetchScalarGridSpec(
            num_scalar_prefetch=1, grid=(cols // COL_TILE,),
            in_specs=[spec, spec, spec, pl.BlockSpec((prow, COL_TILE), lambda i, ids: (0, i)), part(0), part(1), part(2)]
            + [ANY] * len(after), out_specs=[spec] * 4),
        out_shape=[jax.ShapeDtypeStruct((rows, cols), F32)] * 4,
        compiler_params=_params(("parallel",)),
    )(other_chips, w, m, v, chip_sum, from_chips, from_chips, from_chips, *after)


def _chip_sum(mine, recv, name):
    rows, cols = mine.shape[1:]

    def body(a_ref, b_ref, s_ref, sb_ref):
        s = a_ref[0] + b_ref[0].astype(F32)
        sb_ref[0] = s.astype(BF16)

        @pl.when(pl.program_id(0) == 2 * lax.axis_index("x") + lax.axis_index("y"))
        def _():
            s_ref[...] = s

    by_chip = pl.BlockSpec((1, rows, cols), lambda k: (k, 0, 0))
    return pl.pallas_call(
        body, name=name, grid=(N_DEV // 2,),
        in_specs=[by_chip, by_chip], out_specs=[_const((rows, cols)), by_chip],
        out_shape=[jax.ShapeDtypeStruct((rows, cols), F32), jax.ShapeDtypeStruct((N_DEV // 2, rows, cols), BF16)],
        compiler_params=_params(("arbitrary",)),
    )(mine, recv)


def _all_reduce_small(v, after=()):
    rows = v.shape[0]

    def body(v_ref, *rest):
        out_ref, gath, send_sems, recv_sems = rest[len(after):]
        x, y, c = _place()
        me, sibling = (x, y, c), (x, y, 1 - c)
        chips = [(1 - x, y), (x, 1 - y), (1 - x, 1 - y)]

        def blk(px, py, pc):
            return gath.at[pl.ds((4 * px + 2 * py + pc) * rows, rows), :]

        def copy(k, block, to, src=None):
            return pltpu.make_async_remote_copy(src_ref=blk(*block) if src is None else src, dst_ref=blk(*block),
                                                send_sem=send_sems.at[k], recv_sem=recv_sems.at[k], device_id=to, device_id_type=MESH)

        gath[pl.ds((4 * x + 2 * y + c) * rows, rows), :] = v_ref[...]
        first = [copy(0, me, sibling, src=v_ref)] + [copy(1 + j, me, (*chip, c), src=v_ref) for j, chip in enumerate(chips)]
        for cp in first:
            cp.start()
        passed = [copy(4 + j, (*chip, c), sibling) for j, chip in enumerate(chips)]
        for j, chip in enumerate(chips):
            copy(1 + j, (*chip, c), me).wait_recv()
            passed[j].start()
        copy(0, sibling, me).wait_recv()
        for j, chip in enumerate(chips):
            copy(4 + j, (*chip, 1 - c), me).wait_recv()
        for cp in first + passed:
            cp.wait_send()
        acc = gath[0:rows, :]
        for d in range(1, N_DEV):
            acc = acc + gath[d * rows:(d + 1) * rows, :]
        out_ref[...] = acc

    vm = pl.BlockSpec(memory_space=pltpu.VMEM)
    return pl.pallas_call(
        body, name="all_reduce_small",
        in_specs=[vm] + [ANY] * len(after), out_specs=vm,
        out_shape=jax.ShapeDtypeStruct(v.shape, F32),
        scratch_shapes=[pltpu.VMEM((N_DEV * rows, 128), F32), pltpu.SemaphoreType.DMA((7,)), pltpu.SemaphoreType.DMA((7,))],
    )(v, *after)


def _rope_tables(positions):
    inv_freq = ROPE_THETA ** (-jnp.arange(0, HD, 2, dtype=F32) / HD)
    ang = positions.reshape(T).astype(F32)[:, None] * inv_freq
    ang = jnp.concatenate([ang, ang, ang, ang], axis=-1)
    lo_half = (jnp.arange(128) % HD) < (HD // 2)
    return jnp.cos(ang), jnp.where(lo_half, -jnp.sin(ang), jnp.sin(ang))


def _selectors():
    lane = jnp.arange(QW)
    e = (lane[None, :] // HD == jnp.arange(SSM_H)[:, None]).astype(F32)
    e1 = ((lane[None, :] == HD * jnp.arange(128)[:, None]) & (jnp.arange(128)[:, None] < SSM_H)).astype(F32)
    src = jnp.arange(KVW)
    ex = ((lane[None, :] // (HD * (NQ // NKV)) == src[:, None] // HD) & (lane[None, :] % HD == src[:, None] % HD)).astype(F32)
    return e, e1, ex


WEIGHTS = ['ffn1_pre_norm', 'ffn1_w_gate', 'ffn1_w_up', 'ffn1_w_down', 'ffn1_post_norm', 'mix_pre_norm', 'w_in', 'conv_w', 'conv_b',
           'dt_bias', 'a_log', 'd_skip', 'ssm_norm', 'w_out', 'mix_post_norm', 'ffn2_pre_norm', 'ffn2_w_gate', 'ffn2_w_up',
           'ffn2_w_down', 'ffn2_post_norm']
COL_SHARDED = ['ffn1_w_gate', 'ffn1_w_up', 'ffn2_w_gate', 'ffn2_w_up', 'w_in']
ROW_SHARDED = ['ffn1_w_down', 'ffn2_w_down', 'w_out']
BIG = COL_SHARDED + ROW_SHARDED
FFN_BIG = COL_SHARDED[:4] + ROW_SHARDED[:2]
SMALL = ['ffn1_pre_norm', 'ffn1_post_norm', 'mix_pre_norm', 'conv_b', 'dt_bias', 'a_log', 'd_skip', 'ssm_norm', 'mix_post_norm',
         'ffn2_pre_norm', 'ffn2_post_norm']
FFN1 = ['ffn1_w_gate', 'ffn1_w_up', 'ffn1_w_down']
FFN2 = ['ffn2_w_gate', 'ffn2_w_up', 'ffn2_w_down']


def _wire_block(name, a):
    if name in FFN_BIG:
        return jnp.pad(a.astype(BF16), ((0, FSH - FSR), (0, 0)))
    if name == "w_in":
        return jnp.pad(a.astype(BF16), ((0, ISW - ISR), (0, 0)))
    return a if name == "conv_w" else a.astype(BF16)


def _whole_from_gathered(name, a):
    if name == "conv_w":
        return jnp.transpose(a, (1, 0, 2)).reshape(a.shape[1], -1)
    return a.reshape(-1, D)


def _step(x, positions, target, small, blocks=None, whole=None):
    dist = blocks is not None
    core = "mesh" if dist else 0
    w = dict(small)
    if whole:
        w.update(whole)

    def gather(names):
        return [_gather_duty([_wire_block(n, blocks[n]) for n in names])] if dist else []

    def put(names, results):
        if dist:
            for n, r in zip(names, results[0]):
                w[n] = _whole_from_gathered(n, r)

    g, sums, red = {}, {}, {}

    def swap(names):
        return [_swap_duty([g[n][1] for n in names])] if dist else []

    def chip_sums(names, from_sibling):
        if dist:
            for n, recv in zip(names, from_sibling):
                sums[n] = _chip_sum(g[n][0], recv, "chip_sum_" + n)

    def exchange(names):
        return [_exchange_duty([sums[n][1] for n in names])] if dist else []

    def reduced(names, from_chips):
        if dist:
            for n, recv in zip(names, from_chips):
                red[n] = (sums[n][0], recv)

    cos, sin_s = _rope_tables(positions)
    e, e1, exf = _selectors()
    bias = _attn_bias()
    alx = jnp.repeat(w["a_log"], HD, axis=1)
    dskx = jnp.repeat(w["d_skip"], HD, axis=1)

    if dist:
        put(FFN1, _comm_only(gather(FFN1), "gather_ffn1"))
    (x1, n1, a1, b1, hm1, h1), got = _ffn_fwd(x, w["ffn1_pre_norm"], w["ffn1_w_gate"], w["ffn1_w_up"], w["ffn1_w_down"],
                                              w["ffn1_post_norm"], "ffn1_fwd", gather(["w_in", "conv_w"]))
    put(["w_in", "conv_w"], got)
    (n2, q, kx, vx, xbc, z, dtr), got = _inproj_fwd(x1, w["mix_pre_norm"], w["w_in"], cos, sin_s, exf.astype(BF16), gather(["w_out"]))
    put(["w_out"], got)
    (attn, lse), got = _attn_fwd(q, kx, vx, bias, gather(FFN2[:2]))
    put(FFN2[:2], got)
    (yn, y, hs), got = _ssd_fwd(xbc, z, dtr, w["conv_w"], w["conv_b"], w["dt_bias"], alx, dskx, w["ssm_norm"], e, gather(FFN2[2:]))
    put(FFN2[2:], got)
    x2, h2 = _outproj_fwd(x1, attn, yn, w["w_out"], w["mix_post_norm"])
    (x3, n3, a3, b3, hm3, h3), _ = _ffn_fwd(x2, w["ffn2_pre_norm"], w["ffn2_w_gate"], w["ffn2_w_up"], w["ffn2_w_down"],
                                            w["ffn2_post_norm"], "ffn2_fwd")
    dx3, ss = _loss_head(x3, target)

    (dx2, da3, db3, dh3, g["ffn2_pre_norm"], g["ffn2_post_norm"]), _ = _ffn_bwd(
        dx3, x2, a3, b3, h3, w["ffn2_pre_norm"], w["ffn2_post_norm"], w["ffn2_w_gate"], w["ffn2_w_up"], w["ffn2_w_down"], "ffn2_bwd")
    g["ffn2_w_down"] = _matmul_tn(hm3, dh3, "ffn2_dwd", core)[0]
    g["ffn2_w_gate"] = _matmul_tn(da3, n3, "ffn2_dwg", core)[0]
    g["ffn2_w_up"] = _matmul_tn(db3, n3, "ffn2_dwu", core)[0]

    (dh2, dmix, g["mix_post_norm"]), got = _outproj_bwd(dx2, h2, w["mix_post_norm"], w["w_out"], swap(FFN2))
    chip_sums(FFN2, got[0] if dist else None)
    g["w_out"] = _dwout(attn, yn, dh2, core)
    (dq, dkx, dvx), got = _attn_bwd(q, kx, vx, attn, dmix, lse, bias, exchange(FFN2) + swap(["w_out"]))
    if dist:
        reduced(FFN2, got[0])
        chip_sums(["w_out"], got[1])
    (dxbc, dz, ddt, dcw, g["conv_b"], g["ssm_norm"], dpar), got = _ssd_bwd(
        dmix, xbc, z, dtr, y, hs, w["conv_w"], w["conv_b"], w["dt_bias"], alx, dskx, w["ssm_norm"], e, e1, exchange(["w_out"]))
    reduced(["w_out"], got[0] if dist else None)
    g["conv_w"] = dcw[0:4]
    g["dt_bias"], g["a_log"], g["d_skip"] = dpar[0:1], dpar[1:2], dpar[2:3]
    dx1, dproj, g["mix_pre_norm"] = _inproj_bwd(dx2, dq, dkx, dvx, dxbc, dz, ddt, x1, w["mix_pre_norm"], w["w_in"], cos, sin_s, exf)
    g["w_in"] = _matmul_tn(dproj, n2, "dwin", core)[0]

    started = {}

    def start(n):
        started[n] = _exchange_start(sums[n][1], "start_exchange_" + n)
        return [started[n]["token"]]

    after = []
    if dist:
        chip_sums(["w_in"], _comm_only(swap(["w_in"]), "swap_w_in")[0])
        after = start("w_in")
    (dx0, da1, db1, dh1, g["ffn1_pre_norm"], g["ffn1_post_norm"]), _ = _ffn_bwd(
        dx1, x, a1, b1, h1, w["ffn1_pre_norm"], w["ffn1_post_norm"], w["ffn1_w_gate"], w["ffn1_w_up"], w["ffn1_w_down"], "ffn1_bwd",
        after)
    total = None
    if dist:
        flat = jnp.concatenate([g[n].reshape(-1) for n in SMALL] + [g["conv_w"].reshape(-1), ss[0, 0:1]])
        rows = -(-flat.shape[0] // 128 // 8) * 8
        total = _all_reduce_small(jnp.pad(flat, (0, rows * 128 - flat.shape[0])).reshape(rows, 128), after)
        after = [total]
    g["ffn1_w_down"], _ = _matmul_tn(hm1, dh1, "ffn1_dwd", core, after=after)
    g["ffn1_w_gate"], got = _matmul_tn(da1, n1, "ffn1_dwg", core, duties=swap(["ffn1_w_down"]))
    if dist:
        chip_sums(["ffn1_w_down"], got[0])
        after = start("ffn1_w_down")
    g["ffn1_w_up"], got = _matmul_tn(db1, n1, "ffn1_dwu", core, after=after, duties=swap(["ffn1_w_gate"]))
    if dist:
        chip_sums(["ffn1_w_gate"], got[0])
        after = start("ffn1_w_gate")
        chip_sums(["ffn1_w_up"], _comm_only(swap(["ffn1_w_up"]), "swap_ffn1_w_up", after=after)[0])
        start("ffn1_w_up")
    return ss, dx0, g, red, {n: (sums[n][0], started[n]) for n in started}, total


def kernel(x, positions, ffn1_pre_norm, ffn1_w_gate, ffn1_w_up, ffn1_w_down, ffn1_post_norm, mix_pre_norm, w_in, conv_w, conv_b, dt_bias, a_log, d_skip, ssm_norm, w_out, mix_post_norm, ffn2_pre_norm, ffn2_w_gate, ffn2_w_up, ffn2_w_down, ffn2_post_norm, loss_target, m_ffn1_pre_norm, m_ffn1_w_gate, m_ffn1_w_up, m_ffn1_w_down, m_ffn1_post_norm, m_mix_pre_norm, m_w_in, m_conv_w, m_conv_b, m_dt_bias, m_a_log, m_d_skip, m_ssm_norm, m_w_out, m_mix_post_norm, m_ffn2_pre_norm, m_ffn2_w_gate, m_ffn2_w_up, m_ffn2_w_down, m_ffn2_post_norm, v_ffn1_pre_norm, v_ffn1_w_gate, v_ffn1_w_up, v_ffn1_w_down, v_ffn1_post_norm, v_mix_pre_norm, v_w_in, v_conv_w, v_conv_b, v_dt_bias, v_a_log, v_d_skip, v_ssm_norm, v_w_out, v_mix_post_norm, v_ffn2_pre_norm, v_ffn2_w_gate, v_ffn2_w_up, v_ffn2_w_down, v_ffn2_post_norm):
    given = dict(locals())
    drop = lambda n, a: a if n in SMALL else (a[0].T if n in COL_SHARDED else a[0])
    w = {n: drop(n, given[n]) for n in WEIGHTS}
    m = {n: drop(n, given["m_" + n]) for n in WEIGHTS}
    v = {n: drop(n, given["v_" + n]) for n in WEIGHTS}
    cx, cy, cc = _place()
    others = [2 * (1 - cx) + cy, 2 * cx + (1 - cy), 2 * (1 - cx) + (1 - cy)]

    _, grad_x, g, red, pending, total = _step(x[0], positions, loss_target[0], {n: w[n] for n in SMALL},
                                              blocks={n: w[n] for n in BIG + ["conv_w"]})
    total = total.reshape(-1)
    chip_ids = jnp.stack(others).astype(jnp.int32)
    out_g, out_d, out_m, out_v = {}, {}, {}, {}
    last_start = [pending["ffn1_w_up"][1]["token"]]
    for n in BIG:
        if n not in pending:
            out_g[n], out_d[n], out_m[n], out_v[n] = _adamw_sharded(w[n], m[n], v[n], red[n][0], red[n][1], chip_ids, "adamw_" + n,
                                                                    last_start)

    rows = total.shape[0] // 128
    sizes = [w[n].size for n in SMALL]
    offs = [0]
    for s_ in sizes:
        offs.append(offs[-1] + s_)
    gs = {n: total[offs[i]:offs[i + 1]].reshape(w[n].shape) for i, n in enumerate(SMALL)}
    gcw = total[offs[-1]:offs[-1] + 4 * CONV_C].reshape(4, CONV_C)
    loss = 0.5 * total[offs[-1] + 4 * CONV_C] / D
    gs["conv_w"] = lax.dynamic_slice_in_dim(gcw, (4 * cx + 2 * cy + cc) * (CONV_C // N_DEV), CONV_C // N_DEV, axis=1)
    names = SMALL + ["conv_w"]

    def pack(d):
        flat = jnp.concatenate([d[n].reshape(-1) for n in names])
        return jnp.pad(flat, (0, rows * 128 - flat.shape[0])).reshape(rows, 128)

    pg, pd, pm, pv = _adamw(pack(w), pack(m), pack(v), [pack(gs)], "adamw_small", last_start)
    done = [pg] + [out_v[n] for n in BIG if n not in pending]
    for n, (chip_sum, started) in pending.items():
        recv = _exchange_wait(started, done, "wait_exchange_" + n)
        out_g[n], out_d[n], out_m[n], out_v[n] = _adamw_sharded(w[n], m[n], v[n], chip_sum, recv, chip_ids, "adamw_" + n)
        done = [out_v[n]]
    o2 = [0]
    for n in names:
        o2.append(o2[-1] + w[n].size)
    for i, n in enumerate(names):
        for dst, src in ((out_g, pg), (out_d, pd), (out_m, pm), (out_v, pv)):
            dst[n] = src.reshape(-1)[o2[i]:o2[i + 1]].reshape(w[n].shape)

    outs = [loss, grad_x[None]]
    for d in (out_g, out_d, out_m, out_v):
        outs += [d[n] if n in SMALL else (d[n].T[None] if n in COL_SHARDED else d[n][None]) for n in WEIGHTS]
    return tuple(outs)
```

```python
import functools
import math

import jax
import jax.numpy as jnp
from jax import lax
from jax.experimental import pallas as pl
from jax.experimental.pallas import tpu as pltpu

F32 = jnp.float32
BF16 = jnp.bfloat16
MESH = pl.DeviceIdType.MESH

N_DEV = 8
T = 2048
D = 1024
FF = 2816
FSR = FF // N_DEV
FSH = 384
FFP = N_DEV * FSH
HD = 64
NQ = 16
NKV = 4
QW = NQ * HD
KVW = NKV * HD
SSM_W = 1024
SSM_H = 16
SSM_N = 128
CONV_C = SSM_W + 2 * 2 * SSM_N
IN_COLS = 4112
INP = 4224
ISR = IN_COLS // N_DEV
ISW = 528
ISG = 640
L = 128
NCH = T // L
AB = 256
NAB = T // AB
EPS = 1e-6
NEG = -1e30
ROPE_THETA = 10000.0
DILATIONS = ((128, 1), (512, 4), (2048, 16))

ADAM_LR = 0.001
ADAM_B1 = 0.9
ADAM_B2 = 0.999
ADAM_EPS = 1e-08
ADAM_WD = 0.01
ADAM_STEP = 10

VMEM_LIMIT = 58 * 1024 * 1024


def _params(sem, vmem=VMEM_LIMIT):
    return pltpu.CompilerParams(dimension_semantics=sem, vmem_limit_bytes=vmem)


def _dot(a, b):
    return jnp.dot(a, b, preferred_element_type=F32)


def _dot_nt(a, b):
    return lax.dot_general(a, b, (((1,), (1,)), ((), ())), preferred_element_type=F32)


def _dot_tn(a, b):
    return lax.dot_general(a, b, (((0,), (0,)), ((), ())), preferred_element_type=F32)


def _split3(x):
    hi = x.astype(BF16)
    r1 = x - hi.astype(F32)
    mid = r1.astype(BF16)
    lo = (r1 - mid.astype(F32)).astype(BF16)
    return hi, mid, lo


def _dot_hi(a, b, a_is_01=False):
    if a_is_01:
        sel = a.astype(BF16)
        return sum(_dot(sel, p) for p in _split3(b))
    sel = b.astype(BF16)
    return sum(_dot(p, sel) for p in _split3(a))


def _dot_nt_hi(a, b):
    sel = b.astype(BF16)
    return sum(_dot_nt(p, sel) for p in _split3(a))


def _rs(x):
    return lax.rsqrt(jnp.mean(x * x, axis=-1, keepdims=True) + EPS)


def _sigmoid(x):
    return jax.nn.sigmoid(x)


def _dsilu(x, s):
    return s * (1.0 + x * (1.0 - s))


def _resident(shape):
    nd = len(shape)
    return pl.BlockSpec(shape, lambda *_: (0,) * nd, pipeline_mode=pl.Buffered(1))


def _const(shape):
    nd = len(shape)
    return pl.BlockSpec(shape, lambda *_: (0,) * nd)


def _rows(tm, cols):
    return pl.BlockSpec((tm, cols), lambda i: (i, 0))


ANY = pl.BlockSpec(memory_space=pl.ANY)


def _place():
    return lax.axis_index("x"), lax.axis_index("y"), lax.axis_index("c")


def _gather_duty(arrays):
    n = len(arrays)
    results = [jax.ShapeDtypeStruct((N_DEV,) + a.shape, a.dtype) for a in arrays]

    def make(ins, outs, send_sems, recv_sems, local_sems):
        x, y, c = _place()
        me, sibling = (x, y, c), (x, y, 1 - c)
        chips = [(1 - x, y), (x, 1 - y), (1 - x, 1 - y)]

        def place_of(a, px, py, pc):
            return outs[a].at[4 * px + 2 * py + pc]

        def copy(a, k, block, to, src=None):
            dst = place_of(a, *block)
            return pltpu.make_async_remote_copy(src_ref=dst if src is None else src, dst_ref=dst,
                                                send_sem=send_sems.at[7 * a + k], recv_sem=recv_sems.at[7 * a + k],
                                                device_id=to, device_id_type=MESH)

        def own(a):
            return pltpu.make_async_copy(ins[a], place_of(a, *me), local_sems.at[a])

        def first(a):
            return [copy(a, 0, me, sibling, src=ins[a])] + [copy(a, 1 + j, me, (*chip, c), src=ins[a]) for j, chip in enumerate(chips)]

        def start():
            for a in range(n):
                own(a).start()
            for a in range(n):
                for cp in first(a):
                    cp.start()

        def finish():
            for j, chip in enumerate(chips):
                for a in range(n):
                    copy(a, 1 + j, (*chip, c), me).wait_recv()
                    copy(a, 4 + j, (*chip, c), sibling).start()
            for a in range(n):
                copy(a, 0, sibling, me).wait_recv()
                for j, chip in enumerate(chips):
                    copy(a, 4 + j, (*chip, 1 - c), me).wait_recv()
            for a in range(n):
                for cp in first(a) + [copy(a, 4 + j, (*chip, c), sibling) for j, chip in enumerate(chips)]:
                    cp.wait_send()
                own(a).wait()

        return start, finish

    return dict(operands=list(arrays), results=results, sems=(7 * n, 7 * n, n), make=make)


def _swap_duty(arrays):
    n = len(arrays)
    half = N_DEV // 2
    results = [jax.ShapeDtypeStruct(a.shape, a.dtype) for a in arrays]

    def make(ins, outs, send_sems, recv_sems):
        x, y, c = _place()

        def copies():
            return [pltpu.make_async_remote_copy(src_ref=ins[a].at[k], dst_ref=outs[a].at[k],
                                                 send_sem=send_sems.at[half * a + k], recv_sem=recv_sems.at[half * a + k],
                                                 device_id=(x, y, 1 - c), device_id_type=MESH)
                    for a in range(n) for k in range(half)]

        def start():
            for cp in copies():
                cp.start()

        def finish():
            for cp in copies():
                cp.wait()

        return start, finish

    return dict(operands=list(arrays), results=results, sems=(half * n, half * n), make=make)


def _exchange_duty(arrays):
    n = len(arrays)
    results = [jax.ShapeDtypeStruct(a.shape, a.dtype) for a in arrays]

    def make(ins, outs, send_sems, recv_sems):
        x, y, c = _place()
        chips = [(1 - x, y), (x, 1 - y), (1 - x, 1 - y)]
        my_chip = 2 * x + y

        def sends():
            return [pltpu.make_async_remote_copy(src_ref=ins[a].at[2 * px + py], dst_ref=outs[a].at[my_chip],
                                                 send_sem=send_sems.at[3 * a + j], recv_sem=recv_sems.at[3 * a + j],
                                                 device_id=(px, py, c), device_id_type=MESH)
                    for a in range(n) for j, (px, py) in enumerate(chips)]

        def start():
            for cp in sends():
                cp.start()

        def finish():
            for a in range(n):
                for j, (px, py) in enumerate(chips):
                    pltpu.make_async_remote_copy(src_ref=ins[a].at[my_chip], dst_ref=outs[a].at[2 * px + py],
                                                 send_sem=send_sems.at[3 * a + j], recv_sem=recv_sems.at[3 * a + j],
                                                 device_id=(px, py, c), device_id_type=MESH).wait_recv()
            for cp in sends():
                cp.wait_send()

        return start, finish

    return dict(operands=list(arrays), results=results, sems=(3 * n, 3 * n), make=make)


def _call(body, *, name, grid, in_specs, out_specs, out_shape, args, sem, scratch=(), duties=(), after=()):
    n_in, n_out, n_scr = len(in_specs), len(out_specs), len(scratch)
    sem_shapes = [pltpu.SemaphoreType.DMA((k,)) for d in duties for k in d["sems"]]

    def full(*refs):
        pos = [0]

        def take(k):
            pos[0] += k
            return refs[pos[0] - k:pos[0]]

        ins = take(n_in)
        d_ins = [take(len(d["operands"])) for d in duties]
        take(len(after))
        outs = take(n_out)
        d_outs = [take(len(d["results"])) for d in duties]
        scr = take(n_scr)
        d_sems = [take(len(d["sems"])) for d in duties]
        hooks = [d["make"](di, do, *ds) for d, di, do, ds in zip(duties, d_ins, d_outs, d_sems)]
        if grid and hooks:
            ids = [pl.program_id(k) for k in range(len(grid))]
            first = functools.reduce(jnp.logical_and, [i == 0 for i in ids])
            last = functools.reduce(jnp.logical_and, [i == g - 1 for i, g in zip(ids, grid)])

            @pl.when(first)
            def _():
                for start, _ in hooks:
                    start()

            body(*ins, *outs, *scr)

            @pl.when(last)
            def _():
                for _, finish in hooks:
                    finish()
        else:
            for start, _ in hooks:
                start()
            body(*ins, *outs, *scr)
            for _, finish in hooks:
                finish()

    d_args = [a for d in duties for a in d["operands"]]
    d_res = [r for d in duties for r in d["results"]]
    kwargs = dict(grid=grid) if grid else {}
    res = pl.pallas_call(
        full, name=name, in_specs=list(in_specs) + [ANY] * (len(d_args) + len(after)), out_specs=list(out_specs) + [ANY] * len(d_res),
        out_shape=list(out_shape) + d_res, scratch_shapes=list(scratch) + sem_shapes,
        compiler_params=_params(sem) if grid else None, **kwargs,
    )(*args, *d_args, *after)
    own, rest = list(res[:n_out]), list(res[n_out:])
    by_duty = []
    for d in duties:
        by_duty.append(rest[:len(d["results"])])
        rest = rest[len(d["results"]):]
    return own, by_duty


def _comm_only(duties, name, after=()):
    return _call(lambda: None, name=name, grid=None, in_specs=[], out_specs=[], out_shape=[], args=[], sem=None, duties=duties,
                 after=after)[1]


HBM = pl.BlockSpec(memory_space=pltpu.HBM)
SEMS = pl.BlockSpec(memory_space=pltpu.SEMAPHORE)
SIDE_EFFECT = pltpu.SideEffectType.DATAFLOW_SIDE_EFFECTING
N_OTHER_CHIPS = 3


def _chip_copies(src_ref, land_ref, sems):
    x, y, c = _place()
    chips = [(1 - x, y), (x, 1 - y), (1 - x, 1 - y)]
    return [pltpu.make_async_remote_copy(src_ref=src_ref.at[2 * px + py], dst_ref=land_ref.at[2 * x + y],
                                         send_sem=sems[j], recv_sem=sems[N_OTHER_CHIPS + j], device_id=(px, py, c), device_id_type=MESH)
            for j, (px, py) in enumerate(chips)]


def _exchange_start(pb, name):
    n_sem = 2 * N_OTHER_CHIPS

    def body(pb_ref, land_ref, *rest):
        for cp in _chip_copies(pb_ref, land_ref, rest[:n_sem]):
            cp.start()
        token = rest[n_sem + 2]
        token[...] = jnp.zeros_like(token)

    res = pl.pallas_call(
        body, name=name,
        out_shape=(pltpu.SemaphoreType.DMA(()),) * n_sem + (pltpu.HBM(pb.shape, pb.dtype), pltpu.HBM(pb.shape, pb.dtype),
                                                              jax.ShapeDtypeStruct((8, 128), F32)),
        in_specs=(HBM, HBM), out_specs=(SEMS,) * n_sem + (HBM, HBM, pl.BlockSpec(memory_space=pltpu.VMEM)),
        input_output_aliases={0: n_sem, 1: n_sem + 1},
        compiler_params=pltpu.CompilerParams(has_side_effects=SIDE_EFFECT),
    )(pltpu.with_memory_space_constraint(pb, pltpu.HBM), pltpu.with_memory_space_constraint(lax.empty(pb.shape, pb.dtype), pltpu.HBM))
    return dict(sems=res[:n_sem], src=res[n_sem], land=res[n_sem + 1], token=res[n_sem + 2])


def _exchange_wait(started, after, name):
    n_sem = 2 * N_OTHER_CHIPS

    def body(pb_ref, land_ref, *rest):
        for cp in _chip_copies(pb_ref, land_ref, rest[:n_sem]):
            cp.wait_send()
            cp.wait_recv()

    src, land = started["src"], started["land"]
    return pl.pallas_call(
        body, name=name, out_shape=(pltpu.HBM(src.shape, src.dtype), pltpu.HBM(land.shape, land.dtype)),
        in_specs=(HBM, HBM) + (SEMS,) * n_sem + (ANY,) * len(after), out_specs=(HBM, HBM), input_output_aliases={0: 0, 1: 1},
        compiler_params=pltpu.CompilerParams(has_side_effects=SIDE_EFFECT),
    )(src, land, *started["sems"], *after)[1]


def _ffn_fwd(x, gpre, wg, wu, wd, gpost, name, duties=()):
    tm = 256

    def body(x_ref, gpre_ref, wg_ref, wu_ref, wd_ref, gpost_ref, xo_ref, n_ref, a_ref, b_ref, hm_ref, h_ref):
        xv = x_ref[...]
        n = (xv * _rs(xv) * gpre_ref[...]).astype(BF16)
        a = _dot_nt(n, wg_ref[...])
        b = _dot_nt(n, wu_ref[...])
        hm = (a * _sigmoid(a) * b).astype(BF16)
        h = _dot(hm, wd_ref[...])
        xo_ref[...] = xv + 0.5 * (h * _rs(h) * gpost_ref[...])
        n_ref[...] = n
        a_ref[...] = a.astype(BF16)
        b_ref[...] = b.astype(BF16)
        hm_ref[...] = hm
        h_ref[...] = h

    return _call(
        body, name=name, grid=(T // tm,),
        in_specs=[_rows(tm, D), _const((1, D)), _resident((FFP, D)), _resident((FFP, D)), _resident((FFP, D)), _const((1, D))],
        out_specs=[_rows(tm, D), _rows(tm, D), _rows(tm, FFP), _rows(tm, FFP), _rows(tm, FFP), _rows(tm, D)],
        out_shape=[jax.ShapeDtypeStruct((T, D), F32), jax.ShapeDtypeStruct((T, D), BF16), jax.ShapeDtypeStruct((T, FFP), BF16),
                   jax.ShapeDtypeStruct((T, FFP), BF16), jax.ShapeDtypeStruct((T, FFP), BF16), jax.ShapeDtypeStruct((T, D), F32)],
        args=[x, gpre, wg, wu, wd, gpost], sem=("arbitrary",), duties=duties)


def _ffn_bwd(dxo, x, a, b, h, gpre, gpost, wg, wu, wd, name, after=()):
    tm = 256

    def body(dxo_ref, x_ref, a_ref, b_ref, h_ref, gpre_ref, gpost_ref, wg_ref, wu_ref, wd_ref,
             dx_ref, da_ref, db_ref, dh_ref, dgpre_ref, dgpost_ref):
        @pl.when(pl.program_id(0) == 0)
        def _():
            dgpre_ref[...] = jnp.zeros_like(dgpre_ref)
            dgpost_ref[...] = jnp.zeros_like(dgpost_ref)

        dy = dxo_ref[...]
        h = h_ref[...]
        hn = h * _rs(h)
        r2 = _rs(h)
        dgpost_ref[...] += jnp.sum(0.5 * dy * hn, axis=0, keepdims=True)
        gdy = 0.5 * dy * gpost_ref[...]
        dh = r2 * (gdy - hn * jnp.mean(gdy * hn, axis=-1, keepdims=True))
        dhb = dh.astype(BF16)
        dh_ref[...] = dhb
        dhm = _dot_nt(dhb, wd_ref[...])
        av = a_ref[...].astype(F32)
        bv = b_ref[...].astype(F32)
        sg = _sigmoid(av)
        db = (dhm * (av * sg)).astype(BF16)
        da = (dhm * bv * _dsilu(av, sg)).astype(BF16)
        da_ref[...] = da
        db_ref[...] = db
        dn = _dot(da, wg_ref[...]) + _dot(db, wu_ref[...])
        xv = x_ref[...]
        r = _rs(xv)
        xn = xv * r
        dgpre_ref[...] += jnp.sum(dn * xn, axis=0, keepdims=True)
        gdn = dn * gpre_ref[...]
        dx_ref[...] = dy + r * (gdn - xn * jnp.mean(gdn * xn, axis=-1, keepdims=True))

    return _call(
        body, name=name, grid=(T // tm,),
        in_specs=[_rows(tm, D), _rows(tm, D), _rows(tm, FFP), _rows(tm, FFP), _rows(tm, D), _const((1, D)), _const((1, D)),
                  _resident((FFP, D)), _resident((FFP, D)), _resident((FFP, D))],
        out_specs=[_rows(tm, D), _rows(tm, FFP), _rows(tm, FFP), _rows(tm, D), _const((1, D)), _const((1, D))],
        out_shape=[jax.ShapeDtypeStruct((T, D), F32), jax.ShapeDtypeStruct((T, FFP), BF16), jax.ShapeDtypeStruct((T, FFP), BF16),
                   jax.ShapeDtypeStruct((T, D), BF16), jax.ShapeDtypeStruct((1, D), F32), jax.ShapeDtypeStruct((1, D), F32)],
        args=[dxo, x, a, b, h, gpre, gpost, wg, wu, wd], sem=("arbitrary",), after=after)


def _core_index(core):
    return lax.axis_index("c") if core == "mesh" else core


def _by_core(put32, put16, step, core):
    mine = (step % 2) == _core_index(core)
    pl.when(mine)(put32)
    pl.when(jnp.logical_not(mine))(put16)


def _matmul_tn(a, b, name, core, after=(), duties=()):
    k, m = a.shape
    n = b.shape[1]
    r = m // N_DEV
    assert m == N_DEV * r and r % 128 == 0

    def body(a_ref, b_ref, o_ref, ob_ref):
        res = _dot_tn(a_ref[...], b_ref[...])

        def put32():
            o_ref[0] = res

        def put16():
            ob_ref[0] = res.astype(BF16)

        _by_core(put32, put16, pl.program_id(0), core)

    spec = pl.BlockSpec((1, r, n), lambda i: (i // 2, 0, 0))
    return _call(body, name=name, grid=(N_DEV,), in_specs=[pl.BlockSpec((k, r), lambda i: (0, i)), _resident((k, n))],
                 out_specs=[spec, spec],
                 out_shape=[jax.ShapeDtypeStruct((N_DEV // 2, r, n), F32), jax.ShapeDtypeStruct((N_DEV // 2, r, n), BF16)],
                 args=[a, b], sem=("arbitrary",), duties=duties, after=after)


def _dwout(attn, yn, dh2, core):
    rs = (QW + SSM_W) // N_DEV
    half = N_DEV // 2

    def body(at_ref, yn_ref, dh_ref, o_ref, ob_ref):
        i = pl.program_id(0)

        def put(r):
            def put32():
                o_ref[0] = r

            def put16():
                ob_ref[0] = r.astype(BF16)

            _by_core(put32, put16, i, core)

        @pl.when(i < half)
        def _():
            put(_dot_tn(at_ref[...], dh_ref[...]))

        @pl.when(i >= half)
        def _():
            put(_dot_tn(yn_ref[...], dh_ref[...]))

    spec = pl.BlockSpec((1, rs, D), lambda i: (i // 2, 0, 0))
    return pl.pallas_call(
        body, name="dwout", grid=(N_DEV,),
        in_specs=[pl.BlockSpec((T, rs), lambda i: (0, jnp.minimum(i, half - 1))),
                  pl.BlockSpec((T, rs), lambda i: (0, jnp.maximum(i - half, 0))), _resident((T, D))],
        out_specs=[spec, spec],
        out_shape=[jax.ShapeDtypeStruct((half, rs, D), F32), jax.ShapeDtypeStruct((half, rs, D), BF16)],
        compiler_params=_params(("arbitrary",)),
    )(attn, yn, dh2)


def _rope_swap(t, lo_half):
    return jnp.where(lo_half, pltpu.roll(t, 96, 1), pltpu.roll(t, 32, 1))


def _inproj_fwd(x1, gpre, win, cos, sin_s, ex, duties=()):
    tm = 256

    def body(x_ref, g_ref, w_ref, cos_ref, sin_ref, ex_ref, n_ref, q_ref, kx_ref, vx_ref, xbc_ref, z_ref, dt_ref):
        xv = x_ref[...]
        n = (xv * _rs(xv) * g_ref[...]).astype(BF16)
        n_ref[...] = n
        by_dev = _dot_nt(n, w_ref[...])
        proj = jnp.concatenate([by_dev[:, ISW * d:ISW * d + ISR] for d in range(N_DEV)], axis=1)
        cs = cos_ref[...]
        sn = sin_ref[...]
        lo_half = (lax.broadcasted_iota(jnp.int32, (1, 128), 1) % HD) < (HD // 2)

        def rope(t):
            return t * cs + _rope_swap(t, lo_half) * sn

        for j in range(QW // 128):
            t = proj[:, 128 * j:128 * j + 128]
            q_ref[:, 128 * j:128 * j + 128] = (rope(t) * (HD ** -0.5)).astype(BF16)
        k = jnp.concatenate([rope(proj[:, QW + 128 * j:QW + 128 * j + 128]) for j in range(KVW // 128)], axis=1)
        v = proj[:, QW + KVW:QW + 2 * KVW]
        kx_ref[...] = _dot(k.astype(BF16), ex_ref[...]).astype(BF16)
        vx_ref[...] = _dot(v.astype(BF16), ex_ref[...]).astype(BF16)
        c0 = QW + 2 * KVW
        xbc_ref[...] = proj[:, c0:c0 + CONV_C]
        z_ref[...] = proj[:, c0 + CONV_C:c0 + CONV_C + SSM_W]
        dt_ref[...] = proj[:, c0 + CONV_C + SSM_W:IN_COLS]

    return _call(
        body, name="inproj_fwd", grid=(T // tm,),
        in_specs=[_rows(tm, D), _const((1, D)), _resident((INP, D)), _rows(tm, 128), _rows(tm, 128), _const((KVW, QW))],
        out_specs=[_rows(tm, D), _rows(tm, QW), _rows(tm, QW), _rows(tm, QW), _rows(tm, CONV_C), _rows(tm, SSM_W), _rows(tm, SSM_H)],
        out_shape=[jax.ShapeDtypeStruct((T, D), BF16), jax.ShapeDtypeStruct((T, QW), BF16), jax.ShapeDtypeStruct((T, QW), BF16),
                   jax.ShapeDtypeStruct((T, QW), BF16), jax.ShapeDtypeStruct((T, CONV_C), F32), jax.ShapeDtypeStruct((T, SSM_W), F32),
                   jax.ShapeDtypeStruct((T, SSM_H), F32)],
        args=[x1, gpre, win, cos, sin_s, ex], sem=("arbitrary",), duties=duties)


def _inproj_bwd(dres, dq, dkx, dvx, dxbc, dz, ddt, x1, gpre, win, cos, sin_s, exf):
    tm = 256

    def body(dres_ref, dq_ref, dkx_ref, dvx_ref, dxbc_ref, dz_ref, ddt_ref, x_ref, g_ref, w_ref, cos_ref, sin_ref, ex_ref,
             dx_ref, dps_ref, dg_ref, dp_ref):
        @pl.when(pl.program_id(0) == 0)
        def _():
            dg_ref[...] = jnp.zeros_like(dg_ref)

        cs = cos_ref[...]
        sn = sin_ref[...]
        lo_half = (lax.broadcasted_iota(jnp.int32, (1, 128), 1) % HD) < (HD // 2)

        def rope_t(t):
            return t * cs - _rope_swap(t, lo_half) * sn

        for j in range(QW // 128):
            dp_ref[:, 128 * j:128 * j + 128] = rope_t(dq_ref[:, 128 * j:128 * j + 128] * (HD ** -0.5)).astype(BF16)
        dk = _dot_nt_hi(dkx_ref[...], ex_ref[...])
        dv = _dot_nt_hi(dvx_ref[...], ex_ref[...])
        for j in range(KVW // 128):
            dp_ref[:, QW + 128 * j:QW + 128 * j + 128] = rope_t(dk[:, 128 * j:128 * j + 128]).astype(BF16)
        dp_ref[:, QW + KVW:QW + 2 * KVW] = dv.astype(BF16)
        c0 = QW + 2 * KVW
        dp_ref[:, c0:c0 + CONV_C] = dxbc_ref[...].astype(BF16)
        dp_ref[:, c0 + CONV_C:c0 + CONV_C + SSM_W] = dz_ref[...].astype(BF16)
        dp_ref[:, c0 + CONV_C + SSM_W:INP] = ddt_ref[...].astype(BF16)
        pieces = [dp_ref[:, ISR * d:ISR * (d + 1)] for d in range(N_DEV)]
        zw = jnp.zeros((tm, ISW - ISR), BF16)
        zg = jnp.zeros((tm, ISG - ISR), BF16)
        dn = _dot(jnp.concatenate([t for p in pieces for t in (p, zw)], axis=1), w_ref[...])
        for d in range(N_DEV):
            dps_ref[:, ISG * d:ISG * (d + 1)] = jnp.concatenate([pieces[d], zg], axis=1)
        xv = x_ref[...]
        r = _rs(xv)
        xn = xv * r
        dg_ref[...] += jnp.sum(dn * xn, axis=0, keepdims=True)
        gdn = dn * g_ref[...]
        dx_ref[...] = dres_ref[...] + r * (gdn - xn * jnp.mean(gdn * xn, axis=-1, keepdims=True))

    return pl.pallas_call(
        body, name="inproj_bwd", grid=(T // tm,),
        in_specs=[_rows(tm, D), _rows(tm, QW), _rows(tm, QW), _rows(tm, QW), _rows(tm, CONV_C), _rows(tm, SSM_W), _rows(tm, 128),
                  _rows(tm, D), _const((1, D)), _resident((INP, D)), _rows(tm, 128), _rows(tm, 128), _const((KVW, QW))],
        out_specs=[_rows(tm, D), _rows(tm, N_DEV * ISG), _const((1, D))],
        out_shape=[jax.ShapeDtypeStruct((T, D), F32), jax.ShapeDtypeStruct((T, N_DEV * ISG), BF16), jax.ShapeDtypeStruct((1, D), F32)],
        scratch_shapes=[pltpu.VMEM((tm, INP), BF16)],
        compiler_params=_params(("arbitrary",)),
    )(dres, dq, dkx, dvx, dxbc, dz, ddt, x1, gpre, win, cos, sin_s, exf)


def _outproj_fwd(x1, attn, yn, wout, gpost):
    tm = 256

    def body(x_ref, at_ref, yn_ref, w_ref, g_ref, xo_ref, h_ref):
        h = _dot(at_ref[...], w_ref[0:QW, :]) + _dot(yn_ref[...], w_ref[QW:QW + SSM_W, :])
        h_ref[...] = h
        xo_ref[...] = x_ref[...] + h * _rs(h) * g_ref[...]

    return pl.pallas_call(
        body, name="outproj_fwd", grid=(T // tm,),
        in_specs=[_rows(tm, D), _rows(tm, QW), _rows(tm, SSM_W), _resident((QW + SSM_W, D)), _const((1, D))],
        out_specs=[_rows(tm, D), _rows(tm, D)],
        out_shape=[jax.ShapeDtypeStruct((T, D), F32), jax.ShapeDtypeStruct((T, D), F32)],
        compiler_params=_params(("parallel",)),
    )(x1, attn, yn, wout, gpost)


def _outproj_bwd(dx2, h2, gpost, wout, duties=()):
    tm = 256

    def body(dy_ref, h_ref, g_ref, w_ref, dh_ref, dm_ref, dg_ref):
        @pl.when(pl.program_id(0) == 0)
        def _():
            dg_ref[...] = jnp.zeros_like(dg_ref)

        dy = dy_ref[...]
        h = h_ref[...]
        r = _rs(h)
        hn = h * r
        dg_ref[...] += jnp.sum(dy * hn, axis=0, keepdims=True)
        gdy = dy * g_ref[...]
        dh = (r * (gdy - hn * jnp.mean(gdy * hn, axis=-1, keepdims=True))).astype(BF16)
        dh_ref[...] = dh
        dm_ref[...] = _dot_nt(dh, w_ref[...])

    return _call(
        body, name="outproj_bwd", grid=(T // tm,),
        in_specs=[_rows(tm, D), _rows(tm, D), _const((1, D)), _resident((QW + SSM_W, D))],
        out_specs=[_rows(tm, D), _rows(tm, QW + SSM_W), _const((1, D))],
        out_shape=[jax.ShapeDtypeStruct((T, D), BF16), jax.ShapeDtypeStruct((T, QW + SSM_W), F32), jax.ShapeDtypeStruct((1, D), F32)],
        args=[dx2, h2, gpost, wout], sem=("arbitrary",), duties=duties)


def _attn_bias():
    d = jnp.arange(AB)[:, None] - jnp.arange(T)[None, :] + (T - AB)
    cnt = jnp.zeros(d.shape, F32)
    for window, dil in DILATIONS:
        cnt = cnt + ((d >= 0) & (d % dil == 0) & (d <= window)).astype(F32)
    return jnp.where(cnt > 0, jnp.log(jnp.maximum(cnt, 1.0)), NEG)


G_PER = NQ // NKV
WK = G_PER * HD


def _attn_fwd(q, kx, vx, bias, duties=()):
    def body(q_ref, kx_ref, vx_ref, bias_ref, o_ref, lse_ref):
        lane = lax.broadcasted_iota(jnp.int32, (1, WK), 1)
        lse_ref[...] = jnp.zeros_like(lse_ref)
        for i in range(NAB):
            n = (i + 1) * AB
            rows = slice(i * AB, n)
            qi = q_ref[rows, :]
            kxi = kx_ref[0:n, :]
            vxi = vx_ref[0:n, :]
            bb = bias_ref[:, (NAB - 1 - i) * AB:]
            o_acc = jnp.zeros((AB, WK), F32)
            for g in range(G_PER):
                mg = (lane // HD) == g
                s = _dot_nt(jnp.where(mg, qi, jnp.zeros_like(qi)), kxi) + bb
                m = jnp.max(s, axis=1, keepdims=True)
                p = jnp.exp(s - m)
                l = jnp.sum(p, axis=1, keepdims=True)
                o_acc = jnp.where(mg, _dot(p.astype(BF16), vxi) / l, o_acc)
                lse_ref[rows, g:g + 1] = m + jnp.log(l)
            o_ref[rows, :] = o_acc.astype(BF16)

    col = lambda kv: (0, kv)
    return _call(
        body, name="attn_fwd", grid=(NKV,),
        in_specs=[pl.BlockSpec((T, WK), col), pl.BlockSpec((T, WK), col), pl.BlockSpec((T, WK), col), _const((AB, T))],
        out_specs=[pl.BlockSpec((T, WK), col), pl.BlockSpec((T, 128), col)],
        out_shape=[jax.ShapeDtypeStruct((T, QW), BF16), jax.ShapeDtypeStruct((T, NKV * 128), F32)],
        args=[q, kx, vx, bias], sem=("arbitrary",), duties=duties)


def _attn_bwd(q, kx, vx, o, dmix, lse, bias, duties=()):
    def body(q_ref, kx_ref, vx_ref, o_ref, do_ref, lse_ref, bias_ref, dq_ref, dkx_ref, dvx_ref):
        lane = lax.broadcasted_iota(jnp.int32, (1, WK), 1)
        dkx_ref[...] = jnp.zeros_like(dkx_ref)
        dvx_ref[...] = jnp.zeros_like(dvx_ref)
        for i in range(NAB):
            n = (i + 1) * AB
            rows = slice(i * AB, n)
            qi = q_ref[rows, :]
            dof = do_ref[rows, :]
            doi = dof.astype(BF16)
            prod = dof * o_ref[rows, :].astype(F32)
            kxi = kx_ref[0:n, :]
            vxi = vx_ref[0:n, :]
            bb = bias_ref[:, (NAB - 1 - i) * AB:]
            dq_acc = jnp.zeros((AB, WK), F32)
            for g in range(G_PER):
                mg = (lane // HD) == g
                qm = jnp.where(mg, qi, jnp.zeros_like(qi))
                dom = jnp.where(mg, doi, jnp.zeros_like(doi))
                delta = jnp.sum(jnp.where(mg, prod, 0.0), axis=1, keepdims=True)
                p = jnp.exp(_dot_nt(qm, kxi) + bb - lse_ref[rows, g:g + 1])
                ds = (p * (_dot_nt(dom, vxi) - delta)).astype(BF16)
                dvx_ref[0:n, :] += _dot_tn(p.astype(BF16), dom)
                dkx_ref[0:n, :] += _dot_tn(ds, qm)
                dq_acc = jnp.where(mg, _dot(ds, kxi), dq_acc)
            dq_ref[rows, :] = dq_acc

    col = lambda kv: (0, kv)
    return _call(
        body, name="attn_bwd", grid=(NKV,),
        in_specs=[pl.BlockSpec((T, WK), col), pl.BlockSpec((T, WK), col), pl.BlockSpec((T, WK), col), pl.BlockSpec((T, WK), col),
                  pl.BlockSpec((T, WK), col), pl.BlockSpec((T, 128), col), _const((AB, T))],
        out_specs=[pl.BlockSpec((T, WK), col), pl.BlockSpec((T, WK), col), pl.BlockSpec((T, WK), col)],
        out_shape=[jax.ShapeDtypeStruct((T, QW), F32)] * 3,
        args=[q, kx, vx, o, dmix, lse, bias], sem=("arbitrary",), duties=duties)


def _softplus(x):
    return jnp.maximum(x, 0.0) + jnp.log1p(jnp.exp(-jnp.abs(x)))


def _causal_conv(u, zs, cw_ref, cb_ref):
    zs[8:, :] = u
    sh1, sh2, sh3 = (zs[8 - m:8 - m + L, :] for m in (1, 2, 3))
    return cb_ref[...] + cw_ref[3:4, :] * u + cw_ref[2:3, :] * sh1 + cw_ref[1:2, :] * sh2 + cw_ref[0:1, :] * sh3


def _ssd_chunk_common(xc, dtr, dtb_ref, alx_ref, e_ref):
    sg = _sigmoid(xc)
    act = xc * sg
    pre = dtr + dtb_ref[...]
    dt_x = _dot_hi(_softplus(pre), e_ref[...])
    a_x = -jnp.exp(alx_ref[...])
    ri = lax.broadcasted_iota(jnp.int32, (L, L), 0)
    ci = lax.broadcasted_iota(jnp.int32, (L, L), 1)
    tri = ri >= ci
    acs_x = _dot_hi(tri, dt_x * a_x, a_is_01=True)
    return dict(sg=sg, act=act, pre=pre, dt_x=dt_x, a_x=a_x, tri=tri, acs_x=acs_x)


def _decay(acs_x, acs_t, h, tri):
    col = acs_x[:, HD * h:HD * h + 1]
    row = acs_t[HD * h:HD * h + 1, :]
    return jnp.exp(jnp.where(tri, col - row, NEG))


def _ssd_fwd(xbc, z, dtr, convw, convb, dtb, alx, dskx, ssmn, e, duties=()):
    def body(u_ref, z_ref, dtr_ref, cw_ref, cb_ref, dtb_ref, alx_ref, dsk_ref, sn_ref, e_ref,
             yn_ref, y_ref, hs_ref, xc_ref, zs, hst):
        @pl.when(pl.program_id(0) == 0)
        def _():
            zs[0:8, :] = jnp.zeros((8, CONV_C), F32)
            hst[...] = jnp.zeros_like(hst)

        u = u_ref[...]
        xc = _causal_conv(u, zs, cw_ref, cb_ref)
        xc_ref[...] = xc
        zs[0:8, :] = u[L - 8:, :]
        cm = _ssd_chunk_common(xc, dtr_ref[...], dtb_ref, alx_ref, e_ref)
        act, dt_x, acs_x, tri = cm["act"], cm["dt_x"], cm["acs_x"], cm["tri"]
        xs = act[:, :SSM_W]
        acs_l = acs_x[L - 1:L, :]
        lam_x = jnp.exp(acs_x)
        w_x = jnp.exp(acs_l - acs_x)
        gam_x = jnp.exp(acs_l)
        acs_t = acs_x.T
        xd = xs * dt_x
        xb = xd.astype(BF16)
        xw = (xd * w_x).astype(BF16)
        lo = lax.broadcasted_iota(jnp.int32, (1, 128), 1) < HD
        hs_ref[0] = hst[...]
        pieces = []
        for grp in range(2):
            bb = act[:, SSM_W + SSM_N * grp:SSM_W + SSM_N * (grp + 1)].astype(BF16)
            cb_ = act[:, SSM_W + 2 * SSM_N + SSM_N * grp:SSM_W + 2 * SSM_N + SSM_N * (grp + 1)].astype(BF16)
            cbm = _dot_nt(cb_, bb)
            for jj in range(4):
                j = 4 * grp + jj
                sl = slice(128 * j, 128 * j + 128)
                m0 = (cbm * _decay(acs_x, acs_t, 2 * j, tri)).astype(BF16)
                m1 = (cbm * _decay(acs_x, acs_t, 2 * j + 1, tri)).astype(BF16)
                x2 = xb[:, sl]
                ydiag = jnp.where(lo, _dot(m0, x2), _dot(m1, x2))
                hprev = hst[j]
                yoff = lam_x[:, sl] * _dot(cb_, hprev.astype(BF16))
                pieces.append(ydiag + yoff)
                hst[j] = gam_x[:, sl] * hprev + _dot_tn(bb, xw[:, sl])
        y = jnp.concatenate(pieces, axis=1) + dsk_ref[...] * xs
        y_ref[...] = y
        zv = z_ref[...]
        yz = y * (zv * _sigmoid(zv))
        half = SSM_W // 2
        yn = jnp.concatenate([yz[:, :half] * _rs(yz[:, :half]), yz[:, half:] * _rs(yz[:, half:])], axis=1)
        yn_ref[...] = (yn * sn_ref[...]).astype(BF16)

    return _call(
        body, name="ssd_fwd", grid=(NCH,),
        in_specs=[_rows(L, CONV_C), _rows(L, SSM_W), _rows(L, SSM_H), _const((4, CONV_C)), _const((1, CONV_C)), _const((1, SSM_H)),
                  _const((1, SSM_W)), _const((1, SSM_W)), _const((1, SSM_W)), _const((SSM_H, SSM_W))],
        out_specs=[_rows(L, SSM_W), _rows(L, SSM_W), pl.BlockSpec((1, 8, SSM_N, 128), lambda c: (c, 0, 0, 0)), _rows(L, CONV_C)],
        out_shape=[jax.ShapeDtypeStruct((T, SSM_W), BF16), jax.ShapeDtypeStruct((T, SSM_W), F32),
                   jax.ShapeDtypeStruct((NCH, 8, SSM_N, 128), F32), jax.ShapeDtypeStruct((T, CONV_C), F32)],
        scratch=[pltpu.VMEM((8 + L, CONV_C), F32), pltpu.VMEM((8, SSM_N, 128), F32)],
        args=[xbc, z, dtr, convw, convb, dtb, alx, dskx, ssmn, e], sem=("arbitrary",), duties=duties)


def _ssd_bwd(dmix, xbc, xconv, z, dtr, y, hs, convw, dtb, alx, dskx, ssmn, e, e1, duties=()):
    rev = lambda i: (NCH - 1 - i, 0)

    def body(dyn_ref, u_ref, xc_ref, z_ref, dtr_ref, y_ref, hs_ref, cw_ref, dtb_ref, alx_ref, dsk_ref, sn_ref, e_ref, e1_ref,
             dxbc_ref, dz_ref, ddt_ref, dcw_ref, dcb_ref, dsn_ref, dpar_ref, dh, zd, colbuf):
        step = pl.program_id(0)

        @pl.when(step == 0)
        def _():
            for r in (dh, dcw_ref, dcb_ref, dsn_ref, dpar_ref):
                r[...] = jnp.zeros_like(r)
            zd[L:, :] = jnp.zeros((8, CONV_C), F32)

        u = u_ref[...]
        xc = xc_ref[...]
        cm = _ssd_chunk_common(xc, dtr_ref[...], dtb_ref, alx_ref, e_ref)
        sg, act, pre, dt_x, a_x, tri, acs_x = (cm[k] for k in ("sg", "act", "pre", "dt_x", "a_x", "tri", "acs_x"))
        xs = act[:, :SSM_W]
        acs_l = acs_x[L - 1:L, :]
        lam_x = jnp.exp(acs_x)
        w_x = jnp.exp(acs_l - acs_x)
        gam_x = jnp.exp(acs_l)
        acs_t = acs_x.T
        xd = xs * dt_x
        xb = xd.astype(BF16)
        xdw = xd * w_x
        xw = xdw.astype(BF16)
        lo = lax.broadcasted_iota(jnp.int32, (1, 128), 1) < HD
        row8 = lax.broadcasted_iota(jnp.int32, (8, 1), 0)

        dyn = dyn_ref[...]
        yv = y_ref[...]
        zv = z_ref[...]
        sz = _sigmoid(zv)
        siluz = zv * sz
        yz = yv * siluz
        half = SSM_W // 2
        gy = dyn * sn_ref[...]
        dyz_parts, yzn_parts = [], []
        for hf in range(2):
            part = yz[:, hf * half:(hf + 1) * half]
            r = _rs(part)
            pn = part * r
            gp = gy[:, hf * half:(hf + 1) * half]
            dyz_parts.append(r * (gp - pn * jnp.mean(gp * pn, axis=-1, keepdims=True)))
            yzn_parts.append(pn)
        dyz = jnp.concatenate(dyz_parts, axis=1)
        dsn_ref[...] += jnp.sum(dyn * jnp.concatenate(yzn_parts, axis=1), axis=0, keepdims=True)
        dy = dyz * siluz
        dz_ref[...] = dyz * yv * _dsilu(zv, sz)

        colbuf[...] = jnp.zeros_like(colbuf)
        dx_pieces, dacs_pieces, dacsl_pieces, db_pieces, dc_pieces = [], [], [], [], []
        for grp in range(2):
            bb = act[:, SSM_W + SSM_N * grp:SSM_W + SSM_N * (grp + 1)].astype(BF16)
            cb_ = act[:, SSM_W + 2 * SSM_N + SSM_N * grp:SSM_W + 2 * SSM_N + SSM_N * (grp + 1)].astype(BF16)
            cbm = _dot_nt(cb_, bb)
            dcbm = jnp.zeros((L, L), F32)
            dc_g = jnp.zeros((L, SSM_N), F32)
            db_g = jnp.zeros((L, SSM_N), F32)
            for jj in range(4):
                j = 4 * grp + jj
                sl = slice(128 * j, 128 * j + 128)
                dy2 = dy[:, sl]
                dy2b = dy2.astype(BF16)
                d0 = _decay(acs_x, acs_t, 2 * j, tri)
                d1 = _decay(acs_x, acs_t, 2 * j + 1, tri)
                m0 = cbm * d0
                m1 = cbm * d1
                x2 = xb[:, sl]
                hprev = hs_ref[0, j]
                hprevb = hprev.astype(BF16)
                dhn = dh[j]
                dhnb = dhn.astype(BF16)
                g2 = _dot(bb, dhnb)
                dx_pieces.append(jnp.where(lo, _dot_tn(m0.astype(BF16), dy2b), _dot_tn(m1.astype(BF16), dy2b)) + w_x[:, sl] * g2)
                zero = jnp.zeros_like(dy2b)
                dm0 = _dot_nt(jnp.where(lo, dy2b, zero), x2)
                dm1 = _dot_nt(jnp.where(lo, zero, dy2b), x2)
                dcbm = dcbm + dm0 * d0 + dm1 * d1
                e0 = dm0 * m0
                e1v = dm1 * m1
                colbuf[:, 2 * j:2 * j + 1] = jnp.sum(e0, axis=1, keepdims=True) - jnp.sum(e0.T, axis=1, keepdims=True)
                colbuf[:, 2 * j + 1:2 * j + 2] = jnp.sum(e1v, axis=1, keepdims=True) - jnp.sum(e1v.T, axis=1, keepdims=True)
                yoff = lam_x[:, sl] * _dot(cb_, hprevb)
                gxw = g2 * xdw[:, sl]
                dacs_pieces.append(dy2 * yoff - gxw)
                dacsl_pieces.append(jnp.sum(gxw, axis=0, keepdims=True) + gam_x[:, sl] * jnp.sum(dhn * hprev, axis=0, keepdims=True))
                dyl = (dy2 * lam_x[:, sl]).astype(BF16)
                dc_g = dc_g + _dot_nt(dyl, hprevb)
                db_g = db_g + _dot_nt(xw[:, sl], dhnb)
                dh[j] = gam_x[:, sl] * dhn + _dot_tn(cb_, dyl)
            dcbb = dcbm.astype(BF16)
            dc_pieces.append(dc_g + _dot(dcbb, bb))
            db_pieces.append(db_g + _dot_tn(dcbb, cb_))

        dxd = jnp.concatenate(dx_pieces, axis=1)
        rowi = lax.broadcasted_iota(jnp.int32, (L, 1), 0)
        dacs_x = (jnp.concatenate(dacs_pieces, axis=1) + _dot_hi(colbuf[...], e1_ref[...])
                  + jnp.where(rowi == L - 1, jnp.concatenate(dacsl_pieces, axis=1), 0.0))
        upper = lax.broadcasted_iota(jnp.int32, (L, L), 0) <= lax.broadcasted_iota(jnp.int32, (L, L), 1)
        dadt_x = _dot_hi(upper, dacs_x, a_is_01=True)
        ddt_x = dxd * xs + dadt_x * a_x
        ddtr = _dot_nt_hi(ddt_x, e_ref[...]) * _sigmoid(pre)
        ddt_ref[...] = jnp.zeros_like(ddt_ref)
        ddt_ref[:, 0:SSM_H] = ddtr
        dalx =jnp.sum(dadt_x * dt_x, axis=0, keepdims=True) * a_x
        ddskx = jnp.sum(dy * xs, axis=0, keepdims=True)
        par_x = jnp.where(row8 == 1, dalx, 0.0) + jnp.where(row8 == 2, ddskx, 0.0)
        dpar_ref[...] += _dot_nt_hi(par_x, e_ref[...]) + jnp.where(row8 == 0, jnp.sum(ddtr, axis=0, keepdims=True), 0.0)

        dxs = dxd * dt_x + dsk_ref[...] * dy
        dact = jnp.concatenate([dxs] + db_pieces + dc_pieces, axis=1)
        du = dact * _dsilu(xc, sg)
        dcb_ref[...] += jnp.sum(du, axis=0, keepdims=True)
        zd[0:L, :] = du
        f1, f2, f3 = (zd[m:m + L, :] for m in (1, 2, 3))
        dxbc_ref[...] = cw_ref[3:4, :] * du + cw_ref[2:3, :] * f1 + cw_ref[1:2, :] * f2 + cw_ref[0:1, :] * f3
        dcw = jnp.zeros((8, CONV_C), F32)
        for k, shifted in enumerate((f3, f2, f1, du)):
            dcw = dcw + jnp.where(row8 == k, jnp.sum(shifted * u, axis=0, keepdims=True), 0.0)
        dcw_ref[...] += dcw
        zd[L:, :] = du[:8, :]

    return _call(
        body, name="ssd_bwd", grid=(NCH,),
        in_specs=[pl.BlockSpec((L, SSM_W), lambda i: (NCH - 1 - i, 1)), pl.BlockSpec((L, CONV_C), rev), pl.BlockSpec((L, CONV_C), rev),
                  pl.BlockSpec((L, SSM_W), rev), pl.BlockSpec((L, SSM_H), rev), pl.BlockSpec((L, SSM_W), rev),
                  pl.BlockSpec((1, 8, SSM_N, 128), lambda i: (NCH - 1 - i, 0, 0, 0)),
                  _const((4, CONV_C)), _const((1, SSM_H)), _const((1, SSM_W)), _const((1, SSM_W)), _const((1, SSM_W)),
                  _const((SSM_H, SSM_W)), _const((128, SSM_W))],
        out_specs=[pl.BlockSpec((L, CONV_C), rev), pl.BlockSpec((L, SSM_W), rev), pl.BlockSpec((L, 128), rev),
                   _const((8, CONV_C)), _const((1, CONV_C)), _const((1, SSM_W)), _const((8, SSM_H))],
        out_shape=[jax.ShapeDtypeStruct((T, CONV_C), F32), jax.ShapeDtypeStruct((T, SSM_W), F32), jax.ShapeDtypeStruct((T, 128), F32),
                   jax.ShapeDtypeStruct((8, CONV_C), F32), jax.ShapeDtypeStruct((1, CONV_C), F32), jax.ShapeDtypeStruct((1, SSM_W), F32),
                   jax.ShapeDtypeStruct((8, SSM_H), F32)],
        scratch=[pltpu.VMEM((8, SSM_N, 128), F32), pltpu.VMEM((L + 8, CONV_C), F32), pltpu.VMEM((L, 128), F32)],
        args=[dmix, xbc, xconv, z, dtr, y, hs, convw, dtb, alx, dskx, ssmn, e, e1], sem=("arbitrary",), duties=duties)


def _loss_head(x3, target):
    tm = 512

    def body(x_ref, t_ref, dy_ref, ss_ref):
        @pl.when(pl.program_id(0) == 0)
        def _():
            ss_ref[...] = jnp.zeros_like(ss_ref)

        err = x_ref[...] - t_ref[...]
        dy_ref[...] = err * (1.0 / D)
        ss_ref[...] += jnp.sum(jnp.sum(err * err, axis=1, keepdims=True), axis=0, keepdims=True)

    return pl.pallas_call(
        body, name="loss_head", grid=(T // tm,),
        in_specs=[_rows(tm, D), _rows(tm, D)],
        out_specs=[_rows(tm, D), _const((1, 128))],
        out_shape=[jax.ShapeDtypeStruct((T, D), F32), jax.ShapeDtypeStruct((1, 128), F32)],
        compiler_params=_params(("arbitrary",)),
    )(x3, target)


def _adam_math(w, g, m, v):
    m = ADAM_B1 * m + (1.0 - ADAM_B1) * g
    v = ADAM_B2 * v + (1.0 - ADAM_B2) * (g * g)
    m_hat = m / (1.0 - ADAM_B1 ** ADAM_STEP)
    v_hat = v / (1.0 - ADAM_B2 ** ADAM_STEP)
    delta = -ADAM_LR * (m_hat / (jnp.sqrt(v_hat) + ADAM_EPS) + ADAM_WD * w)
    return delta, m, v


def _adamw(w, m, v, parts, name, after=()):
    rows, cols = w.shape
    tr = rows if rows <= 512 else 256
    assert rows % tr == 0
    n_parts = len(parts)

    def body(*refs):
        w_ref, m_ref, v_ref = refs[:3]
        p_refs = refs[3:3 + n_parts]
        g_ref, d_ref, nm_ref, nv_ref = refs[3 + n_parts + len(after):]
        g = p_refs[0][...].astype(F32)
        for p in p_refs[1:]:
            g = g + p[...].astype(F32)
        delta, nm, nv = _adam_math(w_ref[...], g, m_ref[...], v_ref[...])
        g_ref[...] = g
        d_ref[...] = delta
        nm_ref[...] = nm
        nv_ref[...] = nv

    spec = pl.BlockSpec((tr, cols), lambda i: (i, 0))
    return pl.pallas_call(
        body, name=name, grid=(rows // tr,),
        in_specs=[spec] * (3 + n_parts) + [ANY] * len(after), out_specs=[spec] * 4,
        out_shape=[jax.ShapeDtypeStruct((rows, cols), F32)] * 4,
        compiler_params=_params(("parallel",)),
    )(w, m, v, *parts, *after)


COL_TILE = 512


def _adamw_sharded(w, m, v, chip_sum, from_chips, other_chips, name, after=()):
    rows, cols = w.shape
    prow = chip_sum.shape[0]
    assert cols % COL_TILE == 0 and prow >= rows and chip_sum.shape[1] == cols

    def body(ids_ref, w_ref, m_ref, v_ref, s_ref, r1_ref, r2_ref, r3_ref, *rest):
        g_ref, d_ref, nm_ref, nv_ref = rest[len(after):]
        g = s_ref[...]
        for r in (r1_ref, r2_ref, r3_ref):
            g = g + r[0].astype(F32)
        g = g[:rows]
        delta, nm, nv = _adam_math(w_ref[...], g, m_ref[...], v_ref[...])
        g_ref[...] = g
        d_ref[...] = delta
        nm_ref[...] = nm
        nv_ref[...] = nv

    spec = pl.BlockSpec((rows, COL_TILE), lambda i, ids: (0, i))
    part = lambda k: pl.BlockSpec((1, prow, COL_TILE), lambda i, ids: (ids[k], 0, i))
    return pl.pallas_call(
        body, name=name,
        grid_spec=pltpu.PrefetchScalarGridSpec(
            num_scalar_prefetch=1, grid=(cols // COL_TILE,),
            in_specs=[spec, spec, spec, pl.BlockSpec((prow, COL_TILE), lambda i, ids: (0, i)), part(0), part(1), part(2)]
            + [ANY] * len(after), out_specs=[spec] * 4),
        out_shape=[jax.ShapeDtypeStruct((rows, cols), F32)] * 4,
        compiler_params=_params(("parallel",)),
    )(other_chips, w, m, v, chip_sum, from_chips, from_chips, from_chips, *after)


def _chip_sum(mine, recv, name):
    rows, cols = mine.shape[1:]

    def body(a_ref, b_ref, s_ref, sb_ref):
        s = a_ref[0] + b_ref[0].astype(F32)
        sb_ref[0] = s.astype(BF16)

        @pl.when(pl.program_id(0) == 2 * lax.axis_index("x") + lax.axis_index("y"))
        def _():
            s_ref[...] = s

    by_chip = pl.BlockSpec((1, rows, cols), lambda k: (k, 0, 0))
    return pl.pallas_call(
        body, name=name, grid=(N_DEV // 2,),
        in_specs=[by_chip, by_chip], out_specs=[_const((rows, cols)), by_chip],
        out_shape=[jax.ShapeDtypeStruct((rows, cols), F32), jax.ShapeDtypeStruct((N_DEV // 2, rows, cols), BF16)],
        compiler_params=_params(("arbitrary",)),
    )(mine, recv)


def _all_reduce_small(v, after=()):
    rows = v.shape[0]

    def body(v_ref, *rest):
        out_ref, gath, send_sems, recv_sems = rest[len(after):]
        x, y, c = _place()
        me, sibling = (x, y, c), (x, y, 1 - c)
        chips = [(1 - x, y), (x, 1 - y), (1 - x, 1 - y)]

        def blk(px, py, pc):
            return gath.at[pl.ds((4 * px + 2 * py + pc) * rows, rows), :]

        def copy(k, block, to, src=None):
            return pltpu.make_async_remote_copy(src_ref=blk(*block) if src is None else src, dst_ref=blk(*block),
                                                send_sem=send_sems.at[k], recv_sem=recv_sems.at[k], device_id=to, device_id_type=MESH)

        gath[pl.ds((4 * x + 2 * y + c) * rows, rows), :] = v_ref[...]
        first = [copy(0, me, sibling, src=v_ref)] + [copy(1 + j, me, (*chip, c), src=v_ref) for j, chip in enumerate(chips)]
        for cp in first:
            cp.start()
        passed = [copy(4 + j, (*chip, c), sibling) for j, chip in enumerate(chips)]
        for j, chip in enumerate(chips):
            copy(1 + j, (*chip, c), me).wait_recv()
            passed[j].start()
        copy(0, sibling, me).wait_recv()
        for j, chip in enumerate(chips):
            copy(4 + j, (*chip, 1 - c), me).wait_recv()
        for cp in first + passed:
            cp.wait_send()
        acc = gath[0:rows, :]
        for d in range(1, N_DEV):
            acc = acc + gath[d * rows:(d + 1) * rows, :]
        out_ref[...] = acc

    vm = pl.BlockSpec(memory_space=pltpu.VMEM)
    return pl.pallas_call(
        body, name="all_reduce_small",
        in_specs=[vm] + [ANY] * len(after), out_specs=vm,
        out_shape=jax.ShapeDtypeStruct(v.shape, F32),
        scratch_shapes=[pltpu.VMEM((N_DEV * rows, 128), F32), pltpu.SemaphoreType.DMA((7,)), pltpu.SemaphoreType.DMA((7,))],
    )(v, *after)


def _rope_tables(positions):
    inv_freq = ROPE_THETA ** (-jnp.arange(0, HD, 2, dtype=F32) / HD)
    ang = positions.reshape(T).astype(F32)[:, None] * inv_freq
    ang = jnp.concatenate([ang, ang, ang, ang], axis=-1)
    lo_half = (jnp.arange(128) % HD) < (HD // 2)
    return jnp.cos(ang), jnp.where(lo_half, -jnp.sin(ang), jnp.sin(ang))


def _selectors():
    lane = jnp.arange(QW)
    e = (lane[None, :] // HD == jnp.arange(SSM_H)[:, None]).astype(F32)
    e1 = ((lane[None, :] == HD * jnp.arange(128)[:, None]) & (jnp.arange(128)[:, None] < SSM_H)).astype(F32)
    src = jnp.arange(KVW)
    ex = ((lane[None, :] // (HD * (NQ // NKV)) == src[:, None] // HD) & (lane[None, :] % HD == src[:, None] % HD)).astype(F32)
    return e, e1, ex


WEIGHTS = ['ffn1_pre_norm', 'ffn1_w_gate', 'ffn1_w_up', 'ffn1_w_down', 'ffn1_post_norm', 'mix_pre_norm', 'w_in', 'conv_w', 'conv_b',
           'dt_bias', 'a_log', 'd_skip', 'ssm_norm', 'w_out', 'mix_post_norm', 'ffn2_pre_norm', 'ffn2_w_gate', 'ffn2_w_up',
           'ffn2_w_down', 'ffn2_post_norm']
COL_SHARDED = ['ffn1_w_gate', 'ffn1_w_up', 'ffn2_w_gate', 'ffn2_w_up', 'w_in']
ROW_SHARDED = ['ffn1_w_down', 'ffn2_w_down', 'w_out']
BIG = COL_SHARDED + ROW_SHARDED
FFN_BIG = COL_SHARDED[:4] + ROW_SHARDED[:2]
SMALL = ['ffn1_pre_norm', 'ffn1_post_norm', 'mix_pre_norm', 'conv_b', 'dt_bias', 'a_log', 'd_skip', 'ssm_norm', 'mix_post_norm',
         'ffn2_pre_norm', 'ffn2_post_norm']
FFN1 = ['ffn1_w_gate', 'ffn1_w_up', 'ffn1_w_down']
FFN2 = ['ffn2_w_gate', 'ffn2_w_up', 'ffn2_w_down']


def _wire_block(name, a):
    if name in FFN_BIG:
        return jnp.pad(a.astype(BF16), ((0, FSH - FSR), (0, 0)))
    if name == "w_in":
        return jnp.pad(a.astype(BF16), ((0, ISW - ISR), (0, 0)))
    return a if name == "conv_w" else a.astype(BF16)


def _whole_from_gathered(name, a):
    if name == "conv_w":
        return jnp.transpose(a, (1, 0, 2)).reshape(a.shape[1], -1)
    return a.reshape(-1, D)


def _step(x, positions, target, small, blocks=None, whole=None):
    dist = blocks is not None
    core = "mesh" if dist else 0
    w = dict(small)
    if whole:
        w.update(whole)

    def gather(names):
        return [_gather_duty([_wire_block(n, blocks[n]) for n in names])] if dist else []

    def put(names, results):
        if dist:
            for n, r in zip(names, results[0]):
                w[n] = _whole_from_gathered(n, r)

    g, sums, red = {}, {}, {}

    def swap(names):
        return [_swap_duty([g[n][1] for n in names])] if dist else []

    def chip_sums(names, from_sibling):
        if dist:
            for n, recv in zip(names, from_sibling):
                sums[n] = _chip_sum(g[n][0], recv, "chip_sum_" + n)

    def exchange(names):
        return [_exchange_duty([sums[n][1] for n in names])] if dist else []

    def reduced(names, from_chips):
        if dist:
            for n, recv in zip(names, from_chips):
                red[n] = (sums[n][0], recv)

    cos, sin_s = _rope_tables(positions)
    e, e1, exf = _selectors()
    bias = _attn_bias()
    alx = jnp.repeat(w["a_log"], HD, axis=1)
    dskx = jnp.repeat(w["d_skip"], HD, axis=1)

    if dist:
        put(FFN1, _comm_only(gather(FFN1), "gather_ffn1"))
    (x1, n1, a1, b1, hm1, h1), got = _ffn_fwd(x, w["ffn1_pre_norm"], w["ffn1_w_gate"], w["ffn1_w_up"], w["ffn1_w_down"],
                                              w["ffn1_post_norm"], "ffn1_fwd", gather(["w_in", "conv_w"]))
    put(["w_in", "conv_w"], got)
    (n2, q, kx, vx, xbc, z, dtr), got = _inproj_fwd(x1, w["mix_pre_norm"], w["w_in"], cos, sin_s, exf.astype(BF16), gather(["w_out"]))
    put(["w_out"], got)
    (attn, lse), got = _attn_fwd(q, kx, vx, bias, gather(FFN2[:2]))
    put(FFN2[:2], got)
    (yn, y, hs, xconv), got = _ssd_fwd(xbc, z, dtr, w["conv_w"], w["conv_b"], w["dt_bias"], alx, dskx, w["ssm_norm"], e, gather(FFN2[2:]))
    put(FFN2[2:], got)
    x2, h2 = _outproj_fwd(x1, attn, yn, w["w_out"], w["mix_post_norm"])
    (x3, n3, a3, b3, hm3, h3), _ = _ffn_fwd(x2, w["ffn2_pre_norm"], w["ffn2_w_gate"], w["ffn2_w_up"], w["ffn2_w_down"],
                                            w["ffn2_post_norm"], "ffn2_fwd")
    dx3, ss = _loss_head(x3, target)

    (dx2, da3, db3, dh3, g["ffn2_pre_norm"], g["ffn2_post_norm"]), _ = _ffn_bwd(
        dx3, x2, a3, b3, h3, w["ffn2_pre_norm"], w["ffn2_post_norm"], w["ffn2_w_gate"], w["ffn2_w_up"], w["ffn2_w_down"], "ffn2_bwd")
    g["ffn2_w_down"] = _matmul_tn(hm3, dh3, "ffn2_dwd", core)[0]
    g["ffn2_w_gate"] = _matmul_tn(da3, n3, "ffn2_dwg", core)[0]
    g["ffn2_w_up"] = _matmul_tn(db3, n3, "ffn2_dwu", core)[0]

    (dh2, dmix, g["mix_post_norm"]), got = _outproj_bwd(dx2, h2, w["mix_post_norm"], w["w_out"], swap(FFN2))
    chip_sums(FFN2, got[0] if dist else None)
    g["w_out"] = _dwout(attn, yn, dh2, core)
    (dq, dkx, dvx), got = _attn_bwd(q, kx, vx, attn, dmix, lse, bias, exchange(FFN2) + swap(["w_out"]))
    if dist:
        reduced(FFN2, got[0])
        chip_sums(["w_out"], got[1])
    (dxbc, dz, ddt, dcw, g["conv_b"], g["ssm_norm"], dpar), got = _ssd_bwd(
        dmix, xbc, xconv, z, dtr, y, hs, w["conv_w"], w["dt_bias"], alx, dskx, w["ssm_norm"], e, e1, exchange(["w_out"]))
    reduced(["w_out"], got[0] if dist else None)
    g["conv_w"] = dcw[0:4]
    g["dt_bias"], g["a_log"], g["d_skip"] = dpar[0:1], dpar[1:2], dpar[2:3]
    dx1, dproj, g["mix_pre_norm"] = _inproj_bwd(dx2, dq, dkx, dvx, dxbc, dz, ddt, x1, w["mix_pre_norm"], w["w_in"], cos, sin_s, exf)
    g["w_in"] = _matmul_tn(dproj, n2, "dwin", core)[0]

    started = {}

    def start(n):
        started[n] = _exchange_start(sums[n][1], "start_exchange_" + n)
        return [started[n]["token"]]

    after = []
    if dist:
        chip_sums(["w_in"], _comm_only(swap(["w_in"]), "swap_w_in")[0])
        after = start("w_in")
    (dx0, da1, db1, dh1, g["ffn1_pre_norm"], g["ffn1_post_norm"]), _ = _ffn_bwd(
        dx1, x, a1, b1, h1, w["ffn1_pre_norm"], w["ffn1_post_norm"], w["ffn1_w_gate"], w["ffn1_w_up"], w["ffn1_w_down"], "ffn1_bwd",
        after)
    total = None
    if dist:
        flat = jnp.concatenate([g[n].reshape(-1) for n in SMALL] + [g["conv_w"].reshape(-1), ss[0, 0:1]])
        rows = -(-flat.shape[0] // 128 // 8) * 8
        total = _all_reduce_small(jnp.pad(flat, (0, rows * 128 - flat.shape[0])).reshape(rows, 128), after)
        after = [total]
    g["ffn1_w_down"], _ = _matmul_tn(hm1, dh1, "ffn1_dwd", core, after=after)
    g["ffn1_w_gate"], got = _matmul_tn(da1, n1, "ffn1_dwg", core, duties=swap(["ffn1_w_down"]))
    if dist:
        chip_sums(["ffn1_w_down"], got[0])
        after = start("ffn1_w_down")
    g["ffn1_w_up"], got = _matmul_tn(db1, n1, "ffn1_dwu", core, after=after, duties=swap(["ffn1_w_gate"]))
    if dist:
        chip_sums(["ffn1_w_gate"], got[0])
        after = start("ffn1_w_gate")
        chip_sums(["ffn1_w_up"], _comm_only(swap(["ffn1_w_up"]), "swap_ffn1_w_up", after=after)[0])
        start("ffn1_w_up")
    return ss, dx0, g, red, {n: (sums[n][0], started[n]) for n in started}, total


def kernel(x, positions, ffn1_pre_norm, ffn1_w_gate, ffn1_w_up, ffn1_w_down, ffn1_post_norm, mix_pre_norm, w_in, conv_w, conv_b, dt_bias, a_log, d_skip, ssm_norm, w_out, mix_post_norm, ffn2_pre_norm, ffn2_w_gate, ffn2_w_up, ffn2_w_down, ffn2_post_norm, loss_target, m_ffn1_pre_norm, m_ffn1_w_gate, m_ffn1_w_up, m_ffn1_w_down, m_ffn1_post_norm, m_mix_pre_norm, m_w_in, m_conv_w, m_conv_b, m_dt_bias, m_a_log, m_d_skip, m_ssm_norm, m_w_out, m_mix_post_norm, m_ffn2_pre_norm, m_ffn2_w_gate, m_ffn2_w_up, m_ffn2_w_down, m_ffn2_post_norm, v_ffn1_pre_norm, v_ffn1_w_gate, v_ffn1_w_up, v_ffn1_w_down, v_ffn1_post_norm, v_mix_pre_norm, v_w_in, v_conv_w, v_conv_b, v_dt_bias, v_a_log, v_d_skip, v_ssm_norm, v_w_out, v_mix_post_norm, v_ffn2_pre_norm, v_ffn2_w_gate, v_ffn2_w_up, v_ffn2_w_down, v_ffn2_post_norm):
    given = dict(locals())
    drop = lambda n, a: a if n in SMALL else (a[0].T if n in COL_SHARDED else a[0])
    w = {n: drop(n, given[n]) for n in WEIGHTS}
    m = {n: drop(n, given["m_" + n]) for n in WEIGHTS}
    v = {n: drop(n, given["v_" + n]) for n in WEIGHTS}
    cx, cy, cc = _place()
    others = [2 * (1 - cx) + cy, 2 * cx + (1 - cy), 2 * (1 - cx) + (1 - cy)]

    _, grad_x, g, red, pending, total = _step(x[0], positions, loss_target[0], {n: w[n] for n in SMALL},
                                              blocks={n: w[n] for n in BIG + ["conv_w"]})
    total = total.reshape(-1)
    chip_ids = jnp.stack(others).astype(jnp.int32)
    out_g, out_d, out_m, out_v = {}, {}, {}, {}
    last_start = [pending["ffn1_w_up"][1]["token"]]
    for n in BIG:
        if n not in pending:
            out_g[n], out_d[n], out_m[n], out_v[n] = _adamw_sharded(w[n], m[n], v[n], red[n][0], red[n][1], chip_ids, "adamw_" + n,
                                                                    last_start)

    rows = total.shape[0] // 128
    sizes = [w[n].size for n in SMALL]
    offs = [0]
    for s_ in sizes:
        offs.append(offs[-1] + s_)
    gs = {n: total[offs[i]:offs[i + 1]].reshape(w[n].shape) for i, n in enumerate(SMALL)}
    gcw = total[offs[-1]:offs[-1] + 4 * CONV_C].reshape(4, CONV_C)
    loss = 0.5 * total[offs[-1] + 4 * CONV_C] / D
    gs["conv_w"] = lax.dynamic_slice_in_dim(gcw, (4 * cx + 2 * cy + cc) * (CONV_C // N_DEV), CONV_C // N_DEV, axis=1)
    names = SMALL + ["conv_w"]

    def pack(d):
        flat = jnp.concatenate([d[n].reshape(-1) for n in names])
        return jnp.pad(flat, (0, rows * 128 - flat.shape[0])).reshape(rows, 128)

    pg, pd, pm, pv = _adamw(pack(w), pack(m), pack(v), [pack(gs)], "adamw_small", last_start)
    done = [pg] + [out_v[n] for n in BIG if n not in pending]
    for n, (chip_sum, started) in pending.items():
        recv = _exchange_wait(started, done, "wait_exchange_" + n)
        out_g[n], out_d[n], out_m[n], out_v[n] = _adamw_sharded(w[n], m[n], v[n], chip_sum, recv, chip_ids, "adamw_" + n)
        done = [out_v[n]]
    o2 = [0]
    for n in names:
        o2.append(o2[-1] + w[n].size)
    for i, n in enumerate(names):
        for dst, src in ((out_g, pg), (out_d, pd), (out_m, pm), (out_v, pv)):
            dst[n] = src.reshape(-1)[o2[i]:o2[i + 1]].reshape(w[n].shape)

    outs = [loss, grad_x[None]]
    for d in (out_g, out_d, out_m, out_v):
        outs += [d[n] if n in SMALL else (d[n].T[None] if n in COL_SHARDED else d[n][None]) for n in WEIGHTS]
    return tuple(outs)
```

```python
import functools
import math

import jax
import jax.numpy as jnp
from jax import lax
from jax.experimental import pallas as pl
from jax.experimental.pallas import tpu as pltpu

F32 = jnp.float32
BF16 = jnp.bfloat16
MESH = pl.DeviceIdType.MESH

N_DEV = 8
T = 2048
D = 1024
FF = 2816
FSR = FF // N_DEV
FSH = 384
FFP = N_DEV * FSH
HD = 64
NQ = 16
NKV = 4
QW = NQ * HD
KVW = NKV * HD
SSM_W = 1024
SSM_H = 16
SSM_N = 128
CONV_C = SSM_W + 2 * 2 * SSM_N
IN_COLS = 4112
INP = 4224
ISR = IN_COLS // N_DEV
ISW = 528
ISG = 640
L = 128
NCH = T // L
AB = 256
NAB = T // AB
EPS = 1e-6
NEG = -1e30
ROPE_THETA = 10000.0
DILATIONS = ((128, 1), (512, 4), (2048, 16))

ADAM_LR = 0.001
ADAM_B1 = 0.9
ADAM_B2 = 0.999
ADAM_EPS = 1e-08
ADAM_WD = 0.01
ADAM_STEP = 10

VMEM_LIMIT = 58 * 1024 * 1024


def _params(sem, vmem=VMEM_LIMIT):
    return pltpu.CompilerParams(dimension_semantics=sem, vmem_limit_bytes=vmem)


def _dot(a, b):
    return jnp.dot(a, b, preferred_element_type=F32)


def _dot_nt(a, b):
    return lax.dot_general(a, b, (((1,), (1,)), ((), ())), preferred_element_type=F32)


def _dot_tn(a, b):
    return lax.dot_general(a, b, (((0,), (0,)), ((), ())), preferred_element_type=F32)


def _split3(x):
    hi = x.astype(BF16)
    r1 = x - hi.astype(F32)
    mid = r1.astype(BF16)
    lo = (r1 - mid.astype(F32)).astype(BF16)
    return hi, mid, lo


def _dot_hi(a, b, a_is_01=False):
    if a_is_01:
        sel = a.astype(BF16)
        return sum(_dot(sel, p) for p in _split3(b))
    sel = b.astype(BF16)
    return sum(_dot(p, sel) for p in _split3(a))


def _dot_nt_hi(a, b):
    sel = b.astype(BF16)
    return sum(_dot_nt(p, sel) for p in _split3(a))


def _rs(x):
    return lax.rsqrt(jnp.mean(x * x, axis=-1, keepdims=True) + EPS)


def _sigmoid(x):
    return jax.nn.sigmoid(x)


def _dsilu(x, s):
    return s * (1.0 + x * (1.0 - s))


def _resident(shape):
    nd = len(shape)
    return pl.BlockSpec(shape, lambda *_: (0,) * nd, pipeline_mode=pl.Buffered(1))


def _const(shape):
    nd = len(shape)
    return pl.BlockSpec(shape, lambda *_: (0,) * nd)


def _rows(tm, cols):
    return pl.BlockSpec((tm, cols), lambda i: (i, 0))


ANY = pl.BlockSpec(memory_space=pl.ANY)


def _place():
    return lax.axis_index("x"), lax.axis_index("y"), lax.axis_index("c")


def _gather_duty(arrays):
    n = len(arrays)
    results = [jax.ShapeDtypeStruct((N_DEV,) + a.shape, a.dtype) for a in arrays]

    def make(ins, outs, send_sems, recv_sems, local_sems):
        x, y, c = _place()
        me, sibling = (x, y, c), (x, y, 1 - c)
        chips = [(1 - x, y), (x, 1 - y), (1 - x, 1 - y)]

        def place_of(a, px, py, pc):
            return outs[a].at[4 * px + 2 * py + pc]

        def copy(a, k, block, to, src=None):
            dst = place_of(a, *block)
            return pltpu.make_async_remote_copy(src_ref=dst if src is None else src, dst_ref=dst,
                                                send_sem=send_sems.at[7 * a + k], recv_sem=recv_sems.at[7 * a + k],
                                                device_id=to, device_id_type=MESH)

        def own(a):
            return pltpu.make_async_copy(ins[a], place_of(a, *me), local_sems.at[a])

        def first(a):
            return [copy(a, 0, me, sibling, src=ins[a])] + [copy(a, 1 + j, me, (*chip, c), src=ins[a]) for j, chip in enumerate(chips)]

        def start():
            for a in range(n):
                own(a).start()
            for a in range(n):
                for cp in first(a):
                    cp.start()

        def finish():
            for j, chip in enumerate(chips):
                for a in range(n):
                    copy(a, 1 + j, (*chip, c), me).wait_recv()
                    copy(a, 4 + j, (*chip, c), sibling).start()
            for a in range(n):
                copy(a, 0, sibling, me).wait_recv()
                for j, chip in enumerate(chips):
                    copy(a, 4 + j, (*chip, 1 - c), me).wait_recv()
            for a in range(n):
                for cp in first(a) + [copy(a, 4 + j, (*chip, c), sibling) for j, chip in enumerate(chips)]:
                    cp.wait_send()
                own(a).wait()

        return start, finish

    return dict(operands=list(arrays), results=results, sems=(7 * n, 7 * n, n), make=make)


def _swap_duty(arrays):
    n = len(arrays)
    half = N_DEV // 2
    results = [jax.ShapeDtypeStruct(a.shape, a.dtype) for a in arrays]

    def make(ins, outs, send_sems, recv_sems):
        x, y, c = _place()

        def copies():
            return [pltpu.make_async_remote_copy(src_ref=ins[a].at[k], dst_ref=outs[a].at[k],
                                                 send_sem=send_sems.at[half * a + k], recv_sem=recv_sems.at[half * a + k],
                                                 device_id=(x, y, 1 - c), device_id_type=MESH)
                    for a in range(n) for k in range(half)]

        def start():
            for cp in copies():
                cp.start()

        def finish():
            for cp in copies():
                cp.wait()

        return start, finish

    return dict(operands=list(arrays), results=results, sems=(half * n, half * n), make=make)


def _exchange_duty(arrays):
    n = len(arrays)
    results = [jax.ShapeDtypeStruct(a.shape, a.dtype) for a in arrays]

    def make(ins, outs, send_sems, recv_sems):
        x, y, c = _place()
        chips = [(1 - x, y), (x, 1 - y), (1 - x, 1 - y)]
        my_chip = 2 * x + y

        def sends():
            return [pltpu.make_async_remote_copy(src_ref=ins[a].at[2 * px + py], dst_ref=outs[a].at[my_chip],
                                                 send_sem=send_sems.at[3 * a + j], recv_sem=recv_sems.at[3 * a + j],
                                                 device_id=(px, py, c), device_id_type=MESH)
                    for a in range(n) for j, (px, py) in enumerate(chips)]

        def start():
            for cp in sends():
                cp.start()

        def finish():
            for a in range(n):
                for j, (px, py) in enumerate(chips):
                    pltpu.make_async_remote_copy(src_ref=ins[a].at[my_chip], dst_ref=outs[a].at[2 * px + py],
                                                 send_sem=send_sems.at[3 * a + j], recv_sem=recv_sems.at[3 * a + j],
                                                 device_id=(px, py, c), device_id_type=MESH).wait_recv()
            for cp in sends():
                cp.wait_send()

        return start, finish

    return dict(operands=list(arrays), results=results, sems=(3 * n, 3 * n), make=make)


def _call(body, *, name, grid, in_specs, out_specs, out_shape, args, sem, scratch=(), duties=(), after=()):
    n_in, n_out, n_scr = len(in_specs), len(out_specs), len(scratch)
    sem_shapes = [pltpu.SemaphoreType.DMA((k,)) for d in duties for k in d["sems"]]

    def full(*refs):
        pos = [0]

        def take(k):
            pos[0] += k
            return refs[pos[0] - k:pos[0]]

        ins = take(n_in)
        d_ins = [take(len(d["operands"])) for d in duties]
        take(len(after))
        outs = take(n_out)
        d_outs = [take(len(d["results"])) for d in duties]
        scr = take(n_scr)
        d_sems = [take(len(d["sems"])) for d in duties]
        hooks = [d["make"](di, do, *ds) for d, di, do, ds in zip(duties, d_ins, d_outs, d_sems)]
        if grid and hooks:
            ids = [pl.program_id(k) for k in range(len(grid))]
            first = functools.reduce(jnp.logical_and, [i == 0 for i in ids])
            last = functools.reduce(jnp.logical_and, [i == g - 1 for i, g in zip(ids, grid)])

            @pl.when(first)
            def _():
                for start, _ in hooks:
                    start()

            body(*ins, *outs, *scr)

            @pl.when(last)
            def _():
                for _, finish in hooks:
                    finish()
        else:
            for start, _ in hooks:
                start()
            body(*ins, *outs, *scr)
            for _, finish in hooks:
                finish()

    d_args = [a for d in duties for a in d["operands"]]
    d_res = [r for d in duties for r in d["results"]]
    kwargs = dict(grid=grid) if grid else {}
    res = pl.pallas_call(
        full, name=name, in_specs=list(in_specs) + [ANY] * (len(d_args) + len(after)), out_specs=list(out_specs) + [ANY] * len(d_res),
        out_shape=list(out_shape) + d_res, scratch_shapes=list(scratch) + sem_shapes,
        compiler_params=_params(sem) if grid else None, **kwargs,
    )(*args, *d_args, *after)
    own, rest = list(res[:n_out]), list(res[n_out:])
    by_duty = []
    for d in duties:
        by_duty.append(rest[:len(d["results"])])
        rest = rest[len(d["results"]):]
    return own, by_duty


def _comm_only(duties, name, after=()):
    return _call(lambda: None, name=name, grid=None, in_specs=[], out_specs=[], out_shape=[], args=[], sem=None, duties=duties,
                 after=after)[1]


HBM = pl.BlockSpec(memory_space=pltpu.HBM)
SEMS = pl.BlockSpec(memory_space=pltpu.SEMAPHORE)
SIDE_EFFECT = pltpu.SideEffectType.DATAFLOW_SIDE_EFFECTING
N_OTHER_CHIPS = 3


def _chip_copies(src_ref, land_ref, sems):
    x, y, c = _place()
    chips = [(1 - x, y), (x, 1 - y), (1 - x, 1 - y)]
    return [pltpu.make_async_remote_copy(src_ref=src_ref.at[2 * px + py], dst_ref=land_ref.at[2 * x + y],
                                         send_sem=sems[j], recv_sem=sems[N_OTHER_CHIPS + j], device_id=(px, py, c), device_id_type=MESH)
            for j, (px, py) in enumerate(chips)]


def _exchange_start(pb, name):
    n_sem = 2 * N_OTHER_CHIPS

    def body(pb_ref, land_ref, *rest):
        for cp in _chip_copies(pb_ref, land_ref, rest[:n_sem]):
            cp.start()
        token = rest[n_sem + 2]
        token[...] = jnp.zeros_like(token)

    res = pl.pallas_call(
        body, name=name,
        out_shape=(pltpu.SemaphoreType.DMA(()),) * n_sem + (pltpu.HBM(pb.shape, pb.dtype), pltpu.HBM(pb.shape, pb.dtype),
                                                              jax.ShapeDtypeStruct((8, 128), F32)),
        in_specs=(HBM, HBM), out_specs=(SEMS,) * n_sem + (HBM, HBM, pl.BlockSpec(memory_space=pltpu.VMEM)),
        input_output_aliases={0: n_sem, 1: n_sem + 1},
        compiler_params=pltpu.CompilerParams(has_side_effects=SIDE_EFFECT),
    )(pltpu.with_memory_space_constraint(pb, pltpu.HBM), pltpu.with_memory_space_constraint(lax.empty(pb.shape, pb.dtype), pltpu.HBM))
    return dict(sems=res[:n_sem], src=res[n_sem], land=res[n_sem + 1], token=res[n_sem + 2])


def _exchange_wait(started, after, name):
    n_sem = 2 * N_OTHER_CHIPS

    def body(pb_ref, land_ref, *rest):
        for cp in _chip_copies(pb_ref, land_ref, rest[:n_sem]):
            cp.wait_send()
            cp.wait_recv()

    src, land = started["src"], started["land"]
    return pl.pallas_call(
        body, name=name, out_shape=(pltpu.HBM(src.shape, src.dtype), pltpu.HBM(land.shape, land.dtype)),
        in_specs=(HBM, HBM) + (SEMS,) * n_sem + (ANY,) * len(after), out_specs=(HBM, HBM), input_output_aliases={0: 0, 1: 1},
        compiler_params=pltpu.CompilerParams(has_side_effects=SIDE_EFFECT),
    )(src, land, *started["sems"], *after)[1]


def _ffn_fwd(x, gpre, wg, wu, wd, gpost, name, duties=(), target=None):
    tm = 256
    n_in = 6 if target is None else 7

    def body(*refs):
        x_ref, gpre_ref, wg_ref, wu_ref, wd_ref, gpost_ref = refs[:6]
        xo_ref, n_ref, a_ref, b_ref, hm_ref, h_ref = refs[n_in:n_in + 6]
        xv = x_ref[...]
        n = (xv * _rs(xv) * gpre_ref[...]).astype(BF16)
        a = _dot_nt(n, wg_ref[...])
        b = _dot_nt(n, wu_ref[...])
        hm = (a * _sigmoid(a) * b).astype(BF16)
        h = _dot(hm, wd_ref[...])
        xo = xv + 0.5 * (h * _rs(h) * gpost_ref[...])
        n_ref[...] = n
        a_ref[...] = a.astype(BF16)
        b_ref[...] = b.astype(BF16)
        hm_ref[...] = hm
        h_ref[...] = h
        if target is None:
            xo_ref[...] = xo
        else:
            ss_ref = refs[n_in + 6]

            @pl.when(pl.program_id(0) == 0)
            def _():
                ss_ref[...] = jnp.zeros_like(ss_ref)

            err = xo - refs[6][...]
            xo_ref[...] = err * (1.0 / D)
            ss_ref[...] += jnp.sum(jnp.sum(err * err, axis=1, keepdims=True), axis=0, keepdims=True)

    loss_in = [] if target is None else [_rows(tm, D)]
    loss_out = [] if target is None else [_const((1, 128))]
    loss_shape = [] if target is None else [jax.ShapeDtypeStruct((1, 128), F32)]
    return _call(
        body, name=name, grid=(T // tm,),
        in_specs=[_rows(tm, D), _const((1, D)), _resident((FFP, D)), _resident((FFP, D)), _resident((FFP, D)), _const((1, D))] + loss_in,
        out_specs=[_rows(tm, D), _rows(tm, D), _rows(tm, FFP), _rows(tm, FFP), _rows(tm, FFP), _rows(tm, D)] + loss_out,
        out_shape=[jax.ShapeDtypeStruct((T, D), F32), jax.ShapeDtypeStruct((T, D), BF16), jax.ShapeDtypeStruct((T, FFP), BF16),
                   jax.ShapeDtypeStruct((T, FFP), BF16), jax.ShapeDtypeStruct((T, FFP), BF16), jax.ShapeDtypeStruct((T, D), F32)]
        + loss_shape,
        args=[x, gpre, wg, wu, wd, gpost] + ([] if target is None else [target]), sem=("arbitrary",), duties=duties)


def _ffn_bwd(dxo, x, a, b, h, gpre, gpost, wg, wu, wd, name, after=()):
    tm = 256

    def body(dxo_ref, x_ref, a_ref, b_ref, h_ref, gpre_ref, gpost_ref, wg_ref, wu_ref, wd_ref,
             dx_ref, da_ref, db_ref, dh_ref, dgpre_ref, dgpost_ref):
        @pl.when(pl.program_id(0) == 0)
        def _():
            dgpre_ref[...] = jnp.zeros_like(dgpre_ref)
            dgpost_ref[...] = jnp.zeros_like(dgpost_ref)

        dy = dxo_ref[...]
        h = h_ref[...]
        hn = h * _rs(h)
        r2 = _rs(h)
        dgpost_ref[...] += jnp.sum(0.5 * dy * hn, axis=0, keepdims=True)
        gdy = 0.5 * dy * gpost_ref[...]
        dh = r2 * (gdy - hn * jnp.mean(gdy * hn, axis=-1, keepdims=True))
        dhb = dh.astype(BF16)
        dh_ref[...] = dhb
        dhm = _dot_nt(dhb, wd_ref[...])
        av = a_ref[...].astype(F32)
        bv = b_ref[...].astype(F32)
        sg = _sigmoid(av)
        db = (dhm * (av * sg)).astype(BF16)
        da = (dhm * bv * _dsilu(av, sg)).astype(BF16)
        da_ref[...] = da
        db_ref[...] = db
        dn = _dot(da, wg_ref[...]) + _dot(db, wu_ref[...])
        xv = x_ref[...]
        r = _rs(xv)
        xn = xv * r
        dgpre_ref[...] += jnp.sum(dn * xn, axis=0, keepdims=True)
        gdn = dn * gpre_ref[...]
        dx_ref[...] = dy + r * (gdn - xn * jnp.mean(gdn * xn, axis=-1, keepdims=True))

    return _call(
        body, name=name, grid=(T // tm,),
        in_specs=[_rows(tm, D), _rows(tm, D), _rows(tm, FFP), _rows(tm, FFP), _rows(tm, D), _const((1, D)), _const((1, D)),
                  _resident((FFP, D)), _resident((FFP, D)), _resident((FFP, D))],
        out_specs=[_rows(tm, D), _rows(tm, FFP), _rows(tm, FFP), _rows(tm, D), _const((1, D)), _const((1, D))],
        out_shape=[jax.ShapeDtypeStruct((T, D), F32), jax.ShapeDtypeStruct((T, FFP), BF16), jax.ShapeDtypeStruct((T, FFP), BF16),
                   jax.ShapeDtypeStruct((T, D), BF16), jax.ShapeDtypeStruct((1, D), F32), jax.ShapeDtypeStruct((1, D), F32)],
        args=[dxo, x, a, b, h, gpre, gpost, wg, wu, wd], sem=("arbitrary",), after=after)


def _core_index(core):
    return lax.axis_index("c") if core == "mesh" else core


def _by_core(res, o_ref, ob_ref, core):
    r = res.shape[0] // 2
    c = jnp.asarray(_core_index(core))

    @pl.when(c == 0)
    def _():
        o_ref[0] = res[:r]
        ob_ref[0] = res[r:].astype(BF16)

    @pl.when(c == 1)
    def _():
        o_ref[0] = res[r:]
        ob_ref[0] = res[:r].astype(BF16)


def _matmul_tn(a, b, name, core, after=(), duties=()):
    k, m = a.shape
    n = b.shape[1]
    r = m // N_DEV
    assert m == N_DEV * r and r % 128 == 0

    def body(a_ref, b_ref, o_ref, ob_ref):
        _by_core(_dot_tn(a_ref[...], b_ref[...]), o_ref, ob_ref, core)

    spec = pl.BlockSpec((1, r, n), lambda i: (i, 0, 0))
    return _call(body, name=name, grid=(N_DEV // 2,), in_specs=[pl.BlockSpec((k, 2 * r), lambda i: (0, i)), _resident((k, n))],
                 out_specs=[spec, spec],
                 out_shape=[jax.ShapeDtypeStruct((N_DEV // 2, r, n), F32), jax.ShapeDtypeStruct((N_DEV // 2, r, n), BF16)],
                 args=[a, b], sem=("arbitrary",), duties=duties, after=after)


def _dwout(attn, yn, dh2, core):
    rs = (QW + SSM_W) // N_DEV
    chips = N_DEV // 2

    def body(at_ref, yn_ref, dh_ref, o_ref, ob_ref):
        i = pl.program_id(0)

        @pl.when(i < chips // 2)
        def _():
            _by_core(_dot_tn(at_ref[...], dh_ref[...]), o_ref, ob_ref, core)

        @pl.when(i >= chips // 2)
        def _():
            _by_core(_dot_tn(yn_ref[...], dh_ref[...]), o_ref, ob_ref, core)

    spec = pl.BlockSpec((1, rs, D), lambda i: (i, 0, 0))
    return pl.pallas_call(
        body, name="dwout", grid=(chips,),
        in_specs=[pl.BlockSpec((T, 2 * rs), lambda i: (0, jnp.minimum(i, chips // 2 - 1))),
                  pl.BlockSpec((T, 2 * rs), lambda i: (0, jnp.maximum(i - chips // 2, 0))), _resident((T, D))],
        out_specs=[spec, spec],
        out_shape=[jax.ShapeDtypeStruct((chips, rs, D), F32), jax.ShapeDtypeStruct((chips, rs, D), BF16)],
        compiler_params=_params(("arbitrary",)),
    )(attn, yn, dh2)


def _rope_swap(t, lo_half):
    return jnp.where(lo_half, pltpu.roll(t, 96, 1), pltpu.roll(t, 32, 1))


def _inproj_fwd(x1, gpre, win, cos, sin_s, ex, duties=()):
    tm = 256

    def body(x_ref, g_ref, w_ref, cos_ref, sin_ref, ex_ref, n_ref, q_ref, kx_ref, vx_ref, xbc_ref, z_ref, dt_ref):
        xv = x_ref[...]
        n = (xv * _rs(xv) * g_ref[...]).astype(BF16)
        n_ref[...] = n
        by_dev = _dot_nt(n, w_ref[...])
        proj = jnp.concatenate([by_dev[:, ISW * d:ISW * d + ISR] for d in range(N_DEV)], axis=1)
        cs = cos_ref[...]
        sn = sin_ref[...]
        lo_half = (lax.broadcasted_iota(jnp.int32, (1, 128), 1) % HD) < (HD // 2)

        def rope(t):
            return t * cs + _rope_swap(t, lo_half) * sn

        for j in range(QW // 128):
            t = proj[:, 128 * j:128 * j + 128]
            q_ref[:, 128 * j:128 * j + 128] = (rope(t) * (HD ** -0.5)).astype(BF16)
        k = jnp.concatenate([rope(proj[:, QW + 128 * j:QW + 128 * j + 128]) for j in range(KVW // 128)], axis=1)
        v = proj[:, QW + KVW:QW + 2 * KVW]
        kx_ref[...] = _dot(k.astype(BF16), ex_ref[...]).astype(BF16)
        vx_ref[...] = _dot(v.astype(BF16), ex_ref[...]).astype(BF16)
        c0 = QW + 2 * KVW
        xbc_ref[...] = proj[:, c0:c0 + CONV_C]
        z_ref[...] = proj[:, c0 + CONV_C:c0 + CONV_C + SSM_W]
        dt_ref[...] = proj[:, c0 + CONV_C + SSM_W:IN_COLS]

    return _call(
        body, name="inproj_fwd", grid=(T // tm,),
        in_specs=[_rows(tm, D), _const((1, D)), _resident((INP, D)), _rows(tm, 128), _rows(tm, 128), _const((KVW, QW))],
        out_specs=[_rows(tm, D), _rows(tm, QW), _rows(tm, QW), _rows(tm, QW), _rows(tm, CONV_C), _rows(tm, SSM_W), _rows(tm, SSM_H)],
        out_shape=[jax.ShapeDtypeStruct((T, D), BF16), jax.ShapeDtypeStruct((T, QW), BF16), jax.ShapeDtypeStruct((T, QW), BF16),
                   jax.ShapeDtypeStruct((T, QW), BF16), jax.ShapeDtypeStruct((T, CONV_C), F32), jax.ShapeDtypeStruct((T, SSM_W), F32),
                   jax.ShapeDtypeStruct((T, SSM_H), F32)],
        args=[x1, gpre, win, cos, sin_s, ex], sem=("arbitrary",), duties=duties)


def _inproj_bwd(dres, dq, dkx, dvx, dxbc, dz, ddt, x1, gpre, win, cos, sin_s, exf):
    tm = 256

    def body(dres_ref, dq_ref, dkx_ref, dvx_ref, dxbc_ref, dz_ref, ddt_ref, x_ref, g_ref, w_ref, cos_ref, sin_ref, ex_ref,
             dx_ref, dps_ref, dg_ref, dp_ref):
        @pl.when(pl.program_id(0) == 0)
        def _():
            dg_ref[...] = jnp.zeros_like(dg_ref)

        cs = cos_ref[...]
        sn = sin_ref[...]
        lo_half = (lax.broadcasted_iota(jnp.int32, (1, 128), 1) % HD) < (HD // 2)

        def rope_t(t):
            return t * cs - _rope_swap(t, lo_half) * sn

        for j in range(QW // 128):
            dp_ref[:, 128 * j:128 * j + 128] = rope_t(dq_ref[:, 128 * j:128 * j + 128] * (HD ** -0.5)).astype(BF16)
        dk = _dot_nt_hi(dkx_ref[...], ex_ref[...])
        dv = _dot_nt_hi(dvx_ref[...], ex_ref[...])
        for j in range(KVW // 128):
            dp_ref[:, QW + 128 * j:QW + 128 * j + 128] = rope_t(dk[:, 128 * j:128 * j + 128]).astype(BF16)
        dp_ref[:, QW + KVW:QW + 2 * KVW] = dv.astype(BF16)
        c0 = QW + 2 * KVW
        dp_ref[:, c0:c0 + CONV_C] = dxbc_ref[...].astype(BF16)
        dp_ref[:, c0 + CONV_C:c0 + CONV_C + SSM_W] = dz_ref[...].astype(BF16)
        dp_ref[:, c0 + CONV_C + SSM_W:INP] = ddt_ref[...].astype(BF16)
        pieces = [dp_ref[:, ISR * d:ISR * (d + 1)] for d in range(N_DEV)]
        zw = jnp.zeros((tm, ISW - ISR), BF16)
        zg = jnp.zeros((tm, ISG - ISR), BF16)
        dn = _dot(jnp.concatenate([t for p in pieces for t in (p, zw)], axis=1), w_ref[...])
        for d in range(N_DEV):
            dps_ref[:, ISG * d:ISG * (d + 1)] = jnp.concatenate([pieces[d], zg], axis=1)
        xv = x_ref[...]
        r = _rs(xv)
        xn = xv * r
        dg_ref[...] += jnp.sum(dn * xn, axis=0, keepdims=True)
        gdn = dn * g_ref[...]
        dx_ref[...] = dres_ref[...] + r * (gdn - xn * jnp.mean(gdn * xn, axis=-1, keepdims=True))

    return pl.pallas_call(
        body, name="inproj_bwd", grid=(T // tm,),
        in_specs=[_rows(tm, D), _rows(tm, QW), _rows(tm, QW), _rows(tm, QW), _rows(tm, CONV_C), _rows(tm, SSM_W), _rows(tm, 128),
                  _rows(tm, D), _const((1, D)), _resident((INP, D)), _rows(tm, 128), _rows(tm, 128), _const((KVW, QW))],
        out_specs=[_rows(tm, D), _rows(tm, N_DEV * ISG), _const((1, D))],
        out_shape=[jax.ShapeDtypeStruct((T, D), F32), jax.ShapeDtypeStruct((T, N_DEV * ISG), BF16), jax.ShapeDtypeStruct((1, D), F32)],
        scratch_shapes=[pltpu.VMEM((tm, INP), BF16)],
        compiler_params=_params(("arbitrary",)),
    )(dres, dq, dkx, dvx, dxbc, dz, ddt, x1, gpre, win, cos, sin_s, exf)


def _outproj_fwd(x1, attn, yn, wout, gpost):
    tm = 512

    def body(x_ref, at_ref, yn_ref, w_ref, g_ref, xo_ref, h_ref):
        h = _dot(at_ref[...], w_ref[0:QW, :]) + _dot(yn_ref[...], w_ref[QW:QW + SSM_W, :])
        h_ref[...] = h
        xo_ref[...] = x_ref[...] + h * _rs(h) * g_ref[...]

    return pl.pallas_call(
        body, name="outproj_fwd", grid=(T // tm,),
        in_specs=[_rows(tm, D), _rows(tm, QW), _rows(tm, SSM_W), _resident((QW + SSM_W, D)), _const((1, D))],
        out_specs=[_rows(tm, D), _rows(tm, D)],
        out_shape=[jax.ShapeDtypeStruct((T, D), F32), jax.ShapeDtypeStruct((T, D), F32)],
        compiler_params=_params(("parallel",)),
    )(x1, attn, yn, wout, gpost)


def _outproj_bwd(dx2, h2, gpost, wout, duties=()):
    tm = 512

    def body(dy_ref, h_ref, g_ref, w_ref, dh_ref, dm_ref, dg_ref):
        @pl.when(pl.program_id(0) == 0)
        def _():
            dg_ref[...] = jnp.zeros_like(dg_ref)

        dy = dy_ref[...]
        h = h_ref[...]
        r = _rs(h)
        hn = h * r
        dg_ref[...] += jnp.sum(dy * hn, axis=0, keepdims=True)
        gdy = dy * g_ref[...]
        dh = (r * (gdy - hn * jnp.mean(gdy * hn, axis=-1, keepdims=True))).astype(BF16)
        dh_ref[...] = dh
        dm_ref[...] = _dot_nt(dh, w_ref[...])

    return _call(
        body, name="outproj_bwd", grid=(T // tm,),
        in_specs=[_rows(tm, D), _rows(tm, D), _const((1, D)), _resident((QW + SSM_W, D))],
        out_specs=[_rows(tm, D), _rows(tm, QW + SSM_W), _const((1, D))],
        out_shape=[jax.ShapeDtypeStruct((T, D), BF16), jax.ShapeDtypeStruct((T, QW + SSM_W), F32), jax.ShapeDtypeStruct((1, D), F32)],
        args=[dx2, h2, gpost, wout], sem=("arbitrary",), duties=duties)


def _attn_bias():
    d = jnp.arange(AB)[:, None] - jnp.arange(T)[None, :] + (T - AB)
    cnt = jnp.zeros(d.shape, F32)
    for window, dil in DILATIONS:
        cnt = cnt + ((d >= 0) & (d % dil == 0) & (d <= window)).astype(F32)
    return jnp.where(cnt > 0, jnp.log(jnp.maximum(cnt, 1.0)), NEG)


G_PER = NQ // NKV
WK = G_PER * HD


def _attn_fwd(q, kx, vx, bias, duties=()):
    def body(q_ref, kx_ref, vx_ref, bias_ref, o_ref, lse_ref):
        lane = lax.broadcasted_iota(jnp.int32, (1, WK), 1)
        lse_ref[...] = jnp.zeros_like(lse_ref)
        for i in range(NAB):
            n = (i + 1) * AB
            rows = slice(i * AB, n)
            qi = q_ref[rows, :]
            kxi = kx_ref[0:n, :]
            vxi = vx_ref[0:n, :]
            bb = bias_ref[:, (NAB - 1 - i) * AB:]
            o_acc = jnp.zeros((AB, WK), F32)
            for g in range(G_PER):
                mg = (lane // HD) == g
                s = _dot_nt(jnp.where(mg, qi, jnp.zeros_like(qi)), kxi) + bb
                m = jnp.max(s, axis=1, keepdims=True)
                p = jnp.exp(s - m)
                l = jnp.sum(p, axis=1, keepdims=True)
                o_acc = jnp.where(mg, _dot(p.astype(BF16), vxi) / l, o_acc)
                lse_ref[rows, g:g + 1] = m + jnp.log(l)
            o_ref[rows, :] = o_acc.astype(BF16)

    col = lambda kv: (0, kv)
    return _call(
        body, name="attn_fwd", grid=(NKV,),
        in_specs=[pl.BlockSpec((T, WK), col), pl.BlockSpec((T, WK), col), pl.BlockSpec((T, WK), col), _const((AB, T))],
        out_specs=[pl.BlockSpec((T, WK), col), pl.BlockSpec((T, 128), col)],
        out_shape=[jax.ShapeDtypeStruct((T, QW), BF16), jax.ShapeDtypeStruct((T, NKV * 128), F32)],
        args=[q, kx, vx, bias], sem=("arbitrary",), duties=duties)


def _attn_bwd(q, kx, vx, o, dmix, lse, bias, duties=()):
    def body(q_ref, kx_ref, vx_ref, o_ref, do_ref, lse_ref, bias_ref, dq_ref, dkx_ref, dvx_ref):
        lane = lax.broadcasted_iota(jnp.int32, (1, WK), 1)
        dkx_ref[...] = jnp.zeros_like(dkx_ref)
        dvx_ref[...] = jnp.zeros_like(dvx_ref)
        for i in range(NAB):
            n = (i + 1) * AB
            rows = slice(i * AB, n)
            qi = q_ref[rows, :]
            dof = do_ref[rows, :]
            doi = dof.astype(BF16)
            prod = dof * o_ref[rows, :].astype(F32)
            kxi = kx_ref[0:n, :]
            vxi = vx_ref[0:n, :]
            bb = bias_ref[:, (NAB - 1 - i) * AB:]
            dq_acc = jnp.zeros((AB, WK), F32)
            for g in range(G_PER):
                mg = (lane // HD) == g
                qm = jnp.where(mg, qi, jnp.zeros_like(qi))
                dom = jnp.where(mg, doi, jnp.zeros_like(doi))
                delta = jnp.sum(jnp.where(mg, prod, 0.0), axis=1, keepdims=True)
                p = jnp.exp(_dot_nt(qm, kxi) + bb - lse_ref[rows, g:g + 1])
                ds = (p * (_dot_nt(dom, vxi) - delta)).astype(BF16)
                dvx_ref[0:n, :] += _dot_tn(p.astype(BF16), dom)
                dkx_ref[0:n, :] += _dot_tn(ds, qm)
                dq_acc = jnp.where(mg, _dot(ds, kxi), dq_acc)
            dq_ref[rows, :] = dq_acc

    col = lambda kv: (0, kv)
    return _call(
        body, name="attn_bwd", grid=(NKV,),
        in_specs=[pl.BlockSpec((T, WK), col), pl.BlockSpec((T, WK), col), pl.BlockSpec((T, WK), col), pl.BlockSpec((T, WK), col),
                  pl.BlockSpec((T, WK), col), pl.BlockSpec((T, 128), col), _const((AB, T))],
        out_specs=[pl.BlockSpec((T, WK), col), pl.BlockSpec((T, WK), col), pl.BlockSpec((T, WK), col)],
        out_shape=[jax.ShapeDtypeStruct((T, QW), F32)] * 3,
        args=[q, kx, vx, o, dmix, lse, bias], sem=("arbitrary",), duties=duties)


def _softplus(x):
    return jnp.maximum(x, 0.0) + jnp.log1p(jnp.exp(-jnp.abs(x)))


def _causal_conv(u, zs, cw_ref, cb_ref):
    zs[8:, :] = u
    sh1, sh2, sh3 = (zs[8 - m:8 - m + L, :] for m in (1, 2, 3))
    return cb_ref[...] + cw_ref[3:4, :] * u + cw_ref[2:3, :] * sh1 + cw_ref[1:2, :] * sh2 + cw_ref[0:1, :] * sh3


def _ssd_chunk_common(xc, dtr, dtb_ref, alx_ref, e_ref):
    sg = _sigmoid(xc)
    act = xc * sg
    pre = dtr + dtb_ref[...]
    dt_x = _dot_hi(_softplus(pre), e_ref[...])
    a_x = -jnp.exp(alx_ref[...])
    ri = lax.broadcasted_iota(jnp.int32, (L, L), 0)
    ci = lax.broadcasted_iota(jnp.int32, (L, L), 1)
    tri = ri >= ci
    acs_x = _dot_hi(tri, dt_x * a_x, a_is_01=True)
    return dict(sg=sg, act=act, pre=pre, dt_x=dt_x, a_x=a_x, tri=tri, acs_x=acs_x)


def _decay(acs_x, acs_t, h, tri):
    col = acs_x[:, HD * h:HD * h + 1]
    row = acs_t[HD * h:HD * h + 1, :]
    return jnp.exp(jnp.where(tri, col - row, NEG))


def _ssd_fwd(xbc, z, dtr, convw, convb, dtb, alx, dskx, ssmn, e, duties=()):
    def body(u_ref, z_ref, dtr_ref, cw_ref, cb_ref, dtb_ref, alx_ref, dsk_ref, sn_ref, e_ref,
             yn_ref, y_ref, hs_ref, xc_ref, zs, hst):
        @pl.when(pl.program_id(0) == 0)
        def _():
            zs[0:8, :] = jnp.zeros((8, CONV_C), F32)
            hst[...] = jnp.zeros_like(hst)

        u = u_ref[...]
        xc = _causal_conv(u, zs, cw_ref, cb_ref)
        xc_ref[...] = xc
        zs[0:8, :] = u[L - 8:, :]
        cm = _ssd_chunk_common(xc, dtr_ref[...], dtb_ref, alx_ref, e_ref)
        act, dt_x, acs_x, tri = cm["act"], cm["dt_x"], cm["acs_x"], cm["tri"]
        xs = act[:, :SSM_W]
        acs_l = acs_x[L - 1:L, :]
        lam_x = jnp.exp(acs_x)
        w_x = jnp.exp(acs_l - acs_x)
        gam_x = jnp.exp(acs_l)
        acs_t = acs_x.T
        xd = xs * dt_x
        xb = xd.astype(BF16)
        xw = (xd * w_x).astype(BF16)
        lo = lax.broadcasted_iota(jnp.int32, (1, 128), 1) < HD
        hs_ref[0] = hst[...]
        pieces = []
        for grp in range(2):
            bb = act[:, SSM_W + SSM_N * grp:SSM_W + SSM_N * (grp + 1)].astype(BF16)
            cb_ = act[:, SSM_W + 2 * SSM_N + SSM_N * grp:SSM_W + 2 * SSM_N + SSM_N * (grp + 1)].astype(BF16)
            cbm = _dot_nt(cb_, bb)
            for jj in range(4):
                j = 4 * grp + jj
                sl = slice(128 * j, 128 * j + 128)
                m0 = (cbm * _decay(acs_x, acs_t, 2 * j, tri)).astype(BF16)
                m1 = (cbm * _decay(acs_x, acs_t, 2 * j + 1, tri)).astype(BF16)
                x2 = xb[:, sl]
                ydiag = jnp.where(lo, _dot(m0, x2), _dot(m1, x2))
                hprev = hst[j]
                yoff = lam_x[:, sl] * _dot(cb_, hprev.astype(BF16))
                pieces.append(ydiag + yoff)
                hst[j] = gam_x[:, sl] * hprev + _dot_tn(bb, xw[:, sl])
        y = jnp.concatenate(pieces, axis=1) + dsk_ref[...] * xs
        y_ref[...] = y
        zv = z_ref[...]
        yz = y * (zv * _sigmoid(zv))
        half = SSM_W // 2
        yn = jnp.concatenate([yz[:, :half] * _rs(yz[:, :half]), yz[:, half:] * _rs(yz[:, half:])], axis=1)
        yn_ref[...] = (yn * sn_ref[...]).astype(BF16)

    return _call(
        body, name="ssd_fwd", grid=(NCH,),
        in_specs=[_rows(L, CONV_C), _rows(L, SSM_W), _rows(L, SSM_H), _const((4, CONV_C)), _const((1, CONV_C)), _const((1, SSM_H)),
                  _const((1, SSM_W)), _const((1, SSM_W)), _const((1, SSM_W)), _const((SSM_H, SSM_W))],
        out_specs=[_rows(L, SSM_W), _rows(L, SSM_W), pl.BlockSpec((1, 8, SSM_N, 128), lambda c: (c, 0, 0, 0)), _rows(L, CONV_C)],
        out_shape=[jax.ShapeDtypeStruct((T, SSM_W), BF16), jax.ShapeDtypeStruct((T, SSM_W), F32),
                   jax.ShapeDtypeStruct((NCH, 8, SSM_N, 128), F32), jax.ShapeDtypeStruct((T, CONV_C), F32)],
        scratch=[pltpu.VMEM((8 + L, CONV_C), F32), pltpu.VMEM((8, SSM_N, 128), F32)],
        args=[xbc, z, dtr, convw, convb, dtb, alx, dskx, ssmn, e], sem=("arbitrary",), duties=duties)


def _ssd_bwd(dmix, xbc, xconv, z, dtr, y, hs, convw, dtb, alx, dskx, ssmn, e, e1, duties=()):
    rev = lambda i: (NCH - 1 - i, 0)

    def body(dyn_ref, u_ref, xc_ref, z_ref, dtr_ref, y_ref, hs_ref, cw_ref, dtb_ref, alx_ref, dsk_ref, sn_ref, e_ref, e1_ref,
             dxbc_ref, dz_ref, ddt_ref, dcw_ref, dcb_ref, dsn_ref, dpar_ref, dh, zd, colbuf):
        step = pl.program_id(0)

        @pl.when(step == 0)
        def _():
            for r in (dh, dcw_ref, dcb_ref, dsn_ref, dpar_ref):
                r[...] = jnp.zeros_like(r)
            zd[L:, :] = jnp.zeros((8, CONV_C), F32)

        u = u_ref[...]
        xc = xc_ref[...]
        cm = _ssd_chunk_common(xc, dtr_ref[...], dtb_ref, alx_ref, e_ref)
        sg, act, pre, dt_x, a_x, tri, acs_x = (cm[k] for k in ("sg", "act", "pre", "dt_x", "a_x", "tri", "acs_x"))
        xs = act[:, :SSM_W]
        acs_l = acs_x[L - 1:L, :]
        lam_x = jnp.exp(acs_x)
        w_x = jnp.exp(acs_l - acs_x)
        gam_x = jnp.exp(acs_l)
        acs_t = acs_x.T
        xd = xs * dt_x
        xb = xd.astype(BF16)
        xdw = xd * w_x
        xw = xdw.astype(BF16)
        lo = lax.broadcasted_iota(jnp.int32, (1, 128), 1) < HD
        row8 = lax.broadcasted_iota(jnp.int32, (8, 1), 0)

        dyn = dyn_ref[...]
        yv = y_ref[...]
        zv = z_ref[...]
        sz = _sigmoid(zv)
        siluz = zv * sz
        yz = yv * siluz
        half = SSM_W // 2
        gy = dyn * sn_ref[...]
        dyz_parts, yzn_parts = [], []
        for hf in range(2):
            part = yz[:, hf * half:(hf + 1) * half]
            r = _rs(part)
            pn = part * r
            gp = gy[:, hf * half:(hf + 1) * half]
            dyz_parts.append(r * (gp - pn * jnp.mean(gp * pn, axis=-1, keepdims=True)))
            yzn_parts.append(pn)
        dyz = jnp.concatenate(dyz_parts, axis=1)
        dsn_ref[...] += jnp.sum(dyn * jnp.concatenate(yzn_parts, axis=1), axis=0, keepdims=True)
        dy = dyz * siluz
        dz_ref[...] = dyz * yv * _dsilu(zv, sz)

        colbuf[...] = jnp.zeros_like(colbuf)
        dx_pieces, dacs_pieces, dacsl_pieces, db_pieces, dc_pieces = [], [], [], [], []
        for grp in range(2):
            bb = act[:, SSM_W + SSM_N * grp:SSM_W + SSM_N * (grp + 1)].astype(BF16)
            cb_ = act[:, SSM_W + 2 * SSM_N + SSM_N * grp:SSM_W + 2 * SSM_N + SSM_N * (grp + 1)].astype(BF16)
            cbm = _dot_nt(cb_, bb)
            dcbm = jnp.zeros((L, L), F32)
            dc_g = jnp.zeros((L, SSM_N), F32)
            db_g = jnp.zeros((L, SSM_N), F32)
            for jj in range(4):
                j = 4 * grp + jj
                sl = slice(128 * j, 128 * j + 128)
                dy2 = dy[:, sl]
                dy2b = dy2.astype(BF16)
                d0 = _decay(acs_x, acs_t, 2 * j, tri)
                d1 = _decay(acs_x, acs_t, 2 * j + 1, tri)
                m0 = cbm * d0
                m1 = cbm * d1
                x2 = xb[:, sl]
                hprev = hs_ref[0, j]
                hprevb = hprev.astype(BF16)
                dhn = dh[j]
                dhnb = dhn.astype(BF16)
                g2 = _dot(bb, dhnb)
                dx_pieces.append(jnp.where(lo, _dot_tn(m0.astype(BF16), dy2b), _dot_tn(m1.astype(BF16), dy2b)) + w_x[:, sl] * g2)
                zero = jnp.zeros_like(dy2b)
                dm0 = _dot_nt(jnp.where(lo, dy2b, zero), x2)
                dm1 = _dot_nt(jnp.where(lo, zero, dy2b), x2)
                dcbm = dcbm + dm0 * d0 + dm1 * d1
                e0 = dm0 * m0
                e1v = dm1 * m1
                colbuf[:, 2 * j:2 * j + 1] = jnp.sum(e0, axis=1, keepdims=True) - jnp.sum(e0.T, axis=1, keepdims=True)
                colbuf[:, 2 * j + 1:2 * j + 2] = jnp.sum(e1v, axis=1, keepdims=True) - jnp.sum(e1v.T, axis=1, keepdims=True)
                yoff = lam_x[:, sl] * _dot(cb_, hprevb)
                gxw = g2 * xdw[:, sl]
                dacs_pieces.append(dy2 * yoff - gxw)
                dacsl_pieces.append(jnp.sum(gxw, axis=0, keepdims=True) + gam_x[:, sl] * jnp.sum(dhn * hprev, axis=0, keepdims=True))
                dyl = (dy2 * lam_x[:, sl]).astype(BF16)
                dc_g = dc_g + _dot_nt(dyl, hprevb)
                db_g = db_g + _dot_nt(xw[:, sl], dhnb)
                dh[j] = gam_x[:, sl] * dhn + _dot_tn(cb_, dyl)
            dcbb = dcbm.astype(BF16)
            dc_pieces.append(dc_g + _dot(dcbb, bb))
            db_pieces.append(db_g + _dot_tn(dcbb, cb_))

        dxd = jnp.concatenate(dx_pieces, axis=1)
        rowi = lax.broadcasted_iota(jnp.int32, (L, 1), 0)
        dacs_x = (jnp.concatenate(dacs_pieces, axis=1) + _dot_hi(colbuf[...], e1_ref[...])
                  + jnp.where(rowi == L - 1, jnp.concatenate(dacsl_pieces, axis=1), 0.0))
        upper = lax.broadcasted_iota(jnp.int32, (L, L), 0) <= lax.broadcasted_iota(jnp.int32, (L, L), 1)
        dadt_x = _dot_hi(upper, dacs_x, a_is_01=True)
        ddt_x = dxd * xs + dadt_x * a_x
        ddtr = _dot_nt_hi(ddt_x, e_ref[...]) * _sigmoid(pre)
        ddt_ref[...] = jnp.zeros_like(ddt_ref)
        ddt_ref[:, 0:SSM_H] = ddtr
        dalx =jnp.sum(dadt_x * dt_x, axis=0, keepdims=True) * a_x
        ddskx = jnp.sum(dy * xs, axis=0, keepdims=True)
        par_x = jnp.where(row8 == 1, dalx, 0.0) + jnp.where(row8 == 2, ddskx, 0.0)
        dpar_ref[...] += _dot_nt_hi(par_x, e_ref[...]) + jnp.where(row8 == 0, jnp.sum(ddtr, axis=0, keepdims=True), 0.0)

        dxs = dxd * dt_x + dsk_ref[...] * dy
        dact = jnp.concatenate([dxs] + db_pieces + dc_pieces, axis=1)
        du = dact * _dsilu(xc, sg)
        dcb_ref[...] += jnp.sum(du, axis=0, keepdims=True)
        zd[0:L, :] = du
        f1, f2, f3 = (zd[m:m + L, :] for m in (1, 2, 3))
        dxbc_ref[...] = cw_ref[3:4, :] * du + cw_ref[2:3, :] * f1 + cw_ref[1:2, :] * f2 + cw_ref[0:1, :] * f3
        dcw = jnp.zeros((8, CONV_C), F32)
        for k, shifted in enumerate((f3, f2, f1, du)):
            dcw = dcw + jnp.where(row8 == k, jnp.sum(shifted * u, axis=0, keepdims=True), 0.0)
        dcw_ref[...] += dcw
        zd[L:, :] = du[:8, :]

    return _call(
        body, name="ssd_bwd", grid=(NCH,),
        in_specs=[pl.BlockSpec((L, SSM_W), lambda i: (NCH - 1 - i, 1)), pl.BlockSpec((L, CONV_C), rev), pl.BlockSpec((L, CONV_C), rev),
                  pl.BlockSpec((L, SSM_W), rev), pl.BlockSpec((L, SSM_H), rev), pl.BlockSpec((L, SSM_W), rev),
                  pl.BlockSpec((1, 8, SSM_N, 128), lambda i: (NCH - 1 - i, 0, 0, 0)),
                  _const((4, CONV_C)), _const((1, SSM_H)), _const((1, SSM_W)), _const((1, SSM_W)), _const((1, SSM_W)),
                  _const((SSM_H, SSM_W)), _const((128, SSM_W))],
        out_specs=[pl.BlockSpec((L, CONV_C), rev), pl.BlockSpec((L, SSM_W), rev), pl.BlockSpec((L, 128), rev),
                   _const((8, CONV_C)), _const((1, CONV_C)), _const((1, SSM_W)), _const((8, SSM_H))],
        out_shape=[jax.ShapeDtypeStruct((T, CONV_C), F32), jax.ShapeDtypeStruct((T, SSM_W), F32), jax.ShapeDtypeStruct((T, 128), F32),
                   jax.ShapeDtypeStruct((8, CONV_C), F32), jax.ShapeDtypeStruct((1, CONV_C), F32), jax.ShapeDtypeStruct((1, SSM_W), F32),
                   jax.ShapeDtypeStruct((8, SSM_H), F32)],
        scratch=[pltpu.VMEM((8, SSM_N, 128), F32), pltpu.VMEM((L + 8, CONV_C), F32), pltpu.VMEM((L, 128), F32)],
        args=[dmix, xbc, xconv, z, dtr, y, hs, convw, dtb, alx, dskx, ssmn, e, e1], sem=("arbitrary",), duties=duties)


def _adam_math(w, g, m, v):
    m = ADAM_B1 * m + (1.0 - ADAM_B1) * g
    v = ADAM_B2 * v + (1.0 - ADAM_B2) * (g * g)
    m_hat = m / (1.0 - ADAM_B1 ** ADAM_STEP)
    v_hat = v / (1.0 - ADAM_B2 ** ADAM_STEP)
    delta = -ADAM_LR * (m_hat / (jnp.sqrt(v_hat) + ADAM_EPS) + ADAM_WD * w)
    return delta, m, v


def _adamw(w, m, v, parts, name, after=()):
    rows, cols = w.shape
    tr = rows if rows <= 512 else 256
    assert rows % tr == 0
    n_parts = len(parts)

    def body(*refs):
        w_ref, m_ref, v_ref = refs[:3]
        p_refs = refs[3:3 + n_parts]
        g_ref, d_ref, nm_ref, nv_ref = refs[3 + n_parts + len(after):]
        g = p_refs[0][...].astype(F32)
        for p in p_refs[1:]:
            g = g + p[...].astype(F32)
        delta, nm, nv = _adam_math(w_ref[...], g, m_ref[...], v_ref[...])
        g_ref[...] = g
        d_ref[...] = delta
        nm_ref[...] = nm
        nv_ref[...] = nv

    spec = pl.BlockSpec((tr, cols), lambda i: (i, 0))
    return pl.pallas_call(
        body, name=name, grid=(rows // tr,),
        in_specs=[spec] * (3 + n_parts) + [ANY] * len(after), out_specs=[spec] * 4,
        out_shape=[jax.ShapeDtypeStruct((rows, cols), F32)] * 4,
        compiler_params=_params(("parallel",)),
    )(w, m, v, *parts, *after)


COL_TILE = 512


def _adamw_sharded(ws, ms, vs, chip_sums, from_chips, other_chips, name, after=()):
    k = len(ws)
    rows, cols = ws[0].shape
    prow = chip_sums[0].shape[0]
    assert cols % COL_TILE == 0 and prow >= rows and all(a.shape == ws[0].shape for a in ws)

    def body(ids_ref, *refs):
        ins, outs = refs[:7 * k], refs[7 * k + len(after):]
        for a in range(k):
            w_ref, m_ref, v_ref, s_ref, r1_ref, r2_ref, r3_ref = ins[7 * a:7 * a + 7]
            g = s_ref[...]
            for r in (r1_ref, r2_ref, r3_ref):
                g = g + r[0].astype(F32)
            g = g[:rows]
            delta, nm, nv = _adam_math(w_ref[...], g, m_ref[...], v_ref[...])
            for ref, val in zip(outs[4 * a:4 * a + 4], (g, delta, nm, nv)):
                ref[...] = val

    spec = pl.BlockSpec((rows, COL_TILE), lambda i, ids: (0, i))
    part = lambda j: pl.BlockSpec((1, prow, COL_TILE), lambda i, ids: (ids[j], 0, i))
    one = [spec, spec, spec, pl.BlockSpec((prow, COL_TILE), lambda i, ids: (0, i)), part(0), part(1), part(2)]
    args = [x for a in range(k) for x in (ws[a], ms[a], vs[a], chip_sums[a], from_chips[a], from_chips[a], from_chips[a])]
    res = pl.pallas_call(
        body, name=name,
        grid_spec=pltpu.PrefetchScalarGridSpec(
            num_scalar_prefetch=1, grid=(cols // COL_TILE,),
            in_specs=one * k + [ANY] * len(after), out_specs=[spec] * (4 * k)),
        out_shape=[jax.ShapeDtypeStruct((rows, cols), F32)] * (4 * k),
        compiler_params=_params(("parallel",)),
    )(other_chips, *args, *after)
    return [tuple(res[4 * a:4 * a + 4]) for a in range(k)]


def _chip_sum(mines, recvs, name):
    k = len(mines)
    rows, cols = mines[0].shape[1:]

    def body(*refs):
        own_chip = pl.program_id(0) == 2 * lax.axis_index("x") + lax.axis_index("y")
        for a in range(k):
            a_ref, b_ref = refs[2 * a:2 * a + 2]
            s_ref, sb_ref = refs[2 * k + 2 * a:2 * k + 2 * a + 2]
            s = a_ref[0] + b_ref[0].astype(F32)
            sb_ref[0] = s.astype(BF16)

            @pl.when(own_chip)
            def _(s_ref=s_ref, s=s):
                s_ref[...] = s

    by_chip = pl.BlockSpec((1, rows, cols), lambda c: (c, 0, 0))
    res = pl.pallas_call(
        body, name=name, grid=(N_DEV // 2,),
        in_specs=[by_chip, by_chip] * k, out_specs=[_const((rows, cols)), by_chip] * k,
        out_shape=[jax.ShapeDtypeStruct((rows, cols), F32), jax.ShapeDtypeStruct((N_DEV // 2, rows, cols), BF16)] * k,
        compiler_params=_params(("arbitrary",)),
    )(*[x for pair in zip(mines, recvs) for x in pair])
    return [tuple(res[2 * a:2 * a + 2]) for a in range(k)]


def _all_reduce_small(v, after=()):
    rows = v.shape[0]

    def body(v_ref, *rest):
        out_ref, gath, send_sems, recv_sems = rest[len(after):]
        x, y, c = _place()
        me, sibling = (x, y, c), (x, y, 1 - c)
        chips = [(1 - x, y), (x, 1 - y), (1 - x, 1 - y)]

        def blk(px, py, pc):
            return gath.at[pl.ds((4 * px + 2 * py + pc) * rows, rows), :]

        def copy(k, block, to, src=None):
            return pltpu.make_async_remote_copy(src_ref=blk(*block) if src is None else src, dst_ref=blk(*block),
                                                send_sem=send_sems.at[k], recv_sem=recv_sems.at[k], device_id=to, device_id_type=MESH)

        gath[pl.ds((4 * x + 2 * y + c) * rows, rows), :] = v_ref[...]
        first = [copy(0, me, sibling, src=v_ref)] + [copy(1 + j, me, (*chip, c), src=v_ref) for j, chip in enumerate(chips)]
        for cp in first:
            cp.start()
        passed = [copy(4 + j, (*chip, c), sibling) for j, chip in enumerate(chips)]
        for j, chip in enumerate(chips):
            copy(1 + j, (*chip, c), me).wait_recv()
            passed[j].start()
        copy(0, sibling, me).wait_recv()
        for j, chip in enumerate(chips):
            copy(4 + j, (*chip, 1 - c), me).wait_recv()
        for cp in first + passed:
            cp.wait_send()
        acc = gath[0:rows, :]
        for d in range(1, N_DEV):
            acc = acc + gath[d * rows:(d + 1) * rows, :]
        out_ref[...] = acc

    vm = pl.BlockSpec(memory_space=pltpu.VMEM)
    return pl.pallas_call(
        body, name="all_reduce_small",
        in_specs=[vm] + [ANY] * len(after), out_specs=vm,
        out_shape=jax.ShapeDtypeStruct(v.shape, F32),
        scratch_shapes=[pltpu.VMEM((N_DEV * rows, 128), F32), pltpu.SemaphoreType.DMA((7,)), pltpu.SemaphoreType.DMA((7,))],
    )(v, *after)


def _rope_tables(positions):
    inv_freq = ROPE_THETA ** (-jnp.arange(0, HD, 2, dtype=F32) / HD)
    ang = positions.reshape(T).astype(F32)[:, None] * inv_freq
    ang = jnp.concatenate([ang, ang, ang, ang], axis=-1)
    lo_half = (jnp.arange(128) % HD) < (HD // 2)
    return jnp.cos(ang), jnp.where(lo_half, -jnp.sin(ang), jnp.sin(ang))


def _selectors():
    lane = jnp.arange(QW)
    e = (lane[None, :] // HD == jnp.arange(SSM_H)[:, None]).astype(F32)
    e1 = ((lane[None, :] == HD * jnp.arange(128)[:, None]) & (jnp.arange(128)[:, None] < SSM_H)).astype(F32)
    src = jnp.arange(KVW)
    ex = ((lane[None, :] // (HD * (NQ // NKV)) == src[:, None] // HD) & (lane[None, :] % HD == src[:, None] % HD)).astype(F32)
    return e, e1, ex


WEIGHTS = ['ffn1_pre_norm', 'ffn1_w_gate', 'ffn1_w_up', 'ffn1_w_down', 'ffn1_post_norm', 'mix_pre_norm', 'w_in', 'conv_w', 'conv_b',
           'dt_bias', 'a_log', 'd_skip', 'ssm_norm', 'w_out', 'mix_post_norm', 'ffn2_pre_norm', 'ffn2_w_gate', 'ffn2_w_up',
           'ffn2_w_down', 'ffn2_post_norm']
COL_SHARDED = ['ffn1_w_gate', 'ffn1_w_up', 'ffn2_w_gate', 'ffn2_w_up', 'w_in']
ROW_SHARDED = ['ffn1_w_down', 'ffn2_w_down', 'w_out']
BIG = COL_SHARDED + ROW_SHARDED
FFN_BIG = COL_SHARDED[:4] + ROW_SHARDED[:2]
SMALL = ['ffn1_pre_norm', 'ffn1_post_norm', 'mix_pre_norm', 'conv_b', 'dt_bias', 'a_log', 'd_skip', 'ssm_norm', 'mix_post_norm',
         'ffn2_pre_norm', 'ffn2_post_norm']
FFN1 = ['ffn1_w_gate', 'ffn1_w_up', 'ffn1_w_down']
FFN2 = ['ffn2_w_gate', 'ffn2_w_up', 'ffn2_w_down']


def _wire_block(name, a):
    if name in FFN_BIG:
        return jnp.pad(a.astype(BF16), ((0, FSH - FSR), (0, 0)))
    if name == "w_in":
        return jnp.pad(a.astype(BF16), ((0, ISW - ISR), (0, 0)))
    return a if name == "conv_w" else a.astype(BF16)


def _whole_from_gathered(name, a):
    if name == "conv_w":
        return jnp.transpose(a, (1, 0, 2)).reshape(a.shape[1], -1)
    return a.reshape(-1, D)


def _step(x, positions, target, small, blocks=None, whole=None):
    dist = blocks is not None
    core = "mesh" if dist else 0
    w = dict(small)
    if whole:
        w.update(whole)

    def gather(names):
        return [_gather_duty([_wire_block(n, blocks[n]) for n in names])] if dist else []

    def put(names, results):
        if dist:
            for n, r in zip(names, results[0]):
                w[n] = _whole_from_gathered(n, r)

    g, sums, red = {}, {}, {}

    def swap(names):
        return [_swap_duty([g[n][1] for n in names])] if dist else []

    def chip_sums(names, from_sibling):
        if dist:
            res = _chip_sum([g[n][0] for n in names], list(from_sibling), "chip_sum_" + names[0])
            sums.update(zip(names, res))

    def exchange(names):
        return [_exchange_duty([sums[n][1] for n in names])] if dist else []

    def reduced(names, from_chips):
        if dist:
            for n, recv in zip(names, from_chips):
                red[n] = (sums[n][0], recv)

    cos, sin_s = _rope_tables(positions)
    e, e1, exf = _selectors()
    bias = _attn_bias()
    alx = jnp.repeat(w["a_log"], HD, axis=1)
    dskx = jnp.repeat(w["d_skip"], HD, axis=1)

    if dist:
        put(FFN1, _comm_only(gather(FFN1), "gather_ffn1"))
    (x1, n1, a1, b1, hm1, h1), got = _ffn_fwd(x, w["ffn1_pre_norm"], w["ffn1_w_gate"], w["ffn1_w_up"], w["ffn1_w_down"],
                                              w["ffn1_post_norm"], "ffn1_fwd", gather(["w_in", "conv_w"]))
    put(["w_in", "conv_w"], got)
    (n2, q, kx, vx, xbc, z, dtr), got = _inproj_fwd(x1, w["mix_pre_norm"], w["w_in"], cos, sin_s, exf.astype(BF16), gather(["w_out"]))
    put(["w_out"], got)
    (attn, lse), got = _attn_fwd(q, kx, vx, bias, gather(FFN2[:2]))
    put(FFN2[:2], got)
    (yn, y, hs, xconv), got = _ssd_fwd(xbc, z, dtr, w["conv_w"], w["conv_b"], w["dt_bias"], alx, dskx, w["ssm_norm"], e, gather(FFN2[2:]))
    put(FFN2[2:], got)
    x2, h2 = _outproj_fwd(x1, attn, yn, w["w_out"], w["mix_post_norm"])
    (dx3, n3, a3, b3, hm3, h3, ss), _ = _ffn_fwd(x2, w["ffn2_pre_norm"], w["ffn2_w_gate"], w["ffn2_w_up"], w["ffn2_w_down"],
                                                 w["ffn2_post_norm"], "ffn2_fwd", target=target)

    (dx2, da3, db3, dh3, g["ffn2_pre_norm"], g["ffn2_post_norm"]), _ = _ffn_bwd(
        dx3, x2, a3, b3, h3, w["ffn2_pre_norm"], w["ffn2_post_norm"], w["ffn2_w_gate"], w["ffn2_w_up"], w["ffn2_w_down"], "ffn2_bwd")
    g["ffn2_w_down"] = _matmul_tn(hm3, dh3, "ffn2_dwd", core)[0]
    g["ffn2_w_gate"] = _matmul_tn(da3, n3, "ffn2_dwg", core)[0]
    g["ffn2_w_up"] = _matmul_tn(db3, n3, "ffn2_dwu", core)[0]

    (dh2, dmix, g["mix_post_norm"]), got = _outproj_bwd(dx2, h2, w["mix_post_norm"], w["w_out"], swap(FFN2))
    chip_sums(FFN2, got[0] if dist else None)
    g["w_out"] = _dwout(attn, yn, dh2, core)
    (dq, dkx, dvx), got = _attn_bwd(q, kx, vx, attn, dmix, lse, bias, exchange(FFN2) + swap(["w_out"]))
    if dist:
        reduced(FFN2, got[0])
        chip_sums(["w_out"], got[1])
    (dxbc, dz, ddt, dcw, g["conv_b"], g["ssm_norm"], dpar), got = _ssd_bwd(
        dmix, xbc, xconv, z, dtr, y, hs, w["conv_w"], w["dt_bias"], alx, dskx, w["ssm_norm"], e, e1, exchange(["w_out"]))
    reduced(["w_out"], got[0] if dist else None)
    g["conv_w"] = dcw[0:4]
    g["dt_bias"], g["a_log"], g["d_skip"] = dpar[0:1], dpar[1:2], dpar[2:3]
    dx1, dproj, g["mix_pre_norm"] = _inproj_bwd(dx2, dq, dkx, dvx, dxbc, dz, ddt, x1, w["mix_pre_norm"], w["w_in"], cos, sin_s, exf)
    g["w_in"] = _matmul_tn(dproj, n2, "dwin", core)[0]

    started = {}

    def start(n):
        started[n] = _exchange_start(sums[n][1], "start_exchange_" + n)
        return [started[n]["token"]]

    after = []
    if dist:
        chip_sums(["w_in"], _comm_only(swap(["w_in"]), "swap_w_in")[0])
        after = start("w_in")
    (dx0, da1, db1, dh1, g["ffn1_pre_norm"], g["ffn1_post_norm"]), _ = _ffn_bwd(
        dx1, x, a1, b1, h1, w["ffn1_pre_norm"], w["ffn1_post_norm"], w["ffn1_w_gate"], w["ffn1_w_up"], w["ffn1_w_down"], "ffn1_bwd",
        after)
    total = None
    if dist:
        flat = jnp.concatenate([g[n].reshape(-1) for n in SMALL] + [g["conv_w"].reshape(-1), ss[0, 0:1]])
        rows = -(-flat.shape[0] // 128 // 8) * 8
        total = _all_reduce_small(jnp.pad(flat, (0, rows * 128 - flat.shape[0])).reshape(rows, 128), after)
        after = [total]
    g["ffn1_w_down"], _ = _matmul_tn(hm1, dh1, "ffn1_dwd", core, after=after)
    g["ffn1_w_gate"], got = _matmul_tn(da1, n1, "ffn1_dwg", core, duties=swap(["ffn1_w_down"]))
    if dist:
        chip_sums(["ffn1_w_down"], got[0])
        after = start("ffn1_w_down")
    g["ffn1_w_up"], got = _matmul_tn(db1, n1, "ffn1_dwu", core, after=after, duties=swap(["ffn1_w_gate"]))
    if dist:
        chip_sums(["ffn1_w_gate"], got[0])
        after = start("ffn1_w_gate")
        chip_sums(["ffn1_w_up"], _comm_only(swap(["ffn1_w_up"]), "swap_ffn1_w_up", after=after)[0])
        start("ffn1_w_up")
    return ss, dx0, g, red, {n: (sums[n][0], started[n]) for n in started}, total


def kernel(x, positions, ffn1_pre_norm, ffn1_w_gate, ffn1_w_up, ffn1_w_down, ffn1_post_norm, mix_pre_norm, w_in, conv_w, conv_b, dt_bias, a_log, d_skip, ssm_norm, w_out, mix_post_norm, ffn2_pre_norm, ffn2_w_gate, ffn2_w_up, ffn2_w_down, ffn2_post_norm, loss_target, m_ffn1_pre_norm, m_ffn1_w_gate, m_ffn1_w_up, m_ffn1_w_down, m_ffn1_post_norm, m_mix_pre_norm, m_w_in, m_conv_w, m_conv_b, m_dt_bias, m_a_log, m_d_skip, m_ssm_norm, m_w_out, m_mix_post_norm, m_ffn2_pre_norm, m_ffn2_w_gate, m_ffn2_w_up, m_ffn2_w_down, m_ffn2_post_norm, v_ffn1_pre_norm, v_ffn1_w_gate, v_ffn1_w_up, v_ffn1_w_down, v_ffn1_post_norm, v_mix_pre_norm, v_w_in, v_conv_w, v_conv_b, v_dt_bias, v_a_log, v_d_skip, v_ssm_norm, v_w_out, v_mix_post_norm, v_ffn2_pre_norm, v_ffn2_w_gate, v_ffn2_w_up, v_ffn2_w_down, v_ffn2_post_norm):
    given = dict(locals())
    drop = lambda n, a: a if n in SMALL else (a[0].T if n in COL_SHARDED else a[0])
    w = {n: drop(n, given[n]) for n in WEIGHTS}
    m = {n: drop(n, given["m_" + n]) for n in WEIGHTS}
    v = {n: drop(n, given["v_" + n]) for n in WEIGHTS}
    cx, cy, cc = _place()
    others = [2 * (1 - cx) + cy, 2 * cx + (1 - cy), 2 * (1 - cx) + (1 - cy)]

    _, grad_x, g, red, pending, total = _step(x[0], positions, loss_target[0], {n: w[n] for n in SMALL},
                                              blocks={n: w[n] for n in BIG + ["conv_w"]})
    total = total.reshape(-1)
    chip_ids = jnp.stack(others).astype(jnp.int32)
    out_g, out_d, out_m, out_v = {}, {}, {}, {}

    def update(names, sums, recvs, label, after=()):
        res = _adamw_sharded([w[n] for n in names], [m[n] for n in names], [v[n] for n in names], sums, recvs, chip_ids,
                             "adamw_" + label, after)
        for n, (gn, dn, mn, vn) in zip(names, res):
            out_g[n], out_d[n], out_m[n], out_v[n] = gn, dn, mn, vn

    last_start = [pending["ffn1_w_up"][1]["token"]]
    update(FFN2, [red[n][0] for n in FFN2], [red[n][1] for n in FFN2], "ffn2", last_start)
    update(["w_out"], [red["w_out"][0]], [red["w_out"][1]], "w_out", last_start)

    rows = total.shape[0] // 128
    sizes = [w[n].size for n in SMALL]
    offs = [0]
    for s_ in sizes:
        offs.append(offs[-1] + s_)
    gs = {n: total[offs[i]:offs[i + 1]].reshape(w[n].shape) for i, n in enumerate(SMALL)}
    gcw = total[offs[-1]:offs[-1] + 4 * CONV_C].reshape(4, CONV_C)
    loss = 0.5 * total[offs[-1] + 4 * CONV_C] / D
    gs["conv_w"] = lax.dynamic_slice_in_dim(gcw, (4 * cx + 2 * cy + cc) * (CONV_C // N_DEV), CONV_C // N_DEV, axis=1)
    names = SMALL + ["conv_w"]

    def pack(d):
        flat = jnp.concatenate([d[n].reshape(-1) for n in names])
        return jnp.pad(flat, (0, rows * 128 - flat.shape[0])).reshape(rows, 128)

    pg, pd, pm, pv = _adamw(pack(w), pack(m), pack(v), [pack(gs)], "adamw_small", last_start)
    done = [pg] + [out_v[n] for n in BIG if n not in pending]
    update(["w_in"], [pending["w_in"][0]], [_exchange_wait(pending["w_in"][1], done, "wait_exchange_w_in")], "w_in")
    done = [out_v["w_in"]]
    update(FFN1, [pending[n][0] for n in FFN1], [_exchange_wait(pending[n][1], done, "wait_exchange_" + n) for n in FFN1], "ffn1")
    o2 = [0]
    for n in names:
        o2.append(o2[-1] + w[n].size)
    for i, n in enumerate(names):
        for dst, src in ((out_g, pg), (out_d, pd), (out_m, pm), (out_v, pv)):
            dst[n] = src.reshape(-1)[o2[i]:o2[i + 1]].reshape(w[n].shape)

    outs = [loss, grad_x[None]]
    for d in (out_g, out_d, out_m, out_v):
        outs += [d[n] if n in SMALL else (d[n].T[None] if n in COL_SHARDED else d[n][None]) for n in WEIGHTS]
    return tuple(outs)
```

```python
import functools
import math

import jax
import jax.numpy as jnp
from jax import lax
from jax.experimental import pallas as pl
from jax.experimental.pallas import tpu as pltpu

F32 = jnp.float32
BF16 = jnp.bfloat16
MESH = pl.DeviceIdType.MESH

N_DEV = 8
T = 2048
D = 1024
FF = 2816
FSR = FF // N_DEV
FSH = 384
FFP = N_DEV * FSH
HD = 64
NQ = 16
NKV = 4
QW = NQ * HD
KVW = NKV * HD
SSM_W = 1024
SSM_H = 16
SSM_N = 128
CONV_C = SSM_W + 2 * 2 * SSM_N
IN_COLS = 4112
INP = 4224
ISR = IN_COLS // N_DEV
ISW = 528
ISG = 640
L = 128
NCH = T // L
AB = 256
NAB = T // AB
EPS = 1e-6
NEG = -1e30
ROPE_THETA = 10000.0
DILATIONS = ((128, 1), (512, 4), (2048, 16))

ADAM_LR = 0.001
ADAM_B1 = 0.9
ADAM_B2 = 0.999
ADAM_EPS = 1e-08
ADAM_WD = 0.01
ADAM_STEP = 10

VMEM_LIMIT = 58 * 1024 * 1024


def _params(sem, vmem=VMEM_LIMIT):
    return pltpu.CompilerParams(dimension_semantics=sem, vmem_limit_bytes=vmem)


def _dot(a, b):
    return jnp.dot(a, b, preferred_element_type=F32)


def _dot_nt(a, b):
    return lax.dot_general(a, b, (((1,), (1,)), ((), ())), preferred_element_type=F32)


def _dot_tn(a, b):
    return lax.dot_general(a, b, (((0,), (0,)), ((), ())), preferred_element_type=F32)


def _split3(x):
    hi = x.astype(BF16)
    r1 = x - hi.astype(F32)
    mid = r1.astype(BF16)
    lo = (r1 - mid.astype(F32)).astype(BF16)
    return hi, mid, lo


def _dot_hi(a, b, a_is_01=False):
    if a_is_01:
        sel = a.astype(BF16)
        return sum(_dot(sel, p) for p in _split3(b))
    sel = b.astype(BF16)
    return sum(_dot(p, sel) for p in _split3(a))


def _dot_nt_hi(a, b):
    sel = b.astype(BF16)
    return sum(_dot_nt(p, sel) for p in _split3(a))


def _rs(x):
    return lax.rsqrt(jnp.mean(x * x, axis=-1, keepdims=True) + EPS)


def _sigmoid(x):
    return jax.nn.sigmoid(x)


def _dsilu(x, s):
    return s * (1.0 + x * (1.0 - s))


def _resident(shape):
    nd = len(shape)
    return pl.BlockSpec(shape, lambda *_: (0,) * nd, pipeline_mode=pl.Buffered(1))


def _const(shape):
    nd = len(shape)
    return pl.BlockSpec(shape, lambda *_: (0,) * nd)


def _rows(tm, cols):
    return pl.BlockSpec((tm, cols), lambda i: (i, 0))


ANY = pl.BlockSpec(memory_space=pl.ANY)


def _place():
    return lax.axis_index("x"), lax.axis_index("y"), lax.axis_index("c")


def _live(ref, rows):
    return ref if rows is None else ref.at[pl.ds(0, rows)]


def _gather_duty(arrays, live_rows):
    n = len(arrays)
    results = [jax.ShapeDtypeStruct((N_DEV,) + a.shape, a.dtype) for a in arrays]

    def make(ins, outs, send_sems, recv_sems, local_sems):
        x, y, c = _place()
        me, sibling = (x, y, c), (x, y, 1 - c)
        chips = [(1 - x, y), (x, 1 - y), (1 - x, 1 - y)]
        my_index = 4 * x + 2 * y + c

        def place_of(a, px, py, pc):
            return _live(outs[a].at[4 * px + 2 * py + pc], live_rows[a])

        def copy(a, k, block, to, own_block=False):
            dst = place_of(a, *block)
            return pltpu.make_async_remote_copy(src_ref=_live(ins[a], live_rows[a]) if own_block else dst, dst_ref=dst,
                                                send_sem=send_sems.at[7 * a + k], recv_sem=recv_sems.at[7 * a + k],
                                                device_id=to, device_id_type=MESH)

        def local(a):
            cps = [pltpu.make_async_copy(ins[a], outs[a].at[my_index], local_sems.at[N_DEV * a])]
            if live_rows[a] is not None:
                pad = pl.ds(live_rows[a], arrays[a].shape[0] - live_rows[a])
                cps += [pltpu.make_async_copy(ins[a].at[pad], outs[a].at[jnp.bitwise_xor(my_index, k), pad], local_sems.at[N_DEV * a + k])
                        for k in range(1, N_DEV)]
            return cps

        def first(a):
            return [copy(a, 0, me, sibling, True)] + [copy(a, 1 + j, me, (*chip, c), True) for j, chip in enumerate(chips)]

        def start():
            for a in range(n):
                for cp in local(a):
                    cp.start()
            for a in range(n):
                for cp in first(a):
                    cp.start()

        def finish():
            for j, chip in enumerate(chips):
                for a in range(n):
                    copy(a, 1 + j, (*chip, c), me).wait_recv()
                    copy(a, 4 + j, (*chip, c), sibling).start()
            for a in range(n):
                copy(a, 0, sibling, me).wait_recv()
                for j, chip in enumerate(chips):
                    copy(a, 4 + j, (*chip, 1 - c), me).wait_recv()
            for a in range(n):
                for cp in first(a) + [copy(a, 4 + j, (*chip, c), sibling) for j, chip in enumerate(chips)]:
                    cp.wait_send()
                for cp in local(a):
                    cp.wait()

        return start, finish

    return dict(operands=list(arrays), results=results, sems=(7 * n, 7 * n, N_DEV * n), make=make)


def _swap_duty(arrays, live_rows):
    n = len(arrays)
    half = N_DEV // 2
    results = [jax.ShapeDtypeStruct(a.shape, a.dtype) for a in arrays]

    def make(ins, outs, send_sems, recv_sems):
        x, y, c = _place()

        def copies():
            return [pltpu.make_async_remote_copy(src_ref=_live(ins[a].at[k], live_rows[a]), dst_ref=_live(outs[a].at[k], live_rows[a]),
                                                 send_sem=send_sems.at[half * a + k], recv_sem=recv_sems.at[half * a + k],
                                                 device_id=(x, y, 1 - c), device_id_type=MESH)
                    for a in range(n) for k in range(half)]

        def start():
            for cp in copies():
                cp.start()

        def finish():
            for cp in copies():
                cp.wait()

        return start, finish

    return dict(operands=list(arrays), results=results, sems=(half * n, half * n), make=make)


def _exchange_duty(arrays, live_rows):
    n = len(arrays)
    results = [jax.ShapeDtypeStruct(a.shape, a.dtype) for a in arrays]

    def make(ins, outs, send_sems, recv_sems):
        x, y, c = _place()
        chips = [(1 - x, y), (x, 1 - y), (1 - x, 1 - y)]
        my_chip = 2 * x + y

        def sends():
            return [pltpu.make_async_remote_copy(src_ref=_live(ins[a].at[2 * px + py], live_rows[a]),
                                                 dst_ref=_live(outs[a].at[my_chip], live_rows[a]),
                                                 send_sem=send_sems.at[3 * a + j], recv_sem=recv_sems.at[3 * a + j],
                                                 device_id=(px, py, c), device_id_type=MESH)
                    for a in range(n) for j, (px, py) in enumerate(chips)]

        def start():
            for cp in sends():
                cp.start()

        def finish():
            for a in range(n):
                for j, (px, py) in enumerate(chips):
                    pltpu.make_async_remote_copy(src_ref=_live(ins[a].at[my_chip], live_rows[a]),
                                                 dst_ref=_live(outs[a].at[2 * px + py], live_rows[a]),
                                                 send_sem=send_sems.at[3 * a + j], recv_sem=recv_sems.at[3 * a + j],
                                                 device_id=(px, py, c), device_id_type=MESH).wait_recv()
            for cp in sends():
                cp.wait_send()

        return start, finish

    return dict(operands=list(arrays), results=results, sems=(3 * n, 3 * n), make=make)


def _call(body, *, name, grid, in_specs, out_specs, out_shape, args, sem, scratch=(), duties=(), after=()):
    n_in, n_out, n_scr = len(in_specs), len(out_specs), len(scratch)
    sem_shapes = [pltpu.SemaphoreType.DMA((k,)) for d in duties for k in d["sems"]]

    def full(*refs):
        pos = [0]

        def take(k):
            pos[0] += k
            return refs[pos[0] - k:pos[0]]

        ins = take(n_in)
        d_ins = [take(len(d["operands"])) for d in duties]
        take(len(after))
        outs = take(n_out)
        d_outs = [take(len(d["results"])) for d in duties]
        scr = take(n_scr)
        d_sems = [take(len(d["sems"])) for d in duties]
        hooks = [d["make"](di, do, *ds) for d, di, do, ds in zip(duties, d_ins, d_outs, d_sems)]
        if grid and hooks:
            ids = [pl.program_id(k) for k in range(len(grid))]
            first = functools.reduce(jnp.logical_and, [i == 0 for i in ids])
            last = functools.reduce(jnp.logical_and, [i == g - 1 for i, g in zip(ids, grid)])

            @pl.when(first)
            def _():
                for start, _ in hooks:
                    start()

            body(*ins, *outs, *scr)

            @pl.when(last)
            def _():
                for _, finish in hooks:
                    finish()
        else:
            for start, _ in hooks:
                start()
            body(*ins, *outs, *scr)
            for _, finish in hooks:
                finish()

    d_args = [a for d in duties for a in d["operands"]]
    d_res = [r for d in duties for r in d["results"]]
    kwargs = dict(grid=grid) if grid else {}
    res = pl.pallas_call(
        full, name=name, in_specs=list(in_specs) + [ANY] * (len(d_args) + len(after)), out_specs=list(out_specs) + [ANY] * len(d_res),
        out_shape=list(out_shape) + d_res, scratch_shapes=list(scratch) + sem_shapes,
        compiler_params=_params(sem) if grid else None, **kwargs,
    )(*args, *d_args, *after)
    own, rest = list(res[:n_out]), list(res[n_out:])
    by_duty = []
    for d in duties:
        by_duty.append(rest[:len(d["results"])])
        rest = rest[len(d["results"]):]
    return own, by_duty


def _comm_only(duties, name, after=()):
    return _call(lambda: None, name=name, grid=None, in_specs=[], out_specs=[], out_shape=[], args=[], sem=None, duties=duties,
                 after=after)[1]


HBM = pl.BlockSpec(memory_space=pltpu.HBM)
SEMS = pl.BlockSpec(memory_space=pltpu.SEMAPHORE)
SIDE_EFFECT = pltpu.SideEffectType.DATAFLOW_SIDE_EFFECTING
N_OTHER_CHIPS = 3


def _chip_copies(src_ref, land_ref, sems, rows):
    x, y, c = _place()
    chips = [(1 - x, y), (x, 1 - y), (1 - x, 1 - y)]
    return [pltpu.make_async_remote_copy(src_ref=_live(src_ref.at[2 * px + py], rows), dst_ref=_live(land_ref.at[2 * x + y], rows),
                                         send_sem=sems[j], recv_sem=sems[N_OTHER_CHIPS + j], device_id=(px, py, c), device_id_type=MESH)
            for j, (px, py) in enumerate(chips)]


def _exchange_start(pb, rows, name):
    n_sem = 2 * N_OTHER_CHIPS

    def body(pb_ref, land_ref, *rest):
        for cp in _chip_copies(pb_ref, land_ref, rest[:n_sem], rows):
            cp.start()
        token = rest[n_sem + 2]
        token[...] = jnp.zeros_like(token)

    res = pl.pallas_call(
        body, name=name,
        out_shape=(pltpu.SemaphoreType.DMA(()),) * n_sem + (pltpu.HBM(pb.shape, pb.dtype), pltpu.HBM(pb.shape, pb.dtype),
                                                              jax.ShapeDtypeStruct((8, 128), F32)),
        in_specs=(HBM, HBM), out_specs=(SEMS,) * n_sem + (HBM, HBM, pl.BlockSpec(memory_space=pltpu.VMEM)),
        input_output_aliases={0: n_sem, 1: n_sem + 1},
        compiler_params=pltpu.CompilerParams(has_side_effects=SIDE_EFFECT),
    )(pltpu.with_memory_space_constraint(pb, pltpu.HBM), pltpu.with_memory_space_constraint(lax.empty(pb.shape, pb.dtype), pltpu.HBM))
    return dict(sems=res[:n_sem], src=res[n_sem], land=res[n_sem + 1], token=res[n_sem + 2], rows=rows)


def _exchange_wait(started, after, name):
    n_sem = 2 * N_OTHER_CHIPS

    def body(pb_ref, land_ref, *rest):
        for cp in _chip_copies(pb_ref, land_ref, rest[:n_sem], started["rows"]):
            cp.wait_send()
            cp.wait_recv()

    src, land = started["src"], started["land"]
    return pl.pallas_call(
        body, name=name, out_shape=(pltpu.HBM(src.shape, src.dtype), pltpu.HBM(land.shape, land.dtype)),
        in_specs=(HBM, HBM) + (SEMS,) * n_sem + (ANY,) * len(after), out_specs=(HBM, HBM), input_output_aliases={0: 0, 1: 1},
        compiler_params=pltpu.CompilerParams(has_side_effects=SIDE_EFFECT),
    )(src, land, *started["sems"], *after)[1]


def _ffn_fwd(x, gpre, wg, wu, wd, gpost, name, duties=(), target=None):
    tm = 256
    n_in = 6 if target is None else 7

    def body(*refs):
        x_ref, gpre_ref, wg_ref, wu_ref, wd_ref, gpost_ref = refs[:6]
        xo_ref, n_ref, a_ref, b_ref, hm_ref, h_ref = refs[n_in:n_in + 6]
        xv = x_ref[...]
        n = (xv * _rs(xv) * gpre_ref[...]).astype(BF16)
        a = _dot_nt(n, wg_ref[...])
        b = _dot_nt(n, wu_ref[...])
        hm = (a * _sigmoid(a) * b).astype(BF16)
        h = _dot(hm, wd_ref[...])
        xo = xv + 0.5 * (h * _rs(h) * gpost_ref[...])
        n_ref[...] = n
        a_ref[...] = a.astype(BF16)
        b_ref[...] = b.astype(BF16)
        hm_ref[...] = hm
        h_ref[...] = h
        if target is None:
            xo_ref[...] = xo
        else:
            ss_ref = refs[n_in + 6]

            @pl.when(pl.program_id(0) == 0)
            def _():
                ss_ref[...] = jnp.zeros_like(ss_ref)

            err = xo - refs[6][...]
            xo_ref[...] = err * (1.0 / D)
            ss_ref[...] += jnp.sum(jnp.sum(err * err, axis=1, keepdims=True), axis=0, keepdims=True)

    loss_in = [] if target is None else [_rows(tm, D)]
    loss_out = [] if target is None else [_const((1, 128))]
    loss_shape = [] if target is None else [jax.ShapeDtypeStruct((1, 128), F32)]
    return _call(
        body, name=name, grid=(T // tm,),
        in_specs=[_rows(tm, D), _const((1, D)), _resident((FFP, D)), _resident((FFP, D)), _resident((FFP, D)), _const((1, D))] + loss_in,
        out_specs=[_rows(tm, D), _rows(tm, D), _rows(tm, FFP), _rows(tm, FFP), _rows(tm, FFP), _rows(tm, D)] + loss_out,
        out_shape=[jax.ShapeDtypeStruct((T, D), F32), jax.ShapeDtypeStruct((T, D), BF16), jax.ShapeDtypeStruct((T, FFP), BF16),
                   jax.ShapeDtypeStruct((T, FFP), BF16), jax.ShapeDtypeStruct((T, FFP), BF16), jax.ShapeDtypeStruct((T, D), F32)]
        + loss_shape,
        args=[x, gpre, wg, wu, wd, gpost] + ([] if target is None else [target]), sem=("arbitrary",), duties=duties)


def _ffn_bwd(dxo, x, a, b, h, gpre, gpost, wg, wu, wd, name, after=()):
    tm = 256

    def body(dxo_ref, x_ref, a_ref, b_ref, h_ref, gpre_ref, gpost_ref, wg_ref, wu_ref, wd_ref,
             dx_ref, da_ref, db_ref, dh_ref, dgpre_ref, dgpost_ref):
        @pl.when(pl.program_id(0) == 0)
        def _():
            dgpre_ref[...] = jnp.zeros_like(dgpre_ref)
            dgpost_ref[...] = jnp.zeros_like(dgpost_ref)

        dy = dxo_ref[...]
        h = h_ref[...]
        hn = h * _rs(h)
        r2 = _rs(h)
        dgpost_ref[...] += jnp.sum(0.5 * dy * hn, axis=0, keepdims=True)
        gdy = 0.5 * dy * gpost_ref[...]
        dh = r2 * (gdy - hn * jnp.mean(gdy * hn, axis=-1, keepdims=True))
        dhb = dh.astype(BF16)
        dh_ref[...] = dhb
        dhm = _dot_nt(dhb, wd_ref[...])
        av = a_ref[...].astype(F32)
        bv = b_ref[...].astype(F32)
        sg = _sigmoid(av)
        db = (dhm * (av * sg)).astype(BF16)
        da = (dhm * bv * _dsilu(av, sg)).astype(BF16)
        da_ref[...] = da
        db_ref[...] = db
        dn = _dot(da, wg_ref[...]) + _dot(db, wu_ref[...])
        xv = x_ref[...]
        r = _rs(xv)
        xn = xv * r
        dgpre_ref[...] += jnp.sum(dn * xn, axis=0, keepdims=True)
        gdn = dn * gpre_ref[...]
        dx_ref[...] = dy + r * (gdn - xn * jnp.mean(gdn * xn, axis=-1, keepdims=True))

    return _call(
        body, name=name, grid=(T // tm,),
        in_specs=[_rows(tm, D), _rows(tm, D), _rows(tm, FFP), _rows(tm, FFP), _rows(tm, D), _const((1, D)), _const((1, D)),
                  _resident((FFP, D)), _resident((FFP, D)), _resident((FFP, D))],
        out_specs=[_rows(tm, D), _rows(tm, FFP), _rows(tm, FFP), _rows(tm, D), _const((1, D)), _const((1, D))],
        out_shape=[jax.ShapeDtypeStruct((T, D), F32), jax.ShapeDtypeStruct((T, FFP), BF16), jax.ShapeDtypeStruct((T, FFP), BF16),
                   jax.ShapeDtypeStruct((T, D), BF16), jax.ShapeDtypeStruct((1, D), F32), jax.ShapeDtypeStruct((1, D), F32)],
        args=[dxo, x, a, b, h, gpre, gpost, wg, wu, wd], sem=("arbitrary",), after=after)


def _core_index(core):
    return lax.axis_index("c") if core == "mesh" else core


def _by_core(res, o_ref, ob_ref, core):
    r = res.shape[0] // 2
    c = jnp.asarray(_core_index(core))

    @pl.when(c == 0)
    def _():
        o_ref[0] = res[:r]
        ob_ref[0] = res[r:].astype(BF16)

    @pl.when(c == 1)
    def _():
        o_ref[0] = res[r:]
        ob_ref[0] = res[:r].astype(BF16)


def _matmul_tn(a, b, name, core, after=(), duties=()):
    k, m = a.shape
    n = b.shape[1]
    r = m // N_DEV
    assert m == N_DEV * r and r % 128 == 0

    def body(a_ref, b_ref, o_ref, ob_ref):
        _by_core(_dot_tn(a_ref[...], b_ref[...]), o_ref, ob_ref, core)

    spec = pl.BlockSpec((1, r, n), lambda i: (i, 0, 0))
    return _call(body, name=name, grid=(N_DEV // 2,), in_specs=[pl.BlockSpec((k, 2 * r), lambda i: (0, i)), _resident((k, n))],
                 out_specs=[spec, spec],
                 out_shape=[jax.ShapeDtypeStruct((N_DEV // 2, r, n), F32), jax.ShapeDtypeStruct((N_DEV // 2, r, n), BF16)],
                 args=[a, b], sem=("arbitrary",), duties=duties, after=after)


def _dwout(attn, yn, dh2, core):
    rs = (QW + SSM_W) // N_DEV
    chips = N_DEV // 2

    def body(at_ref, yn_ref, dh_ref, o_ref, ob_ref):
        i = pl.program_id(0)

        @pl.when(i < chips // 2)
        def _():
            _by_core(_dot_tn(at_ref[...], dh_ref[...]), o_ref, ob_ref, core)

        @pl.when(i >= chips // 2)
        def _():
            _by_core(_dot_tn(yn_ref[...], dh_ref[...]), o_ref, ob_ref, core)

    spec = pl.BlockSpec((1, rs, D), lambda i: (i, 0, 0))
    return pl.pallas_call(
        body, name="dwout", grid=(chips,),
        in_specs=[pl.BlockSpec((T, 2 * rs), lambda i: (0, jnp.minimum(i, chips // 2 - 1))),
                  pl.BlockSpec((T, 2 * rs), lambda i: (0, jnp.maximum(i - chips // 2, 0))), _resident((T, D))],
        out_specs=[spec, spec],
        out_shape=[jax.ShapeDtypeStruct((chips, rs, D), F32), jax.ShapeDtypeStruct((chips, rs, D), BF16)],
        compiler_params=_params(("arbitrary",)),
    )(attn, yn, dh2)


def _rope_swap(t, lo_half):
    return jnp.where(lo_half, pltpu.roll(t, 96, 1), pltpu.roll(t, 32, 1))


def _inproj_fwd(x1, gpre, win, cos, sin_s, ex, duties=()):
    tm = 256

    def body(x_ref, g_ref, w_ref, cos_ref, sin_ref, ex_ref, n_ref, q_ref, kx_ref, vx_ref, xbc_ref, z_ref, dt_ref):
        xv = x_ref[...]
        n = (xv * _rs(xv) * g_ref[...]).astype(BF16)
        n_ref[...] = n
        by_dev = _dot_nt(n, w_ref[...])
        proj = jnp.concatenate([by_dev[:, ISW * d:ISW * d + ISR] for d in range(N_DEV)], axis=1)
        cs = cos_ref[...]
        sn = sin_ref[...]
        lo_half = (lax.broadcasted_iota(jnp.int32, (1, 128), 1) % HD) < (HD // 2)

        def rope(t):
            return t * cs + _rope_swap(t, lo_half) * sn

        for j in range(QW // 128):
            t = proj[:, 128 * j:128 * j + 128]
            q_ref[:, 128 * j:128 * j + 128] = (rope(t) * (HD ** -0.5)).astype(BF16)
        k = jnp.concatenate([rope(proj[:, QW + 128 * j:QW + 128 * j + 128]) for j in range(KVW // 128)], axis=1)
        v = proj[:, QW + KVW:QW + 2 * KVW]
        kx_ref[...] = _dot(k.astype(BF16), ex_ref[...]).astype(BF16)
        vx_ref[...] = _dot(v.astype(BF16), ex_ref[...]).astype(BF16)
        c0 = QW + 2 * KVW
        xbc_ref[...] = proj[:, c0:c0 + CONV_C]
        z_ref[...] = proj[:, c0 + CONV_C:c0 + CONV_C + SSM_W]
        dt_ref[...] = proj[:, c0 + CONV_C + SSM_W:IN_COLS]

    return _call(
        body, name="inproj_fwd", grid=(T // tm,),
        in_specs=[_rows(tm, D), _const((1, D)), _resident((INP, D)), _rows(tm, 128), _rows(tm, 128), _const((KVW, QW))],
        out_specs=[_rows(tm, D), _rows(tm, QW), _rows(tm, QW), _rows(tm, QW), _rows(tm, CONV_C), _rows(tm, SSM_W), _rows(tm, SSM_H)],
        out_shape=[jax.ShapeDtypeStruct((T, D), BF16), jax.ShapeDtypeStruct((T, QW), BF16), jax.ShapeDtypeStruct((T, QW), BF16),
                   jax.ShapeDtypeStruct((T, QW), BF16), jax.ShapeDtypeStruct((T, CONV_C), F32), jax.ShapeDtypeStruct((T, SSM_W), F32),
                   jax.ShapeDtypeStruct((T, SSM_H), F32)],
        args=[x1, gpre, win, cos, sin_s, ex], sem=("arbitrary",), duties=duties)


def _inproj_bwd(dres, dq, dkx, dvx, dxbc, dz, ddt, x1, gpre, win, cos, sin_s, exf):
    tm = 256

    def body(dres_ref, dq_ref, dkx_ref, dvx_ref, dxbc_ref, dz_ref, ddt_ref, x_ref, g_ref, w_ref, cos_ref, sin_ref, ex_ref,
             dx_ref, dps_ref, dg_ref, dp_ref):
        @pl.when(pl.program_id(0) == 0)
        def _():
            dg_ref[...] = jnp.zeros_like(dg_ref)

        cs = cos_ref[...]
        sn = sin_ref[...]
        lo_half = (lax.broadcasted_iota(jnp.int32, (1, 128), 1) % HD) < (HD // 2)

        def rope_t(t):
            return t * cs - _rope_swap(t, lo_half) * sn

        for j in range(QW // 128):
            dp_ref[:, 128 * j:128 * j + 128] = rope_t(dq_ref[:, 128 * j:128 * j + 128] * (HD ** -0.5)).astype(BF16)
        dk = _dot_nt_hi(dkx_ref[...], ex_ref[...])
        dv = _dot_nt_hi(dvx_ref[...], ex_ref[...])
        for j in range(KVW // 128):
            dp_ref[:, QW + 128 * j:QW + 128 * j + 128] = rope_t(dk[:, 128 * j:128 * j + 128]).astype(BF16)
        dp_ref[:, QW + KVW:QW + 2 * KVW] = dv.astype(BF16)
        c0 = QW + 2 * KVW
        dp_ref[:, c0:c0 + CONV_C] = dxbc_ref[...].astype(BF16)
        dp_ref[:, c0 + CONV_C:c0 + CONV_C + SSM_W] = dz_ref[...].astype(BF16)
        dp_ref[:, c0 + CONV_C + SSM_W:INP] = ddt_ref[...].astype(BF16)
        pieces = [dp_ref[:, ISR * d:ISR * (d + 1)] for d in range(N_DEV)]
        zw = jnp.zeros((tm, ISW - ISR), BF16)
        zg = jnp.zeros((tm, ISG - ISR), BF16)
        dn = _dot(jnp.concatenate([t for p in pieces for t in (p, zw)], axis=1), w_ref[...])
        for d in range(N_DEV):
            dps_ref[:, ISG * d:ISG * (d + 1)] = jnp.concatenate([pieces[d], zg], axis=1)
        xv = x_ref[...]
        r = _rs(xv)
        xn = xv * r
        dg_ref[...] += jnp.sum(dn * xn, axis=0, keepdims=True)
        gdn = dn * g_ref[...]
        dx_ref[...] = dres_ref[...] + r * (gdn - xn * jnp.mean(gdn * xn, axis=-1, keepdims=True))

    return pl.pallas_call(
        body, name="inproj_bwd", grid=(T // tm,),
        in_specs=[_rows(tm, D), _rows(tm, QW), _rows(tm, QW), _rows(tm, QW), _rows(tm, CONV_C), _rows(tm, SSM_W), _rows(tm, 128),
                  _rows(tm, D), _const((1, D)), _resident((INP, D)), _rows(tm, 128), _rows(tm, 128), _const((KVW, QW))],
        out_specs=[_rows(tm, D), _rows(tm, N_DEV * ISG), _const((1, D))],
        out_shape=[jax.ShapeDtypeStruct((T, D), F32), jax.ShapeDtypeStruct((T, N_DEV * ISG), BF16), jax.ShapeDtypeStruct((1, D), F32)],
        scratch_shapes=[pltpu.VMEM((tm, INP), BF16)],
        compiler_params=_params(("arbitrary",)),
    )(dres, dq, dkx, dvx, dxbc, dz, ddt, x1, gpre, win, cos, sin_s, exf)


def _outproj_fwd(x1, attn, yn, wout, gpost):
    tm = 512

    def body(x_ref, at_ref, yn_ref, w_ref, g_ref, xo_ref, h_ref):
        h = _dot(at_ref[...], w_ref[0:QW, :]) + _dot(yn_ref[...], w_ref[QW:QW + SSM_W, :])
        h_ref[...] = h
        xo_ref[...] = x_ref[...] + h * _rs(h) * g_ref[...]

    return pl.pallas_call(
        body, name="outproj_fwd", grid=(T // tm,),
        in_specs=[_rows(tm, D), _rows(tm, QW), _rows(tm, SSM_W), _resident((QW + SSM_W, D)), _const((1, D))],
        out_specs=[_rows(tm, D), _rows(tm, D)],
        out_shape=[jax.ShapeDtypeStruct((T, D), F32), jax.ShapeDtypeStruct((T, D), F32)],
        compiler_params=_params(("parallel",)),
    )(x1, attn, yn, wout, gpost)


def _outproj_bwd(dx2, h2, gpost, wout, duties=()):
    tm = 512

    def body(dy_ref, h_ref, g_ref, w_ref, dh_ref, dm_ref, dg_ref):
        @pl.when(pl.program_id(0) == 0)
        def _():
            dg_ref[...] = jnp.zeros_like(dg_ref)

        dy = dy_ref[...]
        h = h_ref[...]
        r = _rs(h)
        hn = h * r
        dg_ref[...] += jnp.sum(dy * hn, axis=0, keepdims=True)
        gdy = dy * g_ref[...]
        dh = (r * (gdy - hn * jnp.mean(gdy * hn, axis=-1, keepdims=True))).astype(BF16)
        dh_ref[...] = dh
        dm_ref[...] = _dot_nt(dh, w_ref[...])

    return _call(
        body, name="outproj_bwd", grid=(T // tm,),
        in_specs=[_rows(tm, D), _rows(tm, D), _const((1, D)), _resident((QW + SSM_W, D))],
        out_specs=[_rows(tm, D), _rows(tm, QW + SSM_W), _const((1, D))],
        out_shape=[jax.ShapeDtypeStruct((T, D), BF16), jax.ShapeDtypeStruct((T, QW + SSM_W), F32), jax.ShapeDtypeStruct((1, D), F32)],
        args=[dx2, h2, gpost, wout], sem=("arbitrary",), duties=duties)


def _attn_bias():
    d = jnp.arange(AB)[:, None] - jnp.arange(T)[None, :] + (T - AB)
    cnt = jnp.zeros(d.shape, F32)
    for window, dil in DILATIONS:
        cnt = cnt + ((d >= 0) & (d % dil == 0) & (d <= window)).astype(F32)
    return jnp.where(cnt > 0, jnp.log(jnp.maximum(cnt, 1.0)), NEG)


G_PER = NQ // NKV
WK = G_PER * HD


def _attn_fwd(q, kx, vx, bias, duties=()):
    def body(q_ref, kx_ref, vx_ref, bias_ref, o_ref, lse_ref):
        lane = lax.broadcasted_iota(jnp.int32, (1, WK), 1)
        lse_ref[...] = jnp.zeros_like(lse_ref)
        for i in range(NAB):
            n = (i + 1) * AB
            rows = slice(i * AB, n)
            qi = q_ref[rows, :]
            kxi = kx_ref[0:n, :]
            vxi = vx_ref[0:n, :]
            bb = bias_ref[:, (NAB - 1 - i) * AB:]
            o_acc = jnp.zeros((AB, WK), F32)
            for g in range(G_PER):
                mg = (lane // HD) == g
                s = _dot_nt(jnp.where(mg, qi, jnp.zeros_like(qi)), kxi) + bb
                m = jnp.max(s, axis=1, keepdims=True)
                p = jnp.exp(s - m)
                l = jnp.sum(p, axis=1, keepdims=True)
                o_acc = jnp.where(mg, _dot(p.astype(BF16), vxi) / l, o_acc)
                lse_ref[rows, g:g + 1] = m + jnp.log(l)
            o_ref[rows, :] = o_acc.astype(BF16)

    col = lambda kv: (0, kv)
    return _call(
        body, name="attn_fwd", grid=(NKV,),
        in_specs=[pl.BlockSpec((T, WK), col), pl.BlockSpec((T, WK), col), pl.BlockSpec((T, WK), col), _const((AB, T))],
        out_specs=[pl.BlockSpec((T, WK), col), pl.BlockSpec((T, 128), col)],
        out_shape=[jax.ShapeDtypeStruct((T, QW), BF16), jax.ShapeDtypeStruct((T, NKV * 128), F32)],
        args=[q, kx, vx, bias], sem=("arbitrary",), duties=duties)


def _attn_bwd(q, kx, vx, o, dmix, lse, bias, duties=()):
    def body(q_ref, kx_ref, vx_ref, o_ref, do_ref, lse_ref, bias_ref, dq_ref, dkx_ref, dvx_ref):
        lane = lax.broadcasted_iota(jnp.int32, (1, WK), 1)
        dkx_ref[...] = jnp.zeros_like(dkx_ref)
        dvx_ref[...] = jnp.zeros_like(dvx_ref)
        for i in range(NAB):
            n = (i + 1) * AB
            rows = slice(i * AB, n)
            qi = q_ref[rows, :]
            dof = do_ref[rows, :]
            doi = dof.astype(BF16)
            prod = dof * o_ref[rows, :].astype(F32)
            kxi = kx_ref[0:n, :]
            vxi = vx_ref[0:n, :]
            bb = bias_ref[:, (NAB - 1 - i) * AB:]
            dq_acc = jnp.zeros((AB, WK), F32)
            for g in range(G_PER):
                mg = (lane // HD) == g
                qm = jnp.where(mg, qi, jnp.zeros_like(qi))
                dom = jnp.where(mg, doi, jnp.zeros_like(doi))
                delta = jnp.sum(jnp.where(mg, prod, 0.0), axis=1, keepdims=True)
                p = jnp.exp(_dot_nt(qm, kxi) + bb - lse_ref[rows, g:g + 1])
                ds = (p * (_dot_nt(dom, vxi) - delta)).astype(BF16)
                dvx_ref[0:n, :] += _dot_tn(p.astype(BF16), dom)
                dkx_ref[0:n, :] += _dot_tn(ds, qm)
                dq_acc = jnp.where(mg, _dot(ds, kxi), dq_acc)
            dq_ref[rows, :] = dq_acc

    col = lambda kv: (0, kv)
    return _call(
        body, name="attn_bwd", grid=(NKV,),
        in_specs=[pl.BlockSpec((T, WK), col), pl.BlockSpec((T, WK), col), pl.BlockSpec((T, WK), col), pl.BlockSpec((T, WK), col),
                  pl.BlockSpec((T, WK), col), pl.BlockSpec((T, 128), col), _const((AB, T))],
        out_specs=[pl.BlockSpec((T, WK), col), pl.BlockSpec((T, WK), col), pl.BlockSpec((T, WK), col)],
        out_shape=[jax.ShapeDtypeStruct((T, QW), F32)] * 3,
        args=[q, kx, vx, o, dmix, lse, bias], sem=("arbitrary",), duties=duties)


def _softplus(x):
    return jnp.maximum(x, 0.0) + jnp.log1p(jnp.exp(-jnp.abs(x)))


def _causal_conv(u, zs, cw_ref, cb_ref):
    zs[8:, :] = u
    sh1, sh2, sh3 = (zs[8 - m:8 - m + L, :] for m in (1, 2, 3))
    return cb_ref[...] + cw_ref[3:4, :] * u + cw_ref[2:3, :] * sh1 + cw_ref[1:2, :] * sh2 + cw_ref[0:1, :] * sh3


def _ssd_chunk_common(xc, dtr, dtb_ref, alx_ref, e_ref):
    sg = _sigmoid(xc)
    act = xc * sg
    pre = dtr + dtb_ref[...]
    dt_x = _dot_hi(_softplus(pre), e_ref[...])
    a_x = -jnp.exp(alx_ref[...])
    ri = lax.broadcasted_iota(jnp.int32, (L, L), 0)
    ci = lax.broadcasted_iota(jnp.int32, (L, L), 1)
    tri = ri >= ci
    acs_x = _dot_hi(tri, dt_x * a_x, a_is_01=True)
    return dict(sg=sg, act=act, pre=pre, dt_x=dt_x, a_x=a_x, tri=tri, acs_x=acs_x)


def _decay(acs_x, acs_t, h, tri):
    col = acs_x[:, HD * h:HD * h + 1]
    row = acs_t[HD * h:HD * h + 1, :]
    return jnp.exp(jnp.where(tri, col - row, NEG))


def _ssd_fwd(xbc, z, dtr, convw, convb, dtb, alx, dskx, ssmn, e, duties=()):
    def body(u_ref, z_ref, dtr_ref, cw_ref, cb_ref, dtb_ref, alx_ref, dsk_ref, sn_ref, e_ref,
             yn_ref, y_ref, hs_ref, xc_ref, zs, hst):
        @pl.when(pl.program_id(0) == 0)
        def _():
            zs[0:8, :] = jnp.zeros((8, CONV_C), F32)
            hst[...] = jnp.zeros_like(hst)

        u = u_ref[...]
        xc = _causal_conv(u, zs, cw_ref, cb_ref)
        xc_ref[...] = xc
        zs[0:8, :] = u[L - 8:, :]
        cm = _ssd_chunk_common(xc, dtr_ref[...], dtb_ref, alx_ref, e_ref)
        act, dt_x, acs_x, tri = cm["act"], cm["dt_x"], cm["acs_x"], cm["tri"]
        xs = act[:, :SSM_W]
        acs_l = acs_x[L - 1:L, :]
        lam_x = jnp.exp(acs_x)
        w_x = jnp.exp(acs_l - acs_x)
        gam_x = jnp.exp(acs_l)
        acs_t = acs_x.T
        xd = xs * dt_x
        xb = xd.astype(BF16)
        xw = (xd * w_x).astype(BF16)
        lo = lax.broadcasted_iota(jnp.int32, (1, 128), 1) < HD
        hs_ref[0] = hst[...]
        pieces = []
        for grp in range(2):
            bb = act[:, SSM_W + SSM_N * grp:SSM_W + SSM_N * (grp + 1)].astype(BF16)
            cb_ = act[:, SSM_W + 2 * SSM_N + SSM_N * grp:SSM_W + 2 * SSM_N + SSM_N * (grp + 1)].astype(BF16)
            cbm = _dot_nt(cb_, bb)
            for jj in range(4):
                j = 4 * grp + jj
                sl = slice(128 * j, 128 * j + 128)
                m0 = (cbm * _decay(acs_x, acs_t, 2 * j, tri)).astype(BF16)
                m1 = (cbm * _decay(acs_x, acs_t, 2 * j + 1, tri)).astype(BF16)
                x2 = xb[:, sl]
                ydiag = jnp.where(lo, _dot(m0, x2), _dot(m1, x2))
                hprev = hst[j]
                yoff = lam_x[:, sl] * _dot(cb_, hprev.astype(BF16))
                pieces.append(ydiag + yoff)
                hst[j] = gam_x[:, sl] * hprev + _dot_tn(bb, xw[:, sl])
        y = jnp.concatenate(pieces, axis=1) + dsk_ref[...] * xs
        y_ref[...] = y
        zv = z_ref[...]
        yz = y * (zv * _sigmoid(zv))
        half = SSM_W // 2
        yn = jnp.concatenate([yz[:, :half] * _rs(yz[:, :half]), yz[:, half:] * _rs(yz[:, half:])], axis=1)
        yn_ref[...] = (yn * sn_ref[...]).astype(BF16)

    return _call(
        body, name="ssd_fwd", grid=(NCH,),
        in_specs=[_rows(L, CONV_C), _rows(L, SSM_W), _rows(L, SSM_H), _const((4, CONV_C)), _const((1, CONV_C)), _const((1, SSM_H)),
                  _const((1, SSM_W)), _const((1, SSM_W)), _const((1, SSM_W)), _const((SSM_H, SSM_W))],
        out_specs=[_rows(L, SSM_W), _rows(L, SSM_W), pl.BlockSpec((1, 8, SSM_N, 128), lambda c: (c, 0, 0, 0)), _rows(L, CONV_C)],
        out_shape=[jax.ShapeDtypeStruct((T, SSM_W), BF16), jax.ShapeDtypeStruct((T, SSM_W), F32),
                   jax.ShapeDtypeStruct((NCH, 8, SSM_N, 128), F32), jax.ShapeDtypeStruct((T, CONV_C), F32)],
        scratch=[pltpu.VMEM((8 + L, CONV_C), F32), pltpu.VMEM((8, SSM_N, 128), F32)],
        args=[xbc, z, dtr, convw, convb, dtb, alx, dskx, ssmn, e], sem=("arbitrary",), duties=duties)


def _ssd_bwd(dmix, xbc, xconv, z, dtr, y, hs, convw, dtb, alx, dskx, ssmn, e, e1, duties=()):
    rev = lambda i: (NCH - 1 - i, 0)

    def body(dyn_ref, u_ref, xc_ref, z_ref, dtr_ref, y_ref, hs_ref, cw_ref, dtb_ref, alx_ref, dsk_ref, sn_ref, e_ref, e1_ref,
             dxbc_ref, dz_ref, ddt_ref, dcw_ref, dcb_ref, dsn_ref, dpar_ref, dh, zd, colbuf):
        step = pl.program_id(0)

        @pl.when(step == 0)
        def _():
            for r in (dh, dcw_ref, dcb_ref, dsn_ref, dpar_ref):
                r[...] = jnp.zeros_like(r)
            zd[L:, :] = jnp.zeros((8, CONV_C), F32)

        u = u_ref[...]
        xc = xc_ref[...]
        cm = _ssd_chunk_common(xc, dtr_ref[...], dtb_ref, alx_ref, e_ref)
        sg, act, pre, dt_x, a_x, tri, acs_x = (cm[k] for k in ("sg", "act", "pre", "dt_x", "a_x", "tri", "acs_x"))
        xs = act[:, :SSM_W]
        acs_l = acs_x[L - 1:L, :]
        lam_x = jnp.exp(acs_x)
        w_x = jnp.exp(acs_l - acs_x)
        gam_x = jnp.exp(acs_l)
        acs_t = acs_x.T
        xd = xs * dt_x
        xb = xd.astype(BF16)
        xdw = xd * w_x
        xw = xdw.astype(BF16)
        lo = lax.broadcasted_iota(jnp.int32, (1, 128), 1) < HD
        row8 = lax.broadcasted_iota(jnp.int32, (8, 1), 0)

        dyn = dyn_ref[...]
        yv = y_ref[...]
        zv = z_ref[...]
        sz = _sigmoid(zv)
        siluz = zv * sz
        yz = yv * siluz
        half = SSM_W // 2
        gy = dyn * sn_ref[...]
        dyz_parts, yzn_parts = [], []
        for hf in range(2):
            part = yz[:, hf * half:(hf + 1) * half]
            r = _rs(part)
            pn = part * r
            gp = gy[:, hf * half:(hf + 1) * half]
            dyz_parts.append(r * (gp - pn * jnp.mean(gp * pn, axis=-1, keepdims=True)))
            yzn_parts.append(pn)
        dyz = jnp.concatenate(dyz_parts, axis=1)
        dsn_ref[...] += jnp.sum(dyn * jnp.concatenate(yzn_parts, axis=1), axis=0, keepdims=True)
        dy = dyz * siluz
        dz_ref[...] = dyz * yv * _dsilu(zv, sz)

        colbuf[...] = jnp.zeros_like(colbuf)
        dx_pieces, dacs_pieces, dacsl_pieces, db_pieces, dc_pieces = [], [], [], [], []
        for grp in range(2):
            bb = act[:, SSM_W + SSM_N * grp:SSM_W + SSM_N * (grp + 1)].astype(BF16)
            cb_ = act[:, SSM_W + 2 * SSM_N + SSM_N * grp:SSM_W + 2 * SSM_N + SSM_N * (grp + 1)].astype(BF16)
            cbm = _dot_nt(cb_, bb)
            dcbm = jnp.zeros((L, L), F32)
            dc_g = jnp.zeros((L, SSM_N), F32)
            db_g = jnp.zeros((L, SSM_N), F32)
            for jj in range(4):
                j = 4 * grp + jj
                sl = slice(128 * j, 128 * j + 128)
                dy2 = dy[:, sl]
                dy2b = dy2.astype(BF16)
                d0 = _decay(acs_x, acs_t, 2 * j, tri)
                d1 = _decay(acs_x, acs_t, 2 * j + 1, tri)
                m0 = cbm * d0
                m1 = cbm * d1
                x2 = xb[:, sl]
                hprev = hs_ref[0, j]
                hprevb = hprev.astype(BF16)
                dhn = dh[j]
                dhnb = dhn.astype(BF16)
                g2 = _dot(bb, dhnb)
                dx_pieces.append(jnp.where(lo, _dot_tn(m0.astype(BF16), dy2b), _dot_tn(m1.astype(BF16), dy2b)) + w_x[:, sl] * g2)
                zero = jnp.zeros_like(dy2b)
                dm0 = _dot_nt(jnp.where(lo, dy2b, zero), x2)
                dm1 = _dot_nt(jnp.where(lo, zero, dy2b), x2)
                dcbm = dcbm + dm0 * d0 + dm1 * d1
                e0 = dm0 * m0
                e1v = dm1 * m1
                colbuf[:, 2 * j:2 * j + 1] = jnp.sum(e0, axis=1, keepdims=True) - jnp.sum(e0.T, axis=1, keepdims=True)
                colbuf[:, 2 * j + 1:2 * j + 2] = jnp.sum(e1v, axis=1, keepdims=True) - jnp.sum(e1v.T, axis=1, keepdims=True)
                yoff = lam_x[:, sl] * _dot(cb_, hprevb)
                gxw = g2 * xdw[:, sl]
                dacs_pieces.append(dy2 * yoff - gxw)
                dacsl_pieces.append(jnp.sum(gxw, axis=0, keepdims=True) + gam_x[:, sl] * jnp.sum(dhn * hprev, axis=0, keepdims=True))
                dyl = (dy2 * lam_x[:, sl]).astype(BF16)
                dc_g = dc_g + _dot_nt(dyl, hprevb)
                db_g = db_g + _dot_nt(xw[:, sl], dhnb)
                dh[j] = gam_x[:, sl] * dhn + _dot_tn(cb_, dyl)
            dcbb = dcbm.astype(BF16)
            dc_pieces.append(dc_g + _dot(dcbb, bb))
            db_pieces.append(db_g + _dot_tn(dcbb, cb_))

        dxd = jnp.concatenate(dx_pieces, axis=1)
        rowi = lax.broadcasted_iota(jnp.int32, (L, 1), 0)
        dacs_x = (jnp.concatenate(dacs_pieces, axis=1) + _dot_hi(colbuf[...], e1_ref[...])
                  + jnp.where(rowi == L - 1, jnp.concatenate(dacsl_pieces, axis=1), 0.0))
        upper = lax.broadcasted_iota(jnp.int32, (L, L), 0) <= lax.broadcasted_iota(jnp.int32, (L, L), 1)
        dadt_x = _dot_hi(upper, dacs_x, a_is_01=True)
        ddt_x = dxd * xs + dadt_x * a_x
        ddtr = _dot_nt_hi(ddt_x, e_ref[...]) * _sigmoid(pre)
        ddt_ref[...] = jnp.zeros_like(ddt_ref)
        ddt_ref[:, 0:SSM_H] = ddtr
        dalx =jnp.sum(dadt_x * dt_x, axis=0, keepdims=True) * a_x
        ddskx = jnp.sum(dy * xs, axis=0, keepdims=True)
        par_x = jnp.where(row8 == 1, dalx, 0.0) + jnp.where(row8 == 2, ddskx, 0.0)
        dpar_ref[...] += _dot_nt_hi(par_x, e_ref[...]) + jnp.where(row8 == 0, jnp.sum(ddtr, axis=0, keepdims=True), 0.0)

        dxs = dxd * dt_x + dsk_ref[...] * dy
        dact = jnp.concatenate([dxs] + db_pieces + dc_pieces, axis=1)
        du = dact * _dsilu(xc, sg)
        dcb_ref[...] += jnp.sum(du, axis=0, keepdims=True)
        zd[0:L, :] = du
        f1, f2, f3 = (zd[m:m + L, :] for m in (1, 2, 3))
        dxbc_ref[...] = cw_ref[3:4, :] * du + cw_ref[2:3, :] * f1 + cw_ref[1:2, :] * f2 + cw_ref[0:1, :] * f3
        dcw = jnp.zeros((8, CONV_C), F32)
        for k, shifted in enumerate((f3, f2, f1, du)):
            dcw = dcw + jnp.where(row8 == k, jnp.sum(shifted * u, axis=0, keepdims=True), 0.0)
        dcw_ref[...] += dcw
        zd[L:, :] = du[:8, :]

    return _call(
        body, name="ssd_bwd", grid=(NCH,),
        in_specs=[pl.BlockSpec((L, SSM_W), lambda i: (NCH - 1 - i, 1)), pl.BlockSpec((L, CONV_C), rev), pl.BlockSpec((L, CONV_C), rev),
                  pl.BlockSpec((L, SSM_W), rev), pl.BlockSpec((L, SSM_H), rev), pl.BlockSpec((L, SSM_W), rev),
                  pl.BlockSpec((1, 8, SSM_N, 128), lambda i: (NCH - 1 - i, 0, 0, 0)),
                  _const((4, CONV_C)), _const((1, SSM_H)), _const((1, SSM_W)), _const((1, SSM_W)), _const((1, SSM_W)),
                  _const((SSM_H, SSM_W)), _const((128, SSM_W))],
        out_specs=[pl.BlockSpec((L, CONV_C), rev), pl.BlockSpec((L, SSM_W), rev), pl.BlockSpec((L, 128), rev),
                   _const((8, CONV_C)), _const((1, CONV_C)), _const((1, SSM_W)), _const((8, SSM_H))],
        out_shape=[jax.ShapeDtypeStruct((T, CONV_C), F32), jax.ShapeDtypeStruct((T, SSM_W), F32), jax.ShapeDtypeStruct((T, 128), F32),
                   jax.ShapeDtypeStruct((8, CONV_C), F32), jax.ShapeDtypeStruct((1, CONV_C), F32), jax.ShapeDtypeStruct((1, SSM_W), F32),
                   jax.ShapeDtypeStruct((8, SSM_H), F32)],
        scratch=[pltpu.VMEM((8, SSM_N, 128), F32), pltpu.VMEM((L + 8, CONV_C), F32), pltpu.VMEM((L, 128), F32)],
        args=[dmix, xbc, xconv, z, dtr, y, hs, convw, dtb, alx, dskx, ssmn, e, e1], sem=("arbitrary",), duties=duties)


def _adam_math(w, g, m, v):
    m = ADAM_B1 * m + (1.0 - ADAM_B1) * g
    v = ADAM_B2 * v + (1.0 - ADAM_B2) * (g * g)
    m_hat = m / (1.0 - ADAM_B1 ** ADAM_STEP)
    v_hat = v / (1.0 - ADAM_B2 ** ADAM_STEP)
    delta = -ADAM_LR * (m_hat / (jnp.sqrt(v_hat) + ADAM_EPS) + ADAM_WD * w)
    return delta, m, v


def _adamw(w, m, v, parts, name, after=()):
    rows, cols = w.shape
    tr = rows if rows <= 512 else 256
    assert rows % tr == 0
    n_parts = len(parts)

    def body(*refs):
        w_ref, m_ref, v_ref = refs[:3]
        p_refs = refs[3:3 + n_parts]
        g_ref, d_ref, nm_ref, nv_ref = refs[3 + n_parts + len(after):]
        g = p_refs[0][...].astype(F32)
        for p in p_refs[1:]:
            g = g + p[...].astype(F32)
        delta, nm, nv = _adam_math(w_ref[...], g, m_ref[...], v_ref[...])
        g_ref[...] = g
        d_ref[...] = delta
        nm_ref[...] = nm
        nv_ref[...] = nv

    spec = pl.BlockSpec((tr, cols), lambda i: (i, 0))
    return pl.pallas_call(
        body, name=name, grid=(rows // tr,),
        in_specs=[spec] * (3 + n_parts) + [ANY] * len(after), out_specs=[spec] * 4,
        out_shape=[jax.ShapeDtypeStruct((rows, cols), F32)] * 4,
        compiler_params=_params(("parallel",)),
    )(w, m, v, *parts, *after)


COL_TILE = 512


def _adamw_sharded(ws, ms, vs, chip_sums, from_chips, other_chips, name, after=()):
    k = len(ws)
    rows, cols = ws[0].shape
    prow = chip_sums[0].shape[0]
    assert cols % COL_TILE == 0 and prow >= rows and all(a.shape == ws[0].shape for a in ws)

    def body(ids_ref, *refs):
        ins, outs = refs[:7 * k], refs[7 * k + len(after):]
        for a in range(k):
            w_ref, m_ref, v_ref, s_ref, r1_ref, r2_ref, r3_ref = ins[7 * a:7 * a + 7]
            g = s_ref[...]
            for r in (r1_ref, r2_ref, r3_ref):
                g = g + r[0].astype(F32)
            g = g[:rows]
            delta, nm, nv = _adam_math(w_ref[...], g, m_ref[...], v_ref[...])
            for ref, val in zip(outs[4 * a:4 * a + 4], (g, delta, nm, nv)):
                ref[...] = val

    spec = pl.BlockSpec((rows, COL_TILE), lambda i, ids: (0, i))
    part = lambda j: pl.BlockSpec((1, prow, COL_TILE), lambda i, ids: (ids[j], 0, i))
    one = [spec, spec, spec, pl.BlockSpec((prow, COL_TILE), lambda i, ids: (0, i)), part(0), part(1), part(2)]
    args = [x for a in range(k) for x in (ws[a], ms[a], vs[a], chip_sums[a], from_chips[a], from_chips[a], from_chips[a])]
    res = pl.pallas_call(
        body, name=name,
        grid_spec=pltpu.PrefetchScalarGridSpec(
            num_scalar_prefetch=1, grid=(cols // COL_TILE,),
            in_specs=one * k + [ANY] * len(after), out_specs=[spec] * (4 * k)),
        out_shape=[jax.ShapeDtypeStruct((rows, cols), F32)] * (4 * k),
        compiler_params=_params(("parallel",)),
    )(other_chips, *args, *after)
    return [tuple(res[4 * a:4 * a + 4]) for a in range(k)]


def _chip_sum(mines, recvs, name):
    k = len(mines)
    rows, cols = mines[0].shape[1:]

    def body(*refs):
        own_chip = pl.program_id(0) == 2 * lax.axis_index("x") + lax.axis_index("y")
        for a in range(k):
            a_ref, b_ref = refs[2 * a:2 * a + 2]
            s_ref, sb_ref = refs[2 * k + 2 * a:2 * k + 2 * a + 2]
            s = a_ref[0] + b_ref[0].astype(F32)
            sb_ref[0] = s.astype(BF16)

            @pl.when(own_chip)
            def _(s_ref=s_ref, s=s):
                s_ref[...] = s

    by_chip = pl.BlockSpec((1, rows, cols), lambda c: (c, 0, 0))
    res = pl.pallas_call(
        body, name=name, grid=(N_DEV // 2,),
        in_specs=[by_chip, by_chip] * k, out_specs=[_const((rows, cols)), by_chip] * k,
        out_shape=[jax.ShapeDtypeStruct((rows, cols), F32), jax.ShapeDtypeStruct((N_DEV // 2, rows, cols), BF16)] * k,
        compiler_params=_params(("arbitrary",)),
    )(*[x for pair in zip(mines, recvs) for x in pair])
    return [tuple(res[2 * a:2 * a + 2]) for a in range(k)]


def _all_reduce_small(v, after=()):
    rows = v.shape[0]

    def body(v_ref, *rest):
        out_ref, gath, send_sems, recv_sems = rest[len(after):]
        x, y, c = _place()
        me, sibling = (x, y, c), (x, y, 1 - c)
        chips = [(1 - x, y), (x, 1 - y), (1 - x, 1 - y)]

        def blk(px, py, pc):
            return gath.at[pl.ds((4 * px + 2 * py + pc) * rows, rows), :]

        def copy(k, block, to, src=None):
            return pltpu.make_async_remote_copy(src_ref=blk(*block) if src is None else src, dst_ref=blk(*block),
                                                send_sem=send_sems.at[k], recv_sem=recv_sems.at[k], device_id=to, device_id_type=MESH)

        gath[pl.ds((4 * x + 2 * y + c) * rows, rows), :] = v_ref[...]
        first = [copy(0, me, sibling, src=v_ref)] + [copy(1 + j, me, (*chip, c), src=v_ref) for j, chip in enumerate(chips)]
        for cp in first:
            cp.start()
        passed = [copy(4 + j, (*chip, c), sibling) for j, chip in enumerate(chips)]
        for j, chip in enumerate(chips):
            copy(1 + j, (*chip, c), me).wait_recv()
            passed[j].start()
        copy(0, sibling, me).wait_recv()
        for j, chip in enumerate(chips):
            copy(4 + j, (*chip, 1 - c), me).wait_recv()
        for cp in first + passed:
            cp.wait_send()
        acc = gath[0:rows, :]
        for d in range(1, N_DEV):
            acc = acc + gath[d * rows:(d + 1) * rows, :]
        out_ref[...] = acc

    vm = pl.BlockSpec(memory_space=pltpu.VMEM)
    return pl.pallas_call(
        body, name="all_reduce_small",
        in_specs=[vm] + [ANY] * len(after), out_specs=vm,
        out_shape=jax.ShapeDtypeStruct(v.shape, F32),
        scratch_shapes=[pltpu.VMEM((N_DEV * rows, 128), F32), pltpu.SemaphoreType.DMA((7,)), pltpu.SemaphoreType.DMA((7,))],
    )(v, *after)


def _rope_tables(positions):
    inv_freq = ROPE_THETA ** (-jnp.arange(0, HD, 2, dtype=F32) / HD)
    ang = positions.reshape(T).astype(F32)[:, None] * inv_freq
    ang = jnp.concatenate([ang, ang, ang, ang], axis=-1)
    lo_half = (jnp.arange(128) % HD) < (HD // 2)
    return jnp.cos(ang), jnp.where(lo_half, -jnp.sin(ang), jnp.sin(ang))


def _selectors():
    lane = jnp.arange(QW)
    e = (lane[None, :] // HD == jnp.arange(SSM_H)[:, None]).astype(F32)
    e1 = ((lane[None, :] == HD * jnp.arange(128)[:, None]) & (jnp.arange(128)[:, None] < SSM_H)).astype(F32)
    src = jnp.arange(KVW)
    ex = ((lane[None, :] // (HD * (NQ // NKV)) == src[:, None] // HD) & (lane[None, :] % HD == src[:, None] % HD)).astype(F32)
    return e, e1, ex


WEIGHTS = ['ffn1_pre_norm', 'ffn1_w_gate', 'ffn1_w_up', 'ffn1_w_down', 'ffn1_post_norm', 'mix_pre_norm', 'w_in', 'conv_w', 'conv_b',
           'dt_bias', 'a_log', 'd_skip', 'ssm_norm', 'w_out', 'mix_post_norm', 'ffn2_pre_norm', 'ffn2_w_gate', 'ffn2_w_up',
           'ffn2_w_down', 'ffn2_post_norm']
COL_SHARDED = ['ffn1_w_gate', 'ffn1_w_up', 'ffn2_w_gate', 'ffn2_w_up', 'w_in']
ROW_SHARDED = ['ffn1_w_down', 'ffn2_w_down', 'w_out']
BIG = COL_SHARDED + ROW_SHARDED
FFN_BIG = COL_SHARDED[:4] + ROW_SHARDED[:2]
SMALL = ['ffn1_pre_norm', 'ffn1_post_norm', 'mix_pre_norm', 'conv_b', 'dt_bias', 'a_log', 'd_skip', 'ssm_norm', 'mix_post_norm',
         'ffn2_pre_norm', 'ffn2_post_norm']
FFN1 = ['ffn1_w_gate', 'ffn1_w_up', 'ffn1_w_down']
FFN2 = ['ffn2_w_gate', 'ffn2_w_up', 'ffn2_w_down']


def _wire_block(name, a):
    if name in FFN_BIG:
        return jnp.pad(a.astype(BF16), ((0, FSH - FSR), (0, 0)))
    if name == "w_in":
        return jnp.pad(a.astype(BF16), ((0, ISW - ISR), (0, 0)))
    return a if name == "conv_w" else a.astype(BF16)


def _whole_from_gathered(name, a):
    if name == "conv_w":
        return jnp.transpose(a, (1, 0, 2)).reshape(a.shape[1], -1)
    return a.reshape(-1, D)


def _step(x, positions, target, small, blocks=None, whole=None):
    dist = blocks is not None
    core = "mesh" if dist else 0
    w = dict(small)
    if whole:
        w.update(whole)

    def live(names):
        return [FSR if n in FFN_BIG else None for n in names]

    def gather(names):
        return [_gather_duty([_wire_block(n, blocks[n]) for n in names], live(names))] if dist else []

    def put(names, results):
        if dist:
            for n, r in zip(names, results[0]):
                w[n] = _whole_from_gathered(n, r)

    g, sums, red = {}, {}, {}

    def swap(names):
        return [_swap_duty([g[n][1] for n in names], live(names))] if dist else []

    def chip_sums(names, from_sibling):
        if dist:
            res = _chip_sum([g[n][0] for n in names], list(from_sibling), "chip_sum_" + names[0])
            sums.update(zip(names, res))

    def exchange(names):
        return [_exchange_duty([sums[n][1] for n in names], live(names))] if dist else []

    def reduced(names, from_chips):
        if dist:
            for n, recv in zip(names, from_chips):
                red[n] = (sums[n][0], recv)

    cos, sin_s = _rope_tables(positions)
    e, e1, exf = _selectors()
    bias = _attn_bias()
    alx = jnp.repeat(w["a_log"], HD, axis=1)
    dskx = jnp.repeat(w["d_skip"], HD, axis=1)

    if dist:
        put(FFN1, _comm_only(gather(FFN1), "gather_ffn1"))
    (x1, n1, a1, b1, hm1, h1), got = _ffn_fwd(x, w["ffn1_pre_norm"], w["ffn1_w_gate"], w["ffn1_w_up"], w["ffn1_w_down"],
                                              w["ffn1_post_norm"], "ffn1_fwd", gather(["w_in", "conv_w"]))
    put(["w_in", "conv_w"], got)
    (n2, q, kx, vx, xbc, z, dtr), got = _inproj_fwd(x1, w["mix_pre_norm"], w["w_in"], cos, sin_s, exf.astype(BF16), gather(["w_out"]))
    put(["w_out"], got)
    (attn, lse), got = _attn_fwd(q, kx, vx, bias, gather(FFN2[:2]))
    put(FFN2[:2], got)
    (yn, y, hs, xconv), got = _ssd_fwd(xbc, z, dtr, w["conv_w"], w["conv_b"], w["dt_bias"], alx, dskx, w["ssm_norm"], e, gather(FFN2[2:]))
    put(FFN2[2:], got)
    x2, h2 = _outproj_fwd(x1, attn, yn, w["w_out"], w["mix_post_norm"])
    (dx3, n3, a3, b3, hm3, h3, ss), _ = _ffn_fwd(x2, w["ffn2_pre_norm"], w["ffn2_w_gate"], w["ffn2_w_up"], w["ffn2_w_down"],
                                                 w["ffn2_post_norm"], "ffn2_fwd", target=target)

    (dx2, da3, db3, dh3, g["ffn2_pre_norm"], g["ffn2_post_norm"]), _ = _ffn_bwd(
        dx3, x2, a3, b3, h3, w["ffn2_pre_norm"], w["ffn2_post_norm"], w["ffn2_w_gate"], w["ffn2_w_up"], w["ffn2_w_down"], "ffn2_bwd")
    g["ffn2_w_down"] = _matmul_tn(hm3, dh3, "ffn2_dwd", core)[0]
    g["ffn2_w_gate"] = _matmul_tn(da3, n3, "ffn2_dwg", core)[0]
    g["ffn2_w_up"] = _matmul_tn(db3, n3, "ffn2_dwu", core)[0]

    (dh2, dmix, g["mix_post_norm"]), got = _outproj_bwd(dx2, h2, w["mix_post_norm"], w["w_out"], swap(FFN2))
    chip_sums(FFN2, got[0] if dist else None)
    g["w_out"] = _dwout(attn, yn, dh2, core)
    (dq, dkx, dvx), got = _attn_bwd(q, kx, vx, attn, dmix, lse, bias, exchange(FFN2) + swap(["w_out"]))
    if dist:
        reduced(FFN2, got[0])
        chip_sums(["w_out"], got[1])
    (dxbc, dz, ddt, dcw, g["conv_b"], g["ssm_norm"], dpar), got = _ssd_bwd(
        dmix, xbc, xconv, z, dtr, y, hs, w["conv_w"], w["dt_bias"], alx, dskx, w["ssm_norm"], e, e1, exchange(["w_out"]))
    reduced(["w_out"], got[0] if dist else None)
    g["conv_w"] = dcw[0:4]
    g["dt_bias"], g["a_log"], g["d_skip"] = dpar[0:1], dpar[1:2], dpar[2:3]
    dx1, dproj, g["mix_pre_norm"] = _inproj_bwd(dx2, dq, dkx, dvx, dxbc, dz, ddt, x1, w["mix_pre_norm"], w["w_in"], cos, sin_s, exf)
    g["w_in"] = _matmul_tn(dproj, n2, "dwin", core)[0]

    started = {}

    def start(n):
        started[n] = _exchange_start(sums[n][1], live([n])[0], "start_exchange_" + n)
        return [started[n]["token"]]

    after = []
    if dist:
        chip_sums(["w_in"], _comm_only(swap(["w_in"]), "swap_w_in")[0])
        after = start("w_in")
    (dx0, da1, db1, dh1, g["ffn1_pre_norm"], g["ffn1_post_norm"]), _ = _ffn_bwd(
        dx1, x, a1, b1, h1, w["ffn1_pre_norm"], w["ffn1_post_norm"], w["ffn1_w_gate"], w["ffn1_w_up"], w["ffn1_w_down"], "ffn1_bwd",
        after)
    total = None
    if dist:
        flat = jnp.concatenate([g[n].reshape(-1) for n in SMALL] + [g["conv_w"].reshape(-1), ss[0, 0:1]])
        rows = -(-flat.shape[0] // 128 // 8) * 8
        total = _all_reduce_small(jnp.pad(flat, (0, rows * 128 - flat.shape[0])).reshape(rows, 128), after)
        after = [total]
    g["ffn1_w_down"], _ = _matmul_tn(hm1, dh1, "ffn1_dwd", core, after=after)
    g["ffn1_w_gate"], got = _matmul_tn(da1, n1, "ffn1_dwg", core, duties=swap(["ffn1_w_down"]))
    if dist:
        chip_sums(["ffn1_w_down"], got[0])
        after = start("ffn1_w_down")
    g["ffn1_w_up"], got = _matmul_tn(db1, n1, "ffn1_dwu", core, after=after, duties=swap(["ffn1_w_gate"]))
    if dist:
        chip_sums(["ffn1_w_gate"], got[0])
        after = start("ffn1_w_gate")
        chip_sums(["ffn1_w_up"], _comm_only(swap(["ffn1_w_up"]), "swap_ffn1_w_up", after=after)[0])
        start("ffn1_w_up")
    return ss, dx0, g, red, {n: (sums[n][0], started[n]) for n in started}, total


def kernel(x, positions, ffn1_pre_norm, ffn1_w_gate, ffn1_w_up, ffn1_w_down, ffn1_post_norm, mix_pre_norm, w_in, conv_w, conv_b, dt_bias, a_log, d_skip, ssm_norm, w_out, mix_post_norm, ffn2_pre_norm, ffn2_w_gate, ffn2_w_up, ffn2_w_down, ffn2_post_norm, loss_target, m_ffn1_pre_norm, m_ffn1_w_gate, m_ffn1_w_up, m_ffn1_w_down, m_ffn1_post_norm, m_mix_pre_norm, m_w_in, m_conv_w, m_conv_b, m_dt_bias, m_a_log, m_d_skip, m_ssm_norm, m_w_out, m_mix_post_norm, m_ffn2_pre_norm, m_ffn2_w_gate, m_ffn2_w_up, m_ffn2_w_down, m_ffn2_post_norm, v_ffn1_pre_norm, v_ffn1_w_gate, v_ffn1_w_up, v_ffn1_w_down, v_ffn1_post_norm, v_mix_pre_norm, v_w_in, v_conv_w, v_conv_b, v_dt_bias, v_a_log, v_d_skip, v_ssm_norm, v_w_out, v_mix_post_norm, v_ffn2_pre_norm, v_ffn2_w_gate, v_ffn2_w_up, v_ffn2_w_down, v_ffn2_post_norm):
    given = dict(locals())
    drop = lambda n, a: a if n in SMALL else (a[0].T if n in COL_SHARDED else a[0])
    w = {n: drop(n, given[n]) for n in WEIGHTS}
    m = {n: drop(n, given["m_" + n]) for n in WEIGHTS}
    v = {n: drop(n, given["v_" + n]) for n in WEIGHTS}
    cx, cy, cc = _place()
    others = [2 * (1 - cx) + cy, 2 * cx + (1 - cy), 2 * (1 - cx) + (1 - cy)]

    _, grad_x, g, red, pending, total = _step(x[0], positions, loss_target[0], {n: w[n] for n in SMALL},
                                              blocks={n: w[n] for n in BIG + ["conv_w"]})
    total = total.reshape(-1)
    chip_ids = jnp.stack(others).astype(jnp.int32)
    out_g, out_d, out_m, out_v = {}, {}, {}, {}

    def update(names, sums, recvs, label, after=()):
        res = _adamw_sharded([w[n] for n in names], [m[n] for n in names], [v[n] for n in names], sums, recvs, chip_ids,
                             "adamw_" + label, after)
        for n, (gn, dn, mn, vn) in zip(names, res):
            out_g[n], out_d[n], out_m[n], out_v[n] = gn, dn, mn, vn

    last_start = [pending["ffn1_w_up"][1]["token"]]
    update(FFN2, [red[n][0] for n in FFN2], [red[n][1] for n in FFN2], "ffn2", last_start)
    update(["w_out"], [red["w_out"][0]], [red["w_out"][1]], "w_out", last_start)

    rows = total.shape[0] // 128
    sizes = [w[n].size for n in SMALL]
    offs = [0]
    for s_ in sizes:
        offs.append(offs[-1] + s_)
    gs = {n: total[offs[i]:offs[i + 1]].reshape(w[n].shape) for i, n in enumerate(SMALL)}
    gcw = total[offs[-1]:offs[-1] + 4 * CONV_C].reshape(4, CONV_C)
    loss = 0.5 * total[offs[-1] + 4 * CONV_C] / D
    gs["conv_w"] = lax.dynamic_slice_in_dim(gcw, (4 * cx + 2 * cy + cc) * (CONV_C // N_DEV), CONV_C // N_DEV, axis=1)
    names = SMALL + ["conv_w"]

    def pack(d):
        flat = jnp.concatenate([d[n].reshape(-1) for n in names])
        return jnp.pad(flat, (0, rows * 128 - flat.shape[0])).reshape(rows, 128)

    pg, pd, pm, pv = _adamw(pack(w), pack(m), pack(v), [pack(gs)], "adamw_small", last_start)
    done = [pg] + [out_v[n] for n in BIG if n not in pending]
    update(["w_in"], [pending["w_in"][0]], [_exchange_wait(pending["w_in"][1], done, "wait_exchange_w_in")], "w_in")
    done = [out_v["w_in"]]
    update(FFN1, [pending[n][0] for n in FFN1], [_exchange_wait(pending[n][1], done, "wait_exchange_" + n) for n in FFN1], "ffn1")
    o2 = [0]
    for n in names:
        o2.append(o2[-1] + w[n].size)
    for i, n in enumerate(names):
        for dst, src in ((out_g, pg), (out_d, pd), (out_m, pm), (out_v, pv)):
            dst[n] = src.reshape(-1)[o2[i]:o2[i + 1]].reshape(w[n].shape)

    outs = [loss, grad_x[None]]
    for d in (out_g, out_d, out_m, out_v):
        outs += [d[n] if n in SMALL else (d[n].T[None] if n in COL_SHARDED else d[n][None]) for n in WEIGHTS]
    return tuple(outs)
```

```python
import functools

import jax
import jax.numpy as jnp
from jax import lax
from jax.experimental import pallas as pl
from jax.experimental.pallas import tpu as pltpu

F32 = jnp.float32
BF16 = jnp.bfloat16
MESH = pl.DeviceIdType.MESH

N_DEV = 8
T = 2048
D = 1024
FF = 2816
FSR = FF // N_DEV
FSH = 384
FFP = N_DEV * FSH
HD = 64
NQ = 16
NKV = 4
QW = NQ * HD
KVW = NKV * HD
SSM_W = 1024
SSM_H = 16
SSM_N = 128
CONV_C = SSM_W + 2 * 2 * SSM_N
IN_COLS = 4112
INP = 4224
ISR = IN_COLS // N_DEV
ISW = 528
ISG = 640
L = 128
NCH = T // L
AB = 256
NAB = T // AB
EPS = 1e-6
NEG = -1e30
ROPE_THETA = 10000.0
DILATIONS = ((128, 1), (512, 4), (2048, 16))

ADAM_LR = 0.001
ADAM_B1 = 0.9
ADAM_B2 = 0.999
ADAM_EPS = 1e-08
ADAM_WD = 0.01
ADAM_STEP = 10

VMEM_LIMIT = 58 * 1024 * 1024


def _params(sem, vmem=VMEM_LIMIT):
    return pltpu.CompilerParams(dimension_semantics=sem, vmem_limit_bytes=vmem)


def _dot(a, b):
    return jnp.dot(a, b, preferred_element_type=F32)


def _dot_nt(a, b):
    return lax.dot_general(a, b, (((1,), (1,)), ((), ())), preferred_element_type=F32)


def _dot_tn(a, b):
    return lax.dot_general(a, b, (((0,), (0,)), ((), ())), preferred_element_type=F32)


def _split3(x):
    hi = x.astype(BF16)
    r1 = x - hi.astype(F32)
    mid = r1.astype(BF16)
    lo = (r1 - mid.astype(F32)).astype(BF16)
    return hi, mid, lo


def _dot_hi(a, b, a_is_01=False):
    if a_is_01:
        sel = a.astype(BF16)
        return sum(_dot(sel, p) for p in _split3(b))
    sel = b.astype(BF16)
    return sum(_dot(p, sel) for p in _split3(a))


def _dot_nt_hi(a, b):
    sel = b.astype(BF16)
    return sum(_dot_nt(p, sel) for p in _split3(a))


def _rs(x):
    return lax.rsqrt(jnp.mean(x * x, axis=-1, keepdims=True) + EPS)


def _sigmoid(x):
    return jax.nn.sigmoid(x)


def _dsilu(x, s):
    return s * (1.0 + x * (1.0 - s))


def _resident(shape):
    nd = len(shape)
    return pl.BlockSpec(shape, lambda *_: (0,) * nd, pipeline_mode=pl.Buffered(1))


def _const(shape):
    nd = len(shape)
    return pl.BlockSpec(shape, lambda *_: (0,) * nd)


def _rows(tm, cols):
    return pl.BlockSpec((tm, cols), lambda i: (i, 0))


ANY = pl.BlockSpec(memory_space=pl.ANY)


def _place():
    return lax.axis_index("x"), lax.axis_index("y"), lax.axis_index("c")


def _gather_duty(arrays):
    n = len(arrays)
    results = [jax.ShapeDtypeStruct((N_DEV,) + a.shape, a.dtype) for a in arrays]

    def make(ins, outs, send_sems, recv_sems, local_sems):
        x, y, c = _place()
        me, sibling = (x, y, c), (x, y, 1 - c)
        chips = [(1 - x, y), (x, 1 - y), (1 - x, 1 - y)]

        def place_of(a, px, py, pc):
            return outs[a].at[4 * px + 2 * py + pc]

        def copy(a, k, block, to, src=None):
            dst = place_of(a, *block)
            return pltpu.make_async_remote_copy(src_ref=dst if src is None else src, dst_ref=dst,
                                                send_sem=send_sems.at[7 * a + k], recv_sem=recv_sems.at[7 * a + k],
                                                device_id=to, device_id_type=MESH)

        def own(a):
            return pltpu.make_async_copy(ins[a], place_of(a, *me), local_sems.at[a])

        def first(a):
            return [copy(a, 0, me, sibling, src=ins[a])] + [copy(a, 1 + j, me, (*chip, c), src=ins[a]) for j, chip in enumerate(chips)]

        def start():
            for a in range(n):
                own(a).start()
            for a in range(n):
                for cp in first(a):
                    cp.start()

        def finish():
            for j, chip in enumerate(chips):
                for a in range(n):
                    copy(a, 1 + j, (*chip, c), me).wait_recv()
                    copy(a, 4 + j, (*chip, c), sibling).start()
            for a in range(n):
                copy(a, 0, sibling, me).wait_recv()
                for j, chip in enumerate(chips):
                    copy(a, 4 + j, (*chip, 1 - c), me).wait_recv()
            for a in range(n):
                for cp in first(a) + [copy(a, 4 + j, (*chip, c), sibling) for j, chip in enumerate(chips)]:
                    cp.wait_send()
                own(a).wait()

        return start, finish

    return dict(operands=list(arrays), results=results, sems=(7 * n, 7 * n, n), make=make)


def _swap_duty(arrays):
    n = len(arrays)
    half = N_DEV // 2
    results = [jax.ShapeDtypeStruct(a.shape, a.dtype) for a in arrays]

    def make(ins, outs, send_sems, recv_sems):
        x, y, c = _place()

        def copies():
            return [pltpu.make_async_remote_copy(src_ref=ins[a].at[k], dst_ref=outs[a].at[k],
                                                 send_sem=send_sems.at[half * a + k], recv_sem=recv_sems.at[half * a + k],
                                                 device_id=(x, y, 1 - c), device_id_type=MESH)
                    for a in range(n) for k in range(half)]

        def start():
            for cp in copies():
                cp.start()

        def finish():
            for cp in copies():
                cp.wait()

        return start, finish

    return dict(operands=list(arrays), results=results, sems=(half * n, half * n), make=make)


def _exchange_duty(arrays):
    n = len(arrays)
    results = [jax.ShapeDtypeStruct(a.shape, a.dtype) for a in arrays]

    def make(ins, outs, send_sems, recv_sems):
        x, y, c = _place()
        chips = [(1 - x, y), (x, 1 - y), (1 - x, 1 - y)]
        my_chip = 2 * x + y

        def sends():
            return [pltpu.make_async_remote_copy(src_ref=ins[a].at[2 * px + py], dst_ref=outs[a].at[my_chip],
                                                 send_sem=send_sems.at[3 * a + j], recv_sem=recv_sems.at[3 * a + j],
                                                 device_id=(px, py, c), device_id_type=MESH)
                    for a in range(n) for j, (px, py) in enumerate(chips)]

        def start():
            for cp in sends():
                cp.start()

        def finish():
            for a in range(n):
                for j, (px, py) in enumerate(chips):
                    pltpu.make_async_remote_copy(src_ref=ins[a].at[my_chip], dst_ref=outs[a].at[2 * px + py],
                                                 send_sem=send_sems.at[3 * a + j], recv_sem=recv_sems.at[3 * a + j],
                                                 device_id=(px, py, c), device_id_type=MESH).wait_recv()
            for cp in sends():
                cp.wait_send()

        return start, finish

    return dict(operands=list(arrays), results=results, sems=(3 * n, 3 * n), make=make)


def _call(body, *, name, grid, in_specs, out_specs, out_shape, args, sem, scratch=(), duties=(), after=()):
    n_in, n_out, n_scr = len(in_specs), len(out_specs), len(scratch)
    sem_shapes = [pltpu.SemaphoreType.DMA((k,)) for d in duties for k in d["sems"]]

    def full(*refs):
        pos = [0]

        def take(k):
            pos[0] += k
            return refs[pos[0] - k:pos[0]]

        ins = take(n_in)
        d_ins = [take(len(d["operands"])) for d in duties]
        take(len(after))
        outs = take(n_out)
        d_outs = [take(len(d["results"])) for d in duties]
        scr = take(n_scr)
        d_sems = [take(len(d["sems"])) for d in duties]
        hooks = [d["make"](di, do, *ds) for d, di, do, ds in zip(duties, d_ins, d_outs, d_sems)]
        if grid and hooks:
            ids = [pl.program_id(k) for k in range(len(grid))]
            first = functools.reduce(jnp.logical_and, [i == 0 for i in ids])
            last = functools.reduce(jnp.logical_and, [i == g - 1 for i, g in zip(ids, grid)])

            @pl.when(first)
            def _():
                for start, _ in hooks:
                    start()

            body(*ins, *outs, *scr)

            @pl.when(last)
            def _():
                for _, finish in hooks:
                    finish()
        else:
            for start, _ in hooks:
                start()
            body(*ins, *outs, *scr)
            for _, finish in hooks:
                finish()

    d_args = [a for d in duties for a in d["operands"]]
    d_res = [r for d in duties for r in d["results"]]
    kwargs = dict(grid=grid) if grid else {}
    res = pl.pallas_call(
        full, name=name, in_specs=list(in_specs) + [ANY] * (len(d_args) + len(after)), out_specs=list(out_specs) + [ANY] * len(d_res),
        out_shape=list(out_shape) + d_res, scratch_shapes=list(scratch) + sem_shapes,
        compiler_params=_params(sem) if grid else None, **kwargs,
    )(*args, *d_args, *after)
    own, rest = list(res[:n_out]), list(res[n_out:])
    by_duty = []
    for d in duties:
        by_duty.append(rest[:len(d["results"])])
        rest = rest[len(d["results"]):]
    return own, by_duty


def _comm_only(duties, name, after=()):
    return _call(lambda: None, name=name, grid=None, in_specs=[], out_specs=[], out_shape=[], args=[], sem=None, duties=duties,
                 after=after)[1]


HBM = pl.BlockSpec(memory_space=pltpu.HBM)
SEMS = pl.BlockSpec(memory_space=pltpu.SEMAPHORE)
SIDE_EFFECT = pltpu.SideEffectType.DATAFLOW_SIDE_EFFECTING
N_OTHER_CHIPS = 3


def _chip_copies(src_ref, land_ref, sems):
    x, y, c = _place()
    chips = [(1 - x, y), (x, 1 - y), (1 - x, 1 - y)]
    return [pltpu.make_async_remote_copy(src_ref=src_ref.at[2 * px + py], dst_ref=land_ref.at[2 * x + y],
                                         send_sem=sems[j], recv_sem=sems[N_OTHER_CHIPS + j], device_id=(px, py, c), device_id_type=MESH)
            for j, (px, py) in enumerate(chips)]


def _exchange_start(pbs, name):
    k = len(pbs)
    n_sem = 2 * N_OTHER_CHIPS * k

    def body(*refs):
        srcs, lands, sems = refs[:k], refs[k:2 * k], refs[2 * k:2 * k + n_sem]
        for a in range(k):
            for cp in _chip_copies(srcs[a], lands[a], sems[2 * N_OTHER_CHIPS * a:2 * N_OTHER_CHIPS * (a + 1)]):
                cp.start()
        token = refs[2 * k + n_sem + 2 * k]
        token[...] = jnp.zeros_like(token)

    thru = tuple(pltpu.HBM(p.shape, p.dtype) for p in pbs)
    res = pl.pallas_call(
        body, name=name,
        out_shape=(pltpu.SemaphoreType.DMA(()),) * n_sem + thru + thru + (jax.ShapeDtypeStruct((8, 128), F32),),
        in_specs=(HBM,) * (2 * k), out_specs=(SEMS,) * n_sem + (HBM,) * (2 * k) + (pl.BlockSpec(memory_space=pltpu.VMEM),),
        input_output_aliases={i: n_sem + i for i in range(2 * k)},
        compiler_params=pltpu.CompilerParams(has_side_effects=SIDE_EFFECT),
    )(*[pltpu.with_memory_space_constraint(p, pltpu.HBM) for p in pbs],
      *[pltpu.with_memory_space_constraint(lax.empty(p.shape, p.dtype), pltpu.HBM) for p in pbs])
    return dict(sems=res[:n_sem], srcs=res[n_sem:n_sem + k], lands=res[n_sem + k:n_sem + 2 * k], token=res[n_sem + 2 * k])


def _exchange_wait(started, after, name):
    srcs, lands = started["srcs"], started["lands"]
    k = len(srcs)
    n_sem = 2 * N_OTHER_CHIPS * k

    def body(*refs):
        sems = refs[2 * k:2 * k + n_sem]
        for a in range(k):
            for cp in _chip_copies(refs[a], refs[k + a], sems[2 * N_OTHER_CHIPS * a:2 * N_OTHER_CHIPS * (a + 1)]):
                cp.wait_send()
                cp.wait_recv()

    return pl.pallas_call(
        body, name=name, out_shape=tuple(pltpu.HBM(p.shape, p.dtype) for p in list(srcs) + list(lands)),
        in_specs=(HBM,) * (2 * k) + (SEMS,) * n_sem + (ANY,) * len(after), out_specs=(HBM,) * (2 * k),
        input_output_aliases={i: i for i in range(2 * k)},
        compiler_params=pltpu.CompilerParams(has_side_effects=SIDE_EFFECT),
    )(*srcs, *lands, *started["sems"], *after)[k:]


def _ffn_fwd(x, gpre, wg, wu, wd, gpost, name, duties=(), target=None):
    tm = 256
    n_in = 6 if target is None else 7

    def body(*refs):
        x_ref, gpre_ref, wg_ref, wu_ref, wd_ref, gpost_ref = refs[:6]
        xo_ref, n_ref, a_ref, b_ref, hm_ref, h_ref = refs[n_in:n_in + 6]
        xv = x_ref[...]
        n = (xv * _rs(xv) * gpre_ref[...]).astype(BF16)
        a = _dot_nt(n, wg_ref[...])
        b = _dot_nt(n, wu_ref[...])
        hm = (a * _sigmoid(a) * b).astype(BF16)
        h = _dot(hm, wd_ref[...])
        xo = xv + 0.5 * (h * _rs(h) * gpost_ref[...])
        n_ref[...] = n
        a_ref[...] = a.astype(BF16)
        b_ref[...] = b.astype(BF16)
        hm_ref[...] = hm
        h_ref[...] = h
        if target is None:
            xo_ref[...] = xo
        else:
            ss_ref = refs[n_in + 6]

            @pl.when(pl.program_id(0) == 0)
            def _():
                ss_ref[...] = jnp.zeros_like(ss_ref)

            err = xo - refs[6][...]
            xo_ref[...] = err * (1.0 / D)
            ss_ref[...] += jnp.sum(jnp.sum(err * err, axis=1, keepdims=True), axis=0, keepdims=True)

    loss_in = [] if target is None else [_rows(tm, D)]
    loss_out = [] if target is None else [_const((1, 128))]
    loss_shape = [] if target is None else [jax.ShapeDtypeStruct((1, 128), F32)]
    return _call(
        body, name=name, grid=(T // tm,),
        in_specs=[_rows(tm, D), _const((1, D)), _resident((FFP, D)), _resident((FFP, D)), _resident((FFP, D)), _const((1, D))] + loss_in,
        out_specs=[_rows(tm, D), _rows(tm, D), _rows(tm, FFP), _rows(tm, FFP), _rows(tm, FFP), _rows(tm, D)] + loss_out,
        out_shape=[jax.ShapeDtypeStruct((T, D), F32), jax.ShapeDtypeStruct((T, D), BF16), jax.ShapeDtypeStruct((T, FFP), BF16),
                   jax.ShapeDtypeStruct((T, FFP), BF16), jax.ShapeDtypeStruct((T, FFP), BF16), jax.ShapeDtypeStruct((T, D), F32)]
        + loss_shape,
        args=[x, gpre, wg, wu, wd, gpost] + ([] if target is None else [target]), sem=("arbitrary",), duties=duties)


def _ffn_bwd(dxo, x, a, b, h, gpre, gpost, wg, wu, wd, name, after=()):
    tm = 256

    def body(dxo_ref, x_ref, a_ref, b_ref, h_ref, gpre_ref, gpost_ref, wg_ref, wu_ref, wd_ref,
             dx_ref, da_ref, db_ref, dh_ref, dgpre_ref, dgpost_ref):
        @pl.when(pl.program_id(0) == 0)
        def _():
            dgpre_ref[...] = jnp.zeros_like(dgpre_ref)
            dgpost_ref[...] = jnp.zeros_like(dgpost_ref)

        dy = dxo_ref[...]
        h = h_ref[...]
        hn = h * _rs(h)
        r2 = _rs(h)
        dgpost_ref[...] += jnp.sum(0.5 * dy * hn, axis=0, keepdims=True)
        gdy = 0.5 * dy * gpost_ref[...]
        dh = r2 * (gdy - hn * jnp.mean(gdy * hn, axis=-1, keepdims=True))
        dhb = dh.astype(BF16)
        dh_ref[...] = dhb
        dhm = _dot_nt(dhb, wd_ref[...])
        av = a_ref[...].astype(F32)
        bv = b_ref[...].astype(F32)
        sg = _sigmoid(av)
        db = (dhm * (av * sg)).astype(BF16)
        da = (dhm * bv * _dsilu(av, sg)).astype(BF16)
        da_ref[...] = da
        db_ref[...] = db
        dn = _dot(da, wg_ref[...]) + _dot(db, wu_ref[...])
        xv = x_ref[...]
        r = _rs(xv)
        xn = xv * r
        dgpre_ref[...] += jnp.sum(dn * xn, axis=0, keepdims=True)
        gdn = dn * gpre_ref[...]
        dx_ref[...] = dy + r * (gdn - xn * jnp.mean(gdn * xn, axis=-1, keepdims=True))

    return _call(
        body, name=name, grid=(T // tm,),
        in_specs=[_rows(tm, D), _rows(tm, D), _rows(tm, FFP), _rows(tm, FFP), _rows(tm, D), _const((1, D)), _const((1, D)),
                  _resident((FFP, D)), _resident((FFP, D)), _resident((FFP, D))],
        out_specs=[_rows(tm, D), _rows(tm, FFP), _rows(tm, FFP), _rows(tm, D), _const((1, D)), _const((1, D))],
        out_shape=[jax.ShapeDtypeStruct((T, D), F32), jax.ShapeDtypeStruct((T, FFP), BF16), jax.ShapeDtypeStruct((T, FFP), BF16),
                   jax.ShapeDtypeStruct((T, D), BF16), jax.ShapeDtypeStruct((1, D), F32), jax.ShapeDtypeStruct((1, D), F32)],
        args=[dxo, x, a, b, h, gpre, gpost, wg, wu, wd], sem=("arbitrary",), after=after)


def _core_index(core):
    return lax.axis_index("c") if core == "mesh" else core


def _by_core(res, o_ref, ob_ref, core):
    r = res.shape[0] // 2
    c = jnp.asarray(_core_index(core))

    @pl.when(c == 0)
    def _():
        o_ref[0] = res[:r]
        ob_ref[0] = res[r:].astype(BF16)

    @pl.when(c == 1)
    def _():
        o_ref[0] = res[r:]
        ob_ref[0] = res[:r].astype(BF16)


def _matmul_tn(a, b, name, core, after=(), duties=()):
    k, m = a.shape
    n = b.shape[1]
    r = m // N_DEV
    assert m == N_DEV * r and r % 128 == 0

    def body(a_ref, b_ref, o_ref, ob_ref):
        _by_core(_dot_tn(a_ref[...], b_ref[...]), o_ref, ob_ref, core)

    spec = pl.BlockSpec((1, r, n), lambda i: (i, 0, 0))
    return _call(body, name=name, grid=(N_DEV // 2,), in_specs=[pl.BlockSpec((k, 2 * r), lambda i: (0, i)), _resident((k, n))],
                 out_specs=[spec, spec],
                 out_shape=[jax.ShapeDtypeStruct((N_DEV // 2, r, n), F32), jax.ShapeDtypeStruct((N_DEV // 2, r, n), BF16)],
                 args=[a, b], sem=("arbitrary",), duties=duties, after=after)


def _dwout(attn, yn, dh2, core):
    rs = (QW + SSM_W) // N_DEV
    chips = N_DEV // 2

    def body(at_ref, yn_ref, dh_ref, o_ref, ob_ref):
        i = pl.program_id(0)

        @pl.when(i < chips // 2)
        def _():
            _by_core(_dot_tn(at_ref[...], dh_ref[...]), o_ref, ob_ref, core)

        @pl.when(i >= chips // 2)
        def _():
            _by_core(_dot_tn(yn_ref[...], dh_ref[...]), o_ref, ob_ref, core)

    spec = pl.BlockSpec((1, rs, D), lambda i: (i, 0, 0))
    return pl.pallas_call(
        body, name="dwout", grid=(chips,),
        in_specs=[pl.BlockSpec((T, 2 * rs), lambda i: (0, jnp.minimum(i, chips // 2 - 1))),
                  pl.BlockSpec((T, 2 * rs), lambda i: (0, jnp.maximum(i - chips // 2, 0))), _resident((T, D))],
        out_specs=[spec, spec],
        out_shape=[jax.ShapeDtypeStruct((chips, rs, D), F32), jax.ShapeDtypeStruct((chips, rs, D), BF16)],
        compiler_params=_params(("arbitrary",)),
    )(attn, yn, dh2)


def _rope_swap(t, lo_half):
    return jnp.where(lo_half, pltpu.roll(t, 96, 1), pltpu.roll(t, 32, 1))


def _inproj_fwd(x1, gpre, win, cos, sin_s, ex, duties=()):
    tm = 256

    def body(x_ref, g_ref, w_ref, cos_ref, sin_ref, ex_ref, n_ref, q_ref, kx_ref, vx_ref, xbc_ref, z_ref, dt_ref):
        xv = x_ref[...]
        n = (xv * _rs(xv) * g_ref[...]).astype(BF16)
        n_ref[...] = n
        by_dev = _dot_nt(n, w_ref[...])
        proj = jnp.concatenate([by_dev[:, ISW * d:ISW * d + ISR] for d in range(N_DEV)], axis=1)
        cs = cos_ref[...]
        sn = sin_ref[...]
        lo_half = (lax.broadcasted_iota(jnp.int32, (1, 128), 1) % HD) < (HD // 2)

        def rope(t):
            return t * cs + _rope_swap(t, lo_half) * sn

        for j in range(QW // 128):
            t = proj[:, 128 * j:128 * j + 128]
            q_ref[:, 128 * j:128 * j + 128] = (rope(t) * (HD ** -0.5)).astype(BF16)
        k = jnp.concatenate([rope(proj[:, QW + 128 * j:QW + 128 * j + 128]) for j in range(KVW // 128)], axis=1)
        v = proj[:, QW + KVW:QW + 2 * KVW]
        kx_ref[...] = _dot(k.astype(BF16), ex_ref[...]).astype(BF16)
        vx_ref[...] = _dot(v.astype(BF16), ex_ref[...]).astype(BF16)
        c0 = QW + 2 * KVW
        xbc_ref[...] = proj[:, c0:c0 + CONV_C]
        z_ref[...] = proj[:, c0 + CONV_C:c0 + CONV_C + SSM_W]
        dt_ref[...] = proj[:, c0 + CONV_C + SSM_W:IN_COLS]

    return _call(
        body, name="inproj_fwd", grid=(T // tm,),
        in_specs=[_rows(tm, D), _const((1, D)), _resident((INP, D)), _rows(tm, 128), _rows(tm, 128), _const((KVW, QW))],
        out_specs=[_rows(tm, D), _rows(tm, QW), _rows(tm, QW), _rows(tm, QW), _rows(tm, CONV_C), _rows(tm, SSM_W), _rows(tm, SSM_H)],
        out_shape=[jax.ShapeDtypeStruct((T, D), BF16), jax.ShapeDtypeStruct((T, QW), BF16), jax.ShapeDtypeStruct((T, QW), BF16),
                   jax.ShapeDtypeStruct((T, QW), BF16), jax.ShapeDtypeStruct((T, CONV_C), F32), jax.ShapeDtypeStruct((T, SSM_W), F32),
                   jax.ShapeDtypeStruct((T, SSM_H), F32)],
        args=[x1, gpre, win, cos, sin_s, ex], sem=("arbitrary",), duties=duties)


def _inproj_bwd(dres, dq, dkx, dvx, dxbc, dz, ddt, x1, gpre, win, cos, sin_s, exf):
    tm = 256

    def body(dres_ref, dq_ref, dkx_ref, dvx_ref, dxbc_ref, dz_ref, ddt_ref, x_ref, g_ref, w_ref, cos_ref, sin_ref, ex_ref,
             dx_ref, dps_ref, dg_ref, dp_ref):
        @pl.when(pl.program_id(0) == 0)
        def _():
            dg_ref[...] = jnp.zeros_like(dg_ref)

        cs = cos_ref[...]
        sn = sin_ref[...]
        lo_half = (lax.broadcasted_iota(jnp.int32, (1, 128), 1) % HD) < (HD // 2)

        def rope_t(t):
            return t * cs - _rope_swap(t, lo_half) * sn

        for j in range(QW // 128):
            dp_ref[:, 128 * j:128 * j + 128] = rope_t(dq_ref[:, 128 * j:128 * j + 128] * (HD ** -0.5)).astype(BF16)
        dk = _dot_nt_hi(dkx_ref[...], ex_ref[...])
        dv = _dot_nt_hi(dvx_ref[...], ex_ref[...])
        for j in range(KVW // 128):
            dp_ref[:, QW + 128 * j:QW + 128 * j + 128] = rope_t(dk[:, 128 * j:128 * j + 128]).astype(BF16)
        dp_ref[:, QW + KVW:QW + 2 * KVW] = dv.astype(BF16)
        c0 = QW + 2 * KVW
        dp_ref[:, c0:c0 + CONV_C] = dxbc_ref[...].astype(BF16)
        dp_ref[:, c0 + CONV_C:c0 + CONV_C + SSM_W] = dz_ref[...].astype(BF16)
        dp_ref[:, c0 + CONV_C + SSM_W:INP] = ddt_ref[...].astype(BF16)
        pieces = [dp_ref[:, ISR * d:ISR * (d + 1)] for d in range(N_DEV)]
        zw = jnp.zeros((tm, ISW - ISR), BF16)
        zg = jnp.zeros((tm, ISG - ISR), BF16)
        dn = _dot(jnp.concatenate([t for p in pieces for t in (p, zw)], axis=1), w_ref[...])
        for d in range(N_DEV):
            dps_ref[:, ISG * d:ISG * (d + 1)] = jnp.concatenate([pieces[d], zg], axis=1)
        xv = x_ref[...]
        r = _rs(xv)
        xn = xv * r
        dg_ref[...] += jnp.sum(dn * xn, axis=0, keepdims=True)
        gdn = dn * g_ref[...]
        dx_ref[...] = dres_ref[...] + r * (gdn - xn * jnp.mean(gdn * xn, axis=-1, keepdims=True))

    return pl.pallas_call(
        body, name="inproj_bwd", grid=(T // tm,),
        in_specs=[_rows(tm, D), _rows(tm, QW), _rows(tm, QW), _rows(tm, QW), _rows(tm, CONV_C), _rows(tm, SSM_W), _rows(tm, 128),
                  _rows(tm, D), _const((1, D)), _resident((INP, D)), _rows(tm, 128), _rows(tm, 128), _const((KVW, QW))],
        out_specs=[_rows(tm, D), _rows(tm, N_DEV * ISG), _const((1, D))],
        out_shape=[jax.ShapeDtypeStruct((T, D), F32), jax.ShapeDtypeStruct((T, N_DEV * ISG), BF16), jax.ShapeDtypeStruct((1, D), F32)],
        scratch_shapes=[pltpu.VMEM((tm, INP), BF16)],
        compiler_params=_params(("arbitrary",)),
    )(dres, dq, dkx, dvx, dxbc, dz, ddt, x1, gpre, win, cos, sin_s, exf)


def _outproj_fwd(x1, attn, yn, wout, gpost):
    tm = 512

    def body(x_ref, at_ref, yn_ref, w_ref, g_ref, xo_ref, h_ref):
        h = _dot(at_ref[...], w_ref[0:QW, :]) + _dot(yn_ref[...], w_ref[QW:QW + SSM_W, :])
        h_ref[...] = h
        xo_ref[...] = x_ref[...] + h * _rs(h) * g_ref[...]

    return pl.pallas_call(
        body, name="outproj_fwd", grid=(T // tm,),
        in_specs=[_rows(tm, D), _rows(tm, QW), _rows(tm, SSM_W), _resident((QW + SSM_W, D)), _const((1, D))],
        out_specs=[_rows(tm, D), _rows(tm, D)],
        out_shape=[jax.ShapeDtypeStruct((T, D), F32), jax.ShapeDtypeStruct((T, D), F32)],
        compiler_params=_params(("parallel",)),
    )(x1, attn, yn, wout, gpost)


def _outproj_bwd(dx2, h2, gpost, wout, duties=()):
    tm = 512

    def body(dy_ref, h_ref, g_ref, w_ref, dh_ref, dm_ref, dg_ref):
        @pl.when(pl.program_id(0) == 0)
        def _():
            dg_ref[...] = jnp.zeros_like(dg_ref)

        dy = dy_ref[...]
        h = h_ref[...]
        r = _rs(h)
        hn = h * r
        dg_ref[...] += jnp.sum(dy * hn, axis=0, keepdims=True)
        gdy = dy * g_ref[...]
        dh = (r * (gdy - hn * jnp.mean(gdy * hn, axis=-1, keepdims=True))).astype(BF16)
        dh_ref[...] = dh
        dm_ref[...] = _dot_nt(dh, w_ref[...])

    return _call(
        body, name="outproj_bwd", grid=(T // tm,),
        in_specs=[_rows(tm, D), _rows(tm, D), _const((1, D)), _resident((QW + SSM_W, D))],
        out_specs=[_rows(tm, D), _rows(tm, QW + SSM_W), _const((1, D))],
        out_shape=[jax.ShapeDtypeStruct((T, D), BF16), jax.ShapeDtypeStruct((T, QW + SSM_W), F32), jax.ShapeDtypeStruct((1, D), F32)],
        args=[dx2, h2, gpost, wout], sem=("arbitrary",), duties=duties)


def _attn_bias():
    d = jnp.arange(AB)[:, None] - jnp.arange(T)[None, :] + (T - AB)
    cnt = jnp.zeros(d.shape, F32)
    for window, dil in DILATIONS:
        cnt = cnt + ((d >= 0) & (d % dil == 0) & (d <= window)).astype(F32)
    return jnp.where(cnt > 0, jnp.log(jnp.maximum(cnt, 1.0)), NEG)


G_PER = NQ // NKV
WK = G_PER * HD


def _attn_fwd(q, kx, vx, bias, duties=()):
    def body(q_ref, kx_ref, vx_ref, bias_ref, o_ref, lse_ref):
        lane = lax.broadcasted_iota(jnp.int32, (1, WK), 1)
        lse_ref[...] = jnp.zeros_like(lse_ref)
        for i in range(NAB):
            n = (i + 1) * AB
            rows = slice(i * AB, n)
            qi = q_ref[rows, :]
            kxi = kx_ref[0:n, :]
            vxi = vx_ref[0:n, :]
            bb = bias_ref[:, (NAB - 1 - i) * AB:]
            o_acc = jnp.zeros((AB, WK), F32)
            for g in range(G_PER):
                mg = (lane // HD) == g
                s = _dot_nt(jnp.where(mg, qi, jnp.zeros_like(qi)), kxi) + bb
                m = jnp.max(s, axis=1, keepdims=True)
                p = jnp.exp(s - m)
                l = jnp.sum(p, axis=1, keepdims=True)
                o_acc = jnp.where(mg, _dot(p.astype(BF16), vxi) / l, o_acc)
                lse_ref[rows, g:g + 1] = m + jnp.log(l)
            o_ref[rows, :] = o_acc.astype(BF16)

    col = lambda kv: (0, kv)
    return _call(
        body, name="attn_fwd", grid=(NKV,),
        in_specs=[pl.BlockSpec((T, WK), col), pl.BlockSpec((T, WK), col), pl.BlockSpec((T, WK), col), _const((AB, T))],
        out_specs=[pl.BlockSpec((T, WK), col), pl.BlockSpec((T, 128), col)],
        out_shape=[jax.ShapeDtypeStruct((T, QW), BF16), jax.ShapeDtypeStruct((T, NKV * 128), F32)],
        args=[q, kx, vx, bias], sem=("arbitrary",), duties=duties)


def _attn_bwd(q, kx, vx, o, dmix, lse, bias, duties=(), after=()):
    def body(q_ref, kx_ref, vx_ref, o_ref, do_ref, lse_ref, bias_ref, dq_ref, dkx_ref, dvx_ref):
        lane = lax.broadcasted_iota(jnp.int32, (1, WK), 1)
        dkx_ref[...] = jnp.zeros_like(dkx_ref)
        dvx_ref[...] = jnp.zeros_like(dvx_ref)
        for i in range(NAB):
            n = (i + 1) * AB
            rows = slice(i * AB, n)
            qi = q_ref[rows, :]
            dof = do_ref[rows, :]
            doi = dof.astype(BF16)
            prod = dof * o_ref[rows, :].astype(F32)
            kxi = kx_ref[0:n, :]
            vxi = vx_ref[0:n, :]
            bb = bias_ref[:, (NAB - 1 - i) * AB:]
            dq_acc = jnp.zeros((AB, WK), F32)
            for g in range(G_PER):
                mg = (lane // HD) == g
                qm = jnp.where(mg, qi, jnp.zeros_like(qi))
                dom = jnp.where(mg, doi, jnp.zeros_like(doi))
                delta = jnp.sum(jnp.where(mg, prod, 0.0), axis=1, keepdims=True)
                p = jnp.exp(_dot_nt(qm, kxi) + bb - lse_ref[rows, g:g + 1])
                ds = (p * (_dot_nt(dom, vxi) - delta)).astype(BF16)
                dvx_ref[0:n, :] += _dot_tn(p.astype(BF16), dom)
                dkx_ref[0:n, :] += _dot_tn(ds, qm)
                dq_acc = jnp.where(mg, _dot(ds, kxi), dq_acc)
            dq_ref[rows, :] = dq_acc

    col = lambda kv: (0, kv)
    return _call(
        body, name="attn_bwd", grid=(NKV,),
        in_specs=[pl.BlockSpec((T, WK), col), pl.BlockSpec((T, WK), col), pl.BlockSpec((T, WK), col), pl.BlockSpec((T, WK), col),
                  pl.BlockSpec((T, WK), col), pl.BlockSpec((T, 128), col), _const((AB, T))],
        out_specs=[pl.BlockSpec((T, WK), col), pl.BlockSpec((T, WK), col), pl.BlockSpec((T, WK), col)],
        out_shape=[jax.ShapeDtypeStruct((T, QW), F32)] * 3,
        args=[q, kx, vx, o, dmix, lse, bias], sem=("arbitrary",), duties=duties, after=after)


def _softplus(x):
    return jnp.maximum(x, 0.0) + jnp.log1p(jnp.exp(-jnp.abs(x)))


def _causal_conv(u, zs, cw_ref, cb_ref):
    zs[8:, :] = u
    sh1, sh2, sh3 = (zs[8 - m:8 - m + L, :] for m in (1, 2, 3))
    return cb_ref[...] + cw_ref[3:4, :] * u + cw_ref[2:3, :] * sh1 + cw_ref[1:2, :] * sh2 + cw_ref[0:1, :] * sh3


def _ssd_chunk_common(xc, dtr, dtb_ref, alx_ref, e_ref):
    sg = _sigmoid(xc)
    act = xc * sg
    pre = dtr + dtb_ref[...]
    dt_x = _dot_hi(_softplus(pre), e_ref[...])
    a_x = -jnp.exp(alx_ref[...])
    ri = lax.broadcasted_iota(jnp.int32, (L, L), 0)
    ci = lax.broadcasted_iota(jnp.int32, (L, L), 1)
    tri = ri >= ci
    acs_x = _dot_hi(tri, dt_x * a_x, a_is_01=True)
    return dict(sg=sg, act=act, pre=pre, dt_x=dt_x, a_x=a_x, tri=tri, acs_x=acs_x)


def _decay(acs_x, acs_t, h, tri):
    col = acs_x[:, HD * h:HD * h + 1]
    row = acs_t[HD * h:HD * h + 1, :]
    return jnp.exp(jnp.where(tri, col - row, NEG))


def _ssd_fwd(xbc, z, dtr, convw, convb, dtb, alx, dskx, ssmn, e, duties=()):
    def body(u_ref, z_ref, dtr_ref, cw_ref, cb_ref, dtb_ref, alx_ref, dsk_ref, sn_ref, e_ref,
             yn_ref, y_ref, hs_ref, xc_ref, zs, hst):
        @pl.when(pl.program_id(0) == 0)
        def _():
            zs[0:8, :] = jnp.zeros((8, CONV_C), F32)
            hst[...] = jnp.zeros_like(hst)

        u = u_ref[...]
        xc = _causal_conv(u, zs, cw_ref, cb_ref)
        xc_ref[...] = xc
        zs[0:8, :] = u[L - 8:, :]
        cm = _ssd_chunk_common(xc, dtr_ref[...], dtb_ref, alx_ref, e_ref)
        act, dt_x, acs_x, tri = cm["act"], cm["dt_x"], cm["acs_x"], cm["tri"]
        xs = act[:, :SSM_W]
        acs_l = acs_x[L - 1:L, :]
        lam_x = jnp.exp(acs_x)
        w_x = jnp.exp(acs_l - acs_x)
        gam_x = jnp.exp(acs_l)
        acs_t = acs_x.T
        xd = xs * dt_x
        xb = xd.astype(BF16)
        xw = (xd * w_x).astype(BF16)
        lo = lax.broadcasted_iota(jnp.int32, (1, 128), 1) < HD
        hs_ref[0] = hst[...]
        pieces = []
        for grp in range(2):
            bb = act[:, SSM_W + SSM_N * grp:SSM_W + SSM_N * (grp + 1)].astype(BF16)
            cb_ = act[:, SSM_W + 2 * SSM_N + SSM_N * grp:SSM_W + 2 * SSM_N + SSM_N * (grp + 1)].astype(BF16)
            cbm = _dot_nt(cb_, bb)
            for jj in range(4):
                j = 4 * grp + jj
                sl = slice(128 * j, 128 * j + 128)
                m0 = (cbm * _decay(acs_x, acs_t, 2 * j, tri)).astype(BF16)
                m1 = (cbm * _decay(acs_x, acs_t, 2 * j + 1, tri)).astype(BF16)
                x2 = xb[:, sl]
                ydiag = jnp.where(lo, _dot(m0, x2), _dot(m1, x2))
                hprev = hst[j]
                yoff = lam_x[:, sl] * _dot(cb_, hprev.astype(BF16))
                pieces.append(ydiag + yoff)
                hst[j] = gam_x[:, sl] * hprev + _dot_tn(bb, xw[:, sl])
        y = jnp.concatenate(pieces, axis=1) + dsk_ref[...] * xs
        y_ref[...] = y
        zv = z_ref[...]
        yz = y * (zv * _sigmoid(zv))
        half = SSM_W // 2
        yn = jnp.concatenate([yz[:, :half] * _rs(yz[:, :half]), yz[:, half:] * _rs(yz[:, half:])], axis=1)
        yn_ref[...] = (yn * sn_ref[...]).astype(BF16)

    return _call(
        body, name="ssd_fwd", grid=(NCH,),
        in_specs=[_rows(L, CONV_C), _rows(L, SSM_W), _rows(L, SSM_H), _const((4, CONV_C)), _const((1, CONV_C)), _const((1, SSM_H)),
                  _const((1, SSM_W)), _const((1, SSM_W)), _const((1, SSM_W)), _const((SSM_H, SSM_W))],
        out_specs=[_rows(L, SSM_W), _rows(L, SSM_W), pl.BlockSpec((1, 8, SSM_N, 128), lambda c: (c, 0, 0, 0)), _rows(L, CONV_C)],
        out_shape=[jax.ShapeDtypeStruct((T, SSM_W), BF16), jax.ShapeDtypeStruct((T, SSM_W), F32),
                   jax.ShapeDtypeStruct((NCH, 8, SSM_N, 128), F32), jax.ShapeDtypeStruct((T, CONV_C), F32)],
        scratch=[pltpu.VMEM((8 + L, CONV_C), F32), pltpu.VMEM((8, SSM_N, 128), F32)],
        args=[xbc, z, dtr, convw, convb, dtb, alx, dskx, ssmn, e], sem=("arbitrary",), duties=duties)


def _ssd_bwd(dmix, xbc, xconv, z, dtr, y, hs, convw, dtb, alx, dskx, ssmn, e, e1, duties=()):
    rev = lambda i: (NCH - 1 - i, 0)

    def body(dyn_ref, u_ref, xc_ref, z_ref, dtr_ref, y_ref, hs_ref, cw_ref, dtb_ref, alx_ref, dsk_ref, sn_ref, e_ref, e1_ref,
             dxbc_ref, dz_ref, ddt_ref, dcw_ref, dcb_ref, dsn_ref, dpar_ref, dh, zd, colbuf):
        step = pl.program_id(0)

        @pl.when(step == 0)
        def _():
            for r in (dh, dcw_ref, dcb_ref, dsn_ref, dpar_ref):
                r[...] = jnp.zeros_like(r)
            zd[L:, :] = jnp.zeros((8, CONV_C), F32)

        u = u_ref[...]
        xc = xc_ref[...]
        cm = _ssd_chunk_common(xc, dtr_ref[...], dtb_ref, alx_ref, e_ref)
        sg, act, pre, dt_x, a_x, tri, acs_x = (cm[k] for k in ("sg", "act", "pre", "dt_x", "a_x", "tri", "acs_x"))
        xs = act[:, :SSM_W]
        acs_l = acs_x[L - 1:L, :]
        lam_x = jnp.exp(acs_x)
        w_x = jnp.exp(acs_l - acs_x)
        gam_x = jnp.exp(acs_l)
        acs_t = acs_x.T
        xd = xs * dt_x
        xb = xd.astype(BF16)
        xdw = xd * w_x
        xw = xdw.astype(BF16)
        lo = lax.broadcasted_iota(jnp.int32, (1, 128), 1) < HD
        row8 = lax.broadcasted_iota(jnp.int32, (8, 1), 0)

        dyn = dyn_ref[...]
        yv = y_ref[...]
        zv = z_ref[...]
        sz = _sigmoid(zv)
        siluz = zv * sz
        yz = yv * siluz
        half = SSM_W // 2
        gy = dyn * sn_ref[...]
        dyz_parts, yzn_parts = [], []
        for hf in range(2):
            part = yz[:, hf * half:(hf + 1) * half]
            r = _rs(part)
            pn = part * r
            gp = gy[:, hf * half:(hf + 1) * half]
            dyz_parts.append(r * (gp - pn * jnp.mean(gp * pn, axis=-1, keepdims=True)))
            yzn_parts.append(pn)
        dyz = jnp.concatenate(dyz_parts, axis=1)
        dsn_ref[...] += jnp.sum(dyn * jnp.concatenate(yzn_parts, axis=1), axis=0, keepdims=True)
        dy = dyz * siluz
        dz_ref[...] = dyz * yv * _dsilu(zv, sz)

        colbuf[...] = jnp.zeros_like(colbuf)
        dx_pieces, dacs_pieces, dacsl_pieces, db_pieces, dc_pieces = [], [], [], [], []
        for grp in range(2):
            bb = act[:, SSM_W + SSM_N * grp:SSM_W + SSM_N * (grp + 1)].astype(BF16)
            cb_ = act[:, SSM_W + 2 * SSM_N + SSM_N * grp:SSM_W + 2 * SSM_N + SSM_N * (grp + 1)].astype(BF16)
            cbm = _dot_nt(cb_, bb)
            dcbm = jnp.zeros((L, L), F32)
            dc_g = jnp.zeros((L, SSM_N), F32)
            db_g = jnp.zeros((L, SSM_N), F32)
            for jj in range(4):
                j = 4 * grp + jj
                sl = slice(128 * j, 128 * j + 128)
                dy2 = dy[:, sl]
                dy2b = dy2.astype(BF16)
                d0 = _decay(acs_x, acs_t, 2 * j, tri)
                d1 = _decay(acs_x, acs_t, 2 * j + 1, tri)
                m0 = cbm * d0
                m1 = cbm * d1
                x2 = xb[:, sl]
                hprev = hs_ref[0, j]
                hprevb = hprev.astype(BF16)
                dhn = dh[j]
                dhnb = dhn.astype(BF16)
                g2 = _dot(bb, dhnb)
                dx_pieces.append(jnp.where(lo, _dot_tn(m0.astype(BF16), dy2b), _dot_tn(m1.astype(BF16), dy2b)) + w_x[:, sl] * g2)
                zero = jnp.zeros_like(dy2b)
                dm0 = _dot_nt(jnp.where(lo, dy2b, zero), x2)
                dm1 = _dot_nt(jnp.where(lo, zero, dy2b), x2)
                dcbm = dcbm + dm0 * d0 + dm1 * d1
                e0 = dm0 * m0
                e1v = dm1 * m1
                colbuf[:, 2 * j:2 * j + 1] = jnp.sum(e0, axis=1, keepdims=True) - jnp.sum(e0.T, axis=1, keepdims=True)
                colbuf[:, 2 * j + 1:2 * j + 2] = jnp.sum(e1v, axis=1, keepdims=True) - jnp.sum(e1v.T, axis=1, keepdims=True)
                yoff = lam_x[:, sl] * _dot(cb_, hprevb)
                gxw = g2 * xdw[:, sl]
                dacs_pieces.append(dy2 * yoff - gxw)
                dacsl_pieces.append(jnp.sum(gxw, axis=0, keepdims=True) + gam_x[:, sl] * jnp.sum(dhn * hprev, axis=0, keepdims=True))
                dyl = (dy2 * lam_x[:, sl]).astype(BF16)
                dc_g = dc_g + _dot_nt(dyl, hprevb)
                db_g = db_g + _dot_nt(xw[:, sl], dhnb)
                dh[j] = gam_x[:, sl] * dhn + _dot_tn(cb_, dyl)
            dcbb = dcbm.astype(BF16)
            dc_pieces.append(dc_g + _dot(dcbb, bb))
            db_pieces.append(db_g + _dot_tn(dcbb, cb_))

        dxd = jnp.concatenate(dx_pieces, axis=1)
        rowi = lax.broadcasted_iota(jnp.int32, (L, 1), 0)
        dacs_x = (jnp.concatenate(dacs_pieces, axis=1) + _dot_hi(colbuf[...], e1_ref[...])
                  + jnp.where(rowi == L - 1, jnp.concatenate(dacsl_pieces, axis=1), 0.0))
        upper = lax.broadcasted_iota(jnp.int32, (L, L), 0) <= lax.broadcasted_iota(jnp.int32, (L, L), 1)
        dadt_x = _dot_hi(upper, dacs_x, a_is_01=True)
        ddt_x = dxd * xs + dadt_x * a_x
        ddtr = _dot_nt_hi(ddt_x, e_ref[...]) * _sigmoid(pre)
        ddt_ref[...] = jnp.zeros_like(ddt_ref)
        ddt_ref[:, 0:SSM_H] = ddtr
        dalx =jnp.sum(dadt_x * dt_x, axis=0, keepdims=True) * a_x
        ddskx = jnp.sum(dy * xs, axis=0, keepdims=True)
        par_x = jnp.where(row8 == 1, dalx, 0.0) + jnp.where(row8 == 2, ddskx, 0.0)
        dpar_ref[...] += _dot_nt_hi(par_x, e_ref[...]) + jnp.where(row8 == 0, jnp.sum(ddtr, axis=0, keepdims=True), 0.0)

        dxs = dxd * dt_x + dsk_ref[...] * dy
        dact = jnp.concatenate([dxs] + db_pieces + dc_pieces, axis=1)
        du = dact * _dsilu(xc, sg)
        dcb_ref[...] += jnp.sum(du, axis=0, keepdims=True)
        zd[0:L, :] = du
        f1, f2, f3 = (zd[m:m + L, :] for m in (1, 2, 3))
        dxbc_ref[...] = cw_ref[3:4, :] * du + cw_ref[2:3, :] * f1 + cw_ref[1:2, :] * f2 + cw_ref[0:1, :] * f3
        dcw = jnp.zeros((8, CONV_C), F32)
        for k, shifted in enumerate((f3, f2, f1, du)):
            dcw = dcw + jnp.where(row8 == k, jnp.sum(shifted * u, axis=0, keepdims=True), 0.0)
        dcw_ref[...] += dcw
        zd[L:, :] = du[:8, :]

    return _call(
        body, name="ssd_bwd", grid=(NCH,),
        in_specs=[pl.BlockSpec((L, SSM_W), lambda i: (NCH - 1 - i, 1)), pl.BlockSpec((L, CONV_C), rev), pl.BlockSpec((L, CONV_C), rev),
                  pl.BlockSpec((L, SSM_W), rev), pl.BlockSpec((L, SSM_H), rev), pl.BlockSpec((L, SSM_W), rev),
                  pl.BlockSpec((1, 8, SSM_N, 128), lambda i: (NCH - 1 - i, 0, 0, 0)),
                  _const((4, CONV_C)), _const((1, SSM_H)), _const((1, SSM_W)), _const((1, SSM_W)), _const((1, SSM_W)),
                  _const((SSM_H, SSM_W)), _const((128, SSM_W))],
        out_specs=[pl.BlockSpec((L, CONV_C), rev), pl.BlockSpec((L, SSM_W), rev), pl.BlockSpec((L, 128), rev),
                   _const((8, CONV_C)), _const((1, CONV_C)), _const((1, SSM_W)), _const((8, SSM_H))],
        out_shape=[jax.ShapeDtypeStruct((T, CONV_C), F32), jax.ShapeDtypeStruct((T, SSM_W), F32), jax.ShapeDtypeStruct((T, 128), F32),
                   jax.ShapeDtypeStruct((8, CONV_C), F32), jax.ShapeDtypeStruct((1, CONV_C), F32), jax.ShapeDtypeStruct((1, SSM_W), F32),
                   jax.ShapeDtypeStruct((8, SSM_H), F32)],
        scratch=[pltpu.VMEM((8, SSM_N, 128), F32), pltpu.VMEM((L + 8, CONV_C), F32), pltpu.VMEM((L, 128), F32)],
        args=[dmix, xbc, xconv, z, dtr, y, hs, convw, dtb, alx, dskx, ssmn, e, e1], sem=("arbitrary",), duties=duties)


def _adam_math(w, g, m, v):
    m = ADAM_B1 * m + (1.0 - ADAM_B1) * g
    v = ADAM_B2 * v + (1.0 - ADAM_B2) * (g * g)
    m_hat = m / (1.0 - ADAM_B1 ** ADAM_STEP)
    v_hat = v / (1.0 - ADAM_B2 ** ADAM_STEP)
    delta = -ADAM_LR * (m_hat / (jnp.sqrt(v_hat) + ADAM_EPS) + ADAM_WD * w)
    return delta, m, v


def _adamw(w, m, v, parts, name, after=()):
    rows, cols = w.shape
    tr = rows if rows <= 512 else 256
    assert rows % tr == 0
    n_parts = len(parts)

    def body(*refs):
        w_ref, m_ref, v_ref = refs[:3]
        p_refs = refs[3:3 + n_parts]
        g_ref, d_ref, nm_ref, nv_ref = refs[3 + n_parts + len(after):]
        g = p_refs[0][...].astype(F32)
        for p in p_refs[1:]:
            g = g + p[...].astype(F32)
        delta, nm, nv = _adam_math(w_ref[...], g, m_ref[...], v_ref[...])
        g_ref[...] = g
        d_ref[...] = delta
        nm_ref[...] = nm
        nv_ref[...] = nv

    spec = pl.BlockSpec((tr, cols), lambda i: (i, 0))
    return pl.pallas_call(
        body, name=name, grid=(rows // tr,),
        in_specs=[spec] * (3 + n_parts) + [ANY] * len(after), out_specs=[spec] * 4,
        out_shape=[jax.ShapeDtypeStruct((rows, cols), F32)] * 4,
        compiler_params=_params(("parallel",)),
    )(w, m, v, *parts, *after)


COL_TILE = 512


def _adamw_sharded(ws, ms, vs, chip_sums, from_chips, other_chips, name, after=()):
    k = len(ws)
    rows, cols = ws[0].shape
    prow = chip_sums[0].shape[0]
    assert cols % COL_TILE == 0 and prow >= rows and all(a.shape == ws[0].shape for a in ws)

    def body(ids_ref, *refs):
        ins, outs = refs[:7 * k], refs[7 * k + len(after):]
        for a in range(k):
            w_ref, m_ref, v_ref, s_ref, r1_ref, r2_ref, r3_ref = ins[7 * a:7 * a + 7]
            g = s_ref[...]
            for r in (r1_ref, r2_ref, r3_ref):
                g = g + r[0].astype(F32)
            g = g[:rows]
            delta, nm, nv = _adam_math(w_ref[...], g, m_ref[...], v_ref[...])
            for ref, val in zip(outs[4 * a:4 * a + 4], (g, delta, nm, nv)):
                ref[...] = val

    spec = pl.BlockSpec((rows, COL_TILE), lambda i, ids: (0, i))
    part = lambda j: pl.BlockSpec((1, prow, COL_TILE), lambda i, ids: (ids[j], 0, i))
    one = [spec, spec, spec, pl.BlockSpec((prow, COL_TILE), lambda i, ids: (0, i)), part(0), part(1), part(2)]
    args = [x for a in range(k) for x in (ws[a], ms[a], vs[a], chip_sums[a], from_chips[a], from_chips[a], from_chips[a])]
    res = pl.pallas_call(
        body, name=name,
        grid_spec=pltpu.PrefetchScalarGridSpec(
            num_scalar_prefetch=1, grid=(cols // COL_TILE,),
            in_specs=one * k + [ANY] * len(after), out_specs=[spec] * (4 * k)),
        out_shape=[jax.ShapeDtypeStruct((rows, cols), F32)] * (4 * k),
        compiler_params=_params(("parallel",)),
    )(other_chips, *args, *after)
    return [tuple(res[4 * a:4 * a + 4]) for a in range(k)]


def _chip_sum(mines, recvs, name):
    k = len(mines)
    rows, cols = mines[0].shape[1:]

    def body(*refs):
        own_chip = pl.program_id(0) == 2 * lax.axis_index("x") + lax.axis_index("y")
        for a in range(k):
            a_ref, b_ref = refs[2 * a:2 * a + 2]
            s_ref, sb_ref = refs[2 * k + 2 * a:2 * k + 2 * a + 2]
            s = a_ref[0] + b_ref[0].astype(F32)
            sb_ref[0] = s.astype(BF16)

            @pl.when(own_chip)
            def _(s_ref=s_ref, s=s):
                s_ref[...] = s

    by_chip = pl.BlockSpec((1, rows, cols), lambda c: (c, 0, 0))
    res = pl.pallas_call(
        body, name=name, grid=(N_DEV // 2,),
        in_specs=[by_chip, by_chip] * k, out_specs=[_const((rows, cols)), by_chip] * k,
        out_shape=[jax.ShapeDtypeStruct((rows, cols), F32), jax.ShapeDtypeStruct((N_DEV // 2, rows, cols), BF16)] * k,
        compiler_params=_params(("arbitrary",)),
    )(*[x for pair in zip(mines, recvs) for x in pair])
    return [tuple(res[2 * a:2 * a + 2]) for a in range(k)]


def _all_reduce_small(v, after=()):
    rows = v.shape[0]

    def body(v_ref, *rest):
        out_ref, gath, send_sems, recv_sems = rest[len(after):]
        x, y, c = _place()
        me, sibling = (x, y, c), (x, y, 1 - c)
        chips = [(1 - x, y), (x, 1 - y), (1 - x, 1 - y)]

        def blk(px, py, pc):
            return gath.at[pl.ds((4 * px + 2 * py + pc) * rows, rows), :]

        def copy(k, block, to, src=None):
            return pltpu.make_async_remote_copy(src_ref=blk(*block) if src is None else src, dst_ref=blk(*block),
                                                send_sem=send_sems.at[k], recv_sem=recv_sems.at[k], device_id=to, device_id_type=MESH)

        gath[pl.ds((4 * x + 2 * y + c) * rows, rows), :] = v_ref[...]
        first = [copy(0, me, sibling, src=v_ref)] + [copy(1 + j, me, (*chip, c), src=v_ref) for j, chip in enumerate(chips)]
        for cp in first:
            cp.start()
        passed = [copy(4 + j, (*chip, c), sibling) for j, chip in enumerate(chips)]
        for j, chip in enumerate(chips):
            copy(1 + j, (*chip, c), me).wait_recv()
            passed[j].start()
        copy(0, sibling, me).wait_recv()
        for j, chip in enumerate(chips):
            copy(4 + j, (*chip, 1 - c), me).wait_recv()
        for cp in first + passed:
            cp.wait_send()
        acc = gath[0:rows, :]
        for d in range(1, N_DEV):
            acc = acc + gath[d * rows:(d + 1) * rows, :]
        out_ref[...] = acc

    vm = pl.BlockSpec(memory_space=pltpu.VMEM)
    return pl.pallas_call(
        body, name="all_reduce_small",
        in_specs=[vm] + [ANY] * len(after), out_specs=vm,
        out_shape=jax.ShapeDtypeStruct(v.shape, F32),
        scratch_shapes=[pltpu.VMEM((N_DEV * rows, 128), F32), pltpu.SemaphoreType.DMA((7,)), pltpu.SemaphoreType.DMA((7,))],
    )(v, *after)


def _rope_tables(positions):
    inv_freq = ROPE_THETA ** (-jnp.arange(0, HD, 2, dtype=F32) / HD)
    ang = positions.reshape(T).astype(F32)[:, None] * inv_freq
    ang = jnp.concatenate([ang, ang, ang, ang], axis=-1)
    lo_half = (jnp.arange(128) % HD) < (HD // 2)
    return jnp.cos(ang), jnp.where(lo_half, -jnp.sin(ang), jnp.sin(ang))


def _selectors():
    lane = jnp.arange(QW)
    e = (lane[None, :] // HD == jnp.arange(SSM_H)[:, None]).astype(F32)
    e1 = ((lane[None, :] == HD * jnp.arange(128)[:, None]) & (jnp.arange(128)[:, None] < SSM_H)).astype(F32)
    src = jnp.arange(KVW)
    ex = ((lane[None, :] // (HD * (NQ // NKV)) == src[:, None] // HD) & (lane[None, :] % HD == src[:, None] % HD)).astype(F32)
    return e, e1, ex


WEIGHTS = ['ffn1_pre_norm', 'ffn1_w_gate', 'ffn1_w_up', 'ffn1_w_down', 'ffn1_post_norm', 'mix_pre_norm', 'w_in', 'conv_w', 'conv_b',
           'dt_bias', 'a_log', 'd_skip', 'ssm_norm', 'w_out', 'mix_post_norm', 'ffn2_pre_norm', 'ffn2_w_gate', 'ffn2_w_up',
           'ffn2_w_down', 'ffn2_post_norm']
COL_SHARDED = ['ffn1_w_gate', 'ffn1_w_up', 'ffn2_w_gate', 'ffn2_w_up', 'w_in']
ROW_SHARDED = ['ffn1_w_down', 'ffn2_w_down', 'w_out']
BIG = COL_SHARDED + ROW_SHARDED
FFN_BIG = COL_SHARDED[:4] + ROW_SHARDED[:2]
SMALL = ['ffn1_pre_norm', 'ffn1_post_norm', 'mix_pre_norm', 'conv_b', 'dt_bias', 'a_log', 'd_skip', 'ssm_norm', 'mix_post_norm',
         'ffn2_pre_norm', 'ffn2_post_norm']
FFN1 = ['ffn1_w_gate', 'ffn1_w_up', 'ffn1_w_down']
FFN2 = ['ffn2_w_gate', 'ffn2_w_up', 'ffn2_w_down']


def _wire_block(name, a):
    if name in FFN_BIG:
        return jnp.pad(a.astype(BF16), ((0, FSH - FSR), (0, 0)))
    if name == "w_in":
        return jnp.pad(a.astype(BF16), ((0, ISW - ISR), (0, 0)))
    return a if name == "conv_w" else a.astype(BF16)


def _whole_from_gathered(name, a):
    if name == "conv_w":
        return jnp.transpose(a, (1, 0, 2)).reshape(a.shape[1], -1)
    return a.reshape(-1, D)


def _step(x, positions, target, small, blocks=None, whole=None):
    dist = blocks is not None
    core = "mesh" if dist else 0
    w = dict(small)
    if whole:
        w.update(whole)

    def gather(names):
        return [_gather_duty([_wire_block(n, blocks[n]) for n in names])] if dist else []

    def put(names, results):
        if dist:
            for n, r in zip(names, results[0]):
                w[n] = _whole_from_gathered(n, r)

    g, sums, red = {}, {}, {}

    def swap(names):
        return [_swap_duty([g[n][1] for n in names])] if dist else []

    def chip_sums(names, from_sibling):
        if dist:
            res = _chip_sum([g[n][0] for n in names], list(from_sibling), "chip_sum_" + names[0])
            sums.update(zip(names, res))

    def exchange(names):
        return [_exchange_duty([sums[n][1] for n in names])] if dist else []

    def reduced(names, from_chips):
        if dist:
            for n, recv in zip(names, from_chips):
                red[n] = (sums[n][0], recv)

    cos, sin_s = _rope_tables(positions)
    e, e1, exf = _selectors()
    bias = _attn_bias()
    alx = jnp.repeat(w["a_log"], HD, axis=1)
    dskx = jnp.repeat(w["d_skip"], HD, axis=1)

    if dist:
        put(FFN1, _comm_only(gather(FFN1), "gather_ffn1"))
    (x1, n1, a1, b1, hm1, h1), got = _ffn_fwd(x, w["ffn1_pre_norm"], w["ffn1_w_gate"], w["ffn1_w_up"], w["ffn1_w_down"],
                                              w["ffn1_post_norm"], "ffn1_fwd", gather(["w_in", "conv_w"]))
    put(["w_in", "conv_w"], got)
    (n2, q, kx, vx, xbc, z, dtr), got = _inproj_fwd(x1, w["mix_pre_norm"], w["w_in"], cos, sin_s, exf.astype(BF16), gather(["w_out"]))
    put(["w_out"], got)
    (attn, lse), got = _attn_fwd(q, kx, vx, bias, gather(FFN2[:2]))
    put(FFN2[:2], got)
    (yn, y, hs, xconv), got = _ssd_fwd(xbc, z, dtr, w["conv_w"], w["conv_b"], w["dt_bias"], alx, dskx, w["ssm_norm"], e, gather(FFN2[2:]))
    put(FFN2[2:], got)
    x2, h2 = _outproj_fwd(x1, attn, yn, w["w_out"], w["mix_post_norm"])
    (dx3, n3, a3, b3, hm3, h3, ss), _ = _ffn_fwd(x2, w["ffn2_pre_norm"], w["ffn2_w_gate"], w["ffn2_w_up"], w["ffn2_w_down"],
                                                 w["ffn2_post_norm"], "ffn2_fwd", target=target)

    (dx2, da3, db3, dh3, g["ffn2_pre_norm"], g["ffn2_post_norm"]), _ = _ffn_bwd(
        dx3, x2, a3, b3, h3, w["ffn2_pre_norm"], w["ffn2_post_norm"], w["ffn2_w_gate"], w["ffn2_w_up"], w["ffn2_w_down"], "ffn2_bwd")
    g["ffn2_w_down"] = _matmul_tn(hm3, dh3, "ffn2_dwd", core)[0]
    g["ffn2_w_gate"] = _matmul_tn(da3, n3, "ffn2_dwg", core)[0]
    g["ffn2_w_up"] = _matmul_tn(db3, n3, "ffn2_dwu", core)[0]

    started = []

    def start(names):
        started.append((names, _exchange_start([sums[n][1] for n in names], "start_exchange_" + names[0])))
        return [started[-1][1]["token"]]

    (dh2, dmix, g["mix_post_norm"]), got = _outproj_bwd(dx2, h2, w["mix_post_norm"], w["w_out"], swap(FFN2))
    after = []
    if dist:
        chip_sums(FFN2, got[0])
        after = start(FFN2)
    g["w_out"] = _dwout(attn, yn, dh2, core)
    (dq, dkx, dvx), got = _attn_bwd(q, kx, vx, attn, dmix, lse, bias, swap(["w_out"]), after)
    if dist:
        chip_sums(["w_out"], got[0])
    (dxbc, dz, ddt, dcw, g["conv_b"], g["ssm_norm"], dpar), got = _ssd_bwd(
        dmix, xbc, xconv, z, dtr, y, hs, w["conv_w"], w["dt_bias"], alx, dskx, w["ssm_norm"], e, e1, exchange(["w_out"]))
    reduced(["w_out"], got[0] if dist else None)
    g["conv_w"] = dcw[0:4]
    g["dt_bias"], g["a_log"], g["d_skip"] = dpar[0:1], dpar[1:2], dpar[2:3]
    dx1, dproj, g["mix_pre_norm"] = _inproj_bwd(dx2, dq, dkx, dvx, dxbc, dz, ddt, x1, w["mix_pre_norm"], w["w_in"], cos, sin_s, exf)
    g["w_in"] = _matmul_tn(dproj, n2, "dwin", core)[0]

    after = []
    if dist:
        chip_sums(["w_in"], _comm_only(swap(["w_in"]), "swap_w_in")[0])
        after = start(["w_in"])
    (dx0, da1, db1, dh1, g["ffn1_pre_norm"], g["ffn1_post_norm"]), _ = _ffn_bwd(
        dx1, x, a1, b1, h1, w["ffn1_pre_norm"], w["ffn1_post_norm"], w["ffn1_w_gate"], w["ffn1_w_up"], w["ffn1_w_down"], "ffn1_bwd",
        after)
    total = None
    if dist:
        flat = jnp.concatenate([g[n].reshape(-1) for n in SMALL] + [g["conv_w"].reshape(-1), ss[0, 0:1]])
        rows = -(-flat.shape[0] // 128 // 8) * 8
        total = _all_reduce_small(jnp.pad(flat, (0, rows * 128 - flat.shape[0])).reshape(rows, 128), after)
        after = [total]
    g["ffn1_w_down"], _ = _matmul_tn(hm1, dh1, "ffn1_dwd", core, after=after)
    g["ffn1_w_gate"], got = _matmul_tn(da1, n1, "ffn1_dwg", core, duties=swap(["ffn1_w_down"]))
    if dist:
        chip_sums(["ffn1_w_down"], got[0])
        after = start(["ffn1_w_down"])
    g["ffn1_w_up"], got = _matmul_tn(db1, n1, "ffn1_dwu", core, after=after, duties=swap(["ffn1_w_gate"]))
    if dist:
        chip_sums(["ffn1_w_gate"], got[0])
        after = start(["ffn1_w_gate"])
        chip_sums(["ffn1_w_up"], _comm_only(swap(["ffn1_w_up"]), "swap_ffn1_w_up", after=after)[0])
        start(["ffn1_w_up"])
    return ss, dx0, g, red, started, {n: s[0] for n, s in sums.items()}, total


def kernel(x, positions, ffn1_pre_norm, ffn1_w_gate, ffn1_w_up, ffn1_w_down, ffn1_post_norm, mix_pre_norm, w_in, conv_w, conv_b, dt_bias, a_log, d_skip, ssm_norm, w_out, mix_post_norm, ffn2_pre_norm, ffn2_w_gate, ffn2_w_up, ffn2_w_down, ffn2_post_norm, loss_target, m_ffn1_pre_norm, m_ffn1_w_gate, m_ffn1_w_up, m_ffn1_w_down, m_ffn1_post_norm, m_mix_pre_norm, m_w_in, m_conv_w, m_conv_b, m_dt_bias, m_a_log, m_d_skip, m_ssm_norm, m_w_out, m_mix_post_norm, m_ffn2_pre_norm, m_ffn2_w_gate, m_ffn2_w_up, m_ffn2_w_down, m_ffn2_post_norm, v_ffn1_pre_norm, v_ffn1_w_gate, v_ffn1_w_up, v_ffn1_w_down, v_ffn1_post_norm, v_mix_pre_norm, v_w_in, v_conv_w, v_conv_b, v_dt_bias, v_a_log, v_d_skip, v_ssm_norm, v_w_out, v_mix_post_norm, v_ffn2_pre_norm, v_ffn2_w_gate, v_ffn2_w_up, v_ffn2_w_down, v_ffn2_post_norm):
    given = dict(locals())
    drop = lambda n, a: a if n in SMALL else (a[0].T if n in COL_SHARDED else a[0])
    w = {n: drop(n, given[n]) for n in WEIGHTS}
    m = {n: drop(n, given["m_" + n]) for n in WEIGHTS}
    v = {n: drop(n, given["v_" + n]) for n in WEIGHTS}
    cx, cy, cc = _place()
    others = [2 * (1 - cx) + cy, 2 * cx + (1 - cy), 2 * (1 - cx) + (1 - cy)]

    _, grad_x, g, red, started, chip_sum, total = _step(x[0], positions, loss_target[0], {n: w[n] for n in SMALL},
                                                        blocks={n: w[n] for n in BIG + ["conv_w"]})
    started = {names[0]: (names, st) for names, st in started}
    total = total.reshape(-1)
    chip_ids = jnp.stack(others).astype(jnp.int32)
    out_g, out_d, out_m, out_v = {}, {}, {}, {}

    def update(names, sums, recvs, label, after=()):
        res = _adamw_sharded([w[n] for n in names], [m[n] for n in names], [v[n] for n in names], sums, recvs, chip_ids,
                             "adamw_" + label, after)
        for n, (gn, dn, mn, vn) in zip(names, res):
            out_g[n], out_d[n], out_m[n], out_v[n] = gn, dn, mn, vn

    def arrived(first, after):
        names, st = started[first]
        return list(_exchange_wait(st, after, "wait_exchange_" + first))

    last_start = [started["ffn1_w_up"][1]["token"]]
    update(FFN2, [chip_sum[n] for n in FFN2], arrived(FFN2[0], last_start), "ffn2")
    update(["w_out"], [red["w_out"][0]], [red["w_out"][1]], "w_out", last_start)

    rows = total.shape[0] // 128
    sizes = [w[n].size for n in SMALL]
    offs = [0]
    for s_ in sizes:
        offs.append(offs[-1] + s_)
    gs = {n: total[offs[i]:offs[i + 1]].reshape(w[n].shape) for i, n in enumerate(SMALL)}
    gcw = total[offs[-1]:offs[-1] + 4 * CONV_C].reshape(4, CONV_C)
    loss = 0.5 * total[offs[-1] + 4 * CONV_C] / D
    gs["conv_w"] = lax.dynamic_slice_in_dim(gcw, (4 * cx + 2 * cy + cc) * (CONV_C // N_DEV), CONV_C // N_DEV, axis=1)
    names = SMALL + ["conv_w"]

    def pack(d):
        flat = jnp.concatenate([d[n].reshape(-1) for n in names])
        return jnp.pad(flat, (0, rows * 128 - flat.shape[0])).reshape(rows, 128)

    pg, pd, pm, pv = _adamw(pack(w), pack(m), pack(v), [pack(gs)], "adamw_small", last_start)
    done = [pg] + [out_v[n] for n in FFN2 + ["w_out"]]
    update(["w_in"], [chip_sum["w_in"]], arrived("w_in", done), "w_in")
    done = [out_v["w_in"]]
    update(FFN1, [chip_sum[n] for n in FFN1], [arrived(n, done)[0] for n in FFN1], "ffn1")
    o2 = [0]
    for n in names:
        o2.append(o2[-1] + w[n].size)
    for i, n in enumerate(names):
        for dst, src in ((out_g, pg), (out_d, pd), (out_m, pm), (out_v, pv)):
            dst[n] = src.reshape(-1)[o2[i]:o2[i + 1]].reshape(w[n].shape)

    outs = [loss, grad_x[None]]
    for d in (out_g, out_d, out_m, out_v):
        outs += [d[n] if n in SMALL else (d[n].T[None] if n in COL_SHARDED else d[n][None]) for n in WEIGHTS]
    return tuple(outs)
```

```python
import functools

import jax
import jax.numpy as jnp
from jax import lax
from jax.experimental import pallas as pl
from jax.experimental.pallas import tpu as pltpu

F32 = jnp.float32
BF16 = jnp.bfloat16
MESH = pl.DeviceIdType.MESH

N_DEV = 8
T = 2048
D = 1024
FF = 2816
FSR = FF // N_DEV
FSH = 384
FFP = N_DEV * FSH
HD = 64
NQ = 16
NKV = 4
QW = NQ * HD
KVW = NKV * HD
SSM_W = 1024
SSM_H = 16
SSM_N = 128
CONV_C = SSM_W + 2 * 2 * SSM_N
IN_COLS = 4112
INP = 4224
ISR = IN_COLS // N_DEV
ISW = 528
ISG = 640
L = 128
NCH = T // L
AB = 256
NAB = T // AB
EPS = 1e-6
NEG = -1e30
ROPE_THETA = 10000.0
DILATIONS = ((128, 1), (512, 4), (2048, 16))

ADAM_LR = 0.001
ADAM_B1 = 0.9
ADAM_B2 = 0.999
ADAM_EPS = 1e-08
ADAM_WD = 0.01
ADAM_STEP = 10

VMEM_LIMIT = 58 * 1024 * 1024


def _params(sem, vmem=VMEM_LIMIT):
    return pltpu.CompilerParams(dimension_semantics=sem, vmem_limit_bytes=vmem)


def _dot(a, b):
    return jnp.dot(a, b, preferred_element_type=F32)


def _dot_nt(a, b):
    return lax.dot_general(a, b, (((1,), (1,)), ((), ())), preferred_element_type=F32)


def _dot_tn(a, b):
    return lax.dot_general(a, b, (((0,), (0,)), ((), ())), preferred_element_type=F32)


def _split3(x):
    hi = x.astype(BF16)
    r1 = x - hi.astype(F32)
    mid = r1.astype(BF16)
    lo = (r1 - mid.astype(F32)).astype(BF16)
    return hi, mid, lo


def _dot_hi(a, b, a_is_01=False):
    if a_is_01:
        sel = a.astype(BF16)
        return sum(_dot(sel, p) for p in _split3(b))
    sel = b.astype(BF16)
    return sum(_dot(p, sel) for p in _split3(a))


def _dot_nt_hi(a, b):
    sel = b.astype(BF16)
    return sum(_dot_nt(p, sel) for p in _split3(a))


def _rs(x):
    return lax.rsqrt(jnp.mean(x * x, axis=-1, keepdims=True) + EPS)


def _sigmoid(x):
    return jax.nn.sigmoid(x)


def _dsilu(x, s):
    return s * (1.0 + x * (1.0 - s))


def _resident(shape):
    nd = len(shape)
    return pl.BlockSpec(shape, lambda *_: (0,) * nd, pipeline_mode=pl.Buffered(1))


def _const(shape):
    nd = len(shape)
    return pl.BlockSpec(shape, lambda *_: (0,) * nd)


def _rows(tm, cols):
    return pl.BlockSpec((tm, cols), lambda i: (i, 0))


ANY = pl.BlockSpec(memory_space=pl.ANY)


def _place():
    return lax.axis_index("x"), lax.axis_index("y"), lax.axis_index("c")


def _gather_duty(arrays):
    n = len(arrays)
    results = [jax.ShapeDtypeStruct((N_DEV,) + a.shape, a.dtype) for a in arrays]

    def make(ins, outs, send_sems, recv_sems, local_sems):
        x, y, c = _place()
        me, sibling = (x, y, c), (x, y, 1 - c)
        chips = [(1 - x, y), (x, 1 - y), (1 - x, 1 - y)]

        def place_of(a, px, py, pc):
            return outs[a].at[4 * px + 2 * py + pc]

        def copy(a, k, block, to, src=None):
            dst = place_of(a, *block)
            return pltpu.make_async_remote_copy(src_ref=dst if src is None else src, dst_ref=dst,
                                                send_sem=send_sems.at[7 * a + k], recv_sem=recv_sems.at[7 * a + k],
                                                device_id=to, device_id_type=MESH)

        def own(a):
            return pltpu.make_async_copy(ins[a], place_of(a, *me), local_sems.at[a])

        def first(a):
            return [copy(a, 0, me, sibling, src=ins[a])] + [copy(a, 1 + j, me, (*chip, c), src=ins[a]) for j, chip in enumerate(chips)]

        def start():
            for a in range(n):
                own(a).start()
            for a in range(n):
                for cp in first(a):
                    cp.start()

        def finish():
            for j, chip in enumerate(chips):
                for a in range(n):
                    copy(a, 1 + j, (*chip, c), me).wait_recv()
                    copy(a, 4 + j, (*chip, c), sibling).start()
            for a in range(n):
                copy(a, 0, sibling, me).wait_recv()
                for j, chip in enumerate(chips):
                    copy(a, 4 + j, (*chip, 1 - c), me).wait_recv()
            for a in range(n):
                for cp in first(a) + [copy(a, 4 + j, (*chip, c), sibling) for j, chip in enumerate(chips)]:
                    cp.wait_send()
                own(a).wait()

        return start, finish

    return dict(operands=list(arrays), results=results, sems=(7 * n, 7 * n, n), make=make)


def _swap_duty(arrays):
    n = len(arrays)
    half = N_DEV // 2
    results = [jax.ShapeDtypeStruct(a.shape, a.dtype) for a in arrays]

    def make(ins, outs, send_sems, recv_sems):
        x, y, c = _place()

        def copies():
            return [pltpu.make_async_remote_copy(src_ref=ins[a].at[k], dst_ref=outs[a].at[k],
                                                 send_sem=send_sems.at[half * a + k], recv_sem=recv_sems.at[half * a + k],
                                                 device_id=(x, y, 1 - c), device_id_type=MESH)
                    for a in range(n) for k in range(half)]

        def start():
            for cp in copies():
                cp.start()

        def finish():
            for cp in copies():
                cp.wait()

        return start, finish

    return dict(operands=list(arrays), results=results, sems=(half * n, half * n), make=make)


def _exchange_duty(arrays):
    n = len(arrays)
    results = [jax.ShapeDtypeStruct(a.shape, a.dtype) for a in arrays]

    def make(ins, outs, send_sems, recv_sems):
        x, y, c = _place()
        chips = [(1 - x, y), (x, 1 - y), (1 - x, 1 - y)]
        my_chip = 2 * x + y

        def sends():
            return [pltpu.make_async_remote_copy(src_ref=ins[a].at[2 * px + py], dst_ref=outs[a].at[my_chip],
                                                 send_sem=send_sems.at[3 * a + j], recv_sem=recv_sems.at[3 * a + j],
                                                 device_id=(px, py, c), device_id_type=MESH)
                    for a in range(n) for j, (px, py) in enumerate(chips)]

        def start():
            for cp in sends():
                cp.start()

        def finish():
            for a in range(n):
                for j, (px, py) in enumerate(chips):
                    pltpu.make_async_remote_copy(src_ref=ins[a].at[my_chip], dst_ref=outs[a].at[2 * px + py],
                                                 send_sem=send_sems.at[3 * a + j], recv_sem=recv_sems.at[3 * a + j],
                                                 device_id=(px, py, c), device_id_type=MESH).wait_recv()
            for cp in sends():
                cp.wait_send()

        return start, finish

    return dict(operands=list(arrays), results=results, sems=(3 * n, 3 * n), make=make)


def _call(body, *, name, grid, in_specs, out_specs, out_shape, args, sem, scratch=(), duties=(), after=()):
    n_in, n_out, n_scr = len(in_specs), len(out_specs), len(scratch)
    sem_shapes = [pltpu.SemaphoreType.DMA((k,)) for d in duties for k in d["sems"]]

    def full(*refs):
        pos = [0]

        def take(k):
            pos[0] += k
            return refs[pos[0] - k:pos[0]]

        ins = take(n_in)
        d_ins = [take(len(d["operands"])) for d in duties]
        take(len(after))
        outs = take(n_out)
        d_outs = [take(len(d["results"])) for d in duties]
        scr = take(n_scr)
        d_sems = [take(len(d["sems"])) for d in duties]
        hooks = [d["make"](di, do, *ds) for d, di, do, ds in zip(duties, d_ins, d_outs, d_sems)]
        if grid and hooks:
            ids = [pl.program_id(k) for k in range(len(grid))]
            first = functools.reduce(jnp.logical_and, [i == 0 for i in ids])
            last = functools.reduce(jnp.logical_and, [i == g - 1 for i, g in zip(ids, grid)])

            @pl.when(first)
            def _():
                for start, _ in hooks:
                    start()

            body(*ins, *outs, *scr)

            @pl.when(last)
            def _():
                for _, finish in hooks:
                    finish()
        else:
            for start, _ in hooks:
                start()
            body(*ins, *outs, *scr)
            for _, finish in hooks:
                finish()

    d_args = [a for d in duties for a in d["operands"]]
    d_res = [r for d in duties for r in d["results"]]
    kwargs = dict(grid=grid) if grid else {}
    res = pl.pallas_call(
        full, name=name, in_specs=list(in_specs) + [ANY] * (len(d_args) + len(after)), out_specs=list(out_specs) + [ANY] * len(d_res),
        out_shape=list(out_shape) + d_res, scratch_shapes=list(scratch) + sem_shapes,
        compiler_params=_params(sem) if grid else None, **kwargs,
    )(*args, *d_args, *after)
    own, rest = list(res[:n_out]), list(res[n_out:])
    by_duty = []
    for d in duties:
        by_duty.append(rest[:len(d["results"])])
        rest = rest[len(d["results"]):]
    return own, by_duty


def _comm_only(duties, name, after=()):
    return _call(lambda: None, name=name, grid=None, in_specs=[], out_specs=[], out_shape=[], args=[], sem=None, duties=duties,
                 after=after)[1]


HBM = pl.BlockSpec(memory_space=pltpu.HBM)
SEMS = pl.BlockSpec(memory_space=pltpu.SEMAPHORE)
SIDE_EFFECT = pltpu.SideEffectType.DATAFLOW_SIDE_EFFECTING
N_OTHER_CHIPS = 3


def _chip_copies(src_ref, land_ref, sems):
    x, y, c = _place()
    chips = [(1 - x, y), (x, 1 - y), (1 - x, 1 - y)]
    return [pltpu.make_async_remote_copy(src_ref=src_ref.at[2 * px + py], dst_ref=land_ref.at[2 * x + y],
                                         send_sem=sems[j], recv_sem=sems[N_OTHER_CHIPS + j], device_id=(px, py, c), device_id_type=MESH)
            for j, (px, py) in enumerate(chips)]


def _exchange_start(pb, name):
    n_sem = 2 * N_OTHER_CHIPS

    def body(pb_ref, land_ref, *rest):
        for cp in _chip_copies(pb_ref, land_ref, rest[:n_sem]):
            cp.start()
        token = rest[n_sem + 2]
        token[...] = jnp.zeros_like(token)

    res = pl.pallas_call(
        body, name=name,
        out_shape=(pltpu.SemaphoreType.DMA(()),) * n_sem + (pltpu.HBM(pb.shape, pb.dtype), pltpu.HBM(pb.shape, pb.dtype),
                                                              jax.ShapeDtypeStruct((8, 128), F32)),
        in_specs=(HBM, HBM), out_specs=(SEMS,) * n_sem + (HBM, HBM, pl.BlockSpec(memory_space=pltpu.VMEM)),
        input_output_aliases={0: n_sem, 1: n_sem + 1},
        compiler_params=pltpu.CompilerParams(has_side_effects=SIDE_EFFECT),
    )(pltpu.with_memory_space_constraint(pb, pltpu.HBM), pltpu.with_memory_space_constraint(lax.empty(pb.shape, pb.dtype), pltpu.HBM))
    return dict(sems=res[:n_sem], src=res[n_sem], land=res[n_sem + 1], token=res[n_sem + 2])


def _exchange_wait(started, after, name):
    n_sem = 2 * N_OTHER_CHIPS

    def body(pb_ref, land_ref, *rest):
        for cp in _chip_copies(pb_ref, land_ref, rest[:n_sem]):
            cp.wait_send()
            cp.wait_recv()

    src, land = started["src"], started["land"]
    return pl.pallas_call(
        body, name=name, out_shape=(pltpu.HBM(src.shape, src.dtype), pltpu.HBM(land.shape, land.dtype)),
        in_specs=(HBM, HBM) + (SEMS,) * n_sem + (ANY,) * len(after), out_specs=(HBM, HBM), input_output_aliases={0: 0, 1: 1},
        compiler_params=pltpu.CompilerParams(has_side_effects=SIDE_EFFECT),
    )(src, land, *started["sems"], *after)[1]


def _ffn_fwd(x, gpre, wg, wu, wd, gpost, name, duties=(), target=None):
    tm = 256
    n_in = 6 if target is None else 7

    def body(*refs):
        x_ref, gpre_ref, wg_ref, wu_ref, wd_ref, gpost_ref = refs[:6]
        xo_ref, n_ref, a_ref, b_ref, hm_ref, h_ref = refs[n_in:n_in + 6]
        xv = x_ref[...]
        n = (xv * _rs(xv) * gpre_ref[...]).astype(BF16)
        a = _dot_nt(n, wg_ref[...])
        b = _dot_nt(n, wu_ref[...])
        hm = (a * _sigmoid(a) * b).astype(BF16)
        h = _dot(hm, wd_ref[...])
        xo = xv + 0.5 * (h * _rs(h) * gpost_ref[...])
        n_ref[...] = n
        a_ref[...] = a.astype(BF16)
        b_ref[...] = b.astype(BF16)
        hm_ref[...] = hm
        h_ref[...] = h
        if target is None:
            xo_ref[...] = xo
        else:
            ss_ref = refs[n_in + 6]

            @pl.when(pl.program_id(0) == 0)
            def _():
                ss_ref[...] = jnp.zeros_like(ss_ref)

            err = xo - refs[6][...]
            xo_ref[...] = err * (1.0 / D)
            ss_ref[...] += jnp.sum(jnp.sum(err * err, axis=1, keepdims=True), axis=0, keepdims=True)

    loss_in = [] if target is None else [_rows(tm, D)]
    loss_out = [] if target is None else [_const((1, 128))]
    loss_shape = [] if target is None else [jax.ShapeDtypeStruct((1, 128), F32)]
    return _call(
        body, name=name, grid=(T // tm,),
        in_specs=[_rows(tm, D), _const((1, D)), _resident((FFP, D)), _resident((FFP, D)), _resident((FFP, D)), _const((1, D))] + loss_in,
        out_specs=[_rows(tm, D), _rows(tm, D), _rows(tm, FFP), _rows(tm, FFP), _rows(tm, FFP), _rows(tm, D)] + loss_out,
        out_shape=[jax.ShapeDtypeStruct((T, D), F32), jax.ShapeDtypeStruct((T, D), BF16), jax.ShapeDtypeStruct((T, FFP), BF16),
                   jax.ShapeDtypeStruct((T, FFP), BF16), jax.ShapeDtypeStruct((T, FFP), BF16), jax.ShapeDtypeStruct((T, D), F32)]
        + loss_shape,
        args=[x, gpre, wg, wu, wd, gpost] + ([] if target is None else [target]), sem=("arbitrary",), duties=duties)


def _ffn_bwd(dxo, x, a, b, h, gpre, gpost, wg, wu, wd, name, after=()):
    tm = 256

    def body(dxo_ref, x_ref, a_ref, b_ref, h_ref, gpre_ref, gpost_ref, wg_ref, wu_ref, wd_ref,
             dx_ref, da_ref, db_ref, dh_ref, dgpre_ref, dgpost_ref):
        @pl.when(pl.program_id(0) == 0)
        def _():
            dgpre_ref[...] = jnp.zeros_like(dgpre_ref)
            dgpost_ref[...] = jnp.zeros_like(dgpost_ref)

        dy = dxo_ref[...]
        h = h_ref[...]
        hn = h * _rs(h)
        r2 = _rs(h)
        dgpost_ref[...] += jnp.sum(0.5 * dy * hn, axis=0, keepdims=True)
        gdy = 0.5 * dy * gpost_ref[...]
        dh = r2 * (gdy - hn * jnp.mean(gdy * hn, axis=-1, keepdims=True))
        dhb = dh.astype(BF16)
        dh_ref[...] = dhb
        dhm = _dot_nt(dhb, wd_ref[...])
        av = a_ref[...].astype(F32)
        bv = b_ref[...].astype(F32)
        sg = _sigmoid(av)
        db = (dhm * (av * sg)).astype(BF16)
        da = (dhm * bv * _dsilu(av, sg)).astype(BF16)
        da_ref[...] = da
        db_ref[...] = db
        dn = _dot(da, wg_ref[...]) + _dot(db, wu_ref[...])
        xv = x_ref[...]
        r = _rs(xv)
        xn = xv * r
        dgpre_ref[...] += jnp.sum(dn * xn, axis=0, keepdims=True)
        gdn = dn * gpre_ref[...]
        dx_ref[...] = dy + r * (gdn - xn * jnp.mean(gdn * xn, axis=-1, keepdims=True))

    return _call(
        body, name=name, grid=(T // tm,),
        in_specs=[_rows(tm, D), _rows(tm, D), _rows(tm, FFP), _rows(tm, FFP), _rows(tm, D), _const((1, D)), _const((1, D)),
                  _resident((FFP, D)), _resident((FFP, D)), _resident((FFP, D))],
        out_specs=[_rows(tm, D), _rows(tm, FFP), _rows(tm, FFP), _rows(tm, D), _const((1, D)), _const((1, D))],
        out_shape=[jax.ShapeDtypeStruct((T, D), F32), jax.ShapeDtypeStruct((T, FFP), BF16), jax.ShapeDtypeStruct((T, FFP), BF16),
                   jax.ShapeDtypeStruct((T, D), BF16), jax.ShapeDtypeStruct((1, D), F32), jax.ShapeDtypeStruct((1, D), F32)],
        args=[dxo, x, a, b, h, gpre, gpost, wg, wu, wd], sem=("arbitrary",), after=after)


def _core_index(core):
    return lax.axis_index("c") if core == "mesh" else core


def _by_core(res, o_ref, ob_ref, core):
    r = res.shape[0] // 2
    c = jnp.asarray(_core_index(core))

    @pl.when(c == 0)
    def _():
        o_ref[0] = res[:r]
        ob_ref[0] = res[r:].astype(BF16)

    @pl.when(c == 1)
    def _():
        o_ref[0] = res[r:]
        ob_ref[0] = res[:r].astype(BF16)


def _matmul_tn(a, b, name, core, after=(), duties=()):
    k, m = a.shape
    n = b.shape[1]
    r = m // N_DEV
    assert m == N_DEV * r and r % 128 == 0

    def body(a_ref, b_ref, o_ref, ob_ref):
        _by_core(_dot_tn(a_ref[...], b_ref[...]), o_ref, ob_ref, core)

    spec = pl.BlockSpec((1, r, n), lambda i: (i, 0, 0))
    return _call(body, name=name, grid=(N_DEV // 2,), in_specs=[pl.BlockSpec((k, 2 * r), lambda i: (0, i)), _resident((k, n))],
                 out_specs=[spec, spec],
                 out_shape=[jax.ShapeDtypeStruct((N_DEV // 2, r, n), F32), jax.ShapeDtypeStruct((N_DEV // 2, r, n), BF16)],
                 args=[a, b], sem=("arbitrary",), duties=duties, after=after)


def _dwout(attn, yn, dh2, core):
    rs = (QW + SSM_W) // N_DEV
    chips = N_DEV // 2

    def body(at_ref, yn_ref, dh_ref, o_ref, ob_ref):
        i = pl.program_id(0)

        @pl.when(i < chips // 2)
        def _():
            _by_core(_dot_tn(at_ref[...], dh_ref[...]), o_ref, ob_ref, core)

        @pl.when(i >= chips // 2)
        def _():
            _by_core(_dot_tn(yn_ref[...], dh_ref[...]), o_ref, ob_ref, core)

    spec = pl.BlockSpec((1, rs, D), lambda i: (i, 0, 0))
    return pl.pallas_call(
        body, name="dwout", grid=(chips,),
        in_specs=[pl.BlockSpec((T, 2 * rs), lambda i: (0, jnp.minimum(i, chips // 2 - 1))),
                  pl.BlockSpec((T, 2 * rs), lambda i: (0, jnp.maximum(i - chips // 2, 0))), _resident((T, D))],
        out_specs=[spec, spec],
        out_shape=[jax.ShapeDtypeStruct((chips, rs, D), F32), jax.ShapeDtypeStruct((chips, rs, D), BF16)],
        compiler_params=_params(("arbitrary",)),
    )(attn, yn, dh2)


def _rope_swap(t, lo_half):
    return jnp.where(lo_half, pltpu.roll(t, 96, 1), pltpu.roll(t, 32, 1))


def _inproj_fwd(x1, gpre, win, cos, sin_s, ex, duties=()):
    tm = 256

    def body(x_ref, g_ref, w_ref, cos_ref, sin_ref, ex_ref, n_ref, q_ref, kx_ref, vx_ref, xbc_ref, z_ref, dt_ref):
        xv = x_ref[...]
        n = (xv * _rs(xv) * g_ref[...]).astype(BF16)
        n_ref[...] = n
        by_dev = _dot_nt(n, w_ref[...])
        proj = jnp.concatenate([by_dev[:, ISW * d:ISW * d + ISR] for d in range(N_DEV)], axis=1)
        cs = cos_ref[...]
        sn = sin_ref[...]
        lo_half = (lax.broadcasted_iota(jnp.int32, (1, 128), 1) % HD) < (HD // 2)

        def rope(t):
            return t * cs + _rope_swap(t, lo_half) * sn

        for j in range(QW // 128):
            t = proj[:, 128 * j:128 * j + 128]
            q_ref[:, 128 * j:128 * j + 128] = (rope(t) * (HD ** -0.5)).astype(BF16)
        k = jnp.concatenate([rope(proj[:, QW + 128 * j:QW + 128 * j + 128]) for j in range(KVW // 128)], axis=1)
        v = proj[:, QW + KVW:QW + 2 * KVW]
        kx_ref[...] = _dot(k.astype(BF16), ex_ref[...]).astype(BF16)
        vx_ref[...] = _dot(v.astype(BF16), ex_ref[...]).astype(BF16)
        c0 = QW + 2 * KVW
        xbc_ref[...] = proj[:, c0:c0 + CONV_C]
        z_ref[...] = proj[:, c0 + CONV_C:c0 + CONV_C + SSM_W]
        dt_ref[...] = proj[:, c0 + CONV_C + SSM_W:IN_COLS]

    return _call(
        body, name="inproj_fwd", grid=(T // tm,),
        in_specs=[_rows(tm, D), _const((1, D)), _resident((INP, D)), _rows(tm, 128), _rows(tm, 128), _const((KVW, QW))],
        out_specs=[_rows(tm, D), _rows(tm, QW), _rows(tm, QW), _rows(tm, QW), _rows(tm, CONV_C), _rows(tm, SSM_W), _rows(tm, SSM_H)],
        out_shape=[jax.ShapeDtypeStruct((T, D), BF16), jax.ShapeDtypeStruct((T, QW), BF16), jax.ShapeDtypeStruct((T, QW), BF16),
                   jax.ShapeDtypeStruct((T, QW), BF16), jax.ShapeDtypeStruct((T, CONV_C), F32), jax.ShapeDtypeStruct((T, SSM_W), F32),
                   jax.ShapeDtypeStruct((T, SSM_H), F32)],
        args=[x1, gpre, win, cos, sin_s, ex], sem=("arbitrary",), duties=duties)


def _inproj_bwd(dres, dq, dkx, dvx, dxbc, dz, ddt, x1, gpre, win, cos, sin_s, exf):
    tm = 256

    def body(dres_ref, dq_ref, dkx_ref, dvx_ref, dxbc_ref, dz_ref, ddt_ref, x_ref, g_ref, w_ref, cos_ref, sin_ref, ex_ref,
             dx_ref, dps_ref, dg_ref, dp_ref):
        @pl.when(pl.program_id(0) == 0)
        def _():
            dg_ref[...] = jnp.zeros_like(dg_ref)

        cs = cos_ref[...]
        sn = sin_ref[...]
        lo_half = (lax.broadcasted_iota(jnp.int32, (1, 128), 1) % HD) < (HD // 2)

        def rope_t(t):
            return t * cs - _rope_swap(t, lo_half) * sn

        for j in range(QW // 128):
            dp_ref[:, 128 * j:128 * j + 128] = rope_t(dq_ref[:, 128 * j:128 * j + 128] * (HD ** -0.5)).astype(BF16)
        dk = _dot_nt_hi(dkx_ref[...], ex_ref[...])
        dv = _dot_nt_hi(dvx_ref[...], ex_ref[...])
        for j in range(KVW // 128):
            dp_ref[:, QW + 128 * j:QW + 128 * j + 128] = rope_t(dk[:, 128 * j:128 * j + 128]).astype(BF16)
        dp_ref[:, QW + KVW:QW + 2 * KVW] = dv.astype(BF16)
        c0 = QW + 2 * KVW
        dp_ref[:, c0:c0 + CONV_C] = dxbc_ref[...].astype(BF16)
        dp_ref[:, c0 + CONV_C:c0 + CONV_C + SSM_W] = dz_ref[...].astype(BF16)
        dp_ref[:, c0 + CONV_C + SSM_W:INP] = ddt_ref[...].astype(BF16)
        pieces = [dp_ref[:, ISR * d:ISR * (d + 1)] for d in range(N_DEV)]
        zw = jnp.zeros((tm, ISW - ISR), BF16)
        zg = jnp.zeros((tm, ISG - ISR), BF16)
        dn = _dot(jnp.concatenate([t for p in pieces for t in (p, zw)], axis=1), w_ref[...])
        for d in range(N_DEV):
            dps_ref[:, ISG * d:ISG * (d + 1)] = jnp.concatenate([pieces[d], zg], axis=1)
        xv = x_ref[...]
        r = _rs(xv)
        xn = xv * r
        dg_ref[...] += jnp.sum(dn * xn, axis=0, keepdims=True)
        gdn = dn * g_ref[...]
        dx_ref[...] = dres_ref[...] + r * (gdn - xn * jnp.mean(gdn * xn, axis=-1, keepdims=True))

    return pl.pallas_call(
        body, name="inproj_bwd", grid=(T // tm,),
        in_specs=[_rows(tm, D), _rows(tm, QW), _rows(tm, QW), _rows(tm, QW), _rows(tm, CONV_C), _rows(tm, SSM_W), _rows(tm, 128),
                  _rows(tm, D), _const((1, D)), _resident((INP, D)), _rows(tm, 128), _rows(tm, 128), _const((KVW, QW))],
        out_specs=[_rows(tm, D), _rows(tm, N_DEV * ISG), _const((1, D))],
        out_shape=[jax.ShapeDtypeStruct((T, D), F32), jax.ShapeDtypeStruct((T, N_DEV * ISG), BF16), jax.ShapeDtypeStruct((1, D), F32)],
        scratch_shapes=[pltpu.VMEM((tm, INP), BF16)],
        compiler_params=_params(("arbitrary",)),
    )(dres, dq, dkx, dvx, dxbc, dz, ddt, x1, gpre, win, cos, sin_s, exf)


def _outproj_fwd(x1, attn, yn, wout, gpost):
    tm = 512

    def body(x_ref, at_ref, yn_ref, w_ref, g_ref, xo_ref, h_ref):
        h = _dot(at_ref[...], w_ref[0:QW, :]) + _dot(yn_ref[...], w_ref[QW:QW + SSM_W, :])
        h_ref[...] = h
        xo_ref[...] = x_ref[...] + h * _rs(h) * g_ref[...]

    return pl.pallas_call(
        body, name="outproj_fwd", grid=(T // tm,),
        in_specs=[_rows(tm, D), _rows(tm, QW), _rows(tm, SSM_W), _resident((QW + SSM_W, D)), _const((1, D))],
        out_specs=[_rows(tm, D), _rows(tm, D)],
        out_shape=[jax.ShapeDtypeStruct((T, D), F32), jax.ShapeDtypeStruct((T, D), F32)],
        compiler_params=_params(("parallel",)),
    )(x1, attn, yn, wout, gpost)


def _outproj_bwd(dx2, h2, gpost, wout, duties=()):
    tm = 512

    def body(dy_ref, h_ref, g_ref, w_ref, dh_ref, dm_ref, dg_ref):
        @pl.when(pl.program_id(0) == 0)
        def _():
            dg_ref[...] = jnp.zeros_like(dg_ref)

        dy = dy_ref[...]
        h = h_ref[...]
        r = _rs(h)
        hn = h * r
        dg_ref[...] += jnp.sum(dy * hn, axis=0, keepdims=True)
        gdy = dy * g_ref[...]
        dh = (r * (gdy - hn * jnp.mean(gdy * hn, axis=-1, keepdims=True))).astype(BF16)
        dh_ref[...] = dh
        dm_ref[...] = _dot_nt(dh, w_ref[...])

    return _call(
        body, name="outproj_bwd", grid=(T // tm,),
        in_specs=[_rows(tm, D), _rows(tm, D), _const((1, D)), _resident((QW + SSM_W, D))],
        out_specs=[_rows(tm, D), _rows(tm, QW + SSM_W), _const((1, D))],
        out_shape=[jax.ShapeDtypeStruct((T, D), BF16), jax.ShapeDtypeStruct((T, QW + SSM_W), F32), jax.ShapeDtypeStruct((1, D), F32)],
        args=[dx2, h2, gpost, wout], sem=("arbitrary",), duties=duties)


def _attn_bias():
    d = jnp.arange(AB)[:, None] - jnp.arange(T)[None, :] + (T - AB)
    cnt = jnp.zeros(d.shape, F32)
    for window, dil in DILATIONS:
        cnt = cnt + ((d >= 0) & (d % dil == 0) & (d <= window)).astype(F32)
    return jnp.where(cnt > 0, jnp.log(jnp.maximum(cnt, 1.0)), NEG)


G_PER = NQ // NKV
WK = G_PER * HD


def _attn_fwd(q, kx, vx, bias, duties=()):
    def body(q_ref, kx_ref, vx_ref, bias_ref, o_ref, lse_ref):
        lane = lax.broadcasted_iota(jnp.int32, (1, WK), 1)
        lse_ref[...] = jnp.zeros_like(lse_ref)
        for i in range(NAB):
            n = (i + 1) * AB
            rows = slice(i * AB, n)
            qi = q_ref[rows, :]
            kxi = kx_ref[0:n, :]
            vxi = vx_ref[0:n, :]
            bb = bias_ref[:, (NAB - 1 - i) * AB:]
            o_acc = jnp.zeros((AB, WK), F32)
            for g in range(G_PER):
                mg = (lane // HD) == g
                s = _dot_nt(jnp.where(mg, qi, jnp.zeros_like(qi)), kxi) + bb
                m = jnp.max(s, axis=1, keepdims=True)
                p = jnp.exp(s - m)
                l = jnp.sum(p, axis=1, keepdims=True)
                o_acc = jnp.where(mg, _dot(p.astype(BF16), vxi) / l, o_acc)
                lse_ref[rows, g:g + 1] = m + jnp.log(l)
            o_ref[rows, :] = o_acc.astype(BF16)

    col = lambda kv: (0, kv)
    return _call(
        body, name="attn_fwd", grid=(NKV,),
        in_specs=[pl.BlockSpec((T, WK), col), pl.BlockSpec((T, WK), col), pl.BlockSpec((T, WK), col), _const((AB, T))],
        out_specs=[pl.BlockSpec((T, WK), col), pl.BlockSpec((T, 128), col)],
        out_shape=[jax.ShapeDtypeStruct((T, QW), BF16), jax.ShapeDtypeStruct((T, NKV * 128), F32)],
        args=[q, kx, vx, bias], sem=("arbitrary",), duties=duties)


def _attn_bwd(q, kx, vx, o, dmix, lse, bias, duties=()):
    def body(q_ref, kx_ref, vx_ref, o_ref, do_ref, lse_ref, bias_ref, dq_ref, dkx_ref, dvx_ref):
        lane = lax.broadcasted_iota(jnp.int32, (1, WK), 1)
        dkx_ref[...] = jnp.zeros_like(dkx_ref)
        dvx_ref[...] = jnp.zeros_like(dvx_ref)
        for i in range(NAB):
            n = (i + 1) * AB
            rows = slice(i * AB, n)
            qi = q_ref[rows, :]
            dof = do_ref[rows, :]
            doi = dof.astype(BF16)
            prod = dof * o_ref[rows, :].astype(F32)
            kxi = kx_ref[0:n, :]
            vxi = vx_ref[0:n, :]
            bb = bias_ref[:, (NAB - 1 - i) * AB:]
            dq_acc = jnp.zeros((AB, WK), F32)
            for g in range(G_PER):
                mg = (lane // HD) == g
                qm = jnp.where(mg, qi, jnp.zeros_like(qi))
                dom = jnp.where(mg, doi, jnp.zeros_like(doi))
                delta = jnp.sum(jnp.where(mg, prod, 0.0), axis=1, keepdims=True)
                p = jnp.exp(_dot_nt(qm, kxi) + bb - lse_ref[rows, g:g + 1])
                ds = (p * (_dot_nt(dom, vxi) - delta)).astype(BF16)
                dvx_ref[0:n, :] += _dot_tn(p.astype(BF16), dom)
                dkx_ref[0:n, :] += _dot_tn(ds, qm)
                dq_acc = jnp.where(mg, _dot(ds, kxi), dq_acc)
            dq_ref[rows, :] = dq_acc

    col = lambda kv: (0, kv)
    return _call(
        body, name="attn_bwd", grid=(NKV,),
        in_specs=[pl.BlockSpec((T, WK), col), pl.BlockSpec((T, WK), col), pl.BlockSpec((T, WK), col), pl.BlockSpec((T, WK), col),
                  pl.BlockSpec((T, WK), col), pl.BlockSpec((T, 128), col), _const((AB, T))],
        out_specs=[pl.BlockSpec((T, WK), col), pl.BlockSpec((T, WK), col), pl.BlockSpec((T, WK), col)],
        out_shape=[jax.ShapeDtypeStruct((T, QW), F32)] * 3,
        args=[q, kx, vx, o, dmix, lse, bias], sem=("arbitrary",), duties=duties)


def _softplus(x):
    return jnp.maximum(x, 0.0) + jnp.log1p(jnp.exp(-jnp.abs(x)))


def _causal_conv(u, zs, cw_ref, cb_ref):
    zs[8:, :] = u
    sh1, sh2, sh3 = (zs[8 - m:8 - m + L, :] for m in (1, 2, 3))
    return cb_ref[...] + cw_ref[3:4, :] * u + cw_ref[2:3, :] * sh1 + cw_ref[1:2, :] * sh2 + cw_ref[0:1, :] * sh3


def _ssd_chunk_common(xc, dtr, dtb_ref, alx_ref, e_ref):
    sg = _sigmoid(xc)
    act = xc * sg
    pre = dtr + dtb_ref[...]
    dt_x = _dot_hi(_softplus(pre), e_ref[...])
    a_x = -jnp.exp(alx_ref[...])
    ri = lax.broadcasted_iota(jnp.int32, (L, L), 0)
    ci = lax.broadcasted_iota(jnp.int32, (L, L), 1)
    tri = ri >= ci
    acs_x = _dot_hi(tri, dt_x * a_x, a_is_01=True)
    return dict(sg=sg, act=act, pre=pre, dt_x=dt_x, a_x=a_x, tri=tri, acs_x=acs_x)


def _decay(acs_x, acs_t, h, tri):
    col = acs_x[:, HD * h:HD * h + 1]
    row = acs_t[HD * h:HD * h + 1, :]
    return jnp.exp(jnp.where(tri, col - row, NEG))


def _ssd_fwd(xbc, z, dtr, convw, convb, dtb, alx, dskx, ssmn, e, duties=()):
    def body(u_ref, z_ref, dtr_ref, cw_ref, cb_ref, dtb_ref, alx_ref, dsk_ref, sn_ref, e_ref,
             yn_ref, y_ref, hs_ref, xc_ref, zs, hst):
        @pl.when(pl.program_id(0) == 0)
        def _():
            zs[0:8, :] = jnp.zeros((8, CONV_C), F32)
            hst[...] = jnp.zeros_like(hst)

        u = u_ref[...]
        xc = _causal_conv(u, zs, cw_ref, cb_ref)
        xc_ref[...] = xc
        zs[0:8, :] = u[L - 8:, :]
        cm = _ssd_chunk_common(xc, dtr_ref[...], dtb_ref, alx_ref, e_ref)
        act, dt_x, acs_x, tri = cm["act"], cm["dt_x"], cm["acs_x"], cm["tri"]
        xs = act[:, :SSM_W]
        acs_l = acs_x[L - 1:L, :]
        lam_x = jnp.exp(acs_x)
        w_x = jnp.exp(acs_l - acs_x)
        gam_x = jnp.exp(acs_l)
        acs_t = acs_x.T
        xd = xs * dt_x
        xb = xd.astype(BF16)
        xw = (xd * w_x).astype(BF16)
        lo = lax.broadcasted_iota(jnp.int32, (1, 128), 1) < HD
        hs_ref[0] = hst[...]
        pieces = []
        for grp in range(2):
            bb = act[:, SSM_W + SSM_N * grp:SSM_W + SSM_N * (grp + 1)].astype(BF16)
            cb_ = act[:, SSM_W + 2 * SSM_N + SSM_N * grp:SSM_W + 2 * SSM_N + SSM_N * (grp + 1)].astype(BF16)
            cbm = _dot_nt(cb_, bb)
            for jj in range(4):
                j = 4 * grp + jj
                sl = slice(128 * j, 128 * j + 128)
                m0 = (cbm * _decay(acs_x, acs_t, 2 * j, tri)).astype(BF16)
                m1 = (cbm * _decay(acs_x, acs_t, 2 * j + 1, tri)).astype(BF16)
                x2 = xb[:, sl]
                ydiag = jnp.where(lo, _dot(m0, x2), _dot(m1, x2))
                hprev = hst[j]
                yoff = lam_x[:, sl] * _dot(cb_, hprev.astype(BF16))
                pieces.append(ydiag + yoff)
                hst[j] = gam_x[:, sl] * hprev + _dot_tn(bb, xw[:, sl])
        y = jnp.concatenate(pieces, axis=1) + dsk_ref[...] * xs
        y_ref[...] = y
        zv = z_ref[...]
        yz = y * (zv * _sigmoid(zv))
        half = SSM_W // 2
        yn = jnp.concatenate([yz[:, :half] * _rs(yz[:, :half]), yz[:, half:] * _rs(yz[:, half:])], axis=1)
        yn_ref[...] = (yn * sn_ref[...]).astype(BF16)

    return _call(
        body, name="ssd_fwd", grid=(NCH,),
        in_specs=[_rows(L, CONV_C), _rows(L, SSM_W), _rows(L, SSM_H), _const((4, CONV_C)), _const((1, CONV_C)), _const((1, SSM_H)),
                  _const((1, SSM_W)), _const((1, SSM_W)), _const((1, SSM_W)), _const((SSM_H, SSM_W))],
        out_specs=[_rows(L, SSM_W), _rows(L, SSM_W), pl.BlockSpec((1, 8, SSM_N, 128), lambda c: (c, 0, 0, 0)), _rows(L, CONV_C)],
        out_shape=[jax.ShapeDtypeStruct((T, SSM_W), BF16), jax.ShapeDtypeStruct((T, SSM_W), F32),
                   jax.ShapeDtypeStruct((NCH, 8, SSM_N, 128), F32), jax.ShapeDtypeStruct((T, CONV_C), F32)],
        scratch=[pltpu.VMEM((8 + L, CONV_C), F32), pltpu.VMEM((8, SSM_N, 128), F32)],
        args=[xbc, z, dtr, convw, convb, dtb, alx, dskx, ssmn, e], sem=("arbitrary",), duties=duties)


def _ssd_bwd(dmix, xbc, xconv, z, dtr, y, hs, convw, dtb, alx, dskx, ssmn, e, e1, duties=()):
    rev = lambda i: (NCH - 1 - i, 0)

    def body(dyn_ref, u_ref, xc_ref, z_ref, dtr_ref, y_ref, hs_ref, cw_ref, dtb_ref, alx_ref, dsk_ref, sn_ref, e_ref, e1_ref,
             dxbc_ref, dz_ref, ddt_ref, dcw_ref, dcb_ref, dsn_ref, dpar_ref, dh, zd, colbuf):
        step = pl.program_id(0)

        @pl.when(step == 0)
        def _():
            for r in (dh, dcw_ref, dcb_ref, dsn_ref, dpar_ref):
                r[...] = jnp.zeros_like(r)
            zd[L:, :] = jnp.zeros((8, CONV_C), F32)

        u = u_ref[...]
        xc = xc_ref[...]
        cm = _ssd_chunk_common(xc, dtr_ref[...], dtb_ref, alx_ref, e_ref)
        sg, act, pre, dt_x, a_x, tri, acs_x = (cm[k] for k in ("sg", "act", "pre", "dt_x", "a_x", "tri", "acs_x"))
        xs = act[:, :SSM_W]
        acs_l = acs_x[L - 1:L, :]
        lam_x = jnp.exp(acs_x)
        w_x = jnp.exp(acs_l - acs_x)
        gam_x = jnp.exp(acs_l)
        acs_t = acs_x.T
        xd = xs * dt_x
        xb = xd.astype(BF16)
        xdw = xd * w_x
        xw = xdw.astype(BF16)
        lo = lax.broadcasted_iota(jnp.int32, (1, 128), 1) < HD
        row8 = lax.broadcasted_iota(jnp.int32, (8, 1), 0)

        dyn = dyn_ref[...]
        yv = y_ref[...]
        zv = z_ref[...]
        sz = _sigmoid(zv)
        siluz = zv * sz
        yz = yv * siluz
        half = SSM_W // 2
        gy = dyn * sn_ref[...]
        dyz_parts, yzn_parts = [], []
        for hf in range(2):
            part = yz[:, hf * half:(hf + 1) * half]
            r = _rs(part)
            pn = part * r
            gp = gy[:, hf * half:(hf + 1) * half]
            dyz_parts.append(r * (gp - pn * jnp.mean(gp * pn, axis=-1, keepdims=True)))
            yzn_parts.append(pn)
        dyz = jnp.concatenate(dyz_parts, axis=1)
        dsn_ref[...] += jnp.sum(dyn * jnp.concatenate(yzn_parts, axis=1), axis=0, keepdims=True)
        dy = dyz * siluz
        dz_ref[...] = dyz * yv * _dsilu(zv, sz)

        colbuf[...] = jnp.zeros_like(colbuf)
        dx_pieces, dacs_pieces, dacsl_pieces, db_pieces, dc_pieces = [], [], [], [], []
        for grp in range(2):
            bb = act[:, SSM_W + SSM_N * grp:SSM_W + SSM_N * (grp + 1)].astype(BF16)
            cb_ = act[:, SSM_W + 2 * SSM_N + SSM_N * grp:SSM_W + 2 * SSM_N + SSM_N * (grp + 1)].astype(BF16)
            cbm = _dot_nt(cb_, bb)
            dcbm = jnp.zeros((L, L), F32)
            dc_g = jnp.zeros((L, SSM_N), F32)
            db_g = jnp.zeros((L, SSM_N), F32)
            for jj in range(4):
                j = 4 * grp + jj
                sl = slice(128 * j, 128 * j + 128)
                dy2 = dy[:, sl]
                dy2b = dy2.astype(BF16)
                d0 = _decay(acs_x, acs_t, 2 * j, tri)
                d1 = _decay(acs_x, acs_t, 2 * j + 1, tri)
                m0 = cbm * d0
                m1 = cbm * d1
                x2 = xb[:, sl]
                hprev = hs_ref[0, j]
                hprevb = hprev.astype(BF16)
                dhn = dh[j]
                dhnb = dhn.astype(BF16)
                g2 = _dot(bb, dhnb)
                dx_pieces.append(jnp.where(lo, _dot_tn(m0.astype(BF16), dy2b), _dot_tn(m1.astype(BF16), dy2b)) + w_x[:, sl] * g2)
                zero = jnp.zeros_like(dy2b)
                dm0 = _dot_nt(jnp.where(lo, dy2b, zero), x2)
                dm1 = _dot_nt(jnp.where(lo, zero, dy2b), x2)
                dcbm = dcbm + dm0 * d0 + dm1 * d1
                e0 = dm0 * m0
                e1v = dm1 * m1
                colbuf[:, 2 * j:2 * j + 1] = jnp.sum(e0, axis=1, keepdims=True) - jnp.sum(e0.T, axis=1, keepdims=True)
                colbuf[:, 2 * j + 1:2 * j + 2] = jnp.sum(e1v, axis=1, keepdims=True) - jnp.sum(e1v.T, axis=1, keepdims=True)
                yoff = lam_x[:, sl] * _dot(cb_, hprevb)
                gxw = g2 * xdw[:, sl]
                dacs_pieces.append(dy2 * yoff - gxw)
                dacsl_pieces.append(jnp.sum(gxw, axis=0, keepdims=True) + gam_x[:, sl] * jnp.sum(dhn * hprev, axis=0, keepdims=True))
                dyl = (dy2 * lam_x[:, sl]).astype(BF16)
                dc_g = dc_g + _dot_nt(dyl, hprevb)
                db_g = db_g + _dot_nt(xw[:, sl], dhnb)
                dh[j] = gam_x[:, sl] * dhn + _dot_tn(cb_, dyl)
            dcbb = dcbm.astype(BF16)
            dc_pieces.append(dc_g + _dot(dcbb, bb))
            db_pieces.append(db_g + _dot_tn(dcbb, cb_))

        dxd = jnp.concatenate(dx_pieces, axis=1)
        rowi = lax.broadcasted_iota(jnp.int32, (L, 1), 0)
        dacs_x = (jnp.concatenate(dacs_pieces, axis=1) + _dot_hi(colbuf[...], e1_ref[...])
                  + jnp.where(rowi == L - 1, jnp.concatenate(dacsl_pieces, axis=1), 0.0))
        upper = lax.broadcasted_iota(jnp.int32, (L, L), 0) <= lax.broadcasted_iota(jnp.int32, (L, L), 1)
        dadt_x = _dot_hi(upper, dacs_x, a_is_01=True)
        ddt_x = dxd * xs + dadt_x * a_x
        ddtr = _dot_nt_hi(ddt_x, e_ref[...]) * _sigmoid(pre)
        ddt_ref[...] = jnp.zeros_like(ddt_ref)
        ddt_ref[:, 0:SSM_H] = ddtr
        dalx =jnp.sum(dadt_x * dt_x, axis=0, keepdims=True) * a_x
        ddskx = jnp.sum(dy * xs, axis=0, keepdims=True)
        par_x = jnp.where(row8 == 1, dalx, 0.0) + jnp.where(row8 == 2, ddskx, 0.0)
        dpar_ref[...] += _dot_nt_hi(par_x, e_ref[...]) + jnp.where(row8 == 0, jnp.sum(ddtr, axis=0, keepdims=True), 0.0)

        dxs = dxd * dt_x + dsk_ref[...] * dy
        dact = jnp.concatenate([dxs] + db_pieces + dc_pieces, axis=1)
        du = dact * _dsilu(xc, sg)
        dcb_ref[...] += jnp.sum(du, axis=0, keepdims=True)
        zd[0:L, :] = du
        f1, f2, f3 = (zd[m:m + L, :] for m in (1, 2, 3))
        dxbc_ref[...] = cw_ref[3:4, :] * du + cw_ref[2:3, :] * f1 + cw_ref[1:2, :] * f2 + cw_ref[0:1, :] * f3
        dcw = jnp.zeros((8, CONV_C), F32)
        for k, shifted in enumerate((f3, f2, f1, du)):
            dcw = dcw + jnp.where(row8 == k, jnp.sum(shifted * u, axis=0, keepdims=True), 0.0)
        dcw_ref[...] += dcw
        zd[L:, :] = du[:8, :]

    return _call(
        body, name="ssd_bwd", grid=(NCH,),
        in_specs=[pl.BlockSpec((L, SSM_W), lambda i: (NCH - 1 - i, 1)), pl.BlockSpec((L, CONV_C), rev), pl.BlockSpec((L, CONV_C), rev),
                  pl.BlockSpec((L, SSM_W), rev), pl.BlockSpec((L, SSM_H), rev), pl.BlockSpec((L, SSM_W), rev),
                  pl.BlockSpec((1, 8, SSM_N, 128), lambda i: (NCH - 1 - i, 0, 0, 0)),
                  _const((4, CONV_C)), _const((1, SSM_H)), _const((1, SSM_W)), _const((1, SSM_W)), _const((1, SSM_W)),
                  _const((SSM_H, SSM_W)), _const((128, SSM_W))],
        out_specs=[pl.BlockSpec((L, CONV_C), rev), pl.BlockSpec((L, SSM_W), rev), pl.BlockSpec((L, 128), rev),
                   _const((8, CONV_C)), _const((1, CONV_C)), _const((1, SSM_W)), _const((8, SSM_H))],
        out_shape=[jax.ShapeDtypeStruct((T, CONV_C), F32), jax.ShapeDtypeStruct((T, SSM_W), F32), jax.ShapeDtypeStruct((T, 128), F32),
                   jax.ShapeDtypeStruct((8, CONV_C), F32), jax.ShapeDtypeStruct((1, CONV_C), F32), jax.ShapeDtypeStruct((1, SSM_W), F32),
                   jax.ShapeDtypeStruct((8, SSM_H), F32)],
        scratch=[pltpu.VMEM((8, SSM_N, 128), F32), pltpu.VMEM((L + 8, CONV_C), F32), pltpu.VMEM((L, 128), F32)],
        args=[dmix, xbc, xconv, z, dtr, y, hs, convw, dtb, alx, dskx, ssmn, e, e1], sem=("arbitrary",), duties=duties)


def _adam_math(w, g, m, v):
    m = ADAM_B1 * m + (1.0 - ADAM_B1) * g
    v = ADAM_B2 * v + (1.0 - ADAM_B2) * (g * g)
    m_hat = m / (1.0 - ADAM_B1 ** ADAM_STEP)
    v_hat = v / (1.0 - ADAM_B2 ** ADAM_STEP)
    delta = -ADAM_LR * (m_hat / (jnp.sqrt(v_hat) + ADAM_EPS) + ADAM_WD * w)
    return delta, m, v


PACK_W = CONV_C


def _adamw_small(ws, ms, vs, total, conv_g, after=()):
    k = len(ws)

    def body(*refs):
        ins, outs = refs[:3 * k + 2], refs[3 * k + 2 + len(after):]
        total_ref, conv_ref = ins[3 * k], ins[3 * k + 1]
        for a in range(k):
            w_ref, m_ref, v_ref = ins[3 * a:3 * a + 3]
            g = conv_ref[...] if a == k - 1 else total_ref[a:a + 1, 0:w_ref.shape[1]]
            delta, nm, nv = _adam_math(w_ref[...], g, m_ref[...], v_ref[...])
            for ref, val in zip(outs[4 * a:4 * a + 4], (g, delta, nm, nv)):
                ref[...] = val

    vm = pl.BlockSpec(memory_space=pltpu.VMEM)
    args = [x for a in range(k) for x in (ws[a], ms[a], vs[a])] + [total, conv_g]
    res = pl.pallas_call(
        body, name="adamw_small", in_specs=[vm] * len(args) + [ANY] * len(after), out_specs=[vm] * (4 * k),
        out_shape=[jax.ShapeDtypeStruct(ws[a].shape, F32) for a in range(k) for _ in range(4)],
    )(*args, *after)
    return [tuple(res[4 * a:4 * a + 4]) for a in range(k)]


COL_TILE = 512


def _adamw_sharded(ws, ms, vs, chip_sums, from_chips, other_chips, name, after=()):
    k = len(ws)
    rows, cols = ws[0].shape
    prow = chip_sums[0].shape[0]
    assert cols % COL_TILE == 0 and prow >= rows and all(a.shape == ws[0].shape for a in ws)

    def body(ids_ref, *refs):
        ins, outs = refs[:7 * k], refs[7 * k + len(after):]
        for a in range(k):
            w_ref, m_ref, v_ref, s_ref, r1_ref, r2_ref, r3_ref = ins[7 * a:7 * a + 7]
            g = s_ref[...]
            for r in (r1_ref, r2_ref, r3_ref):
                g = g + r[0].astype(F32)
            g = g[:rows]
            delta, nm, nv = _adam_math(w_ref[...], g, m_ref[...], v_ref[...])
            for ref, val in zip(outs[4 * a:4 * a + 4], (g, delta, nm, nv)):
                ref[...] = val

    spec = pl.BlockSpec((rows, COL_TILE), lambda i, ids: (0, i))
    part = lambda j: pl.BlockSpec((1, prow, COL_TILE), lambda i, ids: (ids[j], 0, i))
    one = [spec, spec, spec, pl.BlockSpec((prow, COL_TILE), lambda i, ids: (0, i)), part(0), part(1), part(2)]
    args = [x for a in range(k) for x in (ws[a], ms[a], vs[a], chip_sums[a], from_chips[a], from_chips[a], from_chips[a])]
    res = pl.pallas_call(
        body, name=name,
        grid_spec=pltpu.PrefetchScalarGridSpec(
            num_scalar_prefetch=1, grid=(cols // COL_TILE,),
            in_specs=one * k + [ANY] * len(after), out_specs=[spec] * (4 * k)),
        out_shape=[jax.ShapeDtypeStruct((rows, cols), F32)] * (4 * k),
        compiler_params=_params(("parallel",)),
    )(other_chips, *args, *after)
    return [tuple(res[4 * a:4 * a + 4]) for a in range(k)]


def _chip_sum(mines, recvs, name):
    k = len(mines)
    rows, cols = mines[0].shape[1:]

    def body(*refs):
        own_chip = pl.program_id(0) == 2 * lax.axis_index("x") + lax.axis_index("y")
        for a in range(k):
            a_ref, b_ref = refs[2 * a:2 * a + 2]
            s_ref, sb_ref = refs[2 * k + 2 * a:2 * k + 2 * a + 2]
            s = a_ref[0] + b_ref[0].astype(F32)
            sb_ref[0] = s.astype(BF16)

            @pl.when(own_chip)
            def _(s_ref=s_ref, s=s):
                s_ref[...] = s

    by_chip = pl.BlockSpec((1, rows, cols), lambda c: (c, 0, 0))
    res = pl.pallas_call(
        body, name=name, grid=(N_DEV // 2,),
        in_specs=[by_chip, by_chip] * k, out_specs=[_const((rows, cols)), by_chip] * k,
        out_shape=[jax.ShapeDtypeStruct((rows, cols), F32), jax.ShapeDtypeStruct((N_DEV // 2, rows, cols), BF16)] * k,
        compiler_params=_params(("arbitrary",)),
    )(*[x for pair in zip(mines, recvs) for x in pair])
    return [tuple(res[2 * a:2 * a + 2]) for a in range(k)]


def _all_reduce_small(v, after=()):
    rows = v.shape[0]

    def body(v_ref, *rest):
        out_ref, gath, send_sems, recv_sems = rest[len(after):]
        x, y, c = _place()
        me, sibling = (x, y, c), (x, y, 1 - c)
        chips = [(1 - x, y), (x, 1 - y), (1 - x, 1 - y)]

        def blk(px, py, pc):
            return gath.at[pl.ds((4 * px + 2 * py + pc) * rows, rows), :]

        def copy(k, block, to, src=None):
            return pltpu.make_async_remote_copy(src_ref=blk(*block) if src is None else src, dst_ref=blk(*block),
                                                send_sem=send_sems.at[k], recv_sem=recv_sems.at[k], device_id=to, device_id_type=MESH)

        gath[pl.ds((4 * x + 2 * y + c) * rows, rows), :] = v_ref[...]
        first = [copy(0, me, sibling, src=v_ref)] + [copy(1 + j, me, (*chip, c), src=v_ref) for j, chip in enumerate(chips)]
        for cp in first:
            cp.start()
        passed = [copy(4 + j, (*chip, c), sibling) for j, chip in enumerate(chips)]
        for j, chip in enumerate(chips):
            copy(1 + j, (*chip, c), me).wait_recv()
            passed[j].start()
        copy(0, sibling, me).wait_recv()
        for j, chip in enumerate(chips):
            copy(4 + j, (*chip, 1 - c), me).wait_recv()
        for cp in first + passed:
            cp.wait_send()
        acc = gath[0:rows, :]
        for d in range(1, N_DEV):
            acc = acc + gath[d * rows:(d + 1) * rows, :]
        out_ref[...] = acc

    vm = pl.BlockSpec(memory_space=pltpu.VMEM)
    return pl.pallas_call(
        body, name="all_reduce_small",
        in_specs=[vm] + [ANY] * len(after), out_specs=vm,
        out_shape=jax.ShapeDtypeStruct(v.shape, F32),
        scratch_shapes=[pltpu.VMEM((N_DEV * rows, v.shape[1]), F32), pltpu.SemaphoreType.DMA((7,)), pltpu.SemaphoreType.DMA((7,))],
    )(v, *after)


def _rope_tables(positions):
    inv_freq = ROPE_THETA ** (-jnp.arange(0, HD, 2, dtype=F32) / HD)
    ang = positions.reshape(T).astype(F32)[:, None] * inv_freq
    ang = jnp.concatenate([ang, ang, ang, ang], axis=-1)
    lo_half = (jnp.arange(128) % HD) < (HD // 2)
    return jnp.cos(ang), jnp.where(lo_half, -jnp.sin(ang), jnp.sin(ang))


def _selectors():
    lane = jnp.arange(QW)
    e = (lane[None, :] // HD == jnp.arange(SSM_H)[:, None]).astype(F32)
    e1 = ((lane[None, :] == HD * jnp.arange(128)[:, None]) & (jnp.arange(128)[:, None] < SSM_H)).astype(F32)
    src = jnp.arange(KVW)
    ex = ((lane[None, :] // (HD * (NQ // NKV)) == src[:, None] // HD) & (lane[None, :] % HD == src[:, None] % HD)).astype(F32)
    return e, e1, ex


WEIGHTS = ['ffn1_pre_norm', 'ffn1_w_gate', 'ffn1_w_up', 'ffn1_w_down', 'ffn1_post_norm', 'mix_pre_norm', 'w_in', 'conv_w', 'conv_b',
           'dt_bias', 'a_log', 'd_skip', 'ssm_norm', 'w_out', 'mix_post_norm', 'ffn2_pre_norm', 'ffn2_w_gate', 'ffn2_w_up',
           'ffn2_w_down', 'ffn2_post_norm']
COL_SHARDED = ['ffn1_w_gate', 'ffn1_w_up', 'ffn2_w_gate', 'ffn2_w_up', 'w_in']
ROW_SHARDED = ['ffn1_w_down', 'ffn2_w_down', 'w_out']
BIG = COL_SHARDED + ROW_SHARDED
FFN_BIG = COL_SHARDED[:4] + ROW_SHARDED[:2]
SMALL = ['ffn1_pre_norm', 'ffn1_post_norm', 'mix_pre_norm', 'conv_b', 'dt_bias', 'a_log', 'd_skip', 'ssm_norm', 'mix_post_norm',
         'ffn2_pre_norm', 'ffn2_post_norm']
FFN1 = ['ffn1_w_gate', 'ffn1_w_up', 'ffn1_w_down']
FFN2 = ['ffn2_w_gate', 'ffn2_w_up', 'ffn2_w_down']


def _wire_block(name, a):
    if name in FFN_BIG:
        return jnp.pad(a.astype(BF16), ((0, FSH - FSR), (0, 0)))
    if name == "w_in":
        return jnp.pad(a.astype(BF16), ((0, ISW - ISR), (0, 0)))
    return a if name == "conv_w" else a.astype(BF16)


def _whole_from_gathered(name, a):
    if name == "conv_w":
        return jnp.transpose(a, (1, 0, 2)).reshape(a.shape[1], -1)
    return a.reshape(-1, D)


def _step(x, positions, target, small, blocks=None, whole=None):
    dist = blocks is not None
    core = "mesh" if dist else 0
    w = dict(small)
    if whole:
        w.update(whole)

    def gather(names):
        return [_gather_duty([_wire_block(n, blocks[n]) for n in names])] if dist else []

    def put(names, results):
        if dist:
            for n, r in zip(names, results[0]):
                w[n] = _whole_from_gathered(n, r)

    g, sums, red = {}, {}, {}

    def swap(names):
        return [_swap_duty([g[n][1] for n in names])] if dist else []

    def chip_sums(names, from_sibling):
        if dist:
            res = _chip_sum([g[n][0] for n in names], list(from_sibling), "chip_sum_" + names[0])
            sums.update(zip(names, res))

    def exchange(names):
        return [_exchange_duty([sums[n][1] for n in names])] if dist else []

    def reduced(names, from_chips):
        if dist:
            for n, recv in zip(names, from_chips):
                red[n] = (sums[n][0], recv)

    cos, sin_s = _rope_tables(positions)
    e, e1, exf = _selectors()
    bias = _attn_bias()
    alx = jnp.repeat(w["a_log"], HD, axis=1)
    dskx = jnp.repeat(w["d_skip"], HD, axis=1)

    if dist:
        put(FFN1, _comm_only(gather(FFN1), "gather_ffn1"))
    (x1, n1, a1, b1, hm1, h1), got = _ffn_fwd(x, w["ffn1_pre_norm"], w["ffn1_w_gate"], w["ffn1_w_up"], w["ffn1_w_down"],
                                              w["ffn1_post_norm"], "ffn1_fwd", gather(["w_in", "conv_w"]))
    put(["w_in", "conv_w"], got)
    (n2, q, kx, vx, xbc, z, dtr), got = _inproj_fwd(x1, w["mix_pre_norm"], w["w_in"], cos, sin_s, exf.astype(BF16), gather(["w_out"]))
    put(["w_out"], got)
    (attn, lse), got = _attn_fwd(q, kx, vx, bias, gather(FFN2[:2]))
    put(FFN2[:2], got)
    (yn, y, hs, xconv), got = _ssd_fwd(xbc, z, dtr, w["conv_w"], w["conv_b"], w["dt_bias"], alx, dskx, w["ssm_norm"], e, gather(FFN2[2:]))
    put(FFN2[2:], got)
    x2, h2 = _outproj_fwd(x1, attn, yn, w["w_out"], w["mix_post_norm"])
    (dx3, n3, a3, b3, hm3, h3, ss), _ = _ffn_fwd(x2, w["ffn2_pre_norm"], w["ffn2_w_gate"], w["ffn2_w_up"], w["ffn2_w_down"],
                                                 w["ffn2_post_norm"], "ffn2_fwd", target=target)

    (dx2, da3, db3, dh3, g["ffn2_pre_norm"], g["ffn2_post_norm"]), _ = _ffn_bwd(
        dx3, x2, a3, b3, h3, w["ffn2_pre_norm"], w["ffn2_post_norm"], w["ffn2_w_gate"], w["ffn2_w_up"], w["ffn2_w_down"], "ffn2_bwd")
    g["ffn2_w_down"] = _matmul_tn(hm3, dh3, "ffn2_dwd", core)[0]
    g["ffn2_w_gate"] = _matmul_tn(da3, n3, "ffn2_dwg", core)[0]
    g["ffn2_w_up"] = _matmul_tn(db3, n3, "ffn2_dwu", core)[0]

    (dh2, dmix, g["mix_post_norm"]), got = _outproj_bwd(dx2, h2, w["mix_post_norm"], w["w_out"], swap(FFN2))
    chip_sums(FFN2, got[0] if dist else None)
    g["w_out"] = _dwout(attn, yn, dh2, core)
    (dq, dkx, dvx), got = _attn_bwd(q, kx, vx, attn, dmix, lse, bias, exchange(FFN2) + swap(["w_out"]))
    if dist:
        reduced(FFN2, got[0])
        chip_sums(["w_out"], got[1])
    (dxbc, dz, ddt, dcw, g["conv_b"], g["ssm_norm"], dpar), got = _ssd_bwd(
        dmix, xbc, xconv, z, dtr, y, hs, w["conv_w"], w["dt_bias"], alx, dskx, w["ssm_norm"], e, e1, exchange(["w_out"]))
    reduced(["w_out"], got[0] if dist else None)
    g["conv_w"] = dcw[0:4]
    g["dt_bias"], g["a_log"], g["d_skip"] = dpar[0:1], dpar[1:2], dpar[2:3]
    dx1, dproj, g["mix_pre_norm"] = _inproj_bwd(dx2, dq, dkx, dvx, dxbc, dz, ddt, x1, w["mix_pre_norm"], w["w_in"], cos, sin_s, exf)
    g["w_in"] = _matmul_tn(dproj, n2, "dwin", core)[0]

    started = {}

    def start(n):
        started[n] = _exchange_start(sums[n][1], "start_exchange_" + n)
        return [started[n]["token"]]

    after = []
    if dist:
        chip_sums(["w_in"], _comm_only(swap(["w_in"]), "swap_w_in")[0])
        after = start("w_in")
    (dx0, da1, db1, dh1, g["ffn1_pre_norm"], g["ffn1_post_norm"]), _ = _ffn_bwd(
        dx1, x, a1, b1, h1, w["ffn1_pre_norm"], w["ffn1_post_norm"], w["ffn1_w_gate"], w["ffn1_w_up"], w["ffn1_w_down"], "ffn1_bwd",
        after)
    total = None
    if dist:
        widen = lambda a: jnp.pad(a, ((0, 0), (0, PACK_W - a.shape[1])))
        pack = jnp.concatenate([widen(g[n]) for n in SMALL] + [g["conv_w"], widen(ss[:, 0:1])])
        assert pack.shape[0] % 8 == 0
        total = _all_reduce_small(pack, after)
        after = [total]
    g["ffn1_w_down"], _ = _matmul_tn(hm1, dh1, "ffn1_dwd", core, after=after)
    g["ffn1_w_gate"], got = _matmul_tn(da1, n1, "ffn1_dwg", core, duties=swap(["ffn1_w_down"]))
    if dist:
        chip_sums(["ffn1_w_down"], got[0])
        after = start("ffn1_w_down")
    g["ffn1_w_up"], got = _matmul_tn(db1, n1, "ffn1_dwu", core, after=after, duties=swap(["ffn1_w_gate"]))
    if dist:
        chip_sums(["ffn1_w_gate"], got[0])
        after = start("ffn1_w_gate")
        chip_sums(["ffn1_w_up"], _comm_only(swap(["ffn1_w_up"]), "swap_ffn1_w_up", after=after)[0])
        start("ffn1_w_up")
    return ss, dx0, g, red, {n: (sums[n][0], started[n]) for n in started}, total


def kernel(x, positions, ffn1_pre_norm, ffn1_w_gate, ffn1_w_up, ffn1_w_down, ffn1_post_norm, mix_pre_norm, w_in, conv_w, conv_b, dt_bias, a_log, d_skip, ssm_norm, w_out, mix_post_norm, ffn2_pre_norm, ffn2_w_gate, ffn2_w_up, ffn2_w_down, ffn2_post_norm, loss_target, m_ffn1_pre_norm, m_ffn1_w_gate, m_ffn1_w_up, m_ffn1_w_down, m_ffn1_post_norm, m_mix_pre_norm, m_w_in, m_conv_w, m_conv_b, m_dt_bias, m_a_log, m_d_skip, m_ssm_norm, m_w_out, m_mix_post_norm, m_ffn2_pre_norm, m_ffn2_w_gate, m_ffn2_w_up, m_ffn2_w_down, m_ffn2_post_norm, v_ffn1_pre_norm, v_ffn1_w_gate, v_ffn1_w_up, v_ffn1_w_down, v_ffn1_post_norm, v_mix_pre_norm, v_w_in, v_conv_w, v_conv_b, v_dt_bias, v_a_log, v_d_skip, v_ssm_norm, v_w_out, v_mix_post_norm, v_ffn2_pre_norm, v_ffn2_w_gate, v_ffn2_w_up, v_ffn2_w_down, v_ffn2_post_norm):
    given = dict(locals())
    drop = lambda n, a: a if n in SMALL else (a[0].T if n in COL_SHARDED else a[0])
    w = {n: drop(n, given[n]) for n in WEIGHTS}
    m = {n: drop(n, given["m_" + n]) for n in WEIGHTS}
    v = {n: drop(n, given["v_" + n]) for n in WEIGHTS}
    cx, cy, cc = _place()
    others = [2 * (1 - cx) + cy, 2 * cx + (1 - cy), 2 * (1 - cx) + (1 - cy)]

    _, grad_x, g, red, pending, total = _step(x[0], positions, loss_target[0], {n: w[n] for n in SMALL},
                                              blocks={n: w[n] for n in BIG + ["conv_w"]})
    chip_ids = jnp.stack(others).astype(jnp.int32)
    out_g, out_d, out_m, out_v = {}, {}, {}, {}

    def update(names, sums, recvs, label, after=()):
        res = _adamw_sharded([w[n] for n in names], [m[n] for n in names], [v[n] for n in names], sums, recvs, chip_ids,
                             "adamw_" + label, after)
        for n, (gn, dn, mn, vn) in zip(names, res):
            out_g[n], out_d[n], out_m[n], out_v[n] = gn, dn, mn, vn

    last_start = [pending["ffn1_w_up"][1]["token"]]
    update(FFN2, [red[n][0] for n in FFN2], [red[n][1] for n in FFN2], "ffn2", last_start)
    update(["w_out"], [red["w_out"][0]], [red["w_out"][1]], "w_out", last_start)

    n_small = len(SMALL)
    conv_g = lax.dynamic_slice_in_dim(total[n_small:n_small + 4], (4 * cx + 2 * cy + cc) * (CONV_C // N_DEV), CONV_C // N_DEV, axis=1)
    loss = 0.5 * total[n_small + 4, 0] / D
    names = SMALL + ["conv_w"]
    res = _adamw_small([w[n] for n in names], [m[n] for n in names], [v[n] for n in names], total, conv_g, last_start)
    for n, (gn, dn, mn, vn) in zip(names, res):
        out_g[n], out_d[n], out_m[n], out_v[n] = gn, dn, mn, vn
    done = [out_v[n] for n in FFN2 + ["w_out", "conv_w"]]
    update(["w_in"], [pending["w_in"][0]], [_exchange_wait(pending["w_in"][1], done, "wait_exchange_w_in")], "w_in")
    done = [out_v["w_in"]]
    update(FFN1, [pending[n][0] for n in FFN1], [_exchange_wait(pending[n][1], done, "wait_exchange_" + n) for n in FFN1], "ffn1")

    outs = [loss, grad_x[None]]
    for d in (out_g, out_d, out_m, out_v):
        outs += [d[n] if n in SMALL else (d[n].T[None] if n in COL_SHARDED else d[n][None]) for n in WEIGHTS]
    return tuple(outs)
```

```python
import functools

import jax
import jax.numpy as jnp
from jax import lax
from jax.experimental import pallas as pl
from jax.experimental.pallas import tpu as pltpu

F32 = jnp.float32
BF16 = jnp.bfloat16
MESH = pl.DeviceIdType.MESH

N_DEV = 8
T = 2048
D = 1024
FF = 2816
FSR = FF // N_DEV
FSH = 384
FFP = N_DEV * FSH
HD = 64
NQ = 16
NKV = 4
QW = NQ * HD
KVW = NKV * HD
SSM_W = 1024
SSM_H = 16
SSM_N = 128
CONV_C = SSM_W + 2 * 2 * SSM_N
IN_COLS = 4112
INP = 4224
ISR = IN_COLS // N_DEV
ISW = 528
ISG = 640
L = 128
NCH = T // L
AB = 256
NAB = T // AB
EPS = 1e-6
NEG = -1e30
ROPE_THETA = 10000.0
DILATIONS = ((128, 1), (512, 4), (2048, 16))

ADAM_LR = 0.001
ADAM_B1 = 0.9
ADAM_B2 = 0.999
ADAM_EPS = 1e-08
ADAM_WD = 0.01
ADAM_STEP = 10

VMEM_LIMIT = 58 * 1024 * 1024


def _params(sem, vmem=VMEM_LIMIT):
    return pltpu.CompilerParams(dimension_semantics=sem, vmem_limit_bytes=vmem)


def _dot(a, b):
    return jnp.dot(a, b, preferred_element_type=F32)


def _dot_nt(a, b):
    return lax.dot_general(a, b, (((1,), (1,)), ((), ())), preferred_element_type=F32)


def _dot_tn(a, b):
    return lax.dot_general(a, b, (((0,), (0,)), ((), ())), preferred_element_type=F32)


def _split3(x):
    hi = x.astype(BF16)
    r1 = x - hi.astype(F32)
    mid = r1.astype(BF16)
    lo = (r1 - mid.astype(F32)).astype(BF16)
    return hi, mid, lo


def _dot_hi(a, b, a_is_01=False):
    if a_is_01:
        sel = a.astype(BF16)
        return sum(_dot(sel, p) for p in _split3(b))
    sel = b.astype(BF16)
    return sum(_dot(p, sel) for p in _split3(a))


def _dot_nt_hi(a, b):
    sel = b.astype(BF16)
    return sum(_dot_nt(p, sel) for p in _split3(a))


def _rs(x):
    return lax.rsqrt(jnp.mean(x * x, axis=-1, keepdims=True) + EPS)


def _sigmoid(x):
    return jax.nn.sigmoid(x)


def _dsilu(x, s):
    return s * (1.0 + x * (1.0 - s))


def _resident(shape):
    nd = len(shape)
    return pl.BlockSpec(shape, lambda *_: (0,) * nd, pipeline_mode=pl.Buffered(1))


def _const(shape):
    nd = len(shape)
    return pl.BlockSpec(shape, lambda *_: (0,) * nd)


def _rows(tm, cols):
    return pl.BlockSpec((tm, cols), lambda i: (i, 0))


ANY = pl.BlockSpec(memory_space=pl.ANY)


def _place():
    return lax.axis_index("x"), lax.axis_index("y"), lax.axis_index("c")


def _gather_duty(arrays):
    n = len(arrays)
    results = [jax.ShapeDtypeStruct((N_DEV,) + a.shape, a.dtype) for a in arrays]

    def make(ins, outs, send_sems, recv_sems, local_sems):
        x, y, c = _place()
        me, sibling = (x, y, c), (x, y, 1 - c)
        chips = [(1 - x, y), (x, 1 - y), (1 - x, 1 - y)]

        def place_of(a, px, py, pc):
            return outs[a].at[4 * px + 2 * py + pc]

        def copy(a, k, block, to, src=None):
            dst = place_of(a, *block)
            return pltpu.make_async_remote_copy(src_ref=dst if src is None else src, dst_ref=dst,
                                                send_sem=send_sems.at[7 * a + k], recv_sem=recv_sems.at[7 * a + k],
                                                device_id=to, device_id_type=MESH)

        def own(a):
            return pltpu.make_async_copy(ins[a], place_of(a, *me), local_sems.at[a])

        def first(a):
            return [copy(a, 0, me, sibling, src=ins[a])] + [copy(a, 1 + j, me, (*chip, c), src=ins[a]) for j, chip in enumerate(chips)]

        def start():
            for a in range(n):
                own(a).start()
            for a in range(n):
                for cp in first(a):
                    cp.start()

        def finish():
            for j, chip in enumerate(chips):
                for a in range(n):
                    copy(a, 1 + j, (*chip, c), me).wait_recv()
                    copy(a, 4 + j, (*chip, c), sibling).start()
            for a in range(n):
                copy(a, 0, sibling, me).wait_recv()
                for j, chip in enumerate(chips):
                    copy(a, 4 + j, (*chip, 1 - c), me).wait_recv()
            for a in range(n):
                for cp in first(a) + [copy(a, 4 + j, (*chip, c), sibling) for j, chip in enumerate(chips)]:
                    cp.wait_send()
                own(a).wait()

        return start, finish

    return dict(operands=list(arrays), results=results, sems=(7 * n, 7 * n, n), make=make)


def _swap_duty(arrays):
    n = len(arrays)
    half = N_DEV // 2
    results = [jax.ShapeDtypeStruct(a.shape, a.dtype) for a in arrays]

    def make(ins, outs, send_sems, recv_sems):
        x, y, c = _place()

        def copies():
            return [pltpu.make_async_remote_copy(src_ref=ins[a].at[k], dst_ref=outs[a].at[k],
                                                 send_sem=send_sems.at[half * a + k], recv_sem=recv_sems.at[half * a + k],
                                                 device_id=(x, y, 1 - c), device_id_type=MESH)
                    for a in range(n) for k in range(half)]

        def start():
            for cp in copies():
                cp.start()

        def finish():
            for cp in copies():
                cp.wait()

        return start, finish

    return dict(operands=list(arrays), results=results, sems=(half * n, half * n), make=make)


def _exchange_duty(arrays):
    n = len(arrays)
    results = [jax.ShapeDtypeStruct(a.shape, a.dtype) for a in arrays]

    def make(ins, outs, send_sems, recv_sems):
        x, y, c = _place()
        chips = [(1 - x, y), (x, 1 - y), (1 - x, 1 - y)]
        my_chip = 2 * x + y

        def sends():
            return [pltpu.make_async_remote_copy(src_ref=ins[a].at[2 * px + py], dst_ref=outs[a].at[my_chip],
                                                 send_sem=send_sems.at[3 * a + j], recv_sem=recv_sems.at[3 * a + j],
                                                 device_id=(px, py, c), device_id_type=MESH)
                    for a in range(n) for j, (px, py) in enumerate(chips)]

        def start():
            for cp in sends():
                cp.start()

        def finish():
            for a in range(n):
                for j, (px, py) in enumerate(chips):
                    pltpu.make_async_remote_copy(src_ref=ins[a].at[my_chip], dst_ref=outs[a].at[2 * px + py],
                                                 send_sem=send_sems.at[3 * a + j], recv_sem=recv_sems.at[3 * a + j],
                                                 device_id=(px, py, c), device_id_type=MESH).wait_recv()
            for cp in sends():
                cp.wait_send()

        return start, finish

    return dict(operands=list(arrays), results=results, sems=(3 * n, 3 * n), make=make)


def _call(body, *, name, grid, in_specs, out_specs, out_shape, args, sem, scratch=(), duties=(), after=()):
    n_in, n_out, n_scr = len(in_specs), len(out_specs), len(scratch)
    sem_shapes = [pltpu.SemaphoreType.DMA((k,)) for d in duties for k in d["sems"]]

    def full(*refs):
        pos = [0]

        def take(k):
            pos[0] += k
            return refs[pos[0] - k:pos[0]]

        ins = take(n_in)
        d_ins = [take(len(d["operands"])) for d in duties]
        take(len(after))
        outs = take(n_out)
        d_outs = [take(len(d["results"])) for d in duties]
        scr = take(n_scr)
        d_sems = [take(len(d["sems"])) for d in duties]
        hooks = [d["make"](di, do, *ds) for d, di, do, ds in zip(duties, d_ins, d_outs, d_sems)]
        if grid and hooks:
            ids = [pl.program_id(k) for k in range(len(grid))]
            first = functools.reduce(jnp.logical_and, [i == 0 for i in ids])
            last = functools.reduce(jnp.logical_and, [i == g - 1 for i, g in zip(ids, grid)])

            @pl.when(first)
            def _():
                for start, _ in hooks:
                    start()

            body(*ins, *outs, *scr)

            @pl.when(last)
            def _():
                for _, finish in hooks:
                    finish()
        else:
            for start, _ in hooks:
                start()
            body(*ins, *outs, *scr)
            for _, finish in hooks:
                finish()

    d_args = [a for d in duties for a in d["operands"]]
    d_res = [r for d in duties for r in d["results"]]
    kwargs = dict(grid=grid) if grid else {}
    res = pl.pallas_call(
        full, name=name, in_specs=list(in_specs) + [ANY] * (len(d_args) + len(after)), out_specs=list(out_specs) + [ANY] * len(d_res),
        out_shape=list(out_shape) + d_res, scratch_shapes=list(scratch) + sem_shapes,
        compiler_params=_params(sem) if grid else None, **kwargs,
    )(*args, *d_args, *after)
    own, rest = list(res[:n_out]), list(res[n_out:])
    by_duty = []
    for d in duties:
        by_duty.append(rest[:len(d["results"])])
        rest = rest[len(d["results"]):]
    return own, by_duty


def _comm_only(duties, name, after=()):
    return _call(lambda: None, name=name, grid=None, in_specs=[], out_specs=[], out_shape=[], args=[], sem=None, duties=duties,
                 after=after)[1]


HBM = pl.BlockSpec(memory_space=pltpu.HBM)
SEMS = pl.BlockSpec(memory_space=pltpu.SEMAPHORE)
SIDE_EFFECT = pltpu.SideEffectType.DATAFLOW_SIDE_EFFECTING
N_OTHER_CHIPS = 3


def _chip_copies(src_ref, land_ref, sems):
    x, y, c = _place()
    chips = [(1 - x, y), (x, 1 - y), (1 - x, 1 - y)]
    return [pltpu.make_async_remote_copy(src_ref=src_ref.at[2 * px + py], dst_ref=land_ref.at[2 * x + y],
                                         send_sem=sems[j], recv_sem=sems[N_OTHER_CHIPS + j], device_id=(px, py, c), device_id_type=MESH)
            for j, (px, py) in enumerate(chips)]


def _exchange_start(pb, name):
    n_sem = 2 * N_OTHER_CHIPS

    def body(pb_ref, land_ref, *rest):
        for cp in _chip_copies(pb_ref, land_ref, rest[:n_sem]):
            cp.start()
        token = rest[n_sem + 2]
        token[...] = jnp.zeros_like(token)

    res = pl.pallas_call(
        body, name=name,
        out_shape=(pltpu.SemaphoreType.DMA(()),) * n_sem + (pltpu.HBM(pb.shape, pb.dtype), pltpu.HBM(pb.shape, pb.dtype),
                                                              jax.ShapeDtypeStruct((8, 128), F32)),
        in_specs=(HBM, HBM), out_specs=(SEMS,) * n_sem + (HBM, HBM, pl.BlockSpec(memory_space=pltpu.VMEM)),
        input_output_aliases={0: n_sem, 1: n_sem + 1},
        compiler_params=pltpu.CompilerParams(has_side_effects=SIDE_EFFECT),
    )(pltpu.with_memory_space_constraint(pb, pltpu.HBM), pltpu.with_memory_space_constraint(lax.empty(pb.shape, pb.dtype), pltpu.HBM))
    return dict(sems=res[:n_sem], src=res[n_sem], land=res[n_sem + 1], token=res[n_sem + 2])


def _exchange_wait(started, after, name):
    n_sem = 2 * N_OTHER_CHIPS

    def body(pb_ref, land_ref, *rest):
        for cp in _chip_copies(pb_ref, land_ref, rest[:n_sem]):
            cp.wait_send()
            cp.wait_recv()

    src, land = started["src"], started["land"]
    return pl.pallas_call(
        body, name=name, out_shape=(pltpu.HBM(src.shape, src.dtype), pltpu.HBM(land.shape, land.dtype)),
        in_specs=(HBM, HBM) + (SEMS,) * n_sem + (ANY,) * len(after), out_specs=(HBM, HBM), input_output_aliases={0: 0, 1: 1},
        compiler_params=pltpu.CompilerParams(has_side_effects=SIDE_EFFECT),
    )(src, land, *started["sems"], *after)[1]


def _ffn_fwd(x, gpre, wg, wu, wd, gpost, name, duties=(), target=None):
    tm = 256
    n_in = 6 if target is None else 7

    def body(*refs):
        x_ref, gpre_ref, wg_ref, wu_ref, wd_ref, gpost_ref = refs[:6]
        xo_ref, n_ref, a_ref, b_ref, hm_ref, h_ref = refs[n_in:n_in + 6]
        xv = x_ref[...]
        n = (xv * _rs(xv) * gpre_ref[...]).astype(BF16)
        a = _dot_nt(n, wg_ref[...])
        b = _dot_nt(n, wu_ref[...])
        hm = (a * _sigmoid(a) * b).astype(BF16)
        h = _dot(hm, wd_ref[...])
        xo = xv + 0.5 * (h * _rs(h) * gpost_ref[...])
        n_ref[...] = n
        a_ref[...] = a.astype(BF16)
        b_ref[...] = b.astype(BF16)
        hm_ref[...] = hm
        h_ref[...] = h
        if target is None:
            xo_ref[...] = xo
        else:
            ss_ref = refs[n_in + 6]

            @pl.when(pl.program_id(0) == 0)
            def _():
                ss_ref[...] = jnp.zeros_like(ss_ref)

            err = xo - refs[6][...]
            xo_ref[...] = err * (1.0 / D)
            ss_ref[...] += jnp.sum(jnp.sum(err * err, axis=1, keepdims=True), axis=0, keepdims=True)

    loss_in = [] if target is None else [_rows(tm, D)]
    loss_out = [] if target is None else [_const((1, 128))]
    loss_shape = [] if target is None else [jax.ShapeDtypeStruct((1, 128), F32)]
    return _call(
        body, name=name, grid=(T // tm,),
        in_specs=[_rows(tm, D), _const((1, D)), _resident((FFP, D)), _resident((FFP, D)), _resident((FFP, D)), _const((1, D))] + loss_in,
        out_specs=[_rows(tm, D), _rows(tm, D), _rows(tm, FFP), _rows(tm, FFP), _rows(tm, FFP), _rows(tm, D)] + loss_out,
        out_shape=[jax.ShapeDtypeStruct((T, D), F32), jax.ShapeDtypeStruct((T, D), BF16), jax.ShapeDtypeStruct((T, FFP), BF16),
                   jax.ShapeDtypeStruct((T, FFP), BF16), jax.ShapeDtypeStruct((T, FFP), BF16), jax.ShapeDtypeStruct((T, D), F32)]
        + loss_shape,
        args=[x, gpre, wg, wu, wd, gpost] + ([] if target is None else [target]), sem=("arbitrary",), duties=duties)


def _ffn_bwd(dxo, x, a, b, h, gpre, gpost, wg, wu, wd, name, after=()):
    tm = 256

    def body(dxo_ref, x_ref, a_ref, b_ref, h_ref, gpre_ref, gpost_ref, wg_ref, wu_ref, wd_ref,
             dx_ref, da_ref, db_ref, dh_ref, dgpre_ref, dgpost_ref):
        @pl.when(pl.program_id(0) == 0)
        def _():
            dgpre_ref[...] = jnp.zeros_like(dgpre_ref)
            dgpost_ref[...] = jnp.zeros_like(dgpost_ref)

        dy = dxo_ref[...]
        h = h_ref[...]
        hn = h * _rs(h)
        r2 = _rs(h)
        dgpost_ref[...] += jnp.sum(0.5 * dy * hn, axis=0, keepdims=True)
        gdy = 0.5 * dy * gpost_ref[...]
        dh = r2 * (gdy - hn * jnp.mean(gdy * hn, axis=-1, keepdims=True))
        dhb = dh.astype(BF16)
        dh_ref[...] = dhb
        dhm = _dot_nt(dhb, wd_ref[...])
        av = a_ref[...].astype(F32)
        bv = b_ref[...].astype(F32)
        sg = _sigmoid(av)
        db = (dhm * (av * sg)).astype(BF16)
        da = (dhm * bv * _dsilu(av, sg)).astype(BF16)
        da_ref[...] = da
        db_ref[...] = db
        dn = _dot(da, wg_ref[...]) + _dot(db, wu_ref[...])
        xv = x_ref[...]
        r = _rs(xv)
        xn = xv * r
        dgpre_ref[...] += jnp.sum(dn * xn, axis=0, keepdims=True)
        gdn = dn * gpre_ref[...]
        dx_ref[...] = dy + r * (gdn - xn * jnp.mean(gdn * xn, axis=-1, keepdims=True))

    return _call(
        body, name=name, grid=(T // tm,),
        in_specs=[_rows(tm, D), _rows(tm, D), _rows(tm, FFP), _rows(tm, FFP), _rows(tm, D), _const((1, D)), _const((1, D)),
                  _resident((FFP, D)), _resident((FFP, D)), _resident((FFP, D))],
        out_specs=[_rows(tm, D), _rows(tm, FFP), _rows(tm, FFP), _rows(tm, D), _const((1, D)), _const((1, D))],
        out_shape=[jax.ShapeDtypeStruct((T, D), F32), jax.ShapeDtypeStruct((T, FFP), BF16), jax.ShapeDtypeStruct((T, FFP), BF16),
                   jax.ShapeDtypeStruct((T, D), BF16), jax.ShapeDtypeStruct((1, D), F32), jax.ShapeDtypeStruct((1, D), F32)],
        args=[dxo, x, a, b, h, gpre, gpost, wg, wu, wd], sem=("arbitrary",), after=after)


def _core_index(core):
    return lax.axis_index("c") if core == "mesh" else core


def _by_core(res, o_ref, ob_ref, core):
    r = res.shape[0] // 2
    c = jnp.asarray(_core_index(core))

    @pl.when(c == 0)
    def _():
        o_ref[0] = res[:r]
        ob_ref[0] = res[r:].astype(BF16)

    @pl.when(c == 1)
    def _():
        o_ref[0] = res[r:]
        ob_ref[0] = res[:r].astype(BF16)


def _matmul_tn(a, b, name, core, after=(), duties=()):
    k, m = a.shape
    n = b.shape[1]
    r = m // N_DEV
    assert m == N_DEV * r and r % 128 == 0

    def body(a_ref, b_ref, o_ref, ob_ref):
        _by_core(_dot_tn(a_ref[...], b_ref[...]), o_ref, ob_ref, core)

    spec = pl.BlockSpec((1, r, n), lambda i: (i, 0, 0))
    return _call(body, name=name, grid=(N_DEV // 2,), in_specs=[pl.BlockSpec((k, 2 * r), lambda i: (0, i)), _resident((k, n))],
                 out_specs=[spec, spec],
                 out_shape=[jax.ShapeDtypeStruct((N_DEV // 2, r, n), F32), jax.ShapeDtypeStruct((N_DEV // 2, r, n), BF16)],
                 args=[a, b], sem=("arbitrary",), duties=duties, after=after)


def _dwout(attn, yn, dh2, core):
    rs = (QW + SSM_W) // N_DEV
    chips = N_DEV // 2

    def body(at_ref, yn_ref, dh_ref, o_ref, ob_ref):
        i = pl.program_id(0)

        @pl.when(i < chips // 2)
        def _():
            _by_core(_dot_tn(at_ref[...], dh_ref[...]), o_ref, ob_ref, core)

        @pl.when(i >= chips // 2)
        def _():
            _by_core(_dot_tn(yn_ref[...], dh_ref[...]), o_ref, ob_ref, core)

    spec = pl.BlockSpec((1, rs, D), lambda i: (i, 0, 0))
    return pl.pallas_call(
        body, name="dwout", grid=(chips,),
        in_specs=[pl.BlockSpec((T, 2 * rs), lambda i: (0, jnp.minimum(i, chips // 2 - 1))),
                  pl.BlockSpec((T, 2 * rs), lambda i: (0, jnp.maximum(i - chips // 2, 0))), _resident((T, D))],
        out_specs=[spec, spec],
        out_shape=[jax.ShapeDtypeStruct((chips, rs, D), F32), jax.ShapeDtypeStruct((chips, rs, D), BF16)],
        compiler_params=_params(("arbitrary",)),
    )(attn, yn, dh2)


def _rope_swap(t, lo_half):
    return jnp.where(lo_half, pltpu.roll(t, 96, 1), pltpu.roll(t, 32, 1))


def _inproj_fwd(x1, gpre, win, cos, sin_s, ex, duties=()):
    tm = 256

    def body(x_ref, g_ref, w_ref, cos_ref, sin_ref, ex_ref, n_ref, q_ref, kx_ref, vx_ref, xbc_ref, z_ref, dt_ref):
        xv = x_ref[...]
        n = (xv * _rs(xv) * g_ref[...]).astype(BF16)
        n_ref[...] = n
        by_dev = _dot_nt(n, w_ref[...])
        proj = jnp.concatenate([by_dev[:, ISW * d:ISW * d + ISR] for d in range(N_DEV)], axis=1)
        cs = cos_ref[...]
        sn = sin_ref[...]
        lo_half = (lax.broadcasted_iota(jnp.int32, (1, 128), 1) % HD) < (HD // 2)

        def rope(t):
            return t * cs + _rope_swap(t, lo_half) * sn

        for j in range(QW // 128):
            t = proj[:, 128 * j:128 * j + 128]
            q_ref[:, 128 * j:128 * j + 128] = (rope(t) * (HD ** -0.5)).astype(BF16)
        k = jnp.concatenate([rope(proj[:, QW + 128 * j:QW + 128 * j + 128]) for j in range(KVW // 128)], axis=1)
        v = proj[:, QW + KVW:QW + 2 * KVW]
        kx_ref[...] = _dot(k.astype(BF16), ex_ref[...]).astype(BF16)
        vx_ref[...] = _dot(v.astype(BF16), ex_ref[...]).astype(BF16)
        c0 = QW + 2 * KVW
        xbc_ref[...] = proj[:, c0:c0 + CONV_C]
        z_ref[...] = proj[:, c0 + CONV_C:c0 + CONV_C + SSM_W]
        dt_ref[...] = proj[:, c0 + CONV_C + SSM_W:IN_COLS]

    return _call(
        body, name="inproj_fwd", grid=(T // tm,),
        in_specs=[_rows(tm, D), _const((1, D)), _resident((INP, D)), _rows(tm, 128), _rows(tm, 128), _const((KVW, QW))],
        out_specs=[_rows(tm, D), _rows(tm, QW), _rows(tm, QW), _rows(tm, QW), _rows(tm, CONV_C), _rows(tm, SSM_W), _rows(tm, SSM_H)],
        out_shape=[jax.ShapeDtypeStruct((T, D), BF16), jax.ShapeDtypeStruct((T, QW), BF16), jax.ShapeDtypeStruct((T, QW), BF16),
                   jax.ShapeDtypeStruct((T, QW), BF16), jax.ShapeDtypeStruct((T, CONV_C), F32), jax.ShapeDtypeStruct((T, SSM_W), F32),
                   jax.ShapeDtypeStruct((T, SSM_H), F32)],
        args=[x1, gpre, win, cos, sin_s, ex], sem=("arbitrary",), duties=duties)


def _inproj_bwd(dres, dq, dkx, dvx, dxbc, dz, ddt, x1, gpre, win, cos, sin_s, exf):
    tm = 256

    def body(dres_ref, dq_ref, dkx_ref, dvx_ref, dxbc_ref, dz_ref, ddt_ref, x_ref, g_ref, w_ref, cos_ref, sin_ref, ex_ref,
             dx_ref, dps_ref, dg_ref, dp_ref):
        @pl.when(pl.program_id(0) == 0)
        def _():
            dg_ref[...] = jnp.zeros_like(dg_ref)

        cs = cos_ref[...]
        sn = sin_ref[...]
        lo_half = (lax.broadcasted_iota(jnp.int32, (1, 128), 1) % HD) < (HD // 2)

        def rope_t(t):
            return t * cs - _rope_swap(t, lo_half) * sn

        for j in range(QW // 128):
            dp_ref[:, 128 * j:128 * j + 128] = rope_t(dq_ref[:, 128 * j:128 * j + 128] * (HD ** -0.5)).astype(BF16)
        dk = _dot_nt_hi(dkx_ref[...], ex_ref[...])
        dv = _dot_nt_hi(dvx_ref[...], ex_ref[...])
        for j in range(KVW // 128):
            dp_ref[:, QW + 128 * j:QW + 128 * j + 128] = rope_t(dk[:, 128 * j:128 * j + 128]).astype(BF16)
        dp_ref[:, QW + KVW:QW + 2 * KVW] = dv.astype(BF16)
        c0 = QW + 2 * KVW
        dp_ref[:, c0:c0 + CONV_C] = dxbc_ref[...].astype(BF16)
        dp_ref[:, c0 + CONV_C:c0 + CONV_C + SSM_W] = dz_ref[...].astype(BF16)
        dp_ref[:, c0 + CONV_C + SSM_W:INP] = ddt_ref[...].astype(BF16)
        pieces = [dp_ref[:, ISR * d:ISR * (d + 1)] for d in range(N_DEV)]
        zw = jnp.zeros((tm, ISW - ISR), BF16)
        zg = jnp.zeros((tm, ISG - ISR), BF16)
        dn = _dot(jnp.concatenate([t for p in pieces for t in (p, zw)], axis=1), w_ref[...])
        for d in range(N_DEV):
            dps_ref[:, ISG * d:ISG * (d + 1)] = jnp.concatenate([pieces[d], zg], axis=1)
        xv = x_ref[...]
        r = _rs(xv)
        xn = xv * r
        dg_ref[...] += jnp.sum(dn * xn, axis=0, keepdims=True)
        gdn = dn * g_ref[...]
        dx_ref[...] = dres_ref[...] + r * (gdn - xn * jnp.mean(gdn * xn, axis=-1, keepdims=True))

    return pl.pallas_call(
        body, name="inproj_bwd", grid=(T // tm,),
        in_specs=[_rows(tm, D), _rows(tm, QW), _rows(tm, QW), _rows(tm, QW), _rows(tm, CONV_C), _rows(tm, SSM_W), _rows(tm, 128),
                  _rows(tm, D), _const((1, D)), _resident((INP, D)), _rows(tm, 128), _rows(tm, 128), _const((KVW, QW))],
        out_specs=[_rows(tm, D), _rows(tm, N_DEV * ISG), _const((1, D))],
        out_shape=[jax.ShapeDtypeStruct((T, D), F32), jax.ShapeDtypeStruct((T, N_DEV * ISG), BF16), jax.ShapeDtypeStruct((1, D), F32)],
        scratch_shapes=[pltpu.VMEM((tm, INP), BF16)],
        compiler_params=_params(("arbitrary",)),
    )(dres, dq, dkx, dvx, dxbc, dz, ddt, x1, gpre, win, cos, sin_s, exf)


def _outproj_fwd(x1, attn, yn, wout, gpost):
    tm = 512

    def body(x_ref, at_ref, yn_ref, w_ref, g_ref, xo_ref, h_ref):
        h = _dot(at_ref[...], w_ref[0:QW, :]) + _dot(yn_ref[...], w_ref[QW:QW + SSM_W, :])
        h_ref[...] = h
        xo_ref[...] = x_ref[...] + h * _rs(h) * g_ref[...]

    return pl.pallas_call(
        body, name="outproj_fwd", grid=(T // tm,),
        in_specs=[_rows(tm, D), _rows(tm, QW), _rows(tm, SSM_W), _resident((QW + SSM_W, D)), _const((1, D))],
        out_specs=[_rows(tm, D), _rows(tm, D)],
        out_shape=[jax.ShapeDtypeStruct((T, D), F32), jax.ShapeDtypeStruct((T, D), F32)],
        compiler_params=_params(("parallel",)),
    )(x1, attn, yn, wout, gpost)


def _outproj_bwd(dx2, h2, gpost, wout, duties=()):
    tm = 512

    def body(dy_ref, h_ref, g_ref, w_ref, dh_ref, dm_ref, dg_ref):
        @pl.when(pl.program_id(0) == 0)
        def _():
            dg_ref[...] = jnp.zeros_like(dg_ref)

        dy = dy_ref[...]
        h = h_ref[...]
        r = _rs(h)
        hn = h * r
        dg_ref[...] += jnp.sum(dy * hn, axis=0, keepdims=True)
        gdy = dy * g_ref[...]
        dh = (r * (gdy - hn * jnp.mean(gdy * hn, axis=-1, keepdims=True))).astype(BF16)
        dh_ref[...] = dh
        dm_ref[...] = _dot_nt(dh, w_ref[...])

    return _call(
        body, name="outproj_bwd", grid=(T // tm,),
        in_specs=[_rows(tm, D), _rows(tm, D), _const((1, D)), _resident((QW + SSM_W, D))],
        out_specs=[_rows(tm, D), _rows(tm, QW + SSM_W), _const((1, D))],
        out_shape=[jax.ShapeDtypeStruct((T, D), BF16), jax.ShapeDtypeStruct((T, QW + SSM_W), F32), jax.ShapeDtypeStruct((1, D), F32)],
        args=[dx2, h2, gpost, wout], sem=("arbitrary",), duties=duties)


def _attn_bias():
    d = jnp.arange(AB)[:, None] - jnp.arange(T)[None, :] + (T - AB)
    cnt = jnp.zeros(d.shape, F32)
    for window, dil in DILATIONS:
        cnt = cnt + ((d >= 0) & (d % dil == 0) & (d <= window)).astype(F32)
    return jnp.where(cnt > 0, jnp.log(jnp.maximum(cnt, 1.0)), NEG)


G_PER = NQ // NKV
WK = G_PER * HD


def _attn_fwd(q, kx, vx, bias, duties=()):
    def body(q_ref, kx_ref, vx_ref, bias_ref, o_ref, lse_ref):
        lane = lax.broadcasted_iota(jnp.int32, (1, WK), 1)
        lse_ref[...] = jnp.zeros_like(lse_ref)
        for i in range(NAB):
            n = (i + 1) * AB
            rows = slice(i * AB, n)
            qi = q_ref[rows, :]
            kxi = kx_ref[0:n, :]
            vxi = vx_ref[0:n, :]
            bb = bias_ref[:, (NAB - 1 - i) * AB:]
            o_acc = jnp.zeros((AB, WK), F32)
            for g in range(G_PER):
                mg = (lane // HD) == g
                s = _dot_nt(jnp.where(mg, qi, jnp.zeros_like(qi)), kxi) + bb
                m = jnp.max(s, axis=1, keepdims=True)
                p = jnp.exp(s - m)
                l = jnp.sum(p, axis=1, keepdims=True)
                o_acc = jnp.where(mg, _dot(p.astype(BF16), vxi) / l, o_acc)
                lse_ref[rows, g:g + 1] = m + jnp.log(l)
            o_ref[rows, :] = o_acc.astype(BF16)

    col = lambda kv: (0, kv)
    return _call(
        body, name="attn_fwd", grid=(NKV,),
        in_specs=[pl.BlockSpec((T, WK), col), pl.BlockSpec((T, WK), col), pl.BlockSpec((T, WK), col), _const((AB, T))],
        out_specs=[pl.BlockSpec((T, WK), col), pl.BlockSpec((T, 128), col)],
        out_shape=[jax.ShapeDtypeStruct((T, QW), BF16), jax.ShapeDtypeStruct((T, NKV * 128), F32)],
        args=[q, kx, vx, bias], sem=("arbitrary",), duties=duties)


def _attn_bwd(q, kx, vx, o, dmix, lse, bias, duties=()):
    def body(q_ref, kx_ref, vx_ref, o_ref, do_ref, lse_ref, bias_ref, dq_ref, dkx_ref, dvx_ref):
        lane = lax.broadcasted_iota(jnp.int32, (1, WK), 1)
        dkx_ref[...] = jnp.zeros_like(dkx_ref)
        dvx_ref[...] = jnp.zeros_like(dvx_ref)
        for i in range(NAB):
            n = (i + 1) * AB
            rows = slice(i * AB, n)
            qi = q_ref[rows, :]
            dof = do_ref[rows, :]
            doi = dof.astype(BF16)
            prod = dof * o_ref[rows, :].astype(F32)
            kxi = kx_ref[0:n, :]
            vxi = vx_ref[0:n, :]
            bb = bias_ref[:, (NAB - 1 - i) * AB:]
            dq_acc = jnp.zeros((AB, WK), F32)
            for g in range(G_PER):
                mg = (lane // HD) == g
                qm = jnp.where(mg, qi, jnp.zeros_like(qi))
                dom = jnp.where(mg, doi, jnp.zeros_like(doi))
                delta = jnp.sum(jnp.where(mg, prod, 0.0), axis=1, keepdims=True)
                p = jnp.exp(_dot_nt(qm, kxi) + bb - lse_ref[rows, g:g + 1])
                ds = (p * (_dot_nt(dom, vxi) - delta)).astype(BF16)
                dvx_ref[0:n, :] += _dot_tn(p.astype(BF16), dom)
                dkx_ref[0:n, :] += _dot_tn(ds, qm)
                dq_acc = jnp.where(mg, _dot(ds, kxi), dq_acc)
            dq_ref[rows, :] = dq_acc

    col = lambda kv: (0, kv)
    return _call(
        body, name="attn_bwd", grid=(NKV,),
        in_specs=[pl.BlockSpec((T, WK), col), pl.BlockSpec((T, WK), col), pl.BlockSpec((T, WK), col), pl.BlockSpec((T, WK), col),
                  pl.BlockSpec((T, WK), col), pl.BlockSpec((T, 128), col), _const((AB, T))],
        out_specs=[pl.BlockSpec((T, WK), col), pl.BlockSpec((T, WK), col), pl.BlockSpec((T, WK), col)],
        out_shape=[jax.ShapeDtypeStruct((T, QW), F32)] * 3,
        args=[q, kx, vx, o, dmix, lse, bias], sem=("arbitrary",), duties=duties)


def _softplus(x):
    return jnp.maximum(x, 0.0) + jnp.log1p(jnp.exp(-jnp.abs(x)))


def _causal_conv(u, zs, cw_ref, cb_ref):
    zs[8:, :] = u
    sh1, sh2, sh3 = (zs[8 - m:8 - m + L, :] for m in (1, 2, 3))
    return cb_ref[...] + cw_ref[3:4, :] * u + cw_ref[2:3, :] * sh1 + cw_ref[1:2, :] * sh2 + cw_ref[0:1, :] * sh3


def _ssd_chunk_common(xc, dtr, dtb_ref, alx_ref, e_ref):
    sg = _sigmoid(xc)
    act = xc * sg
    pre = dtr + dtb_ref[...]
    dt_x = _dot_hi(_softplus(pre), e_ref[...])
    a_x = -jnp.exp(alx_ref[...])
    ri = lax.broadcasted_iota(jnp.int32, (L, L), 0)
    ci = lax.broadcasted_iota(jnp.int32, (L, L), 1)
    tri = ri >= ci
    acs_x = _dot_hi(tri, dt_x * a_x, a_is_01=True)
    return dict(sg=sg, act=act, pre=pre, dt_x=dt_x, a_x=a_x, tri=tri, acs_x=acs_x)


def _decay(acs_x, acs_t, h, tri):
    col = acs_x[:, HD * h:HD * h + 1]
    row = acs_t[HD * h:HD * h + 1, :]
    return jnp.exp(jnp.where(tri, col - row, NEG))


def _ssd_fwd(xbc, z, dtr, convw, convb, dtb, alx, dskx, ssmn, e, duties=()):
    def body(u_ref, z_ref, dtr_ref, cw_ref, cb_ref, dtb_ref, alx_ref, dsk_ref, sn_ref, e_ref,
             yn_ref, y_ref, hs_ref, xc_ref, zs, hst):
        @pl.when(pl.program_id(0) == 0)
        def _():
            zs[0:8, :] = jnp.zeros((8, CONV_C), F32)
            hst[...] = jnp.zeros_like(hst)

        u = u_ref[...]
        xc = _causal_conv(u, zs, cw_ref, cb_ref)
        xc_ref[...] = xc
        zs[0:8, :] = u[L - 8:, :]
        cm = _ssd_chunk_common(xc, dtr_ref[...], dtb_ref, alx_ref, e_ref)
        act, dt_x, acs_x, tri = cm["act"], cm["dt_x"], cm["acs_x"], cm["tri"]
        xs = act[:, :SSM_W]
        acs_l = acs_x[L - 1:L, :]
        lam_x = jnp.exp(acs_x)
        w_x = jnp.exp(acs_l - acs_x)
        gam_x = jnp.exp(acs_l)
        acs_t = acs_x.T
        xd = xs * dt_x
        xb = xd.astype(BF16)
        xw = (xd * w_x).astype(BF16)
        lo = lax.broadcasted_iota(jnp.int32, (1, 128), 1) < HD
        hs_ref[0] = hst[...]
        pieces = []
        for grp in range(2):
            bb = act[:, SSM_W + SSM_N * grp:SSM_W + SSM_N * (grp + 1)].astype(BF16)
            cb_ = act[:, SSM_W + 2 * SSM_N + SSM_N * grp:SSM_W + 2 * SSM_N + SSM_N * (grp + 1)].astype(BF16)
            cbm = _dot_nt(cb_, bb)
            for jj in range(4):
                j = 4 * grp + jj
                sl = slice(128 * j, 128 * j + 128)
                m0 = (cbm * _decay(acs_x, acs_t, 2 * j, tri)).astype(BF16)
                m1 = (cbm * _decay(acs_x, acs_t, 2 * j + 1, tri)).astype(BF16)
                x2 = xb[:, sl]
                ydiag = jnp.where(lo, _dot(m0, x2), _dot(m1, x2))
                hprev = hst[j]
                yoff = lam_x[:, sl] * _dot(cb_, hprev.astype(BF16))
                pieces.append(ydiag + yoff)
                hst[j] = gam_x[:, sl] * hprev + _dot_tn(bb, xw[:, sl])
        y = jnp.concatenate(pieces, axis=1) + dsk_ref[...] * xs
        y_ref[...] = y
        zv = z_ref[...]
        yz = y * (zv * _sigmoid(zv))
        half = SSM_W // 2
        yn = jnp.concatenate([yz[:, :half] * _rs(yz[:, :half]), yz[:, half:] * _rs(yz[:, half:])], axis=1)
        yn_ref[...] = (yn * sn_ref[...]).astype(BF16)

    return _call(
        body, name="ssd_fwd", grid=(NCH,),
        in_specs=[_rows(L, CONV_C), _rows(L, SSM_W), _rows(L, SSM_H), _const((4, CONV_C)), _const((1, CONV_C)), _const((1, SSM_H)),
                  _const((1, SSM_W)), _const((1, SSM_W)), _const((1, SSM_W)), _const((SSM_H, SSM_W))],
        out_specs=[_rows(L, SSM_W), _rows(L, SSM_W), pl.BlockSpec((1, 8, SSM_N, 128), lambda c: (c, 0, 0, 0)), _rows(L, CONV_C)],
        out_shape=[jax.ShapeDtypeStruct((T, SSM_W), BF16), jax.ShapeDtypeStruct((T, SSM_W), F32),
                   jax.ShapeDtypeStruct((NCH, 8, SSM_N, 128), F32), jax.ShapeDtypeStruct((T, CONV_C), F32)],
        scratch=[pltpu.VMEM((8 + L, CONV_C), F32), pltpu.VMEM((8, SSM_N, 128), F32)],
        args=[xbc, z, dtr, convw, convb, dtb, alx, dskx, ssmn, e], sem=("arbitrary",), duties=duties)


def _ssd_bwd(dmix, xbc, xconv, z, dtr, y, hs, convw, dtb, alx, dskx, ssmn, e, e1, duties=()):
    rev = lambda i: (NCH - 1 - i, 0)

    def body(dyn_ref, u_ref, xc_ref, z_ref, dtr_ref, y_ref, hs_ref, cw_ref, dtb_ref, alx_ref, dsk_ref, sn_ref, e_ref, e1_ref,
             dxbc_ref, dz_ref, ddt_ref, dcw_ref, dcb_ref, dsn_ref, dpar_ref, dh, zd, colbuf):
        step = pl.program_id(0)

        @pl.when(step == 0)
        def _():
            for r in (dh, dcw_ref, dcb_ref, dsn_ref, dpar_ref):
                r[...] = jnp.zeros_like(r)
            zd[L:, :] = jnp.zeros((8, CONV_C), F32)

        u = u_ref[...]
        xc = xc_ref[...]
        cm = _ssd_chunk_common(xc, dtr_ref[...], dtb_ref, alx_ref, e_ref)
        sg, act, pre, dt_x, a_x, tri, acs_x = (cm[k] for k in ("sg", "act", "pre", "dt_x", "a_x", "tri", "acs_x"))
        xs = act[:, :SSM_W]
        acs_l = acs_x[L - 1:L, :]
        lam_x = jnp.exp(acs_x)
        w_x = jnp.exp(acs_l - acs_x)
        gam_x = jnp.exp(acs_l)
        acs_t = acs_x.T
        xd = xs * dt_x
        xb = xd.astype(BF16)
        xdw = xd * w_x
        xw = xdw.astype(BF16)
        lo = lax.broadcasted_iota(jnp.int32, (1, 128), 1) < HD
        row8 = lax.broadcasted_iota(jnp.int32, (8, 1), 0)

        dyn = dyn_ref[...]
        yv = y_ref[...]
        zv = z_ref[...]
        sz = _sigmoid(zv)
        siluz = zv * sz
        yz = yv * siluz
        half = SSM_W // 2
        gy = dyn * sn_ref[...]
        dyz_parts, yzn_parts = [], []
        for hf in range(2):
            part = yz[:, hf * half:(hf + 1) * half]
            r = _rs(part)
            pn = part * r
            gp = gy[:, hf * half:(hf + 1) * half]
            dyz_parts.append(r * (gp - pn * jnp.mean(gp * pn, axis=-1, keepdims=True)))
            yzn_parts.append(pn)
        dyz = jnp.concatenate(dyz_parts, axis=1)
        dsn_ref[...] += jnp.sum(dyn * jnp.concatenate(yzn_parts, axis=1), axis=0, keepdims=True)
        dy = dyz * siluz
        dz_ref[...] = dyz * yv * _dsilu(zv, sz)

        colbuf[...] = jnp.zeros_like(colbuf)
        dx_pieces, dacs_pieces, dacsl_pieces, db_pieces, dc_pieces = [], [], [], [], []
        for grp in range(2):
            bb = act[:, SSM_W + SSM_N * grp:SSM_W + SSM_N * (grp + 1)].astype(BF16)
            cb_ = act[:, SSM_W + 2 * SSM_N + SSM_N * grp:SSM_W + 2 * SSM_N + SSM_N * (grp + 1)].astype(BF16)
            cbm = _dot_nt(cb_, bb)
            dcbm = jnp.zeros((L, L), F32)
            dc_g = jnp.zeros((L, SSM_N), F32)
            db_g = jnp.zeros((L, SSM_N), F32)
            for jj in range(4):
                j = 4 * grp + jj
                sl = slice(128 * j, 128 * j + 128)
                dy2 = dy[:, sl]
                dy2b = dy2.astype(BF16)
                d0 = _decay(acs_x, acs_t, 2 * j, tri)
                d1 = _decay(acs_x, acs_t, 2 * j + 1, tri)
                m0 = cbm * d0
                m1 = cbm * d1
                x2 = xb[:, sl]
                hprev = hs_ref[0, j]
                hprevb = hprev.astype(BF16)
                dhn = dh[j]
                dhnb = dhn.astype(BF16)
                g2 = _dot(bb, dhnb)
                dx_pieces.append(jnp.where(lo, _dot_tn(m0.astype(BF16), dy2b), _dot_tn(m1.astype(BF16), dy2b)) + w_x[:, sl] * g2)
                zero = jnp.zeros_like(dy2b)
                dm0 = _dot_nt(jnp.where(lo, dy2b, zero), x2)
                dm1 = _dot_nt(jnp.where(lo, zero, dy2b), x2)
                dcbm = dcbm + dm0 * d0 + dm1 * d1
                e0 = dm0 * m0
                e1v = dm1 * m1
                colbuf[:, 2 * j:2 * j + 1] = jnp.sum(e0, axis=1, keepdims=True) - jnp.sum(e0.T, axis=1, keepdims=True)
                colbuf[:, 2 * j + 1:2 * j + 2] = jnp.sum(e1v, axis=1, keepdims=True) - jnp.sum(e1v.T, axis=1, keepdims=True)
                yoff = lam_x[:, sl] * _dot(cb_, hprevb)
                gxw = g2 * xdw[:, sl]
                dacs_pieces.append(dy2 * yoff - gxw)
                dacsl_pieces.append(jnp.sum(gxw, axis=0, keepdims=True) + gam_x[:, sl] * jnp.sum(dhn * hprev, axis=0, keepdims=True))
                dyl = (dy2 * lam_x[:, sl]).astype(BF16)
                dc_g = dc_g + _dot_nt(dyl, hprevb)
                db_g = db_g + _dot_nt(xw[:, sl], dhnb)
                dh[j] = gam_x[:, sl] * dhn + _dot_tn(cb_, dyl)
            dcbb = dcbm.astype(BF16)
            dc_pieces.append(dc_g + _dot(dcbb, bb))
            db_pieces.append(db_g + _dot_tn(dcbb, cb_))

        dxd = jnp.concatenate(dx_pieces, axis=1)
        rowi = lax.broadcasted_iota(jnp.int32, (L, 1), 0)
        dacs_x = (jnp.concatenate(dacs_pieces, axis=1) + _dot_hi(colbuf[...], e1_ref[...])
                  + jnp.where(rowi == L - 1, jnp.concatenate(dacsl_pieces, axis=1), 0.0))
        upper = lax.broadcasted_iota(jnp.int32, (L, L), 0) <= lax.broadcasted_iota(jnp.int32, (L, L), 1)
        dadt_x = _dot_hi(upper, dacs_x, a_is_01=True)
        ddt_x = dxd * xs + dadt_x * a_x
        ddtr = _dot_nt_hi(ddt_x, e_ref[...]) * _sigmoid(pre)
        ddt_ref[...] = jnp.zeros_like(ddt_ref)
        ddt_ref[:, 0:SSM_H] = ddtr
        dalx =jnp.sum(dadt_x * dt_x, axis=0, keepdims=True) * a_x
        ddskx = jnp.sum(dy * xs, axis=0, keepdims=True)
        par_x = jnp.where(row8 == 1, dalx, 0.0) + jnp.where(row8 == 2, ddskx, 0.0)
        dpar_ref[...] += _dot_nt_hi(par_x, e_ref[...]) + jnp.where(row8 == 0, jnp.sum(ddtr, axis=0, keepdims=True), 0.0)

        dxs = dxd * dt_x + dsk_ref[...] * dy
        dact = jnp.concatenate([dxs] + db_pieces + dc_pieces, axis=1)
        du = dact * _dsilu(xc, sg)
        dcb_ref[...] += jnp.sum(du, axis=0, keepdims=True)
        zd[0:L, :] = du
        f1, f2, f3 = (zd[m:m + L, :] for m in (1, 2, 3))
        dxbc_ref[...] = cw_ref[3:4, :] * du + cw_ref[2:3, :] * f1 + cw_ref[1:2, :] * f2 + cw_ref[0:1, :] * f3
        dcw = jnp.zeros((8, CONV_C), F32)
        for k, shifted in enumerate((f3, f2, f1, du)):
            dcw = dcw + jnp.where(row8 == k, jnp.sum(shifted * u, axis=0, keepdims=True), 0.0)
        dcw_ref[...] += dcw
        zd[L:, :] = du[:8, :]

    return _call(
        body, name="ssd_bwd", grid=(NCH,),
        in_specs=[pl.BlockSpec((L, SSM_W), lambda i: (NCH - 1 - i, 1)), pl.BlockSpec((L, CONV_C), rev), pl.BlockSpec((L, CONV_C), rev),
                  pl.BlockSpec((L, SSM_W), rev), pl.BlockSpec((L, SSM_H), rev), pl.BlockSpec((L, SSM_W), rev),
                  pl.BlockSpec((1, 8, SSM_N, 128), lambda i: (NCH - 1 - i, 0, 0, 0)),
                  _const((4, CONV_C)), _const((1, SSM_H)), _const((1, SSM_W)), _const((1, SSM_W)), _const((1, SSM_W)),
                  _const((SSM_H, SSM_W)), _const((128, SSM_W))],
        out_specs=[pl.BlockSpec((L, CONV_C), rev), pl.BlockSpec((L, SSM_W), rev), pl.BlockSpec((L, 128), rev),
                   _const((8, CONV_C)), _const((1, CONV_C)), _const((1, SSM_W)), _const((8, SSM_H))],
        out_shape=[jax.ShapeDtypeStruct((T, CONV_C), F32), jax.ShapeDtypeStruct((T, SSM_W), F32), jax.ShapeDtypeStruct((T, 128), F32),
                   jax.ShapeDtypeStruct((8, CONV_C), F32), jax.ShapeDtypeStruct((1, CONV_C), F32), jax.ShapeDtypeStruct((1, SSM_W), F32),
                   jax.ShapeDtypeStruct((8, SSM_H), F32)],
        scratch=[pltpu.VMEM((8, SSM_N, 128), F32), pltpu.VMEM((L + 8, CONV_C), F32), pltpu.VMEM((L, 128), F32)],
        args=[dmix, xbc, xconv, z, dtr, y, hs, convw, dtb, alx, dskx, ssmn, e, e1], sem=("arbitrary",), duties=duties)


def _adam_math(w, g, m, v):
    m = ADAM_B1 * m + (1.0 - ADAM_B1) * g
    v = ADAM_B2 * v + (1.0 - ADAM_B2) * (g * g)
    m_hat = m / (1.0 - ADAM_B1 ** ADAM_STEP)
    v_hat = v / (1.0 - ADAM_B2 ** ADAM_STEP)
    delta = -ADAM_LR * (m_hat / (jnp.sqrt(v_hat) + ADAM_EPS) + ADAM_WD * w)
    return delta, m, v


PACK_W = CONV_C


def _adamw_small(ws, ms, vs, total, conv_g, after=()):
    k = len(ws)

    def body(*refs):
        ins, outs = refs[:3 * k + 2], refs[3 * k + 2 + len(after):]
        total_ref, conv_ref = ins[3 * k], ins[3 * k + 1]
        for a in range(k):
            w_ref, m_ref, v_ref = ins[3 * a:3 * a + 3]
            g = conv_ref[...] if a == k - 1 else total_ref[a:a + 1, 0:w_ref.shape[1]]
            delta, nm, nv = _adam_math(w_ref[...], g, m_ref[...], v_ref[...])
            for ref, val in zip(outs[4 * a:4 * a + 4], (g, delta, nm, nv)):
                ref[...] = val

    vm = pl.BlockSpec(memory_space=pltpu.VMEM)
    args = [x for a in range(k) for x in (ws[a], ms[a], vs[a])] + [total, conv_g]
    res = pl.pallas_call(
        body, name="adamw_small", in_specs=[vm] * len(args) + [ANY] * len(after), out_specs=[vm] * (4 * k),
        out_shape=[jax.ShapeDtypeStruct(ws[a].shape, F32) for a in range(k) for _ in range(4)],
    )(*args, *after)
    return [tuple(res[4 * a:4 * a + 4]) for a in range(k)]


COL_TILE = 512


def _adamw_sharded(ws, ms, vs, chip_sums, from_chips, other_chips, name, after=()):
    k = len(ws)
    rows, cols = ws[0].shape
    prow = chip_sums[0].shape[0]
    assert cols % COL_TILE == 0 and prow >= rows and all(a.shape == ws[0].shape for a in ws)

    def body(ids_ref, *refs):
        ins, outs = refs[:7 * k], refs[7 * k + len(after):]
        for a in range(k):
            w_ref, m_ref, v_ref, s_ref, r1_ref, r2_ref, r3_ref = ins[7 * a:7 * a + 7]
            g = s_ref[...]
            for r in (r1_ref, r2_ref, r3_ref):
                g = g + r[0].astype(F32)
            g = g[:rows]
            delta, nm, nv = _adam_math(w_ref[...], g, m_ref[...], v_ref[...])
            for ref, val in zip(outs[4 * a:4 * a + 4], (g, delta, nm, nv)):
                ref[...] = val

    spec = pl.BlockSpec((rows, COL_TILE), lambda i, ids: (0, i))
    part = lambda j: pl.BlockSpec((1, prow, COL_TILE), lambda i, ids: (ids[j], 0, i))
    one = [spec, spec, spec, pl.BlockSpec((prow, COL_TILE), lambda i, ids: (0, i)), part(0), part(1), part(2)]
    args = [x for a in range(k) for x in (ws[a], ms[a], vs[a], chip_sums[a], from_chips[a], from_chips[a], from_chips[a])]
    res = pl.pallas_call(
        body, name=name,
        grid_spec=pltpu.PrefetchScalarGridSpec(
            num_scalar_prefetch=1, grid=(cols // COL_TILE,),
            in_specs=one * k + [ANY] * len(after), out_specs=[spec] * (4 * k)),
        out_shape=[jax.ShapeDtypeStruct((rows, cols), F32)] * (4 * k),
        compiler_params=_params(("parallel",)),
    )(other_chips, *args, *after)
    return [tuple(res[4 * a:4 * a + 4]) for a in range(k)]


def _chip_sum(mines, recvs, name):
    k = len(mines)
    rows, cols = mines[0].shape[1:]

    def body(*refs):
        own_chip = pl.program_id(0) == 2 * lax.axis_index("x") + lax.axis_index("y")
        for a in range(k):
            a_ref, b_ref = refs[2 * a:2 * a + 2]
            s_ref, sb_ref = refs[2 * k + 2 * a:2 * k + 2 * a + 2]
            s = a_ref[0] + b_ref[0].astype(F32)
            sb_ref[0] = s.astype(BF16)

            @pl.when(own_chip)
            def _(s_ref=s_ref, s=s):
                s_ref[...] = s

    by_chip = pl.BlockSpec((1, rows, cols), lambda c: (c, 0, 0))
    res = pl.pallas_call(
        body, name=name, grid=(N_DEV // 2,),
        in_specs=[by_chip, by_chip] * k, out_specs=[_const((rows, cols)), by_chip] * k,
        out_shape=[jax.ShapeDtypeStruct((rows, cols), F32), jax.ShapeDtypeStruct((N_DEV // 2, rows, cols), BF16)] * k,
        compiler_params=_params(("arbitrary",)),
    )(*[x for pair in zip(mines, recvs) for x in pair])
    return [tuple(res[2 * a:2 * a + 2]) for a in range(k)]


def _all_reduce_small(v, after=()):
    rows = v.shape[0]

    def body(v_ref, *rest):
        out_ref, gath, send_sems, recv_sems = rest[len(after):]
        x, y, c = _place()
        me, sibling = (x, y, c), (x, y, 1 - c)
        chips = [(1 - x, y), (x, 1 - y), (1 - x, 1 - y)]

        def blk(px, py, pc):
            return gath.at[pl.ds((4 * px + 2 * py + pc) * rows, rows), :]

        def copy(k, block, to, src=None):
            return pltpu.make_async_remote_copy(src_ref=blk(*block) if src is None else src, dst_ref=blk(*block),
                                                send_sem=send_sems.at[k], recv_sem=recv_sems.at[k], device_id=to, device_id_type=MESH)

        gath[pl.ds((4 * x + 2 * y + c) * rows, rows), :] = v_ref[...]
        first = [copy(0, me, sibling, src=v_ref)] + [copy(1 + j, me, (*chip, c), src=v_ref) for j, chip in enumerate(chips)]
        for cp in first:
            cp.start()
        passed = [copy(4 + j, (*chip, c), sibling) for j, chip in enumerate(chips)]
        for j, chip in enumerate(chips):
            copy(1 + j, (*chip, c), me).wait_recv()
            passed[j].start()
        copy(0, sibling, me).wait_recv()
        for j, chip in enumerate(chips):
            copy(4 + j, (*chip, 1 - c), me).wait_recv()
        for cp in first + passed:
            cp.wait_send()
        acc = gath[0:rows, :]
        for d in range(1, N_DEV):
            acc = acc + gath[d * rows:(d + 1) * rows, :]
        out_ref[...] = acc

    vm = pl.BlockSpec(memory_space=pltpu.VMEM)
    return pl.pallas_call(
        body, name="all_reduce_small",
        in_specs=[vm] + [ANY] * len(after), out_specs=vm,
        out_shape=jax.ShapeDtypeStruct(v.shape, F32),
        scratch_shapes=[pltpu.VMEM((N_DEV * rows, v.shape[1]), F32), pltpu.SemaphoreType.DMA((7,)), pltpu.SemaphoreType.DMA((7,))],
    )(v, *after)


def _rope_tables(positions):
    inv_freq = ROPE_THETA ** (-jnp.arange(0, HD, 2, dtype=F32) / HD)
    ang = positions.reshape(T).astype(F32)[:, None] * inv_freq
    ang = jnp.concatenate([ang, ang, ang, ang], axis=-1)
    lo_half = (jnp.arange(128) % HD) < (HD // 2)
    return jnp.cos(ang), jnp.where(lo_half, -jnp.sin(ang), jnp.sin(ang))


def _selectors():
    lane = jnp.arange(QW)
    e = (lane[None, :] // HD == jnp.arange(SSM_H)[:, None]).astype(F32)
    e1 = ((lane[None, :] == HD * jnp.arange(128)[:, None]) & (jnp.arange(128)[:, None] < SSM_H)).astype(F32)
    src = jnp.arange(KVW)
    ex = ((lane[None, :] // (HD * (NQ // NKV)) == src[:, None] // HD) & (lane[None, :] % HD == src[:, None] % HD)).astype(F32)
    return e, e1, ex


WEIGHTS = ['ffn1_pre_norm', 'ffn1_w_gate', 'ffn1_w_up', 'ffn1_w_down', 'ffn1_post_norm', 'mix_pre_norm', 'w_in', 'conv_w', 'conv_b',
           'dt_bias', 'a_log', 'd_skip', 'ssm_norm', 'w_out', 'mix_post_norm', 'ffn2_pre_norm', 'ffn2_w_gate', 'ffn2_w_up',
           'ffn2_w_down', 'ffn2_post_norm']
COL_SHARDED = ['ffn1_w_gate', 'ffn1_w_up', 'ffn2_w_gate', 'ffn2_w_up', 'w_in']
ROW_SHARDED = ['ffn1_w_down', 'ffn2_w_down', 'w_out']
BIG = COL_SHARDED + ROW_SHARDED
FFN_BIG = COL_SHARDED[:4] + ROW_SHARDED[:2]
SMALL = ['ffn1_pre_norm', 'ffn1_post_norm', 'mix_pre_norm', 'conv_b', 'dt_bias', 'a_log', 'd_skip', 'ssm_norm', 'mix_post_norm',
         'ffn2_pre_norm', 'ffn2_post_norm']
FFN1 = ['ffn1_w_gate', 'ffn1_w_up', 'ffn1_w_down']
FFN2 = ['ffn2_w_gate', 'ffn2_w_up', 'ffn2_w_down']


def _wire_block(name, a):
    if name in FFN_BIG:
        return jnp.pad(a.astype(BF16), ((0, FSH - FSR), (0, 0)))
    if name == "w_in":
        return jnp.pad(a.astype(BF16), ((0, ISW - ISR), (0, 0)))
    return a if name == "conv_w" else a.astype(BF16)


def _whole_from_gathered(name, a):
    if name == "conv_w":
        return jnp.transpose(a, (1, 0, 2)).reshape(a.shape[1], -1)
    return a.reshape(-1, D)


def _step(x, positions, target, small, blocks=None, whole=None):
    dist = blocks is not None
    core = "mesh" if dist else 0
    w = dict(small)
    if whole:
        w.update(whole)

    def gather(names):
        return [_gather_duty([_wire_block(n, blocks[n]) for n in names])] if dist else []

    def put(names, results):
        if dist:
            for n, r in zip(names, results[0]):
                w[n] = _whole_from_gathered(n, r)

    g, sums, red = {}, {}, {}

    def swap(names):
        return [_swap_duty([g[n][1] for n in names])] if dist else []

    def chip_sums(names, from_sibling):
        if dist:
            res = _chip_sum([g[n][0] for n in names], list(from_sibling), "chip_sum_" + names[0])
            sums.update(zip(names, res))

    def exchange(names):
        return [_exchange_duty([sums[n][1] for n in names])] if dist else []

    def reduced(names, from_chips):
        if dist:
            for n, recv in zip(names, from_chips):
                red[n] = (sums[n][0], recv)

    cos, sin_s = _rope_tables(positions)
    e, e1, exf = _selectors()
    bias = _attn_bias()
    alx = jnp.repeat(w["a_log"], HD, axis=1)
    dskx = jnp.repeat(w["d_skip"], HD, axis=1)

    if dist:
        put(FFN1, _comm_only(gather(FFN1), "gather_ffn1"))
    (x1, n1, a1, b1, hm1, h1), got = _ffn_fwd(x, w["ffn1_pre_norm"], w["ffn1_w_gate"], w["ffn1_w_up"], w["ffn1_w_down"],
                                              w["ffn1_post_norm"], "ffn1_fwd", gather(["w_in", "conv_w"]))
    put(["w_in", "conv_w"], got)
    (n2, q, kx, vx, xbc, z, dtr), got = _inproj_fwd(x1, w["mix_pre_norm"], w["w_in"], cos, sin_s, exf.astype(BF16), gather(["w_out"]))
    put(["w_out"], got)
    (attn, lse), got = _attn_fwd(q, kx, vx, bias, gather(FFN2[:2]))
    put(FFN2[:2], got)
    (yn, y, hs, xconv), got = _ssd_fwd(xbc, z, dtr, w["conv_w"], w["conv_b"], w["dt_bias"], alx, dskx, w["ssm_norm"], e, gather(FFN2[2:]))
    put(FFN2[2:], got)
    x2, h2 = _outproj_fwd(x1, attn, yn, w["w_out"], w["mix_post_norm"])
    (dx3, n3, a3, b3, hm3, h3, ss), _ = _ffn_fwd(x2, w["ffn2_pre_norm"], w["ffn2_w_gate"], w["ffn2_w_up"], w["ffn2_w_down"],
                                                 w["ffn2_post_norm"], "ffn2_fwd", target=target)

    (dx2, da3, db3, dh3, g["ffn2_pre_norm"], g["ffn2_post_norm"]), _ = _ffn_bwd(
        dx3, x2, a3, b3, h3, w["ffn2_pre_norm"], w["ffn2_post_norm"], w["ffn2_w_gate"], w["ffn2_w_up"], w["ffn2_w_down"], "ffn2_bwd")
    g["ffn2_w_down"] = _matmul_tn(hm3, dh3, "ffn2_dwd", core)[0]
    g["ffn2_w_gate"] = _matmul_tn(da3, n3, "ffn2_dwg", core)[0]
    g["ffn2_w_up"] = _matmul_tn(db3, n3, "ffn2_dwu", core)[0]

    (dh2, dmix, g["mix_post_norm"]), got = _outproj_bwd(dx2, h2, w["mix_post_norm"], w["w_out"], swap(FFN2))
    chip_sums(FFN2, got[0] if dist else None)
    g["w_out"] = _dwout(attn, yn, dh2, core)
    (dq, dkx, dvx), got = _attn_bwd(q, kx, vx, attn, dmix, lse, bias, exchange(FFN2) + swap(["w_out"]))
    if dist:
        reduced(FFN2, got[0])
        chip_sums(["w_out"], got[1])
    (dxbc, dz, ddt, dcw, g["conv_b"], g["ssm_norm"], dpar), got = _ssd_bwd(
        dmix, xbc, xconv, z, dtr, y, hs, w["conv_w"], w["dt_bias"], alx, dskx, w["ssm_norm"], e, e1, exchange(["w_out"]))
    reduced(["w_out"], got[0] if dist else None)
    g["conv_w"] = dcw[0:4]
    g["dt_bias"], g["a_log"], g["d_skip"] = dpar[0:1], dpar[1:2], dpar[2:3]
    dx1, dproj, g["mix_pre_norm"] = _inproj_bwd(dx2, dq, dkx, dvx, dxbc, dz, ddt, x1, w["mix_pre_norm"], w["w_in"], cos, sin_s, exf)
    g["w_in"] = _matmul_tn(dproj, n2, "dwin", core)[0]

    started = {}

    def start(n):
        started[n] = _exchange_start(sums[n][1], "start_exchange_" + n)
        return [started[n]["token"]]

    after = []
    if dist:
        chip_sums(["w_in"], _comm_only(swap(["w_in"]), "swap_w_in")[0])
        after = start("w_in")
    (dx0, da1, db1, dh1, g["ffn1_pre_norm"], g["ffn1_post_norm"]), _ = _ffn_bwd(
        dx1, x, a1, b1, h1, w["ffn1_pre_norm"], w["ffn1_post_norm"], w["ffn1_w_gate"], w["ffn1_w_up"], w["ffn1_w_down"], "ffn1_bwd",
        after)
    total = None
    if dist:
        widen = lambda a: jnp.pad(a, ((0, 0), (0, PACK_W - a.shape[1])))
        pack = jnp.concatenate([widen(g[n]) for n in SMALL] + [g["conv_w"], widen(ss[:, 0:1])])
        assert pack.shape[0] % 8 == 0
        total = _all_reduce_small(pack, after)
        after = [total]
    g["ffn1_w_down"], _ = _matmul_tn(hm1, dh1, "ffn1_dwd", core, after=after)
    g["ffn1_w_gate"], got = _matmul_tn(da1, n1, "ffn1_dwg", core, duties=swap(["ffn1_w_down"]))
    if dist:
        chip_sums(["ffn1_w_down"], got[0])
        after = start("ffn1_w_down")
    g["ffn1_w_up"], got = _matmul_tn(db1, n1, "ffn1_dwu", core, after=after, duties=swap(["ffn1_w_gate"]))
    if dist:
        chip_sums(["ffn1_w_gate"], got[0])
        after = start("ffn1_w_gate")
        chip_sums(["ffn1_w_up"], _comm_only(swap(["ffn1_w_up"]), "swap_ffn1_w_up", after=after)[0])
        start("ffn1_w_up")
    return ss, dx0, g, red, {n: (sums[n][0], started[n]) for n in started}, total


def kernel(x, positions, ffn1_pre_norm, ffn1_w_gate, ffn1_w_up, ffn1_w_down, ffn1_post_norm, mix_pre_norm, w_in, conv_w, conv_b, dt_bias, a_log, d_skip, ssm_norm, w_out, mix_post_norm, ffn2_pre_norm, ffn2_w_gate, ffn2_w_up, ffn2_w_down, ffn2_post_norm, loss_target, m_ffn1_pre_norm, m_ffn1_w_gate, m_ffn1_w_up, m_ffn1_w_down, m_ffn1_post_norm, m_mix_pre_norm, m_w_in, m_conv_w, m_conv_b, m_dt_bias, m_a_log, m_d_skip, m_ssm_norm, m_w_out, m_mix_post_norm, m_ffn2_pre_norm, m_ffn2_w_gate, m_ffn2_w_up, m_ffn2_w_down, m_ffn2_post_norm, v_ffn1_pre_norm, v_ffn1_w_gate, v_ffn1_w_up, v_ffn1_w_down, v_ffn1_post_norm, v_mix_pre_norm, v_w_in, v_conv_w, v_conv_b, v_dt_bias, v_a_log, v_d_skip, v_ssm_norm, v_w_out, v_mix_post_norm, v_ffn2_pre_norm, v_ffn2_w_gate, v_ffn2_w_up, v_ffn2_w_down, v_ffn2_post_norm):
    given = dict(locals())
    drop = lambda n, a: a if n in SMALL else (a[0].T if n in COL_SHARDED else a[0])
    w = {n: drop(n, given[n]) for n in WEIGHTS}
    m = {n: drop(n, given["m_" + n]) for n in WEIGHTS}
    v = {n: drop(n, given["v_" + n]) for n in WEIGHTS}
    cx, cy, cc = _place()
    others = [2 * (1 - cx) + cy, 2 * cx + (1 - cy), 2 * (1 - cx) + (1 - cy)]

    _, grad_x, g, red, pending, total = _step(x[0], positions, loss_target[0], {n: w[n] for n in SMALL},
                                              blocks={n: w[n] for n in BIG + ["conv_w"]})
    chip_ids = jnp.stack(others).astype(jnp.int32)
    out_g, out_d, out_m, out_v = {}, {}, {}, {}

    def update(names, sums, recvs, label, after=()):
        res = _adamw_sharded([w[n] for n in names], [m[n] for n in names], [v[n] for n in names], sums, recvs, chip_ids,
                             "adamw_" + label, after)
        for n, (gn, dn, mn, vn) in zip(names, res):
            out_g[n], out_d[n], out_m[n], out_v[n] = gn, dn, mn, vn

    last_start = [pending["ffn1_w_up"][1]["token"]]
    update(FFN2, [red[n][0] for n in FFN2], [red[n][1] for n in FFN2], "ffn2", last_start)
    update(["w_out"], [red["w_out"][0]], [red["w_out"][1]], "w_out", last_start)

    n_small = len(SMALL)
    conv_g = lax.dynamic_slice_in_dim(total[n_small:n_small + 4], (4 * cx + 2 * cy + cc) * (CONV_C // N_DEV), CONV_C // N_DEV, axis=1)
    loss = 0.5 * total[n_small + 4, 0] / D
    names = SMALL + ["conv_w"]
    res = _adamw_small([w[n] for n in names], [m[n] for n in names], [v[n] for n in names], total, conv_g, last_start)
    for n, (gn, dn, mn, vn) in zip(names, res):
        out_g[n], out_d[n], out_m[n], out_v[n] = gn, dn, mn, vn
    done = [out_v[n] for n in FFN2 + ["w_out", "conv_w"]]
    update(["w_in"], [pending["w_in"][0]], [_exchange_wait(pending["w_in"][1], done, "wait_exchange_w_in")], "w_in")

    def as_given(n, a):
        return a if n in SMALL else (a.T[None] if n in COL_SHARDED else a[None])

    w_in_out, done = lax.optimization_barrier(([as_given("w_in", d["w_in"]) for d in (out_g, out_d, out_m, out_v)], [out_v["w_in"]]))
    update(FFN1, [pending[n][0] for n in FFN1], [_exchange_wait(pending[n][1], done, "wait_exchange_" + n) for n in FFN1], "ffn1")

    outs = [loss, grad_x[None]]
    for k, d in enumerate((out_g, out_d, out_m, out_v)):
        outs += [w_in_out[k] if n == "w_in" else as_given(n, d[n]) for n in WEIGHTS]
    return tuple(outs)
```

```python
import functools

import jax
import jax.numpy as jnp
from jax import lax
from jax.experimental import pallas as pl
from jax.experimental.pallas import tpu as pltpu

F32 = jnp.float32
BF16 = jnp.bfloat16
MESH = pl.DeviceIdType.MESH

N_DEV = 8
T = 2048
D = 1024
FF = 2816
FSR = FF // N_DEV
FSH = 384
FFP = N_DEV * FSH
HD = 64
NQ = 16
NKV = 4
QW = NQ * HD
KVW = NKV * HD
SSM_W = 1024
SSM_H = 16
SSM_N = 128
CONV_C = SSM_W + 2 * 2 * SSM_N
IN_COLS = 4112
INP = 4224
ISR = IN_COLS // N_DEV
ISW = 528
ISG = 640
L = 128
NCH = T // L
AB = 256
NAB = T // AB
EPS = 1e-6
NEG = -1e30
ROPE_THETA = 10000.0
DILATIONS = ((128, 1), (512, 4), (2048, 16))

ADAM_LR = 0.001
ADAM_B1 = 0.9
ADAM_B2 = 0.999
ADAM_EPS = 1e-08
ADAM_WD = 0.01
ADAM_STEP = 10

VMEM_LIMIT = 58 * 1024 * 1024


def _params(sem, vmem=VMEM_LIMIT):
    return pltpu.CompilerParams(dimension_semantics=sem, vmem_limit_bytes=vmem)


def _dot(a, b):
    return jnp.dot(a, b, preferred_element_type=F32)


def _dot_nt(a, b):
    return lax.dot_general(a, b, (((1,), (1,)), ((), ())), preferred_element_type=F32)


def _dot_tn(a, b):
    return lax.dot_general(a, b, (((0,), (0,)), ((), ())), preferred_element_type=F32)


def _split3(x):
    hi = x.astype(BF16)
    r1 = x - hi.astype(F32)
    mid = r1.astype(BF16)
    lo = (r1 - mid.astype(F32)).astype(BF16)
    return hi, mid, lo


def _dot_hi(a, b, a_is_01=False):
    if a_is_01:
        sel = a.astype(BF16)
        return sum(_dot(sel, p) for p in _split3(b))
    sel = b.astype(BF16)
    return sum(_dot(p, sel) for p in _split3(a))


def _dot_nt_hi(a, b):
    sel = b.astype(BF16)
    return sum(_dot_nt(p, sel) for p in _split3(a))


def _rs(x):
    return lax.rsqrt(jnp.mean(x * x, axis=-1, keepdims=True) + EPS)


def _sigmoid(x):
    return jax.nn.sigmoid(x)


def _dsilu(x, s):
    return s * (1.0 + x * (1.0 - s))


def _resident(shape):
    nd = len(shape)
    return pl.BlockSpec(shape, lambda *_: (0,) * nd, pipeline_mode=pl.Buffered(1))


def _const(shape):
    nd = len(shape)
    return pl.BlockSpec(shape, lambda *_: (0,) * nd)


def _rows(tm, cols):
    return pl.BlockSpec((tm, cols), lambda i: (i, 0))


ANY = pl.BlockSpec(memory_space=pl.ANY)


def _place():
    return lax.axis_index("x"), lax.axis_index("y"), lax.axis_index("c")


def _gather_duty(arrays):
    n = len(arrays)
    results = [jax.ShapeDtypeStruct((N_DEV,) + a.shape, a.dtype) for a in arrays]

    def make(ins, outs, send_sems, recv_sems, local_sems):
        x, y, c = _place()
        me, sibling = (x, y, c), (x, y, 1 - c)
        chips = [(1 - x, y), (x, 1 - y), (1 - x, 1 - y)]

        def place_of(a, px, py, pc):
            return outs[a].at[4 * px + 2 * py + pc]

        def copy(a, k, block, to, src=None):
            dst = place_of(a, *block)
            return pltpu.make_async_remote_copy(src_ref=dst if src is None else src, dst_ref=dst,
                                                send_sem=send_sems.at[7 * a + k], recv_sem=recv_sems.at[7 * a + k],
                                                device_id=to, device_id_type=MESH)

        def own(a):
            return pltpu.make_async_copy(ins[a], place_of(a, *me), local_sems.at[a])

        def first(a):
            return [copy(a, 0, me, sibling, src=ins[a])] + [copy(a, 1 + j, me, (*chip, c), src=ins[a]) for j, chip in enumerate(chips)]

        def start():
            for a in range(n):
                own(a).start()
            for a in range(n):
                for cp in first(a):
                    cp.start()

        def finish():
            for j, chip in enumerate(chips):
                for a in range(n):
                    copy(a, 1 + j, (*chip, c), me).wait_recv()
                    copy(a, 4 + j, (*chip, c), sibling).start()
            for a in range(n):
                copy(a, 0, sibling, me).wait_recv()
                for j, chip in enumerate(chips):
                    copy(a, 4 + j, (*chip, 1 - c), me).wait_recv()
            for a in range(n):
                for cp in first(a) + [copy(a, 4 + j, (*chip, c), sibling) for j, chip in enumerate(chips)]:
                    cp.wait_send()
                own(a).wait()

        return start, finish

    return dict(operands=list(arrays), results=results, sems=(7 * n, 7 * n, n), make=make)


def _swap_duty(arrays):
    n = len(arrays)
    half = N_DEV // 2
    results = [jax.ShapeDtypeStruct(a.shape, a.dtype) for a in arrays]

    def make(ins, outs, send_sems, recv_sems):
        x, y, c = _place()

        def copies():
            return [pltpu.make_async_remote_copy(src_ref=ins[a].at[k], dst_ref=outs[a].at[k],
                                                 send_sem=send_sems.at[half * a + k], recv_sem=recv_sems.at[half * a + k],
                                                 device_id=(x, y, 1 - c), device_id_type=MESH)
                    for a in range(n) for k in range(half)]

        def start():
            for cp in copies():
                cp.start()

        def finish():
            for cp in copies():
                cp.wait()

        return start, finish

    return dict(operands=list(arrays), results=results, sems=(half * n, half * n), make=make)


def _exchange_duty(arrays):
    n = len(arrays)
    results = [jax.ShapeDtypeStruct(a.shape, a.dtype) for a in arrays]

    def make(ins, outs, send_sems, recv_sems):
        x, y, c = _place()
        chips = [(1 - x, y), (x, 1 - y), (1 - x, 1 - y)]
        my_chip = 2 * x + y

        def sends():
            return [pltpu.make_async_remote_copy(src_ref=ins[a].at[2 * px + py], dst_ref=outs[a].at[my_chip],
                                                 send_sem=send_sems.at[3 * a + j], recv_sem=recv_sems.at[3 * a + j],
                                                 device_id=(px, py, c), device_id_type=MESH)
                    for a in range(n) for j, (px, py) in enumerate(chips)]

        def start():
            for cp in sends():
                cp.start()

        def finish():
            for a in range(n):
                for j, (px, py) in enumerate(chips):
                    pltpu.make_async_remote_copy(src_ref=ins[a].at[my_chip], dst_ref=outs[a].at[2 * px + py],
                                                 send_sem=send_sems.at[3 * a + j], recv_sem=recv_sems.at[3 * a + j],
                                                 device_id=(px, py, c), device_id_type=MESH).wait_recv()
            for cp in sends():
                cp.wait_send()

        return start, finish

    return dict(operands=list(arrays), results=results, sems=(3 * n, 3 * n), make=make)


def _call(body, *, name, grid, in_specs, out_specs, out_shape, args, sem, scratch=(), duties=(), after=()):
    n_in, n_out, n_scr = len(in_specs), len(out_specs), len(scratch)
    sem_shapes = [pltpu.SemaphoreType.DMA((k,)) for d in duties for k in d["sems"]]

    def full(*refs):
        pos = [0]

        def take(k):
            pos[0] += k
            return refs[pos[0] - k:pos[0]]

        ins = take(n_in)
        d_ins = [take(len(d["operands"])) for d in duties]
        take(len(after))
        outs = take(n_out)
        d_outs = [take(len(d["results"])) for d in duties]
        scr = take(n_scr)
        d_sems = [take(len(d["sems"])) for d in duties]
        hooks = [d["make"](di, do, *ds) for d, di, do, ds in zip(duties, d_ins, d_outs, d_sems)]
        if grid and hooks:
            ids = [pl.program_id(k) for k in range(len(grid))]
            first = functools.reduce(jnp.logical_and, [i == 0 for i in ids])
            last = functools.reduce(jnp.logical_and, [i == g - 1 for i, g in zip(ids, grid)])

            @pl.when(first)
            def _():
                for start, _ in hooks:
                    start()

            body(*ins, *outs, *scr)

            @pl.when(last)
            def _():
                for _, finish in hooks:
                    finish()
        else:
            for start, _ in hooks:
                start()
            body(*ins, *outs, *scr)
            for _, finish in hooks:
                finish()

    d_args = [a for d in duties for a in d["operands"]]
    d_res = [r for d in duties for r in d["results"]]
    kwargs = dict(grid=grid) if grid else {}
    res = pl.pallas_call(
        full, name=name, in_specs=list(in_specs) + [ANY] * (len(d_args) + len(after)), out_specs=list(out_specs) + [ANY] * len(d_res),
        out_shape=list(out_shape) + d_res, scratch_shapes=list(scratch) + sem_shapes,
        compiler_params=_params(sem) if grid else None, **kwargs,
    )(*args, *d_args, *after)
    own, rest = list(res[:n_out]), list(res[n_out:])
    by_duty = []
    for d in duties:
        by_duty.append(rest[:len(d["results"])])
        rest = rest[len(d["results"]):]
    return own, by_duty


def _comm_only(duties, name, after=()):
    return _call(lambda: None, name=name, grid=None, in_specs=[], out_specs=[], out_shape=[], args=[], sem=None, duties=duties,
                 after=after)[1]


HBM = pl.BlockSpec(memory_space=pltpu.HBM)
SEMS = pl.BlockSpec(memory_space=pltpu.SEMAPHORE)
SIDE_EFFECT = pltpu.SideEffectType.DATAFLOW_SIDE_EFFECTING
N_OTHER_CHIPS = 3


def _chip_copies(src_ref, land_ref, sems):
    x, y, c = _place()
    chips = [(1 - x, y), (x, 1 - y), (1 - x, 1 - y)]
    return [pltpu.make_async_remote_copy(src_ref=src_ref.at[2 * px + py], dst_ref=land_ref.at[2 * x + y],
                                         send_sem=sems[j], recv_sem=sems[N_OTHER_CHIPS + j], device_id=(px, py, c), device_id_type=MESH)
            for j, (px, py) in enumerate(chips)]


def _exchange_start(pb, name):
    n_sem = 2 * N_OTHER_CHIPS

    def body(pb_ref, land_ref, *rest):
        for cp in _chip_copies(pb_ref, land_ref, rest[:n_sem]):
            cp.start()
        token = rest[n_sem + 2]
        token[...] = jnp.zeros_like(token)

    res = pl.pallas_call(
        body, name=name,
        out_shape=(pltpu.SemaphoreType.DMA(()),) * n_sem + (pltpu.HBM(pb.shape, pb.dtype), pltpu.HBM(pb.shape, pb.dtype),
                                                              jax.ShapeDtypeStruct((8, 128), F32)),
        in_specs=(HBM, HBM), out_specs=(SEMS,) * n_sem + (HBM, HBM, pl.BlockSpec(memory_space=pltpu.VMEM)),
        input_output_aliases={0: n_sem, 1: n_sem + 1},
        compiler_params=pltpu.CompilerParams(has_side_effects=SIDE_EFFECT),
    )(pltpu.with_memory_space_constraint(pb, pltpu.HBM), pltpu.with_memory_space_constraint(lax.empty(pb.shape, pb.dtype), pltpu.HBM))
    return dict(sems=res[:n_sem], src=res[n_sem], land=res[n_sem + 1], token=res[n_sem + 2])


def _exchange_wait(started, after, name):
    n_sem = 2 * N_OTHER_CHIPS

    def body(pb_ref, land_ref, *rest):
        for cp in _chip_copies(pb_ref, land_ref, rest[:n_sem]):
            cp.wait_send()
            cp.wait_recv()

    src, land = started["src"], started["land"]
    return pl.pallas_call(
        body, name=name, out_shape=(pltpu.HBM(src.shape, src.dtype), pltpu.HBM(land.shape, land.dtype)),
        in_specs=(HBM, HBM) + (SEMS,) * n_sem + (ANY,) * len(after), out_specs=(HBM, HBM), input_output_aliases={0: 0, 1: 1},
        compiler_params=pltpu.CompilerParams(has_side_effects=SIDE_EFFECT),
    )(src, land, *started["sems"], *after)[1]


def _ffn_fwd(x, gpre, wg, wu, wd, gpost, name, duties=(), target=None):
    tm = 256
    n_in = 6 if target is None else 7

    def body(*refs):
        x_ref, gpre_ref, wg_ref, wu_ref, wd_ref, gpost_ref = refs[:6]
        xo_ref, n_ref, a_ref, b_ref, hm_ref, h_ref = refs[n_in:n_in + 6]
        xv = x_ref[...]
        n = (xv * _rs(xv) * gpre_ref[...]).astype(BF16)
        a = _dot_nt(n, wg_ref[...])
        b = _dot_nt(n, wu_ref[...])
        hm = (a * _sigmoid(a) * b).astype(BF16)
        h = _dot(hm, wd_ref[...])
        xo = xv + 0.5 * (h * _rs(h) * gpost_ref[...])
        n_ref[...] = n
        a_ref[...] = a.astype(BF16)
        b_ref[...] = b.astype(BF16)
        hm_ref[...] = hm
        h_ref[...] = h
        if target is None:
            xo_ref[...] = xo
        else:
            ss_ref = refs[n_in + 6]

            @pl.when(pl.program_id(0) == 0)
            def _():
                ss_ref[...] = jnp.zeros_like(ss_ref)

            err = xo - refs[6][...]
            xo_ref[...] = err * (1.0 / D)
            ss_ref[...] += jnp.sum(jnp.sum(err * err, axis=1, keepdims=True), axis=0, keepdims=True)

    loss_in = [] if target is None else [_rows(tm, D)]
    loss_out = [] if target is None else [_const((1, 128))]
    loss_shape = [] if target is None else [jax.ShapeDtypeStruct((1, 128), F32)]
    return _call(
        body, name=name, grid=(T // tm,),
        in_specs=[_rows(tm, D), _const((1, D)), _resident((FFP, D)), _resident((FFP, D)), _resident((FFP, D)), _const((1, D))] + loss_in,
        out_specs=[_rows(tm, D), _rows(tm, D), _rows(tm, FFP), _rows(tm, FFP), _rows(tm, FFP), _rows(tm, D)] + loss_out,
        out_shape=[jax.ShapeDtypeStruct((T, D), F32), jax.ShapeDtypeStruct((T, D), BF16), jax.ShapeDtypeStruct((T, FFP), BF16),
                   jax.ShapeDtypeStruct((T, FFP), BF16), jax.ShapeDtypeStruct((T, FFP), BF16), jax.ShapeDtypeStruct((T, D), F32)]
        + loss_shape,
        args=[x, gpre, wg, wu, wd, gpost] + ([] if target is None else [target]), sem=("arbitrary",), duties=duties)


def _ffn_bwd(dxo, x, a, b, h, gpre, gpost, wg, wu, wd, name, after=()):
    tm = 256

    def body(dxo_ref, x_ref, a_ref, b_ref, h_ref, gpre_ref, gpost_ref, wg_ref, wu_ref, wd_ref,
             dx_ref, da_ref, db_ref, dh_ref, dgpre_ref, dgpost_ref):
        @pl.when(pl.program_id(0) == 0)
        def _():
            dgpre_ref[...] = jnp.zeros_like(dgpre_ref)
            dgpost_ref[...] = jnp.zeros_like(dgpost_ref)

        dy = dxo_ref[...]
        h = h_ref[...]
        hn = h * _rs(h)
        r2 = _rs(h)
        dgpost_ref[...] += jnp.sum(0.5 * dy * hn, axis=0, keepdims=True)
        gdy = 0.5 * dy * gpost_ref[...]
        dh = r2 * (gdy - hn * jnp.mean(gdy * hn, axis=-1, keepdims=True))
        dhb = dh.astype(BF16)
        dh_ref[...] = dhb
        dhm = _dot_nt(dhb, wd_ref[...])
        av = a_ref[...].astype(F32)
        bv = b_ref[...].astype(F32)
        sg = _sigmoid(av)
        db = (dhm * (av * sg)).astype(BF16)
        da = (dhm * bv * _dsilu(av, sg)).astype(BF16)
        da_ref[...] = da
        db_ref[...] = db
        dn = _dot(da, wg_ref[...]) + _dot(db, wu_ref[...])
        xv = x_ref[...]
        r = _rs(xv)
        xn = xv * r
        dgpre_ref[...] += jnp.sum(dn * xn, axis=0, keepdims=True)
        gdn = dn * gpre_ref[...]
        dx_ref[...] = dy + r * (gdn - xn * jnp.mean(gdn * xn, axis=-1, keepdims=True))

    return _call(
        body, name=name, grid=(T // tm,),
        in_specs=[_rows(tm, D), _rows(tm, D), _rows(tm, FFP), _rows(tm, FFP), _rows(tm, D), _const((1, D)), _const((1, D)),
                  _resident((FFP, D)), _resident((FFP, D)), _resident((FFP, D))],
        out_specs=[_rows(tm, D), _rows(tm, FFP), _rows(tm, FFP), _rows(tm, D), _const((1, D)), _const((1, D))],
        out_shape=[jax.ShapeDtypeStruct((T, D), F32), jax.ShapeDtypeStruct((T, FFP), BF16), jax.ShapeDtypeStruct((T, FFP), BF16),
                   jax.ShapeDtypeStruct((T, D), BF16), jax.ShapeDtypeStruct((1, D), F32), jax.ShapeDtypeStruct((1, D), F32)],
        args=[dxo, x, a, b, h, gpre, gpost, wg, wu, wd], sem=("arbitrary",), after=after)


def _core_index(core):
    return lax.axis_index("c") if core == "mesh" else core


def _by_core(res, o_ref, ob_ref, core):
    r = res.shape[0] // 2
    c = jnp.asarray(_core_index(core))

    @pl.when(c == 0)
    def _():
        o_ref[0] = res[:r]
        ob_ref[0] = res[r:].astype(BF16)

    @pl.when(c == 1)
    def _():
        o_ref[0] = res[r:]
        ob_ref[0] = res[:r].astype(BF16)


def _matmul_tn(a, b, name, core, after=(), duties=()):
    k, m = a.shape
    n = b.shape[1]
    r = m // N_DEV
    assert m == N_DEV * r and r % 128 == 0

    def body(a_ref, b_ref, o_ref, ob_ref):
        _by_core(_dot_tn(a_ref[...], b_ref[...]), o_ref, ob_ref, core)

    spec = pl.BlockSpec((1, r, n), lambda i: (i, 0, 0))
    return _call(body, name=name, grid=(N_DEV // 2,), in_specs=[pl.BlockSpec((k, 2 * r), lambda i: (0, i)), _resident((k, n))],
                 out_specs=[spec, spec],
                 out_shape=[jax.ShapeDtypeStruct((N_DEV // 2, r, n), F32), jax.ShapeDtypeStruct((N_DEV // 2, r, n), BF16)],
                 args=[a, b], sem=("arbitrary",), duties=duties, after=after)


def _dwout(attn, yn, dh2, core):
    rs = (QW + SSM_W) // N_DEV
    chips = N_DEV // 2

    def body(at_ref, yn_ref, dh_ref, o_ref, ob_ref):
        i = pl.program_id(0)

        @pl.when(i < chips // 2)
        def _():
            _by_core(_dot_tn(at_ref[...], dh_ref[...]), o_ref, ob_ref, core)

        @pl.when(i >= chips // 2)
        def _():
            _by_core(_dot_tn(yn_ref[...], dh_ref[...]), o_ref, ob_ref, core)

    spec = pl.BlockSpec((1, rs, D), lambda i: (i, 0, 0))
    return pl.pallas_call(
        body, name="dwout", grid=(chips,),
        in_specs=[pl.BlockSpec((T, 2 * rs), lambda i: (0, jnp.minimum(i, chips // 2 - 1))),
                  pl.BlockSpec((T, 2 * rs), lambda i: (0, jnp.maximum(i - chips // 2, 0))), _resident((T, D))],
        out_specs=[spec, spec],
        out_shape=[jax.ShapeDtypeStruct((chips, rs, D), F32), jax.ShapeDtypeStruct((chips, rs, D), BF16)],
        compiler_params=_params(("arbitrary",)),
    )(attn, yn, dh2)


def _rope_swap(t, lo_half):
    return jnp.where(lo_half, pltpu.roll(t, 96, 1), pltpu.roll(t, 32, 1))


def _inproj_fwd(x1, gpre, win, cos, sin_s, ex, duties=()):
    tm = 256

    def body(x_ref, g_ref, w_ref, cos_ref, sin_ref, ex_ref, n_ref, q_ref, kx_ref, vx_ref, xbc_ref, z_ref, dt_ref):
        xv = x_ref[...]
        n = (xv * _rs(xv) * g_ref[...]).astype(BF16)
        n_ref[...] = n
        by_dev = _dot_nt(n, w_ref[...])
        proj = jnp.concatenate([by_dev[:, ISW * d:ISW * d + ISR] for d in range(N_DEV)], axis=1)
        cs = cos_ref[...]
        sn = sin_ref[...]
        lo_half = (lax.broadcasted_iota(jnp.int32, (1, 128), 1) % HD) < (HD // 2)

        def rope(t):
            return t * cs + _rope_swap(t, lo_half) * sn

        for j in range(QW // 128):
            t = proj[:, 128 * j:128 * j + 128]
            q_ref[:, 128 * j:128 * j + 128] = (rope(t) * (HD ** -0.5)).astype(BF16)
        k = jnp.concatenate([rope(proj[:, QW + 128 * j:QW + 128 * j + 128]) for j in range(KVW // 128)], axis=1)
        v = proj[:, QW + KVW:QW + 2 * KVW]
        kx_ref[...] = _dot(k.astype(BF16), ex_ref[...]).astype(BF16)
        vx_ref[...] = _dot(v.astype(BF16), ex_ref[...]).astype(BF16)
        c0 = QW + 2 * KVW
        xbc_ref[...] = proj[:, c0:c0 + CONV_C]
        z_ref[...] = proj[:, c0 + CONV_C:c0 + CONV_C + SSM_W]
        dt_ref[...] = proj[:, c0 + CONV_C + SSM_W:IN_COLS]

    return _call(
        body, name="inproj_fwd", grid=(T // tm,),
        in_specs=[_rows(tm, D), _const((1, D)), _resident((INP, D)), _rows(tm, 128), _rows(tm, 128), _const((KVW, QW))],
        out_specs=[_rows(tm, D), _rows(tm, QW), _rows(tm, QW), _rows(tm, QW), _rows(tm, CONV_C), _rows(tm, SSM_W), _rows(tm, SSM_H)],
        out_shape=[jax.ShapeDtypeStruct((T, D), BF16), jax.ShapeDtypeStruct((T, QW), BF16), jax.ShapeDtypeStruct((T, QW), BF16),
                   jax.ShapeDtypeStruct((T, QW), BF16), jax.ShapeDtypeStruct((T, CONV_C), F32), jax.ShapeDtypeStruct((T, SSM_W), F32),
                   jax.ShapeDtypeStruct((T, SSM_H), F32)],
        args=[x1, gpre, win, cos, sin_s, ex], sem=("arbitrary",), duties=duties)


def _inproj_bwd(dres, dq, dkx, dvx, dxbc, dz, ddt, x1, gpre, win, cos, sin_s, exf):
    tm = 256

    def body(dres_ref, dq_ref, dkx_ref, dvx_ref, dxbc_ref, dz_ref, ddt_ref, x_ref, g_ref, w_ref, cos_ref, sin_ref, ex_ref,
             dx_ref, dps_ref, dg_ref, dp_ref):
        @pl.when(pl.program_id(0) == 0)
        def _():
            dg_ref[...] = jnp.zeros_like(dg_ref)

        cs = cos_ref[...]
        sn = sin_ref[...]
        lo_half = (lax.broadcasted_iota(jnp.int32, (1, 128), 1) % HD) < (HD // 2)

        def rope_t(t):
            return t * cs - _rope_swap(t, lo_half) * sn

        for j in range(QW // 128):
            dp_ref[:, 128 * j:128 * j + 128] = rope_t(dq_ref[:, 128 * j:128 * j + 128] * (HD ** -0.5)).astype(BF16)
        dk = _dot_nt_hi(dkx_ref[...], ex_ref[...])
        dv = _dot_nt_hi(dvx_ref[...], ex_ref[...])
        for j in range(KVW // 128):
            dp_ref[:, QW + 128 * j:QW + 128 * j + 128] = rope_t(dk[:, 128 * j:128 * j + 128]).astype(BF16)
        dp_ref[:, QW + KVW:QW + 2 * KVW] = dv.astype(BF16)
        c0 = QW + 2 * KVW
        dp_ref[:, c0:c0 + CONV_C] = dxbc_ref[...].astype(BF16)
        dp_ref[:, c0 + CONV_C:c0 + CONV_C + SSM_W] = dz_ref[...].astype(BF16)
        dp_ref[:, c0 + CONV_C + SSM_W:INP] = ddt_ref[...].astype(BF16)
        pieces = [dp_ref[:, ISR * d:ISR * (d + 1)] for d in range(N_DEV)]
        zw = jnp.zeros((tm, ISW - ISR), BF16)
        zg = jnp.zeros((tm, ISG - ISR), BF16)
        dn = _dot(jnp.concatenate([t for p in pieces for t in (p, zw)], axis=1), w_ref[...])
        for d in range(N_DEV):
            dps_ref[:, ISG * d:ISG * (d + 1)] = jnp.concatenate([pieces[d], zg], axis=1)
        xv = x_ref[...]
        r = _rs(xv)
        xn = xv * r
        dg_ref[...] += jnp.sum(dn * xn, axis=0, keepdims=True)
        gdn = dn * g_ref[...]
        dx_ref[...] = dres_ref[...] + r * (gdn - xn * jnp.mean(gdn * xn, axis=-1, keepdims=True))

    return pl.pallas_call(
        body, name="inproj_bwd", grid=(T // tm,),
        in_specs=[_rows(tm, D), _rows(tm, QW), _rows(tm, QW), _rows(tm, QW), _rows(tm, CONV_C), _rows(tm, SSM_W), _rows(tm, 128),
                  _rows(tm, D), _const((1, D)), _resident((INP, D)), _rows(tm, 128), _rows(tm, 128), _const((KVW, QW))],
        out_specs=[_rows(tm, D), _rows(tm, N_DEV * ISG), _const((1, D))],
        out_shape=[jax.ShapeDtypeStruct((T, D), F32), jax.ShapeDtypeStruct((T, N_DEV * ISG), BF16), jax.ShapeDtypeStruct((1, D), F32)],
        scratch_shapes=[pltpu.VMEM((tm, INP), BF16)],
        compiler_params=_params(("arbitrary",)),
    )(dres, dq, dkx, dvx, dxbc, dz, ddt, x1, gpre, win, cos, sin_s, exf)


def _outproj_fwd(x1, attn, yn, wout, gpost):
    tm = 512

    def body(x_ref, at_ref, yn_ref, w_ref, g_ref, xo_ref, h_ref):
        h = _dot(at_ref[...], w_ref[0:QW, :]) + _dot(yn_ref[...], w_ref[QW:QW + SSM_W, :])
        h_ref[...] = h
        xo_ref[...] = x_ref[...] + h * _rs(h) * g_ref[...]

    return pl.pallas_call(
        body, name="outproj_fwd", grid=(T // tm,),
        in_specs=[_rows(tm, D), _rows(tm, QW), _rows(tm, SSM_W), _resident((QW + SSM_W, D)), _const((1, D))],
        out_specs=[_rows(tm, D), _rows(tm, D)],
        out_shape=[jax.ShapeDtypeStruct((T, D), F32), jax.ShapeDtypeStruct((T, D), F32)],
        compiler_params=_params(("parallel",)),
    )(x1, attn, yn, wout, gpost)


def _outproj_bwd(dx2, h2, gpost, wout, duties=()):
    tm = 512

    def body(dy_ref, h_ref, g_ref, w_ref, dh_ref, dm_ref, dg_ref):
        @pl.when(pl.program_id(0) == 0)
        def _():
            dg_ref[...] = jnp.zeros_like(dg_ref)

        dy = dy_ref[...]
        h = h_ref[...]
        r = _rs(h)
        hn = h * r
        dg_ref[...] += jnp.sum(dy * hn, axis=0, keepdims=True)
        gdy = dy * g_ref[...]
        dh = (r * (gdy - hn * jnp.mean(gdy * hn, axis=-1, keepdims=True))).astype(BF16)
        dh_ref[...] = dh
        dm_ref[...] = _dot_nt(dh, w_ref[...])

    return _call(
        body, name="outproj_bwd", grid=(T // tm,),
        in_specs=[_rows(tm, D), _rows(tm, D), _const((1, D)), _resident((QW + SSM_W, D))],
        out_specs=[_rows(tm, D), _rows(tm, QW + SSM_W), _const((1, D))],
        out_shape=[jax.ShapeDtypeStruct((T, D), BF16), jax.ShapeDtypeStruct((T, QW + SSM_W), F32), jax.ShapeDtypeStruct((1, D), F32)],
        args=[dx2, h2, gpost, wout], sem=("arbitrary",), duties=duties)


def _attn_bias():
    d = jnp.arange(AB)[:, None] - jnp.arange(T)[None, :] + (T - AB)
    cnt = jnp.zeros(d.shape, F32)
    for window, dil in DILATIONS:
        cnt = cnt + ((d >= 0) & (d % dil == 0) & (d <= window)).astype(F32)
    return jnp.where(cnt > 0, jnp.log(jnp.maximum(cnt, 1.0)), NEG)


G_PER = NQ // NKV
WK = G_PER * HD


def _attn_fwd(q, kx, vx, bias, duties=()):
    def body(q_ref, kx_ref, vx_ref, bias_ref, o_ref, lse_ref):
        lane = lax.broadcasted_iota(jnp.int32, (1, WK), 1)
        lse_ref[...] = jnp.zeros_like(lse_ref)
        for i in range(NAB):
            n = (i + 1) * AB
            rows = slice(i * AB, n)
            qi = q_ref[rows, :]
            kxi = kx_ref[0:n, :]
            vxi = vx_ref[0:n, :]
            bb = bias_ref[:, (NAB - 1 - i) * AB:]
            o_acc = jnp.zeros((AB, WK), F32)
            for g in range(G_PER):
                mg = (lane // HD) == g
                s = _dot_nt(jnp.where(mg, qi, jnp.zeros_like(qi)), kxi) + bb
                m = jnp.max(s, axis=1, keepdims=True)
                p = jnp.exp(s - m)
                l = jnp.sum(p, axis=1, keepdims=True)
                o_acc = jnp.where(mg, _dot(p.astype(BF16), vxi) / l, o_acc)
                lse_ref[rows, g:g + 1] = m + jnp.log(l)
            o_ref[rows, :] = o_acc.astype(BF16)

    col = lambda kv: (0, kv)
    return _call(
        body, name="attn_fwd", grid=(NKV,),
        in_specs=[pl.BlockSpec((T, WK), col), pl.BlockSpec((T, WK), col), pl.BlockSpec((T, WK), col), _const((AB, T))],
        out_specs=[pl.BlockSpec((T, WK), col), pl.BlockSpec((T, 128), col)],
        out_shape=[jax.ShapeDtypeStruct((T, QW), BF16), jax.ShapeDtypeStruct((T, NKV * 128), F32)],
        args=[q, kx, vx, bias], sem=("arbitrary",), duties=duties)


def _attn_bwd(q, kx, vx, o, dmix, lse, bias, duties=()):
    def body(q_ref, kx_ref, vx_ref, o_ref, do_ref, lse_ref, bias_ref, dq_ref, dkx_ref, dvx_ref):
        lane = lax.broadcasted_iota(jnp.int32, (1, WK), 1)
        dkx_ref[...] = jnp.zeros_like(dkx_ref)
        dvx_ref[...] = jnp.zeros_like(dvx_ref)
        for i in range(NAB):
            n = (i + 1) * AB
            rows = slice(i * AB, n)
            qi = q_ref[rows, :]
            dof = do_ref[rows, :]
            doi = dof.astype(BF16)
            prod = dof * o_ref[rows, :].astype(F32)
            kxi = kx_ref[0:n, :]
            vxi = vx_ref[0:n, :]
            bb = bias_ref[:, (NAB - 1 - i) * AB:]
            dq_acc = jnp.zeros((AB, WK), F32)
            for g in range(G_PER):
                mg = (lane // HD) == g
                qm = jnp.where(mg, qi, jnp.zeros_like(qi))
                dom = jnp.where(mg, doi, jnp.zeros_like(doi))
                delta = jnp.sum(jnp.where(mg, prod, 0.0), axis=1, keepdims=True)
                p = jnp.exp(_dot_nt(qm, kxi) + bb - lse_ref[rows, g:g + 1])
                ds = (p * (_dot_nt(dom, vxi) - delta)).astype(BF16)
                dvx_ref[0:n, :] += _dot_tn(p.astype(BF16), dom)
                dkx_ref[0:n, :] += _dot_tn(ds, qm)
                dq_acc = jnp.where(mg, _dot(ds, kxi), dq_acc)
            dq_ref[rows, :] = dq_acc

    col = lambda kv: (0, kv)
    return _call(
        body, name="attn_bwd", grid=(NKV,),
        in_specs=[pl.BlockSpec((T, WK), col), pl.BlockSpec((T, WK), col), pl.BlockSpec((T, WK), col), pl.BlockSpec((T, WK), col),
                  pl.BlockSpec((T, WK), col), pl.BlockSpec((T, 128), col), _const((AB, T))],
        out_specs=[pl.BlockSpec((T, WK), col), pl.BlockSpec((T, WK), col), pl.BlockSpec((T, WK), col)],
        out_shape=[jax.ShapeDtypeStruct((T, QW), F32)] * 3,
        args=[q, kx, vx, o, dmix, lse, bias], sem=("arbitrary",), duties=duties)


def _softplus(x):
    return jnp.maximum(x, 0.0) + jnp.log1p(jnp.exp(-jnp.abs(x)))


def _causal_conv(u, zs, cw_ref, cb_ref):
    zs[8:, :] = u
    sh1, sh2, sh3 = (zs[8 - m:8 - m + L, :] for m in (1, 2, 3))
    return cb_ref[...] + cw_ref[3:4, :] * u + cw_ref[2:3, :] * sh1 + cw_ref[1:2, :] * sh2 + cw_ref[0:1, :] * sh3


def _ssd_chunk_common(xc, dtr, dtb_ref, alx_ref, e_ref):
    sg = _sigmoid(xc)
    act = xc * sg
    pre = dtr + dtb_ref[...]
    dt_x = _dot_hi(_softplus(pre), e_ref[...])
    a_x = -jnp.exp(alx_ref[...])
    ri = lax.broadcasted_iota(jnp.int32, (L, L), 0)
    ci = lax.broadcasted_iota(jnp.int32, (L, L), 1)
    tri = ri >= ci
    acs_x = _dot_hi(tri, dt_x * a_x, a_is_01=True)
    return dict(sg=sg, act=act, pre=pre, dt_x=dt_x, a_x=a_x, tri=tri, acs_x=acs_x)


def _decay(acs_x, acs_t, h, tri):
    col = acs_x[:, HD * h:HD * h + 1]
    row = acs_t[HD * h:HD * h + 1, :]
    return jnp.exp(jnp.where(tri, col - row, NEG))


def _ssd_fwd(xbc, z, dtr, convw, convb, dtb, alx, dskx, ssmn, e, duties=()):
    def body(u_ref, z_ref, dtr_ref, cw_ref, cb_ref, dtb_ref, alx_ref, dsk_ref, sn_ref, e_ref,
             yn_ref, y_ref, hs_ref, xc_ref, zs, hst):
        @pl.when(pl.program_id(0) == 0)
        def _():
            zs[0:8, :] = jnp.zeros((8, CONV_C), F32)
            hst[...] = jnp.zeros_like(hst)

        u = u_ref[...]
        xc = _causal_conv(u, zs, cw_ref, cb_ref)
        xc_ref[...] = xc
        zs[0:8, :] = u[L - 8:, :]
        cm = _ssd_chunk_common(xc, dtr_ref[...], dtb_ref, alx_ref, e_ref)
        act, dt_x, acs_x, tri = cm["act"], cm["dt_x"], cm["acs_x"], cm["tri"]
        xs = act[:, :SSM_W]
        acs_l = acs_x[L - 1:L, :]
        lam_x = jnp.exp(acs_x)
        w_x = jnp.exp(acs_l - acs_x)
        gam_x = jnp.exp(acs_l)
        acs_t = acs_x.T
        xd = xs * dt_x
        xb = xd.astype(BF16)
        xw = (xd * w_x).astype(BF16)
        lo = lax.broadcasted_iota(jnp.int32, (1, 128), 1) < HD
        hs_ref[0] = hst[...]
        pieces = []
        for grp in range(2):
            bb = act[:, SSM_W + SSM_N * grp:SSM_W + SSM_N * (grp + 1)].astype(BF16)
            cb_ = act[:, SSM_W + 2 * SSM_N + SSM_N * grp:SSM_W + 2 * SSM_N + SSM_N * (grp + 1)].astype(BF16)
            cbm = _dot_nt(cb_, bb)
            for jj in range(4):
                j = 4 * grp + jj
                sl = slice(128 * j, 128 * j + 128)
                m0 = (cbm * _decay(acs_x, acs_t, 2 * j, tri)).astype(BF16)
                m1 = (cbm * _decay(acs_x, acs_t, 2 * j + 1, tri)).astype(BF16)
                x2 = xb[:, sl]
                ydiag = jnp.where(lo, _dot(m0, x2), _dot(m1, x2))
                hprev = hst[j]
                yoff = lam_x[:, sl] * _dot(cb_, hprev.astype(BF16))
                pieces.append(ydiag + yoff)
                hst[j] = gam_x[:, sl] * hprev + _dot_tn(bb, xw[:, sl])
        y = jnp.concatenate(pieces, axis=1) + dsk_ref[...] * xs
        y_ref[...] = y
        zv = z_ref[...]
        yz = y * (zv * _sigmoid(zv))
        half = SSM_W // 2
        yn = jnp.concatenate([yz[:, :half] * _rs(yz[:, :half]), yz[:, half:] * _rs(yz[:, half:])], axis=1)
        yn_ref[...] = (yn * sn_ref[...]).astype(BF16)

    return _call(
        body, name="ssd_fwd", grid=(NCH,),
        in_specs=[_rows(L, CONV_C), _rows(L, SSM_W), _rows(L, SSM_H), _const((4, CONV_C)), _const((1, CONV_C)), _const((1, SSM_H)),
                  _const((1, SSM_W)), _const((1, SSM_W)), _const((1, SSM_W)), _const((SSM_H, SSM_W))],
        out_specs=[_rows(L, SSM_W), _rows(L, SSM_W), pl.BlockSpec((1, 8, SSM_N, 128), lambda c: (c, 0, 0, 0)), _rows(L, CONV_C)],
        out_shape=[jax.ShapeDtypeStruct((T, SSM_W), BF16), jax.ShapeDtypeStruct((T, SSM_W), F32),
                   jax.ShapeDtypeStruct((NCH, 8, SSM_N, 128), F32), jax.ShapeDtypeStruct((T, CONV_C), F32)],
        scratch=[pltpu.VMEM((8 + L, CONV_C), F32), pltpu.VMEM((8, SSM_N, 128), F32)],
        args=[xbc, z, dtr, convw, convb, dtb, alx, dskx, ssmn, e], sem=("arbitrary",), duties=duties)


def _ssd_bwd(dmix, xbc, xconv, z, dtr, y, hs, convw, dtb, alx, dskx, ssmn, e, e1, duties=()):
    rev = lambda i: (NCH - 1 - i, 0)

    def body(dyn_ref, u_ref, xc_ref, z_ref, dtr_ref, y_ref, hs_ref, cw_ref, dtb_ref, alx_ref, dsk_ref, sn_ref, e_ref, e1_ref,
             dxbc_ref, dz_ref, ddt_ref, dcw_ref, dcb_ref, dsn_ref, dpar_ref, dh, zd, colbuf):
        step = pl.program_id(0)

        @pl.when(step == 0)
        def _():
            for r in (dh, dcw_ref, dcb_ref, dsn_ref, dpar_ref):
                r[...] = jnp.zeros_like(r)
            zd[L:, :] = jnp.zeros((8, CONV_C), F32)

        u = u_ref[...]
        xc = xc_ref[...]
        cm = _ssd_chunk_common(xc, dtr_ref[...], dtb_ref, alx_ref, e_ref)
        sg, act, pre, dt_x, a_x, tri, acs_x = (cm[k] for k in ("sg", "act", "pre", "dt_x", "a_x", "tri", "acs_x"))
        xs = act[:, :SSM_W]
        acs_l = acs_x[L - 1:L, :]
        lam_x = jnp.exp(acs_x)
        w_x = jnp.exp(acs_l - acs_x)
        gam_x = jnp.exp(acs_l)
        acs_t = acs_x.T
        xd = xs * dt_x
        xb = xd.astype(BF16)
        xdw = xd * w_x
        xw = xdw.astype(BF16)
        lo = lax.broadcasted_iota(jnp.int32, (1, 128), 1) < HD
        row8 = lax.broadcasted_iota(jnp.int32, (8, 1), 0)

        dyn = dyn_ref[...]
        yv = y_ref[...]
        zv = z_ref[...]
        sz = _sigmoid(zv)
        siluz = zv * sz
        yz = yv * siluz
        half = SSM_W // 2
        gy = dyn * sn_ref[...]
        dyz_parts, yzn_parts = [], []
        for hf in range(2):
            part = yz[:, hf * half:(hf + 1) * half]
            r = _rs(part)
            pn = part * r
            gp = gy[:, hf * half:(hf + 1) * half]
            dyz_parts.append(r * (gp - pn * jnp.mean(gp * pn, axis=-1, keepdims=True)))
            yzn_parts.append(pn)
        dyz = jnp.concatenate(dyz_parts, axis=1)
        dsn_ref[...] += jnp.sum(dyn * jnp.concatenate(yzn_parts, axis=1), axis=0, keepdims=True)
        dy = dyz * siluz
        dz_ref[...] = dyz * yv * _dsilu(zv, sz)

        colbuf[...] = jnp.zeros_like(colbuf)
        dx_pieces, dacs_pieces, dacsl_pieces, db_pieces, dc_pieces = [], [], [], [], []
        for grp in range(2):
            bb = act[:, SSM_W + SSM_N * grp:SSM_W + SSM_N * (grp + 1)].astype(BF16)
            cb_ = act[:, SSM_W + 2 * SSM_N + SSM_N * grp:SSM_W + 2 * SSM_N + SSM_N * (grp + 1)].astype(BF16)
            cbm = _dot_nt(cb_, bb)
            dcbm = jnp.zeros((L, L), F32)
            dc_g = jnp.zeros((L, SSM_N), F32)
            db_g = jnp.zeros((L, SSM_N), F32)
            for jj in range(4):
                j = 4 * grp + jj
                sl = slice(128 * j, 128 * j + 128)
                dy2 = dy[:, sl]
                dy2b = dy2.astype(BF16)
                d0 = _decay(acs_x, acs_t, 2 * j, tri)
                d1 = _decay(acs_x, acs_t, 2 * j + 1, tri)
                m0 = cbm * d0
                m1 = cbm * d1
                x2 = xb[:, sl]
                hprev = hs_ref[0, j]
                hprevb = hprev.astype(BF16)
                dhn = dh[j]
                dhnb = dhn.astype(BF16)
                g2 = _dot(bb, dhnb)
                dx_pieces.append(jnp.where(lo, _dot_tn(m0.astype(BF16), dy2b), _dot_tn(m1.astype(BF16), dy2b)) + w_x[:, sl] * g2)
                zero = jnp.zeros_like(dy2b)
                dm0 = _dot_nt(jnp.where(lo, dy2b, zero), x2)
                dm1 = _dot_nt(jnp.where(lo, zero, dy2b), x2)
                dcbm = dcbm + dm0 * d0 + dm1 * d1
                e0 = dm0 * m0
                e1v = dm1 * m1
                colbuf[:, 2 * j:2 * j + 1] = jnp.sum(e0, axis=1, keepdims=True) - jnp.sum(e0.T, axis=1, keepdims=True)
                colbuf[:, 2 * j + 1:2 * j + 2] = jnp.sum(e1v, axis=1, keepdims=True) - jnp.sum(e1v.T, axis=1, keepdims=True)
                yoff = lam_x[:, sl] * _dot(cb_, hprevb)
                gxw = g2 * xdw[:, sl]
                dacs_pieces.append(dy2 * yoff - gxw)
                dacsl_pieces.append(jnp.sum(gxw, axis=0, keepdims=True) + gam_x[:, sl] * jnp.sum(dhn * hprev, axis=0, keepdims=True))
                dyl = (dy2 * lam_x[:, sl]).astype(BF16)
                dc_g = dc_g + _dot_nt(dyl, hprevb)
                db_g = db_g + _dot_nt(xw[:, sl], dhnb)
                dh[j] = gam_x[:, sl] * dhn + _dot_tn(cb_, dyl)
            dcbb = dcbm.astype(BF16)
            dc_pieces.append(dc_g + _dot(dcbb, bb))
            db_pieces.append(db_g + _dot_tn(dcbb, cb_))

        dxd = jnp.concatenate(dx_pieces, axis=1)
        rowi = lax.broadcasted_iota(jnp.int32, (L, 1), 0)
        dacs_x = (jnp.concatenate(dacs_pieces, axis=1) + _dot_hi(colbuf[...], e1_ref[...])
                  + jnp.where(rowi == L - 1, jnp.concatenate(dacsl_pieces, axis=1), 0.0))
        upper = lax.broadcasted_iota(jnp.int32, (L, L), 0) <= lax.broadcasted_iota(jnp.int32, (L, L), 1)
        dadt_x = _dot_hi(upper, dacs_x, a_is_01=True)
        ddt_x = dxd * xs + dadt_x * a_x
        ddtr = _dot_nt_hi(ddt_x, e_ref[...]) * _sigmoid(pre)
        ddt_ref[...] = jnp.zeros_like(ddt_ref)
        ddt_ref[:, 0:SSM_H] = ddtr
        dalx =jnp.sum(dadt_x * dt_x, axis=0, keepdims=True) * a_x
        ddskx = jnp.sum(dy * xs, axis=0, keepdims=True)
        par_x = jnp.where(row8 == 1, dalx, 0.0) + jnp.where(row8 == 2, ddskx, 0.0)
        dpar_ref[...] += _dot_nt_hi(par_x, e_ref[...]) + jnp.where(row8 == 0, jnp.sum(ddtr, axis=0, keepdims=True), 0.0)

        dxs = dxd * dt_x + dsk_ref[...] * dy
        dact = jnp.concatenate([dxs] + db_pieces + dc_pieces, axis=1)
        du = dact * _dsilu(xc, sg)
        dcb_ref[...] += jnp.sum(du, axis=0, keepdims=True)
        zd[0:L, :] = du
        f1, f2, f3 = (zd[m:m + L, :] for m in (1, 2, 3))
        dxbc_ref[...] = cw_ref[3:4, :] * du + cw_ref[2:3, :] * f1 + cw_ref[1:2, :] * f2 + cw_ref[0:1, :] * f3
        dcw = jnp.zeros((8, CONV_C), F32)
        for k, shifted in enumerate((f3, f2, f1, du)):
            dcw = dcw + jnp.where(row8 == k, jnp.sum(shifted * u, axis=0, keepdims=True), 0.0)
        dcw_ref[...] += dcw
        zd[L:, :] = du[:8, :]

    return _call(
        body, name="ssd_bwd", grid=(NCH,),
        in_specs=[pl.BlockSpec((L, SSM_W), lambda i: (NCH - 1 - i, 1)), pl.BlockSpec((L, CONV_C), rev), pl.BlockSpec((L, CONV_C), rev),
                  pl.BlockSpec((L, SSM_W), rev), pl.BlockSpec((L, SSM_H), rev), pl.BlockSpec((L, SSM_W), rev),
                  pl.BlockSpec((1, 8, SSM_N, 128), lambda i: (NCH - 1 - i, 0, 0, 0)),
                  _const((4, CONV_C)), _const((1, SSM_H)), _const((1, SSM_W)), _const((1, SSM_W)), _const((1, SSM_W)),
                  _const((SSM_H, SSM_W)), _const((128, SSM_W))],
        out_specs=[pl.BlockSpec((L, CONV_C), rev), pl.BlockSpec((L, SSM_W), rev), pl.BlockSpec((L, 128), rev),
                   _const((8, CONV_C)), _const((1, CONV_C)), _const((1, SSM_W)), _const((8, SSM_H))],
        out_shape=[jax.ShapeDtypeStruct((T, CONV_C), F32), jax.ShapeDtypeStruct((T, SSM_W), F32), jax.ShapeDtypeStruct((T, 128), F32),
                   jax.ShapeDtypeStruct((8, CONV_C), F32), jax.ShapeDtypeStruct((1, CONV_C), F32), jax.ShapeDtypeStruct((1, SSM_W), F32),
                   jax.ShapeDtypeStruct((8, SSM_H), F32)],
        scratch=[pltpu.VMEM((8, SSM_N, 128), F32), pltpu.VMEM((L + 8, CONV_C), F32), pltpu.VMEM((L, 128), F32)],
        args=[dmix, xbc, xconv, z, dtr, y, hs, convw, dtb, alx, dskx, ssmn, e, e1], sem=("arbitrary",), duties=duties)


def _adam_math(w, g, m, v):
    m = ADAM_B1 * m + (1.0 - ADAM_B1) * g
    v = ADAM_B2 * v + (1.0 - ADAM_B2) * (g * g)
    m_hat = m / (1.0 - ADAM_B1 ** ADAM_STEP)
    v_hat = v / (1.0 - ADAM_B2 ** ADAM_STEP)
    delta = -ADAM_LR * (m_hat / (jnp.sqrt(v_hat) + ADAM_EPS) + ADAM_WD * w)
    return delta, m, v


PACK_W = CONV_C


def _adamw_small(ws, ms, vs, total, conv_g, after=()):
    k = len(ws)

    def body(*refs):
        ins, outs = refs[:3 * k + 2], refs[3 * k + 2 + len(after):]
        total_ref, conv_ref = ins[3 * k], ins[3 * k + 1]
        for a in range(k):
            w_ref, m_ref, v_ref = ins[3 * a:3 * a + 3]
            g = conv_ref[...] if a == k - 1 else total_ref[a:a + 1, 0:w_ref.shape[1]]
            delta, nm, nv = _adam_math(w_ref[...], g, m_ref[...], v_ref[...])
            for ref, val in zip(outs[4 * a:4 * a + 4], (g, delta, nm, nv)):
                ref[...] = val

    vm = pl.BlockSpec(memory_space=pltpu.VMEM)
    args = [x for a in range(k) for x in (ws[a], ms[a], vs[a])] + [total, conv_g]
    res = pl.pallas_call(
        body, name="adamw_small", in_specs=[vm] * len(args) + [ANY] * len(after), out_specs=[vm] * (4 * k),
        out_shape=[jax.ShapeDtypeStruct(ws[a].shape, F32) for a in range(k) for _ in range(4)],
    )(*args, *after)
    return [tuple(res[4 * a:4 * a + 4]) for a in range(k)]


COL_TILE = 512


def _adamw_sharded(ws, ms, vs, chip_sums, from_chips, other_chips, name, after=()):
    k = len(ws)
    rows, cols = ws[0].shape
    prow = chip_sums[0].shape[0]
    assert cols % COL_TILE == 0 and prow >= rows and all(a.shape == ws[0].shape for a in ws)

    def body(ids_ref, *refs):
        ins, outs = refs[:7 * k], refs[7 * k + len(after):]
        for a in range(k):
            w_ref, m_ref, v_ref, s_ref, r1_ref, r2_ref, r3_ref = ins[7 * a:7 * a + 7]
            g = s_ref[...]
            for r in (r1_ref, r2_ref, r3_ref):
                g = g + r[0].astype(F32)
            g = g[:rows]
            delta, nm, nv = _adam_math(w_ref[...], g, m_ref[...], v_ref[...])
            for ref, val in zip(outs[4 * a:4 * a + 4], (g, delta, nm, nv)):
                ref[...] = val

    spec = pl.BlockSpec((rows, COL_TILE), lambda i, ids: (0, i))
    part = lambda j: pl.BlockSpec((1, prow, COL_TILE), lambda i, ids: (ids[j], 0, i))
    one = [spec, spec, spec, pl.BlockSpec((prow, COL_TILE), lambda i, ids: (0, i)), part(0), part(1), part(2)]
    args = [x for a in range(k) for x in (ws[a], ms[a], vs[a], chip_sums[a], from_chips[a], from_chips[a], from_chips[a])]
    res = pl.pallas_call(
        body, name=name,
        grid_spec=pltpu.PrefetchScalarGridSpec(
            num_scalar_prefetch=1, grid=(cols // COL_TILE,),
            in_specs=one * k + [ANY] * len(after), out_specs=[spec] * (4 * k)),
        out_shape=[jax.ShapeDtypeStruct((rows, cols), F32)] * (4 * k),
        compiler_params=_params(("parallel",)),
    )(other_chips, *args, *after)
    return [tuple(res[4 * a:4 * a + 4]) for a in range(k)]


def _chip_sum(mines, recvs, name, duties=(), after=()):
    k = len(mines)
    rows, cols = mines[0].shape[1:]

    def body(*refs):
        own_chip = pl.program_id(0) == 2 * lax.axis_index("x") + lax.axis_index("y")
        for a in range(k):
            a_ref, b_ref = refs[2 * a:2 * a + 2]
            s_ref, sb_ref = refs[2 * k + 2 * a:2 * k + 2 * a + 2]
            s = a_ref[0] + b_ref[0].astype(F32)
            sb_ref[0] = s.astype(BF16)

            @pl.when(own_chip)
            def _(s_ref=s_ref, s=s):
                s_ref[...] = s

    by_chip = pl.BlockSpec((1, rows, cols), lambda c: (c, 0, 0))
    res, got = _call(
        body, name=name, grid=(N_DEV // 2,),
        in_specs=[by_chip, by_chip] * k, out_specs=[_const((rows, cols)), by_chip] * k,
        out_shape=[jax.ShapeDtypeStruct((rows, cols), F32), jax.ShapeDtypeStruct((N_DEV // 2, rows, cols), BF16)] * k,
        args=[x for pair in zip(mines, recvs) for x in pair], sem=("arbitrary",), duties=duties, after=after)
    return [tuple(res[2 * a:2 * a + 2]) for a in range(k)], got


def _all_reduce_small(v, after=()):
    rows = v.shape[0]

    def body(v_ref, *rest):
        out_ref, gath, send_sems, recv_sems = rest[len(after):]
        x, y, c = _place()
        me, sibling = (x, y, c), (x, y, 1 - c)
        chips = [(1 - x, y), (x, 1 - y), (1 - x, 1 - y)]

        def blk(px, py, pc):
            return gath.at[pl.ds((4 * px + 2 * py + pc) * rows, rows), :]

        def copy(k, block, to, src=None):
            return pltpu.make_async_remote_copy(src_ref=blk(*block) if src is None else src, dst_ref=blk(*block),
                                                send_sem=send_sems.at[k], recv_sem=recv_sems.at[k], device_id=to, device_id_type=MESH)

        gath[pl.ds((4 * x + 2 * y + c) * rows, rows), :] = v_ref[...]
        first = [copy(0, me, sibling, src=v_ref)] + [copy(1 + j, me, (*chip, c), src=v_ref) for j, chip in enumerate(chips)]
        for cp in first:
            cp.start()
        passed = [copy(4 + j, (*chip, c), sibling) for j, chip in enumerate(chips)]
        for j, chip in enumerate(chips):
            copy(1 + j, (*chip, c), me).wait_recv()
            passed[j].start()
        copy(0, sibling, me).wait_recv()
        for j, chip in enumerate(chips):
            copy(4 + j, (*chip, 1 - c), me).wait_recv()
        for cp in first + passed:
            cp.wait_send()
        acc = gath[0:rows, :]
        for d in range(1, N_DEV):
            acc = acc + gath[d * rows:(d + 1) * rows, :]
        out_ref[...] = acc

    vm = pl.BlockSpec(memory_space=pltpu.VMEM)
    return pl.pallas_call(
        body, name="all_reduce_small",
        in_specs=[vm] + [ANY] * len(after), out_specs=vm,
        out_shape=jax.ShapeDtypeStruct(v.shape, F32),
        scratch_shapes=[pltpu.VMEM((N_DEV * rows, v.shape[1]), F32), pltpu.SemaphoreType.DMA((7,)), pltpu.SemaphoreType.DMA((7,))],
    )(v, *after)


def _rope_tables(positions):
    inv_freq = ROPE_THETA ** (-jnp.arange(0, HD, 2, dtype=F32) / HD)
    ang = positions.reshape(T).astype(F32)[:, None] * inv_freq
    ang = jnp.concatenate([ang, ang, ang, ang], axis=-1)
    lo_half = (jnp.arange(128) % HD) < (HD // 2)
    return jnp.cos(ang), jnp.where(lo_half, -jnp.sin(ang), jnp.sin(ang))


def _selectors():
    lane = jnp.arange(QW)
    e = (lane[None, :] // HD == jnp.arange(SSM_H)[:, None]).astype(F32)
    e1 = ((lane[None, :] == HD * jnp.arange(128)[:, None]) & (jnp.arange(128)[:, None] < SSM_H)).astype(F32)
    src = jnp.arange(KVW)
    ex = ((lane[None, :] // (HD * (NQ // NKV)) == src[:, None] // HD) & (lane[None, :] % HD == src[:, None] % HD)).astype(F32)
    return e, e1, ex


WEIGHTS = ['ffn1_pre_norm', 'ffn1_w_gate', 'ffn1_w_up', 'ffn1_w_down', 'ffn1_post_norm', 'mix_pre_norm', 'w_in', 'conv_w', 'conv_b',
           'dt_bias', 'a_log', 'd_skip', 'ssm_norm', 'w_out', 'mix_post_norm', 'ffn2_pre_norm', 'ffn2_w_gate', 'ffn2_w_up',
           'ffn2_w_down', 'ffn2_post_norm']
COL_SHARDED = ['ffn1_w_gate', 'ffn1_w_up', 'ffn2_w_gate', 'ffn2_w_up', 'w_in']
ROW_SHARDED = ['ffn1_w_down', 'ffn2_w_down', 'w_out']
BIG = COL_SHARDED + ROW_SHARDED
FFN_BIG = COL_SHARDED[:4] + ROW_SHARDED[:2]
SMALL = ['ffn1_pre_norm', 'ffn1_post_norm', 'mix_pre_norm', 'conv_b', 'dt_bias', 'a_log', 'd_skip', 'ssm_norm', 'mix_post_norm',
         'ffn2_pre_norm', 'ffn2_post_norm']
FFN1 = ['ffn1_w_gate', 'ffn1_w_up', 'ffn1_w_down']
FFN2 = ['ffn2_w_gate', 'ffn2_w_up', 'ffn2_w_down']


def _wire_block(name, a):
    if name in FFN_BIG:
        return jnp.pad(a.astype(BF16), ((0, FSH - FSR), (0, 0)))
    if name == "w_in":
        return jnp.pad(a.astype(BF16), ((0, ISW - ISR), (0, 0)))
    return a if name == "conv_w" else a.astype(BF16)


def _whole_from_gathered(name, a):
    if name == "conv_w":
        return jnp.transpose(a, (1, 0, 2)).reshape(a.shape[1], -1)
    return a.reshape(-1, D)


def _step(x, positions, target, small, blocks=None, whole=None):
    dist = blocks is not None
    core = "mesh" if dist else 0
    w = dict(small)
    if whole:
        w.update(whole)

    def gather(names):
        return [_gather_duty([_wire_block(n, blocks[n]) for n in names])] if dist else []

    def put(names, results):
        if dist:
            for n, r in zip(names, results[0]):
                w[n] = _whole_from_gathered(n, r)

    g, sums, red = {}, {}, {}

    def swap(names):
        return [_swap_duty([g[n][1] for n in names])] if dist else []

    def chip_sums(names, from_sibling, duties=(), after=()):
        if dist:
            res, got = _chip_sum([g[n][0] for n in names], list(from_sibling), "chip_sum_" + names[0], duties, after)
            sums.update(zip(names, res))
            return got

    def exchange(names):
        return [_exchange_duty([sums[n][1] for n in names])] if dist else []

    def reduced(names, from_chips):
        if dist:
            for n, recv in zip(names, from_chips):
                red[n] = (sums[n][0], recv)

    cos, sin_s = _rope_tables(positions)
    e, e1, exf = _selectors()
    bias = _attn_bias()
    alx = jnp.repeat(w["a_log"], HD, axis=1)
    dskx = jnp.repeat(w["d_skip"], HD, axis=1)

    if dist:
        put(FFN1, _comm_only(gather(FFN1), "gather_ffn1"))
    (x1, n1, a1, b1, hm1, h1), got = _ffn_fwd(x, w["ffn1_pre_norm"], w["ffn1_w_gate"], w["ffn1_w_up"], w["ffn1_w_down"],
                                              w["ffn1_post_norm"], "ffn1_fwd", gather(["w_in", "conv_w"]))
    put(["w_in", "conv_w"], got)
    (n2, q, kx, vx, xbc, z, dtr), got = _inproj_fwd(x1, w["mix_pre_norm"], w["w_in"], cos, sin_s, exf.astype(BF16), gather(["w_out"]))
    put(["w_out"], got)
    (attn, lse), got = _attn_fwd(q, kx, vx, bias, gather(FFN2[:2]))
    put(FFN2[:2], got)
    (yn, y, hs, xconv), got = _ssd_fwd(xbc, z, dtr, w["conv_w"], w["conv_b"], w["dt_bias"], alx, dskx, w["ssm_norm"], e, gather(FFN2[2:]))
    put(FFN2[2:], got)
    x2, h2 = _outproj_fwd(x1, attn, yn, w["w_out"], w["mix_post_norm"])
    (dx3, n3, a3, b3, hm3, h3, ss), _ = _ffn_fwd(x2, w["ffn2_pre_norm"], w["ffn2_w_gate"], w["ffn2_w_up"], w["ffn2_w_down"],
                                                 w["ffn2_post_norm"], "ffn2_fwd", target=target)

    (dx2, da3, db3, dh3, g["ffn2_pre_norm"], g["ffn2_post_norm"]), _ = _ffn_bwd(
        dx3, x2, a3, b3, h3, w["ffn2_pre_norm"], w["ffn2_post_norm"], w["ffn2_w_gate"], w["ffn2_w_up"], w["ffn2_w_down"], "ffn2_bwd")
    g["ffn2_w_down"] = _matmul_tn(hm3, dh3, "ffn2_dwd", core)[0]
    g["ffn2_w_gate"] = _matmul_tn(da3, n3, "ffn2_dwg", core)[0]
    g["ffn2_w_up"] = _matmul_tn(db3, n3, "ffn2_dwu", core)[0]

    (dh2, dmix, g["mix_post_norm"]), got = _outproj_bwd(dx2, h2, w["mix_post_norm"], w["w_out"], swap(FFN2))
    chip_sums(FFN2, got[0] if dist else None)
    g["w_out"] = _dwout(attn, yn, dh2, core)
    (dq, dkx, dvx), got = _attn_bwd(q, kx, vx, attn, dmix, lse, bias, exchange(FFN2) + swap(["w_out"]))
    if dist:
        reduced(FFN2, got[0])
        chip_sums(["w_out"], got[1])
    (dxbc, dz, ddt, dcw, g["conv_b"], g["ssm_norm"], dpar), got = _ssd_bwd(
        dmix, xbc, xconv, z, dtr, y, hs, w["conv_w"], w["dt_bias"], alx, dskx, w["ssm_norm"], e, e1, exchange(["w_out"]))
    reduced(["w_out"], got[0] if dist else None)
    g["conv_w"] = dcw[0:4]
    g["dt_bias"], g["a_log"], g["d_skip"] = dpar[0:1], dpar[1:2], dpar[2:3]
    dx1, dproj, g["mix_pre_norm"] = _inproj_bwd(dx2, dq, dkx, dvx, dxbc, dz, ddt, x1, w["mix_pre_norm"], w["w_in"], cos, sin_s, exf)
    g["w_in"] = _matmul_tn(dproj, n2, "dwin", core)[0]

    started = {}

    def start(n):
        started[n] = _exchange_start(sums[n][1], "start_exchange_" + n)
        return [started[n]["token"]]

    after = []
    if dist:
        chip_sums(["w_in"], _comm_only(swap(["w_in"]), "swap_w_in")[0])
        after = start("w_in")
    (dx0, da1, db1, dh1, g["ffn1_pre_norm"], g["ffn1_post_norm"]), _ = _ffn_bwd(
        dx1, x, a1, b1, h1, w["ffn1_pre_norm"], w["ffn1_post_norm"], w["ffn1_w_gate"], w["ffn1_w_up"], w["ffn1_w_down"], "ffn1_bwd",
        after)
    total = None
    if dist:
        widen = lambda a: jnp.pad(a, ((0, 0), (0, PACK_W - a.shape[1])))
        pack = jnp.concatenate([widen(g[n]) for n in SMALL] + [g["conv_w"], widen(ss[:, 0:1])])
        assert pack.shape[0] % 8 == 0
        total = _all_reduce_small(pack, after)
        after = [total]
    g["ffn1_w_down"], _ = _matmul_tn(hm1, dh1, "ffn1_dwd", core, after=after)
    g["ffn1_w_gate"], got = _matmul_tn(da1, n1, "ffn1_dwg", core, duties=swap(["ffn1_w_down"]))
    if dist:
        chip_sums(["ffn1_w_down"], got[0])
        after = start("ffn1_w_down")
    g["ffn1_w_up"], got = _matmul_tn(db1, n1, "ffn1_dwu", core, after=after, duties=swap(["ffn1_w_gate"]))
    if dist:
        got = chip_sums(["ffn1_w_gate"], got[0], duties=swap(["ffn1_w_up"]))
        after = start("ffn1_w_gate")
        chip_sums(["ffn1_w_up"], got[0], after=after)
        start("ffn1_w_up")
    return ss, dx0, g, red, {n: (sums[n][0], started[n]) for n in started}, total


def kernel(x, positions, ffn1_pre_norm, ffn1_w_gate, ffn1_w_up, ffn1_w_down, ffn1_post_norm, mix_pre_norm, w_in, conv_w, conv_b, dt_bias, a_log, d_skip, ssm_norm, w_out, mix_post_norm, ffn2_pre_norm, ffn2_w_gate, ffn2_w_up, ffn2_w_down, ffn2_post_norm, loss_target, m_ffn1_pre_norm, m_ffn1_w_gate, m_ffn1_w_up, m_ffn1_w_down, m_ffn1_post_norm, m_mix_pre_norm, m_w_in, m_conv_w, m_conv_b, m_dt_bias, m_a_log, m_d_skip, m_ssm_norm, m_w_out, m_mix_post_norm, m_ffn2_pre_norm, m_ffn2_w_gate, m_ffn2_w_up, m_ffn2_w_down, m_ffn2_post_norm, v_ffn1_pre_norm, v_ffn1_w_gate, v_ffn1_w_up, v_ffn1_w_down, v_ffn1_post_norm, v_mix_pre_norm, v_w_in, v_conv_w, v_conv_b, v_dt_bias, v_a_log, v_d_skip, v_ssm_norm, v_w_out, v_mix_post_norm, v_ffn2_pre_norm, v_ffn2_w_gate, v_ffn2_w_up, v_ffn2_w_down, v_ffn2_post_norm):
    given = dict(locals())
    drop = lambda n, a: a if n in SMALL else (a[0].T if n in COL_SHARDED else a[0])
    w = {n: drop(n, given[n]) for n in WEIGHTS}
    m = {n: drop(n, given["m_" + n]) for n in WEIGHTS}
    v = {n: drop(n, given["v_" + n]) for n in WEIGHTS}
    cx, cy, cc = _place()
    others = [2 * (1 - cx) + cy, 2 * cx + (1 - cy), 2 * (1 - cx) + (1 - cy)]

    _, grad_x, g, red, pending, total = _step(x[0], positions, loss_target[0], {n: w[n] for n in SMALL},
                                              blocks={n: w[n] for n in BIG + ["conv_w"]})
    chip_ids = jnp.stack(others).astype(jnp.int32)
    out_g, out_d, out_m, out_v = {}, {}, {}, {}

    def update(names, sums, recvs, label, after=()):
        res = _adamw_sharded([w[n] for n in names], [m[n] for n in names], [v[n] for n in names], sums, recvs, chip_ids,
                             "adamw_" + label, after)
        for n, (gn, dn, mn, vn) in zip(names, res):
            out_g[n], out_d[n], out_m[n], out_v[n] = gn, dn, mn, vn

    last_start = [pending["ffn1_w_up"][1]["token"]]
    update(FFN2, [red[n][0] for n in FFN2], [red[n][1] for n in FFN2], "ffn2", last_start)
    update(["w_out"], [red["w_out"][0]], [red["w_out"][1]], "w_out", last_start)

    n_small = len(SMALL)
    conv_g = lax.dynamic_slice_in_dim(total[n_small:n_small + 4], (4 * cx + 2 * cy + cc) * (CONV_C // N_DEV), CONV_C // N_DEV, axis=1)
    loss = 0.5 * total[n_small + 4, 0] / D
    names = SMALL + ["conv_w"]
    res = _adamw_small([w[n] for n in names], [m[n] for n in names], [v[n] for n in names], total, conv_g, last_start)
    for n, (gn, dn, mn, vn) in zip(names, res):
        out_g[n], out_d[n], out_m[n], out_v[n] = gn, dn, mn, vn
    done = [out_v[n] for n in FFN2 + ["w_out", "conv_w"]]
    update(["w_in"], [pending["w_in"][0]], [_exchange_wait(pending["w_in"][1], done, "wait_exchange_w_in")], "w_in")
    done = [out_v["w_in"]]
    update(FFN1, [pending[n][0] for n in FFN1], [_exchange_wait(pending[n][1], done, "wait_exchange_" + n) for n in FFN1], "ffn1")

    outs = [loss, grad_x[None]]
    for d in (out_g, out_d, out_m, out_v):
        outs += [d[n] if n in SMALL else (d[n].T[None] if n in COL_SHARDED else d[n][None]) for n in WEIGHTS]
    return tuple(outs)
```

```python
import functools

import jax
import jax.numpy as jnp
from jax import lax
from jax.experimental import pallas as pl
from jax.experimental.pallas import tpu as pltpu

F32 = jnp.float32
BF16 = jnp.bfloat16
MESH = pl.DeviceIdType.MESH

N_DEV = 8
T = 2048
D = 1024
FF = 2816
FSR = FF // N_DEV
FSH = 384
FFP = N_DEV * FSH
HD = 64
NQ = 16
NKV = 4
QW = NQ * HD
KVW = NKV * HD
SSM_W = 1024
SSM_H = 16
SSM_N = 128
CONV_C = SSM_W + 2 * 2 * SSM_N
IN_COLS = 4112
INP = 4224
ISR = IN_COLS // N_DEV
ISW = 528
ISG = 640
L = 128
NCH = T // L
AB = 256
NAB = T // AB
EPS = 1e-6
NEG = -1e30
ROPE_THETA = 10000.0
DILATIONS = ((128, 1), (512, 4), (2048, 16))

ADAM_LR = 0.001
ADAM_B1 = 0.9
ADAM_B2 = 0.999
ADAM_EPS = 1e-08
ADAM_WD = 0.01
ADAM_STEP = 10

VMEM_LIMIT = 58 * 1024 * 1024


def _params(sem, vmem=VMEM_LIMIT):
    return pltpu.CompilerParams(dimension_semantics=sem, vmem_limit_bytes=vmem)


def _dot(a, b):
    return jnp.dot(a, b, preferred_element_type=F32)


def _dot_nt(a, b):
    return lax.dot_general(a, b, (((1,), (1,)), ((), ())), preferred_element_type=F32)


def _dot_tn(a, b):
    return lax.dot_general(a, b, (((0,), (0,)), ((), ())), preferred_element_type=F32)


def _split3(x):
    hi = x.astype(BF16)
    r1 = x - hi.astype(F32)
    mid = r1.astype(BF16)
    lo = (r1 - mid.astype(F32)).astype(BF16)
    return hi, mid, lo


def _dot_hi(a, b, a_is_01=False):
    if a_is_01:
        sel = a.astype(BF16)
        return sum(_dot(sel, p) for p in _split3(b))
    sel = b.astype(BF16)
    return sum(_dot(p, sel) for p in _split3(a))


def _dot_nt_hi(a, b):
    sel = b.astype(BF16)
    return sum(_dot_nt(p, sel) for p in _split3(a))


def _rs(x):
    return lax.rsqrt(jnp.mean(x * x, axis=-1, keepdims=True) + EPS)


def _sigmoid(x):
    return jax.nn.sigmoid(x)


def _dsilu(x, s):
    return s * (1.0 + x * (1.0 - s))


def _resident(shape):
    nd = len(shape)
    return pl.BlockSpec(shape, lambda *_: (0,) * nd, pipeline_mode=pl.Buffered(1))


def _const(shape):
    nd = len(shape)
    return pl.BlockSpec(shape, lambda *_: (0,) * nd)


def _rows(tm, cols):
    return pl.BlockSpec((tm, cols), lambda i: (i, 0))


ANY = pl.BlockSpec(memory_space=pl.ANY)


def _place():
    return lax.axis_index("x"), lax.axis_index("y"), lax.axis_index("c")


def _gather_duty(arrays):
    n = len(arrays)
    results = [jax.ShapeDtypeStruct((N_DEV,) + a.shape, a.dtype) for a in arrays]

    def make(ins, outs, send_sems, recv_sems, local_sems):
        x, y, c = _place()
        me, sibling = (x, y, c), (x, y, 1 - c)
        chips = [(1 - x, y), (x, 1 - y), (1 - x, 1 - y)]

        def place_of(a, px, py, pc):
            return outs[a].at[4 * px + 2 * py + pc]

        def copy(a, k, block, to, src=None):
            dst = place_of(a, *block)
            return pltpu.make_async_remote_copy(src_ref=dst if src is None else src, dst_ref=dst,
                                                send_sem=send_sems.at[7 * a + k], recv_sem=recv_sems.at[7 * a + k],
                                                device_id=to, device_id_type=MESH)

        def own(a):
            return pltpu.make_async_copy(ins[a], place_of(a, *me), local_sems.at[a])

        def first(a):
            return [copy(a, 0, me, sibling, src=ins[a])] + [copy(a, 1 + j, me, (*chip, c), src=ins[a]) for j, chip in enumerate(chips)]

        def start():
            for a in range(n):
                own(a).start()
            for a in range(n):
                for cp in first(a):
                    cp.start()

        def finish():
            for j, chip in enumerate(chips):
                for a in range(n):
                    copy(a, 1 + j, (*chip, c), me).wait_recv()
                    copy(a, 4 + j, (*chip, c), sibling).start()
            for a in range(n):
                copy(a, 0, sibling, me).wait_recv()
                for j, chip in enumerate(chips):
                    copy(a, 4 + j, (*chip, 1 - c), me).wait_recv()
            for a in range(n):
                for cp in first(a) + [copy(a, 4 + j, (*chip, c), sibling) for j, chip in enumerate(chips)]:
                    cp.wait_send()
                own(a).wait()

        return start, finish

    return dict(operands=list(arrays), results=results, sems=(7 * n, 7 * n, n), make=make)


def _swap_duty(arrays):
    n = len(arrays)
    half = N_DEV // 2
    results = [jax.ShapeDtypeStruct(a.shape, a.dtype) for a in arrays]

    def make(ins, outs, send_sems, recv_sems):
        x, y, c = _place()

        def copies():
            return [pltpu.make_async_remote_copy(src_ref=ins[a].at[k], dst_ref=outs[a].at[k],
                                                 send_sem=send_sems.at[half * a + k], recv_sem=recv_sems.at[half * a + k],
                                                 device_id=(x, y, 1 - c), device_id_type=MESH)
                    for a in range(n) for k in range(half)]

        def start():
            for cp in copies():
                cp.start()

        def finish():
            for cp in copies():
                cp.wait()

        return start, finish

    return dict(operands=list(arrays), results=results, sems=(half * n, half * n), make=make)


def _exchange_duty(arrays):
    n = len(arrays)
    results = [jax.ShapeDtypeStruct(a.shape, a.dtype) for a in arrays]

    def make(ins, outs, send_sems, recv_sems):
        x, y, c = _place()
        chips = [(1 - x, y), (x, 1 - y), (1 - x, 1 - y)]
        my_chip = 2 * x + y

        def sends():
            return [pltpu.make_async_remote_copy(src_ref=ins[a].at[2 * px + py], dst_ref=outs[a].at[my_chip],
                                                 send_sem=send_sems.at[3 * a + j], recv_sem=recv_sems.at[3 * a + j],
                                                 device_id=(px, py, c), device_id_type=MESH)
                    for a in range(n) for j, (px, py) in enumerate(chips)]

        def start():
            for cp in sends():
                cp.start()

        def finish():
            for a in range(n):
                for j, (px, py) in enumerate(chips):
                    pltpu.make_async_remote_copy(src_ref=ins[a].at[my_chip], dst_ref=outs[a].at[2 * px + py],
                                                 send_sem=send_sems.at[3 * a + j], recv_sem=recv_sems.at[3 * a + j],
                                                 device_id=(px, py, c), device_id_type=MESH).wait_recv()
            for cp in sends():
                cp.wait_send()

        return start, finish

    return dict(operands=list(arrays), results=results, sems=(3 * n, 3 * n), make=make)


def _call(body, *, name, grid, in_specs, out_specs, out_shape, args, sem, scratch=(), duties=(), after=()):
    n_in, n_out, n_scr = len(in_specs), len(out_specs), len(scratch)
    sem_shapes = [pltpu.SemaphoreType.DMA((k,)) for d in duties for k in d["sems"]]

    def full(*refs):
        pos = [0]

        def take(k):
            pos[0] += k
            return refs[pos[0] - k:pos[0]]

        ins = take(n_in)
        d_ins = [take(len(d["operands"])) for d in duties]
        take(len(after))
        outs = take(n_out)
        d_outs = [take(len(d["results"])) for d in duties]
        scr = take(n_scr)
        d_sems = [take(len(d["sems"])) for d in duties]
        hooks = [d["make"](di, do, *ds) for d, di, do, ds in zip(duties, d_ins, d_outs, d_sems)]
        if grid and hooks:
            ids = [pl.program_id(k) for k in range(len(grid))]
            first = functools.reduce(jnp.logical_and, [i == 0 for i in ids])
            last = functools.reduce(jnp.logical_and, [i == g - 1 for i, g in zip(ids, grid)])

            @pl.when(first)
            def _():
                for start, _ in hooks:
                    start()

            body(*ins, *outs, *scr)

            @pl.when(last)
            def _():
                for _, finish in hooks:
                    finish()
        else:
            for start, _ in hooks:
                start()
            body(*ins, *outs, *scr)
            for _, finish in hooks:
                finish()

    d_args = [a for d in duties for a in d["operands"]]
    d_res = [r for d in duties for r in d["results"]]
    kwargs = dict(grid=grid) if grid else {}
    res = pl.pallas_call(
        full, name=name, in_specs=list(in_specs) + [ANY] * (len(d_args) + len(after)), out_specs=list(out_specs) + [ANY] * len(d_res),
        out_shape=list(out_shape) + d_res, scratch_shapes=list(scratch) + sem_shapes,
        compiler_params=_params(sem) if grid else None, **kwargs,
    )(*args, *d_args, *after)
    own, rest = list(res[:n_out]), list(res[n_out:])
    by_duty = []
    for d in duties:
        by_duty.append(rest[:len(d["results"])])
        rest = rest[len(d["results"]):]
    return own, by_duty


def _comm_only(duties, name, after=()):
    return _call(lambda: None, name=name, grid=None, in_specs=[], out_specs=[], out_shape=[], args=[], sem=None, duties=duties,
                 after=after)[1]


HBM = pl.BlockSpec(memory_space=pltpu.HBM)
SEMS = pl.BlockSpec(memory_space=pltpu.SEMAPHORE)
SIDE_EFFECT = pltpu.SideEffectType.DATAFLOW_SIDE_EFFECTING
N_OTHER_CHIPS = 3


def _chip_copies(src_ref, land_ref, sems, rows):
    x, y, c = _place()
    chips = [(1 - x, y), (x, 1 - y), (1 - x, 1 - y)]
    part = (lambda ref: ref) if rows is None else (lambda ref: ref.at[pl.ds(0, rows)])
    return [pltpu.make_async_remote_copy(src_ref=part(src_ref.at[2 * px + py]), dst_ref=part(land_ref.at[2 * x + y]),
                                         send_sem=sems[j], recv_sem=sems[N_OTHER_CHIPS + j], device_id=(px, py, c), device_id_type=MESH)
            for j, (px, py) in enumerate(chips)]


def _exchange_start(pb, name, rows=None):
    n_sem = 2 * N_OTHER_CHIPS

    def body(pb_ref, land_ref, *rest):
        for cp in _chip_copies(pb_ref, land_ref, rest[:n_sem], rows):
            cp.start()
        token = rest[n_sem + 2]
        token[...] = jnp.zeros_like(token)

    res = pl.pallas_call(
        body, name=name,
        out_shape=(pltpu.SemaphoreType.DMA(()),) * n_sem + (pltpu.HBM(pb.shape, pb.dtype), pltpu.HBM(pb.shape, pb.dtype),
                                                              jax.ShapeDtypeStruct((8, 128), F32)),
        in_specs=(HBM, HBM), out_specs=(SEMS,) * n_sem + (HBM, HBM, pl.BlockSpec(memory_space=pltpu.VMEM)),
        input_output_aliases={0: n_sem, 1: n_sem + 1},
        compiler_params=pltpu.CompilerParams(has_side_effects=SIDE_EFFECT),
    )(pltpu.with_memory_space_constraint(pb, pltpu.HBM), pltpu.with_memory_space_constraint(lax.empty(pb.shape, pb.dtype), pltpu.HBM))
    return dict(sems=res[:n_sem], src=res[n_sem], land=res[n_sem + 1], token=res[n_sem + 2], rows=rows)


def _exchange_wait(started, after, name):
    n_sem = 2 * N_OTHER_CHIPS

    def body(pb_ref, land_ref, *rest):
        for cp in _chip_copies(pb_ref, land_ref, rest[:n_sem], started["rows"]):
            cp.wait_send()
            cp.wait_recv()

    src, land = started["src"], started["land"]
    return pl.pallas_call(
        body, name=name, out_shape=(pltpu.HBM(src.shape, src.dtype), pltpu.HBM(land.shape, land.dtype)),
        in_specs=(HBM, HBM) + (SEMS,) * n_sem + (ANY,) * len(after), out_specs=(HBM, HBM), input_output_aliases={0: 0, 1: 1},
        compiler_params=pltpu.CompilerParams(has_side_effects=SIDE_EFFECT),
    )(src, land, *started["sems"], *after)[1]


def _ffn_fwd(x, gpre, wg, wu, wd, gpost, name, duties=(), target=None):
    tm = 256
    n_in = 6 if target is None else 7

    def body(*refs):
        x_ref, gpre_ref, wg_ref, wu_ref, wd_ref, gpost_ref = refs[:6]
        xo_ref, n_ref, a_ref, b_ref, hm_ref, h_ref = refs[n_in:n_in + 6]
        xv = x_ref[...]
        n = (xv * _rs(xv) * gpre_ref[...]).astype(BF16)
        a = _dot_nt(n, wg_ref[...])
        b = _dot_nt(n, wu_ref[...])
        hm = (a * _sigmoid(a) * b).astype(BF16)
        h = _dot(hm, wd_ref[...])
        xo = xv + 0.5 * (h * _rs(h) * gpost_ref[...])
        n_ref[...] = n
        a_ref[...] = a.astype(BF16)
        b_ref[...] = b.astype(BF16)
        hm_ref[...] = hm
        h_ref[...] = h
        if target is None:
            xo_ref[...] = xo
        else:
            ss_ref = refs[n_in + 6]

            @pl.when(pl.program_id(0) == 0)
            def _():
                ss_ref[...] = jnp.zeros_like(ss_ref)

            err = xo - refs[6][...]
            xo_ref[...] = err * (1.0 / D)
            ss_ref[...] += jnp.sum(jnp.sum(err * err, axis=1, keepdims=True), axis=0, keepdims=True)

    loss_in = [] if target is None else [_rows(tm, D)]
    loss_out = [] if target is None else [_const((1, 128))]
    loss_shape = [] if target is None else [jax.ShapeDtypeStruct((1, 128), F32)]
    return _call(
        body, name=name, grid=(T // tm,),
        in_specs=[_rows(tm, D), _const((1, D)), _resident((FFP, D)), _resident((FFP, D)), _resident((FFP, D)), _const((1, D))] + loss_in,
        out_specs=[_rows(tm, D), _rows(tm, D), _rows(tm, FFP), _rows(tm, FFP), _rows(tm, FFP), _rows(tm, D)] + loss_out,
        out_shape=[jax.ShapeDtypeStruct((T, D), F32), jax.ShapeDtypeStruct((T, D), BF16), jax.ShapeDtypeStruct((T, FFP), BF16),
                   jax.ShapeDtypeStruct((T, FFP), BF16), jax.ShapeDtypeStruct((T, FFP), BF16), jax.ShapeDtypeStruct((T, D), F32)]
        + loss_shape,
        args=[x, gpre, wg, wu, wd, gpost] + ([] if target is None else [target]), sem=("arbitrary",), duties=duties)


def _ffn_bwd(dxo, x, a, b, h, gpre, gpost, wg, wu, wd, name, after=()):
    tm = 256

    def body(dxo_ref, x_ref, a_ref, b_ref, h_ref, gpre_ref, gpost_ref, wg_ref, wu_ref, wd_ref,
             dx_ref, da_ref, db_ref, dh_ref, dgpre_ref, dgpost_ref):
        @pl.when(pl.program_id(0) == 0)
        def _():
            dgpre_ref[...] = jnp.zeros_like(dgpre_ref)
            dgpost_ref[...] = jnp.zeros_like(dgpost_ref)

        dy = dxo_ref[...]
        h = h_ref[...]
        hn = h * _rs(h)
        r2 = _rs(h)
        dgpost_ref[...] += jnp.sum(0.5 * dy * hn, axis=0, keepdims=True)
        gdy = 0.5 * dy * gpost_ref[...]
        dh = r2 * (gdy - hn * jnp.mean(gdy * hn, axis=-1, keepdims=True))
        dhb = dh.astype(BF16)
        dh_ref[...] = dhb
        dhm = _dot_nt(dhb, wd_ref[...])
        av = a_ref[...].astype(F32)
        bv = b_ref[...].astype(F32)
        sg = _sigmoid(av)
        db = (dhm * (av * sg)).astype(BF16)
        da = (dhm * bv * _dsilu(av, sg)).astype(BF16)
        da_ref[...] = da
        db_ref[...] = db
        dn = _dot(da, wg_ref[...]) + _dot(db, wu_ref[...])
        xv = x_ref[...]
        r = _rs(xv)
        xn = xv * r
        dgpre_ref[...] += jnp.sum(dn * xn, axis=0, keepdims=True)
        gdn = dn * gpre_ref[...]
        dx_ref[...] = dy + r * (gdn - xn * jnp.mean(gdn * xn, axis=-1, keepdims=True))

    return _call(
        body, name=name, grid=(T // tm,),
        in_specs=[_rows(tm, D), _rows(tm, D), _rows(tm, FFP), _rows(tm, FFP), _rows(tm, D), _const((1, D)), _const((1, D)),
                  _resident((FFP, D)), _resident((FFP, D)), _resident((FFP, D))],
        out_specs=[_rows(tm, D), _rows(tm, FFP), _rows(tm, FFP), _rows(tm, D), _const((1, D)), _const((1, D))],
        out_shape=[jax.ShapeDtypeStruct((T, D), F32), jax.ShapeDtypeStruct((T, FFP), BF16), jax.ShapeDtypeStruct((T, FFP), BF16),
                   jax.ShapeDtypeStruct((T, D), BF16), jax.ShapeDtypeStruct((1, D), F32), jax.ShapeDtypeStruct((1, D), F32)],
        args=[dxo, x, a, b, h, gpre, gpost, wg, wu, wd], sem=("arbitrary",), after=after)


def _core_index(core):
    return lax.axis_index("c") if core == "mesh" else core


def _by_core(res, o_ref, ob_ref, core):
    r = res.shape[0] // 2
    c = jnp.asarray(_core_index(core))

    @pl.when(c == 0)
    def _():
        o_ref[0] = res[:r]
        ob_ref[0] = res[r:].astype(BF16)

    @pl.when(c == 1)
    def _():
        o_ref[0] = res[r:]
        ob_ref[0] = res[:r].astype(BF16)


def _matmul_tn(a, b, name, core, after=(), duties=()):
    k, m = a.shape
    n = b.shape[1]
    r = m // N_DEV
    assert m == N_DEV * r and r % 128 == 0

    def body(a_ref, b_ref, o_ref, ob_ref):
        _by_core(_dot_tn(a_ref[...], b_ref[...]), o_ref, ob_ref, core)

    spec = pl.BlockSpec((1, r, n), lambda i: (i, 0, 0))
    return _call(body, name=name, grid=(N_DEV // 2,), in_specs=[pl.BlockSpec((k, 2 * r), lambda i: (0, i)), _resident((k, n))],
                 out_specs=[spec, spec],
                 out_shape=[jax.ShapeDtypeStruct((N_DEV // 2, r, n), F32), jax.ShapeDtypeStruct((N_DEV // 2, r, n), BF16)],
                 args=[a, b], sem=("arbitrary",), duties=duties, after=after)


def _dwout(attn, yn, dh2, core):
    rs = (QW + SSM_W) // N_DEV
    chips = N_DEV // 2

    def body(at_ref, yn_ref, dh_ref, o_ref, ob_ref):
        i = pl.program_id(0)

        @pl.when(i < chips // 2)
        def _():
            _by_core(_dot_tn(at_ref[...], dh_ref[...]), o_ref, ob_ref, core)

        @pl.when(i >= chips // 2)
        def _():
            _by_core(_dot_tn(yn_ref[...], dh_ref[...]), o_ref, ob_ref, core)

    spec = pl.BlockSpec((1, rs, D), lambda i: (i, 0, 0))
    return pl.pallas_call(
        body, name="dwout", grid=(chips,),
        in_specs=[pl.BlockSpec((T, 2 * rs), lambda i: (0, jnp.minimum(i, chips // 2 - 1))),
                  pl.BlockSpec((T, 2 * rs), lambda i: (0, jnp.maximum(i - chips // 2, 0))), _resident((T, D))],
        out_specs=[spec, spec],
        out_shape=[jax.ShapeDtypeStruct((chips, rs, D), F32), jax.ShapeDtypeStruct((chips, rs, D), BF16)],
        compiler_params=_params(("arbitrary",)),
    )(attn, yn, dh2)


def _rope_swap(t, lo_half):
    return jnp.where(lo_half, pltpu.roll(t, 96, 1), pltpu.roll(t, 32, 1))


def _inproj_fwd(x1, gpre, win, cos, sin_s, ex, duties=()):
    tm = 256

    def body(x_ref, g_ref, w_ref, cos_ref, sin_ref, ex_ref, n_ref, q_ref, kx_ref, vx_ref, xbc_ref, z_ref, dt_ref):
        xv = x_ref[...]
        n = (xv * _rs(xv) * g_ref[...]).astype(BF16)
        n_ref[...] = n
        by_dev = _dot_nt(n, w_ref[...])
        proj = jnp.concatenate([by_dev[:, ISW * d:ISW * d + ISR] for d in range(N_DEV)], axis=1)
        cs = cos_ref[...]
        sn = sin_ref[...]
        lo_half = (lax.broadcasted_iota(jnp.int32, (1, 128), 1) % HD) < (HD // 2)

        def rope(t):
            return t * cs + _rope_swap(t, lo_half) * sn

        for j in range(QW // 128):
            t = proj[:, 128 * j:128 * j + 128]
            q_ref[:, 128 * j:128 * j + 128] = (rope(t) * (HD ** -0.5)).astype(BF16)
        k = jnp.concatenate([rope(proj[:, QW + 128 * j:QW + 128 * j + 128]) for j in range(KVW // 128)], axis=1)
        v = proj[:, QW + KVW:QW + 2 * KVW]
        kx_ref[...] = _dot(k.astype(BF16), ex_ref[...]).astype(BF16)
        vx_ref[...] = _dot(v.astype(BF16), ex_ref[...]).astype(BF16)
        c0 = QW + 2 * KVW
        xbc_ref[...] = proj[:, c0:c0 + CONV_C]
        z_ref[...] = proj[:, c0 + CONV_C:c0 + CONV_C + SSM_W]
        dt_ref[...] = proj[:, c0 + CONV_C + SSM_W:IN_COLS]

    return _call(
        body, name="inproj_fwd", grid=(T // tm,),
        in_specs=[_rows(tm, D), _const((1, D)), _resident((INP, D)), _rows(tm, 128), _rows(tm, 128), _const((KVW, QW))],
        out_specs=[_rows(tm, D), _rows(tm, QW), _rows(tm, QW), _rows(tm, QW), _rows(tm, CONV_C), _rows(tm, SSM_W), _rows(tm, SSM_H)],
        out_shape=[jax.ShapeDtypeStruct((T, D), BF16), jax.ShapeDtypeStruct((T, QW), BF16), jax.ShapeDtypeStruct((T, QW), BF16),
                   jax.ShapeDtypeStruct((T, QW), BF16), jax.ShapeDtypeStruct((T, CONV_C), F32), jax.ShapeDtypeStruct((T, SSM_W), F32),
                   jax.ShapeDtypeStruct((T, SSM_H), F32)],
        args=[x1, gpre, win, cos, sin_s, ex], sem=("arbitrary",), duties=duties)


def _inproj_bwd(dres, dq, dkx, dvx, dxbc, dz, ddt, x1, gpre, win, cos, sin_s, exf):
    tm = 256

    def body(dres_ref, dq_ref, dkx_ref, dvx_ref, dxbc_ref, dz_ref, ddt_ref, x_ref, g_ref, w_ref, cos_ref, sin_ref, ex_ref,
             dx_ref, dps_ref, dg_ref, dp_ref):
        @pl.when(pl.program_id(0) == 0)
        def _():
            dg_ref[...] = jnp.zeros_like(dg_ref)

        cs = cos_ref[...]
        sn = sin_ref[...]
        lo_half = (lax.broadcasted_iota(jnp.int32, (1, 128), 1) % HD) < (HD // 2)

        def rope_t(t):
            return t * cs - _rope_swap(t, lo_half) * sn

        for j in range(QW // 128):
            dp_ref[:, 128 * j:128 * j + 128] = rope_t(dq_ref[:, 128 * j:128 * j + 128] * (HD ** -0.5)).astype(BF16)
        dk = _dot_nt_hi(dkx_ref[...], ex_ref[...])
        dv = _dot_nt_hi(dvx_ref[...], ex_ref[...])
        for j in range(KVW // 128):
            dp_ref[:, QW + 128 * j:QW + 128 * j + 128] = rope_t(dk[:, 128 * j:128 * j + 128]).astype(BF16)
        dp_ref[:, QW + KVW:QW + 2 * KVW] = dv.astype(BF16)
        c0 = QW + 2 * KVW
        dp_ref[:, c0:c0 + CONV_C] = dxbc_ref[...].astype(BF16)
        dp_ref[:, c0 + CONV_C:c0 + CONV_C + SSM_W] = dz_ref[...].astype(BF16)
        dp_ref[:, c0 + CONV_C + SSM_W:INP] = ddt_ref[...].astype(BF16)
        pieces = [dp_ref[:, ISR * d:ISR * (d + 1)] for d in range(N_DEV)]
        zw = jnp.zeros((tm, ISW - ISR), BF16)
        zg = jnp.zeros((tm, ISG - ISR), BF16)
        dn = _dot(jnp.concatenate([t for p in pieces for t in (p, zw)], axis=1), w_ref[...])
        for d in range(N_DEV):
            dps_ref[:, ISG * d:ISG * (d + 1)] = jnp.concatenate([pieces[d], zg], axis=1)
        xv = x_ref[...]
        r = _rs(xv)
        xn = xv * r
        dg_ref[...] += jnp.sum(dn * xn, axis=0, keepdims=True)
        gdn = dn * g_ref[...]
        dx_ref[...] = dres_ref[...] + r * (gdn - xn * jnp.mean(gdn * xn, axis=-1, keepdims=True))

    return pl.pallas_call(
        body, name="inproj_bwd", grid=(T // tm,),
        in_specs=[_rows(tm, D), _rows(tm, QW), _rows(tm, QW), _rows(tm, QW), _rows(tm, CONV_C), _rows(tm, SSM_W), _rows(tm, 128),
                  _rows(tm, D), _const((1, D)), _resident((INP, D)), _rows(tm, 128), _rows(tm, 128), _const((KVW, QW))],
        out_specs=[_rows(tm, D), _rows(tm, N_DEV * ISG), _const((1, D))],
        out_shape=[jax.ShapeDtypeStruct((T, D), F32), jax.ShapeDtypeStruct((T, N_DEV * ISG), BF16), jax.ShapeDtypeStruct((1, D), F32)],
        scratch_shapes=[pltpu.VMEM((tm, INP), BF16)],
        compiler_params=_params(("arbitrary",)),
    )(dres, dq, dkx, dvx, dxbc, dz, ddt, x1, gpre, win, cos, sin_s, exf)


def _outproj_fwd(x1, attn, yn, wout, gpost):
    tm = 512

    def body(x_ref, at_ref, yn_ref, w_ref, g_ref, xo_ref, h_ref):
        h = _dot(at_ref[...], w_ref[0:QW, :]) + _dot(yn_ref[...], w_ref[QW:QW + SSM_W, :])
        h_ref[...] = h
        xo_ref[...] = x_ref[...] + h * _rs(h) * g_ref[...]

    return pl.pallas_call(
        body, name="outproj_fwd", grid=(T // tm,),
        in_specs=[_rows(tm, D), _rows(tm, QW), _rows(tm, SSM_W), _resident((QW + SSM_W, D)), _const((1, D))],
        out_specs=[_rows(tm, D), _rows(tm, D)],
        out_shape=[jax.ShapeDtypeStruct((T, D), F32), jax.ShapeDtypeStruct((T, D), F32)],
        compiler_params=_params(("parallel",)),
    )(x1, attn, yn, wout, gpost)


def _outproj_bwd(dx2, h2, gpost, wout, duties=()):
    tm = 512

    def body(dy_ref, h_ref, g_ref, w_ref, dh_ref, dm_ref, dg_ref):
        @pl.when(pl.program_id(0) == 0)
        def _():
            dg_ref[...] = jnp.zeros_like(dg_ref)

        dy = dy_ref[...]
        h = h_ref[...]
        r = _rs(h)
        hn = h * r
        dg_ref[...] += jnp.sum(dy * hn, axis=0, keepdims=True)
        gdy = dy * g_ref[...]
        dh = (r * (gdy - hn * jnp.mean(gdy * hn, axis=-1, keepdims=True))).astype(BF16)
        dh_ref[...] = dh
        dm_ref[...] = _dot_nt(dh, w_ref[...])

    return _call(
        body, name="outproj_bwd", grid=(T // tm,),
        in_specs=[_rows(tm, D), _rows(tm, D), _const((1, D)), _resident((QW + SSM_W, D))],
        out_specs=[_rows(tm, D), _rows(tm, QW + SSM_W), _const((1, D))],
        out_shape=[jax.ShapeDtypeStruct((T, D), BF16), jax.ShapeDtypeStruct((T, QW + SSM_W), F32), jax.ShapeDtypeStruct((1, D), F32)],
        args=[dx2, h2, gpost, wout], sem=("arbitrary",), duties=duties)


def _attn_bias():
    d = jnp.arange(AB)[:, None] - jnp.arange(T)[None, :] + (T - AB)
    cnt = jnp.zeros(d.shape, F32)
    for window, dil in DILATIONS:
        cnt = cnt + ((d >= 0) & (d % dil == 0) & (d <= window)).astype(F32)
    return jnp.where(cnt > 0, jnp.log(jnp.maximum(cnt, 1.0)), NEG)


G_PER = NQ // NKV
WK = G_PER * HD


def _attn_fwd(q, kx, vx, bias, duties=()):
    def body(q_ref, kx_ref, vx_ref, bias_ref, o_ref, lse_ref):
        lane = lax.broadcasted_iota(jnp.int32, (1, WK), 1)
        lse_ref[...] = jnp.zeros_like(lse_ref)
        for i in range(NAB):
            n = (i + 1) * AB
            rows = slice(i * AB, n)
            qi = q_ref[rows, :]
            kxi = kx_ref[0:n, :]
            vxi = vx_ref[0:n, :]
            bb = bias_ref[:, (NAB - 1 - i) * AB:]
            o_acc = jnp.zeros((AB, WK), F32)
            for g in range(G_PER):
                mg = (lane // HD) == g
                s = _dot_nt(jnp.where(mg, qi, jnp.zeros_like(qi)), kxi) + bb
                m = jnp.max(s, axis=1, keepdims=True)
                p = jnp.exp(s - m)
                l = jnp.sum(p, axis=1, keepdims=True)
                o_acc = jnp.where(mg, _dot(p.astype(BF16), vxi) / l, o_acc)
                lse_ref[rows, g:g + 1] = m + jnp.log(l)
            o_ref[rows, :] = o_acc.astype(BF16)

    col = lambda kv: (0, kv)
    return _call(
        body, name="attn_fwd", grid=(NKV,),
        in_specs=[pl.BlockSpec((T, WK), col), pl.BlockSpec((T, WK), col), pl.BlockSpec((T, WK), col), _const((AB, T))],
        out_specs=[pl.BlockSpec((T, WK), col), pl.BlockSpec((T, 128), col)],
        out_shape=[jax.ShapeDtypeStruct((T, QW), BF16), jax.ShapeDtypeStruct((T, NKV * 128), F32)],
        args=[q, kx, vx, bias], sem=("arbitrary",), duties=duties)


def _attn_bwd(q, kx, vx, o, dmix, lse, bias, duties=()):
    def body(q_ref, kx_ref, vx_ref, o_ref, do_ref, lse_ref, bias_ref, dq_ref, dkx_ref, dvx_ref):
        lane = lax.broadcasted_iota(jnp.int32, (1, WK), 1)
        dkx_ref[...] = jnp.zeros_like(dkx_ref)
        dvx_ref[...] = jnp.zeros_like(dvx_ref)
        for i in range(NAB):
            n = (i + 1) * AB
            rows = slice(i * AB, n)
            qi = q_ref[rows, :]
            dof = do_ref[rows, :]
            doi = dof.astype(BF16)
            prod = dof * o_ref[rows, :].astype(F32)
            kxi = kx_ref[0:n, :]
            vxi = vx_ref[0:n, :]
            bb = bias_ref[:, (NAB - 1 - i) * AB:]
            dq_acc = jnp.zeros((AB, WK), F32)
            for g in range(G_PER):
                mg = (lane // HD) == g
                qm = jnp.where(mg, qi, jnp.zeros_like(qi))
                dom = jnp.where(mg, doi, jnp.zeros_like(doi))
                delta = jnp.sum(jnp.where(mg, prod, 0.0), axis=1, keepdims=True)
                p = jnp.exp(_dot_nt(qm, kxi) + bb - lse_ref[rows, g:g + 1])
                ds = (p * (_dot_nt(dom, vxi) - delta)).astype(BF16)
                dvx_ref[0:n, :] += _dot_tn(p.astype(BF16), dom)
                dkx_ref[0:n, :] += _dot_tn(ds, qm)
                dq_acc = jnp.where(mg, _dot(ds, kxi), dq_acc)
            dq_ref[rows, :] = dq_acc

    col = lambda kv: (0, kv)
    return _call(
        body, name="attn_bwd", grid=(NKV,),
        in_specs=[pl.BlockSpec((T, WK), col), pl.BlockSpec((T, WK), col), pl.BlockSpec((T, WK), col), pl.BlockSpec((T, WK), col),
                  pl.BlockSpec((T, WK), col), pl.BlockSpec((T, 128), col), _const((AB, T))],
        out_specs=[pl.BlockSpec((T, WK), col), pl.BlockSpec((T, WK), col), pl.BlockSpec((T, WK), col)],
        out_shape=[jax.ShapeDtypeStruct((T, QW), F32)] * 3,
        args=[q, kx, vx, o, dmix, lse, bias], sem=("arbitrary",), duties=duties)


def _softplus(x):
    return jnp.maximum(x, 0.0) + jnp.log1p(jnp.exp(-jnp.abs(x)))


def _causal_conv(u, zs, cw_ref, cb_ref):
    zs[8:, :] = u
    sh1, sh2, sh3 = (zs[8 - m:8 - m + L, :] for m in (1, 2, 3))
    return cb_ref[...] + cw_ref[3:4, :] * u + cw_ref[2:3, :] * sh1 + cw_ref[1:2, :] * sh2 + cw_ref[0:1, :] * sh3


def _ssd_chunk_common(xc, dtr, dtb_ref, alx_ref, e_ref):
    sg = _sigmoid(xc)
    act = xc * sg
    pre = dtr + dtb_ref[...]
    dt_x = _dot_hi(_softplus(pre), e_ref[...])
    a_x = -jnp.exp(alx_ref[...])
    ri = lax.broadcasted_iota(jnp.int32, (L, L), 0)
    ci = lax.broadcasted_iota(jnp.int32, (L, L), 1)
    tri = ri >= ci
    acs_x = _dot_hi(tri, dt_x * a_x, a_is_01=True)
    return dict(sg=sg, act=act, pre=pre, dt_x=dt_x, a_x=a_x, tri=tri, acs_x=acs_x)


def _decay(acs_x, acs_t, h, tri):
    col = acs_x[:, HD * h:HD * h + 1]
    row = acs_t[HD * h:HD * h + 1, :]
    return jnp.exp(jnp.where(tri, col - row, NEG))


def _ssd_fwd(xbc, z, dtr, convw, convb, dtb, alx, dskx, ssmn, e, duties=()):
    def body(u_ref, z_ref, dtr_ref, cw_ref, cb_ref, dtb_ref, alx_ref, dsk_ref, sn_ref, e_ref,
             yn_ref, y_ref, hs_ref, xc_ref, zs, hst):
        @pl.when(pl.program_id(0) == 0)
        def _():
            zs[0:8, :] = jnp.zeros((8, CONV_C), F32)
            hst[...] = jnp.zeros_like(hst)

        u = u_ref[...]
        xc = _causal_conv(u, zs, cw_ref, cb_ref)
        xc_ref[...] = xc
        zs[0:8, :] = u[L - 8:, :]
        cm = _ssd_chunk_common(xc, dtr_ref[...], dtb_ref, alx_ref, e_ref)
        act, dt_x, acs_x, tri = cm["act"], cm["dt_x"], cm["acs_x"], cm["tri"]
        xs = act[:, :SSM_W]
        acs_l = acs_x[L - 1:L, :]
        lam_x = jnp.exp(acs_x)
        w_x = jnp.exp(acs_l - acs_x)
        gam_x = jnp.exp(acs_l)
        acs_t = acs_x.T
        xd = xs * dt_x
        xb = xd.astype(BF16)
        xw = (xd * w_x).astype(BF16)
        lo = lax.broadcasted_iota(jnp.int32, (1, 128), 1) < HD
        hs_ref[0] = hst[...]
        pieces = []
        for grp in range(2):
            bb = act[:, SSM_W + SSM_N * grp:SSM_W + SSM_N * (grp + 1)].astype(BF16)
            cb_ = act[:, SSM_W + 2 * SSM_N + SSM_N * grp:SSM_W + 2 * SSM_N + SSM_N * (grp + 1)].astype(BF16)
            cbm = _dot_nt(cb_, bb)
            for jj in range(4):
                j = 4 * grp + jj
                sl = slice(128 * j, 128 * j + 128)
                m0 = (cbm * _decay(acs_x, acs_t, 2 * j, tri)).astype(BF16)
                m1 = (cbm * _decay(acs_x, acs_t, 2 * j + 1, tri)).astype(BF16)
                x2 = xb[:, sl]
                ydiag = jnp.where(lo, _dot(m0, x2), _dot(m1, x2))
                hprev = hst[j]
                yoff = lam_x[:, sl] * _dot(cb_, hprev.astype(BF16))
                pieces.append(ydiag + yoff)
                hst[j] = gam_x[:, sl] * hprev + _dot_tn(bb, xw[:, sl])
        y = jnp.concatenate(pieces, axis=1) + dsk_ref[...] * xs
        y_ref[...] = y
        zv = z_ref[...]
        yz = y * (zv * _sigmoid(zv))
        half = SSM_W // 2
        yn = jnp.concatenate([yz[:, :half] * _rs(yz[:, :half]), yz[:, half:] * _rs(yz[:, half:])], axis=1)
        yn_ref[...] = (yn * sn_ref[...]).astype(BF16)

    return _call(
        body, name="ssd_fwd", grid=(NCH,),
        in_specs=[_rows(L, CONV_C), _rows(L, SSM_W), _rows(L, SSM_H), _const((4, CONV_C)), _const((1, CONV_C)), _const((1, SSM_H)),
                  _const((1, SSM_W)), _const((1, SSM_W)), _const((1, SSM_W)), _const((SSM_H, SSM_W))],
        out_specs=[_rows(L, SSM_W), _rows(L, SSM_W), pl.BlockSpec((1, 8, SSM_N, 128), lambda c: (c, 0, 0, 0)), _rows(L, CONV_C)],
        out_shape=[jax.ShapeDtypeStruct((T, SSM_W), BF16), jax.ShapeDtypeStruct((T, SSM_W), F32),
                   jax.ShapeDtypeStruct((NCH, 8, SSM_N, 128), F32), jax.ShapeDtypeStruct((T, CONV_C), F32)],
        scratch=[pltpu.VMEM((8 + L, CONV_C), F32), pltpu.VMEM((8, SSM_N, 128), F32)],
        args=[xbc, z, dtr, convw, convb, dtb, alx, dskx, ssmn, e], sem=("arbitrary",), duties=duties)


def _ssd_bwd(dmix, xbc, xconv, z, dtr, y, hs, convw, dtb, alx, dskx, ssmn, e, e1, duties=()):
    rev = lambda i: (NCH - 1 - i, 0)

    def body(dyn_ref, u_ref, xc_ref, z_ref, dtr_ref, y_ref, hs_ref, cw_ref, dtb_ref, alx_ref, dsk_ref, sn_ref, e_ref, e1_ref,
             dxbc_ref, dz_ref, ddt_ref, dcw_ref, dcb_ref, dsn_ref, dpar_ref, dh, zd, colbuf):
        step = pl.program_id(0)

        @pl.when(step == 0)
        def _():
            for r in (dh, dcw_ref, dcb_ref, dsn_ref, dpar_ref):
                r[...] = jnp.zeros_like(r)
            zd[L:, :] = jnp.zeros((8, CONV_C), F32)

        u = u_ref[...]
        xc = xc_ref[...]
        cm = _ssd_chunk_common(xc, dtr_ref[...], dtb_ref, alx_ref, e_ref)
        sg, act, pre, dt_x, a_x, tri, acs_x = (cm[k] for k in ("sg", "act", "pre", "dt_x", "a_x", "tri", "acs_x"))
        xs = act[:, :SSM_W]
        acs_l = acs_x[L - 1:L, :]
        lam_x = jnp.exp(acs_x)
        w_x = jnp.exp(acs_l - acs_x)
        gam_x = jnp.exp(acs_l)
        acs_t = acs_x.T
        xd = xs * dt_x
        xb = xd.astype(BF16)
        xdw = xd * w_x
        xw = xdw.astype(BF16)
        lo = lax.broadcasted_iota(jnp.int32, (1, 128), 1) < HD
        row8 = lax.broadcasted_iota(jnp.int32, (8, 1), 0)

        dyn = dyn_ref[...]
        yv = y_ref[...]
        zv = z_ref[...]
        sz = _sigmoid(zv)
        siluz = zv * sz
        yz = yv * siluz
        half = SSM_W // 2
        gy = dyn * sn_ref[...]
        dyz_parts, yzn_parts = [], []
        for hf in range(2):
            part = yz[:, hf * half:(hf + 1) * half]
            r = _rs(part)
            pn = part * r
            gp = gy[:, hf * half:(hf + 1) * half]
            dyz_parts.append(r * (gp - pn * jnp.mean(gp * pn, axis=-1, keepdims=True)))
            yzn_parts.append(pn)
        dyz = jnp.concatenate(dyz_parts, axis=1)
        dsn_ref[...] += jnp.sum(dyn * jnp.concatenate(yzn_parts, axis=1), axis=0, keepdims=True)
        dy = dyz * siluz
        dz_ref[...] = dyz * yv * _dsilu(zv, sz)

        colbuf[...] = jnp.zeros_like(colbuf)
        dx_pieces, dacs_pieces, dacsl_pieces, db_pieces, dc_pieces = [], [], [], [], []
        for grp in range(2):
            bb = act[:, SSM_W + SSM_N * grp:SSM_W + SSM_N * (grp + 1)].astype(BF16)
            cb_ = act[:, SSM_W + 2 * SSM_N + SSM_N * grp:SSM_W + 2 * SSM_N + SSM_N * (grp + 1)].astype(BF16)
            cbm = _dot_nt(cb_, bb)
            dcbm = jnp.zeros((L, L), F32)
            dc_g = jnp.zeros((L, SSM_N), F32)
            db_g = jnp.zeros((L, SSM_N), F32)
            for jj in range(4):
                j = 4 * grp + jj
                sl = slice(128 * j, 128 * j + 128)
                dy2 = dy[:, sl]
                dy2b = dy2.astype(BF16)
                d0 = _decay(acs_x, acs_t, 2 * j, tri)
                d1 = _decay(acs_x, acs_t, 2 * j + 1, tri)
                m0 = cbm * d0
                m1 = cbm * d1
                x2 = xb[:, sl]
                hprev = hs_ref[0, j]
                hprevb = hprev.astype(BF16)
                dhn = dh[j]
                dhnb = dhn.astype(BF16)
                g2 = _dot(bb, dhnb)
                dx_pieces.append(jnp.where(lo, _dot_tn(m0.astype(BF16), dy2b), _dot_tn(m1.astype(BF16), dy2b)) + w_x[:, sl] * g2)
                zero = jnp.zeros_like(dy2b)
                dm0 = _dot_nt(jnp.where(lo, dy2b, zero), x2)
                dm1 = _dot_nt(jnp.where(lo, zero, dy2b), x2)
                dcbm = dcbm + dm0 * d0 + dm1 * d1
                e0 = dm0 * m0
                e1v = dm1 * m1
                colbuf[:, 2 * j:2 * j + 1] = jnp.sum(e0, axis=1, keepdims=True) - jnp.sum(e0.T, axis=1, keepdims=True)
                colbuf[:, 2 * j + 1:2 * j + 2] = jnp.sum(e1v, axis=1, keepdims=True) - jnp.sum(e1v.T, axis=1, keepdims=True)
                yoff = lam_x[:, sl] * _dot(cb_, hprevb)
                gxw = g2 * xdw[:, sl]
                dacs_pieces.append(dy2 * yoff - gxw)
                dacsl_pieces.append(jnp.sum(gxw, axis=0, keepdims=True) + gam_x[:, sl] * jnp.sum(dhn * hprev, axis=0, keepdims=True))
                dyl = (dy2 * lam_x[:, sl]).astype(BF16)
                dc_g = dc_g + _dot_nt(dyl, hprevb)
                db_g = db_g + _dot_nt(xw[:, sl], dhnb)
                dh[j] = gam_x[:, sl] * dhn + _dot_tn(cb_, dyl)
            dcbb = dcbm.astype(BF16)
            dc_pieces.append(dc_g + _dot(dcbb, bb))
            db_pieces.append(db_g + _dot_tn(dcbb, cb_))

        dxd = jnp.concatenate(dx_pieces, axis=1)
        rowi = lax.broadcasted_iota(jnp.int32, (L, 1), 0)
        dacs_x = (jnp.concatenate(dacs_pieces, axis=1) + _dot_hi(colbuf[...], e1_ref[...])
                  + jnp.where(rowi == L - 1, jnp.concatenate(dacsl_pieces, axis=1), 0.0))
        upper = lax.broadcasted_iota(jnp.int32, (L, L), 0) <= lax.broadcasted_iota(jnp.int32, (L, L), 1)
        dadt_x = _dot_hi(upper, dacs_x, a_is_01=True)
        ddt_x = dxd * xs + dadt_x * a_x
        ddtr = _dot_nt_hi(ddt_x, e_ref[...]) * _sigmoid(pre)
        ddt_ref[...] = jnp.zeros_like(ddt_ref)
        ddt_ref[:, 0:SSM_H] = ddtr
        dalx =jnp.sum(dadt_x * dt_x, axis=0, keepdims=True) * a_x
        ddskx = jnp.sum(dy * xs, axis=0, keepdims=True)
        par_x = jnp.where(row8 == 1, dalx, 0.0) + jnp.where(row8 == 2, ddskx, 0.0)
        dpar_ref[...] += _dot_nt_hi(par_x, e_ref[...]) + jnp.where(row8 == 0, jnp.sum(ddtr, axis=0, keepdims=True), 0.0)

        dxs = dxd * dt_x + dsk_ref[...] * dy
        dact = jnp.concatenate([dxs] + db_pieces + dc_pieces, axis=1)
        du = dact * _dsilu(xc, sg)
        dcb_ref[...] += jnp.sum(du, axis=0, keepdims=True)
        zd[0:L, :] = du
        f1, f2, f3 = (zd[m:m + L, :] for m in (1, 2, 3))
        dxbc_ref[...] = cw_ref[3:4, :] * du + cw_ref[2:3, :] * f1 + cw_ref[1:2, :] * f2 + cw_ref[0:1, :] * f3
        dcw = jnp.zeros((8, CONV_C), F32)
        for k, shifted in enumerate((f3, f2, f1, du)):
            dcw = dcw + jnp.where(row8 == k, jnp.sum(shifted * u, axis=0, keepdims=True), 0.0)
        dcw_ref[...] += dcw
        zd[L:, :] = du[:8, :]

    return _call(
        body, name="ssd_bwd", grid=(NCH,),
        in_specs=[pl.BlockSpec((L, SSM_W), lambda i: (NCH - 1 - i, 1)), pl.BlockSpec((L, CONV_C), rev), pl.BlockSpec((L, CONV_C), rev),
                  pl.BlockSpec((L, SSM_W), rev), pl.BlockSpec((L, SSM_H), rev), pl.BlockSpec((L, SSM_W), rev),
                  pl.BlockSpec((1, 8, SSM_N, 128), lambda i: (NCH - 1 - i, 0, 0, 0)),
                  _const((4, CONV_C)), _const((1, SSM_H)), _const((1, SSM_W)), _const((1, SSM_W)), _const((1, SSM_W)),
                  _const((SSM_H, SSM_W)), _const((128, SSM_W))],
        out_specs=[pl.BlockSpec((L, CONV_C), rev), pl.BlockSpec((L, SSM_W), rev), pl.BlockSpec((L, 128), rev),
                   _const((8, CONV_C)), _const((1, CONV_C)), _const((1, SSM_W)), _const((8, SSM_H))],
        out_shape=[jax.ShapeDtypeStruct((T, CONV_C), F32), jax.ShapeDtypeStruct((T, SSM_W), F32), jax.ShapeDtypeStruct((T, 128), F32),
                   jax.ShapeDtypeStruct((8, CONV_C), F32), jax.ShapeDtypeStruct((1, CONV_C), F32), jax.ShapeDtypeStruct((1, SSM_W), F32),
                   jax.ShapeDtypeStruct((8, SSM_H), F32)],
        scratch=[pltpu.VMEM((8, SSM_N, 128), F32), pltpu.VMEM((L + 8, CONV_C), F32), pltpu.VMEM((L, 128), F32)],
        args=[dmix, xbc, xconv, z, dtr, y, hs, convw, dtb, alx, dskx, ssmn, e, e1], sem=("arbitrary",), duties=duties)


def _adam_math(w, g, m, v):
    m = ADAM_B1 * m + (1.0 - ADAM_B1) * g
    v = ADAM_B2 * v + (1.0 - ADAM_B2) * (g * g)
    m_hat = m / (1.0 - ADAM_B1 ** ADAM_STEP)
    v_hat = v / (1.0 - ADAM_B2 ** ADAM_STEP)
    delta = -ADAM_LR * (m_hat / (jnp.sqrt(v_hat) + ADAM_EPS) + ADAM_WD * w)
    return delta, m, v


PACK_W = CONV_C


def _adamw_small(ws, ms, vs, total, conv_g, after=()):
    k = len(ws)

    def body(*refs):
        ins, outs = refs[:3 * k + 2], refs[3 * k + 2 + len(after):]
        total_ref, conv_ref = ins[3 * k], ins[3 * k + 1]
        for a in range(k):
            w_ref, m_ref, v_ref = ins[3 * a:3 * a + 3]
            g = conv_ref[...] if a == k - 1 else total_ref[a:a + 1, 0:w_ref.shape[1]]
            delta, nm, nv = _adam_math(w_ref[...], g, m_ref[...], v_ref[...])
            for ref, val in zip(outs[4 * a:4 * a + 4], (g, delta, nm, nv)):
                ref[...] = val

    vm = pl.BlockSpec(memory_space=pltpu.VMEM)
    args = [x for a in range(k) for x in (ws[a], ms[a], vs[a])] + [total, conv_g]
    res = pl.pallas_call(
        body, name="adamw_small", in_specs=[vm] * len(args) + [ANY] * len(after), out_specs=[vm] * (4 * k),
        out_shape=[jax.ShapeDtypeStruct(ws[a].shape, F32) for a in range(k) for _ in range(4)],
    )(*args, *after)
    return [tuple(res[4 * a:4 * a + 4]) for a in range(k)]


COL_TILE = 512


def _adamw_sharded(ws, ms, vs, chip_sums, from_chips, other_chips, name, after=()):
    k = len(ws)
    rows, cols = ws[0].shape
    prow = chip_sums[0].shape[0]
    assert cols % COL_TILE == 0 and prow >= rows and all(a.shape == ws[0].shape for a in ws)

    def body(ids_ref, *refs):
        ins, outs = refs[:7 * k], refs[7 * k + len(after):]
        for a in range(k):
            w_ref, m_ref, v_ref, s_ref, r1_ref, r2_ref, r3_ref = ins[7 * a:7 * a + 7]
            g = s_ref[...]
            for r in (r1_ref, r2_ref, r3_ref):
                g = g + r[0].astype(F32)
            g = g[:rows]
            delta, nm, nv = _adam_math(w_ref[...], g, m_ref[...], v_ref[...])
            for ref, val in zip(outs[4 * a:4 * a + 4], (g, delta, nm, nv)):
                ref[...] = val

    spec = pl.BlockSpec((rows, COL_TILE), lambda i, ids: (0, i))
    part = lambda j: pl.BlockSpec((1, prow, COL_TILE), lambda i, ids: (ids[j], 0, i))
    one = [spec, spec, spec, pl.BlockSpec((prow, COL_TILE), lambda i, ids: (0, i)), part(0), part(1), part(2)]
    args = [x for a in range(k) for x in (ws[a], ms[a], vs[a], chip_sums[a], from_chips[a], from_chips[a], from_chips[a])]
    res = pl.pallas_call(
        body, name=name,
        grid_spec=pltpu.PrefetchScalarGridSpec(
            num_scalar_prefetch=1, grid=(cols // COL_TILE,),
            in_specs=one * k + [ANY] * len(after), out_specs=[spec] * (4 * k)),
        out_shape=[jax.ShapeDtypeStruct((rows, cols), F32)] * (4 * k),
        compiler_params=_params(("parallel",)),
    )(other_chips, *args, *after)
    return [tuple(res[4 * a:4 * a + 4]) for a in range(k)]


def _chip_sum(mines, recvs, name):
    k = len(mines)
    rows, cols = mines[0].shape[1:]

    def body(*refs):
        own_chip = pl.program_id(0) == 2 * lax.axis_index("x") + lax.axis_index("y")
        for a in range(k):
            a_ref, b_ref = refs[2 * a:2 * a + 2]
            s_ref, sb_ref = refs[2 * k + 2 * a:2 * k + 2 * a + 2]
            s = a_ref[0] + b_ref[0].astype(F32)
            sb_ref[0] = s.astype(BF16)

            @pl.when(own_chip)
            def _(s_ref=s_ref, s=s):
                s_ref[...] = s

    by_chip = pl.BlockSpec((1, rows, cols), lambda c: (c, 0, 0))
    res = pl.pallas_call(
        body, name=name, grid=(N_DEV // 2,),
        in_specs=[by_chip, by_chip] * k, out_specs=[_const((rows, cols)), by_chip] * k,
        out_shape=[jax.ShapeDtypeStruct((rows, cols), F32), jax.ShapeDtypeStruct((N_DEV // 2, rows, cols), BF16)] * k,
        compiler_params=_params(("arbitrary",)),
    )(*[x for pair in zip(mines, recvs) for x in pair])
    return [tuple(res[2 * a:2 * a + 2]) for a in range(k)]


def _all_reduce_small(v, after=()):
    rows = v.shape[0]

    def body(v_ref, *rest):
        out_ref, gath, send_sems, recv_sems = rest[len(after):]
        x, y, c = _place()
        me, sibling = (x, y, c), (x, y, 1 - c)
        chips = [(1 - x, y), (x, 1 - y), (1 - x, 1 - y)]

        def blk(px, py, pc):
            return gath.at[pl.ds((4 * px + 2 * py + pc) * rows, rows), :]

        def copy(k, block, to, src=None):
            return pltpu.make_async_remote_copy(src_ref=blk(*block) if src is None else src, dst_ref=blk(*block),
                                                send_sem=send_sems.at[k], recv_sem=recv_sems.at[k], device_id=to, device_id_type=MESH)

        gath[pl.ds((4 * x + 2 * y + c) * rows, rows), :] = v_ref[...]
        first = [copy(0, me, sibling, src=v_ref)] + [copy(1 + j, me, (*chip, c), src=v_ref) for j, chip in enumerate(chips)]
        for cp in first:
            cp.start()
        passed = [copy(4 + j, (*chip, c), sibling) for j, chip in enumerate(chips)]
        for j, chip in enumerate(chips):
            copy(1 + j, (*chip, c), me).wait_recv()
            passed[j].start()
        copy(0, sibling, me).wait_recv()
        for j, chip in enumerate(chips):
            copy(4 + j, (*chip, 1 - c), me).wait_recv()
        for cp in first + passed:
            cp.wait_send()
        acc = gath[0:rows, :]
        for d in range(1, N_DEV):
            acc = acc + gath[d * rows:(d + 1) * rows, :]
        out_ref[...] = acc

    vm = pl.BlockSpec(memory_space=pltpu.VMEM)
    return pl.pallas_call(
        body, name="all_reduce_small",
        in_specs=[vm] + [ANY] * len(after), out_specs=vm,
        out_shape=jax.ShapeDtypeStruct(v.shape, F32),
        scratch_shapes=[pltpu.VMEM((N_DEV * rows, v.shape[1]), F32), pltpu.SemaphoreType.DMA((7,)), pltpu.SemaphoreType.DMA((7,))],
    )(v, *after)


def _rope_tables(positions):
    inv_freq = ROPE_THETA ** (-jnp.arange(0, HD, 2, dtype=F32) / HD)
    ang = positions.reshape(T).astype(F32)[:, None] * inv_freq
    ang = jnp.concatenate([ang, ang, ang, ang], axis=-1)
    lo_half = (jnp.arange(128) % HD) < (HD // 2)
    return jnp.cos(ang), jnp.where(lo_half, -jnp.sin(ang), jnp.sin(ang))


def _selectors():
    lane = jnp.arange(QW)
    e = (lane[None, :] // HD == jnp.arange(SSM_H)[:, None]).astype(F32)
    e1 = ((lane[None, :] == HD * jnp.arange(128)[:, None]) & (jnp.arange(128)[:, None] < SSM_H)).astype(F32)
    src = jnp.arange(KVW)
    ex = ((lane[None, :] // (HD * (NQ // NKV)) == src[:, None] // HD) & (lane[None, :] % HD == src[:, None] % HD)).astype(F32)
    return e, e1, ex


WEIGHTS = ['ffn1_pre_norm', 'ffn1_w_gate', 'ffn1_w_up', 'ffn1_w_down', 'ffn1_post_norm', 'mix_pre_norm', 'w_in', 'conv_w', 'conv_b',
           'dt_bias', 'a_log', 'd_skip', 'ssm_norm', 'w_out', 'mix_post_norm', 'ffn2_pre_norm', 'ffn2_w_gate', 'ffn2_w_up',
           'ffn2_w_down', 'ffn2_post_norm']
COL_SHARDED = ['ffn1_w_gate', 'ffn1_w_up', 'ffn2_w_gate', 'ffn2_w_up', 'w_in']
ROW_SHARDED = ['ffn1_w_down', 'ffn2_w_down', 'w_out']
BIG = COL_SHARDED + ROW_SHARDED
FFN_BIG = COL_SHARDED[:4] + ROW_SHARDED[:2]
SMALL = ['ffn1_pre_norm', 'ffn1_post_norm', 'mix_pre_norm', 'conv_b', 'dt_bias', 'a_log', 'd_skip', 'ssm_norm', 'mix_post_norm',
         'ffn2_pre_norm', 'ffn2_post_norm']
FFN1 = ['ffn1_w_gate', 'ffn1_w_up', 'ffn1_w_down']
FFN2 = ['ffn2_w_gate', 'ffn2_w_up', 'ffn2_w_down']


def _wire_block(name, a):
    if name in FFN_BIG:
        return jnp.pad(a.astype(BF16), ((0, FSH - FSR), (0, 0)))
    if name == "w_in":
        return jnp.pad(a.astype(BF16), ((0, ISW - ISR), (0, 0)))
    return a if name == "conv_w" else a.astype(BF16)


def _whole_from_gathered(name, a):
    if name == "conv_w":
        return jnp.transpose(a, (1, 0, 2)).reshape(a.shape[1], -1)
    return a.reshape(-1, D)


def _step(x, positions, target, small, blocks=None, whole=None):
    dist = blocks is not None
    core = "mesh" if dist else 0
    w = dict(small)
    if whole:
        w.update(whole)

    def gather(names):
        return [_gather_duty([_wire_block(n, blocks[n]) for n in names])] if dist else []

    def put(names, results):
        if dist:
            for n, r in zip(names, results[0]):
                w[n] = _whole_from_gathered(n, r)

    g, sums, red = {}, {}, {}

    def swap(names):
        return [_swap_duty([g[n][1] for n in names])] if dist else []

    def chip_sums(names, from_sibling):
        if dist:
            res = _chip_sum([g[n][0] for n in names], list(from_sibling), "chip_sum_" + names[0])
            sums.update(zip(names, res))

    def exchange(names):
        return [_exchange_duty([sums[n][1] for n in names])] if dist else []

    def reduced(names, from_chips):
        if dist:
            for n, recv in zip(names, from_chips):
                red[n] = (sums[n][0], recv)

    cos, sin_s = _rope_tables(positions)
    e, e1, exf = _selectors()
    bias = _attn_bias()
    alx = jnp.repeat(w["a_log"], HD, axis=1)
    dskx = jnp.repeat(w["d_skip"], HD, axis=1)

    if dist:
        put(FFN1, _comm_only(gather(FFN1), "gather_ffn1"))
    (x1, n1, a1, b1, hm1, h1), got = _ffn_fwd(x, w["ffn1_pre_norm"], w["ffn1_w_gate"], w["ffn1_w_up"], w["ffn1_w_down"],
                                              w["ffn1_post_norm"], "ffn1_fwd", gather(["w_in", "conv_w"]))
    put(["w_in", "conv_w"], got)
    (n2, q, kx, vx, xbc, z, dtr), got = _inproj_fwd(x1, w["mix_pre_norm"], w["w_in"], cos, sin_s, exf.astype(BF16), gather(["w_out"]))
    put(["w_out"], got)
    (attn, lse), got = _attn_fwd(q, kx, vx, bias, gather(FFN2[:2]))
    put(FFN2[:2], got)
    (yn, y, hs, xconv), got = _ssd_fwd(xbc, z, dtr, w["conv_w"], w["conv_b"], w["dt_bias"], alx, dskx, w["ssm_norm"], e, gather(FFN2[2:]))
    put(FFN2[2:], got)
    x2, h2 = _outproj_fwd(x1, attn, yn, w["w_out"], w["mix_post_norm"])
    (dx3, n3, a3, b3, hm3, h3, ss), _ = _ffn_fwd(x2, w["ffn2_pre_norm"], w["ffn2_w_gate"], w["ffn2_w_up"], w["ffn2_w_down"],
                                                 w["ffn2_post_norm"], "ffn2_fwd", target=target)

    (dx2, da3, db3, dh3, g["ffn2_pre_norm"], g["ffn2_post_norm"]), _ = _ffn_bwd(
        dx3, x2, a3, b3, h3, w["ffn2_pre_norm"], w["ffn2_post_norm"], w["ffn2_w_gate"], w["ffn2_w_up"], w["ffn2_w_down"], "ffn2_bwd")
    g["ffn2_w_down"] = _matmul_tn(hm3, dh3, "ffn2_dwd", core)[0]
    g["ffn2_w_gate"] = _matmul_tn(da3, n3, "ffn2_dwg", core)[0]
    g["ffn2_w_up"] = _matmul_tn(db3, n3, "ffn2_dwu", core)[0]

    (dh2, dmix, g["mix_post_norm"]), got = _outproj_bwd(dx2, h2, w["mix_post_norm"], w["w_out"], swap(FFN2))
    chip_sums(FFN2, got[0] if dist else None)
    g["w_out"] = _dwout(attn, yn, dh2, core)
    (dq, dkx, dvx), got = _attn_bwd(q, kx, vx, attn, dmix, lse, bias, exchange(FFN2) + swap(["w_out"]))
    if dist:
        reduced(FFN2, got[0])
        chip_sums(["w_out"], got[1])
    (dxbc, dz, ddt, dcw, g["conv_b"], g["ssm_norm"], dpar), got = _ssd_bwd(
        dmix, xbc, xconv, z, dtr, y, hs, w["conv_w"], w["dt_bias"], alx, dskx, w["ssm_norm"], e, e1, exchange(["w_out"]))
    reduced(["w_out"], got[0] if dist else None)
    g["conv_w"] = dcw[0:4]
    g["dt_bias"], g["a_log"], g["d_skip"] = dpar[0:1], dpar[1:2], dpar[2:3]
    dx1, dproj, g["mix_pre_norm"] = _inproj_bwd(dx2, dq, dkx, dvx, dxbc, dz, ddt, x1, w["mix_pre_norm"], w["w_in"], cos, sin_s, exf)
    g["w_in"] = _matmul_tn(dproj, n2, "dwin", core)[0]

    started = {}

    def start(n):
        started[n] = _exchange_start(sums[n][1], "start_exchange_" + n, FSR if n in FFN_BIG else None)
        return [started[n]["token"]]

    after = []
    if dist:
        chip_sums(["w_in"], _comm_only(swap(["w_in"]), "swap_w_in")[0])
        after = start("w_in")
    (dx0, da1, db1, dh1, g["ffn1_pre_norm"], g["ffn1_post_norm"]), _ = _ffn_bwd(
        dx1, x, a1, b1, h1, w["ffn1_pre_norm"], w["ffn1_post_norm"], w["ffn1_w_gate"], w["ffn1_w_up"], w["ffn1_w_down"], "ffn1_bwd",
        after)
    total = None
    if dist:
        widen = lambda a: jnp.pad(a, ((0, 0), (0, PACK_W - a.shape[1])))
        pack = jnp.concatenate([widen(g[n]) for n in SMALL] + [g["conv_w"], widen(ss[:, 0:1])])
        assert pack.shape[0] % 8 == 0
        total = _all_reduce_small(pack, after)
        after = [total]
    g["ffn1_w_down"], _ = _matmul_tn(hm1, dh1, "ffn1_dwd", core, after=after)
    g["ffn1_w_gate"], got = _matmul_tn(da1, n1, "ffn1_dwg", core, duties=swap(["ffn1_w_down"]))
    if dist:
        chip_sums(["ffn1_w_down"], got[0])
        after = start("ffn1_w_down")
    g["ffn1_w_up"], got = _matmul_tn(db1, n1, "ffn1_dwu", core, after=after, duties=swap(["ffn1_w_gate"]))
    if dist:
        chip_sums(["ffn1_w_gate"], got[0])
        after = start("ffn1_w_gate")
        chip_sums(["ffn1_w_up"], _comm_only(swap(["ffn1_w_up"]), "swap_ffn1_w_up", after=after)[0])
        start("ffn1_w_up")
    return ss, dx0, g, red, {n: (sums[n][0], started[n]) for n in started}, total


def kernel(x, positions, ffn1_pre_norm, ffn1_w_gate, ffn1_w_up, ffn1_w_down, ffn1_post_norm, mix_pre_norm, w_in, conv_w, conv_b, dt_bias, a_log, d_skip, ssm_norm, w_out, mix_post_norm, ffn2_pre_norm, ffn2_w_gate, ffn2_w_up, ffn2_w_down, ffn2_post_norm, loss_target, m_ffn1_pre_norm, m_ffn1_w_gate, m_ffn1_w_up, m_ffn1_w_down, m_ffn1_post_norm, m_mix_pre_norm, m_w_in, m_conv_w, m_conv_b, m_dt_bias, m_a_log, m_d_skip, m_ssm_norm, m_w_out, m_mix_post_norm, m_ffn2_pre_norm, m_ffn2_w_gate, m_ffn2_w_up, m_ffn2_w_down, m_ffn2_post_norm, v_ffn1_pre_norm, v_ffn1_w_gate, v_ffn1_w_up, v_ffn1_w_down, v_ffn1_post_norm, v_mix_pre_norm, v_w_in, v_conv_w, v_conv_b, v_dt_bias, v_a_log, v_d_skip, v_ssm_norm, v_w_out, v_mix_post_norm, v_ffn2_pre_norm, v_ffn2_w_gate, v_ffn2_w_up, v_ffn2_w_down, v_ffn2_post_norm):
    given = dict(locals())
    drop = lambda n, a: a if n in SMALL else (a[0].T if n in COL_SHARDED else a[0])
    w = {n: drop(n, given[n]) for n in WEIGHTS}
    m = {n: drop(n, given["m_" + n]) for n in WEIGHTS}
    v = {n: drop(n, given["v_" + n]) for n in WEIGHTS}
    cx, cy, cc = _place()
    others = [2 * (1 - cx) + cy, 2 * cx + (1 - cy), 2 * (1 - cx) + (1 - cy)]

    _, grad_x, g, red, pending, total = _step(x[0], positions, loss_target[0], {n: w[n] for n in SMALL},
                                              blocks={n: w[n] for n in BIG + ["conv_w"]})
    chip_ids = jnp.stack(others).astype(jnp.int32)
    out_g, out_d, out_m, out_v = {}, {}, {}, {}

    def update(names, sums, recvs, label, after=()):
        res = _adamw_sharded([w[n] for n in names], [m[n] for n in names], [v[n] for n in names], sums, recvs, chip_ids,
                             "adamw_" + label, after)
        for n, (gn, dn, mn, vn) in zip(names, res):
            out_g[n], out_d[n], out_m[n], out_v[n] = gn, dn, mn, vn

    last_start = [pending["ffn1_w_up"][1]["token"]]
    update(FFN2, [red[n][0] for n in FFN2], [red[n][1] for n in FFN2], "ffn2", last_start)
    update(["w_out"], [red["w_out"][0]], [red["w_out"][1]], "w_out", last_start)

    n_small = len(SMALL)
    conv_g = lax.dynamic_slice_in_dim(total[n_small:n_small + 4], (4 * cx + 2 * cy + cc) * (CONV_C // N_DEV), CONV_C // N_DEV, axis=1)
    loss = 0.5 * total[n_small + 4, 0] / D
    names = SMALL + ["conv_w"]
    res = _adamw_small([w[n] for n in names], [m[n] for n in names], [v[n] for n in names], total, conv_g, last_start)
    for n, (gn, dn, mn, vn) in zip(names, res):
        out_g[n], out_d[n], out_m[n], out_v[n] = gn, dn, mn, vn
    done = [out_v[n] for n in FFN2 + ["w_out", "conv_w"]]
    update(["w_in"], [pending["w_in"][0]], [_exchange_wait(pending["w_in"][1], done, "wait_exchange_w_in")], "w_in")
    done = [out_v["w_in"]]
    update(FFN1, [pending[n][0] for n in FFN1], [_exchange_wait(pending[n][1], done, "wait_exchange_" + n) for n in FFN1], "ffn1")

    outs = [loss, grad_x[None]]
    for d in (out_g, out_d, out_m, out_v):
        outs += [d[n] if n in SMALL else (d[n].T[None] if n in COL_SHARDED else d[n][None]) for n in WEIGHTS]
    return tuple(outs)
```

```python
import functools

import jax
import jax.numpy as jnp
from jax import lax
from jax.experimental import pallas as pl
from jax.experimental.pallas import tpu as pltpu

F32 = jnp.float32
BF16 = jnp.bfloat16
MESH = pl.DeviceIdType.MESH

N_DEV = 8
T = 2048
D = 1024
FF = 2816
FSR = FF // N_DEV
FSH = 384
FFP = N_DEV * FSH
HD = 64
NQ = 16
NKV = 4
QW = NQ * HD
KVW = NKV * HD
SSM_W = 1024
SSM_H = 16
SSM_N = 128
CONV_C = SSM_W + 2 * 2 * SSM_N
IN_COLS = 4112
INP = 4224
ISR = IN_COLS // N_DEV
ISW = 528
ISG = 640
L = 128
NCH = T // L
AB = 256
NAB = T // AB
EPS = 1e-6
NEG = -1e30
ROPE_THETA = 10000.0
DILATIONS = ((128, 1), (512, 4), (2048, 16))

ADAM_LR = 0.001
ADAM_B1 = 0.9
ADAM_B2 = 0.999
ADAM_EPS = 1e-08
ADAM_WD = 0.01
ADAM_STEP = 10

VMEM_LIMIT = 58 * 1024 * 1024


def _params(sem, vmem=VMEM_LIMIT):
    return pltpu.CompilerParams(dimension_semantics=sem, vmem_limit_bytes=vmem)


def _dot(a, b):
    return jnp.dot(a, b, preferred_element_type=F32)


def _dot_nt(a, b):
    return lax.dot_general(a, b, (((1,), (1,)), ((), ())), preferred_element_type=F32)


def _dot_tn(a, b):
    return lax.dot_general(a, b, (((0,), (0,)), ((), ())), preferred_element_type=F32)


def _split3(x):
    hi = x.astype(BF16)
    r1 = x - hi.astype(F32)
    mid = r1.astype(BF16)
    lo = (r1 - mid.astype(F32)).astype(BF16)
    return hi, mid, lo


def _dot_hi(a, b, a_is_01=False):
    if a_is_01:
        sel = a.astype(BF16)
        return sum(_dot(sel, p) for p in _split3(b))
    sel = b.astype(BF16)
    return sum(_dot(p, sel) for p in _split3(a))


def _dot_nt_hi(a, b):
    sel = b.astype(BF16)
    return sum(_dot_nt(p, sel) for p in _split3(a))


def _rs(x):
    return lax.rsqrt(jnp.mean(x * x, axis=-1, keepdims=True) + EPS)


def _sigmoid(x):
    return jax.nn.sigmoid(x)


def _dsilu(x, s):
    return s * (1.0 + x * (1.0 - s))


def _resident(shape):
    nd = len(shape)
    return pl.BlockSpec(shape, lambda *_: (0,) * nd, pipeline_mode=pl.Buffered(1))


def _const(shape):
    nd = len(shape)
    return pl.BlockSpec(shape, lambda *_: (0,) * nd)


def _rows(tm, cols):
    return pl.BlockSpec((tm, cols), lambda i: (i, 0))


ANY = pl.BlockSpec(memory_space=pl.ANY)


def _place():
    return lax.axis_index("x"), lax.axis_index("y"), lax.axis_index("c")


def _gather_duty(arrays):
    n = len(arrays)
    results = [jax.ShapeDtypeStruct((N_DEV,) + a.shape, a.dtype) for a in arrays]

    def make(ins, outs, send_sems, recv_sems, local_sems):
        x, y, c = _place()
        me, sibling = (x, y, c), (x, y, 1 - c)
        chips = [(1 - x, y), (x, 1 - y), (1 - x, 1 - y)]

        def place_of(a, px, py, pc):
            return outs[a].at[4 * px + 2 * py + pc]

        def copy(a, k, block, to, src=None):
            dst = place_of(a, *block)
            return pltpu.make_async_remote_copy(src_ref=dst if src is None else src, dst_ref=dst,
                                                send_sem=send_sems.at[7 * a + k], recv_sem=recv_sems.at[7 * a + k],
                                                device_id=to, device_id_type=MESH)

        def own(a):
            return pltpu.make_async_copy(ins[a], place_of(a, *me), local_sems.at[a])

        def first(a):
            return [copy(a, 0, me, sibling, src=ins[a])] + [copy(a, 1 + j, me, (*chip, c), src=ins[a]) for j, chip in enumerate(chips)]

        def start():
            for a in range(n):
                own(a).start()
            for a in range(n):
                for cp in first(a):
                    cp.start()

        def finish():
            for j, chip in enumerate(chips):
                for a in range(n):
                    copy(a, 1 + j, (*chip, c), me).wait_recv()
                    copy(a, 4 + j, (*chip, c), sibling).start()
            for a in range(n):
                copy(a, 0, sibling, me).wait_recv()
                for j, chip in enumerate(chips):
                    copy(a, 4 + j, (*chip, 1 - c), me).wait_recv()
            for a in range(n):
                for cp in first(a) + [copy(a, 4 + j, (*chip, c), sibling) for j, chip in enumerate(chips)]:
                    cp.wait_send()
                own(a).wait()

        return start, finish

    return dict(operands=list(arrays), results=results, sems=(7 * n, 7 * n, n), make=make)


def _swap_duty(arrays):
    n = len(arrays)
    half = N_DEV // 2
    results = [jax.ShapeDtypeStruct(a.shape, a.dtype) for a in arrays]

    def make(ins, outs, send_sems, recv_sems):
        x, y, c = _place()

        def copies():
            return [pltpu.make_async_remote_copy(src_ref=ins[a].at[k], dst_ref=outs[a].at[k],
                                                 send_sem=send_sems.at[half * a + k], recv_sem=recv_sems.at[half * a + k],
                                                 device_id=(x, y, 1 - c), device_id_type=MESH)
                    for a in range(n) for k in range(half)]

        def start():
            for cp in copies():
                cp.start()

        def finish():
            for cp in copies():
                cp.wait()

        return start, finish

    return dict(operands=list(arrays), results=results, sems=(half * n, half * n), make=make)


def _exchange_duty(arrays):
    n = len(arrays)
    results = [jax.ShapeDtypeStruct(a.shape, a.dtype) for a in arrays]

    def make(ins, outs, send_sems, recv_sems):
        x, y, c = _place()
        chips = [(1 - x, y), (x, 1 - y), (1 - x, 1 - y)]
        my_chip = 2 * x + y

        def sends():
            return [pltpu.make_async_remote_copy(src_ref=ins[a].at[2 * px + py], dst_ref=outs[a].at[my_chip],
                                                 send_sem=send_sems.at[3 * a + j], recv_sem=recv_sems.at[3 * a + j],
                                                 device_id=(px, py, c), device_id_type=MESH)
                    for a in range(n) for j, (px, py) in enumerate(chips)]

        def start():
            for cp in sends():
                cp.start()

        def finish():
            for a in range(n):
                for j, (px, py) in enumerate(chips):
                    pltpu.make_async_remote_copy(src_ref=ins[a].at[my_chip], dst_ref=outs[a].at[2 * px + py],
                                                 send_sem=send_sems.at[3 * a + j], recv_sem=recv_sems.at[3 * a + j],
                                                 device_id=(px, py, c), device_id_type=MESH).wait_recv()
            for cp in sends():
                cp.wait_send()

        return start, finish

    return dict(operands=list(arrays), results=results, sems=(3 * n, 3 * n), make=make)


def _call(body, *, name, grid, in_specs, out_specs, out_shape, args, sem, scratch=(), duties=(), after=()):
    n_in, n_out, n_scr = len(in_specs), len(out_specs), len(scratch)
    sem_shapes = [pltpu.SemaphoreType.DMA((k,)) for d in duties for k in d["sems"]]

    def full(*refs):
        pos = [0]

        def take(k):
            pos[0] += k
            return refs[pos[0] - k:pos[0]]

        ins = take(n_in)
        d_ins = [take(len(d["operands"])) for d in duties]
        take(len(after))
        outs = take(n_out)
        d_outs = [take(len(d["results"])) for d in duties]
        scr = take(n_scr)
        d_sems = [take(len(d["sems"])) for d in duties]
        hooks = [d["make"](di, do, *ds) for d, di, do, ds in zip(duties, d_ins, d_outs, d_sems)]
        if grid and hooks:
            ids = [pl.program_id(k) for k in range(len(grid))]
            first = functools.reduce(jnp.logical_and, [i == 0 for i in ids])
            last = functools.reduce(jnp.logical_and, [i == g - 1 for i, g in zip(ids, grid)])

            @pl.when(first)
            def _():
                for start, _ in hooks:
                    start()

            body(*ins, *outs, *scr)

            @pl.when(last)
            def _():
                for _, finish in hooks:
                    finish()
        else:
            for start, _ in hooks:
                start()
            body(*ins, *outs, *scr)
            for _, finish in hooks:
                finish()

    d_args = [a for d in duties for a in d["operands"]]
    d_res = [r for d in duties for r in d["results"]]
    kwargs = dict(grid=grid) if grid else {}
    res = pl.pallas_call(
        full, name=name, in_specs=list(in_specs) + [ANY] * (len(d_args) + len(after)), out_specs=list(out_specs) + [ANY] * len(d_res),
        out_shape=list(out_shape) + d_res, scratch_shapes=list(scratch) + sem_shapes,
        compiler_params=_params(sem) if grid else None, **kwargs,
    )(*args, *d_args, *after)
    own, rest = list(res[:n_out]), list(res[n_out:])
    by_duty = []
    for d in duties:
        by_duty.append(rest[:len(d["results"])])
        rest = rest[len(d["results"]):]
    return own, by_duty


def _comm_only(duties, name, after=()):
    return _call(lambda: None, name=name, grid=None, in_specs=[], out_specs=[], out_shape=[], args=[], sem=None, duties=duties,
                 after=after)[1]


HBM = pl.BlockSpec(memory_space=pltpu.HBM)
SEMS = pl.BlockSpec(memory_space=pltpu.SEMAPHORE)
SIDE_EFFECT = pltpu.SideEffectType.DATAFLOW_SIDE_EFFECTING
N_OTHER_CHIPS = 3


def _chip_copies(src_ref, land_ref, sems, rows):
    x, y, c = _place()
    chips = [(1 - x, y), (x, 1 - y), (1 - x, 1 - y)]
    part = (lambda ref: ref) if rows is None else (lambda ref: ref.at[pl.ds(0, rows)])
    return [pltpu.make_async_remote_copy(src_ref=part(src_ref.at[2 * px + py]), dst_ref=part(land_ref.at[2 * x + y]),
                                         send_sem=sems[j], recv_sem=sems[N_OTHER_CHIPS + j], device_id=(px, py, c), device_id_type=MESH)
            for j, (px, py) in enumerate(chips)]


def _exchange_start(pb, name, rows=None):
    n_sem = 2 * N_OTHER_CHIPS

    def body(pb_ref, land_ref, *rest):
        for cp in _chip_copies(pb_ref, land_ref, rest[:n_sem], rows):
            cp.start()
        token = rest[n_sem + 2]
        token[...] = jnp.zeros_like(token)

    res = pl.pallas_call(
        body, name=name,
        out_shape=(pltpu.SemaphoreType.DMA(()),) * n_sem + (pltpu.HBM(pb.shape, pb.dtype), pltpu.HBM(pb.shape, pb.dtype),
                                                              jax.ShapeDtypeStruct((8, 128), F32)),
        in_specs=(HBM, HBM), out_specs=(SEMS,) * n_sem + (HBM, HBM, pl.BlockSpec(memory_space=pltpu.VMEM)),
        input_output_aliases={0: n_sem, 1: n_sem + 1},
        compiler_params=pltpu.CompilerParams(has_side_effects=SIDE_EFFECT),
    )(pltpu.with_memory_space_constraint(pb, pltpu.HBM), pltpu.with_memory_space_constraint(lax.empty(pb.shape, pb.dtype), pltpu.HBM))
    return dict(sems=res[:n_sem], src=res[n_sem], land=res[n_sem + 1], token=res[n_sem + 2], rows=rows)


def _exchange_wait(started, after, name):
    n_sem = 2 * N_OTHER_CHIPS

    def body(pb_ref, land_ref, *rest):
        for cp in _chip_copies(pb_ref, land_ref, rest[:n_sem], started["rows"]):
            cp.wait_send()
            cp.wait_recv()

    src, land = started["src"], started["land"]
    return pl.pallas_call(
        body, name=name, out_shape=(pltpu.HBM(src.shape, src.dtype), pltpu.HBM(land.shape, land.dtype)),
        in_specs=(HBM, HBM) + (SEMS,) * n_sem + (ANY,) * len(after), out_specs=(HBM, HBM), input_output_aliases={0: 0, 1: 1},
        compiler_params=pltpu.CompilerParams(has_side_effects=SIDE_EFFECT),
    )(src, land, *started["sems"], *after)[1]


def _ffn_fwd(x, gpre, wg, wu, wd, gpost, name, duties=(), target=None):
    tm = 256
    n_in = 6 if target is None else 7

    def body(*refs):
        x_ref, gpre_ref, wg_ref, wu_ref, wd_ref, gpost_ref = refs[:6]
        xo_ref, n_ref, a_ref, b_ref, hm_ref, h_ref = refs[n_in:n_in + 6]
        xv = x_ref[...]
        n = (xv * _rs(xv) * gpre_ref[...]).astype(BF16)
        a = _dot_nt(n, wg_ref[...])
        b = _dot_nt(n, wu_ref[...])
        hm = (a * _sigmoid(a) * b).astype(BF16)
        h = _dot(hm, wd_ref[...])
        xo = xv + 0.5 * (h * _rs(h) * gpost_ref[...])
        n_ref[...] = n
        a_ref[...] = a.astype(BF16)
        b_ref[...] = b.astype(BF16)
        hm_ref[...] = hm
        h_ref[...] = h
        if target is None:
            xo_ref[...] = xo
        else:
            ss_ref = refs[n_in + 6]

            @pl.when(pl.program_id(0) == 0)
            def _():
                ss_ref[...] = jnp.zeros_like(ss_ref)

            err = xo - refs[6][...]
            xo_ref[...] = err * (1.0 / D)
            ss_ref[...] += jnp.sum(jnp.sum(err * err, axis=1, keepdims=True), axis=0, keepdims=True)

    loss_in = [] if target is None else [_rows(tm, D)]
    loss_out = [] if target is None else [_const((1, 128))]
    loss_shape = [] if target is None else [jax.ShapeDtypeStruct((1, 128), F32)]
    return _call(
        body, name=name, grid=(T // tm,),
        in_specs=[_rows(tm, D), _const((1, D)), _resident((FFP, D)), _resident((FFP, D)), _resident((FFP, D)), _const((1, D))] + loss_in,
        out_specs=[_rows(tm, D), _rows(tm, D), _rows(tm, FFP), _rows(tm, FFP), _rows(tm, FFP), _rows(tm, D)] + loss_out,
        out_shape=[jax.ShapeDtypeStruct((T, D), F32), jax.ShapeDtypeStruct((T, D), BF16), jax.ShapeDtypeStruct((T, FFP), BF16),
                   jax.ShapeDtypeStruct((T, FFP), BF16), jax.ShapeDtypeStruct((T, FFP), BF16), jax.ShapeDtypeStruct((T, D), F32)]
        + loss_shape,
        args=[x, gpre, wg, wu, wd, gpost] + ([] if target is None else [target]), sem=("arbitrary",), duties=duties)


def _ffn_bwd(dxo, x, a, b, h, gpre, gpost, wg, wu, wd, name, after=()):
    tm = 256

    def body(dxo_ref, x_ref, a_ref, b_ref, h_ref, gpre_ref, gpost_ref, wg_ref, wu_ref, wd_ref,
             dx_ref, da_ref, db_ref, dh_ref, dgpre_ref, dgpost_ref):
        @pl.when(pl.program_id(0) == 0)
        def _():
            dgpre_ref[...] = jnp.zeros_like(dgpre_ref)
            dgpost_ref[...] = jnp.zeros_like(dgpost_ref)

        dy = dxo_ref[...]
        h = h_ref[...]
        hn = h * _rs(h)
        r2 = _rs(h)
        dgpost_ref[...] += jnp.sum(0.5 * dy * hn, axis=0, keepdims=True)
        gdy = 0.5 * dy * gpost_ref[...]
        dh = r2 * (gdy - hn * jnp.mean(gdy * hn, axis=-1, keepdims=True))
        dhb = dh.astype(BF16)
        dh_ref[...] = dhb
        dhm = _dot_nt(dhb, wd_ref[...])
        av = a_ref[...].astype(F32)
        bv = b_ref[...].astype(F32)
        sg = _sigmoid(av)
        db = (dhm * (av * sg)).astype(BF16)
        da = (dhm * bv * _dsilu(av, sg)).astype(BF16)
        da_ref[...] = da
        db_ref[...] = db
        dn = _dot(da, wg_ref[...]) + _dot(db, wu_ref[...])
        xv = x_ref[...]
        r = _rs(xv)
        xn = xv * r
        dgpre_ref[...] += jnp.sum(dn * xn, axis=0, keepdims=True)
        gdn = dn * gpre_ref[...]
        dx_ref[...] = dy + r * (gdn - xn * jnp.mean(gdn * xn, axis=-1, keepdims=True))

    return _call(
        body, name=name, grid=(T // tm,),
        in_specs=[_rows(tm, D), _rows(tm, D), _rows(tm, FFP), _rows(tm, FFP), _rows(tm, D), _const((1, D)), _const((1, D)),
                  _resident((FFP, D)), _resident((FFP, D)), _resident((FFP, D))],
        out_specs=[_rows(tm, D), _rows(tm, FFP), _rows(tm, FFP), _rows(tm, D), _const((1, D)), _const((1, D))],
        out_shape=[jax.ShapeDtypeStruct((T, D), F32), jax.ShapeDtypeStruct((T, FFP), BF16), jax.ShapeDtypeStruct((T, FFP), BF16),
                   jax.ShapeDtypeStruct((T, D), BF16), jax.ShapeDtypeStruct((1, D), F32), jax.ShapeDtypeStruct((1, D), F32)],
        args=[dxo, x, a, b, h, gpre, gpost, wg, wu, wd], sem=("arbitrary",), after=after)


def _core_index(core):
    return lax.axis_index("c") if core == "mesh" else core


def _by_core(res, o_ref, ob_ref, core):
    r = res.shape[0] // 2
    c = jnp.asarray(_core_index(core))

    @pl.when(c == 0)
    def _():
        o_ref[0] = res[:r]
        ob_ref[0] = res[r:].astype(BF16)

    @pl.when(c == 1)
    def _():
        o_ref[0] = res[r:]
        ob_ref[0] = res[:r].astype(BF16)


def _matmul_tn(a, b, name, core, after=(), duties=()):
    k, m = a.shape
    n = b.shape[1]
    r = m // N_DEV
    assert m == N_DEV * r and r % 128 == 0

    def body(a_ref, b_ref, o_ref, ob_ref):
        _by_core(_dot_tn(a_ref[...], b_ref[...]), o_ref, ob_ref, core)

    spec = pl.BlockSpec((1, r, n), lambda i: (i, 0, 0))
    return _call(body, name=name, grid=(N_DEV // 2,), in_specs=[pl.BlockSpec((k, 2 * r), lambda i: (0, i)), _resident((k, n))],
                 out_specs=[spec, spec],
                 out_shape=[jax.ShapeDtypeStruct((N_DEV // 2, r, n), F32), jax.ShapeDtypeStruct((N_DEV // 2, r, n), BF16)],
                 args=[a, b], sem=("arbitrary",), duties=duties, after=after)


def _dwout(attn, yn, dh2, core):
    rs = (QW + SSM_W) // N_DEV
    chips = N_DEV // 2

    def body(at_ref, yn_ref, dh_ref, o_ref, ob_ref):
        i = pl.program_id(0)

        @pl.when(i < chips // 2)
        def _():
            _by_core(_dot_tn(at_ref[...], dh_ref[...]), o_ref, ob_ref, core)

        @pl.when(i >= chips // 2)
        def _():
            _by_core(_dot_tn(yn_ref[...], dh_ref[...]), o_ref, ob_ref, core)

    spec = pl.BlockSpec((1, rs, D), lambda i: (i, 0, 0))
    return pl.pallas_call(
        body, name="dwout", grid=(chips,),
        in_specs=[pl.BlockSpec((T, 2 * rs), lambda i: (0, jnp.minimum(i, chips // 2 - 1))),
                  pl.BlockSpec((T, 2 * rs), lambda i: (0, jnp.maximum(i - chips // 2, 0))), _resident((T, D))],
        out_specs=[spec, spec],
        out_shape=[jax.ShapeDtypeStruct((chips, rs, D), F32), jax.ShapeDtypeStruct((chips, rs, D), BF16)],
        compiler_params=_params(("arbitrary",)),
    )(attn, yn, dh2)


def _rope_swap(t, lo_half):
    return jnp.where(lo_half, pltpu.roll(t, 96, 1), pltpu.roll(t, 32, 1))


def _to_query_heads(t):
    return jnp.concatenate([t[:, HD * (h // G_PER):HD * (h // G_PER + 1)] for h in range(NQ)], axis=1)


def _from_query_heads(t):
    def kv_sum(kv):
        parts = [t[:, HD * (G_PER * kv + g):HD * (G_PER * kv + g + 1)] for g in range(G_PER)]
        return (parts[0] + parts[1]) + (parts[2] + parts[3])

    return jnp.concatenate([kv_sum(kv) for kv in range(NKV)], axis=1)


def _inproj_fwd(x1, gpre, win, cos, sin_s, duties=()):
    tm = 256

    def body(x_ref, g_ref, w_ref, cos_ref, sin_ref, n_ref, q_ref, kx_ref, vx_ref, xbc_ref, z_ref, dt_ref):
        xv = x_ref[...]
        n = (xv * _rs(xv) * g_ref[...]).astype(BF16)
        n_ref[...] = n
        by_dev = _dot_nt(n, w_ref[...])
        proj = jnp.concatenate([by_dev[:, ISW * d:ISW * d + ISR] for d in range(N_DEV)], axis=1)
        cs = cos_ref[...]
        sn = sin_ref[...]
        lo_half = (lax.broadcasted_iota(jnp.int32, (1, 128), 1) % HD) < (HD // 2)

        def rope(t):
            return t * cs + _rope_swap(t, lo_half) * sn

        for j in range(QW // 128):
            t = proj[:, 128 * j:128 * j + 128]
            q_ref[:, 128 * j:128 * j + 128] = (rope(t) * (HD ** -0.5)).astype(BF16)
        k = jnp.concatenate([rope(proj[:, QW + 128 * j:QW + 128 * j + 128]) for j in range(KVW // 128)], axis=1)
        v = proj[:, QW + KVW:QW + 2 * KVW]
        kx_ref[...] = _to_query_heads(k).astype(BF16)
        vx_ref[...] = _to_query_heads(v).astype(BF16)
        c0 = QW + 2 * KVW
        xbc_ref[...] = proj[:, c0:c0 + CONV_C]
        z_ref[...] = proj[:, c0 + CONV_C:c0 + CONV_C + SSM_W]
        dt_ref[...] = proj[:, c0 + CONV_C + SSM_W:IN_COLS]

    return _call(
        body, name="inproj_fwd", grid=(T // tm,),
        in_specs=[_rows(tm, D), _const((1, D)), _resident((INP, D)), _rows(tm, 128), _rows(tm, 128)],
        out_specs=[_rows(tm, D), _rows(tm, QW), _rows(tm, QW), _rows(tm, QW), _rows(tm, CONV_C), _rows(tm, SSM_W), _rows(tm, SSM_H)],
        out_shape=[jax.ShapeDtypeStruct((T, D), BF16), jax.ShapeDtypeStruct((T, QW), BF16), jax.ShapeDtypeStruct((T, QW), BF16),
                   jax.ShapeDtypeStruct((T, QW), BF16), jax.ShapeDtypeStruct((T, CONV_C), F32), jax.ShapeDtypeStruct((T, SSM_W), F32),
                   jax.ShapeDtypeStruct((T, SSM_H), F32)],
        args=[x1, gpre, win, cos, sin_s], sem=("arbitrary",), duties=duties)


def _inproj_bwd(dres, dq, dkx, dvx, dxbc, dz, ddt, x1, gpre, win, cos, sin_s):
    tm = 256

    def body(dres_ref, dq_ref, dkx_ref, dvx_ref, dxbc_ref, dz_ref, ddt_ref, x_ref, g_ref, w_ref, cos_ref, sin_ref,
             dx_ref, dps_ref, dg_ref, dp_ref):
        @pl.when(pl.program_id(0) == 0)
        def _():
            dg_ref[...] = jnp.zeros_like(dg_ref)

        cs = cos_ref[...]
        sn = sin_ref[...]
        lo_half = (lax.broadcasted_iota(jnp.int32, (1, 128), 1) % HD) < (HD // 2)

        def rope_t(t):
            return t * cs - _rope_swap(t, lo_half) * sn

        for j in range(QW // 128):
            dp_ref[:, 128 * j:128 * j + 128] = rope_t(dq_ref[:, 128 * j:128 * j + 128] * (HD ** -0.5)).astype(BF16)
        dk = _from_query_heads(dkx_ref[...])
        dv = _from_query_heads(dvx_ref[...])
        for j in range(KVW // 128):
            dp_ref[:, QW + 128 * j:QW + 128 * j + 128] = rope_t(dk[:, 128 * j:128 * j + 128]).astype(BF16)
        dp_ref[:, QW + KVW:QW + 2 * KVW] = dv.astype(BF16)
        c0 = QW + 2 * KVW
        dp_ref[:, c0:c0 + CONV_C] = dxbc_ref[...].astype(BF16)
        dp_ref[:, c0 + CONV_C:c0 + CONV_C + SSM_W] = dz_ref[...].astype(BF16)
        dp_ref[:, c0 + CONV_C + SSM_W:INP] = ddt_ref[...].astype(BF16)
        pieces = [dp_ref[:, ISR * d:ISR * (d + 1)] for d in range(N_DEV)]
        zw = jnp.zeros((tm, ISW - ISR), BF16)
        zg = jnp.zeros((tm, ISG - ISR), BF16)
        dn = _dot(jnp.concatenate([t for p in pieces for t in (p, zw)], axis=1), w_ref[...])
        for d in range(N_DEV):
            dps_ref[:, ISG * d:ISG * (d + 1)] = jnp.concatenate([pieces[d], zg], axis=1)
        xv = x_ref[...]
        r = _rs(xv)
        xn = xv * r
        dg_ref[...] += jnp.sum(dn * xn, axis=0, keepdims=True)
        gdn = dn * g_ref[...]
        dx_ref[...] = dres_ref[...] + r * (gdn - xn * jnp.mean(gdn * xn, axis=-1, keepdims=True))

    return pl.pallas_call(
        body, name="inproj_bwd", grid=(T // tm,),
        in_specs=[_rows(tm, D), _rows(tm, QW), _rows(tm, QW), _rows(tm, QW), _rows(tm, CONV_C), _rows(tm, SSM_W), _rows(tm, 128),
                  _rows(tm, D), _const((1, D)), _resident((INP, D)), _rows(tm, 128), _rows(tm, 128)],
        out_specs=[_rows(tm, D), _rows(tm, N_DEV * ISG), _const((1, D))],
        out_shape=[jax.ShapeDtypeStruct((T, D), F32), jax.ShapeDtypeStruct((T, N_DEV * ISG), BF16), jax.ShapeDtypeStruct((1, D), F32)],
        scratch_shapes=[pltpu.VMEM((tm, INP), BF16)],
        compiler_params=_params(("arbitrary",)),
    )(dres, dq, dkx, dvx, dxbc, dz, ddt, x1, gpre, win, cos, sin_s)


def _outproj_fwd(x1, attn, yn, wout, gpost):
    tm = 512

    def body(x_ref, at_ref, yn_ref, w_ref, g_ref, xo_ref, h_ref):
        h = _dot(at_ref[...], w_ref[0:QW, :]) + _dot(yn_ref[...], w_ref[QW:QW + SSM_W, :])
        h_ref[...] = h
        xo_ref[...] = x_ref[...] + h * _rs(h) * g_ref[...]

    return pl.pallas_call(
        body, name="outproj_fwd", grid=(T // tm,),
        in_specs=[_rows(tm, D), _rows(tm, QW), _rows(tm, SSM_W), _resident((QW + SSM_W, D)), _const((1, D))],
        out_specs=[_rows(tm, D), _rows(tm, D)],
        out_shape=[jax.ShapeDtypeStruct((T, D), F32), jax.ShapeDtypeStruct((T, D), F32)],
        compiler_params=_params(("parallel",)),
    )(x1, attn, yn, wout, gpost)


def _outproj_bwd(dx2, h2, gpost, wout, duties=()):
    tm = 512

    def body(dy_ref, h_ref, g_ref, w_ref, dh_ref, dm_ref, dg_ref):
        @pl.when(pl.program_id(0) == 0)
        def _():
            dg_ref[...] = jnp.zeros_like(dg_ref)

        dy = dy_ref[...]
        h = h_ref[...]
        r = _rs(h)
        hn = h * r
        dg_ref[...] += jnp.sum(dy * hn, axis=0, keepdims=True)
        gdy = dy * g_ref[...]
        dh = (r * (gdy - hn * jnp.mean(gdy * hn, axis=-1, keepdims=True))).astype(BF16)
        dh_ref[...] = dh
        dm_ref[...] = _dot_nt(dh, w_ref[...])

    return _call(
        body, name="outproj_bwd", grid=(T // tm,),
        in_specs=[_rows(tm, D), _rows(tm, D), _const((1, D)), _resident((QW + SSM_W, D))],
        out_specs=[_rows(tm, D), _rows(tm, QW + SSM_W), _const((1, D))],
        out_shape=[jax.ShapeDtypeStruct((T, D), BF16), jax.ShapeDtypeStruct((T, QW + SSM_W), F32), jax.ShapeDtypeStruct((1, D), F32)],
        args=[dx2, h2, gpost, wout], sem=("arbitrary",), duties=duties)


def _attn_bias():
    d = jnp.arange(AB)[:, None] - jnp.arange(T)[None, :] + (T - AB)
    cnt = jnp.zeros(d.shape, F32)
    for window, dil in DILATIONS:
        cnt = cnt + ((d >= 0) & (d % dil == 0) & (d <= window)).astype(F32)
    return jnp.where(cnt > 0, jnp.log(jnp.maximum(cnt, 1.0)), NEG)


G_PER = NQ // NKV
WK = G_PER * HD


def _attn_fwd(q, kx, vx, bias, duties=()):
    def body(q_ref, kx_ref, vx_ref, bias_ref, o_ref, lse_ref):
        lane = lax.broadcasted_iota(jnp.int32, (1, WK), 1)
        lse_ref[...] = jnp.zeros_like(lse_ref)
        for i in range(NAB):
            n = (i + 1) * AB
            rows = slice(i * AB, n)
            qi = q_ref[rows, :]
            kxi = kx_ref[0:n, :]
            vxi = vx_ref[0:n, :]
            bb = bias_ref[:, (NAB - 1 - i) * AB:]
            o_acc = jnp.zeros((AB, WK), F32)
            for g in range(G_PER):
                mg = (lane // HD) == g
                s = _dot_nt(jnp.where(mg, qi, jnp.zeros_like(qi)), kxi) + bb
                m = jnp.max(s, axis=1, keepdims=True)
                p = jnp.exp(s - m)
                l = jnp.sum(p, axis=1, keepdims=True)
                o_acc = jnp.where(mg, _dot(p.astype(BF16), vxi) / l, o_acc)
                lse_ref[rows, g:g + 1] = m + jnp.log(l)
            o_ref[rows, :] = o_acc.astype(BF16)

    col = lambda kv: (0, kv)
    return _call(
        body, name="attn_fwd", grid=(NKV,),
        in_specs=[pl.BlockSpec((T, WK), col), pl.BlockSpec((T, WK), col), pl.BlockSpec((T, WK), col), _const((AB, T))],
        out_specs=[pl.BlockSpec((T, WK), col), pl.BlockSpec((T, 128), col)],
        out_shape=[jax.ShapeDtypeStruct((T, QW), BF16), jax.ShapeDtypeStruct((T, NKV * 128), F32)],
        args=[q, kx, vx, bias], sem=("arbitrary",), duties=duties)


def _attn_bwd(q, kx, vx, o, dmix, lse, bias, duties=()):
    def body(q_ref, kx_ref, vx_ref, o_ref, do_ref, lse_ref, bias_ref, dq_ref, dkx_ref, dvx_ref):
        lane = lax.broadcasted_iota(jnp.int32, (1, WK), 1)
        dkx_ref[...] = jnp.zeros_like(dkx_ref)
        dvx_ref[...] = jnp.zeros_like(dvx_ref)
        for i in range(NAB):
            n = (i + 1) * AB
            rows = slice(i * AB, n)
            qi = q_ref[rows, :]
            dof = do_ref[rows, :]
            doi = dof.astype(BF16)
            prod = dof * o_ref[rows, :].astype(F32)
            kxi = kx_ref[0:n, :]
            vxi = vx_ref[0:n, :]
            bb = bias_ref[:, (NAB - 1 - i) * AB:]
            dq_acc = jnp.zeros((AB, WK), F32)
            for g in range(G_PER):
                mg = (lane // HD) == g
                qm = jnp.where(mg, qi, jnp.zeros_like(qi))
                dom = jnp.where(mg, doi, jnp.zeros_like(doi))
                delta = jnp.sum(jnp.where(mg, prod, 0.0), axis=1, keepdims=True)
                p = jnp.exp(_dot_nt(qm, kxi) + bb - lse_ref[rows, g:g + 1])
                ds = (p * (_dot_nt(dom, vxi) - delta)).astype(BF16)
                dvx_ref[0:n, :] += _dot_tn(p.astype(BF16), dom)
                dkx_ref[0:n, :] += _dot_tn(ds, qm)
                dq_acc = jnp.where(mg, _dot(ds, kxi), dq_acc)
            dq_ref[rows, :] = dq_acc

    col = lambda kv: (0, kv)
    return _call(
        body, name="attn_bwd", grid=(NKV,),
        in_specs=[pl.BlockSpec((T, WK), col), pl.BlockSpec((T, WK), col), pl.BlockSpec((T, WK), col), pl.BlockSpec((T, WK), col),
                  pl.BlockSpec((T, WK), col), pl.BlockSpec((T, 128), col), _const((AB, T))],
        out_specs=[pl.BlockSpec((T, WK), col), pl.BlockSpec((T, WK), col), pl.BlockSpec((T, WK), col)],
        out_shape=[jax.ShapeDtypeStruct((T, QW), F32)] * 3,
        args=[q, kx, vx, o, dmix, lse, bias], sem=("arbitrary",), duties=duties)


def _softplus(x):
    return jnp.maximum(x, 0.0) + jnp.log1p(jnp.exp(-jnp.abs(x)))


def _causal_conv(u, zs, cw_ref, cb_ref):
    zs[8:, :] = u
    sh1, sh2, sh3 = (zs[8 - m:8 - m + L, :] for m in (1, 2, 3))
    return cb_ref[...] + cw_ref[3:4, :] * u + cw_ref[2:3, :] * sh1 + cw_ref[1:2, :] * sh2 + cw_ref[0:1, :] * sh3


def _ssd_chunk_common(xc, dtr, dtb_ref, alx_ref, e_ref):
    sg = _sigmoid(xc)
    act = xc * sg
    pre = dtr + dtb_ref[...]
    dt_x = _dot_hi(_softplus(pre), e_ref[...])
    a_x = -jnp.exp(alx_ref[...])
    ri = lax.broadcasted_iota(jnp.int32, (L, L), 0)
    ci = lax.broadcasted_iota(jnp.int32, (L, L), 1)
    tri = ri >= ci
    acs_x = _dot_hi(tri, dt_x * a_x, a_is_01=True)
    return dict(sg=sg, act=act, pre=pre, dt_x=dt_x, a_x=a_x, tri=tri, acs_x=acs_x)


def _decay(acs_x, acs_t, h, tri):
    col = acs_x[:, HD * h:HD * h + 1]
    row = acs_t[HD * h:HD * h + 1, :]
    return jnp.exp(jnp.where(tri, col - row, NEG))


def _ssd_fwd(xbc, z, dtr, convw, convb, dtb, alx, dskx, ssmn, e, duties=()):
    def body(u_ref, z_ref, dtr_ref, cw_ref, cb_ref, dtb_ref, alx_ref, dsk_ref, sn_ref, e_ref,
             yn_ref, y_ref, hs_ref, xc_ref, zs, hst):
        @pl.when(pl.program_id(0) == 0)
        def _():
            zs[0:8, :] = jnp.zeros((8, CONV_C), F32)
            hst[...] = jnp.zeros_like(hst)

        u = u_ref[...]
        xc = _causal_conv(u, zs, cw_ref, cb_ref)
        xc_ref[...] = xc
        zs[0:8, :] = u[L - 8:, :]
        cm = _ssd_chunk_common(xc, dtr_ref[...], dtb_ref, alx_ref, e_ref)
        act, dt_x, acs_x, tri = cm["act"], cm["dt_x"], cm["acs_x"], cm["tri"]
        xs = act[:, :SSM_W]
        acs_l = acs_x[L - 1:L, :]
        lam_x = jnp.exp(acs_x)
        w_x = jnp.exp(acs_l - acs_x)
        gam_x = jnp.exp(acs_l)
        acs_t = acs_x.T
        xd = xs * dt_x
        xb = xd.astype(BF16)
        xw = (xd * w_x).astype(BF16)
        lo = lax.broadcasted_iota(jnp.int32, (1, 128), 1) < HD
        hs_ref[0] = hst[...]
        pieces = []
        for grp in range(2):
            bb = act[:, SSM_W + SSM_N * grp:SSM_W + SSM_N * (grp + 1)].astype(BF16)
            cb_ = act[:, SSM_W + 2 * SSM_N + SSM_N * grp:SSM_W + 2 * SSM_N + SSM_N * (grp + 1)].astype(BF16)
            cbm = _dot_nt(cb_, bb)
            for jj in range(4):
                j = 4 * grp + jj
                sl = slice(128 * j, 128 * j + 128)
                m0 = (cbm * _decay(acs_x, acs_t, 2 * j, tri)).astype(BF16)
                m1 = (cbm * _decay(acs_x, acs_t, 2 * j + 1, tri)).astype(BF16)
                x2 = xb[:, sl]
                ydiag = jnp.where(lo, _dot(m0, x2), _dot(m1, x2))
                hprev = hst[j]
                yoff = lam_x[:, sl] * _dot(cb_, hprev.astype(BF16))
                pieces.append(ydiag + yoff)
                hst[j] = gam_x[:, sl] * hprev + _dot_tn(bb, xw[:, sl])
        y = jnp.concatenate(pieces, axis=1) + dsk_ref[...] * xs
        y_ref[...] = y
        zv = z_ref[...]
        yz = y * (zv * _sigmoid(zv))
        half = SSM_W // 2
        yn = jnp.concatenate([yz[:, :half] * _rs(yz[:, :half]), yz[:, half:] * _rs(yz[:, half:])], axis=1)
        yn_ref[...] = (yn * sn_ref[...]).astype(BF16)

    return _call(
        body, name="ssd_fwd", grid=(NCH,),
        in_specs=[_rows(L, CONV_C), _rows(L, SSM_W), _rows(L, SSM_H), _const((4, CONV_C)), _const((1, CONV_C)), _const((1, SSM_H)),
                  _const((1, SSM_W)), _const((1, SSM_W)), _const((1, SSM_W)), _const((SSM_H, SSM_W))],
        out_specs=[_rows(L, SSM_W), _rows(L, SSM_W), pl.BlockSpec((1, 8, SSM_N, 128), lambda c: (c, 0, 0, 0)), _rows(L, CONV_C)],
        out_shape=[jax.ShapeDtypeStruct((T, SSM_W), BF16), jax.ShapeDtypeStruct((T, SSM_W), F32),
                   jax.ShapeDtypeStruct((NCH, 8, SSM_N, 128), F32), jax.ShapeDtypeStruct((T, CONV_C), F32)],
        scratch=[pltpu.VMEM((8 + L, CONV_C), F32), pltpu.VMEM((8, SSM_N, 128), F32)],
        args=[xbc, z, dtr, convw, convb, dtb, alx, dskx, ssmn, e], sem=("arbitrary",), duties=duties)


def _ssd_bwd(dmix, xbc, xconv, z, dtr, y, hs, convw, dtb, alx, dskx, ssmn, e, e1, duties=()):
    rev = lambda i: (NCH - 1 - i, 0)

    def body(dyn_ref, u_ref, xc_ref, z_ref, dtr_ref, y_ref, hs_ref, cw_ref, dtb_ref, alx_ref, dsk_ref, sn_ref, e_ref, e1_ref,
             dxbc_ref, dz_ref, ddt_ref, dcw_ref, dcb_ref, dsn_ref, dpar_ref, dh, zd, colbuf):
        step = pl.program_id(0)

        @pl.when(step == 0)
        def _():
            for r in (dh, dcw_ref, dcb_ref, dsn_ref, dpar_ref):
                r[...] = jnp.zeros_like(r)
            zd[L:, :] = jnp.zeros((8, CONV_C), F32)

        u = u_ref[...]
        xc = xc_ref[...]
        cm = _ssd_chunk_common(xc, dtr_ref[...], dtb_ref, alx_ref, e_ref)
        sg, act, pre, dt_x, a_x, tri, acs_x = (cm[k] for k in ("sg", "act", "pre", "dt_x", "a_x", "tri", "acs_x"))
        xs = act[:, :SSM_W]
        acs_l = acs_x[L - 1:L, :]
        lam_x = jnp.exp(acs_x)
        w_x = jnp.exp(acs_l - acs_x)
        gam_x = jnp.exp(acs_l)
        acs_t = acs_x.T
        xd = xs * dt_x
        xb = xd.astype(BF16)
        xdw = xd * w_x
        xw = xdw.astype(BF16)
        lo = lax.broadcasted_iota(jnp.int32, (1, 128), 1) < HD
        row8 = lax.broadcasted_iota(jnp.int32, (8, 1), 0)

        dyn = dyn_ref[...]
        yv = y_ref[...]
        zv = z_ref[...]
        sz = _sigmoid(zv)
        siluz = zv * sz
        yz = yv * siluz
        half = SSM_W // 2
        gy = dyn * sn_ref[...]
        dyz_parts, yzn_parts = [], []
        for hf in range(2):
            part = yz[:, hf * half:(hf + 1) * half]
            r = _rs(part)
            pn = part * r
            gp = gy[:, hf * half:(hf + 1) * half]
            dyz_parts.append(r * (gp - pn * jnp.mean(gp * pn, axis=-1, keepdims=True)))
            yzn_parts.append(pn)
        dyz = jnp.concatenate(dyz_parts, axis=1)
        dsn_ref[...] += jnp.sum(dyn * jnp.concatenate(yzn_parts, axis=1), axis=0, keepdims=True)
        dy = dyz * siluz
        dz_ref[...] = dyz * yv * _dsilu(zv, sz)

        colbuf[...] = jnp.zeros_like(colbuf)
        dx_pieces, dacs_pieces, dacsl_pieces, db_pieces, dc_pieces = [], [], [], [], []
        for grp in range(2):
            bb = act[:, SSM_W + SSM_N * grp:SSM_W + SSM_N * (grp + 1)].astype(BF16)
            cb_ = act[:, SSM_W + 2 * SSM_N + SSM_N * grp:SSM_W + 2 * SSM_N + SSM_N * (grp + 1)].astype(BF16)
            cbm = _dot_nt(cb_, bb)
            dcbm = jnp.zeros((L, L), F32)
            dc_g = jnp.zeros((L, SSM_N), F32)
            db_g = jnp.zeros((L, SSM_N), F32)
            for jj in range(4):
                j = 4 * grp + jj
                sl = slice(128 * j, 128 * j + 128)
                dy2 = dy[:, sl]
                dy2b = dy2.astype(BF16)
                d0 = _decay(acs_x, acs_t, 2 * j, tri)
                d1 = _decay(acs_x, acs_t, 2 * j + 1, tri)
                m0 = cbm * d0
                m1 = cbm * d1
                x2 = xb[:, sl]
                hprev = hs_ref[0, j]
                hprevb = hprev.astype(BF16)
                dhn = dh[j]
                dhnb = dhn.astype(BF16)
                g2 = _dot(bb, dhnb)
                dx_pieces.append(jnp.where(lo, _dot_tn(m0.astype(BF16), dy2b), _dot_tn(m1.astype(BF16), dy2b)) + w_x[:, sl] * g2)
                zero = jnp.zeros_like(dy2b)
                dm0 = _dot_nt(jnp.where(lo, dy2b, zero), x2)
                dm1 = _dot_nt(jnp.where(lo, zero, dy2b), x2)
                dcbm = dcbm + dm0 * d0 + dm1 * d1
                e0 = dm0 * m0
                e1v = dm1 * m1
                colbuf[:, 2 * j:2 * j + 1] = jnp.sum(e0, axis=1, keepdims=True) - jnp.sum(e0.T, axis=1, keepdims=True)
                colbuf[:, 2 * j + 1:2 * j + 2] = jnp.sum(e1v, axis=1, keepdims=True) - jnp.sum(e1v.T, axis=1, keepdims=True)
                yoff = lam_x[:, sl] * _dot(cb_, hprevb)
                gxw = g2 * xdw[:, sl]
                dacs_pieces.append(dy2 * yoff - gxw)
                dacsl_pieces.append(jnp.sum(gxw, axis=0, keepdims=True) + gam_x[:, sl] * jnp.sum(dhn * hprev, axis=0, keepdims=True))
                dyl = (dy2 * lam_x[:, sl]).astype(BF16)
                dc_g = dc_g + _dot_nt(dyl, hprevb)
                db_g = db_g + _dot_nt(xw[:, sl], dhnb)
                dh[j] = gam_x[:, sl] * dhn + _dot_tn(cb_, dyl)
            dcbb = dcbm.astype(BF16)
            dc_pieces.append(dc_g + _dot(dcbb, bb))
            db_pieces.append(db_g + _dot_tn(dcbb, cb_))

        dxd = jnp.concatenate(dx_pieces, axis=1)
        rowi = lax.broadcasted_iota(jnp.int32, (L, 1), 0)
        dacs_x = (jnp.concatenate(dacs_pieces, axis=1) + _dot_hi(colbuf[...], e1_ref[...])
                  + jnp.where(rowi == L - 1, jnp.concatenate(dacsl_pieces, axis=1), 0.0))
        upper = lax.broadcasted_iota(jnp.int32, (L, L), 0) <= lax.broadcasted_iota(jnp.int32, (L, L), 1)
        dadt_x = _dot_hi(upper, dacs_x, a_is_01=True)
        ddt_x = dxd * xs + dadt_x * a_x
        ddtr = _dot_nt_hi(ddt_x, e_ref[...]) * _sigmoid(pre)
        ddt_ref[...] = jnp.zeros_like(ddt_ref)
        ddt_ref[:, 0:SSM_H] = ddtr
        dalx =jnp.sum(dadt_x * dt_x, axis=0, keepdims=True) * a_x
        ddskx = jnp.sum(dy * xs, axis=0, keepdims=True)
        par_x = jnp.where(row8 == 1, dalx, 0.0) + jnp.where(row8 == 2, ddskx, 0.0)
        dpar_ref[...] += _dot_nt_hi(par_x, e_ref[...]) + jnp.where(row8 == 0, jnp.sum(ddtr, axis=0, keepdims=True), 0.0)

        dxs = dxd * dt_x + dsk_ref[...] * dy
        dact = jnp.concatenate([dxs] + db_pieces + dc_pieces, axis=1)
        du = dact * _dsilu(xc, sg)
        dcb_ref[...] += jnp.sum(du, axis=0, keepdims=True)
        zd[0:L, :] = du
        f1, f2, f3 = (zd[m:m + L, :] for m in (1, 2, 3))
        dxbc_ref[...] = cw_ref[3:4, :] * du + cw_ref[2:3, :] * f1 + cw_ref[1:2, :] * f2 + cw_ref[0:1, :] * f3
        dcw = jnp.zeros((8, CONV_C), F32)
        for k, shifted in enumerate((f3, f2, f1, du)):
            dcw = dcw + jnp.where(row8 == k, jnp.sum(shifted * u, axis=0, keepdims=True), 0.0)
        dcw_ref[...] += dcw
        zd[L:, :] = du[:8, :]

    return _call(
        body, name="ssd_bwd", grid=(NCH,),
        in_specs=[pl.BlockSpec((L, SSM_W), lambda i: (NCH - 1 - i, 1)), pl.BlockSpec((L, CONV_C), rev), pl.BlockSpec((L, CONV_C), rev),
                  pl.BlockSpec((L, SSM_W), rev), pl.BlockSpec((L, SSM_H), rev), pl.BlockSpec((L, SSM_W), rev),
                  pl.BlockSpec((1, 8, SSM_N, 128), lambda i: (NCH - 1 - i, 0, 0, 0)),
                  _const((4, CONV_C)), _const((1, SSM_H)), _const((1, SSM_W)), _const((1, SSM_W)), _const((1, SSM_W)),
                  _const((SSM_H, SSM_W)), _const((128, SSM_W))],
        out_specs=[pl.BlockSpec((L, CONV_C), rev), pl.BlockSpec((L, SSM_W), rev), pl.BlockSpec((L, 128), rev),
                   _const((8, CONV_C)), _const((1, CONV_C)), _const((1, SSM_W)), _const((8, SSM_H))],
        out_shape=[jax.ShapeDtypeStruct((T, CONV_C), F32), jax.ShapeDtypeStruct((T, SSM_W), F32), jax.ShapeDtypeStruct((T, 128), F32),
                   jax.ShapeDtypeStruct((8, CONV_C), F32), jax.ShapeDtypeStruct((1, CONV_C), F32), jax.ShapeDtypeStruct((1, SSM_W), F32),
                   jax.ShapeDtypeStruct((8, SSM_H), F32)],
        scratch=[pltpu.VMEM((8, SSM_N, 128), F32), pltpu.VMEM((L + 8, CONV_C), F32), pltpu.VMEM((L, 128), F32)],
        args=[dmix, xbc, xconv, z, dtr, y, hs, convw, dtb, alx, dskx, ssmn, e, e1], sem=("arbitrary",), duties=duties)


def _adam_math(w, g, m, v):
    m = ADAM_B1 * m + (1.0 - ADAM_B1) * g
    v = ADAM_B2 * v + (1.0 - ADAM_B2) * (g * g)
    m_hat = m / (1.0 - ADAM_B1 ** ADAM_STEP)
    v_hat = v / (1.0 - ADAM_B2 ** ADAM_STEP)
    delta = -ADAM_LR * (m_hat / (jnp.sqrt(v_hat) + ADAM_EPS) + ADAM_WD * w)
    return delta, m, v


PACK_W = CONV_C


def _adamw_small(ws, ms, vs, total, conv_g, after=()):
    k = len(ws)

    def body(*refs):
        ins, outs = refs[:3 * k + 2], refs[3 * k + 2 + len(after):]
        total_ref, conv_ref = ins[3 * k], ins[3 * k + 1]
        for a in range(k):
            w_ref, m_ref, v_ref = ins[3 * a:3 * a + 3]
            g = conv_ref[...] if a == k - 1 else total_ref[a:a + 1, 0:w_ref.shape[1]]
            delta, nm, nv = _adam_math(w_ref[...], g, m_ref[...], v_ref[...])
            for ref, val in zip(outs[4 * a:4 * a + 4], (g, delta, nm, nv)):
                ref[...] = val

    vm = pl.BlockSpec(memory_space=pltpu.VMEM)
    args = [x for a in range(k) for x in (ws[a], ms[a], vs[a])] + [total, conv_g]
    res = pl.pallas_call(
        body, name="adamw_small", in_specs=[vm] * len(args) + [ANY] * len(after), out_specs=[vm] * (4 * k),
        out_shape=[jax.ShapeDtypeStruct(ws[a].shape, F32) for a in range(k) for _ in range(4)],
    )(*args, *after)
    return [tuple(res[4 * a:4 * a + 4]) for a in range(k)]


COL_TILE = 512


def _adamw_sharded(ws, ms, vs, chip_sums, from_chips, other_chips, name, after=()):
    k = len(ws)
    rows, cols = ws[0].shape
    prow = chip_sums[0].shape[0]
    assert cols % COL_TILE == 0 and prow >= rows and all(a.shape == ws[0].shape for a in ws)

    def body(ids_ref, *refs):
        ins, outs = refs[:7 * k], refs[7 * k + len(after):]
        for a in range(k):
            w_ref, m_ref, v_ref, s_ref, r1_ref, r2_ref, r3_ref = ins[7 * a:7 * a + 7]
            g = s_ref[...]
            for r in (r1_ref, r2_ref, r3_ref):
                g = g + r[0].astype(F32)
            g = g[:rows]
            delta, nm, nv = _adam_math(w_ref[...], g, m_ref[...], v_ref[...])
            for ref, val in zip(outs[4 * a:4 * a + 4], (g, delta, nm, nv)):
                ref[...] = val

    spec = pl.BlockSpec((rows, COL_TILE), lambda i, ids: (0, i))
    part = lambda j: pl.BlockSpec((1, prow, COL_TILE), lambda i, ids: (ids[j], 0, i))
    one = [spec, spec, spec, pl.BlockSpec((prow, COL_TILE), lambda i, ids: (0, i)), part(0), part(1), part(2)]
    args = [x for a in range(k) for x in (ws[a], ms[a], vs[a], chip_sums[a], from_chips[a], from_chips[a], from_chips[a])]
    res = pl.pallas_call(
        body, name=name,
        grid_spec=pltpu.PrefetchScalarGridSpec(
            num_scalar_prefetch=1, grid=(cols // COL_TILE,),
            in_specs=one * k + [ANY] * len(after), out_specs=[spec] * (4 * k)),
        out_shape=[jax.ShapeDtypeStruct((rows, cols), F32)] * (4 * k),
        compiler_params=_params(("parallel",)),
    )(other_chips, *args, *after)
    return [tuple(res[4 * a:4 * a + 4]) for a in range(k)]


def _chip_sum(mines, recvs, name):
    k = len(mines)
    rows, cols = mines[0].shape[1:]

    def body(*refs):
        own_chip = pl.program_id(0) == 2 * lax.axis_index("x") + lax.axis_index("y")
        for a in range(k):
            a_ref, b_ref = refs[2 * a:2 * a + 2]
            s_ref, sb_ref = refs[2 * k + 2 * a:2 * k + 2 * a + 2]
            s = a_ref[0] + b_ref[0].astype(F32)
            sb_ref[0] = s.astype(BF16)

            @pl.when(own_chip)
            def _(s_ref=s_ref, s=s):
                s_ref[...] = s

    by_chip = pl.BlockSpec((1, rows, cols), lambda c: (c, 0, 0))
    res = pl.pallas_call(
        body, name=name, grid=(N_DEV // 2,),
        in_specs=[by_chip, by_chip] * k, out_specs=[_const((rows, cols)), by_chip] * k,
        out_shape=[jax.ShapeDtypeStruct((rows, cols), F32), jax.ShapeDtypeStruct((N_DEV // 2, rows, cols), BF16)] * k,
        compiler_params=_params(("arbitrary",)),
    )(*[x for pair in zip(mines, recvs) for x in pair])
    return [tuple(res[2 * a:2 * a + 2]) for a in range(k)]


def _all_reduce_small(v, after=()):
    rows = v.shape[0]

    def body(v_ref, *rest):
        out_ref, gath, send_sems, recv_sems = rest[len(after):]
        x, y, c = _place()
        me, sibling = (x, y, c), (x, y, 1 - c)
        chips = [(1 - x, y), (x, 1 - y), (1 - x, 1 - y)]

        def blk(px, py, pc):
            return gath.at[pl.ds((4 * px + 2 * py + pc) * rows, rows), :]

        def copy(k, block, to, src=None):
            return pltpu.make_async_remote_copy(src_ref=blk(*block) if src is None else src, dst_ref=blk(*block),
                                                send_sem=send_sems.at[k], recv_sem=recv_sems.at[k], device_id=to, device_id_type=MESH)

        gath[pl.ds((4 * x + 2 * y + c) * rows, rows), :] = v_ref[...]
        first = [copy(0, me, sibling, src=v_ref)] + [copy(1 + j, me, (*chip, c), src=v_ref) for j, chip in enumerate(chips)]
        for cp in first:
            cp.start()
        passed = [copy(4 + j, (*chip, c), sibling) for j, chip in enumerate(chips)]
        for j, chip in enumerate(chips):
            copy(1 + j, (*chip, c), me).wait_recv()
            passed[j].start()
        copy(0, sibling, me).wait_recv()
        for j, chip in enumerate(chips):
            copy(4 + j, (*chip, 1 - c), me).wait_recv()
        for cp in first + passed:
            cp.wait_send()
        acc = gath[0:rows, :]
        for d in range(1, N_DEV):
            acc = acc + gath[d * rows:(d + 1) * rows, :]
        out_ref[...] = acc

    vm = pl.BlockSpec(memory_space=pltpu.VMEM)
    return pl.pallas_call(
        body, name="all_reduce_small",
        in_specs=[vm] + [ANY] * len(after), out_specs=vm,
        out_shape=jax.ShapeDtypeStruct(v.shape, F32),
        scratch_shapes=[pltpu.VMEM((N_DEV * rows, v.shape[1]), F32), pltpu.SemaphoreType.DMA((7,)), pltpu.SemaphoreType.DMA((7,))],
    )(v, *after)


def _rope_tables(positions):
    inv_freq = ROPE_THETA ** (-jnp.arange(0, HD, 2, dtype=F32) / HD)
    ang = positions.reshape(T).astype(F32)[:, None] * inv_freq
    ang = jnp.concatenate([ang, ang, ang, ang], axis=-1)
    lo_half = (jnp.arange(128) % HD) < (HD // 2)
    return jnp.cos(ang), jnp.where(lo_half, -jnp.sin(ang), jnp.sin(ang))


def _selectors():
    lane = jnp.arange(QW)
    e = (lane[None, :] // HD == jnp.arange(SSM_H)[:, None]).astype(F32)
    e1 = ((lane[None, :] == HD * jnp.arange(128)[:, None]) & (jnp.arange(128)[:, None] < SSM_H)).astype(F32)
    return e, e1


WEIGHTS = ['ffn1_pre_norm', 'ffn1_w_gate', 'ffn1_w_up', 'ffn1_w_down', 'ffn1_post_norm', 'mix_pre_norm', 'w_in', 'conv_w', 'conv_b',
           'dt_bias', 'a_log', 'd_skip', 'ssm_norm', 'w_out', 'mix_post_norm', 'ffn2_pre_norm', 'ffn2_w_gate', 'ffn2_w_up',
           'ffn2_w_down', 'ffn2_post_norm']
COL_SHARDED = ['ffn1_w_gate', 'ffn1_w_up', 'ffn2_w_gate', 'ffn2_w_up', 'w_in']
ROW_SHARDED = ['ffn1_w_down', 'ffn2_w_down', 'w_out']
BIG = COL_SHARDED + ROW_SHARDED
FFN_BIG = COL_SHARDED[:4] + ROW_SHARDED[:2]
SMALL = ['ffn1_pre_norm', 'ffn1_post_norm', 'mix_pre_norm', 'conv_b', 'dt_bias', 'a_log', 'd_skip', 'ssm_norm', 'mix_post_norm',
         'ffn2_pre_norm', 'ffn2_post_norm']
FFN1 = ['ffn1_w_gate', 'ffn1_w_up', 'ffn1_w_down']
FFN2 = ['ffn2_w_gate', 'ffn2_w_up', 'ffn2_w_down']


def _wire_block(name, a):
    if name in FFN_BIG:
        return jnp.pad(a.astype(BF16), ((0, FSH - FSR), (0, 0)))
    if name == "w_in":
        return jnp.pad(a.astype(BF16), ((0, ISW - ISR), (0, 0)))
    return a if name == "conv_w" else a.astype(BF16)


def _whole_from_gathered(name, a):
    if name == "conv_w":
        return jnp.transpose(a, (1, 0, 2)).reshape(a.shape[1], -1)
    return a.reshape(-1, D)


def _step(x, positions, target, small, blocks=None, whole=None):
    dist = blocks is not None
    core = "mesh" if dist else 0
    w = dict(small)
    if whole:
        w.update(whole)

    def gather(names):
        return [_gather_duty([_wire_block(n, blocks[n]) for n in names])] if dist else []

    def put(names, results):
        if dist:
            for n, r in zip(names, results[0]):
                w[n] = _whole_from_gathered(n, r)

    g, sums, red = {}, {}, {}

    def swap(names):
        return [_swap_duty([g[n][1] for n in names])] if dist else []

    def chip_sums(names, from_sibling):
        if dist:
            res = _chip_sum([g[n][0] for n in names], list(from_sibling), "chip_sum_" + names[0])
            sums.update(zip(names, res))

    def exchange(names):
        return [_exchange_duty([sums[n][1] for n in names])] if dist else []

    def reduced(names, from_chips):
        if dist:
            for n, recv in zip(names, from_chips):
                red[n] = (sums[n][0], recv)

    cos, sin_s = _rope_tables(positions)
    e, e1 = _selectors()
    bias = _attn_bias()
    alx = jnp.repeat(w["a_log"], HD, axis=1)
    dskx = jnp.repeat(w["d_skip"], HD, axis=1)

    if dist:
        put(FFN1, _comm_only(gather(FFN1), "gather_ffn1"))
    (x1, n1, a1, b1, hm1, h1), got = _ffn_fwd(x, w["ffn1_pre_norm"], w["ffn1_w_gate"], w["ffn1_w_up"], w["ffn1_w_down"],
                                              w["ffn1_post_norm"], "ffn1_fwd", gather(["w_in", "conv_w"]))
    put(["w_in", "conv_w"], got)
    (n2, q, kx, vx, xbc, z, dtr), got = _inproj_fwd(x1, w["mix_pre_norm"], w["w_in"], cos, sin_s, gather(["w_out"]))
    put(["w_out"], got)
    (attn, lse), got = _attn_fwd(q, kx, vx, bias, gather(FFN2[:2]))
    put(FFN2[:2], got)
    (yn, y, hs, xconv), got = _ssd_fwd(xbc, z, dtr, w["conv_w"], w["conv_b"], w["dt_bias"], alx, dskx, w["ssm_norm"], e, gather(FFN2[2:]))
    put(FFN2[2:], got)
    x2, h2 = _outproj_fwd(x1, attn, yn, w["w_out"], w["mix_post_norm"])
    (dx3, n3, a3, b3, hm3, h3, ss), _ = _ffn_fwd(x2, w["ffn2_pre_norm"], w["ffn2_w_gate"], w["ffn2_w_up"], w["ffn2_w_down"],
                                                 w["ffn2_post_norm"], "ffn2_fwd", target=target)

    (dx2, da3, db3, dh3, g["ffn2_pre_norm"], g["ffn2_post_norm"]), _ = _ffn_bwd(
        dx3, x2, a3, b3, h3, w["ffn2_pre_norm"], w["ffn2_post_norm"], w["ffn2_w_gate"], w["ffn2_w_up"], w["ffn2_w_down"], "ffn2_bwd")
    g["ffn2_w_down"] = _matmul_tn(hm3, dh3, "ffn2_dwd", core)[0]
    g["ffn2_w_gate"] = _matmul_tn(da3, n3, "ffn2_dwg", core)[0]
    g["ffn2_w_up"] = _matmul_tn(db3, n3, "ffn2_dwu", core)[0]

    (dh2, dmix, g["mix_post_norm"]), got = _outproj_bwd(dx2, h2, w["mix_post_norm"], w["w_out"], swap(FFN2))
    chip_sums(FFN2, got[0] if dist else None)
    g["w_out"] = _dwout(attn, yn, dh2, core)
    (dq, dkx, dvx), got = _attn_bwd(q, kx, vx, attn, dmix, lse, bias, exchange(FFN2) + swap(["w_out"]))
    if dist:
        reduced(FFN2, got[0])
        chip_sums(["w_out"], got[1])
    (dxbc, dz, ddt, dcw, g["conv_b"], g["ssm_norm"], dpar), got = _ssd_bwd(
        dmix, xbc, xconv, z, dtr, y, hs, w["conv_w"], w["dt_bias"], alx, dskx, w["ssm_norm"], e, e1, exchange(["w_out"]))
    reduced(["w_out"], got[0] if dist else None)
    g["conv_w"] = dcw[0:4]
    g["dt_bias"], g["a_log"], g["d_skip"] = dpar[0:1], dpar[1:2], dpar[2:3]
    dx1, dproj, g["mix_pre_norm"] = _inproj_bwd(dx2, dq, dkx, dvx, dxbc, dz, ddt, x1, w["mix_pre_norm"], w["w_in"], cos, sin_s)
    g["w_in"] = _matmul_tn(dproj, n2, "dwin", core)[0]

    started = {}

    def start(n):
        started[n] = _exchange_start(sums[n][1], "start_exchange_" + n, FSR if n in FFN_BIG else None)
        return [started[n]["token"]]

    after = []
    if dist:
        chip_sums(["w_in"], _comm_only(swap(["w_in"]), "swap_w_in")[0])
        after = start("w_in")
    (dx0, da1, db1, dh1, g["ffn1_pre_norm"], g["ffn1_post_norm"]), _ = _ffn_bwd(
        dx1, x, a1, b1, h1, w["ffn1_pre_norm"], w["ffn1_post_norm"], w["ffn1_w_gate"], w["ffn1_w_up"], w["ffn1_w_down"], "ffn1_bwd",
        after)
    total = None
    if dist:
        widen = lambda a: jnp.pad(a, ((0, 0), (0, PACK_W - a.shape[1])))
        pack = jnp.concatenate([widen(g[n]) for n in SMALL] + [g["conv_w"], widen(ss[:, 0:1])])
        assert pack.shape[0] % 8 == 0
        total = _all_reduce_small(pack, after)
        after = [total]
    g["ffn1_w_down"], _ = _matmul_tn(hm1, dh1, "ffn1_dwd", core, after=after)
    g["ffn1_w_gate"], got = _matmul_tn(da1, n1, "ffn1_dwg", core, duties=swap(["ffn1_w_down"]))
    if dist:
        chip_sums(["ffn1_w_down"], got[0])
        after = start("ffn1_w_down")
    g["ffn1_w_up"], got = _matmul_tn(db1, n1, "ffn1_dwu", core, after=after, duties=swap(["ffn1_w_gate"]))
    if dist:
        chip_sums(["ffn1_w_gate"], got[0])
        after = start("ffn1_w_gate")
        chip_sums(["ffn1_w_up"], _comm_only(swap(["ffn1_w_up"]), "swap_ffn1_w_up", after=after)[0])
        start("ffn1_w_up")
    return ss, dx0, g, red, {n: (sums[n][0], started[n]) for n in started}, total


def kernel(x, positions, ffn1_pre_norm, ffn1_w_gate, ffn1_w_up, ffn1_w_down, ffn1_post_norm, mix_pre_norm, w_in, conv_w, conv_b, dt_bias, a_log, d_skip, ssm_norm, w_out, mix_post_norm, ffn2_pre_norm, ffn2_w_gate, ffn2_w_up, ffn2_w_down, ffn2_post_norm, loss_target, m_ffn1_pre_norm, m_ffn1_w_gate, m_ffn1_w_up, m_ffn1_w_down, m_ffn1_post_norm, m_mix_pre_norm, m_w_in, m_conv_w, m_conv_b, m_dt_bias, m_a_log, m_d_skip, m_ssm_norm, m_w_out, m_mix_post_norm, m_ffn2_pre_norm, m_ffn2_w_gate, m_ffn2_w_up, m_ffn2_w_down, m_ffn2_post_norm, v_ffn1_pre_norm, v_ffn1_w_gate, v_ffn1_w_up, v_ffn1_w_down, v_ffn1_post_norm, v_mix_pre_norm, v_w_in, v_conv_w, v_conv_b, v_dt_bias, v_a_log, v_d_skip, v_ssm_norm, v_w_out, v_mix_post_norm, v_ffn2_pre_norm, v_ffn2_w_gate, v_ffn2_w_up, v_ffn2_w_down, v_ffn2_post_norm):
    given = dict(locals())
    drop = lambda n, a: a if n in SMALL else (a[0].T if n in COL_SHARDED else a[0])
    w = {n: drop(n, given[n]) for n in WEIGHTS}
    m = {n: drop(n, given["m_" + n]) for n in WEIGHTS}
    v = {n: drop(n, given["v_" + n]) for n in WEIGHTS}
    cx, cy, cc = _place()
    others = [2 * (1 - cx) + cy, 2 * cx + (1 - cy), 2 * (1 - cx) + (1 - cy)]

    _, grad_x, g, red, pending, total = _step(x[0], positions, loss_target[0], {n: w[n] for n in SMALL},
                                              blocks={n: w[n] for n in BIG + ["conv_w"]})
    chip_ids = jnp.stack(others).astype(jnp.int32)
    out_g, out_d, out_m, out_v = {}, {}, {}, {}

    def update(names, sums, recvs, label, after=()):
        res = _adamw_sharded([w[n] for n in names], [m[n] for n in names], [v[n] for n in names], sums, recvs, chip_ids,
                             "adamw_" + label, after)
        for n, (gn, dn, mn, vn) in zip(names, res):
            out_g[n], out_d[n], out_m[n], out_v[n] = gn, dn, mn, vn

    last_start = [pending["ffn1_w_up"][1]["token"]]
    update(FFN2, [red[n][0] for n in FFN2], [red[n][1] for n in FFN2], "ffn2", last_start)
    update(["w_out"], [red["w_out"][0]], [red["w_out"][1]], "w_out", last_start)

    n_small = len(SMALL)
    conv_g = lax.dynamic_slice_in_dim(total[n_small:n_small + 4], (4 * cx + 2 * cy + cc) * (CONV_C // N_DEV), CONV_C // N_DEV, axis=1)
    loss = 0.5 * total[n_small + 4, 0] / D
    names = SMALL + ["conv_w"]
    res = _adamw_small([w[n] for n in names], [m[n] for n in names], [v[n] for n in names], total, conv_g, last_start)
    for n, (gn, dn, mn, vn) in zip(names, res):
        out_g[n], out_d[n], out_m[n], out_v[n] = gn, dn, mn, vn
    done = [out_v[n] for n in FFN2 + ["w_out", "conv_w"]]
    update(["w_in"], [pending["w_in"][0]], [_exchange_wait(pending["w_in"][1], done, "wait_exchange_w_in")], "w_in")
    done = [out_v["w_in"]]
    update(FFN1, [pending[n][0] for n in FFN1], [_exchange_wait(pending[n][1], done, "wait_exchange_" + n) for n in FFN1], "ffn1")

    outs = [loss, grad_x[None]]
    for d in (out_g, out_d, out_m, out_v):
        outs += [d[n] if n in SMALL else (d[n].T[None] if n in COL_SHARDED else d[n][None]) for n in WEIGHTS]
    return tuple(outs)
```

```python
import functools

import jax
import jax.numpy as jnp
from jax import lax
from jax.experimental import pallas as pl
from jax.experimental.pallas import tpu as pltpu

F32 = jnp.float32
BF16 = jnp.bfloat16
MESH = pl.DeviceIdType.MESH

N_DEV = 8
T = 2048
D = 1024
FF = 2816
FSR = FF // N_DEV
FSH = 384
FFP = N_DEV * FSH
HD = 64
NQ = 16
NKV = 4
QW = NQ * HD
KVW = NKV * HD
SSM_W = 1024
SSM_H = 16
SSM_N = 128
CONV_C = SSM_W + 2 * 2 * SSM_N
IN_COLS = 4112
INP = 4224
ISR = IN_COLS // N_DEV
ISW = 528
ISG = 640
L = 128
NCH = T // L
AB = 256
NAB = T // AB
EPS = 1e-6
NEG = -1e30
ROPE_THETA = 10000.0
DILATIONS = ((128, 1), (512, 4), (2048, 16))

ADAM_LR = 0.001
ADAM_B1 = 0.9
ADAM_B2 = 0.999
ADAM_EPS = 1e-08
ADAM_WD = 0.01
ADAM_STEP = 10

VMEM_LIMIT = 58 * 1024 * 1024


def _params(sem, vmem=VMEM_LIMIT):
    return pltpu.CompilerParams(dimension_semantics=sem, vmem_limit_bytes=vmem)


def _dot(a, b):
    return jnp.dot(a, b, preferred_element_type=F32)


def _dot_nt(a, b):
    return lax.dot_general(a, b, (((1,), (1,)), ((), ())), preferred_element_type=F32)


def _dot_tn(a, b):
    return lax.dot_general(a, b, (((0,), (0,)), ((), ())), preferred_element_type=F32)


def _split3(x):
    hi = x.astype(BF16)
    r1 = x - hi.astype(F32)
    mid = r1.astype(BF16)
    lo = (r1 - mid.astype(F32)).astype(BF16)
    return hi, mid, lo


def _dot_hi(a, b, a_is_01=False):
    if a_is_01:
        sel = a.astype(BF16)
        return sum(_dot(sel, p) for p in _split3(b))
    sel = b.astype(BF16)
    return sum(_dot(p, sel) for p in _split3(a))


def _dot_nt_hi(a, b):
    sel = b.astype(BF16)
    return sum(_dot_nt(p, sel) for p in _split3(a))


def _rs(x):
    return lax.rsqrt(jnp.mean(x * x, axis=-1, keepdims=True) + EPS)


def _sigmoid(x):
    return jax.nn.sigmoid(x)


def _dsilu(x, s):
    return s * (1.0 + x * (1.0 - s))


def _resident(shape):
    nd = len(shape)
    return pl.BlockSpec(shape, lambda *_: (0,) * nd, pipeline_mode=pl.Buffered(1))


def _const(shape):
    nd = len(shape)
    return pl.BlockSpec(shape, lambda *_: (0,) * nd)


def _rows(tm, cols):
    return pl.BlockSpec((tm, cols), lambda i: (i, 0))


ANY = pl.BlockSpec(memory_space=pl.ANY)


def _place():
    return lax.axis_index("x"), lax.axis_index("y"), lax.axis_index("c")


def _gather_duty(arrays):
    n = len(arrays)
    results = [jax.ShapeDtypeStruct((N_DEV,) + a.shape, a.dtype) for a in arrays]

    def make(ins, outs, send_sems, recv_sems, local_sems):
        x, y, c = _place()
        me, sibling = (x, y, c), (x, y, 1 - c)
        chips = [(1 - x, y), (x, 1 - y), (1 - x, 1 - y)]

        def place_of(a, px, py, pc):
            return outs[a].at[4 * px + 2 * py + pc]

        def copy(a, k, block, to, src=None):
            dst = place_of(a, *block)
            return pltpu.make_async_remote_copy(src_ref=dst if src is None else src, dst_ref=dst,
                                                send_sem=send_sems.at[7 * a + k], recv_sem=recv_sems.at[7 * a + k],
                                                device_id=to, device_id_type=MESH)

        def own(a):
            return pltpu.make_async_copy(ins[a], place_of(a, *me), local_sems.at[a])

        def first(a):
            return [copy(a, 0, me, sibling, src=ins[a])] + [copy(a, 1 + j, me, (*chip, c), src=ins[a]) for j, chip in enumerate(chips)]

        def start():
            for a in range(n):
                own(a).start()
            for a in range(n):
                for cp in first(a):
                    cp.start()

        def finish():
            for j, chip in enumerate(chips):
                for a in range(n):
                    copy(a, 1 + j, (*chip, c), me).wait_recv()
                    copy(a, 4 + j, (*chip, c), sibling).start()
            for a in range(n):
                copy(a, 0, sibling, me).wait_recv()
                for j, chip in enumerate(chips):
                    copy(a, 4 + j, (*chip, 1 - c), me).wait_recv()
            for a in range(n):
                for cp in first(a) + [copy(a, 4 + j, (*chip, c), sibling) for j, chip in enumerate(chips)]:
                    cp.wait_send()
                own(a).wait()

        return start, finish

    return dict(operands=list(arrays), results=results, sems=(7 * n, 7 * n, n), make=make)


def _swap_duty(arrays):
    n = len(arrays)
    half = N_DEV // 2
    results = [jax.ShapeDtypeStruct(a.shape, a.dtype) for a in arrays]

    def make(ins, outs, send_sems, recv_sems):
        x, y, c = _place()

        def copies():
            return [pltpu.make_async_remote_copy(src_ref=ins[a].at[k], dst_ref=outs[a].at[k],
                                                 send_sem=send_sems.at[half * a + k], recv_sem=recv_sems.at[half * a + k],
                                                 device_id=(x, y, 1 - c), device_id_type=MESH)
                    for a in range(n) for k in range(half)]

        def start():
            for cp in copies():
                cp.start()

        def finish():
            for cp in copies():
                cp.wait()

        return start, finish

    return dict(operands=list(arrays), results=results, sems=(half * n, half * n), make=make)


def _exchange_duty(arrays):
    n = len(arrays)
    results = [jax.ShapeDtypeStruct(a.shape, a.dtype) for a in arrays]

    def make(ins, outs, send_sems, recv_sems):
        x, y, c = _place()
        chips = [(1 - x, y), (x, 1 - y), (1 - x, 1 - y)]
        my_chip = 2 * x + y

        def sends():
            return [pltpu.make_async_remote_copy(src_ref=ins[a].at[2 * px + py], dst_ref=outs[a].at[my_chip],
                                                 send_sem=send_sems.at[3 * a + j], recv_sem=recv_sems.at[3 * a + j],
                                                 device_id=(px, py, c), device_id_type=MESH)
                    for a in range(n) for j, (px, py) in enumerate(chips)]

        def start():
            for cp in sends():
                cp.start()

        def finish():
            for a in range(n):
                for j, (px, py) in enumerate(chips):
                    pltpu.make_async_remote_copy(src_ref=ins[a].at[my_chip], dst_ref=outs[a].at[2 * px + py],
                                                 send_sem=send_sems.at[3 * a + j], recv_sem=recv_sems.at[3 * a + j],
                                                 device_id=(px, py, c), device_id_type=MESH).wait_recv()
            for cp in sends():
                cp.wait_send()

        return start, finish

    return dict(operands=list(arrays), results=results, sems=(3 * n, 3 * n), make=make)


def _call(body, *, name, grid, in_specs, out_specs, out_shape, args, sem, scratch=(), duties=(), after=()):
    n_in, n_out, n_scr = len(in_specs), len(out_specs), len(scratch)
    sem_shapes = [pltpu.SemaphoreType.DMA((k,)) for d in duties for k in d["sems"]]

    def full(*refs):
        pos = [0]

        def take(k):
            pos[0] += k
            return refs[pos[0] - k:pos[0]]

        ins = take(n_in)
        d_ins = [take(len(d["operands"])) for d in duties]
        take(len(after))
        outs = take(n_out)
        d_outs = [take(len(d["results"])) for d in duties]
        scr = take(n_scr)
        d_sems = [take(len(d["sems"])) for d in duties]
        hooks = [d["make"](di, do, *ds) for d, di, do, ds in zip(duties, d_ins, d_outs, d_sems)]
        if grid and hooks:
            ids = [pl.program_id(k) for k in range(len(grid))]
            first = functools.reduce(jnp.logical_and, [i == 0 for i in ids])
            last = functools.reduce(jnp.logical_and, [i == g - 1 for i, g in zip(ids, grid)])

            @pl.when(first)
            def _():
                for start, _ in hooks:
                    start()

            body(*ins, *outs, *scr)

            @pl.when(last)
            def _():
                for _, finish in hooks:
                    finish()
        else:
            for start, _ in hooks:
                start()
            body(*ins, *outs, *scr)
            for _, finish in hooks:
                finish()

    d_args = [a for d in duties for a in d["operands"]]
    d_res = [r for d in duties for r in d["results"]]
    kwargs = dict(grid=grid) if grid else {}
    res = pl.pallas_call(
        full, name=name, in_specs=list(in_specs) + [ANY] * (len(d_args) + len(after)), out_specs=list(out_specs) + [ANY] * len(d_res),
        out_shape=list(out_shape) + d_res, scratch_shapes=list(scratch) + sem_shapes,
        compiler_params=_params(sem) if grid else None, **kwargs,
    )(*args, *d_args, *after)
    own, rest = list(res[:n_out]), list(res[n_out:])
    by_duty = []
    for d in duties:
        by_duty.append(rest[:len(d["results"])])
        rest = rest[len(d["results"]):]
    return own, by_duty


def _comm_only(duties, name, after=()):
    return _call(lambda: None, name=name, grid=None, in_specs=[], out_specs=[], out_shape=[], args=[], sem=None, duties=duties,
                 after=after)[1]


HBM = pl.BlockSpec(memory_space=pltpu.HBM)
SEMS = pl.BlockSpec(memory_space=pltpu.SEMAPHORE)
SIDE_EFFECT = pltpu.SideEffectType.DATAFLOW_SIDE_EFFECTING
N_OTHER_CHIPS = 3


def _chip_copies(src_ref, land_ref, sems, rows):
    x, y, c = _place()
    chips = [(1 - x, y), (x, 1 - y), (1 - x, 1 - y)]
    part = (lambda ref: ref) if rows is None else (lambda ref: ref.at[pl.ds(0, rows)])
    return [pltpu.make_async_remote_copy(src_ref=part(src_ref.at[2 * px + py]), dst_ref=part(land_ref.at[2 * x + y]),
                                         send_sem=sems[j], recv_sem=sems[N_OTHER_CHIPS + j], device_id=(px, py, c), device_id_type=MESH)
            for j, (px, py) in enumerate(chips)]


def _exchange_start(pb, name, rows=None):
    n_sem = 2 * N_OTHER_CHIPS

    def body(pb_ref, land_ref, *rest):
        for cp in _chip_copies(pb_ref, land_ref, rest[:n_sem], rows):
            cp.start()
        token = rest[n_sem + 2]
        token[...] = jnp.zeros_like(token)

    res = pl.pallas_call(
        body, name=name,
        out_shape=(pltpu.SemaphoreType.DMA(()),) * n_sem + (pltpu.HBM(pb.shape, pb.dtype), pltpu.HBM(pb.shape, pb.dtype),
                                                              jax.ShapeDtypeStruct((8, 128), F32)),
        in_specs=(HBM, HBM), out_specs=(SEMS,) * n_sem + (HBM, HBM, pl.BlockSpec(memory_space=pltpu.VMEM)),
        input_output_aliases={0: n_sem, 1: n_sem + 1},
        compiler_params=pltpu.CompilerParams(has_side_effects=SIDE_EFFECT),
    )(pltpu.with_memory_space_constraint(pb, pltpu.HBM), pltpu.with_memory_space_constraint(lax.empty(pb.shape, pb.dtype), pltpu.HBM))
    return dict(sems=res[:n_sem], src=res[n_sem], land=res[n_sem + 1], token=res[n_sem + 2], rows=rows)


def _exchange_wait(started, after, name):
    n_sem = 2 * N_OTHER_CHIPS

    def body(pb_ref, land_ref, *rest):
        for cp in _chip_copies(pb_ref, land_ref, rest[:n_sem], started["rows"]):
            cp.wait_send()
            cp.wait_recv()

    src, land = started["src"], started["land"]
    return pl.pallas_call(
        body, name=name, out_shape=(pltpu.HBM(src.shape, src.dtype), pltpu.HBM(land.shape, land.dtype)),
        in_specs=(HBM, HBM) + (SEMS,) * n_sem + (ANY,) * len(after), out_specs=(HBM, HBM), input_output_aliases={0: 0, 1: 1},
        compiler_params=pltpu.CompilerParams(has_side_effects=SIDE_EFFECT),
    )(src, land, *started["sems"], *after)[1]


def _ffn_fwd(x, gpre, wg, wu, wd, gpost, name, duties=(), target=None):
    tm = 256
    n_in = 6 if target is None else 7

    def body(*refs):
        x_ref, gpre_ref, wg_ref, wu_ref, wd_ref, gpost_ref = refs[:6]
        xo_ref, n_ref, a_ref, b_ref, hm_ref, h_ref = refs[n_in:n_in + 6]
        xv = x_ref[...]
        n = (xv * _rs(xv) * gpre_ref[...]).astype(BF16)
        a = _dot_nt(n, wg_ref[...])
        b = _dot_nt(n, wu_ref[...])
        hm = (a * _sigmoid(a) * b).astype(BF16)
        h = _dot(hm, wd_ref[...])
        xo = xv + 0.5 * (h * _rs(h) * gpost_ref[...])
        n_ref[...] = n
        a_ref[...] = a.astype(BF16)
        b_ref[...] = b.astype(BF16)
        hm_ref[...] = hm
        h_ref[...] = h
        if target is None:
            xo_ref[...] = xo
        else:
            ss_ref = refs[n_in + 6]

            @pl.when(pl.program_id(0) == 0)
            def _():
                ss_ref[...] = jnp.zeros_like(ss_ref)

            err = xo - refs[6][...]
            xo_ref[...] = err * (1.0 / D)
            ss_ref[...] += jnp.sum(jnp.sum(err * err, axis=1, keepdims=True), axis=0, keepdims=True)

    loss_in = [] if target is None else [_rows(tm, D)]
    loss_out = [] if target is None else [_const((1, 128))]
    loss_shape = [] if target is None else [jax.ShapeDtypeStruct((1, 128), F32)]
    return _call(
        body, name=name, grid=(T // tm,),
        in_specs=[_rows(tm, D), _const((1, D)), _resident((FFP, D)), _resident((FFP, D)), _resident((FFP, D)), _const((1, D))] + loss_in,
        out_specs=[_rows(tm, D), _rows(tm, D), _rows(tm, FFP), _rows(tm, FFP), _rows(tm, FFP), _rows(tm, D)] + loss_out,
        out_shape=[jax.ShapeDtypeStruct((T, D), F32), jax.ShapeDtypeStruct((T, D), BF16), jax.ShapeDtypeStruct((T, FFP), BF16),
                   jax.ShapeDtypeStruct((T, FFP), BF16), jax.ShapeDtypeStruct((T, FFP), BF16), jax.ShapeDtypeStruct((T, D), F32)]
        + loss_shape,
        args=[x, gpre, wg, wu, wd, gpost] + ([] if target is None else [target]), sem=("arbitrary",), duties=duties)


def _ffn_up(x, gpre, wg, wu, name, duties=()):
    tm = 256

    def body(x_ref, gpre_ref, wg_ref, wu_ref, n_ref, a_ref, b_ref, hm_ref):
        xv = x_ref[...]
        n = (xv * _rs(xv) * gpre_ref[...]).astype(BF16)
        a = _dot_nt(n, wg_ref[...])
        b = _dot_nt(n, wu_ref[...])
        n_ref[...] = n
        a_ref[...] = a.astype(BF16)
        b_ref[...] = b.astype(BF16)
        hm_ref[...] = (a * _sigmoid(a) * b).astype(BF16)

    return _call(
        body, name=name, grid=(T // tm,),
        in_specs=[_rows(tm, D), _const((1, D)), _resident((FFP, D)), _resident((FFP, D))],
        out_specs=[_rows(tm, D), _rows(tm, FFP), _rows(tm, FFP), _rows(tm, FFP)],
        out_shape=[jax.ShapeDtypeStruct((T, D), BF16), jax.ShapeDtypeStruct((T, FFP), BF16),
                   jax.ShapeDtypeStruct((T, FFP), BF16), jax.ShapeDtypeStruct((T, FFP), BF16)],
        args=[x, gpre, wg, wu], sem=("arbitrary",), duties=duties)


def _ffn_down(x, hm, wd, gpost, name, duties=()):
    tm = 256

    def body(x_ref, hm_ref, wd_ref, gpost_ref, xo_ref, h_ref):
        h = _dot(hm_ref[...], wd_ref[...])
        xo_ref[...] = x_ref[...] + 0.5 * (h * _rs(h) * gpost_ref[...])
        h_ref[...] = h

    return _call(
        body, name=name, grid=(T // tm,),
        in_specs=[_rows(tm, D), _rows(tm, FFP), _resident((FFP, D)), _const((1, D))],
        out_specs=[_rows(tm, D), _rows(tm, D)],
        out_shape=[jax.ShapeDtypeStruct((T, D), F32), jax.ShapeDtypeStruct((T, D), F32)],
        args=[x, hm, wd, gpost], sem=("arbitrary",), duties=duties)


def _ffn_bwd(dxo, x, a, b, h, gpre, gpost, wg, wu, wd, name, after=()):
    tm = 256

    def body(dxo_ref, x_ref, a_ref, b_ref, h_ref, gpre_ref, gpost_ref, wg_ref, wu_ref, wd_ref,
             dx_ref, da_ref, db_ref, dh_ref, dgpre_ref, dgpost_ref):
        @pl.when(pl.program_id(0) == 0)
        def _():
            dgpre_ref[...] = jnp.zeros_like(dgpre_ref)
            dgpost_ref[...] = jnp.zeros_like(dgpost_ref)

        dy = dxo_ref[...]
        h = h_ref[...]
        hn = h * _rs(h)
        r2 = _rs(h)
        dgpost_ref[...] += jnp.sum(0.5 * dy * hn, axis=0, keepdims=True)
        gdy = 0.5 * dy * gpost_ref[...]
        dh = r2 * (gdy - hn * jnp.mean(gdy * hn, axis=-1, keepdims=True))
        dhb = dh.astype(BF16)
        dh_ref[...] = dhb
        dhm = _dot_nt(dhb, wd_ref[...])
        av = a_ref[...].astype(F32)
        bv = b_ref[...].astype(F32)
        sg = _sigmoid(av)
        db = (dhm * (av * sg)).astype(BF16)
        da = (dhm * bv * _dsilu(av, sg)).astype(BF16)
        da_ref[...] = da
        db_ref[...] = db
        dn = _dot(da, wg_ref[...]) + _dot(db, wu_ref[...])
        xv = x_ref[...]
        r = _rs(xv)
        xn = xv * r
        dgpre_ref[...] += jnp.sum(dn * xn, axis=0, keepdims=True)
        gdn = dn * gpre_ref[...]
        dx_ref[...] = dy + r * (gdn - xn * jnp.mean(gdn * xn, axis=-1, keepdims=True))

    return _call(
        body, name=name, grid=(T // tm,),
        in_specs=[_rows(tm, D), _rows(tm, D), _rows(tm, FFP), _rows(tm, FFP), _rows(tm, D), _const((1, D)), _const((1, D)),
                  _resident((FFP, D)), _resident((FFP, D)), _resident((FFP, D))],
        out_specs=[_rows(tm, D), _rows(tm, FFP), _rows(tm, FFP), _rows(tm, D), _const((1, D)), _const((1, D))],
        out_shape=[jax.ShapeDtypeStruct((T, D), F32), jax.ShapeDtypeStruct((T, FFP), BF16), jax.ShapeDtypeStruct((T, FFP), BF16),
                   jax.ShapeDtypeStruct((T, D), BF16), jax.ShapeDtypeStruct((1, D), F32), jax.ShapeDtypeStruct((1, D), F32)],
        args=[dxo, x, a, b, h, gpre, gpost, wg, wu, wd], sem=("arbitrary",), after=after)


def _core_index(core):
    return lax.axis_index("c") if core == "mesh" else core


def _by_core(res, o_ref, ob_ref, core):
    r = res.shape[0] // 2
    c = jnp.asarray(_core_index(core))

    @pl.when(c == 0)
    def _():
        o_ref[0] = res[:r]
        ob_ref[0] = res[r:].astype(BF16)

    @pl.when(c == 1)
    def _():
        o_ref[0] = res[r:]
        ob_ref[0] = res[:r].astype(BF16)


def _matmul_tn(a, b, name, core, after=(), duties=()):
    k, m = a.shape
    n = b.shape[1]
    r = m // N_DEV
    assert m == N_DEV * r and r % 128 == 0

    def body(a_ref, b_ref, o_ref, ob_ref):
        _by_core(_dot_tn(a_ref[...], b_ref[...]), o_ref, ob_ref, core)

    spec = pl.BlockSpec((1, r, n), lambda i: (i, 0, 0))
    return _call(body, name=name, grid=(N_DEV // 2,), in_specs=[pl.BlockSpec((k, 2 * r), lambda i: (0, i)), _resident((k, n))],
                 out_specs=[spec, spec],
                 out_shape=[jax.ShapeDtypeStruct((N_DEV // 2, r, n), F32), jax.ShapeDtypeStruct((N_DEV // 2, r, n), BF16)],
                 args=[a, b], sem=("arbitrary",), duties=duties, after=after)


def _dwout(attn, yn, dh2, core):
    rs = (QW + SSM_W) // N_DEV
    chips = N_DEV // 2

    def body(at_ref, yn_ref, dh_ref, o_ref, ob_ref):
        i = pl.program_id(0)

        @pl.when(i < chips // 2)
        def _():
            _by_core(_dot_tn(at_ref[...], dh_ref[...]), o_ref, ob_ref, core)

        @pl.when(i >= chips // 2)
        def _():
            _by_core(_dot_tn(yn_ref[...], dh_ref[...]), o_ref, ob_ref, core)

    spec = pl.BlockSpec((1, rs, D), lambda i: (i, 0, 0))
    return pl.pallas_call(
        body, name="dwout", grid=(chips,),
        in_specs=[pl.BlockSpec((T, 2 * rs), lambda i: (0, jnp.minimum(i, chips // 2 - 1))),
                  pl.BlockSpec((T, 2 * rs), lambda i: (0, jnp.maximum(i - chips // 2, 0))), _resident((T, D))],
        out_specs=[spec, spec],
        out_shape=[jax.ShapeDtypeStruct((chips, rs, D), F32), jax.ShapeDtypeStruct((chips, rs, D), BF16)],
        compiler_params=_params(("arbitrary",)),
    )(attn, yn, dh2)


def _rope_swap(t, lo_half):
    return jnp.where(lo_half, pltpu.roll(t, 96, 1), pltpu.roll(t, 32, 1))


def _to_query_heads(t):
    return jnp.concatenate([t[:, HD * (h // G_PER):HD * (h // G_PER + 1)] for h in range(NQ)], axis=1)


def _from_query_heads(t):
    def kv_sum(kv):
        parts = [t[:, HD * (G_PER * kv + g):HD * (G_PER * kv + g + 1)] for g in range(G_PER)]
        return (parts[0] + parts[1]) + (parts[2] + parts[3])

    return jnp.concatenate([kv_sum(kv) for kv in range(NKV)], axis=1)


def _inproj_fwd(x1, gpre, win, cos, sin_s, duties=()):
    tm = 256

    def body(x_ref, g_ref, w_ref, cos_ref, sin_ref, n_ref, q_ref, kx_ref, vx_ref, xbc_ref, z_ref, dt_ref):
        xv = x_ref[...]
        n = (xv * _rs(xv) * g_ref[...]).astype(BF16)
        n_ref[...] = n
        by_dev = _dot_nt(n, w_ref[...])
        proj = jnp.concatenate([by_dev[:, ISW * d:ISW * d + ISR] for d in range(N_DEV)], axis=1)
        cs = cos_ref[...]
        sn = sin_ref[...]
        lo_half = (lax.broadcasted_iota(jnp.int32, (1, 128), 1) % HD) < (HD // 2)

        def rope(t):
            return t * cs + _rope_swap(t, lo_half) * sn

        for j in range(QW // 128):
            t = proj[:, 128 * j:128 * j + 128]
            q_ref[:, 128 * j:128 * j + 128] = (rope(t) * (HD ** -0.5)).astype(BF16)
        k = jnp.concatenate([rope(proj[:, QW + 128 * j:QW + 128 * j + 128]) for j in range(KVW // 128)], axis=1)
        v = proj[:, QW + KVW:QW + 2 * KVW]
        kx_ref[...] = _to_query_heads(k).astype(BF16)
        vx_ref[...] = _to_query_heads(v).astype(BF16)
        c0 = QW + 2 * KVW
        xbc_ref[...] = proj[:, c0:c0 + CONV_C]
        z_ref[...] = proj[:, c0 + CONV_C:c0 + CONV_C + SSM_W]
        dt_ref[...] = proj[:, c0 + CONV_C + SSM_W:IN_COLS]

    return _call(
        body, name="inproj_fwd", grid=(T // tm,),
        in_specs=[_rows(tm, D), _const((1, D)), _resident((INP, D)), _rows(tm, 128), _rows(tm, 128)],
        out_specs=[_rows(tm, D), _rows(tm, QW), _rows(tm, QW), _rows(tm, QW), _rows(tm, CONV_C), _rows(tm, SSM_W), _rows(tm, SSM_H)],
        out_shape=[jax.ShapeDtypeStruct((T, D), BF16), jax.ShapeDtypeStruct((T, QW), BF16), jax.ShapeDtypeStruct((T, QW), BF16),
                   jax.ShapeDtypeStruct((T, QW), BF16), jax.ShapeDtypeStruct((T, CONV_C), F32), jax.ShapeDtypeStruct((T, SSM_W), F32),
                   jax.ShapeDtypeStruct((T, SSM_H), F32)],
        args=[x1, gpre, win, cos, sin_s], sem=("arbitrary",), duties=duties)


def _inproj_bwd(dres, dq, dkx, dvx, dxbc, dz, ddt, x1, gpre, win, cos, sin_s):
    tm = 256

    def body(dres_ref, dq_ref, dkx_ref, dvx_ref, dxbc_ref, dz_ref, ddt_ref, x_ref, g_ref, w_ref, cos_ref, sin_ref,
             dx_ref, dps_ref, dg_ref, dp_ref):
        @pl.when(pl.program_id(0) == 0)
        def _():
            dg_ref[...] = jnp.zeros_like(dg_ref)

        cs = cos_ref[...]
        sn = sin_ref[...]
        lo_half = (lax.broadcasted_iota(jnp.int32, (1, 128), 1) % HD) < (HD // 2)

        def rope_t(t):
            return t * cs - _rope_swap(t, lo_half) * sn

        for j in range(QW // 128):
            dp_ref[:, 128 * j:128 * j + 128] = rope_t(dq_ref[:, 128 * j:128 * j + 128] * (HD ** -0.5)).astype(BF16)
        dk = _from_query_heads(dkx_ref[...])
        dv = _from_query_heads(dvx_ref[...])
        for j in range(KVW // 128):
            dp_ref[:, QW + 128 * j:QW + 128 * j + 128] = rope_t(dk[:, 128 * j:128 * j + 128]).astype(BF16)
        dp_ref[:, QW + KVW:QW + 2 * KVW] = dv.astype(BF16)
        c0 = QW + 2 * KVW
        dp_ref[:, c0:c0 + CONV_C] = dxbc_ref[...].astype(BF16)
        dp_ref[:, c0 + CONV_C:c0 + CONV_C + SSM_W] = dz_ref[...].astype(BF16)
        dp_ref[:, c0 + CONV_C + SSM_W:INP] = ddt_ref[...].astype(BF16)
        pieces = [dp_ref[:, ISR * d:ISR * (d + 1)] for d in range(N_DEV)]
        zw = jnp.zeros((tm, ISW - ISR), BF16)
        zg = jnp.zeros((tm, ISG - ISR), BF16)
        dn = _dot(jnp.concatenate([t for p in pieces for t in (p, zw)], axis=1), w_ref[...])
        for d in range(N_DEV):
            dps_ref[:, ISG * d:ISG * (d + 1)] = jnp.concatenate([pieces[d], zg], axis=1)
        xv = x_ref[...]
        r = _rs(xv)
        xn = xv * r
        dg_ref[...] += jnp.sum(dn * xn, axis=0, keepdims=True)
        gdn = dn * g_ref[...]
        dx_ref[...] = dres_ref[...] + r * (gdn - xn * jnp.mean(gdn * xn, axis=-1, keepdims=True))

    return pl.pallas_call(
        body, name="inproj_bwd", grid=(T // tm,),
        in_specs=[_rows(tm, D), _rows(tm, QW), _rows(tm, QW), _rows(tm, QW), _rows(tm, CONV_C), _rows(tm, SSM_W), _rows(tm, 128),
                  _rows(tm, D), _const((1, D)), _resident((INP, D)), _rows(tm, 128), _rows(tm, 128)],
        out_specs=[_rows(tm, D), _rows(tm, N_DEV * ISG), _const((1, D))],
        out_shape=[jax.ShapeDtypeStruct((T, D), F32), jax.ShapeDtypeStruct((T, N_DEV * ISG), BF16), jax.ShapeDtypeStruct((1, D), F32)],
        scratch_shapes=[pltpu.VMEM((tm, INP), BF16)],
        compiler_params=_params(("arbitrary",)),
    )(dres, dq, dkx, dvx, dxbc, dz, ddt, x1, gpre, win, cos, sin_s)


def _outproj_fwd(x1, attn, yn, wout, gpost):
    tm = 512

    def body(x_ref, at_ref, yn_ref, w_ref, g_ref, xo_ref, h_ref):
        h = _dot(at_ref[...], w_ref[0:QW, :]) + _dot(yn_ref[...], w_ref[QW:QW + SSM_W, :])
        h_ref[...] = h
        xo_ref[...] = x_ref[...] + h * _rs(h) * g_ref[...]

    return pl.pallas_call(
        body, name="outproj_fwd", grid=(T // tm,),
        in_specs=[_rows(tm, D), _rows(tm, QW), _rows(tm, SSM_W), _resident((QW + SSM_W, D)), _const((1, D))],
        out_specs=[_rows(tm, D), _rows(tm, D)],
        out_shape=[jax.ShapeDtypeStruct((T, D), F32), jax.ShapeDtypeStruct((T, D), F32)],
        compiler_params=_params(("parallel",)),
    )(x1, attn, yn, wout, gpost)


def _outproj_bwd(dx2, h2, gpost, wout, duties=()):
    tm = 512

    def body(dy_ref, h_ref, g_ref, w_ref, dh_ref, dm_ref, dg_ref):
        @pl.when(pl.program_id(0) == 0)
        def _():
            dg_ref[...] = jnp.zeros_like(dg_ref)

        dy = dy_ref[...]
        h = h_ref[...]
        r = _rs(h)
        hn = h * r
        dg_ref[...] += jnp.sum(dy * hn, axis=0, keepdims=True)
        gdy = dy * g_ref[...]
        dh = (r * (gdy - hn * jnp.mean(gdy * hn, axis=-1, keepdims=True))).astype(BF16)
        dh_ref[...] = dh
        dm_ref[...] = _dot_nt(dh, w_ref[...])

    return _call(
        body, name="outproj_bwd", grid=(T // tm,),
        in_specs=[_rows(tm, D), _rows(tm, D), _const((1, D)), _resident((QW + SSM_W, D))],
        out_specs=[_rows(tm, D), _rows(tm, QW + SSM_W), _const((1, D))],
        out_shape=[jax.ShapeDtypeStruct((T, D), BF16), jax.ShapeDtypeStruct((T, QW + SSM_W), F32), jax.ShapeDtypeStruct((1, D), F32)],
        args=[dx2, h2, gpost, wout], sem=("arbitrary",), duties=duties)


def _attn_bias():
    d = jnp.arange(AB)[:, None] - jnp.arange(T)[None, :] + (T - AB)
    cnt = jnp.zeros(d.shape, F32)
    for window, dil in DILATIONS:
        cnt = cnt + ((d >= 0) & (d % dil == 0) & (d <= window)).astype(F32)
    return jnp.where(cnt > 0, jnp.log(jnp.maximum(cnt, 1.0)), NEG)


G_PER = NQ // NKV
WK = G_PER * HD


def _attn_fwd(q, kx, vx, bias, duties=()):
    def body(q_ref, kx_ref, vx_ref, bias_ref, o_ref, lse_ref):
        lane = lax.broadcasted_iota(jnp.int32, (1, WK), 1)
        lse_ref[...] = jnp.zeros_like(lse_ref)
        for i in range(NAB):
            n = (i + 1) * AB
            rows = slice(i * AB, n)
            qi = q_ref[rows, :]
            kxi = kx_ref[0:n, :]
            vxi = vx_ref[0:n, :]
            bb = bias_ref[:, (NAB - 1 - i) * AB:]
            o_acc = jnp.zeros((AB, WK), F32)
            for g in range(G_PER):
                mg = (lane // HD) == g
                s = _dot_nt(jnp.where(mg, qi, jnp.zeros_like(qi)), kxi) + bb
                m = jnp.max(s, axis=1, keepdims=True)
                p = jnp.exp(s - m)
                l = jnp.sum(p, axis=1, keepdims=True)
                o_acc = jnp.where(mg, _dot(p.astype(BF16), vxi) / l, o_acc)
                lse_ref[rows, g:g + 1] = m + jnp.log(l)
            o_ref[rows, :] = o_acc.astype(BF16)

    col = lambda kv: (0, kv)
    return _call(
        body, name="attn_fwd", grid=(NKV,),
        in_specs=[pl.BlockSpec((T, WK), col), pl.BlockSpec((T, WK), col), pl.BlockSpec((T, WK), col), _const((AB, T))],
        out_specs=[pl.BlockSpec((T, WK), col), pl.BlockSpec((T, 128), col)],
        out_shape=[jax.ShapeDtypeStruct((T, QW), BF16), jax.ShapeDtypeStruct((T, NKV * 128), F32)],
        args=[q, kx, vx, bias], sem=("arbitrary",), duties=duties)


def _attn_bwd(q, kx, vx, o, dmix, lse, bias, duties=()):
    def body(q_ref, kx_ref, vx_ref, o_ref, do_ref, lse_ref, bias_ref, dq_ref, dkx_ref, dvx_ref):
        lane = lax.broadcasted_iota(jnp.int32, (1, WK), 1)
        dkx_ref[...] = jnp.zeros_like(dkx_ref)
        dvx_ref[...] = jnp.zeros_like(dvx_ref)
        for i in range(NAB):
            n = (i + 1) * AB
            rows = slice(i * AB, n)
            qi = q_ref[rows, :]
            dof = do_ref[rows, :]
            doi = dof.astype(BF16)
            prod = dof * o_ref[rows, :].astype(F32)
            kxi = kx_ref[0:n, :]
            vxi = vx_ref[0:n, :]
            bb = bias_ref[:, (NAB - 1 - i) * AB:]
            dq_acc = jnp.zeros((AB, WK), F32)
            for g in range(G_PER):
                mg = (lane // HD) == g
                qm = jnp.where(mg, qi, jnp.zeros_like(qi))
                dom = jnp.where(mg, doi, jnp.zeros_like(doi))
                delta = jnp.sum(jnp.where(mg, prod, 0.0), axis=1, keepdims=True)
                p = jnp.exp(_dot_nt(qm, kxi) + bb - lse_ref[rows, g:g + 1])
                ds = (p * (_dot_nt(dom, vxi) - delta)).astype(BF16)
                dvx_ref[0:n, :] += _dot_tn(p.astype(BF16), dom)
                dkx_ref[0:n, :] += _dot_tn(ds, qm)
                dq_acc = jnp.where(mg, _dot(ds, kxi), dq_acc)
            dq_ref[rows, :] = dq_acc

    col = lambda kv: (0, kv)
    return _call(
        body, name="attn_bwd", grid=(NKV,),
        in_specs=[pl.BlockSpec((T, WK), col), pl.BlockSpec((T, WK), col), pl.BlockSpec((T, WK), col), pl.BlockSpec((T, WK), col),
                  pl.BlockSpec((T, WK), col), pl.BlockSpec((T, 128), col), _const((AB, T))],
        out_specs=[pl.BlockSpec((T, WK), col), pl.BlockSpec((T, WK), col), pl.BlockSpec((T, WK), col)],
        out_shape=[jax.ShapeDtypeStruct((T, QW), F32)] * 3,
        args=[q, kx, vx, o, dmix, lse, bias], sem=("arbitrary",), duties=duties)


def _softplus(x):
    return jnp.maximum(x, 0.0) + jnp.log1p(jnp.exp(-jnp.abs(x)))


def _causal_conv(u, zs, cw_ref, cb_ref):
    zs[8:, :] = u
    sh1, sh2, sh3 = (zs[8 - m:8 - m + L, :] for m in (1, 2, 3))
    return cb_ref[...] + cw_ref[3:4, :] * u + cw_ref[2:3, :] * sh1 + cw_ref[1:2, :] * sh2 + cw_ref[0:1, :] * sh3


def _ssd_chunk_common(xc, dtr, dtb_ref, alx_ref, e_ref):
    sg = _sigmoid(xc)
    act = xc * sg
    pre = dtr + dtb_ref[...]
    dt_x = _dot_hi(_softplus(pre), e_ref[...])
    a_x = -jnp.exp(alx_ref[...])
    ri = lax.broadcasted_iota(jnp.int32, (L, L), 0)
    ci = lax.broadcasted_iota(jnp.int32, (L, L), 1)
    tri = ri >= ci
    acs_x = _dot_hi(tri, dt_x * a_x, a_is_01=True)
    return dict(sg=sg, act=act, pre=pre, dt_x=dt_x, a_x=a_x, tri=tri, acs_x=acs_x)


def _decay(acs_x, acs_t, h, tri):
    col = acs_x[:, HD * h:HD * h + 1]
    row = acs_t[HD * h:HD * h + 1, :]
    return jnp.exp(jnp.where(tri, col - row, NEG))


def _ssd_fwd(xbc, z, dtr, convw, convb, dtb, alx, dskx, ssmn, e, duties=()):
    def body(u_ref, z_ref, dtr_ref, cw_ref, cb_ref, dtb_ref, alx_ref, dsk_ref, sn_ref, e_ref,
             yn_ref, y_ref, hs_ref, xc_ref, zs, hst):
        @pl.when(pl.program_id(0) == 0)
        def _():
            zs[0:8, :] = jnp.zeros((8, CONV_C), F32)
            hst[...] = jnp.zeros_like(hst)

        u = u_ref[...]
        xc = _causal_conv(u, zs, cw_ref, cb_ref)
        xc_ref[...] = xc
        zs[0:8, :] = u[L - 8:, :]
        cm = _ssd_chunk_common(xc, dtr_ref[...], dtb_ref, alx_ref, e_ref)
        act, dt_x, acs_x, tri = cm["act"], cm["dt_x"], cm["acs_x"], cm["tri"]
        xs = act[:, :SSM_W]
        acs_l = acs_x[L - 1:L, :]
        lam_x = jnp.exp(acs_x)
        w_x = jnp.exp(acs_l - acs_x)
        gam_x = jnp.exp(acs_l)
        acs_t = acs_x.T
        xd = xs * dt_x
        xb = xd.astype(BF16)
        xw = (xd * w_x).astype(BF16)
        lo = lax.broadcasted_iota(jnp.int32, (1, 128), 1) < HD
        hs_ref[0] = hst[...]
        pieces = []
        for grp in range(2):
            bb = act[:, SSM_W + SSM_N * grp:SSM_W + SSM_N * (grp + 1)].astype(BF16)
            cb_ = act[:, SSM_W + 2 * SSM_N + SSM_N * grp:SSM_W + 2 * SSM_N + SSM_N * (grp + 1)].astype(BF16)
            cbm = _dot_nt(cb_, bb)
            for jj in range(4):
                j = 4 * grp + jj
                sl = slice(128 * j, 128 * j + 128)
                m0 = (cbm * _decay(acs_x, acs_t, 2 * j, tri)).astype(BF16)
                m1 = (cbm * _decay(acs_x, acs_t, 2 * j + 1, tri)).astype(BF16)
                x2 = xb[:, sl]
                ydiag = jnp.where(lo, _dot(m0, x2), _dot(m1, x2))
                hprev = hst[j]
                yoff = lam_x[:, sl] * _dot(cb_, hprev.astype(BF16))
                pieces.append(ydiag + yoff)
                hst[j] = gam_x[:, sl] * hprev + _dot_tn(bb, xw[:, sl])
        y = jnp.concatenate(pieces, axis=1) + dsk_ref[...] * xs
        y_ref[...] = y
        zv = z_ref[...]
        yz = y * (zv * _sigmoid(zv))
        half = SSM_W // 2
        yn = jnp.concatenate([yz[:, :half] * _rs(yz[:, :half]), yz[:, half:] * _rs(yz[:, half:])], axis=1)
        yn_ref[...] = (yn * sn_ref[...]).astype(BF16)

    return _call(
        body, name="ssd_fwd", grid=(NCH,),
        in_specs=[_rows(L, CONV_C), _rows(L, SSM_W), _rows(L, SSM_H), _const((4, CONV_C)), _const((1, CONV_C)), _const((1, SSM_H)),
                  _const((1, SSM_W)), _const((1, SSM_W)), _const((1, SSM_W)), _const((SSM_H, SSM_W))],
        out_specs=[_rows(L, SSM_W), _rows(L, SSM_W), pl.BlockSpec((1, 8, SSM_N, 128), lambda c: (c, 0, 0, 0)), _rows(L, CONV_C)],
        out_shape=[jax.ShapeDtypeStruct((T, SSM_W), BF16), jax.ShapeDtypeStruct((T, SSM_W), F32),
                   jax.ShapeDtypeStruct((NCH, 8, SSM_N, 128), F32), jax.ShapeDtypeStruct((T, CONV_C), F32)],
        scratch=[pltpu.VMEM((8 + L, CONV_C), F32), pltpu.VMEM((8, SSM_N, 128), F32)],
        args=[xbc, z, dtr, convw, convb, dtb, alx, dskx, ssmn, e], sem=("arbitrary",), duties=duties)


def _ssd_bwd(dmix, xbc, xconv, z, dtr, y, hs, convw, dtb, alx, dskx, ssmn, e, e1, duties=()):
    rev = lambda i: (NCH - 1 - i, 0)

    def body(dyn_ref, u_ref, xc_ref, z_ref, dtr_ref, y_ref, hs_ref, cw_ref, dtb_ref, alx_ref, dsk_ref, sn_ref, e_ref, e1_ref,
             dxbc_ref, dz_ref, ddt_ref, dcw_ref, dcb_ref, dsn_ref, dpar_ref, dh, zd, colbuf):
        step = pl.program_id(0)

        @pl.when(step == 0)
        def _():
            for r in (dh, dcw_ref, dcb_ref, dsn_ref, dpar_ref):
                r[...] = jnp.zeros_like(r)
            zd[L:, :] = jnp.zeros((8, CONV_C), F32)

        u = u_ref[...]
        xc = xc_ref[...]
        cm = _ssd_chunk_common(xc, dtr_ref[...], dtb_ref, alx_ref, e_ref)
        sg, act, pre, dt_x, a_x, tri, acs_x = (cm[k] for k in ("sg", "act", "pre", "dt_x", "a_x", "tri", "acs_x"))
        xs = act[:, :SSM_W]
        acs_l = acs_x[L - 1:L, :]
        lam_x = jnp.exp(acs_x)
        w_x = jnp.exp(acs_l - acs_x)
        gam_x = jnp.exp(acs_l)
        acs_t = acs_x.T
        xd = xs * dt_x
        xb = xd.astype(BF16)
        xdw = xd * w_x
        xw = xdw.astype(BF16)
        lo = lax.broadcasted_iota(jnp.int32, (1, 128), 1) < HD
        row8 = lax.broadcasted_iota(jnp.int32, (8, 1), 0)

        dyn = dyn_ref[...]
        yv = y_ref[...]
        zv = z_ref[...]
        sz = _sigmoid(zv)
        siluz = zv * sz
        yz = yv * siluz
        half = SSM_W // 2
        gy = dyn * sn_ref[...]
        dyz_parts, yzn_parts = [], []
        for hf in range(2):
            part = yz[:, hf * half:(hf + 1) * half]
            r = _rs(part)
            pn = part * r
            gp = gy[:, hf * half:(hf + 1) * half]
            dyz_parts.append(r * (gp - pn * jnp.mean(gp * pn, axis=-1, keepdims=True)))
            yzn_parts.append(pn)
        dyz = jnp.concatenate(dyz_parts, axis=1)
        dsn_ref[...] += jnp.sum(dyn * jnp.concatenate(yzn_parts, axis=1), axis=0, keepdims=True)
        dy = dyz * siluz
        dz_ref[...] = dyz * yv * _dsilu(zv, sz)

        colbuf[...] = jnp.zeros_like(colbuf)
        dx_pieces, dacs_pieces, dacsl_pieces, db_pieces, dc_pieces = [], [], [], [], []
        for grp in range(2):
            bb = act[:, SSM_W + SSM_N * grp:SSM_W + SSM_N * (grp + 1)].astype(BF16)
            cb_ = act[:, SSM_W + 2 * SSM_N + SSM_N * grp:SSM_W + 2 * SSM_N + SSM_N * (grp + 1)].astype(BF16)
            cbm = _dot_nt(cb_, bb)
            dcbm = jnp.zeros((L, L), F32)
            dc_g = jnp.zeros((L, SSM_N), F32)
            db_g = jnp.zeros((L, SSM_N), F32)
            for jj in range(4):
                j = 4 * grp + jj
                sl = slice(128 * j, 128 * j + 128)
                dy2 = dy[:, sl]
                dy2b = dy2.astype(BF16)
                d0 = _decay(acs_x, acs_t, 2 * j, tri)
                d1 = _decay(acs_x, acs_t, 2 * j + 1, tri)
                m0 = cbm * d0
                m1 = cbm * d1
                x2 = xb[:, sl]
                hprev = hs_ref[0, j]
                hprevb = hprev.astype(BF16)
                dhn = dh[j]
                dhnb = dhn.astype(BF16)
                g2 = _dot(bb, dhnb)
                dx_pieces.append(jnp.where(lo, _dot_tn(m0.astype(BF16), dy2b), _dot_tn(m1.astype(BF16), dy2b)) + w_x[:, sl] * g2)
                zero = jnp.zeros_like(dy2b)
                dm0 = _dot_nt(jnp.where(lo, dy2b, zero), x2)
                dm1 = _dot_nt(jnp.where(lo, zero, dy2b), x2)
                dcbm = dcbm + dm0 * d0 + dm1 * d1
                e0 = dm0 * m0
                e1v = dm1 * m1
                colbuf[:, 2 * j:2 * j + 1] = jnp.sum(e0, axis=1, keepdims=True) - jnp.sum(e0.T, axis=1, keepdims=True)
                colbuf[:, 2 * j + 1:2 * j + 2] = jnp.sum(e1v, axis=1, keepdims=True) - jnp.sum(e1v.T, axis=1, keepdims=True)
                yoff = lam_x[:, sl] * _dot(cb_, hprevb)
                gxw = g2 * xdw[:, sl]
                dacs_pieces.append(dy2 * yoff - gxw)
                dacsl_pieces.append(jnp.sum(gxw, axis=0, keepdims=True) + gam_x[:, sl] * jnp.sum(dhn * hprev, axis=0, keepdims=True))
                dyl = (dy2 * lam_x[:, sl]).astype(BF16)
                dc_g = dc_g + _dot_nt(dyl, hprevb)
                db_g = db_g + _dot_nt(xw[:, sl], dhnb)
                dh[j] = gam_x[:, sl] * dhn + _dot_tn(cb_, dyl)
            dcbb = dcbm.astype(BF16)
            dc_pieces.append(dc_g + _dot(dcbb, bb))
            db_pieces.append(db_g + _dot_tn(dcbb, cb_))

        dxd = jnp.concatenate(dx_pieces, axis=1)
        rowi = lax.broadcasted_iota(jnp.int32, (L, 1), 0)
        dacs_x = (jnp.concatenate(dacs_pieces, axis=1) + _dot_hi(colbuf[...], e1_ref[...])
                  + jnp.where(rowi == L - 1, jnp.concatenate(dacsl_pieces, axis=1), 0.0))
        upper = lax.broadcasted_iota(jnp.int32, (L, L), 0) <= lax.broadcasted_iota(jnp.int32, (L, L), 1)
        dadt_x = _dot_hi(upper, dacs_x, a_is_01=True)
        ddt_x = dxd * xs + dadt_x * a_x
        ddtr = _dot_nt_hi(ddt_x, e_ref[...]) * _sigmoid(pre)
        ddt_ref[...] = jnp.zeros_like(ddt_ref)
        ddt_ref[:, 0:SSM_H] = ddtr
        dalx =jnp.sum(dadt_x * dt_x, axis=0, keepdims=True) * a_x
        ddskx = jnp.sum(dy * xs, axis=0, keepdims=True)
        par_x = jnp.where(row8 == 1, dalx, 0.0) + jnp.where(row8 == 2, ddskx, 0.0)
        dpar_ref[...] += _dot_nt_hi(par_x, e_ref[...]) + jnp.where(row8 == 0, jnp.sum(ddtr, axis=0, keepdims=True), 0.0)

        dxs = dxd * dt_x + dsk_ref[...] * dy
        dact = jnp.concatenate([dxs] + db_pieces + dc_pieces, axis=1)
        du = dact * _dsilu(xc, sg)
        dcb_ref[...] += jnp.sum(du, axis=0, keepdims=True)
        zd[0:L, :] = du
        f1, f2, f3 = (zd[m:m + L, :] for m in (1, 2, 3))
        dxbc_ref[...] = cw_ref[3:4, :] * du + cw_ref[2:3, :] * f1 + cw_ref[1:2, :] * f2 + cw_ref[0:1, :] * f3
        dcw = jnp.zeros((8, CONV_C), F32)
        for k, shifted in enumerate((f3, f2, f1, du)):
            dcw = dcw + jnp.where(row8 == k, jnp.sum(shifted * u, axis=0, keepdims=True), 0.0)
        dcw_ref[...] += dcw
        zd[L:, :] = du[:8, :]

    return _call(
        body, name="ssd_bwd", grid=(NCH,),
        in_specs=[pl.BlockSpec((L, SSM_W), lambda i: (NCH - 1 - i, 1)), pl.BlockSpec((L, CONV_C), rev), pl.BlockSpec((L, CONV_C), rev),
                  pl.BlockSpec((L, SSM_W), rev), pl.BlockSpec((L, SSM_H), rev), pl.BlockSpec((L, SSM_W), rev),
                  pl.BlockSpec((1, 8, SSM_N, 128), lambda i: (NCH - 1 - i, 0, 0, 0)),
                  _const((4, CONV_C)), _const((1, SSM_H)), _const((1, SSM_W)), _const((1, SSM_W)), _const((1, SSM_W)),
                  _const((SSM_H, SSM_W)), _const((128, SSM_W))],
        out_specs=[pl.BlockSpec((L, CONV_C), rev), pl.BlockSpec((L, SSM_W), rev), pl.BlockSpec((L, 128), rev),
                   _const((8, CONV_C)), _const((1, CONV_C)), _const((1, SSM_W)), _const((8, SSM_H))],
        out_shape=[jax.ShapeDtypeStruct((T, CONV_C), F32), jax.ShapeDtypeStruct((T, SSM_W), F32), jax.ShapeDtypeStruct((T, 128), F32),
                   jax.ShapeDtypeStruct((8, CONV_C), F32), jax.ShapeDtypeStruct((1, CONV_C), F32), jax.ShapeDtypeStruct((1, SSM_W), F32),
                   jax.ShapeDtypeStruct((8, SSM_H), F32)],
        scratch=[pltpu.VMEM((8, SSM_N, 128), F32), pltpu.VMEM((L + 8, CONV_C), F32), pltpu.VMEM((L, 128), F32)],
        args=[dmix, xbc, xconv, z, dtr, y, hs, convw, dtb, alx, dskx, ssmn, e, e1], sem=("arbitrary",), duties=duties)


def _adam_math(w, g, m, v):
    m = ADAM_B1 * m + (1.0 - ADAM_B1) * g
    v = ADAM_B2 * v + (1.0 - ADAM_B2) * (g * g)
    m_hat = m / (1.0 - ADAM_B1 ** ADAM_STEP)
    v_hat = v / (1.0 - ADAM_B2 ** ADAM_STEP)
    delta = -ADAM_LR * (m_hat / (jnp.sqrt(v_hat) + ADAM_EPS) + ADAM_WD * w)
    return delta, m, v


PACK_W = CONV_C


def _adamw_small(ws, ms, vs, total, conv_g, after=()):
    k = len(ws)

    def body(*refs):
        ins, outs = refs[:3 * k + 2], refs[3 * k + 2 + len(after):]
        total_ref, conv_ref = ins[3 * k], ins[3 * k + 1]
        for a in range(k):
            w_ref, m_ref, v_ref = ins[3 * a:3 * a + 3]
            g = conv_ref[...] if a == k - 1 else total_ref[a:a + 1, 0:w_ref.shape[1]]
            delta, nm, nv = _adam_math(w_ref[...], g, m_ref[...], v_ref[...])
            for ref, val in zip(outs[4 * a:4 * a + 4], (g, delta, nm, nv)):
                ref[...] = val

    vm = pl.BlockSpec(memory_space=pltpu.VMEM)
    args = [x for a in range(k) for x in (ws[a], ms[a], vs[a])] + [total, conv_g]
    res = pl.pallas_call(
        body, name="adamw_small", in_specs=[vm] * len(args) + [ANY] * len(after), out_specs=[vm] * (4 * k),
        out_shape=[jax.ShapeDtypeStruct(ws[a].shape, F32) for a in range(k) for _ in range(4)],
    )(*args, *after)
    return [tuple(res[4 * a:4 * a + 4]) for a in range(k)]


COL_TILE = 512


def _adamw_sharded(ws, ms, vs, chip_sums, from_chips, other_chips, name, after=()):
    k = len(ws)
    rows, cols = ws[0].shape
    prow = chip_sums[0].shape[0]
    assert cols % COL_TILE == 0 and prow >= rows and all(a.shape == ws[0].shape for a in ws)

    def body(ids_ref, *refs):
        ins, outs = refs[:7 * k], refs[7 * k + len(after):]
        for a in range(k):
            w_ref, m_ref, v_ref, s_ref, r1_ref, r2_ref, r3_ref = ins[7 * a:7 * a + 7]
            g = s_ref[...]
            for r in (r1_ref, r2_ref, r3_ref):
                g = g + r[0].astype(F32)
            g = g[:rows]
            delta, nm, nv = _adam_math(w_ref[...], g, m_ref[...], v_ref[...])
            for ref, val in zip(outs[4 * a:4 * a + 4], (g, delta, nm, nv)):
                ref[...] = val

    spec = pl.BlockSpec((rows, COL_TILE), lambda i, ids: (0, i))
    part = lambda j: pl.BlockSpec((1, prow, COL_TILE), lambda i, ids: (ids[j], 0, i))
    one = [spec, spec, spec, pl.BlockSpec((prow, COL_TILE), lambda i, ids: (0, i)), part(0), part(1), part(2)]
    args = [x for a in range(k) for x in (ws[a], ms[a], vs[a], chip_sums[a], from_chips[a], from_chips[a], from_chips[a])]
    res = pl.pallas_call(
        body, name=name,
        grid_spec=pltpu.PrefetchScalarGridSpec(
            num_scalar_prefetch=1, grid=(cols // COL_TILE,),
            in_specs=one * k + [ANY] * len(after), out_specs=[spec] * (4 * k)),
        out_shape=[jax.ShapeDtypeStruct((rows, cols), F32)] * (4 * k),
        compiler_params=_params(("parallel",)),
    )(other_chips, *args, *after)
    return [tuple(res[4 * a:4 * a + 4]) for a in range(k)]


def _chip_sum(mines, recvs, name):
    k = len(mines)
    rows, cols = mines[0].shape[1:]

    def body(*refs):
        own_chip = pl.program_id(0) == 2 * lax.axis_index("x") + lax.axis_index("y")
        for a in range(k):
            a_ref, b_ref = refs[2 * a:2 * a + 2]
            s_ref, sb_ref = refs[2 * k + 2 * a:2 * k + 2 * a + 2]
            s = a_ref[0] + b_ref[0].astype(F32)
            sb_ref[0] = s.astype(BF16)

            @pl.when(own_chip)
            def _(s_ref=s_ref, s=s):
                s_ref[...] = s

    by_chip = pl.BlockSpec((1, rows, cols), lambda c: (c, 0, 0))
    res = pl.pallas_call(
        body, name=name, grid=(N_DEV // 2,),
        in_specs=[by_chip, by_chip] * k, out_specs=[_const((rows, cols)), by_chip] * k,
        out_shape=[jax.ShapeDtypeStruct((rows, cols), F32), jax.ShapeDtypeStruct((N_DEV // 2, rows, cols), BF16)] * k,
        compiler_params=_params(("arbitrary",)),
    )(*[x for pair in zip(mines, recvs) for x in pair])
    return [tuple(res[2 * a:2 * a + 2]) for a in range(k)]


def _all_reduce_small(v, after=()):
    rows = v.shape[0]

    def body(v_ref, *rest):
        out_ref, gath, send_sems, recv_sems = rest[len(after):]
        x, y, c = _place()
        me, sibling = (x, y, c), (x, y, 1 - c)
        chips = [(1 - x, y), (x, 1 - y), (1 - x, 1 - y)]

        def blk(px, py, pc):
            return gath.at[pl.ds((4 * px + 2 * py + pc) * rows, rows), :]

        def copy(k, block, to, src=None):
            return pltpu.make_async_remote_copy(src_ref=blk(*block) if src is None else src, dst_ref=blk(*block),
                                                send_sem=send_sems.at[k], recv_sem=recv_sems.at[k], device_id=to, device_id_type=MESH)

        gath[pl.ds((4 * x + 2 * y + c) * rows, rows), :] = v_ref[...]
        first = [copy(0, me, sibling, src=v_ref)] + [copy(1 + j, me, (*chip, c), src=v_ref) for j, chip in enumerate(chips)]
        for cp in first:
            cp.start()
        passed = [copy(4 + j, (*chip, c), sibling) for j, chip in enumerate(chips)]
        for j, chip in enumerate(chips):
            copy(1 + j, (*chip, c), me).wait_recv()
            passed[j].start()
        copy(0, sibling, me).wait_recv()
        for j, chip in enumerate(chips):
            copy(4 + j, (*chip, 1 - c), me).wait_recv()
        for cp in first + passed:
            cp.wait_send()
        acc = gath[0:rows, :]
        for d in range(1, N_DEV):
            acc = acc + gath[d * rows:(d + 1) * rows, :]
        out_ref[...] = acc

    vm = pl.BlockSpec(memory_space=pltpu.VMEM)
    return pl.pallas_call(
        body, name="all_reduce_small",
        in_specs=[vm] + [ANY] * len(after), out_specs=vm,
        out_shape=jax.ShapeDtypeStruct(v.shape, F32),
        scratch_shapes=[pltpu.VMEM((N_DEV * rows, v.shape[1]), F32), pltpu.SemaphoreType.DMA((7,)), pltpu.SemaphoreType.DMA((7,))],
    )(v, *after)


def _rope_tables(positions):
    inv_freq = ROPE_THETA ** (-jnp.arange(0, HD, 2, dtype=F32) / HD)
    ang = positions.reshape(T).astype(F32)[:, None] * inv_freq
    ang = jnp.concatenate([ang, ang, ang, ang], axis=-1)
    lo_half = (jnp.arange(128) % HD) < (HD // 2)
    return jnp.cos(ang), jnp.where(lo_half, -jnp.sin(ang), jnp.sin(ang))


def _selectors():
    lane = jnp.arange(QW)
    e = (lane[None, :] // HD == jnp.arange(SSM_H)[:, None]).astype(F32)
    e1 = ((lane[None, :] == HD * jnp.arange(128)[:, None]) & (jnp.arange(128)[:, None] < SSM_H)).astype(F32)
    return e, e1


WEIGHTS = ['ffn1_pre_norm', 'ffn1_w_gate', 'ffn1_w_up', 'ffn1_w_down', 'ffn1_post_norm', 'mix_pre_norm', 'w_in', 'conv_w', 'conv_b',
           'dt_bias', 'a_log', 'd_skip', 'ssm_norm', 'w_out', 'mix_post_norm', 'ffn2_pre_norm', 'ffn2_w_gate', 'ffn2_w_up',
           'ffn2_w_down', 'ffn2_post_norm']
COL_SHARDED = ['ffn1_w_gate', 'ffn1_w_up', 'ffn2_w_gate', 'ffn2_w_up', 'w_in']
ROW_SHARDED = ['ffn1_w_down', 'ffn2_w_down', 'w_out']
BIG = COL_SHARDED + ROW_SHARDED
FFN_BIG = COL_SHARDED[:4] + ROW_SHARDED[:2]
SMALL = ['ffn1_pre_norm', 'ffn1_post_norm', 'mix_pre_norm', 'conv_b', 'dt_bias', 'a_log', 'd_skip', 'ssm_norm', 'mix_post_norm',
         'ffn2_pre_norm', 'ffn2_post_norm']
FFN1 = ['ffn1_w_gate', 'ffn1_w_up', 'ffn1_w_down']
FFN2 = ['ffn2_w_gate', 'ffn2_w_up', 'ffn2_w_down']


def _wire_block(name, a):
    if name in FFN_BIG:
        return jnp.pad(a.astype(BF16), ((0, FSH - FSR), (0, 0)))
    if name == "w_in":
        return jnp.pad(a.astype(BF16), ((0, ISW - ISR), (0, 0)))
    return a if name == "conv_w" else a.astype(BF16)


def _whole_from_gathered(name, a):
    if name == "conv_w":
        return jnp.transpose(a, (1, 0, 2)).reshape(a.shape[1], -1)
    return a.reshape(-1, D)


def _step(x, positions, target, small, blocks=None, whole=None):
    dist = blocks is not None
    core = "mesh" if dist else 0
    w = dict(small)
    if whole:
        w.update(whole)

    def gather(names):
        return [_gather_duty([_wire_block(n, blocks[n]) for n in names])] if dist else []

    def put(names, results):
        if dist:
            for n, r in zip(names, results[0]):
                w[n] = _whole_from_gathered(n, r)

    g, sums, red = {}, {}, {}

    def swap(names):
        return [_swap_duty([g[n][1] for n in names])] if dist else []

    def chip_sums(names, from_sibling):
        if dist:
            res = _chip_sum([g[n][0] for n in names], list(from_sibling), "chip_sum_" + names[0])
            sums.update(zip(names, res))

    def exchange(names):
        return [_exchange_duty([sums[n][1] for n in names])] if dist else []

    def reduced(names, from_chips):
        if dist:
            for n, recv in zip(names, from_chips):
                red[n] = (sums[n][0], recv)

    cos, sin_s = _rope_tables(positions)
    e, e1 = _selectors()
    bias = _attn_bias()
    alx = jnp.repeat(w["a_log"], HD, axis=1)
    dskx = jnp.repeat(w["d_skip"], HD, axis=1)

    if dist:
        put(FFN1[:2], _comm_only(gather(FFN1[:2]), "gather_ffn1"))
    (n1, a1, b1, hm1), got = _ffn_up(x, w["ffn1_pre_norm"], w["ffn1_w_gate"], w["ffn1_w_up"], "ffn1_up", gather(FFN1[2:]))
    put(FFN1[2:], got)
    (x1, h1), got = _ffn_down(x, hm1, w["ffn1_w_down"], w["ffn1_post_norm"], "ffn1_down", gather(["w_in", "conv_w"]))
    put(["w_in", "conv_w"], got)
    (n2, q, kx, vx, xbc, z, dtr), got = _inproj_fwd(x1, w["mix_pre_norm"], w["w_in"], cos, sin_s, gather(["w_out"]))
    put(["w_out"], got)
    (attn, lse), got = _attn_fwd(q, kx, vx, bias, gather(FFN2[:2]))
    put(FFN2[:2], got)
    (yn, y, hs, xconv), got = _ssd_fwd(xbc, z, dtr, w["conv_w"], w["conv_b"], w["dt_bias"], alx, dskx, w["ssm_norm"], e, gather(FFN2[2:]))
    put(FFN2[2:], got)
    x2, h2 = _outproj_fwd(x1, attn, yn, w["w_out"], w["mix_post_norm"])
    (dx3, n3, a3, b3, hm3, h3, ss), _ = _ffn_fwd(x2, w["ffn2_pre_norm"], w["ffn2_w_gate"], w["ffn2_w_up"], w["ffn2_w_down"],
                                                 w["ffn2_post_norm"], "ffn2_fwd", target=target)

    (dx2, da3, db3, dh3, g["ffn2_pre_norm"], g["ffn2_post_norm"]), _ = _ffn_bwd(
        dx3, x2, a3, b3, h3, w["ffn2_pre_norm"], w["ffn2_post_norm"], w["ffn2_w_gate"], w["ffn2_w_up"], w["ffn2_w_down"], "ffn2_bwd")
    g["ffn2_w_down"] = _matmul_tn(hm3, dh3, "ffn2_dwd", core)[0]
    g["ffn2_w_gate"] = _matmul_tn(da3, n3, "ffn2_dwg", core)[0]
    g["ffn2_w_up"] = _matmul_tn(db3, n3, "ffn2_dwu", core)[0]

    (dh2, dmix, g["mix_post_norm"]), got = _outproj_bwd(dx2, h2, w["mix_post_norm"], w["w_out"], swap(FFN2))
    chip_sums(FFN2, got[0] if dist else None)
    g["w_out"] = _dwout(attn, yn, dh2, core)
    (dq, dkx, dvx), got = _attn_bwd(q, kx, vx, attn, dmix, lse, bias, exchange(FFN2) + swap(["w_out"]))
    if dist:
        reduced(FFN2, got[0])
        chip_sums(["w_out"], got[1])
    (dxbc, dz, ddt, dcw, g["conv_b"], g["ssm_norm"], dpar), got = _ssd_bwd(
        dmix, xbc, xconv, z, dtr, y, hs, w["conv_w"], w["dt_bias"], alx, dskx, w["ssm_norm"], e, e1, exchange(["w_out"]))
    reduced(["w_out"], got[0] if dist else None)
    g["conv_w"] = dcw[0:4]
    g["dt_bias"], g["a_log"], g["d_skip"] = dpar[0:1], dpar[1:2], dpar[2:3]
    dx1, dproj, g["mix_pre_norm"] = _inproj_bwd(dx2, dq, dkx, dvx, dxbc, dz, ddt, x1, w["mix_pre_norm"], w["w_in"], cos, sin_s)
    g["w_in"] = _matmul_tn(dproj, n2, "dwin", core)[0]

    started = {}

    def start(n):
        started[n] = _exchange_start(sums[n][1], "start_exchange_" + n, FSR if n in FFN_BIG else None)
        return [started[n]["token"]]

    after = []
    if dist:
        chip_sums(["w_in"], _comm_only(swap(["w_in"]), "swap_w_in")[0])
        after = start("w_in")
    (dx0, da1, db1, dh1, g["ffn1_pre_norm"], g["ffn1_post_norm"]), _ = _ffn_bwd(
        dx1, x, a1, b1, h1, w["ffn1_pre_norm"], w["ffn1_post_norm"], w["ffn1_w_gate"], w["ffn1_w_up"], w["ffn1_w_down"], "ffn1_bwd",
        after)
    total = None
    if dist:
        widen = lambda a: jnp.pad(a, ((0, 0), (0, PACK_W - a.shape[1])))
        pack = jnp.concatenate([widen(g[n]) for n in SMALL] + [g["conv_w"], widen(ss[:, 0:1])])
        assert pack.shape[0] % 8 == 0
        total = _all_reduce_small(pack, after)
        after = [total]
    g["ffn1_w_down"], _ = _matmul_tn(hm1, dh1, "ffn1_dwd", core, after=after)
    g["ffn1_w_gate"], got = _matmul_tn(da1, n1, "ffn1_dwg", core, duties=swap(["ffn1_w_down"]))
    if dist:
        chip_sums(["ffn1_w_down"], got[0])
        after = start("ffn1_w_down")
    g["ffn1_w_up"], got = _matmul_tn(db1, n1, "ffn1_dwu", core, after=after, duties=swap(["ffn1_w_gate"]))
    if dist:
        chip_sums(["ffn1_w_gate"], got[0])
        after = start("ffn1_w_gate")
        chip_sums(["ffn1_w_up"], _comm_only(swap(["ffn1_w_up"]), "swap_ffn1_w_up", after=after)[0])
        start("ffn1_w_up")
    return ss, dx0, g, red, {n: (sums[n][0], started[n]) for n in started}, total


def kernel(x, positions, ffn1_pre_norm, ffn1_w_gate, ffn1_w_up, ffn1_w_down, ffn1_post_norm, mix_pre_norm, w_in, conv_w, conv_b, dt_bias, a_log, d_skip, ssm_norm, w_out, mix_post_norm, ffn2_pre_norm, ffn2_w_gate, ffn2_w_up, ffn2_w_down, ffn2_post_norm, loss_target, m_ffn1_pre_norm, m_ffn1_w_gate, m_ffn1_w_up, m_ffn1_w_down, m_ffn1_post_norm, m_mix_pre_norm, m_w_in, m_conv_w, m_conv_b, m_dt_bias, m_a_log, m_d_skip, m_ssm_norm, m_w_out, m_mix_post_norm, m_ffn2_pre_norm, m_ffn2_w_gate, m_ffn2_w_up, m_ffn2_w_down, m_ffn2_post_norm, v_ffn1_pre_norm, v_ffn1_w_gate, v_ffn1_w_up, v_ffn1_w_down, v_ffn1_post_norm, v_mix_pre_norm, v_w_in, v_conv_w, v_conv_b, v_dt_bias, v_a_log, v_d_skip, v_ssm_norm, v_w_out, v_mix_post_norm, v_ffn2_pre_norm, v_ffn2_w_gate, v_ffn2_w_up, v_ffn2_w_down, v_ffn2_post_norm):
    given = dict(locals())
    drop = lambda n, a: a if n in SMALL else (a[0].T if n in COL_SHARDED else a[0])
    w = {n: drop(n, given[n]) for n in WEIGHTS}
    m = {n: drop(n, given["m_" + n]) for n in WEIGHTS}
    v = {n: drop(n, given["v_" + n]) for n in WEIGHTS}
    cx, cy, cc = _place()
    others = [2 * (1 - cx) + cy, 2 * cx + (1 - cy), 2 * (1 - cx) + (1 - cy)]

    _, grad_x, g, red, pending, total = _step(x[0], positions, loss_target[0], {n: w[n] for n in SMALL},
                                              blocks={n: w[n] for n in BIG + ["conv_w"]})
    chip_ids = jnp.stack(others).astype(jnp.int32)
    out_g, out_d, out_m, out_v = {}, {}, {}, {}

    def update(names, sums, recvs, label, after=()):
        res = _adamw_sharded([w[n] for n in names], [m[n] for n in names], [v[n] for n in names], sums, recvs, chip_ids,
                             "adamw_" + label, after)
        for n, (gn, dn, mn, vn) in zip(names, res):
            out_g[n], out_d[n], out_m[n], out_v[n] = gn, dn, mn, vn

    last_start = [pending["ffn1_w_up"][1]["token"]]
    update(FFN2, [red[n][0] for n in FFN2], [red[n][1] for n in FFN2], "ffn2", last_start)
    update(["w_out"], [red["w_out"][0]], [red["w_out"][1]], "w_out", last_start)

    n_small = len(SMALL)
    conv_g = lax.dynamic_slice_in_dim(total[n_small:n_small + 4], (4 * cx + 2 * cy + cc) * (CONV_C // N_DEV), CONV_C // N_DEV, axis=1)
    loss = 0.5 * total[n_small + 4, 0] / D
    names = SMALL + ["conv_w"]
    res = _adamw_small([w[n] for n in names], [m[n] for n in names], [v[n] for n in names], total, conv_g, last_start)
    for n, (gn, dn, mn, vn) in zip(names, res):
        out_g[n], out_d[n], out_m[n], out_v[n] = gn, dn, mn, vn
    done = [out_v[n] for n in FFN2 + ["w_out", "conv_w"]]
    update(["w_in"], [pending["w_in"][0]], [_exchange_wait(pending["w_in"][1], done, "wait_exchange_w_in")], "w_in")
    done = [out_v["w_in"]]
    update(FFN1, [pending[n][0] for n in FFN1], [_exchange_wait(pending[n][1], done, "wait_exchange_" + n) for n in FFN1], "ffn1")

    outs = [loss, grad_x[None]]
    for d in (out_g, out_d, out_m, out_v):
        outs += [d[n] if n in SMALL else (d[n].T[None] if n in COL_SHARDED else d[n][None]) for n in WEIGHTS]
    return tuple(outs)
```

```python
import functools

import jax
import jax.numpy as jnp
from jax import lax
from jax.experimental import pallas as pl
from jax.experimental.pallas import tpu as pltpu

F32 = jnp.float32
BF16 = jnp.bfloat16
MESH = pl.DeviceIdType.MESH

N_DEV = 8
T = 2048
D = 1024
FF = 2816
FSR = FF // N_DEV
FSH = 384
FFP = N_DEV * FSH
HD = 64
NQ = 16
NKV = 4
QW = NQ * HD
KVW = NKV * HD
SSM_W = 1024
SSM_H = 16
SSM_N = 128
CONV_C = SSM_W + 2 * 2 * SSM_N
IN_COLS = 4112
INP = 4224
ISR = IN_COLS // N_DEV
ISW = 528
ISG = 640
L = 128
NCH = T // L
AB = 256
NAB = T // AB
EPS = 1e-6
NEG = -1e30
ROPE_THETA = 10000.0
DILATIONS = ((128, 1), (512, 4), (2048, 16))

ADAM_LR = 0.001
ADAM_B1 = 0.9
ADAM_B2 = 0.999
ADAM_EPS = 1e-08
ADAM_WD = 0.01
ADAM_STEP = 10

VMEM_LIMIT = 58 * 1024 * 1024


def _params(sem, vmem=VMEM_LIMIT):
    return pltpu.CompilerParams(dimension_semantics=sem, vmem_limit_bytes=vmem)


def _dot(a, b):
    return jnp.dot(a, b, preferred_element_type=F32)


def _dot_nt(a, b):
    return lax.dot_general(a, b, (((1,), (1,)), ((), ())), preferred_element_type=F32)


def _dot_tn(a, b):
    return lax.dot_general(a, b, (((0,), (0,)), ((), ())), preferred_element_type=F32)


def _split3(x):
    hi = x.astype(BF16)
    r1 = x - hi.astype(F32)
    mid = r1.astype(BF16)
    lo = (r1 - mid.astype(F32)).astype(BF16)
    return hi, mid, lo


def _dot_hi(a, b, a_is_01=False):
    if a_is_01:
        sel = a.astype(BF16)
        return sum(_dot(sel, p) for p in _split3(b))
    sel = b.astype(BF16)
    return sum(_dot(p, sel) for p in _split3(a))


def _dot_nt_hi(a, b):
    sel = b.astype(BF16)
    return sum(_dot_nt(p, sel) for p in _split3(a))


def _rs(x):
    return lax.rsqrt(jnp.mean(x * x, axis=-1, keepdims=True) + EPS)


def _sigmoid(x):
    return jax.nn.sigmoid(x)


def _dsilu(x, s):
    return s * (1.0 + x * (1.0 - s))


def _resident(shape):
    nd = len(shape)
    return pl.BlockSpec(shape, lambda *_: (0,) * nd, pipeline_mode=pl.Buffered(1))


def _const(shape):
    nd = len(shape)
    return pl.BlockSpec(shape, lambda *_: (0,) * nd)


def _rows(tm, cols):
    return pl.BlockSpec((tm, cols), lambda i: (i, 0))


ANY = pl.BlockSpec(memory_space=pl.ANY)


def _place():
    return lax.axis_index("x"), lax.axis_index("y"), lax.axis_index("c")


def _gather_duty(arrays):
    n = len(arrays)
    results = [jax.ShapeDtypeStruct((N_DEV,) + a.shape, a.dtype) for a in arrays]

    def make(ins, outs, send_sems, recv_sems, local_sems):
        x, y, c = _place()
        me, sibling = (x, y, c), (x, y, 1 - c)
        chips = [(1 - x, y), (x, 1 - y), (1 - x, 1 - y)]

        def place_of(a, px, py, pc):
            return outs[a].at[4 * px + 2 * py + pc]

        def copy(a, k, block, to, src=None):
            dst = place_of(a, *block)
            return pltpu.make_async_remote_copy(src_ref=dst if src is None else src, dst_ref=dst,
                                                send_sem=send_sems.at[7 * a + k], recv_sem=recv_sems.at[7 * a + k],
                                                device_id=to, device_id_type=MESH)

        def own(a):
            return pltpu.make_async_copy(ins[a], place_of(a, *me), local_sems.at[a])

        def first(a):
            return [copy(a, 0, me, sibling, src=ins[a])] + [copy(a, 1 + j, me, (*chip, c), src=ins[a]) for j, chip in enumerate(chips)]

        def start():
            for a in range(n):
                own(a).start()
            for a in range(n):
                for cp in first(a):
                    cp.start()

        def finish():
            for j, chip in enumerate(chips):
                for a in range(n):
                    copy(a, 1 + j, (*chip, c), me).wait_recv()
                    copy(a, 4 + j, (*chip, c), sibling).start()
            for a in range(n):
                copy(a, 0, sibling, me).wait_recv()
                for j, chip in enumerate(chips):
                    copy(a, 4 + j, (*chip, 1 - c), me).wait_recv()
            for a in range(n):
                for cp in first(a) + [copy(a, 4 + j, (*chip, c), sibling) for j, chip in enumerate(chips)]:
                    cp.wait_send()
                own(a).wait()

        return start, finish

    return dict(operands=list(arrays), results=results, sems=(7 * n, 7 * n, n), make=make)


def _swap_duty(arrays):
    n = len(arrays)
    half = N_DEV // 2
    results = [jax.ShapeDtypeStruct(a.shape, a.dtype) for a in arrays]

    def make(ins, outs, send_sems, recv_sems):
        x, y, c = _place()

        def copies():
            return [pltpu.make_async_remote_copy(src_ref=ins[a].at[k], dst_ref=outs[a].at[k],
                                                 send_sem=send_sems.at[half * a + k], recv_sem=recv_sems.at[half * a + k],
                                                 device_id=(x, y, 1 - c), device_id_type=MESH)
                    for a in range(n) for k in range(half)]

        def start():
            for cp in copies():
                cp.start()

        def finish():
            for cp in copies():
                cp.wait()

        return start, finish

    return dict(operands=list(arrays), results=results, sems=(half * n, half * n), make=make)


def _exchange_duty(arrays):
    n = len(arrays)
    results = [jax.ShapeDtypeStruct(a.shape, a.dtype) for a in arrays]

    def make(ins, outs, send_sems, recv_sems):
        x, y, c = _place()
        chips = [(1 - x, y), (x, 1 - y), (1 - x, 1 - y)]
        my_chip = 2 * x + y

        def sends():
            return [pltpu.make_async_remote_copy(src_ref=ins[a].at[2 * px + py], dst_ref=outs[a].at[my_chip],
                                                 send_sem=send_sems.at[3 * a + j], recv_sem=recv_sems.at[3 * a + j],
                                                 device_id=(px, py, c), device_id_type=MESH)
                    for a in range(n) for j, (px, py) in enumerate(chips)]

        def start():
            for cp in sends():
                cp.start()

        def finish():
            for a in range(n):
                for j, (px, py) in enumerate(chips):
                    pltpu.make_async_remote_copy(src_ref=ins[a].at[my_chip], dst_ref=outs[a].at[2 * px + py],
                                                 send_sem=send_sems.at[3 * a + j], recv_sem=recv_sems.at[3 * a + j],
                                                 device_id=(px, py, c), device_id_type=MESH).wait_recv()
            for cp in sends():
                cp.wait_send()

        return start, finish

    return dict(operands=list(arrays), results=results, sems=(3 * n, 3 * n), make=make)


def _call(body, *, name, grid, in_specs, out_specs, out_shape, args, sem, scratch=(), duties=(), after=()):
    n_in, n_out, n_scr = len(in_specs), len(out_specs), len(scratch)
    sem_shapes = [pltpu.SemaphoreType.DMA((k,)) for d in duties for k in d["sems"]]

    def full(*refs):
        pos = [0]

        def take(k):
            pos[0] += k
            return refs[pos[0] - k:pos[0]]

        ins = take(n_in)
        d_ins = [take(len(d["operands"])) for d in duties]
        take(len(after))
        outs = take(n_out)
        d_outs = [take(len(d["results"])) for d in duties]
        scr = take(n_scr)
        d_sems = [take(len(d["sems"])) for d in duties]
        hooks = [d["make"](di, do, *ds) for d, di, do, ds in zip(duties, d_ins, d_outs, d_sems)]
        if grid and hooks:
            ids = [pl.program_id(k) for k in range(len(grid))]
            first = functools.reduce(jnp.logical_and, [i == 0 for i in ids])
            last = functools.reduce(jnp.logical_and, [i == g - 1 for i, g in zip(ids, grid)])

            @pl.when(first)
            def _():
                for start, _ in hooks:
                    start()

            body(*ins, *outs, *scr)

            @pl.when(last)
            def _():
                for _, finish in hooks:
                    finish()
        else:
            for start, _ in hooks:
                start()
            body(*ins, *outs, *scr)
            for _, finish in hooks:
                finish()

    d_args = [a for d in duties for a in d["operands"]]
    d_res = [r for d in duties for r in d["results"]]
    kwargs = dict(grid=grid) if grid else {}
    res = pl.pallas_call(
        full, name=name, in_specs=list(in_specs) + [ANY] * (len(d_args) + len(after)), out_specs=list(out_specs) + [ANY] * len(d_res),
        out_shape=list(out_shape) + d_res, scratch_shapes=list(scratch) + sem_shapes,
        compiler_params=_params(sem) if grid else None, **kwargs,
    )(*args, *d_args, *after)
    own, rest = list(res[:n_out]), list(res[n_out:])
    by_duty = []
    for d in duties:
        by_duty.append(rest[:len(d["results"])])
        rest = rest[len(d["results"]):]
    return own, by_duty


def _comm_only(duties, name, after=()):
    return _call(lambda: None, name=name, grid=None, in_specs=[], out_specs=[], out_shape=[], args=[], sem=None, duties=duties,
                 after=after)[1]


HBM = pl.BlockSpec(memory_space=pltpu.HBM)
SEMS = pl.BlockSpec(memory_space=pltpu.SEMAPHORE)
SIDE_EFFECT = pltpu.SideEffectType.DATAFLOW_SIDE_EFFECTING
N_OTHER_CHIPS = 3


def _chip_copies(src_ref, land_ref, sems, rows):
    x, y, c = _place()
    chips = [(1 - x, y), (x, 1 - y), (1 - x, 1 - y)]
    part = (lambda ref: ref) if rows is None else (lambda ref: ref.at[pl.ds(0, rows)])
    return [pltpu.make_async_remote_copy(src_ref=part(src_ref.at[2 * px + py]), dst_ref=part(land_ref.at[2 * x + y]),
                                         send_sem=sems[j], recv_sem=sems[N_OTHER_CHIPS + j], device_id=(px, py, c), device_id_type=MESH)
            for j, (px, py) in enumerate(chips)]


def _exchange_start(pb, name, rows=None):
    n_sem = 2 * N_OTHER_CHIPS

    def body(pb_ref, land_ref, *rest):
        for cp in _chip_copies(pb_ref, land_ref, rest[:n_sem], rows):
            cp.start()
        token = rest[n_sem + 2]
        token[...] = jnp.zeros_like(token)

    res = pl.pallas_call(
        body, name=name,
        out_shape=(pltpu.SemaphoreType.DMA(()),) * n_sem + (pltpu.HBM(pb.shape, pb.dtype), pltpu.HBM(pb.shape, pb.dtype),
                                                              jax.ShapeDtypeStruct((8, 128), F32)),
        in_specs=(HBM, HBM), out_specs=(SEMS,) * n_sem + (HBM, HBM, pl.BlockSpec(memory_space=pltpu.VMEM)),
        input_output_aliases={0: n_sem, 1: n_sem + 1},
        compiler_params=pltpu.CompilerParams(has_side_effects=SIDE_EFFECT),
    )(pltpu.with_memory_space_constraint(pb, pltpu.HBM), pltpu.with_memory_space_constraint(lax.empty(pb.shape, pb.dtype), pltpu.HBM))
    return dict(sems=res[:n_sem], src=res[n_sem], land=res[n_sem + 1], token=res[n_sem + 2], rows=rows)


def _exchange_wait(started, after, name):
    n_sem = 2 * N_OTHER_CHIPS

    def body(pb_ref, land_ref, *rest):
        for cp in _chip_copies(pb_ref, land_ref, rest[:n_sem], started["rows"]):
            cp.wait_send()
            cp.wait_recv()

    src, land = started["src"], started["land"]
    return pl.pallas_call(
        body, name=name, out_shape=(pltpu.HBM(src.shape, src.dtype), pltpu.HBM(land.shape, land.dtype)),
        in_specs=(HBM, HBM) + (SEMS,) * n_sem + (ANY,) * len(after), out_specs=(HBM, HBM), input_output_aliases={0: 0, 1: 1},
        compiler_params=pltpu.CompilerParams(has_side_effects=SIDE_EFFECT),
    )(src, land, *started["sems"], *after)[1]


def _ffn_fwd(x, gpre, wg, wu, wd, gpost, name, duties=(), target=None):
    tm = 256
    n_in = 6 if target is None else 7

    def body(*refs):
        x_ref, gpre_ref, wg_ref, wu_ref, wd_ref, gpost_ref = refs[:6]
        xo_ref, n_ref, a_ref, b_ref, hm_ref, h_ref = refs[n_in:n_in + 6]
        xv = x_ref[...]
        n = (xv * _rs(xv) * gpre_ref[...]).astype(BF16)
        a = _dot_nt(n, wg_ref[...])
        b = _dot_nt(n, wu_ref[...])
        hm = (a * _sigmoid(a) * b).astype(BF16)
        h = _dot(hm, wd_ref[...])
        xo = xv + 0.5 * (h * _rs(h) * gpost_ref[...])
        n_ref[...] = n
        a_ref[...] = a.astype(BF16)
        b_ref[...] = b.astype(BF16)
        hm_ref[...] = hm
        h_ref[...] = h
        if target is None:
            xo_ref[...] = xo
        else:
            ss_ref = refs[n_in + 6]

            @pl.when(pl.program_id(0) == 0)
            def _():
                ss_ref[...] = jnp.zeros_like(ss_ref)

            err = xo - refs[6][...]
            xo_ref[...] = err * (1.0 / D)
            ss_ref[...] += jnp.sum(jnp.sum(err * err, axis=1, keepdims=True), axis=0, keepdims=True)

    loss_in = [] if target is None else [_rows(tm, D)]
    loss_out = [] if target is None else [_const((1, 128))]
    loss_shape = [] if target is None else [jax.ShapeDtypeStruct((1, 128), F32)]
    return _call(
        body, name=name, grid=(T // tm,),
        in_specs=[_rows(tm, D), _const((1, D)), _resident((FFP, D)), _resident((FFP, D)), _resident((FFP, D)), _const((1, D))] + loss_in,
        out_specs=[_rows(tm, D), _rows(tm, D), _rows(tm, FFP), _rows(tm, FFP), _rows(tm, FFP), _rows(tm, D)] + loss_out,
        out_shape=[jax.ShapeDtypeStruct((T, D), F32), jax.ShapeDtypeStruct((T, D), BF16), jax.ShapeDtypeStruct((T, FFP), BF16),
                   jax.ShapeDtypeStruct((T, FFP), BF16), jax.ShapeDtypeStruct((T, FFP), BF16), jax.ShapeDtypeStruct((T, D), F32)]
        + loss_shape,
        args=[x, gpre, wg, wu, wd, gpost] + ([] if target is None else [target]), sem=("arbitrary",), duties=duties)


def _ffn_bwd(dxo, x, a, b, h, gpre, gpost, wg, wu, wd, name, after=()):
    tm = 256

    def body(dxo_ref, x_ref, a_ref, b_ref, h_ref, gpre_ref, gpost_ref, wg_ref, wu_ref, wd_ref,
             dx_ref, da_ref, db_ref, dh_ref, dgpre_ref, dgpost_ref):
        @pl.when(pl.program_id(0) == 0)
        def _():
            dgpre_ref[...] = jnp.zeros_like(dgpre_ref)
            dgpost_ref[...] = jnp.zeros_like(dgpost_ref)

        dy = dxo_ref[...]
        h = h_ref[...]
        hn = h * _rs(h)
        r2 = _rs(h)
        dgpost_ref[...] += jnp.sum(0.5 * dy * hn, axis=0, keepdims=True)
        gdy = 0.5 * dy * gpost_ref[...]
        dh = r2 * (gdy - hn * jnp.mean(gdy * hn, axis=-1, keepdims=True))
        dhb = dh.astype(BF16)
        dh_ref[...] = dhb
        dhm = _dot_nt(dhb, wd_ref[...])
        av = a_ref[...].astype(F32)
        bv = b_ref[...].astype(F32)
        sg = _sigmoid(av)
        db = (dhm * (av * sg)).astype(BF16)
        da = (dhm * bv * _dsilu(av, sg)).astype(BF16)
        da_ref[...] = da
        db_ref[...] = db
        dn = _dot(da, wg_ref[...]) + _dot(db, wu_ref[...])
        xv = x_ref[...]
        r = _rs(xv)
        xn = xv * r
        dgpre_ref[...] += jnp.sum(dn * xn, axis=0, keepdims=True)
        gdn = dn * gpre_ref[...]
        dx_ref[...] = dy + r * (gdn - xn * jnp.mean(gdn * xn, axis=-1, keepdims=True))

    return _call(
        body, name=name, grid=(T // tm,),
        in_specs=[_rows(tm, D), _rows(tm, D), _rows(tm, FFP), _rows(tm, FFP), _rows(tm, D), _const((1, D)), _const((1, D)),
                  _resident((FFP, D)), _resident((FFP, D)), _resident((FFP, D))],
        out_specs=[_rows(tm, D), _rows(tm, FFP), _rows(tm, FFP), _rows(tm, D), _const((1, D)), _const((1, D))],
        out_shape=[jax.ShapeDtypeStruct((T, D), F32), jax.ShapeDtypeStruct((T, FFP), BF16), jax.ShapeDtypeStruct((T, FFP), BF16),
                   jax.ShapeDtypeStruct((T, D), BF16), jax.ShapeDtypeStruct((1, D), F32), jax.ShapeDtypeStruct((1, D), F32)],
        args=[dxo, x, a, b, h, gpre, gpost, wg, wu, wd], sem=("arbitrary",), after=after)


def _core_index(core):
    return lax.axis_index("c") if core == "mesh" else core


def _by_core(res, o_ref, ob_ref, core):
    r = res.shape[0] // 2
    c = jnp.asarray(_core_index(core))

    @pl.when(c == 0)
    def _():
        o_ref[0] = res[:r]
        ob_ref[0] = res[r:].astype(BF16)

    @pl.when(c == 1)
    def _():
        o_ref[0] = res[r:]
        ob_ref[0] = res[:r].astype(BF16)


def _matmul_tn(a, b, name, core, after=(), duties=()):
    k, m = a.shape
    n = b.shape[1]
    r = m // N_DEV
    assert m == N_DEV * r and r % 128 == 0

    def body(a_ref, b_ref, o_ref, ob_ref):
        _by_core(_dot_tn(a_ref[...], b_ref[...]), o_ref, ob_ref, core)

    spec = pl.BlockSpec((1, r, n), lambda i: (i, 0, 0))
    return _call(body, name=name, grid=(N_DEV // 2,), in_specs=[pl.BlockSpec((k, 2 * r), lambda i: (0, i)), _resident((k, n))],
                 out_specs=[spec, spec],
                 out_shape=[jax.ShapeDtypeStruct((N_DEV // 2, r, n), F32), jax.ShapeDtypeStruct((N_DEV // 2, r, n), BF16)],
                 args=[a, b], sem=("arbitrary",), duties=duties, after=after)


def _dwout(attn, yn, dh2, core):
    rs = (QW + SSM_W) // N_DEV
    chips = N_DEV // 2

    def body(at_ref, yn_ref, dh_ref, o_ref, ob_ref):
        i = pl.program_id(0)

        @pl.when(i < chips // 2)
        def _():
            _by_core(_dot_tn(at_ref[...], dh_ref[...]), o_ref, ob_ref, core)

        @pl.when(i >= chips // 2)
        def _():
            _by_core(_dot_tn(yn_ref[...], dh_ref[...]), o_ref, ob_ref, core)

    spec = pl.BlockSpec((1, rs, D), lambda i: (i, 0, 0))
    return pl.pallas_call(
        body, name="dwout", grid=(chips,),
        in_specs=[pl.BlockSpec((T, 2 * rs), lambda i: (0, jnp.minimum(i, chips // 2 - 1))),
                  pl.BlockSpec((T, 2 * rs), lambda i: (0, jnp.maximum(i - chips // 2, 0))), _resident((T, D))],
        out_specs=[spec, spec],
        out_shape=[jax.ShapeDtypeStruct((chips, rs, D), F32), jax.ShapeDtypeStruct((chips, rs, D), BF16)],
        compiler_params=_params(("arbitrary",)),
    )(attn, yn, dh2)


def _rope_swap(t, lo_half):
    return jnp.where(lo_half, pltpu.roll(t, 96, 1), pltpu.roll(t, 32, 1))


def _to_query_heads(t):
    return jnp.concatenate([t[:, HD * (h // G_PER):HD * (h // G_PER + 1)] for h in range(NQ)], axis=1)


def _from_query_heads(t):
    def kv_sum(kv):
        parts = [t[:, HD * (G_PER * kv + g):HD * (G_PER * kv + g + 1)] for g in range(G_PER)]
        return (parts[0] + parts[1]) + (parts[2] + parts[3])

    return jnp.concatenate([kv_sum(kv) for kv in range(NKV)], axis=1)


def _inproj_fwd(x1, gpre, win, cos, sin_s, duties=()):
    tm = 256

    def body(x_ref, g_ref, w_ref, cos_ref, sin_ref, n_ref, q_ref, kx_ref, vx_ref, xbc_ref, z_ref, dt_ref):
        xv = x_ref[...]
        n = (xv * _rs(xv) * g_ref[...]).astype(BF16)
        n_ref[...] = n
        by_dev = _dot_nt(n, w_ref[...])
        proj = jnp.concatenate([by_dev[:, ISW * d:ISW * d + ISR] for d in range(N_DEV)], axis=1)
        cs = cos_ref[...]
        sn = sin_ref[...]
        lo_half = (lax.broadcasted_iota(jnp.int32, (1, 128), 1) % HD) < (HD // 2)

        def rope(t):
            return t * cs + _rope_swap(t, lo_half) * sn

        for j in range(QW // 128):
            t = proj[:, 128 * j:128 * j + 128]
            q_ref[:, 128 * j:128 * j + 128] = (rope(t) * (HD ** -0.5)).astype(BF16)
        k = jnp.concatenate([rope(proj[:, QW + 128 * j:QW + 128 * j + 128]) for j in range(KVW // 128)], axis=1)
        v = proj[:, QW + KVW:QW + 2 * KVW]
        kx_ref[...] = _to_query_heads(k).astype(BF16)
        vx_ref[...] = _to_query_heads(v).astype(BF16)
        c0 = QW + 2 * KVW
        xbc_ref[...] = proj[:, c0:c0 + CONV_C]
        z_ref[...] = proj[:, c0 + CONV_C:c0 + CONV_C + SSM_W]
        dt_ref[...] = proj[:, c0 + CONV_C + SSM_W:IN_COLS]

    return _call(
        body, name="inproj_fwd", grid=(T // tm,),
        in_specs=[_rows(tm, D), _const((1, D)), _resident((INP, D)), _rows(tm, 128), _rows(tm, 128)],
        out_specs=[_rows(tm, D), _rows(tm, QW), _rows(tm, QW), _rows(tm, QW), _rows(tm, CONV_C), _rows(tm, SSM_W), _rows(tm, SSM_H)],
        out_shape=[jax.ShapeDtypeStruct((T, D), BF16), jax.ShapeDtypeStruct((T, QW), BF16), jax.ShapeDtypeStruct((T, QW), BF16),
                   jax.ShapeDtypeStruct((T, QW), BF16), jax.ShapeDtypeStruct((T, CONV_C), F32), jax.ShapeDtypeStruct((T, SSM_W), F32),
                   jax.ShapeDtypeStruct((T, SSM_H), F32)],
        args=[x1, gpre, win, cos, sin_s], sem=("arbitrary",), duties=duties)


def _inproj_bwd(dres, dq, dkx, dvx, dxbc, dz, ddt, x1, gpre, win, cos, sin_s):
    tm = 256

    def body(dres_ref, dq_ref, dkx_ref, dvx_ref, dxbc_ref, dz_ref, ddt_ref, x_ref, g_ref, w_ref, cos_ref, sin_ref,
             dx_ref, dps_ref, dg_ref, dp_ref):
        @pl.when(pl.program_id(0) == 0)
        def _():
            dg_ref[...] = jnp.zeros_like(dg_ref)

        cs = cos_ref[...]
        sn = sin_ref[...]
        lo_half = (lax.broadcasted_iota(jnp.int32, (1, 128), 1) % HD) < (HD // 2)

        def rope_t(t):
            return t * cs - _rope_swap(t, lo_half) * sn

        for j in range(QW // 128):
            dp_ref[:, 128 * j:128 * j + 128] = rope_t(dq_ref[:, 128 * j:128 * j + 128] * (HD ** -0.5)).astype(BF16)
        dk = _from_query_heads(dkx_ref[...])
        dv = _from_query_heads(dvx_ref[...])
        for j in range(KVW // 128):
            dp_ref[:, QW + 128 * j:QW + 128 * j + 128] = rope_t(dk[:, 128 * j:128 * j + 128]).astype(BF16)
        dp_ref[:, QW + KVW:QW + 2 * KVW] = dv.astype(BF16)
        c0 = QW + 2 * KVW
        dp_ref[:, c0:c0 + CONV_C] = dxbc_ref[...].astype(BF16)
        dp_ref[:, c0 + CONV_C:c0 + CONV_C + SSM_W] = dz_ref[...].astype(BF16)
        dp_ref[:, c0 + CONV_C + SSM_W:INP] = ddt_ref[...].astype(BF16)
        pieces = [dp_ref[:, ISR * d:ISR * (d + 1)] for d in range(N_DEV)]
        zw = jnp.zeros((tm, ISW - ISR), BF16)
        zg = jnp.zeros((tm, ISG - ISR), BF16)
        dn = _dot(jnp.concatenate([t for p in pieces for t in (p, zw)], axis=1), w_ref[...])
        for d in range(N_DEV):
            dps_ref[:, ISG * d:ISG * (d + 1)] = jnp.concatenate([pieces[d], zg], axis=1)
        xv = x_ref[...]
        r = _rs(xv)
        xn = xv * r
        dg_ref[...] += jnp.sum(dn * xn, axis=0, keepdims=True)
        gdn = dn * g_ref[...]
        dx_ref[...] = dres_ref[...] + r * (gdn - xn * jnp.mean(gdn * xn, axis=-1, keepdims=True))

    return pl.pallas_call(
        body, name="inproj_bwd", grid=(T // tm,),
        in_specs=[_rows(tm, D), _rows(tm, QW), _rows(tm, QW), _rows(tm, QW), _rows(tm, CONV_C), _rows(tm, SSM_W), _rows(tm, 128),
                  _rows(tm, D), _const((1, D)), _resident((INP, D)), _rows(tm, 128), _rows(tm, 128)],
        out_specs=[_rows(tm, D), _rows(tm, N_DEV * ISG), _const((1, D))],
        out_shape=[jax.ShapeDtypeStruct((T, D), F32), jax.ShapeDtypeStruct((T, N_DEV * ISG), BF16), jax.ShapeDtypeStruct((1, D), F32)],
        scratch_shapes=[pltpu.VMEM((tm, INP), BF16)],
        compiler_params=_params(("arbitrary",)),
    )(dres, dq, dkx, dvx, dxbc, dz, ddt, x1, gpre, win, cos, sin_s)


def _outproj_fwd(x1, attn, yn, wout, gpost):
    tm = 512

    def body(x_ref, at_ref, yn_ref, w_ref, g_ref, xo_ref, h_ref):
        h = _dot(at_ref[...], w_ref[0:QW, :]) + _dot(yn_ref[...], w_ref[QW:QW + SSM_W, :])
        h_ref[...] = h
        xo_ref[...] = x_ref[...] + h * _rs(h) * g_ref[...]

    return pl.pallas_call(
        body, name="outproj_fwd", grid=(T // tm,),
        in_specs=[_rows(tm, D), _rows(tm, QW), _rows(tm, SSM_W), _resident((QW + SSM_W, D)), _const((1, D))],
        out_specs=[_rows(tm, D), _rows(tm, D)],
        out_shape=[jax.ShapeDtypeStruct((T, D), F32), jax.ShapeDtypeStruct((T, D), F32)],
        compiler_params=_params(("parallel",)),
    )(x1, attn, yn, wout, gpost)


def _outproj_bwd(dx2, h2, gpost, wout, duties=()):
    tm = 512

    def body(dy_ref, h_ref, g_ref, w_ref, dh_ref, dm_ref, dg_ref):
        @pl.when(pl.program_id(0) == 0)
        def _():
            dg_ref[...] = jnp.zeros_like(dg_ref)

        dy = dy_ref[...]
        h = h_ref[...]
        r = _rs(h)
        hn = h * r
        dg_ref[...] += jnp.sum(dy * hn, axis=0, keepdims=True)
        gdy = dy * g_ref[...]
        dh = (r * (gdy - hn * jnp.mean(gdy * hn, axis=-1, keepdims=True))).astype(BF16)
        dh_ref[...] = dh
        dm_ref[...] = _dot_nt(dh, w_ref[...])

    return _call(
        body, name="outproj_bwd", grid=(T // tm,),
        in_specs=[_rows(tm, D), _rows(tm, D), _const((1, D)), _resident((QW + SSM_W, D))],
        out_specs=[_rows(tm, D), _rows(tm, QW + SSM_W), _const((1, D))],
        out_shape=[jax.ShapeDtypeStruct((T, D), BF16), jax.ShapeDtypeStruct((T, QW + SSM_W), F32), jax.ShapeDtypeStruct((1, D), F32)],
        args=[dx2, h2, gpost, wout], sem=("arbitrary",), duties=duties)


G_PER = NQ // NKV
WK = G_PER * HD


def _attn_fwd(q, kx, vx, bias, duties=()):
    def body(q_ref, kx_ref, vx_ref, bias_ref, o_ref, lse_ref):
        lane = lax.broadcasted_iota(jnp.int32, (1, WK), 1)
        lse_ref[...] = jnp.zeros_like(lse_ref)
        for i in range(NAB):
            n = (i + 1) * AB
            rows = slice(i * AB, n)
            qi = q_ref[rows, :]
            kxi = kx_ref[0:n, :]
            vxi = vx_ref[0:n, :]
            bb = bias_ref[:, (NAB - 1 - i) * AB:]
            o_acc = jnp.zeros((AB, WK), F32)
            for g in range(G_PER):
                mg = (lane // HD) == g
                s = _dot_nt(jnp.where(mg, qi, jnp.zeros_like(qi)), kxi) + bb
                m = jnp.max(s, axis=1, keepdims=True)
                p = jnp.exp(s - m)
                l = jnp.sum(p, axis=1, keepdims=True)
                o_acc = jnp.where(mg, _dot(p.astype(BF16), vxi) / l, o_acc)
                lse_ref[rows, g:g + 1] = m + jnp.log(l)
            o_ref[rows, :] = o_acc.astype(BF16)

    col = lambda kv: (0, kv)
    return _call(
        body, name="attn_fwd", grid=(NKV,),
        in_specs=[pl.BlockSpec((T, WK), col), pl.BlockSpec((T, WK), col), pl.BlockSpec((T, WK), col), _const((AB, T))],
        out_specs=[pl.BlockSpec((T, WK), col), pl.BlockSpec((T, 128), col)],
        out_shape=[jax.ShapeDtypeStruct((T, QW), BF16), jax.ShapeDtypeStruct((T, NKV * 128), F32)],
        args=[q, kx, vx, bias], sem=("arbitrary",), duties=duties)


def _attn_bwd(q, kx, vx, o, dmix, lse, bias, duties=()):
    def body(q_ref, kx_ref, vx_ref, o_ref, do_ref, lse_ref, bias_ref, dq_ref, dkx_ref, dvx_ref):
        lane = lax.broadcasted_iota(jnp.int32, (1, WK), 1)
        dkx_ref[...] = jnp.zeros_like(dkx_ref)
        dvx_ref[...] = jnp.zeros_like(dvx_ref)
        for i in range(NAB):
            n = (i + 1) * AB
            rows = slice(i * AB, n)
            qi = q_ref[rows, :]
            dof = do_ref[rows, :]
            doi = dof.astype(BF16)
            prod = dof * o_ref[rows, :].astype(F32)
            kxi = kx_ref[0:n, :]
            vxi = vx_ref[0:n, :]
            bb = bias_ref[:, (NAB - 1 - i) * AB:]
            dq_acc = jnp.zeros((AB, WK), F32)
            for g in range(G_PER):
                mg = (lane // HD) == g
                qm = jnp.where(mg, qi, jnp.zeros_like(qi))
                dom = jnp.where(mg, doi, jnp.zeros_like(doi))
                delta = jnp.sum(jnp.where(mg, prod, 0.0), axis=1, keepdims=True)
                p = jnp.exp(_dot_nt(qm, kxi) + bb - lse_ref[rows, g:g + 1])
                ds = (p * (_dot_nt(dom, vxi) - delta)).astype(BF16)
                dvx_ref[0:n, :] += _dot_tn(p.astype(BF16), dom)
                dkx_ref[0:n, :] += _dot_tn(ds, qm)
                dq_acc = jnp.where(mg, _dot(ds, kxi), dq_acc)
            dq_ref[rows, :] = dq_acc

    col = lambda kv: (0, kv)
    return _call(
        body, name="attn_bwd", grid=(NKV,),
        in_specs=[pl.BlockSpec((T, WK), col), pl.BlockSpec((T, WK), col), pl.BlockSpec((T, WK), col), pl.BlockSpec((T, WK), col),
                  pl.BlockSpec((T, WK), col), pl.BlockSpec((T, 128), col), _const((AB, T))],
        out_specs=[pl.BlockSpec((T, WK), col), pl.BlockSpec((T, WK), col), pl.BlockSpec((T, WK), col)],
        out_shape=[jax.ShapeDtypeStruct((T, QW), F32)] * 3,
        args=[q, kx, vx, o, dmix, lse, bias], sem=("arbitrary",), duties=duties)


def _softplus(x):
    return jnp.maximum(x, 0.0) + jnp.log1p(jnp.exp(-jnp.abs(x)))


def _causal_conv(u, zs, cw_ref, cb_ref):
    zs[8:, :] = u
    sh1, sh2, sh3 = (zs[8 - m:8 - m + L, :] for m in (1, 2, 3))
    return cb_ref[...] + cw_ref[3:4, :] * u + cw_ref[2:3, :] * sh1 + cw_ref[1:2, :] * sh2 + cw_ref[0:1, :] * sh3


def _ssd_chunk_common(xc, dtr, dtb_ref, alx_ref, e_ref):
    sg = _sigmoid(xc)
    act = xc * sg
    pre = dtr + dtb_ref[...]
    dt_x = _dot_hi(_softplus(pre), e_ref[...])
    a_x = -jnp.exp(alx_ref[...])
    ri = lax.broadcasted_iota(jnp.int32, (L, L), 0)
    ci = lax.broadcasted_iota(jnp.int32, (L, L), 1)
    tri = ri >= ci
    acs_x = _dot_hi(tri, dt_x * a_x, a_is_01=True)
    return dict(sg=sg, act=act, pre=pre, dt_x=dt_x, a_x=a_x, tri=tri, acs_x=acs_x)


def _decay(acs_x, acs_t, h, tri):
    col = acs_x[:, HD * h:HD * h + 1]
    row = acs_t[HD * h:HD * h + 1, :]
    return jnp.exp(jnp.where(tri, col - row, NEG))


def _ssd_fwd(xbc, z, dtr, convw, convb, dtb, alx, dskx, ssmn, e, duties=()):
    def body(u_ref, z_ref, dtr_ref, cw_ref, cb_ref, dtb_ref, alx_ref, dsk_ref, sn_ref, e_ref,
             yn_ref, y_ref, hs_ref, xc_ref, zs, hst):
        @pl.when(pl.program_id(0) == 0)
        def _():
            zs[0:8, :] = jnp.zeros((8, CONV_C), F32)
            hst[...] = jnp.zeros_like(hst)

        u = u_ref[...]
        xc = _causal_conv(u, zs, cw_ref, cb_ref)
        xc_ref[...] = xc
        zs[0:8, :] = u[L - 8:, :]
        cm = _ssd_chunk_common(xc, dtr_ref[...], dtb_ref, alx_ref, e_ref)
        act, dt_x, acs_x, tri = cm["act"], cm["dt_x"], cm["acs_x"], cm["tri"]
        xs = act[:, :SSM_W]
        acs_l = acs_x[L - 1:L, :]
        lam_x = jnp.exp(acs_x)
        w_x = jnp.exp(acs_l - acs_x)
        gam_x = jnp.exp(acs_l)
        acs_t = acs_x.T
        xd = xs * dt_x
        xb = xd.astype(BF16)
        xw = (xd * w_x).astype(BF16)
        lo = lax.broadcasted_iota(jnp.int32, (1, 128), 1) < HD
        hs_ref[0] = hst[...]
        pieces = []
        for grp in range(2):
            bb = act[:, SSM_W + SSM_N * grp:SSM_W + SSM_N * (grp + 1)].astype(BF16)
            cb_ = act[:, SSM_W + 2 * SSM_N + SSM_N * grp:SSM_W + 2 * SSM_N + SSM_N * (grp + 1)].astype(BF16)
            cbm = _dot_nt(cb_, bb)
            for jj in range(4):
                j = 4 * grp + jj
                sl = slice(128 * j, 128 * j + 128)
                m0 = (cbm * _decay(acs_x, acs_t, 2 * j, tri)).astype(BF16)
                m1 = (cbm * _decay(acs_x, acs_t, 2 * j + 1, tri)).astype(BF16)
                x2 = xb[:, sl]
                ydiag = jnp.where(lo, _dot(m0, x2), _dot(m1, x2))
                hprev = hst[j]
                yoff = lam_x[:, sl] * _dot(cb_, hprev.astype(BF16))
                pieces.append(ydiag + yoff)
                hst[j] = gam_x[:, sl] * hprev + _dot_tn(bb, xw[:, sl])
        y = jnp.concatenate(pieces, axis=1) + dsk_ref[...] * xs
        y_ref[...] = y
        zv = z_ref[...]
        yz = y * (zv * _sigmoid(zv))
        half = SSM_W // 2
        yn = jnp.concatenate([yz[:, :half] * _rs(yz[:, :half]), yz[:, half:] * _rs(yz[:, half:])], axis=1)
        yn_ref[...] = (yn * sn_ref[...]).astype(BF16)

    return _call(
        body, name="ssd_fwd", grid=(NCH,),
        in_specs=[_rows(L, CONV_C), _rows(L, SSM_W), _rows(L, SSM_H), _const((4, CONV_C)), _const((1, CONV_C)), _const((1, SSM_H)),
                  _const((1, SSM_W)), _const((1, SSM_W)), _const((1, SSM_W)), _const((SSM_H, SSM_W))],
        out_specs=[_rows(L, SSM_W), _rows(L, SSM_W), pl.BlockSpec((1, 8, SSM_N, 128), lambda c: (c, 0, 0, 0)), _rows(L, CONV_C)],
        out_shape=[jax.ShapeDtypeStruct((T, SSM_W), BF16), jax.ShapeDtypeStruct((T, SSM_W), F32),
                   jax.ShapeDtypeStruct((NCH, 8, SSM_N, 128), F32), jax.ShapeDtypeStruct((T, CONV_C), F32)],
        scratch=[pltpu.VMEM((8 + L, CONV_C), F32), pltpu.VMEM((8, SSM_N, 128), F32)],
        args=[xbc, z, dtr, convw, convb, dtb, alx, dskx, ssmn, e], sem=("arbitrary",), duties=duties)


def _ssd_bwd(dmix, xbc, xconv, z, dtr, y, hs, convw, dtb, alx, dskx, ssmn, e, e1, duties=()):
    rev = lambda i: (NCH - 1 - i, 0)

    def body(dyn_ref, u_ref, xc_ref, z_ref, dtr_ref, y_ref, hs_ref, cw_ref, dtb_ref, alx_ref, dsk_ref, sn_ref, e_ref, e1_ref,
             dxbc_ref, dz_ref, ddt_ref, dcw_ref, dcb_ref, dsn_ref, dpar_ref, dh, zd, colbuf):
        step = pl.program_id(0)

        @pl.when(step == 0)
        def _():
            for r in (dh, dcw_ref, dcb_ref, dsn_ref, dpar_ref):
                r[...] = jnp.zeros_like(r)
            zd[L:, :] = jnp.zeros((8, CONV_C), F32)

        u = u_ref[...]
        xc = xc_ref[...]
        cm = _ssd_chunk_common(xc, dtr_ref[...], dtb_ref, alx_ref, e_ref)
        sg, act, pre, dt_x, a_x, tri, acs_x = (cm[k] for k in ("sg", "act", "pre", "dt_x", "a_x", "tri", "acs_x"))
        xs = act[:, :SSM_W]
        acs_l = acs_x[L - 1:L, :]
        lam_x = jnp.exp(acs_x)
        w_x = jnp.exp(acs_l - acs_x)
        gam_x = jnp.exp(acs_l)
        acs_t = acs_x.T
        xd = xs * dt_x
        xb = xd.astype(BF16)
        xdw = xd * w_x
        xw = xdw.astype(BF16)
        lo = lax.broadcasted_iota(jnp.int32, (1, 128), 1) < HD
        row8 = lax.broadcasted_iota(jnp.int32, (8, 1), 0)

        dyn = dyn_ref[...]
        yv = y_ref[...]
        zv = z_ref[...]
        sz = _sigmoid(zv)
        siluz = zv * sz
        yz = yv * siluz
        half = SSM_W // 2
        gy = dyn * sn_ref[...]
        dyz_parts, yzn_parts = [], []
        for hf in range(2):
            part = yz[:, hf * half:(hf + 1) * half]
            r = _rs(part)
            pn = part * r
            gp = gy[:, hf * half:(hf + 1) * half]
            dyz_parts.append(r * (gp - pn * jnp.mean(gp * pn, axis=-1, keepdims=True)))
            yzn_parts.append(pn)
        dyz = jnp.concatenate(dyz_parts, axis=1)
        dsn_ref[...] += jnp.sum(dyn * jnp.concatenate(yzn_parts, axis=1), axis=0, keepdims=True)
        dy = dyz * siluz
        dz_ref[...] = dyz * yv * _dsilu(zv, sz)

        colbuf[...] = jnp.zeros_like(colbuf)
        dx_pieces, dacs_pieces, dacsl_pieces, db_pieces, dc_pieces = [], [], [], [], []
        for grp in range(2):
            bb = act[:, SSM_W + SSM_N * grp:SSM_W + SSM_N * (grp + 1)].astype(BF16)
            cb_ = act[:, SSM_W + 2 * SSM_N + SSM_N * grp:SSM_W + 2 * SSM_N + SSM_N * (grp + 1)].astype(BF16)
            cbm = _dot_nt(cb_, bb)
            dcbm = jnp.zeros((L, L), F32)
            dc_g = jnp.zeros((L, SSM_N), F32)
            db_g = jnp.zeros((L, SSM_N), F32)
            for jj in range(4):
                j = 4 * grp + jj
                sl = slice(128 * j, 128 * j + 128)
                dy2 = dy[:, sl]
                dy2b = dy2.astype(BF16)
                d0 = _decay(acs_x, acs_t, 2 * j, tri)
                d1 = _decay(acs_x, acs_t, 2 * j + 1, tri)
                m0 = cbm * d0
                m1 = cbm * d1
                x2 = xb[:, sl]
                hprev = hs_ref[0, j]
                hprevb = hprev.astype(BF16)
                dhn = dh[j]
                dhnb = dhn.astype(BF16)
                g2 = _dot(bb, dhnb)
                dx_pieces.append(jnp.where(lo, _dot_tn(m0.astype(BF16), dy2b), _dot_tn(m1.astype(BF16), dy2b)) + w_x[:, sl] * g2)
                zero = jnp.zeros_like(dy2b)
                dm0 = _dot_nt(jnp.where(lo, dy2b, zero), x2)
                dm1 = _dot_nt(jnp.where(lo, zero, dy2b), x2)
                dcbm = dcbm + dm0 * d0 + dm1 * d1
                e0 = dm0 * m0
                e1v = dm1 * m1
                colbuf[:, 2 * j:2 * j + 1] = jnp.sum(e0, axis=1, keepdims=True) - jnp.sum(e0.T, axis=1, keepdims=True)
                colbuf[:, 2 * j + 1:2 * j + 2] = jnp.sum(e1v, axis=1, keepdims=True) - jnp.sum(e1v.T, axis=1, keepdims=True)
                yoff = lam_x[:, sl] * _dot(cb_, hprevb)
                gxw = g2 * xdw[:, sl]
                dacs_pieces.append(dy2 * yoff - gxw)
                dacsl_pieces.append(jnp.sum(gxw, axis=0, keepdims=True) + gam_x[:, sl] * jnp.sum(dhn * hprev, axis=0, keepdims=True))
                dyl = (dy2 * lam_x[:, sl]).astype(BF16)
                dc_g = dc_g + _dot_nt(dyl, hprevb)
                db_g = db_g + _dot_nt(xw[:, sl], dhnb)
                dh[j] = gam_x[:, sl] * dhn + _dot_tn(cb_, dyl)
            dcbb = dcbm.astype(BF16)
            dc_pieces.append(dc_g + _dot(dcbb, bb))
            db_pieces.append(db_g + _dot_tn(dcbb, cb_))

        dxd = jnp.concatenate(dx_pieces, axis=1)
        rowi = lax.broadcasted_iota(jnp.int32, (L, 1), 0)
        dacs_x = (jnp.concatenate(dacs_pieces, axis=1) + _dot_hi(colbuf[...], e1_ref[...])
                  + jnp.where(rowi == L - 1, jnp.concatenate(dacsl_pieces, axis=1), 0.0))
        upper = lax.broadcasted_iota(jnp.int32, (L, L), 0) <= lax.broadcasted_iota(jnp.int32, (L, L), 1)
        dadt_x = _dot_hi(upper, dacs_x, a_is_01=True)
        ddt_x = dxd * xs + dadt_x * a_x
        ddtr = _dot_nt_hi(ddt_x, e_ref[...]) * _sigmoid(pre)
        ddt_ref[...] = jnp.zeros_like(ddt_ref)
        ddt_ref[:, 0:SSM_H] = ddtr
        dalx =jnp.sum(dadt_x * dt_x, axis=0, keepdims=True) * a_x
        ddskx = jnp.sum(dy * xs, axis=0, keepdims=True)
        par_x = jnp.where(row8 == 1, dalx, 0.0) + jnp.where(row8 == 2, ddskx, 0.0)
        dpar_ref[...] += _dot_nt_hi(par_x, e_ref[...]) + jnp.where(row8 == 0, jnp.sum(ddtr, axis=0, keepdims=True), 0.0)

        dxs = dxd * dt_x + dsk_ref[...] * dy
        dact = jnp.concatenate([dxs] + db_pieces + dc_pieces, axis=1)
        du = dact * _dsilu(xc, sg)
        dcb_ref[...] += jnp.sum(du, axis=0, keepdims=True)
        zd[0:L, :] = du
        f1, f2, f3 = (zd[m:m + L, :] for m in (1, 2, 3))
        dxbc_ref[...] = cw_ref[3:4, :] * du + cw_ref[2:3, :] * f1 + cw_ref[1:2, :] * f2 + cw_ref[0:1, :] * f3
        dcw = jnp.zeros((8, CONV_C), F32)
        for k, shifted in enumerate((f3, f2, f1, du)):
            dcw = dcw + jnp.where(row8 == k, jnp.sum(shifted * u, axis=0, keepdims=True), 0.0)
        dcw_ref[...] += dcw
        zd[L:, :] = du[:8, :]

    return _call(
        body, name="ssd_bwd", grid=(NCH,),
        in_specs=[pl.BlockSpec((L, SSM_W), lambda i: (NCH - 1 - i, 1)), pl.BlockSpec((L, CONV_C), rev), pl.BlockSpec((L, CONV_C), rev),
                  pl.BlockSpec((L, SSM_W), rev), pl.BlockSpec((L, SSM_H), rev), pl.BlockSpec((L, SSM_W), rev),
                  pl.BlockSpec((1, 8, SSM_N, 128), lambda i: (NCH - 1 - i, 0, 0, 0)),
                  _const((4, CONV_C)), _const((1, SSM_H)), _const((1, SSM_W)), _const((1, SSM_W)), _const((1, SSM_W)),
                  _const((SSM_H, SSM_W)), _const((128, SSM_W))],
        out_specs=[pl.BlockSpec((L, CONV_C), rev), pl.BlockSpec((L, SSM_W), rev), pl.BlockSpec((L, 128), rev),
                   _const((8, CONV_C)), _const((1, CONV_C)), _const((1, SSM_W)), _const((8, SSM_H))],
        out_shape=[jax.ShapeDtypeStruct((T, CONV_C), F32), jax.ShapeDtypeStruct((T, SSM_W), F32), jax.ShapeDtypeStruct((T, 128), F32),
                   jax.ShapeDtypeStruct((8, CONV_C), F32), jax.ShapeDtypeStruct((1, CONV_C), F32), jax.ShapeDtypeStruct((1, SSM_W), F32),
                   jax.ShapeDtypeStruct((8, SSM_H), F32)],
        scratch=[pltpu.VMEM((8, SSM_N, 128), F32), pltpu.VMEM((L + 8, CONV_C), F32), pltpu.VMEM((L, 128), F32)],
        args=[dmix, xbc, xconv, z, dtr, y, hs, convw, dtb, alx, dskx, ssmn, e, e1], sem=("arbitrary",), duties=duties)


def _adam_math(w, g, m, v):
    m = ADAM_B1 * m + (1.0 - ADAM_B1) * g
    v = ADAM_B2 * v + (1.0 - ADAM_B2) * (g * g)
    m_hat = m / (1.0 - ADAM_B1 ** ADAM_STEP)
    v_hat = v / (1.0 - ADAM_B2 ** ADAM_STEP)
    delta = -ADAM_LR * (m_hat / (jnp.sqrt(v_hat) + ADAM_EPS) + ADAM_WD * w)
    return delta, m, v


PACK_W = CONV_C


def _adamw_small(ws, ms, vs, total, conv_g, after=()):
    k = len(ws)

    def body(*refs):
        ins, outs = refs[:3 * k + 2], refs[3 * k + 2 + len(after):]
        total_ref, conv_ref = ins[3 * k], ins[3 * k + 1]
        for a in range(k):
            w_ref, m_ref, v_ref = ins[3 * a:3 * a + 3]
            g = conv_ref[...] if a == k - 1 else total_ref[a:a + 1, 0:w_ref.shape[1]]
            delta, nm, nv = _adam_math(w_ref[...], g, m_ref[...], v_ref[...])
            for ref, val in zip(outs[4 * a:4 * a + 4], (g, delta, nm, nv)):
                ref[...] = val

    vm = pl.BlockSpec(memory_space=pltpu.VMEM)
    args = [x for a in range(k) for x in (ws[a], ms[a], vs[a])] + [total, conv_g]
    res = pl.pallas_call(
        body, name="adamw_small", in_specs=[vm] * len(args) + [ANY] * len(after), out_specs=[vm] * (4 * k),
        out_shape=[jax.ShapeDtypeStruct(ws[a].shape, F32) for a in range(k) for _ in range(4)],
    )(*args, *after)
    return [tuple(res[4 * a:4 * a + 4]) for a in range(k)]


COL_TILE = 512


def _adamw_sharded(ws, ms, vs, chip_sums, from_chips, other_chips, name, after=()):
    k = len(ws)
    rows, cols = ws[0].shape
    prow = chip_sums[0].shape[0]
    assert cols % COL_TILE == 0 and prow >= rows and all(a.shape == ws[0].shape for a in ws)

    def body(ids_ref, *refs):
        ins, outs = refs[:7 * k], refs[7 * k + len(after):]
        for a in range(k):
            w_ref, m_ref, v_ref, s_ref, r1_ref, r2_ref, r3_ref = ins[7 * a:7 * a + 7]
            g = s_ref[...]
            for r in (r1_ref, r2_ref, r3_ref):
                g = g + r[0].astype(F32)
            g = g[:rows]
            delta, nm, nv = _adam_math(w_ref[...], g, m_ref[...], v_ref[...])
            for ref, val in zip(outs[4 * a:4 * a + 4], (g, delta, nm, nv)):
                ref[...] = val

    spec = pl.BlockSpec((rows, COL_TILE), lambda i, ids: (0, i))
    part = lambda j: pl.BlockSpec((1, prow, COL_TILE), lambda i, ids: (ids[j], 0, i))
    one = [spec, spec, spec, pl.BlockSpec((prow, COL_TILE), lambda i, ids: (0, i)), part(0), part(1), part(2)]
    args = [x for a in range(k) for x in (ws[a], ms[a], vs[a], chip_sums[a], from_chips[a], from_chips[a], from_chips[a])]
    res = pl.pallas_call(
        body, name=name,
        grid_spec=pltpu.PrefetchScalarGridSpec(
            num_scalar_prefetch=1, grid=(cols // COL_TILE,),
            in_specs=one * k + [ANY] * len(after), out_specs=[spec] * (4 * k)),
        out_shape=[jax.ShapeDtypeStruct((rows, cols), F32)] * (4 * k),
        compiler_params=_params(("parallel",)),
    )(other_chips, *args, *after)
    return [tuple(res[4 * a:4 * a + 4]) for a in range(k)]


def _chip_sum(mines, recvs, name):
    k = len(mines)
    rows, cols = mines[0].shape[1:]

    def body(*refs):
        own_chip = pl.program_id(0) == 2 * lax.axis_index("x") + lax.axis_index("y")
        for a in range(k):
            a_ref, b_ref = refs[2 * a:2 * a + 2]
            s_ref, sb_ref = refs[2 * k + 2 * a:2 * k + 2 * a + 2]
            s = a_ref[0] + b_ref[0].astype(F32)
            sb_ref[0] = s.astype(BF16)

            @pl.when(own_chip)
            def _(s_ref=s_ref, s=s):
                s_ref[...] = s

    by_chip = pl.BlockSpec((1, rows, cols), lambda c: (c, 0, 0))
    res = pl.pallas_call(
        body, name=name, grid=(N_DEV // 2,),
        in_specs=[by_chip, by_chip] * k, out_specs=[_const((rows, cols)), by_chip] * k,
        out_shape=[jax.ShapeDtypeStruct((rows, cols), F32), jax.ShapeDtypeStruct((N_DEV // 2, rows, cols), BF16)] * k,
        compiler_params=_params(("arbitrary",)),
    )(*[x for pair in zip(mines, recvs) for x in pair])
    return [tuple(res[2 * a:2 * a + 2]) for a in range(k)]


def _all_reduce_small(v, after=()):
    rows = v.shape[0]

    def body(v_ref, *rest):
        out_ref, gath, send_sems, recv_sems = rest[len(after):]
        x, y, c = _place()
        me, sibling = (x, y, c), (x, y, 1 - c)
        chips = [(1 - x, y), (x, 1 - y), (1 - x, 1 - y)]

        def blk(px, py, pc):
            return gath.at[pl.ds((4 * px + 2 * py + pc) * rows, rows), :]

        def copy(k, block, to, src=None):
            return pltpu.make_async_remote_copy(src_ref=blk(*block) if src is None else src, dst_ref=blk(*block),
                                                send_sem=send_sems.at[k], recv_sem=recv_sems.at[k], device_id=to, device_id_type=MESH)

        gath[pl.ds((4 * x + 2 * y + c) * rows, rows), :] = v_ref[...]
        first = [copy(0, me, sibling, src=v_ref)] + [copy(1 + j, me, (*chip, c), src=v_ref) for j, chip in enumerate(chips)]
        for cp in first:
            cp.start()
        passed = [copy(4 + j, (*chip, c), sibling) for j, chip in enumerate(chips)]
        for j, chip in enumerate(chips):
            copy(1 + j, (*chip, c), me).wait_recv()
            passed[j].start()
        copy(0, sibling, me).wait_recv()
        for j, chip in enumerate(chips):
            copy(4 + j, (*chip, 1 - c), me).wait_recv()
        for cp in first + passed:
            cp.wait_send()
        acc = gath[0:rows, :]
        for d in range(1, N_DEV):
            acc = acc + gath[d * rows:(d + 1) * rows, :]
        out_ref[...] = acc

    vm = pl.BlockSpec(memory_space=pltpu.VMEM)
    return pl.pallas_call(
        body, name="all_reduce_small",
        in_specs=[vm] + [ANY] * len(after), out_specs=vm,
        out_shape=jax.ShapeDtypeStruct(v.shape, F32),
        scratch_shapes=[pltpu.VMEM((N_DEV * rows, v.shape[1]), F32), pltpu.SemaphoreType.DMA((7,)), pltpu.SemaphoreType.DMA((7,))],
    )(v, *after)


def _tables(positions, duties, name):
    inv_freq = ROPE_THETA ** (-jnp.arange(0, HD, 2, dtype=F32) / HD)
    inv = jnp.concatenate([inv_freq] * 4)[None, :]
    sign = jnp.where((jnp.arange(128) % HD) < (HD // 2), -1.0, 1.0).astype(F32)[None, :]

    def body(pos_ref, inv_ref, sign_ref, cos_ref, sin_ref, bias_ref):
        ang = pos_ref[...] * inv_ref[...]
        cos_ref[...] = jnp.cos(ang)
        sin_ref[...] = sign_ref[...] * jnp.sin(ang)
        d = lax.broadcasted_iota(jnp.int32, (AB, T), 0) - lax.broadcasted_iota(jnp.int32, (AB, T), 1) + (T - AB)
        cnt = jnp.zeros((AB, T), F32)
        for window, dil in DILATIONS:
            cnt = cnt + ((d >= 0) & ((d & (dil - 1)) == 0) & (d <= window)).astype(F32)
        bias_ref[...] = jnp.where(cnt > 0, jnp.log(jnp.maximum(cnt, 1.0)), NEG)

    vm = pl.BlockSpec(memory_space=pltpu.VMEM)
    return _call(body, name=name, grid=None, in_specs=[vm] * 3, out_specs=[vm] * 3,
                 out_shape=[jax.ShapeDtypeStruct((T, 128), F32), jax.ShapeDtypeStruct((T, 128), F32), jax.ShapeDtypeStruct((AB, T), F32)],
                 args=[positions.reshape(T, 1).astype(F32), inv, sign], sem=None, duties=duties)


def _selectors():
    lane = jnp.arange(QW)
    e = (lane[None, :] // HD == jnp.arange(SSM_H)[:, None]).astype(F32)
    e1 = ((lane[None, :] == HD * jnp.arange(128)[:, None]) & (jnp.arange(128)[:, None] < SSM_H)).astype(F32)
    return e, e1


WEIGHTS = ['ffn1_pre_norm', 'ffn1_w_gate', 'ffn1_w_up', 'ffn1_w_down', 'ffn1_post_norm', 'mix_pre_norm', 'w_in', 'conv_w', 'conv_b',
           'dt_bias', 'a_log', 'd_skip', 'ssm_norm', 'w_out', 'mix_post_norm', 'ffn2_pre_norm', 'ffn2_w_gate', 'ffn2_w_up',
           'ffn2_w_down', 'ffn2_post_norm']
COL_SHARDED = ['ffn1_w_gate', 'ffn1_w_up', 'ffn2_w_gate', 'ffn2_w_up', 'w_in']
ROW_SHARDED = ['ffn1_w_down', 'ffn2_w_down', 'w_out']
BIG = COL_SHARDED + ROW_SHARDED
FFN_BIG = COL_SHARDED[:4] + ROW_SHARDED[:2]
SMALL = ['ffn1_pre_norm', 'ffn1_post_norm', 'mix_pre_norm', 'conv_b', 'dt_bias', 'a_log', 'd_skip', 'ssm_norm', 'mix_post_norm',
         'ffn2_pre_norm', 'ffn2_post_norm']
FFN1 = ['ffn1_w_gate', 'ffn1_w_up', 'ffn1_w_down']
FFN2 = ['ffn2_w_gate', 'ffn2_w_up', 'ffn2_w_down']


def _wire_block(name, a):
    if name in FFN_BIG:
        return jnp.pad(a.astype(BF16), ((0, FSH - FSR), (0, 0)))
    if name == "w_in":
        return jnp.pad(a.astype(BF16), ((0, ISW - ISR), (0, 0)))
    return a if name == "conv_w" else a.astype(BF16)


def _whole_from_gathered(name, a):
    if name == "conv_w":
        return jnp.transpose(a, (1, 0, 2)).reshape(a.shape[1], -1)
    return a.reshape(-1, D)


def _step(x, positions, target, small, blocks=None, whole=None):
    dist = blocks is not None
    core = "mesh" if dist else 0
    w = dict(small)
    if whole:
        w.update(whole)

    def gather(names):
        return [_gather_duty([_wire_block(n, blocks[n]) for n in names])] if dist else []

    def put(names, results):
        if dist:
            for n, r in zip(names, results[0]):
                w[n] = _whole_from_gathered(n, r)

    g, sums, red = {}, {}, {}

    def swap(names):
        return [_swap_duty([g[n][1] for n in names])] if dist else []

    def chip_sums(names, from_sibling):
        if dist:
            res = _chip_sum([g[n][0] for n in names], list(from_sibling), "chip_sum_" + names[0])
            sums.update(zip(names, res))

    def exchange(names):
        return [_exchange_duty([sums[n][1] for n in names])] if dist else []

    def reduced(names, from_chips):
        if dist:
            for n, recv in zip(names, from_chips):
                red[n] = (sums[n][0], recv)

    e, e1 = _selectors()
    alx = jnp.repeat(w["a_log"], HD, axis=1)
    dskx = jnp.repeat(w["d_skip"], HD, axis=1)

    (cos, sin_s, bias), got = _tables(positions, gather(FFN1), "gather_ffn1")
    put(FFN1, got)
    (x1, n1, a1, b1, hm1, h1), got = _ffn_fwd(x, w["ffn1_pre_norm"], w["ffn1_w_gate"], w["ffn1_w_up"], w["ffn1_w_down"],
                                              w["ffn1_post_norm"], "ffn1_fwd", gather(["w_in", "conv_w"]))
    put(["w_in", "conv_w"], got)
    (n2, q, kx, vx, xbc, z, dtr), got = _inproj_fwd(x1, w["mix_pre_norm"], w["w_in"], cos, sin_s, gather(["w_out"]))
    put(["w_out"], got)
    (attn, lse), got = _attn_fwd(q, kx, vx, bias, gather(FFN2[:2]))
    put(FFN2[:2], got)
    (yn, y, hs, xconv), got = _ssd_fwd(xbc, z, dtr, w["conv_w"], w["conv_b"], w["dt_bias"], alx, dskx, w["ssm_norm"], e, gather(FFN2[2:]))
    put(FFN2[2:], got)
    x2, h2 = _outproj_fwd(x1, attn, yn, w["w_out"], w["mix_post_norm"])
    (dx3, n3, a3, b3, hm3, h3, ss), _ = _ffn_fwd(x2, w["ffn2_pre_norm"], w["ffn2_w_gate"], w["ffn2_w_up"], w["ffn2_w_down"],
                                                 w["ffn2_post_norm"], "ffn2_fwd", target=target)

    (dx2, da3, db3, dh3, g["ffn2_pre_norm"], g["ffn2_post_norm"]), _ = _ffn_bwd(
        dx3, x2, a3, b3, h3, w["ffn2_pre_norm"], w["ffn2_post_norm"], w["ffn2_w_gate"], w["ffn2_w_up"], w["ffn2_w_down"], "ffn2_bwd")
    g["ffn2_w_down"] = _matmul_tn(hm3, dh3, "ffn2_dwd", core)[0]
    g["ffn2_w_gate"] = _matmul_tn(da3, n3, "ffn2_dwg", core)[0]
    g["ffn2_w_up"] = _matmul_tn(db3, n3, "ffn2_dwu", core)[0]

    (dh2, dmix, g["mix_post_norm"]), got = _outproj_bwd(dx2, h2, w["mix_post_norm"], w["w_out"], swap(FFN2))
    chip_sums(FFN2, got[0] if dist else None)
    g["w_out"] = _dwout(attn, yn, dh2, core)
    (dq, dkx, dvx), got = _attn_bwd(q, kx, vx, attn, dmix, lse, bias, exchange(FFN2) + swap(["w_out"]))
    if dist:
        reduced(FFN2, got[0])
        chip_sums(["w_out"], got[1])
    (dxbc, dz, ddt, dcw, g["conv_b"], g["ssm_norm"], dpar), got = _ssd_bwd(
        dmix, xbc, xconv, z, dtr, y, hs, w["conv_w"], w["dt_bias"], alx, dskx, w["ssm_norm"], e, e1, exchange(["w_out"]))
    reduced(["w_out"], got[0] if dist else None)
    g["conv_w"] = dcw[0:4]
    g["dt_bias"], g["a_log"], g["d_skip"] = dpar[0:1], dpar[1:2], dpar[2:3]
    dx1, dproj, g["mix_pre_norm"] = _inproj_bwd(dx2, dq, dkx, dvx, dxbc, dz, ddt, x1, w["mix_pre_norm"], w["w_in"], cos, sin_s)
    g["w_in"] = _matmul_tn(dproj, n2, "dwin", core)[0]

    started = {}

    def start(n):
        started[n] = _exchange_start(sums[n][1], "start_exchange_" + n, FSR if n in FFN_BIG else None)
        return [started[n]["token"]]

    after = []
    if dist:
        chip_sums(["w_in"], _comm_only(swap(["w_in"]), "swap_w_in")[0])
        after = start("w_in")
    (dx0, da1, db1, dh1, g["ffn1_pre_norm"], g["ffn1_post_norm"]), _ = _ffn_bwd(
        dx1, x, a1, b1, h1, w["ffn1_pre_norm"], w["ffn1_post_norm"], w["ffn1_w_gate"], w["ffn1_w_up"], w["ffn1_w_down"], "ffn1_bwd",
        after)
    total = None
    if dist:
        widen = lambda a: jnp.pad(a, ((0, 0), (0, PACK_W - a.shape[1])))
        pack = jnp.concatenate([widen(g[n]) for n in SMALL] + [g["conv_w"], widen(ss[:, 0:1])])
        assert pack.shape[0] % 8 == 0
        total = _all_reduce_small(pack, after)
        after = [total]
    g["ffn1_w_down"], _ = _matmul_tn(hm1, dh1, "ffn1_dwd", core, after=after)
    g["ffn1_w_gate"], got = _matmul_tn(da1, n1, "ffn1_dwg", core, duties=swap(["ffn1_w_down"]))
    if dist:
        chip_sums(["ffn1_w_down"], got[0])
        after = start("ffn1_w_down")
    g["ffn1_w_up"], got = _matmul_tn(db1, n1, "ffn1_dwu", core, after=after, duties=swap(["ffn1_w_gate"]))
    if dist:
        chip_sums(["ffn1_w_gate"], got[0])
        after = start("ffn1_w_gate")
        chip_sums(["ffn1_w_up"], _comm_only(swap(["ffn1_w_up"]), "swap_ffn1_w_up", after=after)[0])
        start("ffn1_w_up")
    return ss, dx0, g, red, {n: (sums[n][0], started[n]) for n in started}, total


def kernel(x, positions, ffn1_pre_norm, ffn1_w_gate, ffn1_w_up, ffn1_w_down, ffn1_post_norm, mix_pre_norm, w_in, conv_w, conv_b, dt_bias, a_log, d_skip, ssm_norm, w_out, mix_post_norm, ffn2_pre_norm, ffn2_w_gate, ffn2_w_up, ffn2_w_down, ffn2_post_norm, loss_target, m_ffn1_pre_norm, m_ffn1_w_gate, m_ffn1_w_up, m_ffn1_w_down, m_ffn1_post_norm, m_mix_pre_norm, m_w_in, m_conv_w, m_conv_b, m_dt_bias, m_a_log, m_d_skip, m_ssm_norm, m_w_out, m_mix_post_norm, m_ffn2_pre_norm, m_ffn2_w_gate, m_ffn2_w_up, m_ffn2_w_down, m_ffn2_post_norm, v_ffn1_pre_norm, v_ffn1_w_gate, v_ffn1_w_up, v_ffn1_w_down, v_ffn1_post_norm, v_mix_pre_norm, v_w_in, v_conv_w, v_conv_b, v_dt_bias, v_a_log, v_d_skip, v_ssm_norm, v_w_out, v_mix_post_norm, v_ffn2_pre_norm, v_ffn2_w_gate, v_ffn2_w_up, v_ffn2_w_down, v_ffn2_post_norm):
    given = dict(locals())
    drop = lambda n, a: a if n in SMALL else (a[0].T if n in COL_SHARDED else a[0])
    w = {n: drop(n, given[n]) for n in WEIGHTS}
    m = {n: drop(n, given["m_" + n]) for n in WEIGHTS}
    v = {n: drop(n, given["v_" + n]) for n in WEIGHTS}
    cx, cy, cc = _place()
    others = [2 * (1 - cx) + cy, 2 * cx + (1 - cy), 2 * (1 - cx) + (1 - cy)]

    _, grad_x, g, red, pending, total = _step(x[0], positions, loss_target[0], {n: w[n] for n in SMALL},
                                              blocks={n: w[n] for n in BIG + ["conv_w"]})
    chip_ids = jnp.stack(others).astype(jnp.int32)
    out_g, out_d, out_m, out_v = {}, {}, {}, {}

    def update(names, sums, recvs, label, after=()):
        res = _adamw_sharded([w[n] for n in names], [m[n] for n in names], [v[n] for n in names], sums, recvs, chip_ids,
                             "adamw_" + label, after)
        for n, (gn, dn, mn, vn) in zip(names, res):
            out_g[n], out_d[n], out_m[n], out_v[n] = gn, dn, mn, vn

    last_start = [pending["ffn1_w_up"][1]["token"]]
    update(FFN2, [red[n][0] for n in FFN2], [red[n][1] for n in FFN2], "ffn2", last_start)
    update(["w_out"], [red["w_out"][0]], [red["w_out"][1]], "w_out", last_start)

    n_small = len(SMALL)
    conv_g = lax.dynamic_slice_in_dim(total[n_small:n_small + 4], (4 * cx + 2 * cy + cc) * (CONV_C // N_DEV), CONV_C // N_DEV, axis=1)
    loss = 0.5 * total[n_small + 4, 0] / D
    names = SMALL + ["conv_w"]
    res = _adamw_small([w[n] for n in names], [m[n] for n in names], [v[n] for n in names], total, conv_g, last_start)
    for n, (gn, dn, mn, vn) in zip(names, res):
        out_g[n], out_d[n], out_m[n], out_v[n] = gn, dn, mn, vn
    done = [out_v[n] for n in FFN2 + ["w_out", "conv_w"]]
    update(["w_in"], [pending["w_in"][0]], [_exchange_wait(pending["w_in"][1], done, "wait_exchange_w_in")], "w_in")
    done = [out_v["w_in"]]
    update(FFN1, [pending[n][0] for n in FFN1], [_exchange_wait(pending[n][1], done, "wait_exchange_" + n) for n in FFN1], "ffn1")

    outs = [loss, grad_x[None]]
    for d in (out_g, out_d, out_m, out_v):
        outs += [d[n] if n in SMALL else (d[n].T[None] if n in COL_SHARDED else d[n][None]) for n in WEIGHTS]
    return tuple(outs)
```

```python
import functools

import jax
import jax.numpy as jnp
from jax import lax
from jax.experimental import pallas as pl
from jax.experimental.pallas import tpu as pltpu

F32 = jnp.float32
BF16 = jnp.bfloat16
MESH = pl.DeviceIdType.MESH

N_DEV = 8
T = 2048
D = 1024
FF = 2816
FSR = FF // N_DEV
FSH = 384
FFP = N_DEV * FSH
HD = 64
NQ = 16
NKV = 4
QW = NQ * HD
KVW = NKV * HD
SSM_W = 1024
SSM_H = 16
SSM_N = 128
CONV_C = SSM_W + 2 * 2 * SSM_N
IN_COLS = 4112
INP = 4224
ISR = IN_COLS // N_DEV
ISW = 528
ISG = 640
L = 128
NCH = T // L
AB = 256
NAB = T // AB
EPS = 1e-6
NEG = -1e30
ROPE_THETA = 10000.0
DILATIONS = ((128, 1), (512, 4), (2048, 16))

ADAM_LR = 0.001
ADAM_B1 = 0.9
ADAM_B2 = 0.999
ADAM_EPS = 1e-08
ADAM_WD = 0.01
ADAM_STEP = 10

VMEM_LIMIT = 58 * 1024 * 1024


def _params(sem, vmem=VMEM_LIMIT):
    return pltpu.CompilerParams(dimension_semantics=sem, vmem_limit_bytes=vmem)


def _dot(a, b):
    return jnp.dot(a, b, preferred_element_type=F32)


def _dot_nt(a, b):
    return lax.dot_general(a, b, (((1,), (1,)), ((), ())), preferred_element_type=F32)


def _dot_tn(a, b):
    return lax.dot_general(a, b, (((0,), (0,)), ((), ())), preferred_element_type=F32)


def _split3(x):
    hi = x.astype(BF16)
    r1 = x - hi.astype(F32)
    mid = r1.astype(BF16)
    lo = (r1 - mid.astype(F32)).astype(BF16)
    return hi, mid, lo


def _dot_hi(a, b, a_is_01=False):
    if a_is_01:
        sel = a.astype(BF16)
        return sum(_dot(sel, p) for p in _split3(b))
    sel = b.astype(BF16)
    return sum(_dot(p, sel) for p in _split3(a))


def _dot_nt_hi(a, b):
    sel = b.astype(BF16)
    return sum(_dot_nt(p, sel) for p in _split3(a))


def _rs(x):
    return lax.rsqrt(jnp.mean(x * x, axis=-1, keepdims=True) + EPS)


def _sigmoid(x):
    return jax.nn.sigmoid(x)


def _dsilu(x, s):
    return s * (1.0 + x * (1.0 - s))


def _resident(shape):
    nd = len(shape)
    return pl.BlockSpec(shape, lambda *_: (0,) * nd, pipeline_mode=pl.Buffered(1))


def _const(shape):
    nd = len(shape)
    return pl.BlockSpec(shape, lambda *_: (0,) * nd)


def _rows(tm, cols):
    return pl.BlockSpec((tm, cols), lambda i: (i, 0))


ANY = pl.BlockSpec(memory_space=pl.ANY)


def _place():
    return lax.axis_index("x"), lax.axis_index("y"), lax.axis_index("c")


def _gather_duty(arrays):
    n = len(arrays)
    results = [jax.ShapeDtypeStruct((N_DEV,) + a.shape, a.dtype) for a in arrays]

    def make(ins, outs, send_sems, recv_sems, local_sems):
        x, y, c = _place()
        me, sibling = (x, y, c), (x, y, 1 - c)
        chips = [(1 - x, y), (x, 1 - y), (1 - x, 1 - y)]

        def place_of(a, px, py, pc):
            return outs[a].at[4 * px + 2 * py + pc]

        def copy(a, k, block, to, src=None):
            dst = place_of(a, *block)
            return pltpu.make_async_remote_copy(src_ref=dst if src is None else src, dst_ref=dst,
                                                send_sem=send_sems.at[7 * a + k], recv_sem=recv_sems.at[7 * a + k],
                                                device_id=to, device_id_type=MESH)

        def own(a):
            return pltpu.make_async_copy(ins[a], place_of(a, *me), local_sems.at[a])

        def first(a):
            return [copy(a, 0, me, sibling, src=ins[a])] + [copy(a, 1 + j, me, (*chip, c), src=ins[a]) for j, chip in enumerate(chips)]

        def start():
            for a in range(n):
                own(a).start()
            for a in range(n):
                for cp in first(a):
                    cp.start()

        def finish():
            for j, chip in enumerate(chips):
                for a in range(n):
                    copy(a, 1 + j, (*chip, c), me).wait_recv()
                    copy(a, 4 + j, (*chip, c), sibling).start()
            for a in range(n):
                copy(a, 0, sibling, me).wait_recv()
                for j, chip in enumerate(chips):
                    copy(a, 4 + j, (*chip, 1 - c), me).wait_recv()
            for a in range(n):
                for cp in first(a) + [copy(a, 4 + j, (*chip, c), sibling) for j, chip in enumerate(chips)]:
                    cp.wait_send()
                own(a).wait()

        return start, finish

    return dict(operands=list(arrays), results=results, sems=(7 * n, 7 * n, n), make=make)


def _swap_duty(arrays):
    n = len(arrays)
    half = N_DEV // 2
    results = [jax.ShapeDtypeStruct(a.shape, a.dtype) for a in arrays]

    def make(ins, outs, send_sems, recv_sems):
        x, y, c = _place()

        def copies():
            return [pltpu.make_async_remote_copy(src_ref=ins[a].at[k], dst_ref=outs[a].at[k],
                                                 send_sem=send_sems.at[half * a + k], recv_sem=recv_sems.at[half * a + k],
                                                 device_id=(x, y, 1 - c), device_id_type=MESH)
                    for a in range(n) for k in range(half)]

        def start():
            for cp in copies():
                cp.start()

        def finish():
            for cp in copies():
                cp.wait()

        return start, finish

    return dict(operands=list(arrays), results=results, sems=(half * n, half * n), make=make)


def _exchange_duty(arrays):
    n = len(arrays)
    results = [jax.ShapeDtypeStruct(a.shape, a.dtype) for a in arrays]

    def make(ins, outs, send_sems, recv_sems):
        x, y, c = _place()
        chips = [(1 - x, y), (x, 1 - y), (1 - x, 1 - y)]
        my_chip = 2 * x + y

        def sends():
            return [pltpu.make_async_remote_copy(src_ref=ins[a].at[2 * px + py], dst_ref=outs[a].at[my_chip],
                                                 send_sem=send_sems.at[3 * a + j], recv_sem=recv_sems.at[3 * a + j],
                                                 device_id=(px, py, c), device_id_type=MESH)
                    for a in range(n) for j, (px, py) in enumerate(chips)]

        def start():
            for cp in sends():
                cp.start()

        def finish():
            for a in range(n):
                for j, (px, py) in enumerate(chips):
                    pltpu.make_async_remote_copy(src_ref=ins[a].at[my_chip], dst_ref=outs[a].at[2 * px + py],
                                                 send_sem=send_sems.at[3 * a + j], recv_sem=recv_sems.at[3 * a + j],
                                                 device_id=(px, py, c), device_id_type=MESH).wait_recv()
            for cp in sends():
                cp.wait_send()

        return start, finish

    return dict(operands=list(arrays), results=results, sems=(3 * n, 3 * n), make=make)


def _call(body, *, name, grid, in_specs, out_specs, out_shape, args, sem, scratch=(), duties=(), after=()):
    n_in, n_out, n_scr = len(in_specs), len(out_specs), len(scratch)
    sem_shapes = [pltpu.SemaphoreType.DMA((k,)) for d in duties for k in d["sems"]]

    def full(*refs):
        pos = [0]

        def take(k):
            pos[0] += k
            return refs[pos[0] - k:pos[0]]

        ins = take(n_in)
        d_ins = [take(len(d["operands"])) for d in duties]
        take(len(after))
        outs = take(n_out)
        d_outs = [take(len(d["results"])) for d in duties]
        scr = take(n_scr)
        d_sems = [take(len(d["sems"])) for d in duties]
        hooks = [d["make"](di, do, *ds) for d, di, do, ds in zip(duties, d_ins, d_outs, d_sems)]
        if grid and hooks:
            ids = [pl.program_id(k) for k in range(len(grid))]
            first = functools.reduce(jnp.logical_and, [i == 0 for i in ids])
            last = functools.reduce(jnp.logical_and, [i == g - 1 for i, g in zip(ids, grid)])

            @pl.when(first)
            def _():
                for start, _ in hooks:
                    start()

            body(*ins, *outs, *scr)

            @pl.when(last)
            def _():
                for _, finish in hooks:
                    finish()
        else:
            for start, _ in hooks:
                start()
            body(*ins, *outs, *scr)
            for _, finish in hooks:
                finish()

    d_args = [a for d in duties for a in d["operands"]]
    d_res = [r for d in duties for r in d["results"]]
    kwargs = dict(grid=grid) if grid else {}
    res = pl.pallas_call(
        full, name=name, in_specs=list(in_specs) + [ANY] * (len(d_args) + len(after)), out_specs=list(out_specs) + [ANY] * len(d_res),
        out_shape=list(out_shape) + d_res, scratch_shapes=list(scratch) + sem_shapes,
        compiler_params=_params(sem) if grid else None, **kwargs,
    )(*args, *d_args, *after)
    own, rest = list(res[:n_out]), list(res[n_out:])
    by_duty = []
    for d in duties:
        by_duty.append(rest[:len(d["results"])])
        rest = rest[len(d["results"]):]
    return own, by_duty


def _comm_only(duties, name, after=()):
    return _call(lambda: None, name=name, grid=None, in_specs=[], out_specs=[], out_shape=[], args=[], sem=None, duties=duties,
                 after=after)[1]


HBM = pl.BlockSpec(memory_space=pltpu.HBM)
SEMS = pl.BlockSpec(memory_space=pltpu.SEMAPHORE)
SIDE_EFFECT = pltpu.SideEffectType.DATAFLOW_SIDE_EFFECTING
N_OTHER_CHIPS = 3


def _chip_copies(src_ref, land_ref, sems, rows):
    x, y, c = _place()
    chips = [(1 - x, y), (x, 1 - y), (1 - x, 1 - y)]
    part = (lambda ref: ref) if rows is None else (lambda ref: ref.at[pl.ds(0, rows)])
    return [pltpu.make_async_remote_copy(src_ref=part(src_ref.at[2 * px + py]), dst_ref=part(land_ref.at[2 * x + y]),
                                         send_sem=sems[j], recv_sem=sems[N_OTHER_CHIPS + j], device_id=(px, py, c), device_id_type=MESH)
            for j, (px, py) in enumerate(chips)]


def _exchange_start(pb, name, rows=None):
    n_sem = 2 * N_OTHER_CHIPS

    def body(pb_ref, land_ref, *rest):
        for cp in _chip_copies(pb_ref, land_ref, rest[:n_sem], rows):
            cp.start()
        token = rest[n_sem + 2]
        token[...] = jnp.zeros_like(token)

    res = pl.pallas_call(
        body, name=name,
        out_shape=(pltpu.SemaphoreType.DMA(()),) * n_sem + (pltpu.HBM(pb.shape, pb.dtype), pltpu.HBM(pb.shape, pb.dtype),
                                                              jax.ShapeDtypeStruct((8, 128), F32)),
        in_specs=(HBM, HBM), out_specs=(SEMS,) * n_sem + (HBM, HBM, pl.BlockSpec(memory_space=pltpu.VMEM)),
        input_output_aliases={0: n_sem, 1: n_sem + 1},
        compiler_params=pltpu.CompilerParams(has_side_effects=SIDE_EFFECT),
    )(pltpu.with_memory_space_constraint(pb, pltpu.HBM), pltpu.with_memory_space_constraint(lax.empty(pb.shape, pb.dtype), pltpu.HBM))
    return dict(sems=res[:n_sem], src=res[n_sem], land=res[n_sem + 1], token=res[n_sem + 2], rows=rows)


def _exchange_wait(started, after, name):
    n_sem = 2 * N_OTHER_CHIPS

    def body(pb_ref, land_ref, *rest):
        for cp in _chip_copies(pb_ref, land_ref, rest[:n_sem], started["rows"]):
            cp.wait_send()
            cp.wait_recv()

    src, land = started["src"], started["land"]
    return pl.pallas_call(
        body, name=name, out_shape=(pltpu.HBM(src.shape, src.dtype), pltpu.HBM(land.shape, land.dtype)),
        in_specs=(HBM, HBM) + (SEMS,) * n_sem + (ANY,) * len(after), out_specs=(HBM, HBM), input_output_aliases={0: 0, 1: 1},
        compiler_params=pltpu.CompilerParams(has_side_effects=SIDE_EFFECT),
    )(src, land, *started["sems"], *after)[1]


def _ffn_fwd(x, gpre, wg, wu, wd, gpost, name, duties=(), target=None):
    tm = 256
    n_in = 6 if target is None else 7

    def body(*refs):
        x_ref, gpre_ref, wg_ref, wu_ref, wd_ref, gpost_ref = refs[:6]
        xo_ref, n_ref, a_ref, b_ref, hm_ref, h_ref = refs[n_in:n_in + 6]
        xv = x_ref[...]
        n = (xv * _rs(xv) * gpre_ref[...]).astype(BF16)
        a = _dot_nt(n, wg_ref[...])
        b = _dot_nt(n, wu_ref[...])
        hm = (a * _sigmoid(a) * b).astype(BF16)
        h = _dot(hm, wd_ref[...])
        xo = xv + 0.5 * (h * _rs(h) * gpost_ref[...])
        n_ref[...] = n
        a_ref[...] = a.astype(BF16)
        b_ref[...] = b.astype(BF16)
        hm_ref[...] = hm
        h_ref[...] = h
        if target is None:
            xo_ref[...] = xo
        else:
            ss_ref = refs[n_in + 6]

            @pl.when(pl.program_id(0) == 0)
            def _():
                ss_ref[...] = jnp.zeros_like(ss_ref)

            err = xo - refs[6][...]
            xo_ref[...] = err * (1.0 / D)
            ss_ref[...] += jnp.sum(jnp.sum(err * err, axis=1, keepdims=True), axis=0, keepdims=True)

    loss_in = [] if target is None else [_rows(tm, D)]
    loss_out = [] if target is None else [_const((1, 128))]
    loss_shape = [] if target is None else [jax.ShapeDtypeStruct((1, 128), F32)]
    return _call(
        body, name=name, grid=(T // tm,),
        in_specs=[_rows(tm, D), _const((1, D)), _resident((FFP, D)), _resident((FFP, D)), _resident((FFP, D)), _const((1, D))] + loss_in,
        out_specs=[_rows(tm, D), _rows(tm, D), _rows(tm, FFP), _rows(tm, FFP), _rows(tm, FFP), _rows(tm, D)] + loss_out,
        out_shape=[jax.ShapeDtypeStruct((T, D), F32), jax.ShapeDtypeStruct((T, D), BF16), jax.ShapeDtypeStruct((T, FFP), BF16),
                   jax.ShapeDtypeStruct((T, FFP), BF16), jax.ShapeDtypeStruct((T, FFP), BF16), jax.ShapeDtypeStruct((T, D), F32)]
        + loss_shape,
        args=[x, gpre, wg, wu, wd, gpost] + ([] if target is None else [target]), sem=("arbitrary",), duties=duties)


def _ffn_bwd(dxo, x, a, b, h, gpre, gpost, wg, wu, wd, name, after=()):
    tm = 256

    def body(dxo_ref, x_ref, a_ref, b_ref, h_ref, gpre_ref, gpost_ref, wg_ref, wu_ref, wd_ref,
             dx_ref, da_ref, db_ref, dh_ref, dgpre_ref, dgpost_ref):
        @pl.when(pl.program_id(0) == 0)
        def _():
            dgpre_ref[...] = jnp.zeros_like(dgpre_ref)
            dgpost_ref[...] = jnp.zeros_like(dgpost_ref)

        dy = dxo_ref[...]
        h = h_ref[...]
        hn = h * _rs(h)
        r2 = _rs(h)
        dgpost_ref[...] += jnp.sum(0.5 * dy * hn, axis=0, keepdims=True)
        gdy = 0.5 * dy * gpost_ref[...]
        dh = r2 * (gdy - hn * jnp.mean(gdy * hn, axis=-1, keepdims=True))
        dhb = dh.astype(BF16)
        dh_ref[...] = dhb
        dhm = _dot_nt(dhb, wd_ref[...])
        av = a_ref[...].astype(F32)
        bv = b_ref[...].astype(F32)
        sg = _sigmoid(av)
        db = (dhm * (av * sg)).astype(BF16)
        da = (dhm * bv * _dsilu(av, sg)).astype(BF16)
        da_ref[...] = da
        db_ref[...] = db
        dn = _dot(da, wg_ref[...]) + _dot(db, wu_ref[...])
        xv = x_ref[...]
        r = _rs(xv)
        xn = xv * r
        dgpre_ref[...] += jnp.sum(dn * xn, axis=0, keepdims=True)
        gdn = dn * gpre_ref[...]
        dx_ref[...] = dy + r * (gdn - xn * jnp.mean(gdn * xn, axis=-1, keepdims=True))

    return _call(
        body, name=name, grid=(T // tm,),
        in_specs=[_rows(tm, D), _rows(tm, D), _rows(tm, FFP), _rows(tm, FFP), _rows(tm, D), _const((1, D)), _const((1, D)),
                  _resident((FFP, D)), _resident((FFP, D)), _resident((FFP, D))],
        out_specs=[_rows(tm, D), _rows(tm, FFP), _rows(tm, FFP), _rows(tm, D), _const((1, D)), _const((1, D))],
        out_shape=[jax.ShapeDtypeStruct((T, D), F32), jax.ShapeDtypeStruct((T, FFP), BF16), jax.ShapeDtypeStruct((T, FFP), BF16),
                   jax.ShapeDtypeStruct((T, D), BF16), jax.ShapeDtypeStruct((1, D), F32), jax.ShapeDtypeStruct((1, D), F32)],
        args=[dxo, x, a, b, h, gpre, gpost, wg, wu, wd], sem=("arbitrary",), after=after)


def _core_index(core):
    return lax.axis_index("c") if core == "mesh" else core


def _by_core(res, o_ref, ob_ref, core):
    r = res.shape[0] // 2
    c = jnp.asarray(_core_index(core))

    @pl.when(c == 0)
    def _():
        o_ref[0] = res[:r]
        ob_ref[0] = res[r:].astype(BF16)

    @pl.when(c == 1)
    def _():
        o_ref[0] = res[r:]
        ob_ref[0] = res[:r].astype(BF16)


def _matmul_tn(a, b, name, core, after=(), duties=()):
    k, m = a.shape
    n = b.shape[1]
    r = m // N_DEV
    assert m == N_DEV * r and r % 128 == 0

    def body(a_ref, b_ref, o_ref, ob_ref):
        _by_core(_dot_tn(a_ref[...], b_ref[...]), o_ref, ob_ref, core)

    spec = pl.BlockSpec((1, r, n), lambda i: (i, 0, 0))
    return _call(body, name=name, grid=(N_DEV // 2,), in_specs=[pl.BlockSpec((k, 2 * r), lambda i: (0, i)), _resident((k, n))],
                 out_specs=[spec, spec],
                 out_shape=[jax.ShapeDtypeStruct((N_DEV // 2, r, n), F32), jax.ShapeDtypeStruct((N_DEV // 2, r, n), BF16)],
                 args=[a, b], sem=("arbitrary",), duties=duties, after=after)


def _dwout(attn, yn, dh2, core):
    rs = (QW + SSM_W) // N_DEV
    chips = N_DEV // 2

    def body(at_ref, yn_ref, dh_ref, o_ref, ob_ref):
        i = pl.program_id(0)

        @pl.when(i < chips // 2)
        def _():
            _by_core(_dot_tn(at_ref[...], dh_ref[...]), o_ref, ob_ref, core)

        @pl.when(i >= chips // 2)
        def _():
            _by_core(_dot_tn(yn_ref[...], dh_ref[...]), o_ref, ob_ref, core)

    spec = pl.BlockSpec((1, rs, D), lambda i: (i, 0, 0))
    return pl.pallas_call(
        body, name="dwout", grid=(chips,),
        in_specs=[pl.BlockSpec((T, 2 * rs), lambda i: (0, jnp.minimum(i, chips // 2 - 1))),
                  pl.BlockSpec((T, 2 * rs), lambda i: (0, jnp.maximum(i - chips // 2, 0))), _resident((T, D))],
        out_specs=[spec, spec],
        out_shape=[jax.ShapeDtypeStruct((chips, rs, D), F32), jax.ShapeDtypeStruct((chips, rs, D), BF16)],
        compiler_params=_params(("arbitrary",)),
    )(attn, yn, dh2)


def _rope_swap(t, lo_half):
    return jnp.where(lo_half, pltpu.roll(t, 96, 1), pltpu.roll(t, 32, 1))


def _to_query_heads(t):
    return jnp.concatenate([t[:, HD * (h // G_PER):HD * (h // G_PER + 1)] for h in range(NQ)], axis=1)


def _from_query_heads(t):
    def kv_sum(kv):
        parts = [t[:, HD * (G_PER * kv + g):HD * (G_PER * kv + g + 1)] for g in range(G_PER)]
        return (parts[0] + parts[1]) + (parts[2] + parts[3])

    return jnp.concatenate([kv_sum(kv) for kv in range(NKV)], axis=1)


def _inproj_fwd(x1, gpre, win, cos, sin_s, duties=()):
    tm = 256

    def body(x_ref, g_ref, w_ref, cos_ref, sin_ref, n_ref, q_ref, kx_ref, vx_ref, xbc_ref, z_ref, dt_ref):
        xv = x_ref[...]
        n = (xv * _rs(xv) * g_ref[...]).astype(BF16)
        n_ref[...] = n
        by_dev = _dot_nt(n, w_ref[...])
        proj = jnp.concatenate([by_dev[:, ISW * d:ISW * d + ISR] for d in range(N_DEV)], axis=1)
        cs = cos_ref[...]
        sn = sin_ref[...]
        lo_half = (lax.broadcasted_iota(jnp.int32, (1, 128), 1) % HD) < (HD // 2)

        def rope(t):
            return t * cs + _rope_swap(t, lo_half) * sn

        for j in range(QW // 128):
            t = proj[:, 128 * j:128 * j + 128]
            q_ref[:, 128 * j:128 * j + 128] = (rope(t) * (HD ** -0.5)).astype(BF16)
        k = jnp.concatenate([rope(proj[:, QW + 128 * j:QW + 128 * j + 128]) for j in range(KVW // 128)], axis=1)
        v = proj[:, QW + KVW:QW + 2 * KVW]
        kx_ref[...] = _to_query_heads(k).astype(BF16)
        vx_ref[...] = _to_query_heads(v).astype(BF16)
        c0 = QW + 2 * KVW
        xbc_ref[...] = proj[:, c0:c0 + CONV_C]
        z_ref[...] = proj[:, c0 + CONV_C:c0 + CONV_C + SSM_W]
        dt_ref[...] = proj[:, c0 + CONV_C + SSM_W:IN_COLS]

    return _call(
        body, name="inproj_fwd", grid=(T // tm,),
        in_specs=[_rows(tm, D), _const((1, D)), _resident((INP, D)), _rows(tm, 128), _rows(tm, 128)],
        out_specs=[_rows(tm, D), _rows(tm, QW), _rows(tm, QW), _rows(tm, QW), _rows(tm, CONV_C), _rows(tm, SSM_W), _rows(tm, SSM_H)],
        out_shape=[jax.ShapeDtypeStruct((T, D), BF16), jax.ShapeDtypeStruct((T, QW), BF16), jax.ShapeDtypeStruct((T, QW), BF16),
                   jax.ShapeDtypeStruct((T, QW), BF16), jax.ShapeDtypeStruct((T, CONV_C), F32), jax.ShapeDtypeStruct((T, SSM_W), F32),
                   jax.ShapeDtypeStruct((T, SSM_H), F32)],
        args=[x1, gpre, win, cos, sin_s], sem=("arbitrary",), duties=duties)


def _inproj_bwd(dres, dq, dkx, dvx, dxbc, dz, ddt, x1, gpre, win, cos, sin_s):
    tm = 256

    def body(dres_ref, dq_ref, dkx_ref, dvx_ref, dxbc_ref, dz_ref, ddt_ref, x_ref, g_ref, w_ref, cos_ref, sin_ref,
             dx_ref, dps_ref, dg_ref, dp_ref):
        @pl.when(pl.program_id(0) == 0)
        def _():
            dg_ref[...] = jnp.zeros_like(dg_ref)

        cs = cos_ref[...]
        sn = sin_ref[...]
        lo_half = (lax.broadcasted_iota(jnp.int32, (1, 128), 1) % HD) < (HD // 2)

        def rope_t(t):
            return t * cs - _rope_swap(t, lo_half) * sn

        for j in range(QW // 128):
            dp_ref[:, 128 * j:128 * j + 128] = rope_t(dq_ref[:, 128 * j:128 * j + 128] * (HD ** -0.5)).astype(BF16)
        dk = _from_query_heads(dkx_ref[...])
        dv = _from_query_heads(dvx_ref[...])
        for j in range(KVW // 128):
            dp_ref[:, QW + 128 * j:QW + 128 * j + 128] = rope_t(dk[:, 128 * j:128 * j + 128]).astype(BF16)
        dp_ref[:, QW + KVW:QW + 2 * KVW] = dv.astype(BF16)
        c0 = QW + 2 * KVW
        dp_ref[:, c0:c0 + CONV_C] = dxbc_ref[...].astype(BF16)
        dp_ref[:, c0 + CONV_C:c0 + CONV_C + SSM_W] = dz_ref[...].astype(BF16)
        dp_ref[:, c0 + CONV_C + SSM_W:INP] = ddt_ref[...].astype(BF16)
        pieces = [dp_ref[:, ISR * d:ISR * (d + 1)] for d in range(N_DEV)]
        zw = jnp.zeros((tm, ISW - ISR), BF16)
        zg = jnp.zeros((tm, ISG - ISR), BF16)
        dn = _dot(jnp.concatenate([t for p in pieces for t in (p, zw)], axis=1), w_ref[...])
        for d in range(N_DEV):
            dps_ref[:, ISG * d:ISG * (d + 1)] = jnp.concatenate([pieces[d], zg], axis=1)
        xv = x_ref[...]
        r = _rs(xv)
        xn = xv * r
        dg_ref[...] += jnp.sum(dn * xn, axis=0, keepdims=True)
        gdn = dn * g_ref[...]
        dx_ref[...] = dres_ref[...] + r * (gdn - xn * jnp.mean(gdn * xn, axis=-1, keepdims=True))

    return pl.pallas_call(
        body, name="inproj_bwd", grid=(T // tm,),
        in_specs=[_rows(tm, D), _rows(tm, QW), _rows(tm, QW), _rows(tm, QW), _rows(tm, CONV_C), _rows(tm, SSM_W), _rows(tm, 128),
                  _rows(tm, D), _const((1, D)), _resident((INP, D)), _rows(tm, 128), _rows(tm, 128)],
        out_specs=[_rows(tm, D), _rows(tm, N_DEV * ISG), _const((1, D))],
        out_shape=[jax.ShapeDtypeStruct((T, D), F32), jax.ShapeDtypeStruct((T, N_DEV * ISG), BF16), jax.ShapeDtypeStruct((1, D), F32)],
        scratch_shapes=[pltpu.VMEM((tm, INP), BF16)],
        compiler_params=_params(("arbitrary",)),
    )(dres, dq, dkx, dvx, dxbc, dz, ddt, x1, gpre, win, cos, sin_s)


def _outproj_fwd(x1, attn, yn, wout, gpost):
    tm = 512

    def body(x_ref, at_ref, yn_ref, w_ref, g_ref, xo_ref, h_ref):
        h = _dot(at_ref[...], w_ref[0:QW, :]) + _dot(yn_ref[...], w_ref[QW:QW + SSM_W, :])
        h_ref[...] = h
        xo_ref[...] = x_ref[...] + h * _rs(h) * g_ref[...]

    return pl.pallas_call(
        body, name="outproj_fwd", grid=(T // tm,),
        in_specs=[_rows(tm, D), _rows(tm, QW), _rows(tm, SSM_W), _resident((QW + SSM_W, D)), _const((1, D))],
        out_specs=[_rows(tm, D), _rows(tm, D)],
        out_shape=[jax.ShapeDtypeStruct((T, D), F32), jax.ShapeDtypeStruct((T, D), F32)],
        compiler_params=_params(("parallel",)),
    )(x1, attn, yn, wout, gpost)


def _outproj_bwd(dx2, h2, gpost, wout, duties=()):
    tm = 512

    def body(dy_ref, h_ref, g_ref, w_ref, dh_ref, dm_ref, dg_ref):
        @pl.when(pl.program_id(0) == 0)
        def _():
            dg_ref[...] = jnp.zeros_like(dg_ref)

        dy = dy_ref[...]
        h = h_ref[...]
        r = _rs(h)
        hn = h * r
        dg_ref[...] += jnp.sum(dy * hn, axis=0, keepdims=True)
        gdy = dy * g_ref[...]
        dh = (r * (gdy - hn * jnp.mean(gdy * hn, axis=-1, keepdims=True))).astype(BF16)
        dh_ref[...] = dh
        dm_ref[...] = _dot_nt(dh, w_ref[...])

    return _call(
        body, name="outproj_bwd", grid=(T // tm,),
        in_specs=[_rows(tm, D), _rows(tm, D), _const((1, D)), _resident((QW + SSM_W, D))],
        out_specs=[_rows(tm, D), _rows(tm, QW + SSM_W), _const((1, D))],
        out_shape=[jax.ShapeDtypeStruct((T, D), BF16), jax.ShapeDtypeStruct((T, QW + SSM_W), F32), jax.ShapeDtypeStruct((1, D), F32)],
        args=[dx2, h2, gpost, wout], sem=("arbitrary",), duties=duties)


G_PER = NQ // NKV
WK = G_PER * HD


def _attn_fwd(q, kx, vx, bias, duties=()):
    def body(q_ref, kx_ref, vx_ref, bias_ref, o_ref, lse_ref):
        lane = lax.broadcasted_iota(jnp.int32, (1, WK), 1)
        lse_ref[...] = jnp.zeros_like(lse_ref)
        for i in range(NAB):
            n = (i + 1) * AB
            rows = slice(i * AB, n)
            qi = q_ref[rows, :]
            kxi = kx_ref[0:n, :]
            vxi = vx_ref[0:n, :]
            bb = bias_ref[:, (NAB - 1 - i) * AB:]
            o_acc = jnp.zeros((AB, WK), F32)
            for g in range(G_PER):
                mg = (lane // HD) == g
                s = _dot_nt(jnp.where(mg, qi, jnp.zeros_like(qi)), kxi) + bb
                m = jnp.max(s, axis=1, keepdims=True)
                p = jnp.exp(s - m)
                l = jnp.sum(p, axis=1, keepdims=True)
                o_acc = jnp.where(mg, _dot(p.astype(BF16), vxi) / l, o_acc)
                lse_ref[rows, g:g + 1] = m + jnp.log(l)
            o_ref[rows, :] = o_acc.astype(BF16)

    col = lambda kv: (0, kv)
    return _call(
        body, name="attn_fwd", grid=(NKV,),
        in_specs=[pl.BlockSpec((T, WK), col), pl.BlockSpec((T, WK), col), pl.BlockSpec((T, WK), col), _const((AB, T))],
        out_specs=[pl.BlockSpec((T, WK), col), pl.BlockSpec((T, 128), col)],
        out_shape=[jax.ShapeDtypeStruct((T, QW), BF16), jax.ShapeDtypeStruct((T, NKV * 128), F32)],
        args=[q, kx, vx, bias], sem=("arbitrary",), duties=duties)


def _attn_bwd(q, kx, vx, o, dmix, lse, bias, duties=()):
    def body(q_ref, kx_ref, vx_ref, o_ref, do_ref, lse_ref, bias_ref, dq_ref, dkx_ref, dvx_ref):
        lane = lax.broadcasted_iota(jnp.int32, (1, WK), 1)
        dkx_ref[...] = jnp.zeros_like(dkx_ref)
        dvx_ref[...] = jnp.zeros_like(dvx_ref)
        for i in range(NAB):
            n = (i + 1) * AB
            rows = slice(i * AB, n)
            qi = q_ref[rows, :]
            dof = do_ref[rows, :]
            doi = dof.astype(BF16)
            prod = dof * o_ref[rows, :].astype(F32)
            kxi = kx_ref[0:n, :]
            vxi = vx_ref[0:n, :]
            bb = bias_ref[:, (NAB - 1 - i) * AB:]
            dq_acc = jnp.zeros((AB, WK), F32)
            for g in range(G_PER):
                mg = (lane // HD) == g
                qm = jnp.where(mg, qi, jnp.zeros_like(qi))
                dom = jnp.where(mg, doi, jnp.zeros_like(doi))
                delta = jnp.sum(jnp.where(mg, prod, 0.0), axis=1, keepdims=True)
                p = jnp.exp(_dot_nt(qm, kxi) + bb - lse_ref[rows, g:g + 1])
                ds = (p * (_dot_nt(dom, vxi) - delta)).astype(BF16)
                dvx_ref[0:n, :] += _dot_tn(p.astype(BF16), dom)
                dkx_ref[0:n, :] += _dot_tn(ds, qm)
                dq_acc = jnp.where(mg, _dot(ds, kxi), dq_acc)
            dq_ref[rows, :] = dq_acc

    col = lambda kv: (0, kv)
    return _call(
        body, name="attn_bwd", grid=(NKV,),
        in_specs=[pl.BlockSpec((T, WK), col), pl.BlockSpec((T, WK), col), pl.BlockSpec((T, WK), col), pl.BlockSpec((T, WK), col),
                  pl.BlockSpec((T, WK), col), pl.BlockSpec((T, 128), col), _const((AB, T))],
        out_specs=[pl.BlockSpec((T, WK), col), pl.BlockSpec((T, WK), col), pl.BlockSpec((T, WK), col)],
        out_shape=[jax.ShapeDtypeStruct((T, QW), F32)] * 3,
        args=[q, kx, vx, o, dmix, lse, bias], sem=("arbitrary",), duties=duties)


def _softplus(x):
    return jnp.maximum(x, 0.0) + jnp.log1p(jnp.exp(-jnp.abs(x)))


def _causal_conv(u, zs, cw_ref, cb_ref):
    zs[8:, :] = u
    sh1, sh2, sh3 = (zs[8 - m:8 - m + L, :] for m in (1, 2, 3))
    return cb_ref[...] + cw_ref[3:4, :] * u + cw_ref[2:3, :] * sh1 + cw_ref[1:2, :] * sh2 + cw_ref[0:1, :] * sh3


def _ssd_chunk_common(xc, dtr, dtb_ref, alx_ref, e_ref):
    sg = _sigmoid(xc)
    act = xc * sg
    pre = dtr + dtb_ref[...]
    dt_x = _dot_hi(_softplus(pre), e_ref[...])
    a_x = -jnp.exp(alx_ref[...])
    ri = lax.broadcasted_iota(jnp.int32, (L, L), 0)
    ci = lax.broadcasted_iota(jnp.int32, (L, L), 1)
    tri = ri >= ci
    acs_x = _dot_hi(tri, dt_x * a_x, a_is_01=True)
    return dict(sg=sg, act=act, pre=pre, dt_x=dt_x, a_x=a_x, tri=tri, acs_x=acs_x)


def _decay(acs_x, acs_t, h, tri):
    col = acs_x[:, HD * h:HD * h + 1]
    row = acs_t[HD * h:HD * h + 1, :]
    return jnp.exp(jnp.where(tri, col - row, NEG))


def _ssd_fwd(xbc, z, dtr, convw, convb, dtb, alx, dskx, ssmn, e, duties=()):
    def body(u_ref, z_ref, dtr_ref, cw_ref, cb_ref, dtb_ref, alx_ref, dsk_ref, sn_ref, e_ref,
             yn_ref, y_ref, hs_ref, xc_ref, zs, hst):
        @pl.when(pl.program_id(0) == 0)
        def _():
            zs[0:8, :] = jnp.zeros((8, CONV_C), F32)
            hst[...] = jnp.zeros_like(hst)

        u = u_ref[...]
        xc = _causal_conv(u, zs, cw_ref, cb_ref)
        xc_ref[...] = xc
        zs[0:8, :] = u[L - 8:, :]
        cm = _ssd_chunk_common(xc, dtr_ref[...], dtb_ref, alx_ref, e_ref)
        act, dt_x, acs_x, tri = cm["act"], cm["dt_x"], cm["acs_x"], cm["tri"]
        xs = act[:, :SSM_W]
        acs_l = acs_x[L - 1:L, :]
        lam_x = jnp.exp(acs_x)
        w_x = jnp.exp(acs_l - acs_x)
        gam_x = jnp.exp(acs_l)
        acs_t = acs_x.T
        xd = xs * dt_x
        xb = xd.astype(BF16)
        xw = (xd * w_x).astype(BF16)
        lo = lax.broadcasted_iota(jnp.int32, (1, 128), 1) < HD
        hs_ref[0] = hst[...]
        pieces = []
        for grp in range(2):
            bb = act[:, SSM_W + SSM_N * grp:SSM_W + SSM_N * (grp + 1)].astype(BF16)
            cb_ = act[:, SSM_W + 2 * SSM_N + SSM_N * grp:SSM_W + 2 * SSM_N + SSM_N * (grp + 1)].astype(BF16)
            cbm = _dot_nt(cb_, bb)
            for jj in range(4):
                j = 4 * grp + jj
                sl = slice(128 * j, 128 * j + 128)
                m0 = (cbm * _decay(acs_x, acs_t, 2 * j, tri)).astype(BF16)
                m1 = (cbm * _decay(acs_x, acs_t, 2 * j + 1, tri)).astype(BF16)
                x2 = xb[:, sl]
                ydiag = jnp.where(lo, _dot(m0, x2), _dot(m1, x2))
                hprev = hst[j]
                yoff = lam_x[:, sl] * _dot(cb_, hprev.astype(BF16))
                pieces.append(ydiag + yoff)
                hst[j] = gam_x[:, sl] * hprev + _dot_tn(bb, xw[:, sl])
        y = jnp.concatenate(pieces, axis=1) + dsk_ref[...] * xs
        y_ref[...] = y
        zv = z_ref[...]
        yz = y * (zv * _sigmoid(zv))
        half = SSM_W // 2
        yn = jnp.concatenate([yz[:, :half] * _rs(yz[:, :half]), yz[:, half:] * _rs(yz[:, half:])], axis=1)
        yn_ref[...] = (yn * sn_ref[...]).astype(BF16)

    return _call(
        body, name="ssd_fwd", grid=(NCH,),
        in_specs=[_rows(L, CONV_C), _rows(L, SSM_W), _rows(L, SSM_H), _const((4, CONV_C)), _const((1, CONV_C)), _const((1, SSM_H)),
                  _const((1, SSM_W)), _const((1, SSM_W)), _const((1, SSM_W)), _const((SSM_H, SSM_W))],
        out_specs=[_rows(L, SSM_W), _rows(L, SSM_W), pl.BlockSpec((1, 8, SSM_N, 128), lambda c: (c, 0, 0, 0)), _rows(L, CONV_C)],
        out_shape=[jax.ShapeDtypeStruct((T, SSM_W), BF16), jax.ShapeDtypeStruct((T, SSM_W), F32),
                   jax.ShapeDtypeStruct((NCH, 8, SSM_N, 128), F32), jax.ShapeDtypeStruct((T, CONV_C), F32)],
        scratch=[pltpu.VMEM((8 + L, CONV_C), F32), pltpu.VMEM((8, SSM_N, 128), F32)],
        args=[xbc, z, dtr, convw, convb, dtb, alx, dskx, ssmn, e], sem=("arbitrary",), duties=duties)


def _ssd_bwd(dmix, xbc, xconv, z, dtr, y, hs, convw, dtb, alx, dskx, ssmn, e, e1, duties=()):
    rev = lambda i: (NCH - 1 - i, 0)

    def body(dyn_ref, u_ref, xc_ref, z_ref, dtr_ref, y_ref, hs_ref, cw_ref, dtb_ref, alx_ref, dsk_ref, sn_ref, e_ref, e1_ref,
             dxbc_ref, dz_ref, ddt_ref, dcw_ref, dcb_ref, dsn_ref, dpar_ref, dh, zd, colbuf):
        step = pl.program_id(0)

        @pl.when(step == 0)
        def _():
            for r in (dh, dcw_ref, dcb_ref, dsn_ref, dpar_ref):
                r[...] = jnp.zeros_like(r)
            zd[L:, :] = jnp.zeros((8, CONV_C), F32)

        u = u_ref[...]
        xc = xc_ref[...]
        cm = _ssd_chunk_common(xc, dtr_ref[...], dtb_ref, alx_ref, e_ref)
        sg, act, pre, dt_x, a_x, tri, acs_x = (cm[k] for k in ("sg", "act", "pre", "dt_x", "a_x", "tri", "acs_x"))
        xs = act[:, :SSM_W]
        acs_l = acs_x[L - 1:L, :]
        lam_x = jnp.exp(acs_x)
        w_x = jnp.exp(acs_l - acs_x)
        gam_x = jnp.exp(acs_l)
        acs_t = acs_x.T
        xd = xs * dt_x
        xb = xd.astype(BF16)
        xdw = xd * w_x
        xw = xdw.astype(BF16)
        lo = lax.broadcasted_iota(jnp.int32, (1, 128), 1) < HD
        row8 = lax.broadcasted_iota(jnp.int32, (8, 1), 0)

        dyn = dyn_ref[...]
        yv = y_ref[...]
        zv = z_ref[...]
        sz = _sigmoid(zv)
        siluz = zv * sz
        yz = yv * siluz
        half = SSM_W // 2
        gy = dyn * sn_ref[...]
        dyz_parts, yzn_parts = [], []
        for hf in range(2):
            part = yz[:, hf * half:(hf + 1) * half]
            r = _rs(part)
            pn = part * r
            gp = gy[:, hf * half:(hf + 1) * half]
            dyz_parts.append(r * (gp - pn * jnp.mean(gp * pn, axis=-1, keepdims=True)))
            yzn_parts.append(pn)
        dyz = jnp.concatenate(dyz_parts, axis=1)
        dsn_ref[...] += jnp.sum(dyn * jnp.concatenate(yzn_parts, axis=1), axis=0, keepdims=True)
        dy = dyz * siluz
        dz_ref[...] = dyz * yv * _dsilu(zv, sz)

        colbuf[...] = jnp.zeros_like(colbuf)
        dx_pieces, dacs_pieces, dacsl_pieces, db_pieces, dc_pieces = [], [], [], [], []
        for grp in range(2):
            bb = act[:, SSM_W + SSM_N * grp:SSM_W + SSM_N * (grp + 1)].astype(BF16)
            cb_ = act[:, SSM_W + 2 * SSM_N + SSM_N * grp:SSM_W + 2 * SSM_N + SSM_N * (grp + 1)].astype(BF16)
            cbm = _dot_nt(cb_, bb)
            dcbm = jnp.zeros((L, L), F32)
            dc_g = jnp.zeros((L, SSM_N), F32)
            db_g = jnp.zeros((L, SSM_N), F32)
            for jj in range(4):
                j = 4 * grp + jj
                sl = slice(128 * j, 128 * j + 128)
                dy2 = dy[:, sl]
                dy2b = dy2.astype(BF16)
                d0 = _decay(acs_x, acs_t, 2 * j, tri)
                d1 = _decay(acs_x, acs_t, 2 * j + 1, tri)
                m0 = cbm * d0
                m1 = cbm * d1
                x2 = xb[:, sl]
                hprev = hs_ref[0, j]
                hprevb = hprev.astype(BF16)
                dhn = dh[j]
                dhnb = dhn.astype(BF16)
                g2 = _dot(bb, dhnb)
                dx_pieces.append(jnp.where(lo, _dot_tn(m0.astype(BF16), dy2b), _dot_tn(m1.astype(BF16), dy2b)) + w_x[:, sl] * g2)
                zero = jnp.zeros_like(dy2b)
                dm0 = _dot_nt(jnp.where(lo, dy2b, zero), x2)
                dm1 = _dot_nt(jnp.where(lo, zero, dy2b), x2)
                dcbm = dcbm + dm0 * d0 + dm1 * d1
                e0 = dm0 * m0
                e1v = dm1 * m1
                colbuf[:, 2 * j:2 * j + 1] = jnp.sum(e0, axis=1, keepdims=True) - jnp.sum(e0.T, axis=1, keepdims=True)
                colbuf[:, 2 * j + 1:2 * j + 2] = jnp.sum(e1v, axis=1, keepdims=True) - jnp.sum(e1v.T, axis=1, keepdims=True)
                yoff = lam_x[:, sl] * _dot(cb_, hprevb)
                gxw = g2 * xdw[:, sl]
                dacs_pieces.append(dy2 * yoff - gxw)
                dacsl_pieces.append(jnp.sum(gxw, axis=0, keepdims=True) + gam_x[:, sl] * jnp.sum(dhn * hprev, axis=0, keepdims=True))
                dyl = (dy2 * lam_x[:, sl]).astype(BF16)
                dc_g = dc_g + _dot_nt(dyl, hprevb)
                db_g = db_g + _dot_nt(xw[:, sl], dhnb)
                dh[j] = gam_x[:, sl] * dhn + _dot_tn(cb_, dyl)
            dcbb = dcbm.astype(BF16)
            dc_pieces.append(dc_g + _dot(dcbb, bb))
            db_pieces.append(db_g + _dot_tn(dcbb, cb_))

        dxd = jnp.concatenate(dx_pieces, axis=1)
        rowi = lax.broadcasted_iota(jnp.int32, (L, 1), 0)
        dacs_x = (jnp.concatenate(dacs_pieces, axis=1) + _dot_hi(colbuf[...], e1_ref[...])
                  + jnp.where(rowi == L - 1, jnp.concatenate(dacsl_pieces, axis=1), 0.0))
        upper = lax.broadcasted_iota(jnp.int32, (L, L), 0) <= lax.broadcasted_iota(jnp.int32, (L, L), 1)
        dadt_x = _dot_hi(upper, dacs_x, a_is_01=True)
        ddt_x = dxd * xs + dadt_x * a_x
        ddtr = _dot_nt_hi(ddt_x, e_ref[...]) * _sigmoid(pre)
        ddt_ref[...] = jnp.zeros_like(ddt_ref)
        ddt_ref[:, 0:SSM_H] = ddtr
        dalx =jnp.sum(dadt_x * dt_x, axis=0, keepdims=True) * a_x
        ddskx = jnp.sum(dy * xs, axis=0, keepdims=True)
        par_x = jnp.where(row8 == 1, dalx, 0.0) + jnp.where(row8 == 2, ddskx, 0.0)
        dpar_ref[...] += _dot_nt_hi(par_x, e_ref[...]) + jnp.where(row8 == 0, jnp.sum(ddtr, axis=0, keepdims=True), 0.0)

        dxs = dxd * dt_x + dsk_ref[...] * dy
        dact = jnp.concatenate([dxs] + db_pieces + dc_pieces, axis=1)
        du = dact * _dsilu(xc, sg)
        dcb_ref[...] += jnp.sum(du, axis=0, keepdims=True)
        zd[0:L, :] = du
        f1, f2, f3 = (zd[m:m + L, :] for m in (1, 2, 3))
        dxbc_ref[...] = cw_ref[3:4, :] * du + cw_ref[2:3, :] * f1 + cw_ref[1:2, :] * f2 + cw_ref[0:1, :] * f3
        dcw = jnp.zeros((8, CONV_C), F32)
        for k, shifted in enumerate((f3, f2, f1, du)):
            dcw = dcw + jnp.where(row8 == k, jnp.sum(shifted * u, axis=0, keepdims=True), 0.0)
        dcw_ref[...] += dcw
        zd[L:, :] = du[:8, :]

    return _call(
        body, name="ssd_bwd", grid=(NCH,),
        in_specs=[pl.BlockSpec((L, SSM_W), lambda i: (NCH - 1 - i, 1)), pl.BlockSpec((L, CONV_C), rev), pl.BlockSpec((L, CONV_C), rev),
                  pl.BlockSpec((L, SSM_W), rev), pl.BlockSpec((L, SSM_H), rev), pl.BlockSpec((L, SSM_W), rev),
                  pl.BlockSpec((1, 8, SSM_N, 128), lambda i: (NCH - 1 - i, 0, 0, 0)),
                  _const((4, CONV_C)), _const((1, SSM_H)), _const((1, SSM_W)), _const((1, SSM_W)), _const((1, SSM_W)),
                  _const((SSM_H, SSM_W)), _const((128, SSM_W))],
        out_specs=[pl.BlockSpec((L, CONV_C), rev), pl.BlockSpec((L, SSM_W), rev), pl.BlockSpec((L, 128), rev),
                   _const((8, CONV_C)), _const((1, CONV_C)), _const((1, SSM_W)), _const((8, SSM_H))],
        out_shape=[jax.ShapeDtypeStruct((T, CONV_C), F32), jax.ShapeDtypeStruct((T, SSM_W), F32), jax.ShapeDtypeStruct((T, 128), F32),
                   jax.ShapeDtypeStruct((8, CONV_C), F32), jax.ShapeDtypeStruct((1, CONV_C), F32), jax.ShapeDtypeStruct((1, SSM_W), F32),
                   jax.ShapeDtypeStruct((8, SSM_H), F32)],
        scratch=[pltpu.VMEM((8, SSM_N, 128), F32), pltpu.VMEM((L + 8, CONV_C), F32), pltpu.VMEM((L, 128), F32)],
        args=[dmix, xbc, xconv, z, dtr, y, hs, convw, dtb, alx, dskx, ssmn, e, e1], sem=("arbitrary",), duties=duties)


def _adam_math(w, g, m, v):
    m = ADAM_B1 * m + (1.0 - ADAM_B1) * g
    v = ADAM_B2 * v + (1.0 - ADAM_B2) * (g * g)
    m_hat = m / (1.0 - ADAM_B1 ** ADAM_STEP)
    v_hat = v / (1.0 - ADAM_B2 ** ADAM_STEP)
    delta = -ADAM_LR * (m_hat / (jnp.sqrt(v_hat) + ADAM_EPS) + ADAM_WD * w)
    return delta, m, v


PACK_W = CONV_C


def _adamw_small(ws, ms, vs, total, conv_g, after=()):
    k = len(ws)

    def body(*refs):
        ins, outs = refs[:3 * k + 2], refs[3 * k + 2 + len(after):]
        total_ref, conv_ref = ins[3 * k], ins[3 * k + 1]
        for a in range(k):
            w_ref, m_ref, v_ref = ins[3 * a:3 * a + 3]
            g = conv_ref[...] if a == k - 1 else total_ref[a:a + 1, 0:w_ref.shape[1]]
            delta, nm, nv = _adam_math(w_ref[...], g, m_ref[...], v_ref[...])
            for ref, val in zip(outs[4 * a:4 * a + 4], (g, delta, nm, nv)):
                ref[...] = val

    vm = pl.BlockSpec(memory_space=pltpu.VMEM)
    args = [x for a in range(k) for x in (ws[a], ms[a], vs[a])] + [total, conv_g]
    res = pl.pallas_call(
        body, name="adamw_small", in_specs=[vm] * len(args) + [ANY] * len(after), out_specs=[vm] * (4 * k),
        out_shape=[jax.ShapeDtypeStruct(ws[a].shape, F32) for a in range(k) for _ in range(4)],
    )(*args, *after)
    return [tuple(res[4 * a:4 * a + 4]) for a in range(k)]


COL_TILE = 512


def _adamw_sharded(ws, ms, vs, chip_sums, from_chips, other_chips, name, after=()):
    k = len(ws)
    rows, cols = ws[0].shape
    prow = chip_sums[0].shape[0]
    assert cols % COL_TILE == 0 and prow >= rows and all(a.shape == ws[0].shape for a in ws)

    def body(ids_ref, *refs):
        ins, outs = refs[:7 * k], refs[7 * k + len(after):]
        for a in range(k):
            w_ref, m_ref, v_ref, s_ref, r1_ref, r2_ref, r3_ref = ins[7 * a:7 * a + 7]
            g = s_ref[...]
            for r in (r1_ref, r2_ref, r3_ref):
                g = g + r[0].astype(F32)
            g = g[:rows]
            delta, nm, nv = _adam_math(w_ref[...], g, m_ref[...], v_ref[...])
            for ref, val in zip(outs[4 * a:4 * a + 4], (g, delta, nm, nv)):
                ref[...] = val

    spec = pl.BlockSpec((rows, COL_TILE), lambda i, ids: (0, i))
    part = lambda j: pl.BlockSpec((1, prow, COL_TILE), lambda i, ids: (ids[j], 0, i))
    one = [spec, spec, spec, pl.BlockSpec((prow, COL_TILE), lambda i, ids: (0, i)), part(0), part(1), part(2)]
    args = [x for a in range(k) for x in (ws[a], ms[a], vs[a], chip_sums[a], from_chips[a], from_chips[a], from_chips[a])]
    res = pl.pallas_call(
        body, name=name,
        grid_spec=pltpu.PrefetchScalarGridSpec(
            num_scalar_prefetch=1, grid=(cols // COL_TILE,),
            in_specs=one * k + [ANY] * len(after), out_specs=[spec] * (4 * k)),
        out_shape=[jax.ShapeDtypeStruct((rows, cols), F32)] * (4 * k),
        compiler_params=_params(("parallel",)),
    )(other_chips, *args, *after)
    return [tuple(res[4 * a:4 * a + 4]) for a in range(k)]


def _chip_sum(mines, recvs, name):
    k = len(mines)
    rows, cols = mines[0].shape[1:]

    def body(*refs):
        own_chip = pl.program_id(0) == 2 * lax.axis_index("x") + lax.axis_index("y")
        for a in range(k):
            a_ref, b_ref = refs[2 * a:2 * a + 2]
            s_ref, sb_ref = refs[2 * k + 2 * a:2 * k + 2 * a + 2]
            s = a_ref[0] + b_ref[0].astype(F32)
            sb_ref[0] = s.astype(BF16)

            @pl.when(own_chip)
            def _(s_ref=s_ref, s=s):
                s_ref[...] = s

    by_chip = pl.BlockSpec((1, rows, cols), lambda c: (c, 0, 0))
    res = pl.pallas_call(
        body, name=name, grid=(N_DEV // 2,),
        in_specs=[by_chip, by_chip] * k, out_specs=[_const((rows, cols)), by_chip] * k,
        out_shape=[jax.ShapeDtypeStruct((rows, cols), F32), jax.ShapeDtypeStruct((N_DEV // 2, rows, cols), BF16)] * k,
        compiler_params=_params(("arbitrary",)),
    )(*[x for pair in zip(mines, recvs) for x in pair])
    return [tuple(res[2 * a:2 * a + 2]) for a in range(k)]


def _all_reduce_small(v, after=()):
    rows = v.shape[0]

    def body(v_ref, *rest):
        out_ref, gath, send_sems, recv_sems = rest[len(after):]
        x, y, c = _place()
        me, sibling = (x, y, c), (x, y, 1 - c)
        chips = [(1 - x, y), (x, 1 - y), (1 - x, 1 - y)]

        def blk(px, py, pc):
            return gath.at[pl.ds((4 * px + 2 * py + pc) * rows, rows), :]

        def copy(k, block, to, src=None):
            return pltpu.make_async_remote_copy(src_ref=blk(*block) if src is None else src, dst_ref=blk(*block),
                                                send_sem=send_sems.at[k], recv_sem=recv_sems.at[k], device_id=to, device_id_type=MESH)

        gath[pl.ds((4 * x + 2 * y + c) * rows, rows), :] = v_ref[...]
        first = [copy(0, me, sibling, src=v_ref)] + [copy(1 + j, me, (*chip, c), src=v_ref) for j, chip in enumerate(chips)]
        for cp in first:
            cp.start()
        passed = [copy(4 + j, (*chip, c), sibling) for j, chip in enumerate(chips)]
        for j, chip in enumerate(chips):
            copy(1 + j, (*chip, c), me).wait_recv()
            passed[j].start()
        copy(0, sibling, me).wait_recv()
        for j, chip in enumerate(chips):
            copy(4 + j, (*chip, 1 - c), me).wait_recv()
        for cp in first + passed:
            cp.wait_send()
        acc = gath[0:rows, :]
        for d in range(1, N_DEV):
            acc = acc + gath[d * rows:(d + 1) * rows, :]
        out_ref[...] = acc

    vm = pl.BlockSpec(memory_space=pltpu.VMEM)
    return pl.pallas_call(
        body, name="all_reduce_small",
        in_specs=[vm] + [ANY] * len(after), out_specs=vm,
        out_shape=jax.ShapeDtypeStruct(v.shape, F32),
        scratch_shapes=[pltpu.VMEM((N_DEV * rows, v.shape[1]), F32), pltpu.SemaphoreType.DMA((7,)), pltpu.SemaphoreType.DMA((7,))],
    )(v, *after)


def _tables(positions, duties, name):
    inv_freq = ROPE_THETA ** (-jnp.arange(0, HD, 2, dtype=F32) / HD)
    inv = jnp.concatenate([inv_freq] * 4)[None, :]
    sign = jnp.where((jnp.arange(128) % HD) < (HD // 2), -1.0, 1.0).astype(F32)[None, :]

    def body(pos_ref, inv_ref, sign_ref, cos_ref, sin_ref, bias_ref, cos_v, sin_v, bias_v):
        ang = pos_ref[...] * inv_ref[...]
        cos_v[...] = jnp.cos(ang)
        sin_v[...] = sign_ref[...] * jnp.sin(ang)
        d = lax.broadcasted_iota(jnp.int32, (AB, T), 0) - lax.broadcasted_iota(jnp.int32, (AB, T), 1) + (T - AB)
        cnt = jnp.zeros((AB, T), F32)
        for window, dil in DILATIONS:
            cnt = cnt + ((d >= 0) & ((d & (dil - 1)) == 0) & (d <= window)).astype(F32)
        bias_v[...] = jnp.where(cnt > 0, jnp.log(jnp.maximum(cnt, 1.0)), NEG)
        for v, o in ((cos_v, cos_ref), (sin_v, sin_ref), (bias_v, bias_ref)):
            pltpu.sync_copy(v, o)

    vm = pl.BlockSpec(memory_space=pltpu.VMEM)
    shapes = [(T, 128), (T, 128), (AB, T)]
    return _call(body, name=name, grid=None, in_specs=[vm] * 3, out_specs=[ANY] * 3,
                 out_shape=[jax.ShapeDtypeStruct(s, F32) for s in shapes], scratch=[pltpu.VMEM(s, F32) for s in shapes],
                 args=[positions.reshape(T, 1).astype(F32), inv, sign], sem=None, duties=duties)


def _selectors():
    lane = jnp.arange(QW)
    e = (lane[None, :] // HD == jnp.arange(SSM_H)[:, None]).astype(F32)
    e1 = ((lane[None, :] == HD * jnp.arange(128)[:, None]) & (jnp.arange(128)[:, None] < SSM_H)).astype(F32)
    return e, e1


WEIGHTS = ['ffn1_pre_norm', 'ffn1_w_gate', 'ffn1_w_up', 'ffn1_w_down', 'ffn1_post_norm', 'mix_pre_norm', 'w_in', 'conv_w', 'conv_b',
           'dt_bias', 'a_log', 'd_skip', 'ssm_norm', 'w_out', 'mix_post_norm', 'ffn2_pre_norm', 'ffn2_w_gate', 'ffn2_w_up',
           'ffn2_w_down', 'ffn2_post_norm']
COL_SHARDED = ['ffn1_w_gate', 'ffn1_w_up', 'ffn2_w_gate', 'ffn2_w_up', 'w_in']
ROW_SHARDED = ['ffn1_w_down', 'ffn2_w_down', 'w_out']
BIG = COL_SHARDED + ROW_SHARDED
FFN_BIG = COL_SHARDED[:4] + ROW_SHARDED[:2]
SMALL = ['ffn1_pre_norm', 'ffn1_post_norm', 'mix_pre_norm', 'conv_b', 'dt_bias', 'a_log', 'd_skip', 'ssm_norm', 'mix_post_norm',
         'ffn2_pre_norm', 'ffn2_post_norm']
FFN1 = ['ffn1_w_gate', 'ffn1_w_up', 'ffn1_w_down']
FFN2 = ['ffn2_w_gate', 'ffn2_w_up', 'ffn2_w_down']


def _wire_block(name, a):
    if name in FFN_BIG:
        return jnp.pad(a.astype(BF16), ((0, FSH - FSR), (0, 0)))
    if name == "w_in":
        return jnp.pad(a.astype(BF16), ((0, ISW - ISR), (0, 0)))
    return a if name == "conv_w" else a.astype(BF16)


def _whole_from_gathered(name, a):
    if name == "conv_w":
        return jnp.transpose(a, (1, 0, 2)).reshape(a.shape[1], -1)
    return a.reshape(-1, D)


def _step(x, positions, target, small, blocks=None, whole=None):
    dist = blocks is not None
    core = "mesh" if dist else 0
    w = dict(small)
    if whole:
        w.update(whole)

    def gather(names):
        return [_gather_duty([_wire_block(n, blocks[n]) for n in names])] if dist else []

    def put(names, results):
        if dist:
            for n, r in zip(names, results[0]):
                w[n] = _whole_from_gathered(n, r)

    g, sums, red = {}, {}, {}

    def swap(names):
        return [_swap_duty([g[n][1] for n in names])] if dist else []

    def chip_sums(names, from_sibling):
        if dist:
            res = _chip_sum([g[n][0] for n in names], list(from_sibling), "chip_sum_" + names[0])
            sums.update(zip(names, res))

    def exchange(names):
        return [_exchange_duty([sums[n][1] for n in names])] if dist else []

    def reduced(names, from_chips):
        if dist:
            for n, recv in zip(names, from_chips):
                red[n] = (sums[n][0], recv)

    e, e1 = _selectors()
    alx = jnp.repeat(w["a_log"], HD, axis=1)
    dskx = jnp.repeat(w["d_skip"], HD, axis=1)

    (cos, sin_s, bias), got = _tables(positions, gather(FFN1), "gather_ffn1")
    put(FFN1, got)
    (x1, n1, a1, b1, hm1, h1), got = _ffn_fwd(x, w["ffn1_pre_norm"], w["ffn1_w_gate"], w["ffn1_w_up"], w["ffn1_w_down"],
                                              w["ffn1_post_norm"], "ffn1_fwd", gather(["w_in", "conv_w"]))
    put(["w_in", "conv_w"], got)
    (n2, q, kx, vx, xbc, z, dtr), got = _inproj_fwd(x1, w["mix_pre_norm"], w["w_in"], cos, sin_s, gather(["w_out"]))
    put(["w_out"], got)
    (attn, lse), got = _attn_fwd(q, kx, vx, bias, gather(FFN2[:2]))
    put(FFN2[:2], got)
    (yn, y, hs, xconv), got = _ssd_fwd(xbc, z, dtr, w["conv_w"], w["conv_b"], w["dt_bias"], alx, dskx, w["ssm_norm"], e, gather(FFN2[2:]))
    put(FFN2[2:], got)
    x2, h2 = _outproj_fwd(x1, attn, yn, w["w_out"], w["mix_post_norm"])
    (dx3, n3, a3, b3, hm3, h3, ss), _ = _ffn_fwd(x2, w["ffn2_pre_norm"], w["ffn2_w_gate"], w["ffn2_w_up"], w["ffn2_w_down"],
                                                 w["ffn2_post_norm"], "ffn2_fwd", target=target)

    (dx2, da3, db3, dh3, g["ffn2_pre_norm"], g["ffn2_post_norm"]), _ = _ffn_bwd(
        dx3, x2, a3, b3, h3, w["ffn2_pre_norm"], w["ffn2_post_norm"], w["ffn2_w_gate"], w["ffn2_w_up"], w["ffn2_w_down"], "ffn2_bwd")
    g["ffn2_w_down"] = _matmul_tn(hm3, dh3, "ffn2_dwd", core)[0]
    g["ffn2_w_gate"] = _matmul_tn(da3, n3, "ffn2_dwg", core)[0]
    g["ffn2_w_up"] = _matmul_tn(db3, n3, "ffn2_dwu", core)[0]

    (dh2, dmix, g["mix_post_norm"]), got = _outproj_bwd(dx2, h2, w["mix_post_norm"], w["w_out"], swap(FFN2))
    chip_sums(FFN2, got[0] if dist else None)
    g["w_out"] = _dwout(attn, yn, dh2, core)
    (dq, dkx, dvx), got = _attn_bwd(q, kx, vx, attn, dmix, lse, bias, exchange(FFN2) + swap(["w_out"]))
    if dist:
        reduced(FFN2, got[0])
        chip_sums(["w_out"], got[1])
    (dxbc, dz, ddt, dcw, g["conv_b"], g["ssm_norm"], dpar), got = _ssd_bwd(
        dmix, xbc, xconv, z, dtr, y, hs, w["conv_w"], w["dt_bias"], alx, dskx, w["ssm_norm"], e, e1, exchange(["w_out"]))
    reduced(["w_out"], got[0] if dist else None)
    g["conv_w"] = dcw[0:4]
    g["dt_bias"], g["a_log"], g["d_skip"] = dpar[0:1], dpar[1:2], dpar[2:3]
    dx1, dproj, g["mix_pre_norm"] = _inproj_bwd(dx2, dq, dkx, dvx, dxbc, dz, ddt, x1, w["mix_pre_norm"], w["w_in"], cos, sin_s)
    g["w_in"] = _matmul_tn(dproj, n2, "dwin", core)[0]

    started = {}

    def start(n):
        started[n] = _exchange_start(sums[n][1], "start_exchange_" + n, FSR if n in FFN_BIG else None)
        return [started[n]["token"]]

    after = []
    if dist:
        chip_sums(["w_in"], _comm_only(swap(["w_in"]), "swap_w_in")[0])
        after = start("w_in")
    (dx0, da1, db1, dh1, g["ffn1_pre_norm"], g["ffn1_post_norm"]), _ = _ffn_bwd(
        dx1, x, a1, b1, h1, w["ffn1_pre_norm"], w["ffn1_post_norm"], w["ffn1_w_gate"], w["ffn1_w_up"], w["ffn1_w_down"], "ffn1_bwd",
        after)
    total = None
    if dist:
        widen = lambda a: jnp.pad(a, ((0, 0), (0, PACK_W - a.shape[1])))
        pack = jnp.concatenate([widen(g[n]) for n in SMALL] + [g["conv_w"], widen(ss[:, 0:1])])
        assert pack.shape[0] % 8 == 0
        total = _all_reduce_small(pack, after)
        after = [total]
    g["ffn1_w_down"], _ = _matmul_tn(hm1, dh1, "ffn1_dwd", core, after=after)
    g["ffn1_w_gate"], got = _matmul_tn(da1, n1, "ffn1_dwg", core, duties=swap(["ffn1_w_down"]))
    if dist:
        chip_sums(["ffn1_w_down"], got[0])
        after = start("ffn1_w_down")
    g["ffn1_w_up"], got = _matmul_tn(db1, n1, "ffn1_dwu", core, after=after, duties=swap(["ffn1_w_gate"]))
    if dist:
        chip_sums(["ffn1_w_gate"], got[0])
        after = start("ffn1_w_gate")
        chip_sums(["ffn1_w_up"], _comm_only(swap(["ffn1_w_up"]), "swap_ffn1_w_up", after=after)[0])
        start("ffn1_w_up")
    return ss, dx0, g, red, {n: (sums[n][0], started[n]) for n in started}, total


def kernel(x, positions, ffn1_pre_norm, ffn1_w_gate, ffn1_w_up, ffn1_w_down, ffn1_post_norm, mix_pre_norm, w_in, conv_w, conv_b, dt_bias, a_log, d_skip, ssm_norm, w_out, mix_post_norm, ffn2_pre_norm, ffn2_w_gate, ffn2_w_up, ffn2_w_down, ffn2_post_norm, loss_target, m_ffn1_pre_norm, m_ffn1_w_gate, m_ffn1_w_up, m_ffn1_w_down, m_ffn1_post_norm, m_mix_pre_norm, m_w_in, m_conv_w, m_conv_b, m_dt_bias, m_a_log, m_d_skip, m_ssm_norm, m_w_out, m_mix_post_norm, m_ffn2_pre_norm, m_ffn2_w_gate, m_ffn2_w_up, m_ffn2_w_down, m_ffn2_post_norm, v_ffn1_pre_norm, v_ffn1_w_gate, v_ffn1_w_up, v_ffn1_w_down, v_ffn1_post_norm, v_mix_pre_norm, v_w_in, v_conv_w, v_conv_b, v_dt_bias, v_a_log, v_d_skip, v_ssm_norm, v_w_out, v_mix_post_norm, v_ffn2_pre_norm, v_ffn2_w_gate, v_ffn2_w_up, v_ffn2_w_down, v_ffn2_post_norm):
    given = dict(locals())
    drop = lambda n, a: a if n in SMALL else (a[0].T if n in COL_SHARDED else a[0])
    w = {n: drop(n, given[n]) for n in WEIGHTS}
    m = {n: drop(n, given["m_" + n]) for n in WEIGHTS}
    v = {n: drop(n, given["v_" + n]) for n in WEIGHTS}
    cx, cy, cc = _place()
    others = [2 * (1 - cx) + cy, 2 * cx + (1 - cy), 2 * (1 - cx) + (1 - cy)]

    _, grad_x, g, red, pending, total = _step(x[0], positions, loss_target[0], {n: w[n] for n in SMALL},
                                              blocks={n: w[n] for n in BIG + ["conv_w"]})
    chip_ids = jnp.stack(others).astype(jnp.int32)
    out_g, out_d, out_m, out_v = {}, {}, {}, {}

    def update(names, sums, recvs, label, after=()):
        res = _adamw_sharded([w[n] for n in names], [m[n] for n in names], [v[n] for n in names], sums, recvs, chip_ids,
                             "adamw_" + label, after)
        for n, (gn, dn, mn, vn) in zip(names, res):
            out_g[n], out_d[n], out_m[n], out_v[n] = gn, dn, mn, vn

    last_start = [pending["ffn1_w_up"][1]["token"]]
    update(FFN2, [red[n][0] for n in FFN2], [red[n][1] for n in FFN2], "ffn2", last_start)
    update(["w_out"], [red["w_out"][0]], [red["w_out"][1]], "w_out", last_start)

    n_small = len(SMALL)
    conv_g = lax.dynamic_slice_in_dim(total[n_small:n_small + 4], (4 * cx + 2 * cy + cc) * (CONV_C // N_DEV), CONV_C // N_DEV, axis=1)
    loss = 0.5 * total[n_small + 4, 0] / D
    names = SMALL + ["conv_w"]
    res = _adamw_small([w[n] for n in names], [m[n] for n in names], [v[n] for n in names], total, conv_g, last_start)
    for n, (gn, dn, mn, vn) in zip(names, res):
        out_g[n], out_d[n], out_m[n], out_v[n] = gn, dn, mn, vn
    done = [out_v[n] for n in FFN2 + ["w_out", "conv_w"]]
    update(["w_in"], [pending["w_in"][0]], [_exchange_wait(pending["w_in"][1], done, "wait_exchange_w_in")], "w_in")
    done = [out_v["w_in"]]
    update(FFN1, [pending[n][0] for n in FFN1], [_exchange_wait(pending[n][1], done, "wait_exchange_" + n) for n in FFN1], "ffn1")

    outs = [loss, grad_x[None]]
    for d in (out_g, out_d, out_m, out_v):
        outs += [d[n] if n in SMALL else (d[n].T[None] if n in COL_SHARDED else d[n][None]) for n in WEIGHTS]
    return tuple(outs)
```

```python
import functools

import jax
import jax.numpy as jnp
from jax import lax
from jax.experimental import pallas as pl
from jax.experimental.pallas import tpu as pltpu

F32 = jnp.float32
BF16 = jnp.bfloat16
MESH = pl.DeviceIdType.MESH

N_DEV = 8
T = 2048
D = 1024
FF = 2816
FSR = FF // N_DEV
FSH = 384
FFP = N_DEV * FSH
HD = 64
NQ = 16
NKV = 4
QW = NQ * HD
KVW = NKV * HD
SSM_W = 1024
SSM_H = 16
SSM_N = 128
CONV_C = SSM_W + 2 * 2 * SSM_N
IN_COLS = 4112
INP = 4224
ISR = IN_COLS // N_DEV
ISW = 528
ISG = 640
L = 128
NCH = T // L
AB = 256
NAB = T // AB
EPS = 1e-6
NEG = -1e30
ROPE_THETA = 10000.0
DILATIONS = ((128, 1), (512, 4), (2048, 16))

ADAM_LR = 0.001
ADAM_B1 = 0.9
ADAM_B2 = 0.999
ADAM_EPS = 1e-08
ADAM_WD = 0.01
ADAM_STEP = 10

VMEM_LIMIT = 58 * 1024 * 1024


def _params(sem, vmem=VMEM_LIMIT):
    return pltpu.CompilerParams(dimension_semantics=sem, vmem_limit_bytes=vmem)


def _dot(a, b):
    return jnp.dot(a, b, preferred_element_type=F32)


def _dot_nt(a, b):
    return lax.dot_general(a, b, (((1,), (1,)), ((), ())), preferred_element_type=F32)


def _dot_tn(a, b):
    return lax.dot_general(a, b, (((0,), (0,)), ((), ())), preferred_element_type=F32)


def _split3(x):
    hi = x.astype(BF16)
    r1 = x - hi.astype(F32)
    mid = r1.astype(BF16)
    lo = (r1 - mid.astype(F32)).astype(BF16)
    return hi, mid, lo


def _dot_hi(a, b, a_is_01=False):
    if a_is_01:
        sel = a.astype(BF16)
        return sum(_dot(sel, p) for p in _split3(b))
    sel = b.astype(BF16)
    return sum(_dot(p, sel) for p in _split3(a))


def _dot_nt_hi(a, b):
    sel = b.astype(BF16)
    return sum(_dot_nt(p, sel) for p in _split3(a))


def _rs(x):
    return lax.rsqrt(jnp.mean(x * x, axis=-1, keepdims=True) + EPS)


def _sigmoid(x):
    return jax.nn.sigmoid(x)


def _dsilu(x, s):
    return s * (1.0 + x * (1.0 - s))


def _resident(shape):
    nd = len(shape)
    return pl.BlockSpec(shape, lambda *_: (0,) * nd, pipeline_mode=pl.Buffered(1))


def _const(shape):
    nd = len(shape)
    return pl.BlockSpec(shape, lambda *_: (0,) * nd)


def _rows(tm, cols):
    return pl.BlockSpec((tm, cols), lambda i: (i, 0))


ANY = pl.BlockSpec(memory_space=pl.ANY)


def _place():
    return lax.axis_index("x"), lax.axis_index("y"), lax.axis_index("c")


def _gather_duty(arrays):
    n = len(arrays)
    results = [jax.ShapeDtypeStruct((N_DEV,) + a.shape, a.dtype) for a in arrays]

    def make(ins, outs, send_sems, recv_sems, local_sems):
        x, y, c = _place()
        me, sibling = (x, y, c), (x, y, 1 - c)
        chips = [(1 - x, y), (x, 1 - y), (1 - x, 1 - y)]

        def place_of(a, px, py, pc):
            return outs[a].at[4 * px + 2 * py + pc]

        def copy(a, k, block, to, src=None):
            dst = place_of(a, *block)
            return pltpu.make_async_remote_copy(src_ref=dst if src is None else src, dst_ref=dst,
                                                send_sem=send_sems.at[7 * a + k], recv_sem=recv_sems.at[7 * a + k],
                                                device_id=to, device_id_type=MESH)

        def own(a):
            return pltpu.make_async_copy(ins[a], place_of(a, *me), local_sems.at[a])

        def first(a):
            return [copy(a, 0, me, sibling, src=ins[a])] + [copy(a, 1 + j, me, (*chip, c), src=ins[a]) for j, chip in enumerate(chips)]

        def start():
            for a in range(n):
                own(a).start()
            for a in range(n):
                for cp in first(a):
                    cp.start()

        def finish():
            for j, chip in enumerate(chips):
                for a in range(n):
                    copy(a, 1 + j, (*chip, c), me).wait_recv()
                    copy(a, 4 + j, (*chip, c), sibling).start()
            for a in range(n):
                copy(a, 0, sibling, me).wait_recv()
                for j, chip in enumerate(chips):
                    copy(a, 4 + j, (*chip, 1 - c), me).wait_recv()
            for a in range(n):
                for cp in first(a) + [copy(a, 4 + j, (*chip, c), sibling) for j, chip in enumerate(chips)]:
                    cp.wait_send()
                own(a).wait()

        return start, finish

    return dict(operands=list(arrays), results=results, sems=(7 * n, 7 * n, n), make=make)


def _swap_duty(arrays):
    n = len(arrays)
    half = N_DEV // 2
    results = [jax.ShapeDtypeStruct(a.shape, a.dtype) for a in arrays]

    def make(ins, outs, send_sems, recv_sems):
        x, y, c = _place()

        def copies():
            return [pltpu.make_async_remote_copy(src_ref=ins[a].at[k], dst_ref=outs[a].at[k],
                                                 send_sem=send_sems.at[half * a + k], recv_sem=recv_sems.at[half * a + k],
                                                 device_id=(x, y, 1 - c), device_id_type=MESH)
                    for a in range(n) for k in range(half)]

        def start():
            for cp in copies():
                cp.start()

        def finish():
            for cp in copies():
                cp.wait()

        return start, finish

    return dict(operands=list(arrays), results=results, sems=(half * n, half * n), make=make)


def _exchange_duty(arrays):
    n = len(arrays)
    results = [jax.ShapeDtypeStruct(a.shape, a.dtype) for a in arrays]

    def make(ins, outs, send_sems, recv_sems):
        x, y, c = _place()
        chips = [(1 - x, y), (x, 1 - y), (1 - x, 1 - y)]
        my_chip = 2 * x + y

        def sends():
            return [pltpu.make_async_remote_copy(src_ref=ins[a].at[2 * px + py], dst_ref=outs[a].at[my_chip],
                                                 send_sem=send_sems.at[3 * a + j], recv_sem=recv_sems.at[3 * a + j],
                                                 device_id=(px, py, c), device_id_type=MESH)
                    for a in range(n) for j, (px, py) in enumerate(chips)]

        def start():
            for cp in sends():
                cp.start()

        def finish():
            for a in range(n):
                for j, (px, py) in enumerate(chips):
                    pltpu.make_async_remote_copy(src_ref=ins[a].at[my_chip], dst_ref=outs[a].at[2 * px + py],
                                                 send_sem=send_sems.at[3 * a + j], recv_sem=recv_sems.at[3 * a + j],
                                                 device_id=(px, py, c), device_id_type=MESH).wait_recv()
            for cp in sends():
                cp.wait_send()

        return start, finish

    return dict(operands=list(arrays), results=results, sems=(3 * n, 3 * n), make=make)


def _call(body, *, name, grid, in_specs, out_specs, out_shape, args, sem, scratch=(), duties=(), after=()):
    n_in, n_out, n_scr = len(in_specs), len(out_specs), len(scratch)
    sem_shapes = [pltpu.SemaphoreType.DMA((k,)) for d in duties for k in d["sems"]]

    def full(*refs):
        pos = [0]

        def take(k):
            pos[0] += k
            return refs[pos[0] - k:pos[0]]

        ins = take(n_in)
        d_ins = [take(len(d["operands"])) for d in duties]
        take(len(after))
        outs = take(n_out)
        d_outs = [take(len(d["results"])) for d in duties]
        scr = take(n_scr)
        d_sems = [take(len(d["sems"])) for d in duties]
        hooks = [d["make"](di, do, *ds) for d, di, do, ds in zip(duties, d_ins, d_outs, d_sems)]
        if grid and hooks:
            ids = [pl.program_id(k) for k in range(len(grid))]
            first = functools.reduce(jnp.logical_and, [i == 0 for i in ids])
            last = functools.reduce(jnp.logical_and, [i == g - 1 for i, g in zip(ids, grid)])

            @pl.when(first)
            def _():
                for start, _ in hooks:
                    start()

            body(*ins, *outs, *scr)

            @pl.when(last)
            def _():
                for _, finish in hooks:
                    finish()
        else:
            for start, _ in hooks:
                start()
            body(*ins, *outs, *scr)
            for _, finish in hooks:
                finish()

    d_args = [a for d in duties for a in d["operands"]]
    d_res = [r for d in duties for r in d["results"]]
    kwargs = dict(grid=grid) if grid else {}
    res = pl.pallas_call(
        full, name=name, in_specs=list(in_specs) + [ANY] * (len(d_args) + len(after)), out_specs=list(out_specs) + [ANY] * len(d_res),
        out_shape=list(out_shape) + d_res, scratch_shapes=list(scratch) + sem_shapes,
        compiler_params=_params(sem) if grid else None, **kwargs,
    )(*args, *d_args, *after)
    own, rest = list(res[:n_out]), list(res[n_out:])
    by_duty = []
    for d in duties:
        by_duty.append(rest[:len(d["results"])])
        rest = rest[len(d["results"]):]
    return own, by_duty


def _comm_only(duties, name, after=()):
    return _call(lambda: None, name=name, grid=None, in_specs=[], out_specs=[], out_shape=[], args=[], sem=None, duties=duties,
                 after=after)[1]


HBM = pl.BlockSpec(memory_space=pltpu.HBM)
SEMS = pl.BlockSpec(memory_space=pltpu.SEMAPHORE)
SIDE_EFFECT = pltpu.SideEffectType.DATAFLOW_SIDE_EFFECTING
N_OTHER_CHIPS = 3


def _chip_copies(src_ref, land_ref, sems, rows):
    x, y, c = _place()
    chips = [(1 - x, y), (x, 1 - y), (1 - x, 1 - y)]
    part = (lambda ref: ref) if rows is None else (lambda ref: ref.at[pl.ds(0, rows)])
    return [pltpu.make_async_remote_copy(src_ref=part(src_ref.at[2 * px + py]), dst_ref=part(land_ref.at[2 * x + y]),
                                         send_sem=sems[j], recv_sem=sems[N_OTHER_CHIPS + j], device_id=(px, py, c), device_id_type=MESH)
            for j, (px, py) in enumerate(chips)]


def _exchange_start(pb, name, rows=None):
    n_sem = 2 * N_OTHER_CHIPS

    def body(pb_ref, land_ref, *rest):
        for cp in _chip_copies(pb_ref, land_ref, rest[:n_sem], rows):
            cp.start()
        token = rest[n_sem + 2]
        token[...] = jnp.zeros_like(token)

    res = pl.pallas_call(
        body, name=name,
        out_shape=(pltpu.SemaphoreType.DMA(()),) * n_sem + (pltpu.HBM(pb.shape, pb.dtype), pltpu.HBM(pb.shape, pb.dtype),
                                                              jax.ShapeDtypeStruct((8, 128), F32)),
        in_specs=(HBM, HBM), out_specs=(SEMS,) * n_sem + (HBM, HBM, pl.BlockSpec(memory_space=pltpu.VMEM)),
        input_output_aliases={0: n_sem, 1: n_sem + 1},
        compiler_params=pltpu.CompilerParams(has_side_effects=SIDE_EFFECT),
    )(pltpu.with_memory_space_constraint(pb, pltpu.HBM), pltpu.with_memory_space_constraint(lax.empty(pb.shape, pb.dtype), pltpu.HBM))
    return dict(sems=res[:n_sem], src=res[n_sem], land=res[n_sem + 1], token=res[n_sem + 2], rows=rows)


def _exchange_wait(started, after, name):
    n_sem = 2 * N_OTHER_CHIPS

    def body(pb_ref, land_ref, *rest):
        for cp in _chip_copies(pb_ref, land_ref, rest[:n_sem], started["rows"]):
            cp.wait_send()
            cp.wait_recv()

    src, land = started["src"], started["land"]
    return pl.pallas_call(
        body, name=name, out_shape=(pltpu.HBM(src.shape, src.dtype), pltpu.HBM(land.shape, land.dtype)),
        in_specs=(HBM, HBM) + (SEMS,) * n_sem + (ANY,) * len(after), out_specs=(HBM, HBM), input_output_aliases={0: 0, 1: 1},
        compiler_params=pltpu.CompilerParams(has_side_effects=SIDE_EFFECT),
    )(src, land, *started["sems"], *after)[1]


def _ffn_fwd(x, gpre, wg, wu, wd, gpost, name, duties=(), target=None):
    tm = 256
    n_in = 6 if target is None else 7

    def body(*refs):
        x_ref, gpre_ref, wg_ref, wu_ref, wd_ref, gpost_ref = refs[:6]
        xo_ref, n_ref, a_ref, b_ref, hm_ref, h_ref = refs[n_in:n_in + 6]
        xv = x_ref[...]
        n = (xv * _rs(xv) * gpre_ref[...]).astype(BF16)
        a = _dot_nt(n, wg_ref[...])
        b = _dot_nt(n, wu_ref[...])
        hm = (a * _sigmoid(a) * b).astype(BF16)
        h = _dot(hm, wd_ref[...])
        xo = xv + 0.5 * (h * _rs(h) * gpost_ref[...])
        n_ref[...] = n
        a_ref[...] = a.astype(BF16)
        b_ref[...] = b.astype(BF16)
        hm_ref[...] = hm
        h_ref[...] = h
        if target is None:
            xo_ref[...] = xo
        else:
            ss_ref = refs[n_in + 6]

            @pl.when(pl.program_id(0) == 0)
            def _():
                ss_ref[...] = jnp.zeros_like(ss_ref)

            err = xo - refs[6][...]
            xo_ref[...] = err * (1.0 / D)
            ss_ref[...] += jnp.sum(jnp.sum(err * err, axis=1, keepdims=True), axis=0, keepdims=True)

    loss_in = [] if target is None else [_rows(tm, D)]
    loss_out = [] if target is None else [_const((1, 128))]
    loss_shape = [] if target is None else [jax.ShapeDtypeStruct((1, 128), F32)]
    return _call(
        body, name=name, grid=(T // tm,),
        in_specs=[_rows(tm, D), _const((1, D)), _resident((FFP, D)), _resident((FFP, D)), _resident((FFP, D)), _const((1, D))] + loss_in,
        out_specs=[_rows(tm, D), _rows(tm, D), _rows(tm, FFP), _rows(tm, FFP), _rows(tm, FFP), _rows(tm, D)] + loss_out,
        out_shape=[jax.ShapeDtypeStruct((T, D), F32), jax.ShapeDtypeStruct((T, D), BF16), jax.ShapeDtypeStruct((T, FFP), BF16),
                   jax.ShapeDtypeStruct((T, FFP), BF16), jax.ShapeDtypeStruct((T, FFP), BF16), jax.ShapeDtypeStruct((T, D), F32)]
        + loss_shape,
        args=[x, gpre, wg, wu, wd, gpost] + ([] if target is None else [target]), sem=("arbitrary",), duties=duties)


def _ffn_bwd(dxo, x, a, b, h, gpre, gpost, wg, wu, wd, name, after=()):
    tm = 256

    def body(dxo_ref, x_ref, a_ref, b_ref, h_ref, gpre_ref, gpost_ref, wg_ref, wu_ref, wd_ref,
             dx_ref, da_ref, db_ref, dh_ref, dgpre_ref, dgpost_ref):
        @pl.when(pl.program_id(0) == 0)
        def _():
            dgpre_ref[...] = jnp.zeros_like(dgpre_ref)
            dgpost_ref[...] = jnp.zeros_like(dgpost_ref)

        dy = dxo_ref[...]
        h = h_ref[...]
        hn = h * _rs(h)
        r2 = _rs(h)
        dgpost_ref[...] += jnp.sum(0.5 * dy * hn, axis=0, keepdims=True)
        gdy = 0.5 * dy * gpost_ref[...]
        dh = r2 * (gdy - hn * jnp.mean(gdy * hn, axis=-1, keepdims=True))
        dhb = dh.astype(BF16)
        dh_ref[...] = dhb
        dhm = _dot_nt(dhb, wd_ref[...])
        av = a_ref[...].astype(F32)
        bv = b_ref[...].astype(F32)
        sg = _sigmoid(av)
        db = (dhm * (av * sg)).astype(BF16)
        da = (dhm * bv * _dsilu(av, sg)).astype(BF16)
        da_ref[...] = da
        db_ref[...] = db
        dn = _dot(da, wg_ref[...]) + _dot(db, wu_ref[...])
        xv = x_ref[...]
        r = _rs(xv)
        xn = xv * r
        dgpre_ref[...] += jnp.sum(dn * xn, axis=0, keepdims=True)
        gdn = dn * gpre_ref[...]
        dx_ref[...] = dy + r * (gdn - xn * jnp.mean(gdn * xn, axis=-1, keepdims=True))

    return _call(
        body, name=name, grid=(T // tm,),
        in_specs=[_rows(tm, D), _rows(tm, D), _rows(tm, FFP), _rows(tm, FFP), _rows(tm, D), _const((1, D)), _const((1, D)),
                  _resident((FFP, D)), _resident((FFP, D)), _resident((FFP, D))],
        out_specs=[_rows(tm, D), _rows(tm, FFP), _rows(tm, FFP), _rows(tm, D), _const((1, D)), _const((1, D))],
        out_shape=[jax.ShapeDtypeStruct((T, D), F32), jax.ShapeDtypeStruct((T, FFP), BF16), jax.ShapeDtypeStruct((T, FFP), BF16),
                   jax.ShapeDtypeStruct((T, D), BF16), jax.ShapeDtypeStruct((1, D), F32), jax.ShapeDtypeStruct((1, D), F32)],
        args=[dxo, x, a, b, h, gpre, gpost, wg, wu, wd], sem=("arbitrary",), after=after)


def _core_index(core):
    return lax.axis_index("c") if core == "mesh" else core


def _by_core(res, o_ref, ob_ref, core):
    r = res.shape[0] // 2
    c = jnp.asarray(_core_index(core))

    @pl.when(c == 0)
    def _():
        o_ref[0] = res[:r]
        ob_ref[0] = res[r:].astype(BF16)

    @pl.when(c == 1)
    def _():
        o_ref[0] = res[r:]
        ob_ref[0] = res[:r].astype(BF16)


def _matmul_tn(a, b, name, core, after=(), duties=()):
    k, m = a.shape
    n = b.shape[1]
    r = m // N_DEV
    assert m == N_DEV * r and r % 128 == 0

    def body(a_ref, b_ref, o_ref, ob_ref):
        _by_core(_dot_tn(a_ref[...], b_ref[...]), o_ref, ob_ref, core)

    spec = pl.BlockSpec((1, r, n), lambda i: (i, 0, 0))
    return _call(body, name=name, grid=(N_DEV // 2,), in_specs=[pl.BlockSpec((k, 2 * r), lambda i: (0, i)), _resident((k, n))],
                 out_specs=[spec, spec],
                 out_shape=[jax.ShapeDtypeStruct((N_DEV // 2, r, n), F32), jax.ShapeDtypeStruct((N_DEV // 2, r, n), BF16)],
                 args=[a, b], sem=("arbitrary",), duties=duties, after=after)


def _dwout(attn, yn, dh2, core):
    rs = (QW + SSM_W) // N_DEV
    chips = N_DEV // 2

    def body(at_ref, yn_ref, dh_ref, o_ref, ob_ref):
        i = pl.program_id(0)

        @pl.when(i < chips // 2)
        def _():
            _by_core(_dot_tn(at_ref[...], dh_ref[...]), o_ref, ob_ref, core)

        @pl.when(i >= chips // 2)
        def _():
            _by_core(_dot_tn(yn_ref[...], dh_ref[...]), o_ref, ob_ref, core)

    spec = pl.BlockSpec((1, rs, D), lambda i: (i, 0, 0))
    return pl.pallas_call(
        body, name="dwout", grid=(chips,),
        in_specs=[pl.BlockSpec((T, 2 * rs), lambda i: (0, jnp.minimum(i, chips // 2 - 1))),
                  pl.BlockSpec((T, 2 * rs), lambda i: (0, jnp.maximum(i - chips // 2, 0))), _resident((T, D))],
        out_specs=[spec, spec],
        out_shape=[jax.ShapeDtypeStruct((chips, rs, D), F32), jax.ShapeDtypeStruct((chips, rs, D), BF16)],
        compiler_params=_params(("arbitrary",)),
    )(attn, yn, dh2)


def _rope_swap(t, lo_half):
    return jnp.where(lo_half, pltpu.roll(t, 96, 1), pltpu.roll(t, 32, 1))


def _to_query_heads(t):
    return jnp.concatenate([t[:, HD * (h // G_PER):HD * (h // G_PER + 1)] for h in range(NQ)], axis=1)


def _from_query_heads(t):
    def kv_sum(kv):
        parts = [t[:, HD * (G_PER * kv + g):HD * (G_PER * kv + g + 1)] for g in range(G_PER)]
        return (parts[0] + parts[1]) + (parts[2] + parts[3])

    return jnp.concatenate([kv_sum(kv) for kv in range(NKV)], axis=1)


def _inproj_fwd(x1, gpre, win, cos, sin_s, duties=()):
    tm = 256

    def body(x_ref, g_ref, w_ref, cos_ref, sin_ref, n_ref, q_ref, kx_ref, vx_ref, xbc_ref, z_ref, dt_ref):
        xv = x_ref[...]
        n = (xv * _rs(xv) * g_ref[...]).astype(BF16)
        n_ref[...] = n
        by_dev = _dot_nt(n, w_ref[...])
        proj = jnp.concatenate([by_dev[:, ISW * d:ISW * d + ISR] for d in range(N_DEV)], axis=1)
        cs = cos_ref[...]
        sn = sin_ref[...]
        lo_half = (lax.broadcasted_iota(jnp.int32, (1, 128), 1) % HD) < (HD // 2)

        def rope(t):
            return t * cs + _rope_swap(t, lo_half) * sn

        for j in range(QW // 128):
            t = proj[:, 128 * j:128 * j + 128]
            q_ref[:, 128 * j:128 * j + 128] = (rope(t) * (HD ** -0.5)).astype(BF16)
        k = jnp.concatenate([rope(proj[:, QW + 128 * j:QW + 128 * j + 128]) for j in range(KVW // 128)], axis=1)
        v = proj[:, QW + KVW:QW + 2 * KVW]
        kx_ref[...] = _to_query_heads(k).astype(BF16)
        vx_ref[...] = _to_query_heads(v).astype(BF16)
        c0 = QW + 2 * KVW
        xbc_ref[...] = proj[:, c0:c0 + CONV_C]
        z_ref[...] = proj[:, c0 + CONV_C:c0 + CONV_C + SSM_W]
        dt_ref[...] = proj[:, c0 + CONV_C + SSM_W:IN_COLS]

    return _call(
        body, name="inproj_fwd", grid=(T // tm,),
        in_specs=[_rows(tm, D), _const((1, D)), _resident((INP, D)), _rows(tm, 128), _rows(tm, 128)],
        out_specs=[_rows(tm, D), _rows(tm, QW), _rows(tm, QW), _rows(tm, QW), _rows(tm, CONV_C), _rows(tm, SSM_W), _rows(tm, SSM_H)],
        out_shape=[jax.ShapeDtypeStruct((T, D), BF16), jax.ShapeDtypeStruct((T, QW), BF16), jax.ShapeDtypeStruct((T, QW), BF16),
                   jax.ShapeDtypeStruct((T, QW), BF16), jax.ShapeDtypeStruct((T, CONV_C), F32), jax.ShapeDtypeStruct((T, SSM_W), F32),
                   jax.ShapeDtypeStruct((T, SSM_H), F32)],
        args=[x1, gpre, win, cos, sin_s], sem=("arbitrary",), duties=duties)


def _inproj_bwd(dres, dq, dkx, dvx, dxbc, dz, ddt, x1, gpre, win, cos, sin_s):
    tm = 256

    def body(dres_ref, dq_ref, dkx_ref, dvx_ref, dxbc_ref, dz_ref, ddt_ref, x_ref, g_ref, w_ref, cos_ref, sin_ref,
             dx_ref, dps_ref, dg_ref, dp_ref):
        @pl.when(pl.program_id(0) == 0)
        def _():
            dg_ref[...] = jnp.zeros_like(dg_ref)

        cs = cos_ref[...]
        sn = sin_ref[...]
        lo_half = (lax.broadcasted_iota(jnp.int32, (1, 128), 1) % HD) < (HD // 2)

        def rope_t(t):
            return t * cs - _rope_swap(t, lo_half) * sn

        for j in range(QW // 128):
            dp_ref[:, 128 * j:128 * j + 128] = rope_t(dq_ref[:, 128 * j:128 * j + 128] * (HD ** -0.5)).astype(BF16)
        dk = _from_query_heads(dkx_ref[...])
        dv = _from_query_heads(dvx_ref[...])
        for j in range(KVW // 128):
            dp_ref[:, QW + 128 * j:QW + 128 * j + 128] = rope_t(dk[:, 128 * j:128 * j + 128]).astype(BF16)
        dp_ref[:, QW + KVW:QW + 2 * KVW] = dv.astype(BF16)
        c0 = QW + 2 * KVW
        dp_ref[:, c0:c0 + CONV_C] = dxbc_ref[...].astype(BF16)
        dp_ref[:, c0 + CONV_C:c0 + CONV_C + SSM_W] = dz_ref[...].astype(BF16)
        dp_ref[:, c0 + CONV_C + SSM_W:INP] = ddt_ref[...].astype(BF16)
        pieces = [dp_ref[:, ISR * d:ISR * (d + 1)] for d in range(N_DEV)]
        zw = jnp.zeros((tm, ISW - ISR), BF16)
        zg = jnp.zeros((tm, ISG - ISR), BF16)
        dn = _dot(jnp.concatenate([t for p in pieces for t in (p, zw)], axis=1), w_ref[...])
        for d in range(N_DEV):
            dps_ref[:, ISG * d:ISG * (d + 1)] = jnp.concatenate([pieces[d], zg], axis=1)
        xv = x_ref[...]
        r = _rs(xv)
        xn = xv * r
        dg_ref[...] += jnp.sum(dn * xn, axis=0, keepdims=True)
        gdn = dn * g_ref[...]
        dx_ref[...] = dres_ref[...] + r * (gdn - xn * jnp.mean(gdn * xn, axis=-1, keepdims=True))

    return pl.pallas_call(
        body, name="inproj_bwd", grid=(T // tm,),
        in_specs=[_rows(tm, D), _rows(tm, QW), _rows(tm, QW), _rows(tm, QW), _rows(tm, CONV_C), _rows(tm, SSM_W), _rows(tm, 128),
                  _rows(tm, D), _const((1, D)), _resident((INP, D)), _rows(tm, 128), _rows(tm, 128)],
        out_specs=[_rows(tm, D), _rows(tm, N_DEV * ISG), _const((1, D))],
        out_shape=[jax.ShapeDtypeStruct((T, D), F32), jax.ShapeDtypeStruct((T, N_DEV * ISG), BF16), jax.ShapeDtypeStruct((1, D), F32)],
        scratch_shapes=[pltpu.VMEM((tm, INP), BF16)],
        compiler_params=_params(("arbitrary",)),
    )(dres, dq, dkx, dvx, dxbc, dz, ddt, x1, gpre, win, cos, sin_s)


def _outproj_fwd(x1, attn, yn, wout, gpost):
    tm = 512

    def body(x_ref, at_ref, yn_ref, w_ref, g_ref, xo_ref, h_ref):
        h = _dot(at_ref[...], w_ref[0:QW, :]) + _dot(yn_ref[...], w_ref[QW:QW + SSM_W, :])
        h_ref[...] = h
        xo_ref[...] = x_ref[...] + h * _rs(h) * g_ref[...]

    return pl.pallas_call(
        body, name="outproj_fwd", grid=(T // tm,),
        in_specs=[_rows(tm, D), _rows(tm, QW), _rows(tm, SSM_W), _resident((QW + SSM_W, D)), _const((1, D))],
        out_specs=[_rows(tm, D), _rows(tm, D)],
        out_shape=[jax.ShapeDtypeStruct((T, D), F32), jax.ShapeDtypeStruct((T, D), F32)],
        compiler_params=_params(("parallel",)),
    )(x1, attn, yn, wout, gpost)


def _outproj_bwd(dx2, h2, gpost, wout, duties=()):
    tm = 512

    def body(dy_ref, h_ref, g_ref, w_ref, dh_ref, dm_ref, dg_ref):
        @pl.when(pl.program_id(0) == 0)
        def _():
            dg_ref[...] = jnp.zeros_like(dg_ref)

        dy = dy_ref[...]
        h = h_ref[...]
        r = _rs(h)
        hn = h * r
        dg_ref[...] += jnp.sum(dy * hn, axis=0, keepdims=True)
        gdy = dy * g_ref[...]
        dh = (r * (gdy - hn * jnp.mean(gdy * hn, axis=-1, keepdims=True))).astype(BF16)
        dh_ref[...] = dh
        dm_ref[...] = _dot_nt(dh, w_ref[...])

    return _call(
        body, name="outproj_bwd", grid=(T // tm,),
        in_specs=[_rows(tm, D), _rows(tm, D), _const((1, D)), _resident((QW + SSM_W, D))],
        out_specs=[_rows(tm, D), _rows(tm, QW + SSM_W), _const((1, D))],
        out_shape=[jax.ShapeDtypeStruct((T, D), BF16), jax.ShapeDtypeStruct((T, QW + SSM_W), F32), jax.ShapeDtypeStruct((1, D), F32)],
        args=[dx2, h2, gpost, wout], sem=("arbitrary",), duties=duties)


G_PER = NQ // NKV
WK = G_PER * HD


def _attn_fwd(q, kx, vx, bias, duties=()):
    def body(q_ref, kx_ref, vx_ref, bias_ref, o_ref, lse_ref):
        lane = lax.broadcasted_iota(jnp.int32, (1, WK), 1)
        lse_ref[...] = jnp.zeros_like(lse_ref)
        for i in range(NAB):
            n = (i + 1) * AB
            rows = slice(i * AB, n)
            qi = q_ref[rows, :]
            kxi = kx_ref[0:n, :]
            vxi = vx_ref[0:n, :]
            bb = bias_ref[:, (NAB - 1 - i) * AB:]
            o_acc = jnp.zeros((AB, WK), F32)
            for g in range(G_PER):
                mg = (lane // HD) == g
                s = _dot_nt(jnp.where(mg, qi, jnp.zeros_like(qi)), kxi) + bb
                m = jnp.max(s, axis=1, keepdims=True)
                p = jnp.exp(s - m)
                l = jnp.sum(p, axis=1, keepdims=True)
                o_acc = jnp.where(mg, _dot(p.astype(BF16), vxi) / l, o_acc)
                lse_ref[rows, g:g + 1] = m + jnp.log(l)
            o_ref[rows, :] = o_acc.astype(BF16)

    col = lambda kv: (0, kv)
    return _call(
        body, name="attn_fwd", grid=(NKV,),
        in_specs=[pl.BlockSpec((T, WK), col), pl.BlockSpec((T, WK), col), pl.BlockSpec((T, WK), col), _const((AB, T))],
        out_specs=[pl.BlockSpec((T, WK), col), pl.BlockSpec((T, 128), col)],
        out_shape=[jax.ShapeDtypeStruct((T, QW), BF16), jax.ShapeDtypeStruct((T, NKV * 128), F32)],
        args=[q, kx, vx, bias], sem=("arbitrary",), duties=duties)


def _attn_bwd(q, kx, vx, o, dmix, lse, bias, duties=()):
    def body(q_ref, kx_ref, vx_ref, o_ref, do_ref, lse_ref, bias_ref, dq_ref, dkx_ref, dvx_ref):
        lane = lax.broadcasted_iota(jnp.int32, (1, WK), 1)
        dkx_ref[...] = jnp.zeros_like(dkx_ref)
        dvx_ref[...] = jnp.zeros_like(dvx_ref)
        for i in range(NAB):
            n = (i + 1) * AB
            rows = slice(i * AB, n)
            qi = q_ref[rows, :]
            dof = do_ref[rows, :]
            doi = dof.astype(BF16)
            prod = dof * o_ref[rows, :].astype(F32)
            kxi = kx_ref[0:n, :]
            vxi = vx_ref[0:n, :]
            bb = bias_ref[:, (NAB - 1 - i) * AB:]
            dq_acc = jnp.zeros((AB, WK), F32)
            for g in range(G_PER):
                mg = (lane // HD) == g
                qm = jnp.where(mg, qi, jnp.zeros_like(qi))
                dom = jnp.where(mg, doi, jnp.zeros_like(doi))
                delta = jnp.sum(jnp.where(mg, prod, 0.0), axis=1, keepdims=True)
                p = jnp.exp(_dot_nt(qm, kxi) + bb - lse_ref[rows, g:g + 1])
                ds = (p * (_dot_nt(dom, vxi) - delta)).astype(BF16)
                dvx_ref[0:n, :] += _dot_tn(p.astype(BF16), dom)
                dkx_ref[0:n, :] += _dot_tn(ds, qm)
                dq_acc = jnp.where(mg, _dot(ds, kxi), dq_acc)
            dq_ref[rows, :] = dq_acc

    col = lambda kv: (0, kv)
    return _call(
        body, name="attn_bwd", grid=(NKV,),
        in_specs=[pl.BlockSpec((T, WK), col), pl.BlockSpec((T, WK), col), pl.BlockSpec((T, WK), col), pl.BlockSpec((T, WK), col),
                  pl.BlockSpec((T, WK), col), pl.BlockSpec((T, 128), col), _const((AB, T))],
        out_specs=[pl.BlockSpec((T, WK), col), pl.BlockSpec((T, WK), col), pl.BlockSpec((T, WK), col)],
        out_shape=[jax.ShapeDtypeStruct((T, QW), F32)] * 3,
        args=[q, kx, vx, o, dmix, lse, bias], sem=("arbitrary",), duties=duties)


def _softplus(x):
    return jnp.maximum(x, 0.0) + jnp.log1p(jnp.exp(-jnp.abs(x)))


def _causal_conv(u, zs, cw_ref, cb_ref):
    zs[8:, :] = u
    sh1, sh2, sh3 = (zs[8 - m:8 - m + L, :] for m in (1, 2, 3))
    return cb_ref[...] + cw_ref[3:4, :] * u + cw_ref[2:3, :] * sh1 + cw_ref[1:2, :] * sh2 + cw_ref[0:1, :] * sh3


def _ssd_chunk_common(xc, dtr, dtb_ref, alx_ref, e_ref):
    sg = _sigmoid(xc)
    act = xc * sg
    pre = dtr + dtb_ref[...]
    dt_x = _dot_hi(_softplus(pre), e_ref[...])
    a_x = -jnp.exp(alx_ref[...])
    ri = lax.broadcasted_iota(jnp.int32, (L, L), 0)
    ci = lax.broadcasted_iota(jnp.int32, (L, L), 1)
    tri = ri >= ci
    acs_x = _dot_hi(tri, dt_x * a_x, a_is_01=True)
    return dict(sg=sg, act=act, pre=pre, dt_x=dt_x, a_x=a_x, tri=tri, acs_x=acs_x)


def _decay(acs_x, acs_t, h, tri):
    col = acs_x[:, HD * h:HD * h + 1]
    row = acs_t[HD * h:HD * h + 1, :]
    return jnp.exp(jnp.where(tri, col - row, NEG))


def _ssd_fwd(xbc, z, dtr, convw, convb, dtb, alx, dskx, ssmn, e, duties=()):
    def body(u_ref, z_ref, dtr_ref, cw_ref, cb_ref, dtb_ref, alx_ref, dsk_ref, sn_ref, e_ref,
             yn_ref, y_ref, hs_ref, xc_ref, zs, hst):
        @pl.when(pl.program_id(0) == 0)
        def _():
            zs[0:8, :] = jnp.zeros((8, CONV_C), F32)
            hst[...] = jnp.zeros_like(hst)

        u = u_ref[...]
        xc = _causal_conv(u, zs, cw_ref, cb_ref)
        xc_ref[...] = xc
        zs[0:8, :] = u[L - 8:, :]
        cm = _ssd_chunk_common(xc, dtr_ref[...], dtb_ref, alx_ref, e_ref)
        act, dt_x, acs_x, tri = cm["act"], cm["dt_x"], cm["acs_x"], cm["tri"]
        xs = act[:, :SSM_W]
        acs_l = acs_x[L - 1:L, :]
        lam_x = jnp.exp(acs_x)
        w_x = jnp.exp(acs_l - acs_x)
        gam_x = jnp.exp(acs_l)
        acs_t = acs_x.T
        xd = xs * dt_x
        xb = xd.astype(BF16)
        xw = (xd * w_x).astype(BF16)
        lo = lax.broadcasted_iota(jnp.int32, (1, 128), 1) < HD
        hs_ref[0] = hst[...]
        pieces = []
        for grp in range(2):
            bb = act[:, SSM_W + SSM_N * grp:SSM_W + SSM_N * (grp + 1)].astype(BF16)
            cb_ = act[:, SSM_W + 2 * SSM_N + SSM_N * grp:SSM_W + 2 * SSM_N + SSM_N * (grp + 1)].astype(BF16)
            cbm = _dot_nt(cb_, bb)
            for jj in range(4):
                j = 4 * grp + jj
                sl = slice(128 * j, 128 * j + 128)
                m0 = (cbm * _decay(acs_x, acs_t, 2 * j, tri)).astype(BF16)
                m1 = (cbm * _decay(acs_x, acs_t, 2 * j + 1, tri)).astype(BF16)
                x2 = xb[:, sl]
                ydiag = jnp.where(lo, _dot(m0, x2), _dot(m1, x2))
                hprev = hst[j]
                yoff = lam_x[:, sl] * _dot(cb_, hprev.astype(BF16))
                pieces.append(ydiag + yoff)
                hst[j] = gam_x[:, sl] * hprev + _dot_tn(bb, xw[:, sl])
        y = jnp.concatenate(pieces, axis=1) + dsk_ref[...] * xs
        y_ref[...] = y
        zv = z_ref[...]
        yz = y * (zv * _sigmoid(zv))
        half = SSM_W // 2
        yn = jnp.concatenate([yz[:, :half] * _rs(yz[:, :half]), yz[:, half:] * _rs(yz[:, half:])], axis=1)
        yn_ref[...] = (yn * sn_ref[...]).astype(BF16)

    return _call(
        body, name="ssd_fwd", grid=(NCH,),
        in_specs=[_rows(L, CONV_C), _rows(L, SSM_W), _rows(L, SSM_H), _const((4, CONV_C)), _const((1, CONV_C)), _const((1, SSM_H)),
                  _const((1, SSM_W)), _const((1, SSM_W)), _const((1, SSM_W)), _const((SSM_H, SSM_W))],
        out_specs=[_rows(L, SSM_W), _rows(L, SSM_W), pl.BlockSpec((1, 8, SSM_N, 128), lambda c: (c, 0, 0, 0)), _rows(L, CONV_C)],
        out_shape=[jax.ShapeDtypeStruct((T, SSM_W), BF16), jax.ShapeDtypeStruct((T, SSM_W), F32),
                   jax.ShapeDtypeStruct((NCH, 8, SSM_N, 128), F32), jax.ShapeDtypeStruct((T, CONV_C), F32)],
        scratch=[pltpu.VMEM((8 + L, CONV_C), F32), pltpu.VMEM((8, SSM_N, 128), F32)],
        args=[xbc, z, dtr, convw, convb, dtb, alx, dskx, ssmn, e], sem=("arbitrary",), duties=duties)


def _ssd_bwd(dmix, xbc, xconv, z, dtr, y, hs, convw, dtb, alx, dskx, ssmn, e, e1, duties=()):
    rev = lambda i: (NCH - 1 - i, 0)

    def body(dyn_ref, u_ref, xc_ref, z_ref, dtr_ref, y_ref, hs_ref, cw_ref, dtb_ref, alx_ref, dsk_ref, sn_ref, e_ref, e1_ref,
             dxbc_ref, dz_ref, ddt_ref, dcw_ref, dcb_ref, dsn_ref, dpar_ref, dh, zd, colbuf):
        step = pl.program_id(0)

        @pl.when(step == 0)
        def _():
            for r in (dh, dcw_ref, dcb_ref, dsn_ref, dpar_ref):
                r[...] = jnp.zeros_like(r)
            zd[L:, :] = jnp.zeros((8, CONV_C), F32)

        u = u_ref[...]
        xc = xc_ref[...]
        cm = _ssd_chunk_common(xc, dtr_ref[...], dtb_ref, alx_ref, e_ref)
        sg, act, pre, dt_x, a_x, tri, acs_x = (cm[k] for k in ("sg", "act", "pre", "dt_x", "a_x", "tri", "acs_x"))
        xs = act[:, :SSM_W]
        acs_l = acs_x[L - 1:L, :]
        lam_x = jnp.exp(acs_x)
        w_x = jnp.exp(acs_l - acs_x)
        gam_x = jnp.exp(acs_l)
        acs_t = acs_x.T
        xd = xs * dt_x
        xb = xd.astype(BF16)
        xdw = xd * w_x
        xw = xdw.astype(BF16)
        lo = lax.broadcasted_iota(jnp.int32, (1, 128), 1) < HD
        row8 = lax.broadcasted_iota(jnp.int32, (8, 1), 0)

        dyn = dyn_ref[...]
        yv = y_ref[...]
        zv = z_ref[...]
        sz = _sigmoid(zv)
        siluz = zv * sz
        yz = yv * siluz
        half = SSM_W // 2
        gy = dyn * sn_ref[...]
        dyz_parts, yzn_parts = [], []
        for hf in range(2):
            part = yz[:, hf * half:(hf + 1) * half]
            r = _rs(part)
            pn = part * r
            gp = gy[:, hf * half:(hf + 1) * half]
            dyz_parts.append(r * (gp - pn * jnp.mean(gp * pn, axis=-1, keepdims=True)))
            yzn_parts.append(pn)
        dyz = jnp.concatenate(dyz_parts, axis=1)
        dsn_ref[...] += jnp.sum(dyn * jnp.concatenate(yzn_parts, axis=1), axis=0, keepdims=True)
        dy = dyz * siluz
        dz_ref[...] = dyz * yv * _dsilu(zv, sz)

        colbuf[...] = jnp.zeros_like(colbuf)
        dx_pieces, dacs_pieces, dacsl_pieces, db_pieces, dc_pieces = [], [], [], [], []
        for grp in range(2):
            bb = act[:, SSM_W + SSM_N * grp:SSM_W + SSM_N * (grp + 1)].astype(BF16)
            cb_ = act[:, SSM_W + 2 * SSM_N + SSM_N * grp:SSM_W + 2 * SSM_N + SSM_N * (grp + 1)].astype(BF16)
            cbm = _dot_nt(cb_, bb)
            dcbm = jnp.zeros((L, L), F32)
            dc_g = jnp.zeros((L, SSM_N), F32)
            db_g = jnp.zeros((L, SSM_N), F32)
            for jj in range(4):
                j = 4 * grp + jj
                sl = slice(128 * j, 128 * j + 128)
                dy2 = dy[:, sl]
                dy2b = dy2.astype(BF16)
                d0 = _decay(acs_x, acs_t, 2 * j, tri)
                d1 = _decay(acs_x, acs_t, 2 * j + 1, tri)
                m0 = cbm * d0
                m1 = cbm * d1
                x2 = xb[:, sl]
                hprev = hs_ref[0, j]
                hprevb = hprev.astype(BF16)
                dhn = dh[j]
                dhnb = dhn.astype(BF16)
                g2 = _dot(bb, dhnb)
                dx_pieces.append(jnp.where(lo, _dot_tn(m0.astype(BF16), dy2b), _dot_tn(m1.astype(BF16), dy2b)) + w_x[:, sl] * g2)
                zero = jnp.zeros_like(dy2b)
                dm0 = _dot_nt(jnp.where(lo, dy2b, zero), x2)
                dm1 = _dot_nt(jnp.where(lo, zero, dy2b), x2)
                dcbm = dcbm + dm0 * d0 + dm1 * d1
                e0 = dm0 * m0
                e1v = dm1 * m1
                colbuf[:, 2 * j:2 * j + 1] = jnp.sum(e0, axis=1, keepdims=True) - jnp.sum(e0.T, axis=1, keepdims=True)
                colbuf[:, 2 * j + 1:2 * j + 2] = jnp.sum(e1v, axis=1, keepdims=True) - jnp.sum(e1v.T, axis=1, keepdims=True)
                yoff = lam_x[:, sl] * _dot(cb_, hprevb)
                gxw = g2 * xdw[:, sl]
                dacs_pieces.append(dy2 * yoff - gxw)
                dacsl_pieces.append(jnp.sum(gxw, axis=0, keepdims=True) + gam_x[:, sl] * jnp.sum(dhn * hprev, axis=0, keepdims=True))
                dyl = (dy2 * lam_x[:, sl]).astype(BF16)
                dc_g = dc_g + _dot_nt(dyl, hprevb)
                db_g = db_g + _dot_nt(xw[:, sl], dhnb)
                dh[j] = gam_x[:, sl] * dhn + _dot_tn(cb_, dyl)
            dcbb = dcbm.astype(BF16)
            dc_pieces.append(dc_g + _dot(dcbb, bb))
            db_pieces.append(db_g + _dot_tn(dcbb, cb_))

        dxd = jnp.concatenate(dx_pieces, axis=1)
        rowi = lax.broadcasted_iota(jnp.int32, (L, 1), 0)
        dacs_x = (jnp.concatenate(dacs_pieces, axis=1) + _dot_hi(colbuf[...], e1_ref[...])
                  + jnp.where(rowi == L - 1, jnp.concatenate(dacsl_pieces, axis=1), 0.0))
        upper = lax.broadcasted_iota(jnp.int32, (L, L), 0) <= lax.broadcasted_iota(jnp.int32, (L, L), 1)
        dadt_x = _dot_hi(upper, dacs_x, a_is_01=True)
        ddt_x = dxd * xs + dadt_x * a_x
        ddtr = _dot_nt_hi(ddt_x, e_ref[...]) * _sigmoid(pre)
        ddt_ref[...] = jnp.zeros_like(ddt_ref)
        ddt_ref[:, 0:SSM_H] = ddtr
        dalx =jnp.sum(dadt_x * dt_x, axis=0, keepdims=True) * a_x
        ddskx = jnp.sum(dy * xs, axis=0, keepdims=True)
        par_x = jnp.where(row8 == 1, dalx, 0.0) + jnp.where(row8 == 2, ddskx, 0.0)
        dpar_ref[...] += _dot_nt_hi(par_x, e_ref[...]) + jnp.where(row8 == 0, jnp.sum(ddtr, axis=0, keepdims=True), 0.0)

        dxs = dxd * dt_x + dsk_ref[...] * dy
        dact = jnp.concatenate([dxs] + db_pieces + dc_pieces, axis=1)
        du = dact * _dsilu(xc, sg)
        dcb_ref[...] += jnp.sum(du, axis=0, keepdims=True)
        zd[0:L, :] = du
        f1, f2, f3 = (zd[m:m + L, :] for m in (1, 2, 3))
        dxbc_ref[...] = cw_ref[3:4, :] * du + cw_ref[2:3, :] * f1 + cw_ref[1:2, :] * f2 + cw_ref[0:1, :] * f3
        dcw = jnp.zeros((8, CONV_C), F32)
        for k, shifted in enumerate((f3, f2, f1, du)):
            dcw = dcw + jnp.where(row8 == k, jnp.sum(shifted * u, axis=0, keepdims=True), 0.0)
        dcw_ref[...] += dcw
        zd[L:, :] = du[:8, :]

    return _call(
        body, name="ssd_bwd", grid=(NCH,),
        in_specs=[pl.BlockSpec((L, SSM_W), lambda i: (NCH - 1 - i, 1)), pl.BlockSpec((L, CONV_C), rev), pl.BlockSpec((L, CONV_C), rev),
                  pl.BlockSpec((L, SSM_W), rev), pl.BlockSpec((L, SSM_H), rev), pl.BlockSpec((L, SSM_W), rev),
                  pl.BlockSpec((1, 8, SSM_N, 128), lambda i: (NCH - 1 - i, 0, 0, 0)),
                  _const((4, CONV_C)), _const((1, SSM_H)), _const((1, SSM_W)), _const((1, SSM_W)), _const((1, SSM_W)),
                  _const((SSM_H, SSM_W)), _const((128, SSM_W))],
        out_specs=[pl.BlockSpec((L, CONV_C), rev), pl.BlockSpec((L, SSM_W), rev), pl.BlockSpec((L, 128), rev),
                   _const((8, CONV_C)), _const((1, CONV_C)), _const((1, SSM_W)), _const((8, SSM_H))],
        out_shape=[jax.ShapeDtypeStruct((T, CONV_C), F32), jax.ShapeDtypeStruct((T, SSM_W), F32), jax.ShapeDtypeStruct((T, 128), F32),
                   jax.ShapeDtypeStruct((8, CONV_C), F32), jax.ShapeDtypeStruct((1, CONV_C), F32), jax.ShapeDtypeStruct((1, SSM_W), F32),
                   jax.ShapeDtypeStruct((8, SSM_H), F32)],
        scratch=[pltpu.VMEM((8, SSM_N, 128), F32), pltpu.VMEM((L + 8, CONV_C), F32), pltpu.VMEM((L, 128), F32)],
        args=[dmix, xbc, xconv, z, dtr, y, hs, convw, dtb, alx, dskx, ssmn, e, e1], sem=("arbitrary",), duties=duties)


def _adam_math(w, g, m, v):
    m = ADAM_B1 * m + (1.0 - ADAM_B1) * g
    v = ADAM_B2 * v + (1.0 - ADAM_B2) * (g * g)
    m_hat = m / (1.0 - ADAM_B1 ** ADAM_STEP)
    v_hat = v / (1.0 - ADAM_B2 ** ADAM_STEP)
    delta = -ADAM_LR * (m_hat / (jnp.sqrt(v_hat) + ADAM_EPS) + ADAM_WD * w)
    return delta, m, v


PACK_W = CONV_C


def _adamw_small(ws, ms, vs, total, conv_g, after=()):
    k = len(ws)

    def body(*refs):
        ins, outs = refs[:3 * k + 2], refs[3 * k + 2 + len(after):]
        total_ref, conv_ref = ins[3 * k], ins[3 * k + 1]
        for a in range(k):
            w_ref, m_ref, v_ref = ins[3 * a:3 * a + 3]
            g = conv_ref[...] if a == k - 1 else total_ref[a:a + 1, 0:w_ref.shape[1]]
            delta, nm, nv = _adam_math(w_ref[...], g, m_ref[...], v_ref[...])
            for ref, val in zip(outs[4 * a:4 * a + 4], (g, delta, nm, nv)):
                ref[...] = val

    vm = pl.BlockSpec(memory_space=pltpu.VMEM)
    args = [x for a in range(k) for x in (ws[a], ms[a], vs[a])] + [total, conv_g]
    res = pl.pallas_call(
        body, name="adamw_small", in_specs=[vm] * len(args) + [ANY] * len(after), out_specs=[vm] * (4 * k),
        out_shape=[jax.ShapeDtypeStruct(ws[a].shape, F32) for a in range(k) for _ in range(4)],
    )(*args, *after)
    return [tuple(res[4 * a:4 * a + 4]) for a in range(k)]


COL_TILE = 512


def _adamw_sharded(ws, ms, vs, chip_sums, from_chips, other_chips, name, after=()):
    k = len(ws)
    rows, cols = ws[0].shape
    prow = chip_sums[0].shape[0]
    assert cols % COL_TILE == 0 and prow >= rows and all(a.shape == ws[0].shape for a in ws)

    def body(ids_ref, *refs):
        ins, outs = refs[:7 * k], refs[7 * k + len(after):]
        for a in range(k):
            w_ref, m_ref, v_ref, s_ref, r1_ref, r2_ref, r3_ref = ins[7 * a:7 * a + 7]
            g = s_ref[...]
            for r in (r1_ref, r2_ref, r3_ref):
                g = g + r[0].astype(F32)
            g = g[:rows]
            delta, nm, nv = _adam_math(w_ref[...], g, m_ref[...], v_ref[...])
            for ref, val in zip(outs[4 * a:4 * a + 4], (g, delta, nm, nv)):
                ref[...] = val

    spec = pl.BlockSpec((rows, COL_TILE), lambda i, ids: (0, i))
    part = lambda j: pl.BlockSpec((1, prow, COL_TILE), lambda i, ids: (ids[j], 0, i))
    one = [spec, spec, spec, pl.BlockSpec((prow, COL_TILE), lambda i, ids: (0, i)), part(0), part(1), part(2)]
    args = [x for a in range(k) for x in (ws[a], ms[a], vs[a], chip_sums[a], from_chips[a], from_chips[a], from_chips[a])]
    res = pl.pallas_call(
        body, name=name,
        grid_spec=pltpu.PrefetchScalarGridSpec(
            num_scalar_prefetch=1, grid=(cols // COL_TILE,),
            in_specs=one * k + [ANY] * len(after), out_specs=[spec] * (4 * k)),
        out_shape=[jax.ShapeDtypeStruct((rows, cols), F32)] * (4 * k),
        compiler_params=_params(("parallel",)),
    )(other_chips, *args, *after)
    return [tuple(res[4 * a:4 * a + 4]) for a in range(k)]


def _chip_sum(mines, recvs, name):
    k = len(mines)
    rows, cols = mines[0].shape[1:]

    def body(*refs):
        own_chip = pl.program_id(0) == 2 * lax.axis_index("x") + lax.axis_index("y")
        for a in range(k):
            a_ref, b_ref = refs[2 * a:2 * a + 2]
            s_ref, sb_ref = refs[2 * k + 2 * a:2 * k + 2 * a + 2]
            s = a_ref[0] + b_ref[0].astype(F32)
            sb_ref[0] = s.astype(BF16)

            @pl.when(own_chip)
            def _(s_ref=s_ref, s=s):
                s_ref[...] = s

    by_chip = pl.BlockSpec((1, rows, cols), lambda c: (c, 0, 0))
    res = pl.pallas_call(
        body, name=name, grid=(N_DEV // 2,),
        in_specs=[by_chip, by_chip] * k, out_specs=[_const((rows, cols)), by_chip] * k,
        out_shape=[jax.ShapeDtypeStruct((rows, cols), F32), jax.ShapeDtypeStruct((N_DEV // 2, rows, cols), BF16)] * k,
        compiler_params=_params(("arbitrary",)),
    )(*[x for pair in zip(mines, recvs) for x in pair])
    return [tuple(res[2 * a:2 * a + 2]) for a in range(k)]


def _all_reduce_small(v, after=()):
    rows = v.shape[0]

    def body(v_ref, *rest):
        out_ref, gath, send_sems, recv_sems = rest[len(after):]
        x, y, c = _place()
        me = (x, y, c)
        peers = [tuple(1 - p if (k + 1) >> s & 1 else p for p, s in zip(me, (2, 1, 0))) for k in range(N_DEV - 1)]

        def blk(px, py, pc):
            return gath.at[pl.ds((4 * px + 2 * py + pc) * rows, rows), :]

        def copy(k, block, to, src=None):
            return pltpu.make_async_remote_copy(src_ref=blk(*block) if src is None else src, dst_ref=blk(*block),
                                                send_sem=send_sems.at[k], recv_sem=recv_sems.at[k], device_id=to, device_id_type=MESH)

        gath[pl.ds((4 * x + 2 * y + c) * rows, rows), :] = v_ref[...]
        sends = [copy(k, me, peer, src=v_ref) for k, peer in enumerate(peers)]
        for cp in sends:
            cp.start()
        for k, peer in enumerate(peers):
            copy(k, peer, me).wait_recv()
        for cp in sends:
            cp.wait_send()
        acc = gath[0:rows, :]
        for d in range(1, N_DEV):
            acc = acc + gath[d * rows:(d + 1) * rows, :]
        out_ref[...] = acc

    vm = pl.BlockSpec(memory_space=pltpu.VMEM)
    return pl.pallas_call(
        body, name="all_reduce_small",
        in_specs=[vm] + [ANY] * len(after), out_specs=vm,
        out_shape=jax.ShapeDtypeStruct(v.shape, F32),
        scratch_shapes=[pltpu.VMEM((N_DEV * rows, v.shape[1]), F32), pltpu.SemaphoreType.DMA((7,)), pltpu.SemaphoreType.DMA((7,))],
    )(v, *after)


def _tables(positions, duties, name):
    inv_freq = ROPE_THETA ** (-jnp.arange(0, HD, 2, dtype=F32) / HD)
    inv = jnp.concatenate([inv_freq] * 4)[None, :]
    sign = jnp.where((jnp.arange(128) % HD) < (HD // 2), -1.0, 1.0).astype(F32)[None, :]

    def body(pos_ref, inv_ref, sign_ref, cos_ref, sin_ref, bias_ref, cos_v, sin_v, bias_v):
        ang = pos_ref[...] * inv_ref[...]
        cos_v[...] = jnp.cos(ang)
        sin_v[...] = sign_ref[...] * jnp.sin(ang)
        d = lax.broadcasted_iota(jnp.int32, (AB, T), 0) - lax.broadcasted_iota(jnp.int32, (AB, T), 1) + (T - AB)
        cnt = jnp.zeros((AB, T), F32)
        for window, dil in DILATIONS:
            cnt = cnt + ((d >= 0) & ((d & (dil - 1)) == 0) & (d <= window)).astype(F32)
        bias_v[...] = jnp.where(cnt > 0, jnp.log(jnp.maximum(cnt, 1.0)), NEG)
        for v, o in ((cos_v, cos_ref), (sin_v, sin_ref), (bias_v, bias_ref)):
            pltpu.sync_copy(v, o)

    vm = pl.BlockSpec(memory_space=pltpu.VMEM)
    shapes = [(T, 128), (T, 128), (AB, T)]
    return _call(body, name=name, grid=None, in_specs=[vm] * 3, out_specs=[ANY] * 3,
                 out_shape=[jax.ShapeDtypeStruct(s, F32) for s in shapes], scratch=[pltpu.VMEM(s, F32) for s in shapes],
                 args=[positions.reshape(T, 1).astype(F32), inv, sign], sem=None, duties=duties)


def _selectors():
    lane = jnp.arange(QW)
    e = (lane[None, :] // HD == jnp.arange(SSM_H)[:, None]).astype(F32)
    e1 = ((lane[None, :] == HD * jnp.arange(128)[:, None]) & (jnp.arange(128)[:, None] < SSM_H)).astype(F32)
    return e, e1


WEIGHTS = ['ffn1_pre_norm', 'ffn1_w_gate', 'ffn1_w_up', 'ffn1_w_down', 'ffn1_post_norm', 'mix_pre_norm', 'w_in', 'conv_w', 'conv_b',
           'dt_bias', 'a_log', 'd_skip', 'ssm_norm', 'w_out', 'mix_post_norm', 'ffn2_pre_norm', 'ffn2_w_gate', 'ffn2_w_up',
           'ffn2_w_down', 'ffn2_post_norm']
COL_SHARDED = ['ffn1_w_gate', 'ffn1_w_up', 'ffn2_w_gate', 'ffn2_w_up', 'w_in']
ROW_SHARDED = ['ffn1_w_down', 'ffn2_w_down', 'w_out']
BIG = COL_SHARDED + ROW_SHARDED
FFN_BIG = COL_SHARDED[:4] + ROW_SHARDED[:2]
SMALL = ['ffn1_pre_norm', 'ffn1_post_norm', 'mix_pre_norm', 'conv_b', 'dt_bias', 'a_log', 'd_skip', 'ssm_norm', 'mix_post_norm',
         'ffn2_pre_norm', 'ffn2_post_norm']
FFN1 = ['ffn1_w_gate', 'ffn1_w_up', 'ffn1_w_down']
FFN2 = ['ffn2_w_gate', 'ffn2_w_up', 'ffn2_w_down']


def _wire_block(name, a):
    if name in FFN_BIG:
        return jnp.pad(a.astype(BF16), ((0, FSH - FSR), (0, 0)))
    if name == "w_in":
        return jnp.pad(a.astype(BF16), ((0, ISW - ISR), (0, 0)))
    return a if name == "conv_w" else a.astype(BF16)


def _whole_from_gathered(name, a):
    if name == "conv_w":
        return jnp.transpose(a, (1, 0, 2)).reshape(a.shape[1], -1)
    return a.reshape(-1, D)


def _step(x, positions, target, small, blocks=None, whole=None):
    dist = blocks is not None
    core = "mesh" if dist else 0
    w = dict(small)
    if whole:
        w.update(whole)

    def gather(names):
        return [_gather_duty([_wire_block(n, blocks[n]) for n in names])] if dist else []

    def put(names, results):
        if dist:
            for n, r in zip(names, results[0]):
                w[n] = _whole_from_gathered(n, r)

    g, sums, red = {}, {}, {}

    def swap(names):
        return [_swap_duty([g[n][1] for n in names])] if dist else []

    def chip_sums(names, from_sibling):
        if dist:
            res = _chip_sum([g[n][0] for n in names], list(from_sibling), "chip_sum_" + names[0])
            sums.update(zip(names, res))

    def exchange(names):
        return [_exchange_duty([sums[n][1] for n in names])] if dist else []

    def reduced(names, from_chips):
        if dist:
            for n, recv in zip(names, from_chips):
                red[n] = (sums[n][0], recv)

    e, e1 = _selectors()
    alx = jnp.repeat(w["a_log"], HD, axis=1)
    dskx = jnp.repeat(w["d_skip"], HD, axis=1)

    (cos, sin_s, bias), got = _tables(positions, gather(FFN1), "gather_ffn1")
    put(FFN1, got)
    (x1, n1, a1, b1, hm1, h1), got = _ffn_fwd(x, w["ffn1_pre_norm"], w["ffn1_w_gate"], w["ffn1_w_up"], w["ffn1_w_down"],
                                              w["ffn1_post_norm"], "ffn1_fwd", gather(["w_in", "conv_w"]))
    put(["w_in", "conv_w"], got)
    (n2, q, kx, vx, xbc, z, dtr), got = _inproj_fwd(x1, w["mix_pre_norm"], w["w_in"], cos, sin_s, gather(["w_out"]))
    put(["w_out"], got)
    (attn, lse), got = _attn_fwd(q, kx, vx, bias, gather(FFN2[:2]))
    put(FFN2[:2], got)
    (yn, y, hs, xconv), got = _ssd_fwd(xbc, z, dtr, w["conv_w"], w["conv_b"], w["dt_bias"], alx, dskx, w["ssm_norm"], e, gather(FFN2[2:]))
    put(FFN2[2:], got)
    x2, h2 = _outproj_fwd(x1, attn, yn, w["w_out"], w["mix_post_norm"])
    (dx3, n3, a3, b3, hm3, h3, ss), _ = _ffn_fwd(x2, w["ffn2_pre_norm"], w["ffn2_w_gate"], w["ffn2_w_up"], w["ffn2_w_down"],
                                                 w["ffn2_post_norm"], "ffn2_fwd", target=target)

    (dx2, da3, db3, dh3, g["ffn2_pre_norm"], g["ffn2_post_norm"]), _ = _ffn_bwd(
        dx3, x2, a3, b3, h3, w["ffn2_pre_norm"], w["ffn2_post_norm"], w["ffn2_w_gate"], w["ffn2_w_up"], w["ffn2_w_down"], "ffn2_bwd")
    g["ffn2_w_down"] = _matmul_tn(hm3, dh3, "ffn2_dwd", core)[0]
    g["ffn2_w_gate"] = _matmul_tn(da3, n3, "ffn2_dwg", core)[0]
    g["ffn2_w_up"] = _matmul_tn(db3, n3, "ffn2_dwu", core)[0]

    (dh2, dmix, g["mix_post_norm"]), got = _outproj_bwd(dx2, h2, w["mix_post_norm"], w["w_out"], swap(FFN2))
    chip_sums(FFN2, got[0] if dist else None)
    g["w_out"] = _dwout(attn, yn, dh2, core)
    (dq, dkx, dvx), got = _attn_bwd(q, kx, vx, attn, dmix, lse, bias, exchange(FFN2) + swap(["w_out"]))
    if dist:
        reduced(FFN2, got[0])
        chip_sums(["w_out"], got[1])
    (dxbc, dz, ddt, dcw, g["conv_b"], g["ssm_norm"], dpar), got = _ssd_bwd(
        dmix, xbc, xconv, z, dtr, y, hs, w["conv_w"], w["dt_bias"], alx, dskx, w["ssm_norm"], e, e1, exchange(["w_out"]))
    reduced(["w_out"], got[0] if dist else None)
    g["conv_w"] = dcw[0:4]
    g["dt_bias"], g["a_log"], g["d_skip"] = dpar[0:1], dpar[1:2], dpar[2:3]
    dx1, dproj, g["mix_pre_norm"] = _inproj_bwd(dx2, dq, dkx, dvx, dxbc, dz, ddt, x1, w["mix_pre_norm"], w["w_in"], cos, sin_s)
    g["w_in"] = _matmul_tn(dproj, n2, "dwin", core)[0]

    started = {}

    def start(n):
        started[n] = _exchange_start(sums[n][1], "start_exchange_" + n, FSR if n in FFN_BIG else None)
        return [started[n]["token"]]

    after = []
    if dist:
        chip_sums(["w_in"], _comm_only(swap(["w_in"]), "swap_w_in")[0])
        after = start("w_in")
    (dx0, da1, db1, dh1, g["ffn1_pre_norm"], g["ffn1_post_norm"]), _ = _ffn_bwd(
        dx1, x, a1, b1, h1, w["ffn1_pre_norm"], w["ffn1_post_norm"], w["ffn1_w_gate"], w["ffn1_w_up"], w["ffn1_w_down"], "ffn1_bwd",
        after)
    total = None
    if dist:
        widen = lambda a: jnp.pad(a, ((0, 0), (0, PACK_W - a.shape[1])))
        pack = jnp.concatenate([widen(g[n]) for n in SMALL] + [g["conv_w"], widen(ss[:, 0:1])])
        assert pack.shape[0] % 8 == 0
        total = _all_reduce_small(pack, after)
        after = [total]
    g["ffn1_w_down"], _ = _matmul_tn(hm1, dh1, "ffn1_dwd", core, after=after)
    g["ffn1_w_gate"], got = _matmul_tn(da1, n1, "ffn1_dwg", core, duties=swap(["ffn1_w_down"]))
    if dist:
        chip_sums(["ffn1_w_down"], got[0])
        after = start("ffn1_w_down")
    g["ffn1_w_up"], got = _matmul_tn(db1, n1, "ffn1_dwu", core, after=after, duties=swap(["ffn1_w_gate"]))
    if dist:
        chip_sums(["ffn1_w_gate"], got[0])
        after = start("ffn1_w_gate")
        chip_sums(["ffn1_w_up"], _comm_only(swap(["ffn1_w_up"]), "swap_ffn1_w_up", after=after)[0])
        start("ffn1_w_up")
    return ss, dx0, g, red, {n: (sums[n][0], started[n]) for n in started}, total


def kernel(x, positions, ffn1_pre_norm, ffn1_w_gate, ffn1_w_up, ffn1_w_down, ffn1_post_norm, mix_pre_norm, w_in, conv_w, conv_b, dt_bias, a_log, d_skip, ssm_norm, w_out, mix_post_norm, ffn2_pre_norm, ffn2_w_gate, ffn2_w_up, ffn2_w_down, ffn2_post_norm, loss_target, m_ffn1_pre_norm, m_ffn1_w_gate, m_ffn1_w_up, m_ffn1_w_down, m_ffn1_post_norm, m_mix_pre_norm, m_w_in, m_conv_w, m_conv_b, m_dt_bias, m_a_log, m_d_skip, m_ssm_norm, m_w_out, m_mix_post_norm, m_ffn2_pre_norm, m_ffn2_w_gate, m_ffn2_w_up, m_ffn2_w_down, m_ffn2_post_norm, v_ffn1_pre_norm, v_ffn1_w_gate, v_ffn1_w_up, v_ffn1_w_down, v_ffn1_post_norm, v_mix_pre_norm, v_w_in, v_conv_w, v_conv_b, v_dt_bias, v_a_log, v_d_skip, v_ssm_norm, v_w_out, v_mix_post_norm, v_ffn2_pre_norm, v_ffn2_w_gate, v_ffn2_w_up, v_ffn2_w_down, v_ffn2_post_norm):
    given = dict(locals())
    drop = lambda n, a: a if n in SMALL else (a[0].T if n in COL_SHARDED else a[0])
    w = {n: drop(n, given[n]) for n in WEIGHTS}
    m = {n: drop(n, given["m_" + n]) for n in WEIGHTS}
    v = {n: drop(n, given["v_" + n]) for n in WEIGHTS}
    cx, cy, cc = _place()
    others = [2 * (1 - cx) + cy, 2 * cx + (1 - cy), 2 * (1 - cx) + (1 - cy)]

    _, grad_x, g, red, pending, total = _step(x[0], positions, loss_target[0], {n: w[n] for n in SMALL},
                                              blocks={n: w[n] for n in BIG + ["conv_w"]})
    chip_ids = jnp.stack(others).astype(jnp.int32)
    out_g, out_d, out_m, out_v = {}, {}, {}, {}

    def update(names, sums, recvs, label, after=()):
        res = _adamw_sharded([w[n] for n in names], [m[n] for n in names], [v[n] for n in names], sums, recvs, chip_ids,
                             "adamw_" + label, after)
        for n, (gn, dn, mn, vn) in zip(names, res):
            out_g[n], out_d[n], out_m[n], out_v[n] = gn, dn, mn, vn

    last_start = [pending["ffn1_w_up"][1]["token"]]
    update(FFN2, [red[n][0] for n in FFN2], [red[n][1] for n in FFN2], "ffn2", last_start)
    update(["w_out"], [red["w_out"][0]], [red["w_out"][1]], "w_out", last_start)

    n_small = len(SMALL)
    conv_g = lax.dynamic_slice_in_dim(total[n_small:n_small + 4], (4 * cx + 2 * cy + cc) * (CONV_C // N_DEV), CONV_C // N_DEV, axis=1)
    loss = 0.5 * total[n_small + 4, 0] / D
    names = SMALL + ["conv_w"]
    res = _adamw_small([w[n] for n in names], [m[n] for n in names], [v[n] for n in names], total, conv_g, last_start)
    for n, (gn, dn, mn, vn) in zip(names, res):
        out_g[n], out_d[n], out_m[n], out_v[n] = gn, dn, mn, vn
    done = [out_v[n] for n in FFN2 + ["w_out", "conv_w"]]
    update(["w_in"], [pending["w_in"][0]], [_exchange_wait(pending["w_in"][1], done, "wait_exchange_w_in")], "w_in")
    done = [out_v["w_in"]]
    update(FFN1, [pending[n][0] for n in FFN1], [_exchange_wait(pending[n][1], done, "wait_exchange_" + n) for n in FFN1], "ffn1")

    outs = [loss, grad_x[None]]
    for d in (out_g, out_d, out_m, out_v):
        outs += [d[n] if n in SMALL else (d[n].T[None] if n in COL_SHARDED else d[n][None]) for n in WEIGHTS]
    return tuple(outs)
```
